```python
import jax, jax.numpy as jnp
from jax import lax
import numpy as np

D_MODEL = 2048
BATCH = 8
SEQ = 4096
DEPTH = 1

N_META = 16
D_MIX = D_MODEL
D_SC = D_MIX // 2
D_CF = D_MIX - D_SC
SC_WIDTH = 3
CF_WIDTH = 31
D_IN = 3 * D_SC + 2 * D_CF
D_FF = 5632
FFN_RES_SCALE = 0.5
EPS = 1e-6

kernel_name = "hymba_parallel_shortconv_conformer_macaron"


def rmsnorm(x, g):
    xf = x.astype(jnp.float32)
    y = xf * lax.rsqrt(jnp.mean(xf * xf, axis=-1, keepdims=True) + EPS)
    return (y * g.astype(jnp.float32)).astype(x.dtype)


def layernorm(x, g, b):
    xf = x.astype(jnp.float32)
    mu = jnp.mean(xf, axis=-1, keepdims=True)
    var = jnp.mean(jnp.square(xf - mu), axis=-1, keepdims=True)
    y = (xf - mu) * lax.rsqrt(var + EPS)
    return (y * g.astype(jnp.float32) + b.astype(jnp.float32)).astype(x.dtype)


def causal_dwconv(x, w):
    k, c = w.shape
    return lax.conv_general_dilated(
        x, w.astype(x.dtype)[:, None, :], window_strides=(1,), padding=[(k - 1, 0)],
        dimension_numbers=("NWC", "WIO", "NWC"), feature_group_count=c)


def swiglu(h, w_gate, w_up, w_down):
    return (jax.nn.silu(h @ w_gate) * (h @ w_up)) @ w_down


def _fwd_setup_inputs(seed: int = 0) -> dict:
    key = jax.random.key(seed)
    ks = jax.random.split(key, 24)
    f32 = jnp.float32
    nrm = lambda k, shape, scale: (jax.random.normal(k, shape, f32) * scale).astype(f32)
    gain = lambda k, shape: 1.0 + 0.05 * jax.random.normal(k, shape, f32)
    L = DEPTH
    return {
        "x": jax.random.normal(ks[0], (BATCH, SEQ, D_MODEL), f32),
        "meta_tokens": nrm(ks[1], (N_META, D_MODEL), 1.0),
        "ffn1_norm": gain(ks[2], (L, D_MODEL)),
        "ffn1_w_gate": nrm(ks[3], (L, D_MODEL, D_FF), D_MODEL ** -0.5),
        "ffn1_w_up": nrm(ks[4], (L, D_MODEL, D_FF), D_MODEL ** -0.5),
        "ffn1_w_down": nrm(ks[5], (L, D_FF, D_MODEL), D_FF ** -0.5),
        "mix_norm": gain(ks[6], (L, D_MODEL)),
        "w_in": nrm(ks[7], (L, D_MODEL, D_IN), D_MODEL ** -0.5),
        "b_in": nrm(ks[8], (L, D_IN), 0.02),
        "conv_sc_w": nrm(ks[9], (L, SC_WIDTH, D_SC), SC_WIDTH ** -0.5),
        "conv_cf_w": nrm(ks[10], (L, CF_WIDTH, D_CF), CF_WIDTH ** -0.5),
        "conv_cf_b": nrm(ks[11], (L, D_CF), 0.02),
        "ln_cf_g": gain(ks[12], (L, D_CF)),
        "ln_cf_b": nrm(ks[13], (L, D_CF), 0.02),
        "w_out": nrm(ks[14], (L, D_MIX, D_MODEL), D_MIX ** -0.5),
        "ffn2_norm": gain(ks[15], (L, D_MODEL)),
        "ffn2_w_gate": nrm(ks[16], (L, D_MODEL, D_FF), D_MODEL ** -0.5),
        "ffn2_w_up": nrm(ks[17], (L, D_MODEL, D_FF), D_MODEL ** -0.5),
        "ffn2_w_down": nrm(ks[18], (L, D_FF, D_MODEL), D_FF ** -0.5),
        "final_norm": gain(ks[19], (D_MODEL,)),
    }


def token_mixing(h, w_in, b_in, conv_sc_w, conv_cf_w, conv_cf_b, ln_cf_g, ln_cf_b, w_out):
    u = h @ w_in + b_in.astype(h.dtype)
    b_sc = u[..., :D_SC]
    c_sc = u[..., D_SC:2 * D_SC]
    v_sc = u[..., 2 * D_SC:3 * D_SC]
    a_cf = u[..., 3 * D_SC:3 * D_SC + D_CF]
    g_cf = u[..., 3 * D_SC + D_CF:]
    y_sc = b_sc * causal_dwconv(c_sc * v_sc, conv_sc_w)
    z = a_cf * jax.nn.sigmoid(g_cf)
    z = causal_dwconv(z, conv_cf_w) + conv_cf_b.astype(z.dtype)
    y_cf = jax.nn.silu(layernorm(z, ln_cf_g, ln_cf_b))
    return jnp.concatenate([y_sc, y_cf], axis=-1) @ w_out


def _fwd_reference(x, meta_tokens, ffn1_norm, ffn1_w_gate, ffn1_w_up, ffn1_w_down, mix_norm, w_in, b_in,
              conv_sc_w, conv_cf_w, conv_cf_b, ln_cf_g, ln_cf_b, w_out,
              ffn2_norm, ffn2_w_gate, ffn2_w_up, ffn2_w_down, final_norm):
    bsz = x.shape[0]
    meta = jnp.broadcast_to(meta_tokens.astype(x.dtype)[None], (bsz, N_META, x.shape[-1]))
    hs = jnp.concatenate([meta, x], axis=1)
    for l in range(DEPTH):
        hs = hs + FFN_RES_SCALE * swiglu(rmsnorm(hs, ffn1_norm[l]), ffn1_w_gate[l], ffn1_w_up[l], ffn1_w_down[l])
        hs = hs + token_mixing(rmsnorm(hs, mix_norm[l]), w_in[l], b_in[l], conv_sc_w[l], conv_cf_w[l],
                               conv_cf_b[l], ln_cf_g[l], ln_cf_b[l], w_out[l])
        hs = hs + FFN_RES_SCALE * swiglu(rmsnorm(hs, ffn2_norm[l]), ffn2_w_gate[l], ffn2_w_up[l], ffn2_w_down[l])
    out = rmsnorm(hs, final_norm)
    return out[:, N_META:]


import jax as _jax
import jax.numpy as _jnp

TWIN_FORMAT = 'train_step'
FWD_PARAMS = ['x', 'meta_tokens', 'ffn1_norm', 'ffn1_w_gate', 'ffn1_w_up', 'ffn1_w_down', 'mix_norm', 'w_in', 'b_in', 'conv_sc_w', 'conv_cf_w', 'conv_cf_b', 'ln_cf_g', 'ln_cf_b', 'w_out', 'ffn2_norm', 'ffn2_w_gate', 'ffn2_w_up', 'ffn2_w_down', 'final_norm']
TWIN_WEIGHTS = ['meta_tokens', 'ffn1_norm', 'ffn1_w_gate', 'ffn1_w_up', 'ffn1_w_down', 'mix_norm', 'w_in', 'b_in', 'conv_sc_w', 'conv_cf_w', 'conv_cf_b', 'ln_cf_g', 'ln_cf_b', 'w_out', 'ffn2_norm', 'ffn2_w_gate', 'ffn2_w_up', 'ffn2_w_down', 'final_norm']
TWIN_DIFF_INPUT = 'x'
TWIN_INPUTS = ['x', 'meta_tokens', 'ffn1_norm', 'ffn1_w_gate', 'ffn1_w_up', 'ffn1_w_down', 'mix_norm', 'w_in', 'b_in', 'conv_sc_w', 'conv_cf_w', 'conv_cf_b', 'ln_cf_g', 'ln_cf_b', 'w_out', 'ffn2_norm', 'ffn2_w_gate', 'ffn2_w_up', 'ffn2_w_down', 'final_norm', 'loss_target', 'm_meta_tokens', 'm_ffn1_norm', 'm_ffn1_w_gate', 'm_ffn1_w_up', 'm_ffn1_w_down', 'm_mix_norm', 'm_w_in', 'm_b_in', 'm_conv_sc_w', 'm_conv_cf_w', 'm_conv_cf_b', 'm_ln_cf_g', 'm_ln_cf_b', 'm_w_out', 'm_ffn2_norm', 'm_ffn2_w_gate', 'm_ffn2_w_up', 'm_ffn2_w_down', 'm_final_norm', 'v_meta_tokens', 'v_ffn1_norm', 'v_ffn1_w_gate', 'v_ffn1_w_up', 'v_ffn1_w_down', 'v_mix_norm', 'v_w_in', 'v_b_in', 'v_conv_sc_w', 'v_conv_cf_w', 'v_conv_cf_b', 'v_ln_cf_g', 'v_ln_cf_b', 'v_w_out', 'v_ffn2_norm', 'v_ffn2_w_gate', 'v_ffn2_w_up', 'v_ffn2_w_down', 'v_final_norm']
TWIN_OUTPUTS = ['loss', 'grad_x', 'grad_meta_tokens', 'grad_ffn1_norm', 'grad_ffn1_w_gate', 'grad_ffn1_w_up', 'grad_ffn1_w_down', 'grad_mix_norm', 'grad_w_in', 'grad_b_in', 'grad_conv_sc_w', 'grad_conv_cf_w', 'grad_conv_cf_b', 'grad_ln_cf_g', 'grad_ln_cf_b', 'grad_w_out', 'grad_ffn2_norm', 'grad_ffn2_w_gate', 'grad_ffn2_w_up', 'grad_ffn2_w_down', 'grad_final_norm', 'delta_meta_tokens', 'delta_ffn1_norm', 'delta_ffn1_w_gate', 'delta_ffn1_w_up', 'delta_ffn1_w_down', 'delta_mix_norm', 'delta_w_in', 'delta_b_in', 'delta_conv_sc_w', 'delta_conv_cf_w', 'delta_conv_cf_b', 'delta_ln_cf_g', 'delta_ln_cf_b', 'delta_w_out', 'delta_ffn2_norm', 'delta_ffn2_w_gate', 'delta_ffn2_w_up', 'delta_ffn2_w_down', 'delta_final_norm', 'new_m_meta_tokens', 'new_m_ffn1_norm', 'new_m_ffn1_w_gate', 'new_m_ffn1_w_up', 'new_m_ffn1_w_down', 'new_m_mix_norm', 'new_m_w_in', 'new_m_b_in', 'new_m_conv_sc_w', 'new_m_conv_cf_w', 'new_m_conv_cf_b', 'new_m_ln_cf_g', 'new_m_ln_cf_b', 'new_m_w_out', 'new_m_ffn2_norm', 'new_m_ffn2_w_gate', 'new_m_ffn2_w_up', 'new_m_ffn2_w_down', 'new_m_final_norm', 'new_v_meta_tokens', 'new_v_ffn1_norm', 'new_v_ffn1_w_gate', 'new_v_ffn1_w_up', 'new_v_ffn1_w_down', 'new_v_mix_norm', 'new_v_w_in', 'new_v_b_in', 'new_v_conv_sc_w', 'new_v_conv_cf_w', 'new_v_conv_cf_b', 'new_v_ln_cf_g', 'new_v_ln_cf_b', 'new_v_w_out', 'new_v_ffn2_norm', 'new_v_ffn2_w_gate', 'new_v_ffn2_w_up', 'new_v_ffn2_w_down', 'new_v_final_norm']
TWIN_LEAF_KINDS = {'loss': 'loss', 'grad_x': 'grad_x', 'grad_meta_tokens': 'grad_w', 'grad_ffn1_norm': 'grad_w', 'grad_ffn1_w_gate': 'grad_w', 'grad_ffn1_w_up': 'grad_w', 'grad_ffn1_w_down': 'grad_w', 'grad_mix_norm': 'grad_w', 'grad_w_in': 'grad_w', 'grad_b_in': 'grad_w', 'grad_conv_sc_w': 'grad_w', 'grad_conv_cf_w': 'grad_w', 'grad_conv_cf_b': 'grad_w', 'grad_ln_cf_g': 'grad_w', 'grad_ln_cf_b': 'grad_w', 'grad_w_out': 'grad_w', 'grad_ffn2_norm': 'grad_w', 'grad_ffn2_w_gate': 'grad_w', 'grad_ffn2_w_up': 'grad_w', 'grad_ffn2_w_down': 'grad_w', 'grad_final_norm': 'grad_w', 'delta_meta_tokens': 'delta_w', 'delta_ffn1_norm': 'delta_w', 'delta_ffn1_w_gate': 'delta_w', 'delta_ffn1_w_up': 'delta_w', 'delta_ffn1_w_down': 'delta_w', 'delta_mix_norm': 'delta_w', 'delta_w_in': 'delta_w', 'delta_b_in': 'delta_w', 'delta_conv_sc_w': 'delta_w', 'delta_conv_cf_w': 'delta_w', 'delta_conv_cf_b': 'delta_w', 'delta_ln_cf_g': 'delta_w', 'delta_ln_cf_b': 'delta_w', 'delta_w_out': 'delta_w', 'delta_ffn2_norm': 'delta_w', 'delta_ffn2_w_gate': 'delta_w', 'delta_ffn2_w_up': 'delta_w', 'delta_ffn2_w_down': 'delta_w', 'delta_final_norm': 'delta_w', 'new_m_meta_tokens': 'new_m', 'new_m_ffn1_norm': 'new_m', 'new_m_ffn1_w_gate': 'new_m', 'new_m_ffn1_w_up': 'new_m', 'new_m_ffn1_w_down': 'new_m', 'new_m_mix_norm': 'new_m', 'new_m_w_in': 'new_m', 'new_m_b_in': 'new_m', 'new_m_conv_sc_w': 'new_m', 'new_m_conv_cf_w': 'new_m', 'new_m_conv_cf_b': 'new_m', 'new_m_ln_cf_g': 'new_m', 'new_m_ln_cf_b': 'new_m', 'new_m_w_out': 'new_m', 'new_m_ffn2_norm': 'new_m', 'new_m_ffn2_w_gate': 'new_m', 'new_m_ffn2_w_up': 'new_m', 'new_m_ffn2_w_down': 'new_m', 'new_m_final_norm': 'new_m', 'new_v_meta_tokens': 'new_v', 'new_v_ffn1_norm': 'new_v', 'new_v_ffn1_w_gate': 'new_v', 'new_v_ffn1_w_up': 'new_v', 'new_v_ffn1_w_down': 'new_v', 'new_v_mix_norm': 'new_v', 'new_v_w_in': 'new_v', 'new_v_b_in': 'new_v', 'new_v_conv_sc_w': 'new_v', 'new_v_conv_cf_w': 'new_v', 'new_v_conv_cf_b': 'new_v', 'new_v_ln_cf_g': 'new_v', 'new_v_ln_cf_b': 'new_v', 'new_v_w_out': 'new_v', 'new_v_ffn2_norm': 'new_v', 'new_v_ffn2_w_gate': 'new_v', 'new_v_ffn2_w_up': 'new_v', 'new_v_ffn2_w_down': 'new_v', 'new_v_final_norm': 'new_v'}


def _forward(args):
    return _fwd_reference(*[args[k] for k in FWD_PARAMS])


def _output_shape():
    def fwd():
        inp = _fwd_setup_inputs(0)
        return _fwd_reference(*[inp[k] for k in FWD_PARAMS])
    out = _jax.eval_shape(fwd)
    return out.shape, out.dtype

N_MICROBATCH = 1
ADAM_LR = 0.001
ADAM_B1 = 0.9
ADAM_B2 = 0.999
ADAM_EPS = 1e-08
ADAM_WD = 0.01
ADAM_STEP = 10
PER_EXAMPLE_BATCH_AXIS = {'x': 0, 'loss_target': 0}
SHARED_INPUTS = []
_WEIGHT_DTYPES = {'meta_tokens': _jnp.float32, 'ffn1_norm': _jnp.float32, 'ffn1_w_gate': _jnp.float32, 'ffn1_w_up': _jnp.float32, 'ffn1_w_down': _jnp.float32, 'mix_norm': _jnp.float32, 'w_in': _jnp.float32, 'b_in': _jnp.float32, 'conv_sc_w': _jnp.float32, 'conv_cf_w': _jnp.float32, 'conv_cf_b': _jnp.float32, 'ln_cf_g': _jnp.float32, 'ln_cf_b': _jnp.float32, 'w_out': _jnp.float32, 'ffn2_norm': _jnp.float32, 'ffn2_w_gate': _jnp.float32, 'ffn2_w_up': _jnp.float32, 'ffn2_w_down': _jnp.float32, 'final_norm': _jnp.float32}
MOMENT_SCALE = {'meta_tokens': 2.246361e-03, 'ffn1_norm': 4.693509e-02, 'ffn1_w_gate': 2.062026e-02, 'ffn1_w_up': 1.996931e-02, 'ffn1_w_down': 3.316565e-02, 'mix_norm': 9.373761e-02, 'w_in': 5.731907e-02, 'b_in': 6.335200e-02, 'conv_sc_w': 7.109116e-02, 'conv_cf_w': 4.350433e-02, 'conv_cf_b': 1.141295e-01, 'ln_cf_g': 6.024905e-02, 'ln_cf_b': 6.183300e-02, 'w_out': 5.945832e-02, 'ffn2_norm': 2.827305e-02, 'ffn2_w_gate': 1.212946e-02, 'ffn2_w_up': 1.179429e-02, 'ffn2_w_down': 1.958489e-02, 'final_norm': 1.602481e+01}


def _to_microbatches(a, axis):
    t = _jnp.moveaxis(a, axis, 0)
    t = t.reshape((N_MICROBATCH, t.shape[0] // N_MICROBATCH) + t.shape[1:])
    return _jnp.moveaxis(t, 1, axis + 1)


def setup_inputs(seed: int = 0) -> dict:
    inp = _fwd_setup_inputs(seed)
    key = _jax.random.fold_in(_jax.random.key(seed), 7919)
    shape, _ = _output_shape()
    out = dict(inp)
    out["loss_target"] = _jax.random.normal(_jax.random.fold_in(key, 0), shape, _jnp.float32)
    for i, name in enumerate(TWIN_WEIGHTS):
        w = inp[name].astype(_jnp.float32)
        if MOMENT_SCALE is None:
            s = _jnp.sqrt(_jnp.mean(_jnp.square(w)) + 1e-30)
        else:
            s = MOMENT_SCALE[name]
        km, kv = _jax.random.split(_jax.random.fold_in(key, i + 1))
        out[name] = w
        out["m_" + name] = s * _jax.random.normal(km, w.shape, _jnp.float32)
        out["v_" + name] = (s * s) * _jax.random.uniform(kv, w.shape, _jnp.float32, 0.5, 1.5)
    if N_MICROBATCH > 1:
        for name, axis in PER_EXAMPLE_BATCH_AXIS.items():
            out[name] = _to_microbatches(out[name], axis)
    return {'x': out['x'], 'meta_tokens': out['meta_tokens'], 'ffn1_norm': out['ffn1_norm'], 'ffn1_w_gate': out['ffn1_w_gate'], 'ffn1_w_up': out['ffn1_w_up'], 'ffn1_w_down': out['ffn1_w_down'], 'mix_norm': out['mix_norm'], 'w_in': out['w_in'], 'b_in': out['b_in'], 'conv_sc_w': out['conv_sc_w'], 'conv_cf_w': out['conv_cf_w'], 'conv_cf_b': out['conv_cf_b'], 'ln_cf_g': out['ln_cf_g'], 'ln_cf_b': out['ln_cf_b'], 'w_out': out['w_out'], 'ffn2_norm': out['ffn2_norm'], 'ffn2_w_gate': out['ffn2_w_gate'], 'ffn2_w_up': out['ffn2_w_up'], 'ffn2_w_down': out['ffn2_w_down'], 'final_norm': out['final_norm'], 'loss_target': out['loss_target'], 'm_meta_tokens': out['m_meta_tokens'], 'm_ffn1_norm': out['m_ffn1_norm'], 'm_ffn1_w_gate': out['m_ffn1_w_gate'], 'm_ffn1_w_up': out['m_ffn1_w_up'], 'm_ffn1_w_down': out['m_ffn1_w_down'], 'm_mix_norm': out['m_mix_norm'], 'm_w_in': out['m_w_in'], 'm_b_in': out['m_b_in'], 'm_conv_sc_w': out['m_conv_sc_w'], 'm_conv_cf_w': out['m_conv_cf_w'], 'm_conv_cf_b': out['m_conv_cf_b'], 'm_ln_cf_g': out['m_ln_cf_g'], 'm_ln_cf_b': out['m_ln_cf_b'], 'm_w_out': out['m_w_out'], 'm_ffn2_norm': out['m_ffn2_norm'], 'm_ffn2_w_gate': out['m_ffn2_w_gate'], 'm_ffn2_w_up': out['m_ffn2_w_up'], 'm_ffn2_w_down': out['m_ffn2_w_down'], 'm_final_norm': out['m_final_norm'], 'v_meta_tokens': out['v_meta_tokens'], 'v_ffn1_norm': out['v_ffn1_norm'], 'v_ffn1_w_gate': out['v_ffn1_w_gate'], 'v_ffn1_w_up': out['v_ffn1_w_up'], 'v_ffn1_w_down': out['v_ffn1_w_down'], 'v_mix_norm': out['v_mix_norm'], 'v_w_in': out['v_w_in'], 'v_b_in': out['v_b_in'], 'v_conv_sc_w': out['v_conv_sc_w'], 'v_conv_cf_w': out['v_conv_cf_w'], 'v_conv_cf_b': out['v_conv_cf_b'], 'v_ln_cf_g': out['v_ln_cf_g'], 'v_ln_cf_b': out['v_ln_cf_b'], 'v_w_out': out['v_w_out'], 'v_ffn2_norm': out['v_ffn2_norm'], 'v_ffn2_w_gate': out['v_ffn2_w_gate'], 'v_ffn2_w_up': out['v_ffn2_w_up'], 'v_ffn2_w_down': out['v_ffn2_w_down'], 'v_final_norm': out['v_final_norm']}


def _loss(weights, diff, rest, loss_target):
    with _jax.named_scope("forward"):
        args = {**rest, TWIN_DIFF_INPUT: diff, **{k: w.astype(_WEIGHT_DTYPES[k]) for k, w in weights.items()}}
        y = _forward(args)
    with _jax.named_scope("loss_head"):
        err = _jnp.square(y.astype(_jnp.float32) - loss_target)
        return 0.5 * _jnp.sum(_jnp.mean(err, axis=-1)) if err.ndim else 0.5 * err


def _adamw(w, g, m, v):
    m = ADAM_B1 * m + (1.0 - ADAM_B1) * g
    v = ADAM_B2 * v + (1.0 - ADAM_B2) * _jnp.square(g)
    m_hat = m / (1.0 - ADAM_B1 ** ADAM_STEP)
    v_hat = v / (1.0 - ADAM_B2 ** ADAM_STEP)
    delta = -ADAM_LR * (m_hat / (_jnp.sqrt(v_hat) + ADAM_EPS) + ADAM_WD * w)
    return delta, m, v


def reference(x, meta_tokens, ffn1_norm, ffn1_w_gate, ffn1_w_up, ffn1_w_down, mix_norm, w_in, b_in, conv_sc_w, conv_cf_w, conv_cf_b, ln_cf_g, ln_cf_b, w_out, ffn2_norm, ffn2_w_gate, ffn2_w_up, ffn2_w_down, final_norm, loss_target, m_meta_tokens, m_ffn1_norm, m_ffn1_w_gate, m_ffn1_w_up, m_ffn1_w_down, m_mix_norm, m_w_in, m_b_in, m_conv_sc_w, m_conv_cf_w, m_conv_cf_b, m_ln_cf_g, m_ln_cf_b, m_w_out, m_ffn2_norm, m_ffn2_w_gate, m_ffn2_w_up, m_ffn2_w_down, m_final_norm, v_meta_tokens, v_ffn1_norm, v_ffn1_w_gate, v_ffn1_w_up, v_ffn1_w_down, v_mix_norm, v_w_in, v_b_in, v_conv_sc_w, v_conv_cf_w, v_conv_cf_b, v_ln_cf_g, v_ln_cf_b, v_w_out, v_ffn2_norm, v_ffn2_w_gate, v_ffn2_w_up, v_ffn2_w_down, v_final_norm):
    given = dict(x=x, meta_tokens=meta_tokens, ffn1_norm=ffn1_norm, ffn1_w_gate=ffn1_w_gate, ffn1_w_up=ffn1_w_up, ffn1_w_down=ffn1_w_down, mix_norm=mix_norm, w_in=w_in, b_in=b_in, conv_sc_w=conv_sc_w, conv_cf_w=conv_cf_w, conv_cf_b=conv_cf_b, ln_cf_g=ln_cf_g, ln_cf_b=ln_cf_b, w_out=w_out, ffn2_norm=ffn2_norm, ffn2_w_gate=ffn2_w_gate, ffn2_w_up=ffn2_w_up, ffn2_w_down=ffn2_w_down, final_norm=final_norm, loss_target=loss_target, m_meta_tokens=m_meta_tokens, m_ffn1_norm=m_ffn1_norm, m_ffn1_w_gate=m_ffn1_w_gate, m_ffn1_w_up=m_ffn1_w_up, m_ffn1_w_down=m_ffn1_w_down, m_mix_norm=m_mix_norm, m_w_in=m_w_in, m_b_in=m_b_in, m_conv_sc_w=m_conv_sc_w, m_conv_cf_w=m_conv_cf_w, m_conv_cf_b=m_conv_cf_b, m_ln_cf_g=m_ln_cf_g, m_ln_cf_b=m_ln_cf_b, m_w_out=m_w_out, m_ffn2_norm=m_ffn2_norm, m_ffn2_w_gate=m_ffn2_w_gate, m_ffn2_w_up=m_ffn2_w_up, m_ffn2_w_down=m_ffn2_w_down, m_final_norm=m_final_norm, v_meta_tokens=v_meta_tokens, v_ffn1_norm=v_ffn1_norm, v_ffn1_w_gate=v_ffn1_w_gate, v_ffn1_w_up=v_ffn1_w_up, v_ffn1_w_down=v_ffn1_w_down, v_mix_norm=v_mix_norm, v_w_in=v_w_in, v_b_in=v_b_in, v_conv_sc_w=v_conv_sc_w, v_conv_cf_w=v_conv_cf_w, v_conv_cf_b=v_conv_cf_b, v_ln_cf_g=v_ln_cf_g, v_ln_cf_b=v_ln_cf_b, v_w_out=v_w_out, v_ffn2_norm=v_ffn2_norm, v_ffn2_w_gate=v_ffn2_w_gate, v_ffn2_w_up=v_ffn2_w_up, v_ffn2_w_down=v_ffn2_w_down, v_final_norm=v_final_norm)
    weights = {n: given[n] for n in TWIN_WEIGHTS}
    shared = {n: given[n] for n in SHARED_INPUTS}
    per_example = {n: given[n] for n in ['x']}
    grad_fn = _jax.value_and_grad(_loss, argnums=(0, 1))

    def one_microbatch(ex, loss_target):
        ex = dict(ex)
        diff = ex.pop(TWIN_DIFF_INPUT)
        return grad_fn(weights, diff, {**shared, **ex}, loss_target)

    if N_MICROBATCH == 1:
        loss, (grad_w, grad_x) = one_microbatch(per_example, given["loss_target"])
    else:
        def body(carry, xs):
            loss_sum, grad_sum = carry
            l_k, (gw_k, gx_k) = one_microbatch(xs[0], xs[1])
            with _jax.named_scope("update"):
                return (loss_sum + l_k, _jax.tree.map(_jnp.add, grad_sum, gw_k)), gx_k

        init = (_jnp.zeros((), _jnp.float32), _jax.tree.map(_jnp.zeros_like, weights))
        (loss, grad_w), grad_x = _jax.lax.scan(body, init, (per_example, given["loss_target"]))
    with _jax.named_scope("update"):
        delta_w, new_m, new_v = {}, {}, {}
        for n in TWIN_WEIGHTS:
            delta_w[n], new_m[n], new_v[n] = _adamw(weights[n], grad_w[n], given["m_" + n], given["v_" + n])
    return (loss, grad_x, *[grad_w[n] for n in TWIN_WEIGHTS], *[delta_w[n] for n in TWIN_WEIGHTS],
            *[new_m[n] for n in TWIN_WEIGHTS], *[new_v[n] for n in TWIN_WEIGHTS])
```

```python
import functools

import jax
import jax.numpy as jnp
from jax import lax
from jax.experimental import pallas as pl
from jax.experimental.pallas import tpu as pltpu

F32 = jnp.float32
BF16 = jnp.bfloat16
EPS = 1e-6
FFN_RES_SCALE = 0.5
SC_WIDTH = 3
CF_WIDTH = 31
ADAM_LR = 0.001
ADAM_B1 = 0.9
ADAM_B2 = 0.999
ADAM_EPS = 1e-08
ADAM_WD = 0.01
ADAM_STEP = 10

N_DEV = 8
N_ROW_TILES = 8
ROW_ALIGN = 256
CONV_PAD = 32
CONV_CH = 128
SUB_ROWS = 32
SUBLANES = 8
VMEM_LIMIT = 56 * 1024 * 1024
MESH = pl.DeviceIdType.MESH
ANY = pl.BlockSpec(memory_space=pl.ANY)

NT = (((1,), (1,)), ((), ()))
TN = (((0,), (0,)), ((), ()))


def _call(body, **kw):
    return pl.pallas_call(body, **kw)


def _params(**kw):
    return pltpu.CompilerParams(vmem_limit_bytes=VMEM_LIMIT, **kw)


def _sigmoid(x):
    return 1.0 / (1.0 + jnp.exp(-x))


def _sds(shape, dtype):
    return jax.ShapeDtypeStruct(shape, dtype)


def _place():
    x, y, c = lax.axis_index("x"), lax.axis_index("y"), lax.axis_index("c")
    chips = [(1 - x, y), (x, 1 - y), (1 - x, 1 - y)]
    return x, y, c, chips


def _all_gather(arrs, name):
    n = len(arrs)

    def body(*refs):
        ins, outs = refs[:n], refs[n:2 * n]
        send_sems, recv_sems, local_sems = refs[2 * n:]
        x, y, c, chips = _place()
        me, sibling = (x, y, c), (x, y, 1 - c)

        def slot(a, p):
            return outs[a].at[4 * p[0] + 2 * p[1] + p[2]]

        def copy(a, k, block, to, src=None):
            return pltpu.make_async_remote_copy(
                src_ref=slot(a, block) if src is None else src, dst_ref=slot(a, block),
                send_sem=send_sems.at[a, k], recv_sem=recv_sems.at[a, k], device_id=to, device_id_type=MESH)

        started = []
        for a in range(n):
            mine = pltpu.make_async_copy(ins[a], slot(a, me), local_sems.at[a])
            mine.start()
            started.append(mine)
        sends = []
        for a in range(n):
            sends.append(copy(a, 0, me, sibling, src=ins[a]))
            for j, chip in enumerate(chips):
                sends.append(copy(a, 1 + j, me, (*chip, c), src=ins[a]))
        for cp in sends:
            cp.start()
        for a in range(n):
            for j, chip in enumerate(chips):
                copy(a, 1 + j, (*chip, c), me).wait_recv()
                fwd = copy(a, 4 + j, (*chip, c), sibling)
                fwd.start()
                sends.append(fwd)
        for a in range(n):
            copy(a, 0, sibling, me).wait_recv()
            for j, chip in enumerate(chips):
                copy(a, 4 + j, (*chip, 1 - c), me).wait_recv()
        for cp in sends:
            cp.wait_send()
        for mine in started:
            mine.wait()

    return _call(
        body, name=name,
        out_shape=[_sds((N_DEV,) + a.shape, a.dtype) for a in arrs],
        in_specs=[ANY] * n, out_specs=[ANY] * n,
        scratch_shapes=[pltpu.SemaphoreType.DMA((n, 7)), pltpu.SemaphoreType.DMA((n, 7)), pltpu.SemaphoreType.DMA((n,))],
    )(*arrs)


def _sibling_exchange(arrs, name):
    n = len(arrs)

    def body(*refs):
        ins, outs = refs[:n], refs[n:2 * n]
        send_sems, recv_sems = refs[2 * n:]
        x, y, c, _ = _place()
        copies = [
            pltpu.make_async_remote_copy(
                src_ref=ins[a].at[:, 1 - c], dst_ref=outs[a], send_sem=send_sems.at[a], recv_sem=recv_sems.at[a],
                device_id=(x, y, 1 - c), device_id_type=MESH)
            for a in range(n)]
        for cp in copies:
            cp.start()
        for cp in copies:
            cp.wait()

    return _call(
        body, name=name,
        out_shape=[_sds((4,) + a.shape[2:], a.dtype) for a in arrs],
        in_specs=[ANY] * n, out_specs=[ANY] * n,
        scratch_shapes=[pltpu.SemaphoreType.DMA((n,)), pltpu.SemaphoreType.DMA((n,))],
    )(*arrs)


def _chip_exchange(arrs, name):
    n = len(arrs)

    def body(*refs):
        ins, outs = refs[:n], refs[n:2 * n]
        send_sems, recv_sems = refs[2 * n:]
        x, y, c, chips = _place()
        copies = [
            pltpu.make_async_remote_copy(
                src_ref=ins[a].at[1 + j], dst_ref=outs[a].at[j], send_sem=send_sems.at[a, j], recv_sem=recv_sems.at[a, j],
                device_id=(*chip, c), device_id_type=MESH)
            for a in range(n) for j, chip in enumerate(chips)]
        for cp in copies:
            cp.start()
        for cp in copies:
            cp.wait()

    return _call(
        body, name=name,
        out_shape=[_sds((3,) + a.shape[1:], a.dtype) for a in arrs],
        in_specs=[ANY] * n, out_specs=[ANY] * n,
        scratch_shapes=[pltpu.SemaphoreType.DMA((n, 3)), pltpu.SemaphoreType.DMA((n, 3))],
    )(*arrs)


def _gather_rows(vec, name):
    def body(in_ref, out_ref, send_sems, recv_sems, local_sem):
        x, y, c, _ = _place()
        me = 4 * x + 2 * y + c
        mine = pltpu.make_async_copy(in_ref, out_ref.at[me], local_sem)
        mine.start()
        copies = []
        for k in range(1, N_DEV):
            to = (1 - x if k & 4 else x, 1 - y if k & 2 else y, 1 - c if k & 1 else c)
            copies.append(pltpu.make_async_remote_copy(
                src_ref=in_ref, dst_ref=out_ref.at[me], send_sem=send_sems.at[k - 1], recv_sem=recv_sems.at[k - 1],
                device_id=to, device_id_type=MESH))
        for cp in copies:
            cp.start()
        for cp in copies:
            cp.wait()
        mine.wait()

    return _call(
        body, name=name, out_shape=_sds((N_DEV,) + vec.shape, vec.dtype), in_specs=[ANY], out_specs=ANY,
        scratch_shapes=[pltpu.SemaphoreType.DMA((7,)), pltpu.SemaphoreType.DMA((7,)), pltpu.SemaphoreType.DMA],
    )(vec)


def _for_row_groups(tm, fn):
    def step(i, carry):
        fn(pl.ds(pl.multiple_of(i * SUB_ROWS, SUB_ROWS), SUB_ROWS), i * SUB_ROWS)
        return carry

    lax.fori_loop(0, tm // SUB_ROWS, step, 0)


def _once(shape, index_map):
    return pl.BlockSpec(shape, index_map, pipeline_mode=pl.Buffered(1))


def _rms_fwd(hs, g, tm, name):
    rows, d = hs.shape

    def body(hs_ref, g_ref, h_ref, r_ref):
        def group(rs, _):
            xv = hs_ref[rs, :]
            r = lax.rsqrt(jnp.mean(xv * xv, axis=-1, keepdims=True) + EPS)
            h_ref[rs, :] = (xv * r * g_ref[...]).astype(BF16)
            r_ref[rs, :] = r

        _for_row_groups(tm, group)

    return _call(
        body, name=name, grid=(rows // tm,),
        in_specs=[pl.BlockSpec((tm, d), lambda i: (i, 0)), pl.BlockSpec((1, d), lambda i: (0, 0))],
        out_specs=[pl.BlockSpec((tm, d), lambda i: (i, 0)), pl.BlockSpec((tm, 1), lambda i: (i, 0))],
        out_shape=[_sds((rows, d), BF16), _sds((rows, 1), F32)], compiler_params=_params(),
    )(hs, g)


def _ffn_gu(h, wg, wu, tm, name):
    rows, d = h.shape
    nb, _, fb = wg.shape

    def body(h_ref, wg_ref, wu_ref, g_ref, u_ref):
        hv = h_ref[...]
        g_ref[0] = jnp.dot(hv, wg_ref[0], preferred_element_type=F32).astype(BF16)
        u_ref[0] = jnp.dot(hv, wu_ref[0], preferred_element_type=F32).astype(BF16)

    wspec = pl.BlockSpec((1, d, fb), lambda j, i: (j, 0, 0))
    ospec = pl.BlockSpec((1, tm, fb), lambda j, i: (j, i, 0))
    return _call(
        body, name=name, grid=(nb, rows // tm),
        in_specs=[pl.BlockSpec((tm, d), lambda j, i: (i, 0)), wspec, wspec],
        out_specs=[ospec, ospec], out_shape=[_sds((nb, rows, fb), BF16)] * 2, compiler_params=_params(),
    )(h, wg, wu)


def _ffn_down_norm(g, u, wd, hs, gn, tm, name):
    nb, rows, fb = g.shape
    d = hs.shape[1]

    def body(g_ref, u_ref, wd_ref, hs_ref, gn_ref, hsn_ref, hn_ref, rn_ref, acc_ref):
        j = pl.program_id(1)

        @pl.when(j == 0)
        def _():
            acc_ref[...] = jnp.zeros_like(acc_ref)

        gv = g_ref[0].astype(F32)
        a = (gv * _sigmoid(gv) * u_ref[0].astype(F32)).astype(BF16)
        acc_ref[...] += jnp.dot(a, wd_ref[0], preferred_element_type=F32)

        @pl.when(j == nb - 1)
        def _():
            def group(rs, _):
                hsn = hs_ref[rs, :] + FFN_RES_SCALE * acc_ref[rs, :]
                r = lax.rsqrt(jnp.mean(hsn * hsn, axis=-1, keepdims=True) + EPS)
                hsn_ref[rs, :] = hsn
                hn_ref[rs, :] = (hsn * r * gn_ref[...]).astype(BF16)
                rn_ref[rs, :] = r

            _for_row_groups(tm, group)

    aspec = pl.BlockSpec((1, tm, fb), lambda i, j: (j, i, 0))
    row = pl.BlockSpec((tm, d), lambda i, j: (i, 0))
    return _call(
        body, name=name, grid=(rows // tm, nb),
        in_specs=[aspec, aspec, pl.BlockSpec((1, fb, d), lambda i, j: (j, 0, 0)), _once((tm, d), lambda i, j: (i, 0)),
                  pl.BlockSpec((1, d), lambda i, j: (0, 0))],
        out_specs=[row, row, pl.BlockSpec((tm, 1), lambda i, j: (i, 0))],
        out_shape=[_sds((rows, d), F32), _sds((rows, d), BF16), _sds((rows, 1), F32)],
        scratch_shapes=[pltpu.VMEM((tm, d), F32)], compiler_params=_params(),
    )(g, u, wd, hs, gn)


def _ffn_down_loss(g, u, wd, hs, gf, tgt, n_seq, tm, name):
    nb, rows, fb = g.shape
    d = hs.shape[1]
    nt = rows // tm

    def body(g_ref, u_ref, wd_ref, hs_ref, gf_ref, tgt_ref, dhs_ref, dhsb_ref, loss_ref, dgf_ref, acc_ref):
        i, j = pl.program_id(0), pl.program_id(1)

        @pl.when(j == 0)
        def _():
            acc_ref[...] = jnp.zeros_like(acc_ref)

        gv = g_ref[0].astype(F32)
        a = (gv * _sigmoid(gv) * u_ref[0].astype(F32)).astype(BF16)
        acc_ref[...] += jnp.dot(a, wd_ref[0], preferred_element_type=F32)

        @pl.when(j == nb - 1)
        def _():
            loss_ref[0] = jnp.zeros((1, d), F32)
            dgf_ref[0] = jnp.zeros((1, d), F32)

            def group(rs, r0):
                hs3 = hs_ref[rs, :] + FFN_RES_SCALE * acc_ref[rs, :]
                r = lax.rsqrt(jnp.mean(hs3 * hs3, axis=-1, keepdims=True) + EPS)
                gfv = gf_ref[...]
                y = hs3 * r
                rowid = i * tm + r0 + lax.broadcasted_iota(jnp.int32, (SUB_ROWS, 1), 0)
                err = jnp.where(rowid < n_seq, y * gfv - tgt_ref[rs, :], 0.0)
                loss_ref[0] += jnp.sum(err * err, axis=0, keepdims=True)
                dout = err * (1.0 / d)
                dgf_ref[0] += jnp.sum(dout * y, axis=0, keepdims=True)
                t = dout * gfv
                dhs = r * t - hs3 * (r * r * r) * jnp.mean(t * hs3, axis=-1, keepdims=True)
                dhs_ref[rs, :] = dhs
                dhsb_ref[rs, :] = dhs.astype(BF16)

            _for_row_groups(tm, group)

    aspec = pl.BlockSpec((1, tm, fb), lambda i, j: (j, i, 0))
    row = pl.BlockSpec((tm, d), lambda i, j: (i, 0))
    once = _once((tm, d), lambda i, j: (i, 0))
    part = pl.BlockSpec((1, 1, d), lambda i, j: (i, 0, 0))
    return _call(
        body, name=name, grid=(nt, nb),
        in_specs=[aspec, aspec, pl.BlockSpec((1, fb, d), lambda i, j: (j, 0, 0)), once,
                  pl.BlockSpec((1, d), lambda i, j: (0, 0)), once],
        out_specs=[row, row, part, part],
        out_shape=[_sds((rows, d), F32), _sds((rows, d), BF16), _sds((nt, 1, d), F32), _sds((nt, 1, d), F32)],
        scratch_shapes=[pltpu.VMEM((tm, d), F32)], compiler_params=_params(),
    )(g, u, wd, hs, gf, tgt)


def _ffn_bwd_da(do, wd, g, u, tm, name):
    nb, rows, fb = g.shape
    d = do.shape[1]

    def body(do_ref, wd_ref, g_ref, u_ref, dg_ref, du_ref, a_ref):
        da = FFN_RES_SCALE * lax.dot_general(do_ref[...], wd_ref[0], NT, preferred_element_type=F32)
        gv = g_ref[0].astype(F32)
        uv = u_ref[0].astype(F32)
        s = _sigmoid(gv)
        sg = gv * s
        a_ref[0] = (sg * uv).astype(BF16)
        du_ref[0] = (da * sg).astype(BF16)
        dg_ref[0] = (da * uv * (s * (1.0 + gv * (1.0 - s)))).astype(BF16)

    aspec = pl.BlockSpec((1, tm, fb), lambda j, i: (j, i, 0))
    return _call(
        body, name=name, grid=(nb, rows // tm),
        in_specs=[pl.BlockSpec((tm, d), lambda j, i: (i, 0)), pl.BlockSpec((1, fb, d), lambda j, i: (j, 0, 0)), aspec, aspec],
        out_specs=[aspec] * 3, out_shape=[_sds((nb, rows, fb), BF16)] * 3, compiler_params=_params(),
    )(do, wd, g, u)


def _rms_bwd(tm, d, dh_ref, hs_ref, r_ref, gn_ref, dres_ref, dhs_ref, dhsb_ref, dgn_ref):
    dgn_ref[0] = jnp.zeros((1, d), F32)

    def group(rs, _):
        dh, hs, r = dh_ref[rs, :], hs_ref[rs, :], r_ref[rs, :]
        dgn_ref[0] += jnp.sum(dh * (hs * r), axis=0, keepdims=True)
        t = dh * gn_ref[...]
        dhs = dres_ref[rs, :] + r * t - hs * (r * r * r) * jnp.mean(t * hs, axis=-1, keepdims=True)
        dhs_ref[rs, :] = dhs
        dhsb_ref[rs, :] = dhs.astype(BF16)

    _for_row_groups(tm, group)


def _ffn_bwd_dh(dg, du, wg, wu, hs, r, gn, dres, tm, name):
    nb, rows, fb = dg.shape
    d = hs.shape[1]
    nt = rows // tm

    def body(dg_ref, du_ref, wg_ref, wu_ref, hs_ref, r_ref, gn_ref, dres_ref, dhs_ref, dhsb_ref, dgn_ref, acc_ref):
        j = pl.program_id(1)

        @pl.when(j == 0)
        def _():
            acc_ref[...] = jnp.zeros_like(acc_ref)

        acc_ref[...] += (lax.dot_general(dg_ref[0], wg_ref[0], NT, preferred_element_type=F32)
                         + lax.dot_general(du_ref[0], wu_ref[0], NT, preferred_element_type=F32))

        @pl.when(j == nb - 1)
        def _():
            _rms_bwd(tm, d, acc_ref, hs_ref, r_ref, gn_ref, dres_ref, dhs_ref, dhsb_ref, dgn_ref)

    aspec = pl.BlockSpec((1, tm, fb), lambda i, j: (j, i, 0))
    wspec = pl.BlockSpec((1, d, fb), lambda i, j: (j, 0, 0))
    row = pl.BlockSpec((tm, d), lambda i, j: (i, 0))
    once = _once((tm, d), lambda i, j: (i, 0))
    return _call(
        body, name=name, grid=(nt, nb),
        in_specs=[aspec, aspec, wspec, wspec, once, pl.BlockSpec((tm, 1), lambda i, j: (i, 0)),
                  pl.BlockSpec((1, d), lambda i, j: (0, 0)), once],
        out_specs=[row, row, pl.BlockSpec((1, 1, d), lambda i, j: (i, 0, 0))],
        out_shape=[_sds((rows, d), F32), _sds((rows, d), BF16), _sds((nt, 1, d), F32)],
        scratch_shapes=[pltpu.VMEM((tm, d), F32)], compiler_params=_params(),
    )(dg, du, wg, wu, hs, r, gn, dres)


def _ffn_bwd_dw(h, dg, du, a, do, tk, name):
    nb, rows, fb = dg.shape
    d = h.shape[1]
    nk = rows // tk

    def body(h_ref, dg_ref, du_ref, a_ref, do_ref, dwg_ref, dwu_ref, dwd_ref, accg, accu, accd):
        k = pl.program_id(1)

        @pl.when(k == 0)
        def _():
            accg[...] = jnp.zeros_like(accg)
            accu[...] = jnp.zeros_like(accu)
            accd[...] = jnp.zeros_like(accd)

        hv = h_ref[...]
        accg[...] += lax.dot_general(hv, dg_ref[0], TN, preferred_element_type=F32)
        accu[...] += lax.dot_general(hv, du_ref[0], TN, preferred_element_type=F32)
        accd[...] += lax.dot_general(a_ref[0], do_ref[...], TN, preferred_element_type=F32)

        @pl.when(k == nk - 1)
        def _():
            dwg_ref[0] = accg[...].astype(BF16)
            dwu_ref[0] = accu[...].astype(BF16)
            dwd_ref[0] = (FFN_RES_SCALE * accd[...]).astype(BF16)

    aspec = pl.BlockSpec((1, tk, fb), lambda j, k: (j, k, 0))
    row = pl.BlockSpec((tk, d), lambda j, k: (k, 0))
    up = pl.BlockSpec((1, d, fb), lambda j, k: (j, 0, 0))
    down = pl.BlockSpec((1, fb, d), lambda j, k: (j, 0, 0))
    return _call(
        body, name=name, grid=(nb, nk),
        in_specs=[row, aspec, aspec, aspec, row], out_specs=[up, up, down],
        out_shape=[_sds((nb, d, fb), BF16), _sds((nb, d, fb), BF16), _sds((nb, fb, d), BF16)],
        scratch_shapes=[pltpu.VMEM((d, fb), F32), pltpu.VMEM((d, fb), F32), pltpu.VMEM((fb, d), F32)],
        compiler_params=_params(),
    )(h, dg, du, a, do)


def _win_fwd(h, w, b, dc, tm, name):
    rows, d = h.shape
    ng = w.shape[1] // dc

    def body(h_ref, w_ref, b_ref, u_ref):
        u_ref[...] = (jnp.dot(h_ref[...], w_ref[...], preferred_element_type=F32) + b_ref[...]).astype(BF16)

    return _call(
        body, name=name, grid=(ng, rows // tm),
        in_specs=[pl.BlockSpec((tm, d), lambda m, i: (i, 0)), pl.BlockSpec((d, dc), lambda m, i: (0, m)),
                  pl.BlockSpec((1, dc), lambda m, i: (0, m))],
        out_specs=pl.BlockSpec((tm, dc), lambda m, i: (i, m)), out_shape=_sds((rows, ng * dc), BF16),
        compiler_params=_params(),
    )(h, w, b)


def _win_bwd_dh(du, w, hs, r, gn, dres, tm, name):
    rows, d = hs.shape
    ng, _, dc = du.shape
    nt = rows // tm

    def body(du_ref, w_ref, hs_ref, r_ref, gn_ref, dres_ref, dhs_ref, dhsb_ref, dgn_ref, acc_ref):
        m = pl.program_id(1)

        @pl.when(m == 0)
        def _():
            acc_ref[...] = jnp.zeros_like(acc_ref)

        acc_ref[...] += lax.dot_general(du_ref[0], w_ref[...], NT, preferred_element_type=F32)

        @pl.when(m == ng - 1)
        def _():
            _rms_bwd(tm, d, acc_ref, hs_ref, r_ref, gn_ref, dres_ref, dhs_ref, dhsb_ref, dgn_ref)

    row = pl.BlockSpec((tm, d), lambda i, m: (i, 0))
    once = _once((tm, d), lambda i, m: (i, 0))
    return _call(
        body, name=name, grid=(nt, ng),
        in_specs=[pl.BlockSpec((1, tm, dc), lambda i, m: (m, i, 0)), pl.BlockSpec((d, dc), lambda i, m: (0, m)), once,
                  pl.BlockSpec((tm, 1), lambda i, m: (i, 0)), pl.BlockSpec((1, d), lambda i, m: (0, 0)), once],
        out_specs=[row, row, pl.BlockSpec((1, 1, d), lambda i, m: (i, 0, 0))],
        out_shape=[_sds((rows, d), F32), _sds((rows, d), BF16), _sds((nt, 1, d), F32)],
        scratch_shapes=[pltpu.VMEM((tm, d), F32)], compiler_params=_params(),
    )(du, w, hs, r, gn, dres)


def _win_bwd_dw(h, du, tk, name):
    rows, d = h.shape
    ng, _, dc = du.shape
    nk = rows // tk

    def body(h_ref, du_ref, dw_ref, db_ref, acc, accb):
        k = pl.program_id(1)

        @pl.when(k == 0)
        def _():
            acc[...] = jnp.zeros_like(acc)
            accb[...] = jnp.zeros_like(accb)

        duv = du_ref[0]
        acc[...] += lax.dot_general(h_ref[...], duv, TN, preferred_element_type=F32)
        accb[...] += jnp.sum(duv.astype(F32), axis=0, keepdims=True)

        @pl.when(k == nk - 1)
        def _():
            dw_ref[...] = acc[...].astype(BF16)
            db_ref[...] = accb[...]

    return _call(
        body, name=name, grid=(ng, nk),
        in_specs=[pl.BlockSpec((tk, d), lambda m, k: (k, 0)), pl.BlockSpec((1, tk, dc), lambda m, k: (m, k, 0))],
        out_specs=[pl.BlockSpec((d, dc), lambda m, k: (0, m)), pl.BlockSpec((1, dc), lambda m, k: (0, m))],
        out_shape=[_sds((d, ng * dc), BF16), _sds((1, ng * dc), F32)],
        scratch_shapes=[pltpu.VMEM((d, dc), F32), pltpu.VMEM((1, dc), F32)], compiler_params=_params(),
    )(h, du)


def _layernorm_silu(zc, lg, lb):
    mu = jnp.mean(zc, axis=-1, keepdims=True)
    xc = zc - mu
    rstd = lax.rsqrt(jnp.mean(xc * xc, axis=-1, keepdims=True) + EPS)
    nrm = xc * rstd
    lin = nrm * lg + lb
    s = _sigmoid(lin)
    return nrm, rstd, lin, s


def _wout_fwd(zc, ysc, wout, hs, lg, lb, gn, tm, name):
    rows, dc = zc.shape
    d = hs.shape[1]

    def body(zc_ref, ysc_ref, w_ref, hs_ref, lg_ref, lb_ref, gn_ref, y_ref, hsn_ref, hn_ref, rn_ref):
        def mix(rs, _):
            _, _, lin, s = _layernorm_silu(zc_ref[rs, :], lg_ref[...], lb_ref[...])
            y_ref[rs, :dc] = ysc_ref[rs, :]
            y_ref[rs, dc:] = (lin * s).astype(BF16)

        _for_row_groups(tm, mix)
        hsn_ref[...] = jnp.dot(y_ref[...], w_ref[...], preferred_element_type=F32)

        def norm(rs, _):
            hsn = hs_ref[rs, :] + hsn_ref[rs, :]
            r = lax.rsqrt(jnp.mean(hsn * hsn, axis=-1, keepdims=True) + EPS)
            hsn_ref[rs, :] = hsn
            hn_ref[rs, :] = (hsn * r * gn_ref[...]).astype(BF16)
            rn_ref[rs, :] = r

        _for_row_groups(tm, norm)

    half = pl.BlockSpec((tm, dc), lambda i: (i, 0))
    row = pl.BlockSpec((tm, d), lambda i: (i, 0))
    vec_c = pl.BlockSpec((1, dc), lambda i: (0, 0))
    return _call(
        body, name=name, grid=(rows // tm,),
        in_specs=[half, half, _once((2 * dc, d), lambda i: (0, 0)), row, vec_c, vec_c,
                  pl.BlockSpec((1, d), lambda i: (0, 0))],
        out_specs=[pl.BlockSpec((tm, 2 * dc), lambda i: (i, 0)), row, row, pl.BlockSpec((tm, 1), lambda i: (i, 0))],
        out_shape=[_sds((rows, 2 * dc), BF16), _sds((rows, d), F32), _sds((rows, d), BF16), _sds((rows, 1), F32)],
        compiler_params=_params(),
    )(zc, ysc, wout, hs, lg, lb, gn)


def _wout_bwd_dy(do, wout, zc, lg, lb, tm, name):
    rows, dc = zc.shape
    d = do.shape[1]
    nt = rows // tm

    def body(do_ref, w_ref, zc_ref, lg_ref, lb_ref, dysc_ref, dzc_ref, dlg_ref, dlb_ref, dy_ref):
        dy_ref[...] = lax.dot_general(do_ref[...], w_ref[...], NT, preferred_element_type=F32)
        dlb_ref[0] = jnp.zeros((1, dc), F32)
        dlg_ref[0] = jnp.zeros((1, dc), F32)

        def group(rs, _):
            dysc_ref[rs, :] = dy_ref[rs, :dc].astype(BF16)
            nrm, rstd, lin, s = _layernorm_silu(zc_ref[rs, :], lg_ref[...], lb_ref[...])
            dl = dy_ref[rs, dc:] * (s * (1.0 + lin * (1.0 - s)))
            dlb_ref[0] += jnp.sum(dl, axis=0, keepdims=True)
            dlg_ref[0] += jnp.sum(dl * nrm, axis=0, keepdims=True)
            dn = dl * lg_ref[...]
            dzc_ref[rs, :] = rstd * (dn - jnp.mean(dn, axis=-1, keepdims=True)
                                     - nrm * jnp.mean(dn * nrm, axis=-1, keepdims=True))

        _for_row_groups(tm, group)

    half = pl.BlockSpec((tm, dc), lambda i: (i, 0))
    vec_c = pl.BlockSpec((1, dc), lambda i: (0, 0))
    part = pl.BlockSpec((1, 1, dc), lambda i: (i, 0, 0))
    return _call(
        body, name=name, grid=(nt,),
        in_specs=[pl.BlockSpec((tm, d), lambda i: (i, 0)), _once((2 * dc, d), lambda i: (0, 0)), half, vec_c, vec_c],
        out_specs=[half, half, part, part],
        out_shape=[_sds((rows, dc), BF16), _sds((rows, dc), F32), _sds((nt, 1, dc), F32), _sds((nt, 1, dc), F32)],
        scratch_shapes=[pltpu.VMEM((tm, 2 * dc), F32)], compiler_params=_params(),
    )(do, wout, zc, lg, lb)


def _wout_bwd_dw(y, do, nb, tk, name):
    rows, k2 = y.shape
    d = do.shape[1]
    ob = k2 // nb
    nk = rows // tk

    def body(y_ref, do_ref, dw_ref, acc):
        k = pl.program_id(1)

        @pl.when(k == 0)
        def _():
            acc[...] = jnp.zeros_like(acc)

        acc[...] += lax.dot_general(y_ref[...], do_ref[...], TN, preferred_element_type=F32)

        @pl.when(k == nk - 1)
        def _():
            dw_ref[0] = acc[...].astype(BF16)

    return _call(
        body, name=name, grid=(nb, nk),
        in_specs=[pl.BlockSpec((tk, ob), lambda j, k: (k, j)), pl.BlockSpec((tk, d), lambda j, k: (k, 0))],
        out_specs=pl.BlockSpec((1, ob, d), lambda j, k: (j, 0, 0)), out_shape=_sds((nb, ob, d), BF16),
        scratch_shapes=[pltpu.VMEM((ob, d), F32)], compiler_params=_params(),
    )(y, do)


def _windows(win, n_res):
    length = win.shape[0]
    return [win if r == 0 else pltpu.roll(win, length - r, 0) for r in range(n_res)]


def _taps(src_ref, start, offsets, ch):
    span = -(-(max(offsets) + ch) // SUBLANES) * SUBLANES
    win = src_ref[pl.ds(start, span), :]
    shifted = _windows(win, min(SUBLANES, max(offsets) + 1))
    return [shifted[o % SUBLANES][(o // SUBLANES) * SUBLANES:(o // SUBLANES) * SUBLANES + ch] for o in offsets]


def _rows8(v):
    return jnp.sum(v.reshape(v.shape[0] // SUBLANES, SUBLANES, v.shape[1]), axis=0)


def _conv_geometry(n_seq, n_meta):
    base = CONV_PAD + n_meta
    off_cf = base - (CF_WIDTH - 1)
    off_sc = base - (SC_WIDTH - 1)
    logical = -(-(n_meta + n_seq) // CONV_CH) * CONV_CH
    return base, off_cf, off_sc, logical


def _fill_conv_inputs(c_ref, v_ref, a_ref, g_ref, scv, sz, n_seq, n_meta):
    base = CONV_PAD + n_meta
    cb = scv.shape[1]
    scv[0:CONV_PAD, :] = jnp.zeros((CONV_PAD, cb), F32)
    sz[0:CONV_PAD, :] = jnp.zeros((CONV_PAD, cb), F32)

    def put(src, dst, n):
        cv = c_ref[src, :].astype(F32) * v_ref[src, :].astype(F32)
        scv[dst, :] = cv
        sz[dst, :] = a_ref[src, :].astype(F32) * _sigmoid(g_ref[src, :].astype(F32))

    put(pl.ds(n_seq, n_meta), pl.ds(CONV_PAD, n_meta), n_meta)

    def chunk(i, carry):
        t0 = pl.multiple_of(i * CONV_CH, CONV_CH)
        put(pl.ds(t0, CONV_CH), pl.ds(base + t0, CONV_CH), CONV_CH)
        return carry

    lax.fori_loop(0, n_seq // CONV_CH, chunk, 0)


def _conv_fwd(u, wsc, wcf, cbias, n_seq, n_meta, name):
    rows = u.shape[0]
    nb, _, cb = wsc.shape
    dc = nb * cb
    base, off_cf, off_sc, _ = _conv_geometry(n_seq, n_meta)
    a_cf, a_sc = off_cf // SUBLANES * SUBLANES, off_sc // SUBLANES * SUBLANES
    ch = CONV_CH

    def body(b_ref, c_ref, v_ref, a_ref, g_ref, wsc_ref, wcf_ref, cb_ref, ysc_ref, zc_ref, scv, sz):
        _fill_conv_inputs(c_ref, v_ref, a_ref, g_ref, scv, sz, n_seq, n_meta)
        w3, w31, bias = wsc_ref[0], wcf_ref[0], cb_ref[...]

        def chunk(i, carry):
            t0 = pl.multiple_of(i * ch, ch)
            acc = jnp.zeros((ch, cb), F32)
            for k, win in enumerate(_taps(sz, t0 + a_cf, [off_cf - a_cf + k for k in range(CF_WIDTH)], ch)):
                acc = acc + win * w31[k:k + 1, :]
            zc_ref[pl.ds(t0, ch), :] = acc + bias
            s = jnp.zeros((ch, cb), F32)
            for k, win in enumerate(_taps(scv, t0 + a_sc, [off_sc - a_sc + k for k in range(SC_WIDTH)], ch)):
                s = s + win * w3[k:k + 1, :]
            ysc_ref[pl.ds(t0, ch), :] = (b_ref[pl.ds(t0, ch), :].astype(F32) * s).astype(BF16)
            return carry

        lax.fori_loop(0, n_seq // ch, chunk, 0)
        ysc_ref[n_seq:rows, :] = jnp.zeros((rows - n_seq, cb), BF16)
        zc_ref[n_seq:rows, :] = jnp.zeros((rows - n_seq, cb), F32)

    ucol = [pl.BlockSpec((rows, cb), functools.partial(lambda m, j: (0, m * nb + j), m)) for m in range(5)]
    blk = pl.BlockSpec((rows, cb), lambda j: (0, j))
    return _call(
        body, name=name, grid=(nb,),
        in_specs=ucol + [pl.BlockSpec((1, SC_WIDTH, cb), lambda j: (j, 0, 0)),
                         pl.BlockSpec((1, CF_WIDTH, cb), lambda j: (j, 0, 0)), pl.BlockSpec((1, cb), lambda j: (0, j))],
        out_specs=[blk, blk], out_shape=[_sds((rows, dc), BF16), _sds((rows, dc), F32)],
        scratch_shapes=[pltpu.VMEM((base + n_seq, cb), F32)] * 2, compiler_params=_params(),
    )(u, u, u, u, u, wsc, wcf, cbias)


def _conv_bwd(u, dysc, dzc, wsc, wcf, n_seq, n_meta, name):
    rows = u.shape[0]
    nb, _, cb = wsc.shape
    dc = nb * cb
    base, off_cf, off_sc, logical = _conv_geometry(n_seq, n_meta)
    a_cf, a_sc = off_cf // SUBLANES * SUBLANES, off_sc // SUBLANES * SUBLANES
    ch = CONV_CH
    tail = CONV_PAD

    def body(b_ref, c_ref, v_ref, a_ref, g_ref, dysc_ref, dzc_ref, wsc_ref, wcf_ref,
             du_ref, dwsc_ref, dwcf_ref, dcb_ref,
             scv, sz, sds_, sdz, dlcv, dlz, accsc, acccf, accb):
        _fill_conv_inputs(c_ref, v_ref, a_ref, g_ref, scv, sz, n_seq, n_meta)
        w3, w31 = wsc_ref[0], wcf_ref[0]
        dub_ref, duc_ref, duv_ref, dua_ref, dug_ref = (du_ref.at[m] for m in range(5))
        sds_[0:n_meta, :] = jnp.zeros((n_meta, cb), F32)
        sdz[0:n_meta, :] = jnp.zeros((n_meta, cb), F32)
        behind = logical + tail - (n_meta + n_seq)
        sds_[n_meta + n_seq:logical + tail, :] = jnp.zeros((behind, cb), F32)
        sdz[n_meta + n_seq:logical + tail, :] = jnp.zeros((behind, cb), F32)
        accsc[...] = jnp.zeros_like(accsc)
        acccf[...] = jnp.zeros_like(acccf)
        accb[...] = jnp.zeros_like(accb)

        def forward_chunk(i, carry):
            t0 = pl.multiple_of(i * ch, ch)
            rws = pl.ds(t0, ch)
            dy = dysc_ref[rws, :].astype(F32)
            ds = dy * b_ref[rws, :].astype(F32)
            dz = dzc_ref[rws, :]
            sds_[pl.ds(n_meta + t0, ch), :] = ds
            sdz[pl.ds(n_meta + t0, ch), :] = dz
            s = jnp.zeros((ch, cb), F32)
            for k, win in enumerate(_taps(scv, t0 + a_sc, [off_sc - a_sc + k for k in range(SC_WIDTH)], ch)):
                s = s + win * w3[k:k + 1, :]
                accsc[k * SUBLANES:(k + 1) * SUBLANES, :] += _rows8(ds * win)
            dub_ref[rws, :] = (dy * s).astype(BF16)
            for k, win in enumerate(_taps(sz, t0 + a_cf, [off_cf - a_cf + k for k in range(CF_WIDTH)], ch)):
                acccf[k * SUBLANES:(k + 1) * SUBLANES, :] += _rows8(dz * win)
            accb[...] += _rows8(dz)
            return carry

        lax.fori_loop(0, n_seq // ch, forward_chunk, 0)

        def backward_chunk(i, carry):
            p0 = pl.multiple_of(i * ch, ch)
            dcv = jnp.zeros((ch, cb), F32)
            for k, win in enumerate(_taps(sds_, p0, [SC_WIDTH - 1 - k for k in range(SC_WIDTH)], ch)):
                dcv = dcv + win * w3[k:k + 1, :]
            dlcv[pl.ds(p0, ch), :] = dcv
            dzi = jnp.zeros((ch, cb), F32)
            for k, win in enumerate(_taps(sdz, p0, [CF_WIDTH - 1 - k for k in range(CF_WIDTH)], ch)):
                dzi = dzi + win * w31[k:k + 1, :]
            dlz[pl.ds(p0, ch), :] = dzi
            return carry

        lax.fori_loop(0, logical // ch, backward_chunk, 0)

        def gates(phys, logi):
            dcv, dzi = dlcv[logi, :], dlz[logi, :]
            duc_ref[phys, :] = (dcv * v_ref[phys, :].astype(F32)).astype(BF16)
            duv_ref[phys, :] = (dcv * c_ref[phys, :].astype(F32)).astype(BF16)
            s = _sigmoid(g_ref[phys, :].astype(F32))
            dua_ref[phys, :] = (dzi * s).astype(BF16)
            dug_ref[phys, :] = (dzi * a_ref[phys, :].astype(F32) * s * (1.0 - s)).astype(BF16)

        def gate_chunk(i, carry):
            t0 = pl.multiple_of(i * ch, ch)
            gates(pl.ds(t0, ch), pl.ds(n_meta + t0, ch))
            return carry

        lax.fori_loop(0, n_seq // ch, gate_chunk, 0)
        gates(pl.ds(n_seq, n_meta), pl.ds(0, n_meta))
        dub_ref[n_seq:rows, :] = jnp.zeros((rows - n_seq, cb), BF16)
        pad0 = n_seq + n_meta
        for ref in (duc_ref, duv_ref, dua_ref, dug_ref):
            ref[pad0:rows, :] = jnp.zeros((rows - pad0, cb), BF16)
        dwsc_ref[0] = jnp.sum(accsc[...].reshape(SC_WIDTH, SUBLANES, cb), axis=1)
        dwcf_ref[0] = jnp.sum(acccf[...].reshape(CF_WIDTH, SUBLANES, cb), axis=1)
        dcb_ref[...] = jnp.sum(accb[...], axis=0, keepdims=True)

    ucol = [pl.BlockSpec((rows, cb), functools.partial(lambda m, j: (0, m * nb + j), m)) for m in range(5)]
    blk = pl.BlockSpec((rows, cb), lambda j: (0, j))
    wsc_spec = pl.BlockSpec((1, SC_WIDTH, cb), lambda j: (j, 0, 0))
    wcf_spec = pl.BlockSpec((1, CF_WIDTH, cb), lambda j: (j, 0, 0))
    outs = _call(
        body, name=name, grid=(nb,),
        in_specs=ucol + [blk, blk, wsc_spec, wcf_spec],
        out_specs=[pl.BlockSpec((5, rows, cb), lambda j: (0, 0, j)), wsc_spec, wcf_spec,
                   pl.BlockSpec((1, cb), lambda j: (0, j))],
        out_shape=[_sds((5, rows, dc), BF16), _sds((nb, SC_WIDTH, cb), F32), _sds((nb, CF_WIDTH, cb), F32),
                   _sds((1, dc), F32)],
        scratch_shapes=[pltpu.VMEM((base + n_seq, cb), F32)] * 2 + [pltpu.VMEM((logical + tail, cb), F32)] * 2
        + [pltpu.VMEM((logical, cb), F32)] * 2
        + [pltpu.VMEM((SC_WIDTH * SUBLANES, cb), F32), pltpu.VMEM((CF_WIDTH * SUBLANES, cb), F32),
           pltpu.VMEM((SUBLANES, cb), F32)],
        compiler_params=_params(),
    )(u, u, u, u, u, dysc, dzc, wsc, wcf)
    return outs


def _row_tile(rows, cols):
    return rows // 4 if rows % 64 == 0 and rows * cols >= (1 << 18) else rows


def _pair_sum(grad, sib, idx, name):
    _, rows, cols = grad.shape
    tr = _row_tile(rows, cols)

    def body(idx_ref, g_ref, s_ref, o_ref):
        o_ref[0] = (g_ref[0].astype(F32) + s_ref[0].astype(F32)).astype(o_ref.dtype)

    return _call(
        body, name=name,
        grid_spec=pltpu.PrefetchScalarGridSpec(
            num_scalar_prefetch=1, grid=(4, rows // tr),
            in_specs=[pl.BlockSpec((1, tr, cols), lambda k, i, idx_ref: (idx_ref[k], i, 0)),
                      pl.BlockSpec((1, tr, cols), lambda k, i, idx_ref: (idx_ref[4 + k], i, 0))],
            out_specs=pl.BlockSpec((1, tr, cols), lambda k, i, idx_ref: (k, i, 0))),
        out_shape=_sds((4, rows, cols), grad.dtype), compiler_params=_params(),
    )(idx, grad, sib)


def _adamw_math(w, g, m, v):
    m = ADAM_B1 * m + (1.0 - ADAM_B1) * g
    v = ADAM_B2 * v + (1.0 - ADAM_B2) * (g * g)
    m_hat = m / (1.0 - ADAM_B1 ** ADAM_STEP)
    v_hat = v / (1.0 - ADAM_B2 ** ADAM_STEP)
    delta = -ADAM_LR * (m_hat / (jnp.sqrt(v_hat) + ADAM_EPS) + ADAM_WD * w)
    return delta, m, v


def _adamw_sharded(own, got, w, m, v, name):
    rows, cols = w.shape
    tr = _row_tile(rows, cols)

    def body(own_ref, g0_ref, g1_ref, g2_ref, w_ref, m_ref, v_ref, g_ref, d_ref, nm_ref, nv_ref):
        g = own_ref[0].astype(F32) + g0_ref[0].astype(F32) + g1_ref[0].astype(F32) + g2_ref[0].astype(F32)
        delta, nm, nv = _adamw_math(w_ref[...], g, m_ref[...], v_ref[...])
        g_ref[...] = g
        d_ref[...] = delta
        nm_ref[...] = nm
        nv_ref[...] = nv

    flat = pl.BlockSpec((tr, cols), lambda i: (i, 0))
    slot = [pl.BlockSpec((1, tr, cols), functools.partial(lambda k, i: (k, i, 0), k)) for k in range(3)]
    return _call(
        body, name=name, grid=(rows // tr,),
        in_specs=[slot[0]] + slot + [flat] * 3, out_specs=[flat] * 4, out_shape=[_sds((rows, cols), F32)] * 4,
        compiler_params=_params(),
    )(own, got, got, got, w, m, v)


def _adamw_replicated(gathered, segs, ws, ms, vs, loss_scale, name):
    n = len(ws)

    def body(*refs):
        gat = refs[0]
        w_refs, m_refs, v_refs = refs[1:1 + n], refs[1 + n:1 + 2 * n], refs[1 + 2 * n:1 + 3 * n]
        outs = refs[1 + 3 * n:]

        def total(off, width):
            s = gat[0, :, off:off + width]
            for k in range(1, N_DEV):
                s = s + gat[k, :, off:off + width]
            return s

        outs[0][...] = loss_scale * total(segs[n][0], segs[n][1])
        for p in range(n):
            g = total(*segs[p])
            delta, nm, nv = _adamw_math(w_refs[p][...], g, m_refs[p][...], v_refs[p][...])
            for q, val in enumerate((g, delta, nm, nv)):
                outs[1 + 4 * p + q][...] = val

    return _call(
        body, name=name,
        out_shape=[_sds((1, segs[n][1]), F32)] + [_sds(w.shape, F32) for w in ws for _ in range(4)],
        compiler_params=_params(),
    )(gathered, *ws, *ms, *vs)


REPLICATED = ("ffn1_norm", "mix_norm", "b_in", "conv_cf_b", "ln_cf_g", "ln_cf_b", "ffn2_norm", "final_norm")
SHARDED = ("meta_tokens", "ffn1_w_gate", "ffn1_w_up", "ffn1_w_down", "w_in", "conv_sc_w", "conv_cf_w", "w_out",
           "ffn2_w_gate", "ffn2_w_up", "ffn2_w_down")
WEIGHTS = ("meta_tokens", "ffn1_norm", "ffn1_w_gate", "ffn1_w_up", "ffn1_w_down", "mix_norm", "w_in", "b_in",
           "conv_sc_w", "conv_cf_w", "conv_cf_b", "ln_cf_g", "ln_cf_b", "w_out", "ffn2_norm", "ffn2_w_gate",
           "ffn2_w_up", "ffn2_w_down", "final_norm")


def _blocks2d(a):
    return a.reshape(a.shape[-2:]) if a.ndim >= 2 else a.reshape(1, -1)


def _step(x, tgt, w, m, v):
    n_seq, d = x.shape[1], x.shape[2]
    n_meta = w["meta_tokens"].shape[0]
    rows = -(-(n_seq + n_meta) // ROW_ALIGN) * ROW_ALIGN
    tm = rows // N_ROW_TILES
    cb = w["conv_sc_w"].shape[-1]
    dc = cb * N_DEV
    w2 = {k: _blocks2d(a) for k, a in w.items()}
    m2 = {k: _blocks2d(a) for k, a in m.items()}
    v2 = {k: _blocks2d(a) for k, a in v.items()}

    big = ("ffn1_w_gate", "ffn1_w_up", "ffn1_w_down", "w_in", "w_out", "ffn2_w_gate", "ffn2_w_up", "ffn2_w_down")
    small = ("meta_tokens", "conv_sc_w", "conv_cf_w")
    gathered = _all_gather([w2[k].astype(BF16) for k in big] + [w2[k] for k in small], "gather_weights")
    full = dict(zip(big + small, gathered))
    win = jnp.transpose(full["w_in"], (1, 0, 2)).reshape(d, -1)
    wout = full["w_out"].reshape(-1, d)
    meta = jnp.transpose(full["meta_tokens"], (1, 0, 2)).reshape(n_meta, d)

    hs0 = jnp.concatenate([x[0], meta, jnp.zeros((rows - n_seq - n_meta, d), F32)], axis=0)
    tgt_rows = jnp.concatenate([tgt[0], jnp.zeros((rows - n_seq, d), F32)], axis=0)

    h1, r1 = _rms_fwd(hs0, w2["ffn1_norm"], tm, "rms_in")
    g1, u1 = _ffn_gu(h1, full["ffn1_w_gate"], full["ffn1_w_up"], tm, "ffn1_gu")
    hs1, h2, r2 = _ffn_down_norm(g1, u1, full["ffn1_w_down"], hs0, w2["mix_norm"], tm, "ffn1_down")
    u = _win_fwd(h2, win, w2["b_in"], dc, tm, "mix_in")
    ysc, zc = _conv_fwd(u, full["conv_sc_w"], full["conv_cf_w"], w2["conv_cf_b"], n_seq, n_meta, "conv_fwd")
    y, hs2, h3, r3 = _wout_fwd(zc, ysc, wout, hs1, w2["ln_cf_g"], w2["ln_cf_b"], w2["ffn2_norm"], tm, "mix_out")
    g2, u2 = _ffn_gu(h3, full["ffn2_w_gate"], full["ffn2_w_up"], tm, "ffn2_gu")
    dhs3, dhs3b, loss_p, dgf_p = _ffn_down_loss(
        g2, u2, full["ffn2_w_down"], hs2, w2["final_norm"], tgt_rows, n_seq, tm, "ffn2_down_loss")

    dg2, du2, a2 = _ffn_bwd_da(dhs3b, full["ffn2_w_down"], g2, u2, tm, "ffn2_bwd_da")
    dhs2, dhs2b, dn3_p = _ffn_bwd_dh(
        dg2, du2, full["ffn2_w_gate"], full["ffn2_w_up"], hs2, r3, w2["ffn2_norm"], dhs3, tm, "ffn2_bwd_dh")
    grads = {}
    grads["ffn2_w_gate"], grads["ffn2_w_up"], grads["ffn2_w_down"] = _ffn_bwd_dw(h3, dg2, du2, a2, dhs3b, tm, "ffn2_bwd_dw")

    dysc, dzc, dlg_p, dlb_p = _wout_bwd_dy(dhs2b, wout, zc, w2["ln_cf_g"], w2["ln_cf_b"], tm, "mix_out_bwd_dy")
    grads["w_out"] = _wout_bwd_dw(y, dhs2b, N_DEV, tm, "mix_out_bwd_dw")
    du, grads["conv_sc_w"], grads["conv_cf_w"], dcb = _conv_bwd(
        u, dysc, dzc, full["conv_sc_w"], full["conv_cf_w"], n_seq, n_meta, "conv_bwd")
    dhs1, dhs1b, dn2_p = _win_bwd_dh(du, win, hs1, r2, w2["mix_norm"], dhs2, tm, "mix_in_bwd_dh")
    dwin, dbin = _win_bwd_dw(h2, du, tm, "mix_in_bwd_dw")
    grads["w_in"] = jnp.transpose(dwin.reshape(d, N_DEV, -1), (1, 0, 2))

    dg1, du1, a1 = _ffn_bwd_da(dhs1b, full["ffn1_w_down"], g1, u1, tm, "ffn1_bwd_da")
    dhs0, _, dn1_p = _ffn_bwd_dh(
        dg1, du1, full["ffn1_w_gate"], full["ffn1_w_up"], hs0, r1, w2["ffn1_norm"], dhs1, tm, "ffn1_bwd_dh")
    grads["ffn1_w_gate"], grads["ffn1_w_up"], grads["ffn1_w_down"] = _ffn_bwd_dw(h1, dg1, du1, a1, dhs1b, tm, "ffn1_bwd_dw")
    grad_x = dhs0[:n_seq][None]
    dmeta = dhs0[n_seq:n_seq + n_meta]
    grads["meta_tokens"] = jnp.transpose(dmeta.reshape(n_meta, N_DEV, -1), (1, 0, 2))

    xi, yi, ci = lax.axis_index("x"), lax.axis_index("y"), lax.axis_index("c")
    chip_of = [2 * xi + yi, 2 * (1 - xi) + yi, 2 * xi + (1 - yi), 2 * (1 - xi) + (1 - yi)]
    idx = jnp.stack([2 * ch + ci for ch in chip_of] + chip_of).astype(jnp.int32)
    order = list(SHARDED)
    parts = [grads[k] for k in order]
    sib = _sibling_exchange([p.reshape((4, 2) + p.shape[1:]) for p in parts], "reduce_pair")
    sums = [_pair_sum(p, s, idx, "pair_sum_" + k) for k, p, s in zip(order, parts, sib)]
    got = _chip_exchange(sums, "reduce_chips")
    out = {}
    for k, own, g3 in zip(order, sums, got):
        res = _adamw_sharded(own, g3, w2[k], m2[k], v2[k], "adamw_" + k)
        out[k] = [r.reshape(w[k].shape) for r in res]

    partial = {
        "ffn1_norm": dn1_p.sum(0), "mix_norm": dn2_p.sum(0), "b_in": dbin, "conv_cf_b": dcb,
        "ln_cf_g": dlg_p.sum(0), "ln_cf_b": dlb_p.sum(0), "ffn2_norm": dn3_p.sum(0), "final_norm": dgf_p.sum(0),
    }
    loss_seg = jnp.pad(loss_p.sum((0, 2)).reshape(1, 1), ((0, 0), (0, 127)))
    pieces = [partial[k] for k in REPLICATED] + [loss_seg]
    segs, off = [], 0
    for p in pieces:
        segs.append((off, p.shape[1]))
        off += p.shape[1]
    rows8 = _gather_rows(jnp.concatenate(pieces, axis=1), "gather_small")
    res = _adamw_replicated(rows8, segs, [w2[k] for k in REPLICATED], [m2[k] for k in REPLICATED],
                            [v2[k] for k in REPLICATED], 0.5 / d, "adamw_replicated")
    loss = res[0][0, 0]
    for p, k in enumerate(REPLICATED):
        out[k] = [r.reshape(w[k].shape) for r in res[1 + 4 * p:5 + 4 * p]]

    return (loss, grad_x, *[out[k][0] for k in WEIGHTS], *[out[k][1] for k in WEIGHTS],
            *[out[k][2] for k in WEIGHTS], *[out[k][3] for k in WEIGHTS])


def kernel(x, meta_tokens, ffn1_norm, ffn1_w_gate, ffn1_w_up, ffn1_w_down, mix_norm, w_in, b_in, conv_sc_w, conv_cf_w, conv_cf_b, ln_cf_g, ln_cf_b, w_out, ffn2_norm, ffn2_w_gate, ffn2_w_up, ffn2_w_down, final_norm, loss_target, m_meta_tokens, m_ffn1_norm, m_ffn1_w_gate, m_ffn1_w_up, m_ffn1_w_down, m_mix_norm, m_w_in, m_b_in, m_conv_sc_w, m_conv_cf_w, m_conv_cf_b, m_ln_cf_g, m_ln_cf_b, m_w_out, m_ffn2_norm, m_ffn2_w_gate, m_ffn2_w_up, m_ffn2_w_down, m_final_norm, v_meta_tokens, v_ffn1_norm, v_ffn1_w_gate, v_ffn1_w_up, v_ffn1_w_down, v_mix_norm, v_w_in, v_b_in, v_conv_sc_w, v_conv_cf_w, v_conv_cf_b, v_ln_cf_g, v_ln_cf_b, v_w_out, v_ffn2_norm, v_ffn2_w_gate, v_ffn2_w_up, v_ffn2_w_down, v_final_norm):
    given = dict(locals())
    w = {k: given[k] for k in WEIGHTS}
    m = {k: given["m_" + k] for k in WEIGHTS}
    v = {k: given["v_" + k] for k in WEIGHTS}
    return _step(x, loss_target, w, m, v)
```

```python
import functools

import jax
import jax.numpy as jnp
from jax import lax
from jax.experimental import pallas as pl
from jax.experimental.pallas import tpu as pltpu

F32 = jnp.float32
BF16 = jnp.bfloat16
EPS = 1e-6
FFN_RES_SCALE = 0.5
SC_WIDTH = 3
CF_WIDTH = 31
ADAM_LR = 0.001
ADAM_B1 = 0.9
ADAM_B2 = 0.999
ADAM_EPS = 1e-08
ADAM_WD = 0.01
ADAM_STEP = 10

N_DEV = 8
N_ROW_TILES = 8
ROW_ALIGN = 256
CONV_PAD = 32
CONV_CH = 128
SUB_ROWS = 32
SUBLANES = 8
VMEM_LIMIT = 56 * 1024 * 1024
MESH = pl.DeviceIdType.MESH
ANY = pl.BlockSpec(memory_space=pl.ANY)

NT = (((1,), (1,)), ((), ()))
TN = (((0,), (0,)), ((), ()))


def _pallas(body, **kw):
    return pl.pallas_call(body, **kw)


class _Exchange:
    def __init__(self, inputs, out_shapes, sem_shapes, start, finish, aliases=None):
        self.inputs, self.out_shapes, self.sem_shapes = list(inputs), list(out_shapes), list(sem_shapes)
        self.start, self.finish, self.aliases = start, finish, dict(aliases or {})
        self.results = None


def _call(body, comm=(), **kw):
    if not comm:
        return _pallas(body, **kw)
    grid = kw.pop("grid")
    in_specs = list(kw.pop("in_specs"))
    out_specs, out_shape = kw.pop("out_specs"), kw.pop("out_shape")
    scratch = list(kw.pop("scratch_shapes", []))
    single = not isinstance(out_shape, (list, tuple))
    out_specs, out_shape = ([out_specs], [out_shape]) if single else (list(out_specs), list(out_shape))
    n_in, n_out, n_scr = len(in_specs), len(out_shape), len(scratch)
    c_in = [a for job in comm for a in job.inputs]
    c_out = [s for job in comm for s in job.out_shapes]
    c_sem = [s for job in comm for s in job.sem_shapes]
    aliases, i0, o0 = {}, n_in, n_out
    for job in comm:
        aliases.update({i0 + i: o0 + o for i, o in job.aliases.items()})
        i0, o0 = i0 + len(job.inputs), o0 + len(job.out_shapes)

    def hosted(*refs):
        pos = [0]

        def take(n):
            pos[0] += n
            return refs[pos[0] - n:pos[0]]

        ins, cins, outs, couts, scr, sems = take(n_in), take(len(c_in)), take(n_out), take(len(c_out)), take(n_scr), take(len(c_sem))
        ids = [pl.program_id(k) for k in range(len(grid))]
        first = functools.reduce(jnp.logical_and, [i == 0 for i in ids])
        last = functools.reduce(jnp.logical_and, [i == g - 1 for i, g in zip(ids, grid)])

        def each(phase):
            i, o, s = 0, 0, 0
            for job in comm:
                ni, no, ns = len(job.inputs), len(job.out_shapes), len(job.sem_shapes)
                getattr(job, phase)(cins[i:i + ni], couts[o:o + no], sems[s:s + ns])
                i, o, s = i + ni, o + no, s + ns

        @pl.when(first)
        def _():
            each("start")

        body(*ins, *outs, *scr)

        @pl.when(last)
        def _():
            each("finish")

    call = _pallas(
        hosted, grid=grid, in_specs=in_specs + [ANY] * len(c_in), out_specs=out_specs + [ANY] * len(c_out),
        out_shape=out_shape + c_out, scratch_shapes=scratch + c_sem, input_output_aliases=aliases, **kw)

    def run(*args):
        res = call(*args, *c_in)
        o = n_out
        for job in comm:
            job.results = list(res[o:o + len(job.out_shapes)])
            o += len(job.out_shapes)
        return res[0] if single else list(res[:n_out])

    return run


def _exchange_alone(job, name, comm=()):
    n_in, n_out = len(job.inputs), len(job.out_shapes)

    def body(*refs):
        ins, outs, sems = refs[:n_in], refs[n_in:n_in + n_out], refs[n_in + n_out:]
        job.start(ins, outs, sems)
        job.finish(ins, outs, sems)

    res = _pallas(body, name=name, in_specs=[ANY] * n_in, out_specs=[ANY] * n_out, out_shape=job.out_shapes,
                  scratch_shapes=job.sem_shapes, input_output_aliases=job.aliases)(*job.inputs)
    job.results = list(res)
    return job.results


def _params(**kw):
    return pltpu.CompilerParams(vmem_limit_bytes=VMEM_LIMIT, **kw)


def _sigmoid(x):
    return 1.0 / (1.0 + jnp.exp(-x))


def _sds(shape, dtype):
    return jax.ShapeDtypeStruct(shape, dtype)


def _place():
    x, y, c = lax.axis_index("x"), lax.axis_index("y"), lax.axis_index("c")
    chips = [(1 - x, y), (x, 1 - y), (1 - x, 1 - y)]
    return x, y, c, chips


def _slot(ref, p):
    return ref.at[4 * p[0] + 2 * p[1] + p[2]]


def _remote(src, dst, send_sem, recv_sem, to):
    return pltpu.make_async_remote_copy(src_ref=src, dst_ref=dst, send_sem=send_sem, recv_sem=recv_sem,
                                        device_id=to, device_id_type=MESH)


def _gather_direct(arrs):
    n = len(arrs)

    def copies(ins, outs, sems):
        send_sems, recv_sems, local_sems = sems
        x, y, c, chips = _place()
        me = (x, y, c)
        peers = [(x, y, 1 - c)] + [(*chip, c) for chip in chips]
        local = [pltpu.make_async_copy(ins[a], _slot(outs[a], me), local_sems.at[a]) for a in range(n)]
        sends = [_remote(ins[a], _slot(outs[a], me), send_sems.at[a, k], recv_sems.at[a, k], peer)
                 for a in range(n) for k, peer in enumerate(peers)]
        arrivals = [_remote(ins[a], _slot(outs[a], peer), send_sems.at[a, k], recv_sems.at[a, k], peer)
                    for a in range(n) for k, peer in enumerate(peers)]
        return local, sends, arrivals

    def start(ins, outs, sems):
        local, sends, _ = copies(ins, outs, sems)
        for cp in local + sends:
            cp.start()

    def finish(ins, outs, sems):
        local, sends, arrivals = copies(ins, outs, sems)
        for cp in arrivals:
            cp.wait_recv()
        for cp in sends:
            cp.wait_send()
        for cp in local:
            cp.wait()

    dma = pltpu.SemaphoreType.DMA
    return _Exchange(arrs, [_sds((N_DEV,) + a.shape, a.dtype) for a in arrs], [dma((n, 4)), dma((n, 4)), dma((n,))],
                     start, finish)


def _gather_forward(gathered):
    n = len(gathered)

    def copies(ins, outs, sems):
        send_sems, recv_sems = sems
        x, y, c, chips = _place()
        sibling = (x, y, 1 - c)
        sends = [_remote(_slot(ins[a], (*chip, c)), _slot(outs[a], (*chip, c)), send_sems.at[a, j], recv_sems.at[a, j], sibling)
                 for a in range(n) for j, chip in enumerate(chips)]
        arrivals = [_remote(_slot(ins[a], (*chip, c)), _slot(outs[a], (*chip, 1 - c)), send_sems.at[a, j], recv_sems.at[a, j], sibling)
                    for a in range(n) for j, chip in enumerate(chips)]
        return sends, arrivals

    def start(ins, outs, sems):
        for cp in copies(ins, outs, sems)[0]:
            cp.start()

    def finish(ins, outs, sems):
        sends, arrivals = copies(ins, outs, sems)
        for cp in arrivals:
            cp.wait_recv()
        for cp in sends:
            cp.wait_send()

    dma = pltpu.SemaphoreType.DMA
    return _Exchange(gathered, [_sds(a.shape, a.dtype) for a in gathered], [dma((n, 3)), dma((n, 3))], start, finish,
                     aliases={a: a for a in range(n)})


def _pair_exchange(arrs):
    n = len(arrs)

    def copies(ins, outs, sems):
        x, y, c, _ = _place()
        return [_remote(ins[a].at[:, 1 - c], outs[a], sems[0].at[a], sems[1].at[a], (x, y, 1 - c)) for a in range(n)]

    def start(ins, outs, sems):
        for cp in copies(ins, outs, sems):
            cp.start()

    def finish(ins, outs, sems):
        for cp in copies(ins, outs, sems):
            cp.wait()

    dma = pltpu.SemaphoreType.DMA
    return _Exchange(arrs, [_sds((4,) + a.shape[2:], a.dtype) for a in arrs], [dma((n,)), dma((n,))], start, finish)


def _chip_exchange(arrs):
    n = len(arrs)

    def copies(ins, outs, sems):
        x, y, c, chips = _place()
        return [_remote(ins[a].at[1 + j], outs[a].at[j], sems[0].at[a, j], sems[1].at[a, j], (*chip, c))
                for a in range(n) for j, chip in enumerate(chips)]

    def start(ins, outs, sems):
        for cp in copies(ins, outs, sems):
            cp.start()

    def finish(ins, outs, sems):
        for cp in copies(ins, outs, sems):
            cp.wait()

    dma = pltpu.SemaphoreType.DMA
    return _Exchange(arrs, [_sds((3,) + a.shape[1:], a.dtype) for a in arrs], [dma((n, 3)), dma((n, 3))], start, finish)


def _gather_rows(vec, name, comm=()):
    def body(in_ref, out_ref, send_sems, recv_sems, local_sem):
        x, y, c, _ = _place()
        me = 4 * x + 2 * y + c
        mine = pltpu.make_async_copy(in_ref, out_ref.at[me], local_sem)
        mine.start()
        copies = []
        for k in range(1, N_DEV):
            to = (1 - x if k & 4 else x, 1 - y if k & 2 else y, 1 - c if k & 1 else c)
            copies.append(pltpu.make_async_remote_copy(
                src_ref=in_ref, dst_ref=out_ref.at[me], send_sem=send_sems.at[k - 1], recv_sem=recv_sems.at[k - 1],
                device_id=to, device_id_type=MESH))
        for cp in copies:
            cp.start()
        for cp in copies:
            cp.wait()
        mine.wait()

    return _call(
        body, name=name, out_shape=_sds((N_DEV,) + vec.shape, vec.dtype), in_specs=[ANY], out_specs=ANY,
        scratch_shapes=[pltpu.SemaphoreType.DMA((7,)), pltpu.SemaphoreType.DMA((7,)), pltpu.SemaphoreType.DMA],
    )(vec)


def _for_row_groups(tm, fn):
    def step(i, carry):
        fn(pl.ds(pl.multiple_of(i * SUB_ROWS, SUB_ROWS), SUB_ROWS), i * SUB_ROWS)
        return carry

    lax.fori_loop(0, tm // SUB_ROWS, step, 0)


def _once(shape, index_map):
    return pl.BlockSpec(shape, index_map, pipeline_mode=pl.Buffered(1))


def _rms_fwd(hs, g, tm, name, comm=()):
    rows, d = hs.shape

    def body(hs_ref, g_ref, h_ref, r_ref):
        def group(rs, _):
            xv = hs_ref[rs, :]
            r = lax.rsqrt(jnp.mean(xv * xv, axis=-1, keepdims=True) + EPS)
            h_ref[rs, :] = (xv * r * g_ref[...]).astype(BF16)
            r_ref[rs, :] = r

        _for_row_groups(tm, group)

    return _call(
        body, comm=comm, name=name, grid=(rows // tm,),
        in_specs=[pl.BlockSpec((tm, d), lambda i: (i, 0)), pl.BlockSpec((1, d), lambda i: (0, 0))],
        out_specs=[pl.BlockSpec((tm, d), lambda i: (i, 0)), pl.BlockSpec((tm, 1), lambda i: (i, 0))],
        out_shape=[_sds((rows, d), BF16), _sds((rows, 1), F32)], compiler_params=_params(),
    )(hs, g)


def _ffn_gu(h, wg, wu, tm, name, comm=()):
    rows, d = h.shape
    nb, _, fb = wg.shape

    def body(h_ref, wg_ref, wu_ref, g_ref, u_ref):
        hv = h_ref[...]
        g_ref[0] = jnp.dot(hv, wg_ref[0], preferred_element_type=F32).astype(BF16)
        u_ref[0] = jnp.dot(hv, wu_ref[0], preferred_element_type=F32).astype(BF16)

    wspec = pl.BlockSpec((1, d, fb), lambda j, i: (j, 0, 0))
    ospec = pl.BlockSpec((1, tm, fb), lambda j, i: (j, i, 0))
    return _call(
        body, comm=comm, name=name, grid=(nb, rows // tm),
        in_specs=[pl.BlockSpec((tm, d), lambda j, i: (i, 0)), wspec, wspec],
        out_specs=[ospec, ospec], out_shape=[_sds((nb, rows, fb), BF16)] * 2, compiler_params=_params(),
    )(h, wg, wu)


def _ffn_down_norm(g, u, wd, hs, gn, tm, name, comm=()):
    nb, rows, fb = g.shape
    d = hs.shape[1]

    def body(g_ref, u_ref, wd_ref, hs_ref, gn_ref, hsn_ref, hn_ref, rn_ref, acc_ref):
        j = pl.program_id(1)

        @pl.when(j == 0)
        def _():
            acc_ref[...] = jnp.zeros_like(acc_ref)

        gv = g_ref[0].astype(F32)
        a = (gv * _sigmoid(gv) * u_ref[0].astype(F32)).astype(BF16)
        acc_ref[...] += jnp.dot(a, wd_ref[0], preferred_element_type=F32)

        @pl.when(j == nb - 1)
        def _():
            def group(rs, _):
                hsn = hs_ref[rs, :] + FFN_RES_SCALE * acc_ref[rs, :]
                r = lax.rsqrt(jnp.mean(hsn * hsn, axis=-1, keepdims=True) + EPS)
                hsn_ref[rs, :] = hsn
                hn_ref[rs, :] = (hsn * r * gn_ref[...]).astype(BF16)
                rn_ref[rs, :] = r

            _for_row_groups(tm, group)

    aspec = pl.BlockSpec((1, tm, fb), lambda i, j: (j, i, 0))
    row = pl.BlockSpec((tm, d), lambda i, j: (i, 0))
    return _call(
        body, comm=comm, name=name, grid=(rows // tm, nb),
        in_specs=[aspec, aspec, pl.BlockSpec((1, fb, d), lambda i, j: (j, 0, 0)), _once((tm, d), lambda i, j: (i, 0)),
                  pl.BlockSpec((1, d), lambda i, j: (0, 0))],
        out_specs=[row, row, pl.BlockSpec((tm, 1), lambda i, j: (i, 0))],
        out_shape=[_sds((rows, d), F32), _sds((rows, d), BF16), _sds((rows, 1), F32)],
        scratch_shapes=[pltpu.VMEM((tm, d), F32)], compiler_params=_params(),
    )(g, u, wd, hs, gn)


def _ffn_down_loss(g, u, wd, hs, gf, tgt, n_seq, tm, name, comm=()):
    nb, rows, fb = g.shape
    d = hs.shape[1]
    nt = rows // tm

    def body(g_ref, u_ref, wd_ref, hs_ref, gf_ref, tgt_ref, dhs_ref, dhsb_ref, loss_ref, dgf_ref, acc_ref):
        i, j = pl.program_id(0), pl.program_id(1)

        @pl.when(j == 0)
        def _():
            acc_ref[...] = jnp.zeros_like(acc_ref)

        gv = g_ref[0].astype(F32)
        a = (gv * _sigmoid(gv) * u_ref[0].astype(F32)).astype(BF16)
        acc_ref[...] += jnp.dot(a, wd_ref[0], preferred_element_type=F32)

        @pl.when(j == nb - 1)
        def _():
            loss_ref[0] = jnp.zeros((1, d), F32)
            dgf_ref[0] = jnp.zeros((1, d), F32)

            def group(rs, r0):
                hs3 = hs_ref[rs, :] + FFN_RES_SCALE * acc_ref[rs, :]
                r = lax.rsqrt(jnp.mean(hs3 * hs3, axis=-1, keepdims=True) + EPS)
                gfv = gf_ref[...]
                y = hs3 * r
                rowid = i * tm + r0 + lax.broadcasted_iota(jnp.int32, (SUB_ROWS, 1), 0)
                err = jnp.where(rowid < n_seq, y * gfv - tgt_ref[rs, :], 0.0)
                loss_ref[0] += jnp.sum(err * err, axis=0, keepdims=True)
                dout = err * (1.0 / d)
                dgf_ref[0] += jnp.sum(dout * y, axis=0, keepdims=True)
                t = dout * gfv
                dhs = r * t - hs3 * (r * r * r) * jnp.mean(t * hs3, axis=-1, keepdims=True)
                dhs_ref[rs, :] = dhs
                dhsb_ref[rs, :] = dhs.astype(BF16)

            _for_row_groups(tm, group)

    aspec = pl.BlockSpec((1, tm, fb), lambda i, j: (j, i, 0))
    row = pl.BlockSpec((tm, d), lambda i, j: (i, 0))
    once = _once((tm, d), lambda i, j: (i, 0))
    part = pl.BlockSpec((1, 1, d), lambda i, j: (i, 0, 0))
    return _call(
        body, comm=comm, name=name, grid=(nt, nb),
        in_specs=[aspec, aspec, pl.BlockSpec((1, fb, d), lambda i, j: (j, 0, 0)), once,
                  pl.BlockSpec((1, d), lambda i, j: (0, 0)), once],
        out_specs=[row, row, part, part],
        out_shape=[_sds((rows, d), F32), _sds((rows, d), BF16), _sds((nt, 1, d), F32), _sds((nt, 1, d), F32)],
        scratch_shapes=[pltpu.VMEM((tm, d), F32)], compiler_params=_params(),
    )(g, u, wd, hs, gf, tgt)


def _ffn_bwd_da(do, wd, g, u, tm, name, comm=()):
    nb, rows, fb = g.shape
    d = do.shape[1]

    def body(do_ref, wd_ref, g_ref, u_ref, dg_ref, du_ref, a_ref):
        da = FFN_RES_SCALE * lax.dot_general(do_ref[...], wd_ref[0], NT, preferred_element_type=F32)
        gv = g_ref[0].astype(F32)
        uv = u_ref[0].astype(F32)
        s = _sigmoid(gv)
        sg = gv * s
        a_ref[0] = (sg * uv).astype(BF16)
        du_ref[0] = (da * sg).astype(BF16)
        dg_ref[0] = (da * uv * (s * (1.0 + gv * (1.0 - s)))).astype(BF16)

    aspec = pl.BlockSpec((1, tm, fb), lambda j, i: (j, i, 0))
    return _call(
        body, comm=comm, name=name, grid=(nb, rows // tm),
        in_specs=[pl.BlockSpec((tm, d), lambda j, i: (i, 0)), pl.BlockSpec((1, fb, d), lambda j, i: (j, 0, 0)), aspec, aspec],
        out_specs=[aspec] * 3, out_shape=[_sds((nb, rows, fb), BF16)] * 3, compiler_params=_params(),
    )(do, wd, g, u)


def _rms_bwd(tm, d, dh_ref, hs_ref, r_ref, gn_ref, dres_ref, dhs_ref, dhsb_ref, dgn_ref):
    dgn_ref[0] = jnp.zeros((1, d), F32)

    def group(rs, _):
        dh, hs, r = dh_ref[rs, :], hs_ref[rs, :], r_ref[rs, :]
        dgn_ref[0] += jnp.sum(dh * (hs * r), axis=0, keepdims=True)
        t = dh * gn_ref[...]
        dhs = dres_ref[rs, :] + r * t - hs * (r * r * r) * jnp.mean(t * hs, axis=-1, keepdims=True)
        dhs_ref[rs, :] = dhs
        dhsb_ref[rs, :] = dhs.astype(BF16)

    _for_row_groups(tm, group)


def _ffn_bwd_dh(dg, du, wg, wu, hs, r, gn, dres, tm, name, comm=()):
    nb, rows, fb = dg.shape
    d = hs.shape[1]
    nt = rows // tm

    def body(dg_ref, du_ref, wg_ref, wu_ref, hs_ref, r_ref, gn_ref, dres_ref, dhs_ref, dhsb_ref, dgn_ref, acc_ref):
        j = pl.program_id(1)

        @pl.when(j == 0)
        def _():
            acc_ref[...] = jnp.zeros_like(acc_ref)

        acc_ref[...] += (lax.dot_general(dg_ref[0], wg_ref[0], NT, preferred_element_type=F32)
                         + lax.dot_general(du_ref[0], wu_ref[0], NT, preferred_element_type=F32))

        @pl.when(j == nb - 1)
        def _():
            _rms_bwd(tm, d, acc_ref, hs_ref, r_ref, gn_ref, dres_ref, dhs_ref, dhsb_ref, dgn_ref)

    aspec = pl.BlockSpec((1, tm, fb), lambda i, j: (j, i, 0))
    wspec = pl.BlockSpec((1, d, fb), lambda i, j: (j, 0, 0))
    row = pl.BlockSpec((tm, d), lambda i, j: (i, 0))
    once = _once((tm, d), lambda i, j: (i, 0))
    return _call(
        body, comm=comm, name=name, grid=(nt, nb),
        in_specs=[aspec, aspec, wspec, wspec, once, pl.BlockSpec((tm, 1), lambda i, j: (i, 0)),
                  pl.BlockSpec((1, d), lambda i, j: (0, 0)), once],
        out_specs=[row, row, pl.BlockSpec((1, 1, d), lambda i, j: (i, 0, 0))],
        out_shape=[_sds((rows, d), F32), _sds((rows, d), BF16), _sds((nt, 1, d), F32)],
        scratch_shapes=[pltpu.VMEM((tm, d), F32)], compiler_params=_params(),
    )(dg, du, wg, wu, hs, r, gn, dres)


def _ffn_bwd_dw(h, dg, du, a, do, tk, name, comm=()):
    nb, rows, fb = dg.shape
    d = h.shape[1]
    nk = rows // tk

    def body(h_ref, dg_ref, du_ref, a_ref, do_ref, dwg_ref, dwu_ref, dwd_ref, accg, accu, accd):
        k = pl.program_id(1)

        @pl.when(k == 0)
        def _():
            accg[...] = jnp.zeros_like(accg)
            accu[...] = jnp.zeros_like(accu)
            accd[...] = jnp.zeros_like(accd)

        hv = h_ref[...]
        accg[...] += lax.dot_general(hv, dg_ref[0], TN, preferred_element_type=F32)
        accu[...] += lax.dot_general(hv, du_ref[0], TN, preferred_element_type=F32)
        accd[...] += lax.dot_general(a_ref[0], do_ref[...], TN, preferred_element_type=F32)

        @pl.when(k == nk - 1)
        def _():
            dwg_ref[0] = accg[...].astype(BF16)
            dwu_ref[0] = accu[...].astype(BF16)
            dwd_ref[0] = (FFN_RES_SCALE * accd[...]).astype(BF16)

    aspec = pl.BlockSpec((1, tk, fb), lambda j, k: (j, k, 0))
    row = pl.BlockSpec((tk, d), lambda j, k: (k, 0))
    up = pl.BlockSpec((1, d, fb), lambda j, k: (j, 0, 0))
    down = pl.BlockSpec((1, fb, d), lambda j, k: (j, 0, 0))
    return _call(
        body, comm=comm, name=name, grid=(nb, nk),
        in_specs=[row, aspec, aspec, aspec, row], out_specs=[up, up, down],
        out_shape=[_sds((nb, d, fb), BF16), _sds((nb, d, fb), BF16), _sds((nb, fb, d), BF16)],
        scratch_shapes=[pltpu.VMEM((d, fb), F32), pltpu.VMEM((d, fb), F32), pltpu.VMEM((fb, d), F32)],
        compiler_params=_params(),
    )(h, dg, du, a, do)


def _win_fwd(h, w, b, dc, tm, name, comm=()):
    rows, d = h.shape
    ng = w.shape[1] // dc

    def body(h_ref, w_ref, b_ref, u_ref):
        u_ref[...] = (jnp.dot(h_ref[...], w_ref[...], preferred_element_type=F32) + b_ref[...]).astype(BF16)

    return _call(
        body, comm=comm, name=name, grid=(ng, rows // tm),
        in_specs=[pl.BlockSpec((tm, d), lambda m, i: (i, 0)), pl.BlockSpec((d, dc), lambda m, i: (0, m)),
                  pl.BlockSpec((1, dc), lambda m, i: (0, m))],
        out_specs=pl.BlockSpec((tm, dc), lambda m, i: (i, m)), out_shape=_sds((rows, ng * dc), BF16),
        compiler_params=_params(),
    )(h, w, b)


def _win_bwd_dh(du, w, hs, r, gn, dres, tm, name, comm=()):
    rows, d = hs.shape
    ng, _, dc = du.shape
    nt = rows // tm

    def body(du_ref, w_ref, hs_ref, r_ref, gn_ref, dres_ref, dhs_ref, dhsb_ref, dgn_ref, acc_ref):
        m = pl.program_id(1)

        @pl.when(m == 0)
        def _():
            acc_ref[...] = jnp.zeros_like(acc_ref)

        acc_ref[...] += lax.dot_general(du_ref[0], w_ref[...], NT, preferred_element_type=F32)

        @pl.when(m == ng - 1)
        def _():
            _rms_bwd(tm, d, acc_ref, hs_ref, r_ref, gn_ref, dres_ref, dhs_ref, dhsb_ref, dgn_ref)

    row = pl.BlockSpec((tm, d), lambda i, m: (i, 0))
    once = _once((tm, d), lambda i, m: (i, 0))
    return _call(
        body, comm=comm, name=name, grid=(nt, ng),
        in_specs=[pl.BlockSpec((1, tm, dc), lambda i, m: (m, i, 0)), pl.BlockSpec((d, dc), lambda i, m: (0, m)), once,
                  pl.BlockSpec((tm, 1), lambda i, m: (i, 0)), pl.BlockSpec((1, d), lambda i, m: (0, 0)), once],
        out_specs=[row, row, pl.BlockSpec((1, 1, d), lambda i, m: (i, 0, 0))],
        out_shape=[_sds((rows, d), F32), _sds((rows, d), BF16), _sds((nt, 1, d), F32)],
        scratch_shapes=[pltpu.VMEM((tm, d), F32)], compiler_params=_params(),
    )(du, w, hs, r, gn, dres)


def _win_bwd_dw(h, du, tk, name, comm=()):
    rows, d = h.shape
    ng, _, dc = du.shape
    nk = rows // tk

    def body(h_ref, du_ref, dw_ref, db_ref, acc, accb):
        k = pl.program_id(1)

        @pl.when(k == 0)
        def _():
            acc[...] = jnp.zeros_like(acc)
            accb[...] = jnp.zeros_like(accb)

        duv = du_ref[0]
        acc[...] += lax.dot_general(h_ref[...], duv, TN, preferred_element_type=F32)
        accb[...] += jnp.sum(duv.astype(F32), axis=0, keepdims=True)

        @pl.when(k == nk - 1)
        def _():
            dw_ref[...] = acc[...].astype(BF16)
            db_ref[...] = accb[...]

    return _call(
        body, comm=comm, name=name, grid=(ng, nk),
        in_specs=[pl.BlockSpec((tk, d), lambda m, k: (k, 0)), pl.BlockSpec((1, tk, dc), lambda m, k: (m, k, 0))],
        out_specs=[pl.BlockSpec((d, dc), lambda m, k: (0, m)), pl.BlockSpec((1, dc), lambda m, k: (0, m))],
        out_shape=[_sds((d, ng * dc), BF16), _sds((1, ng * dc), F32)],
        scratch_shapes=[pltpu.VMEM((d, dc), F32), pltpu.VMEM((1, dc), F32)], compiler_params=_params(),
    )(h, du)


def _layernorm_silu(zc, lg, lb):
    mu = jnp.mean(zc, axis=-1, keepdims=True)
    xc = zc - mu
    rstd = lax.rsqrt(jnp.mean(xc * xc, axis=-1, keepdims=True) + EPS)
    nrm = xc * rstd
    lin = nrm * lg + lb
    s = _sigmoid(lin)
    return nrm, rstd, lin, s


def _wout_fwd(zc, ysc, wout, hs, lg, lb, gn, tm, name, comm=()):
    rows, dc = zc.shape
    d = hs.shape[1]

    def body(zc_ref, ysc_ref, w_ref, hs_ref, lg_ref, lb_ref, gn_ref, y_ref, hsn_ref, hn_ref, rn_ref):
        def mix(rs, _):
            _, _, lin, s = _layernorm_silu(zc_ref[rs, :], lg_ref[...], lb_ref[...])
            y_ref[rs, :dc] = ysc_ref[rs, :]
            y_ref[rs, dc:] = (lin * s).astype(BF16)

        _for_row_groups(tm, mix)
        hsn_ref[...] = jnp.dot(y_ref[...], w_ref[...], preferred_element_type=F32)

        def norm(rs, _):
            hsn = hs_ref[rs, :] + hsn_ref[rs, :]
            r = lax.rsqrt(jnp.mean(hsn * hsn, axis=-1, keepdims=True) + EPS)
            hsn_ref[rs, :] = hsn
            hn_ref[rs, :] = (hsn * r * gn_ref[...]).astype(BF16)
            rn_ref[rs, :] = r

        _for_row_groups(tm, norm)

    half = pl.BlockSpec((tm, dc), lambda i: (i, 0))
    row = pl.BlockSpec((tm, d), lambda i: (i, 0))
    vec_c = pl.BlockSpec((1, dc), lambda i: (0, 0))
    return _call(
        body, comm=comm, name=name, grid=(rows // tm,),
        in_specs=[half, half, _once((2 * dc, d), lambda i: (0, 0)), row, vec_c, vec_c,
                  pl.BlockSpec((1, d), lambda i: (0, 0))],
        out_specs=[pl.BlockSpec((tm, 2 * dc), lambda i: (i, 0)), row, row, pl.BlockSpec((tm, 1), lambda i: (i, 0))],
        out_shape=[_sds((rows, 2 * dc), BF16), _sds((rows, d), F32), _sds((rows, d), BF16), _sds((rows, 1), F32)],
        compiler_params=_params(),
    )(zc, ysc, wout, hs, lg, lb, gn)


def _wout_bwd_dy(do, wout, zc, lg, lb, tm, name, comm=()):
    rows, dc = zc.shape
    d = do.shape[1]
    nt = rows // tm

    def body(do_ref, w_ref, zc_ref, lg_ref, lb_ref, dysc_ref, dzc_ref, dlg_ref, dlb_ref, dy_ref):
        dy_ref[...] = lax.dot_general(do_ref[...], w_ref[...], NT, preferred_element_type=F32)
        dlb_ref[0] = jnp.zeros((1, dc), F32)
        dlg_ref[0] = jnp.zeros((1, dc), F32)

        def group(rs, _):
            dysc_ref[rs, :] = dy_ref[rs, :dc].astype(BF16)
            nrm, rstd, lin, s = _layernorm_silu(zc_ref[rs, :], lg_ref[...], lb_ref[...])
            dl = dy_ref[rs, dc:] * (s * (1.0 + lin * (1.0 - s)))
            dlb_ref[0] += jnp.sum(dl, axis=0, keepdims=True)
            dlg_ref[0] += jnp.sum(dl * nrm, axis=0, keepdims=True)
            dn = dl * lg_ref[...]
            dzc_ref[rs, :] = rstd * (dn - jnp.mean(dn, axis=-1, keepdims=True)
                                     - nrm * jnp.mean(dn * nrm, axis=-1, keepdims=True))

        _for_row_groups(tm, group)

    half = pl.BlockSpec((tm, dc), lambda i: (i, 0))
    vec_c = pl.BlockSpec((1, dc), lambda i: (0, 0))
    part = pl.BlockSpec((1, 1, dc), lambda i: (i, 0, 0))
    return _call(
        body, comm=comm, name=name, grid=(nt,),
        in_specs=[pl.BlockSpec((tm, d), lambda i: (i, 0)), _once((2 * dc, d), lambda i: (0, 0)), half, vec_c, vec_c],
        out_specs=[half, half, part, part],
        out_shape=[_sds((rows, dc), BF16), _sds((rows, dc), F32), _sds((nt, 1, dc), F32), _sds((nt, 1, dc), F32)],
        scratch_shapes=[pltpu.VMEM((tm, 2 * dc), F32)], compiler_params=_params(),
    )(do, wout, zc, lg, lb)


def _wout_bwd_dw(y, do, nb, tk, name, comm=()):
    rows, k2 = y.shape
    d = do.shape[1]
    ob = k2 // nb
    nk = rows // tk

    def body(y_ref, do_ref, dw_ref, acc):
        k = pl.program_id(1)

        @pl.when(k == 0)
        def _():
            acc[...] = jnp.zeros_like(acc)

        acc[...] += lax.dot_general(y_ref[...], do_ref[...], TN, preferred_element_type=F32)

        @pl.when(k == nk - 1)
        def _():
            dw_ref[0] = acc[...].astype(BF16)

    return _call(
        body, comm=comm, name=name, grid=(nb, nk),
        in_specs=[pl.BlockSpec((tk, ob), lambda j, k: (k, j)), pl.BlockSpec((tk, d), lambda j, k: (k, 0))],
        out_specs=pl.BlockSpec((1, ob, d), lambda j, k: (j, 0, 0)), out_shape=_sds((nb, ob, d), BF16),
        scratch_shapes=[pltpu.VMEM((ob, d), F32)], compiler_params=_params(),
    )(y, do)


def _windows(win, n_res):
    length = win.shape[0]
    return [win if r == 0 else pltpu.roll(win, length - r, 0) for r in range(n_res)]


def _taps(src_ref, start, offsets, ch):
    span = -(-(max(offsets) + ch) // SUBLANES) * SUBLANES
    win = src_ref[pl.ds(start, span), :]
    shifted = _windows(win, min(SUBLANES, max(offsets) + 1))
    return [shifted[o % SUBLANES][(o // SUBLANES) * SUBLANES:(o // SUBLANES) * SUBLANES + ch] for o in offsets]


def _rows8(v):
    return jnp.sum(v.reshape(v.shape[0] // SUBLANES, SUBLANES, v.shape[1]), axis=0)


def _conv_geometry(n_seq, n_meta):
    base = CONV_PAD + n_meta
    off_cf = base - (CF_WIDTH - 1)
    off_sc = base - (SC_WIDTH - 1)
    logical = -(-(n_meta + n_seq) // CONV_CH) * CONV_CH
    return base, off_cf, off_sc, logical


def _fill_conv_inputs(c_ref, v_ref, a_ref, g_ref, scv, sz, n_seq, n_meta):
    base = CONV_PAD + n_meta
    cb = scv.shape[1]
    scv[0:CONV_PAD, :] = jnp.zeros((CONV_PAD, cb), F32)
    sz[0:CONV_PAD, :] = jnp.zeros((CONV_PAD, cb), F32)

    def put(src, dst, n):
        cv = c_ref[src, :].astype(F32) * v_ref[src, :].astype(F32)
        scv[dst, :] = cv
        sz[dst, :] = a_ref[src, :].astype(F32) * _sigmoid(g_ref[src, :].astype(F32))

    put(pl.ds(n_seq, n_meta), pl.ds(CONV_PAD, n_meta), n_meta)

    def chunk(i, carry):
        t0 = pl.multiple_of(i * CONV_CH, CONV_CH)
        put(pl.ds(t0, CONV_CH), pl.ds(base + t0, CONV_CH), CONV_CH)
        return carry

    lax.fori_loop(0, n_seq // CONV_CH, chunk, 0)


def _conv_fwd(u, wsc, wcf, cbias, n_seq, n_meta, name, comm=()):
    rows = u.shape[0]
    nb, _, cb = wsc.shape
    dc = nb * cb
    base, off_cf, off_sc, _ = _conv_geometry(n_seq, n_meta)
    a_cf, a_sc = off_cf // SUBLANES * SUBLANES, off_sc // SUBLANES * SUBLANES
    ch = CONV_CH

    def body(b_ref, c_ref, v_ref, a_ref, g_ref, wsc_ref, wcf_ref, cb_ref, ysc_ref, zc_ref, scv, sz):
        _fill_conv_inputs(c_ref, v_ref, a_ref, g_ref, scv, sz, n_seq, n_meta)
        w3, w31, bias = wsc_ref[0], wcf_ref[0], cb_ref[...]

        def chunk(i, carry):
            t0 = pl.multiple_of(i * ch, ch)
            acc = jnp.zeros((ch, cb), F32)
            for k, win in enumerate(_taps(sz, t0 + a_cf, [off_cf - a_cf + k for k in range(CF_WIDTH)], ch)):
                acc = acc + win * w31[k:k + 1, :]
            zc_ref[pl.ds(t0, ch), :] = acc + bias
            s = jnp.zeros((ch, cb), F32)
            for k, win in enumerate(_taps(scv, t0 + a_sc, [off_sc - a_sc + k for k in range(SC_WIDTH)], ch)):
                s = s + win * w3[k:k + 1, :]
            ysc_ref[pl.ds(t0, ch), :] = (b_ref[pl.ds(t0, ch), :].astype(F32) * s).astype(BF16)
            return carry

        lax.fori_loop(0, n_seq // ch, chunk, 0)
        ysc_ref[n_seq:rows, :] = jnp.zeros((rows - n_seq, cb), BF16)
        zc_ref[n_seq:rows, :] = jnp.zeros((rows - n_seq, cb), F32)

    ucol = [pl.BlockSpec((rows, cb), functools.partial(lambda m, j: (0, m * nb + j), m)) for m in range(5)]
    blk = pl.BlockSpec((rows, cb), lambda j: (0, j))
    return _call(
        body, comm=comm, name=name, grid=(nb,),
        in_specs=ucol + [pl.BlockSpec((1, SC_WIDTH, cb), lambda j: (j, 0, 0)),
                         pl.BlockSpec((1, CF_WIDTH, cb), lambda j: (j, 0, 0)), pl.BlockSpec((1, cb), lambda j: (0, j))],
        out_specs=[blk, blk], out_shape=[_sds((rows, dc), BF16), _sds((rows, dc), F32)],
        scratch_shapes=[pltpu.VMEM((base + n_seq, cb), F32)] * 2, compiler_params=_params(),
    )(u, u, u, u, u, wsc, wcf, cbias)


def _conv_bwd(u, dysc, dzc, wsc, wcf, n_seq, n_meta, name, comm=()):
    rows = u.shape[0]
    nb, _, cb = wsc.shape
    dc = nb * cb
    base, off_cf, off_sc, logical = _conv_geometry(n_seq, n_meta)
    a_cf, a_sc = off_cf // SUBLANES * SUBLANES, off_sc // SUBLANES * SUBLANES
    ch = CONV_CH
    tail = CONV_PAD

    def body(b_ref, c_ref, v_ref, a_ref, g_ref, dysc_ref, dzc_ref, wsc_ref, wcf_ref,
             du_ref, dwsc_ref, dwcf_ref, dcb_ref,
             scv, sz, sds_, sdz, dlcv, dlz, accsc, acccf, accb):
        _fill_conv_inputs(c_ref, v_ref, a_ref, g_ref, scv, sz, n_seq, n_meta)
        w3, w31 = wsc_ref[0], wcf_ref[0]
        dub_ref, duc_ref, duv_ref, dua_ref, dug_ref = (du_ref.at[m] for m in range(5))
        sds_[0:n_meta, :] = jnp.zeros((n_meta, cb), F32)
        sdz[0:n_meta, :] = jnp.zeros((n_meta, cb), F32)
        behind = logical + tail - (n_meta + n_seq)
        sds_[n_meta + n_seq:logical + tail, :] = jnp.zeros((behind, cb), F32)
        sdz[n_meta + n_seq:logical + tail, :] = jnp.zeros((behind, cb), F32)
        accsc[...] = jnp.zeros_like(accsc)
        acccf[...] = jnp.zeros_like(acccf)
        accb[...] = jnp.zeros_like(accb)

        def forward_chunk(i, carry):
            t0 = pl.multiple_of(i * ch, ch)
            rws = pl.ds(t0, ch)
            dy = dysc_ref[rws, :].astype(F32)
            ds = dy * b_ref[rws, :].astype(F32)
            dz = dzc_ref[rws, :]
            sds_[pl.ds(n_meta + t0, ch), :] = ds
            sdz[pl.ds(n_meta + t0, ch), :] = dz
            s = jnp.zeros((ch, cb), F32)
            for k, win in enumerate(_taps(scv, t0 + a_sc, [off_sc - a_sc + k for k in range(SC_WIDTH)], ch)):
                s = s + win * w3[k:k + 1, :]
                accsc[k * SUBLANES:(k + 1) * SUBLANES, :] += _rows8(ds * win)
            dub_ref[rws, :] = (dy * s).astype(BF16)
            for k, win in enumerate(_taps(sz, t0 + a_cf, [off_cf - a_cf + k for k in range(CF_WIDTH)], ch)):
                acccf[k * SUBLANES:(k + 1) * SUBLANES, :] += _rows8(dz * win)
            accb[...] += _rows8(dz)
            return carry

        lax.fori_loop(0, n_seq // ch, forward_chunk, 0)

        def backward_chunk(i, carry):
            p0 = pl.multiple_of(i * ch, ch)
            dcv = jnp.zeros((ch, cb), F32)
            for k, win in enumerate(_taps(sds_, p0, [SC_WIDTH - 1 - k for k in range(SC_WIDTH)], ch)):
                dcv = dcv + win * w3[k:k + 1, :]
            dlcv[pl.ds(p0, ch), :] = dcv
            dzi = jnp.zeros((ch, cb), F32)
            for k, win in enumerate(_taps(sdz, p0, [CF_WIDTH - 1 - k for k in range(CF_WIDTH)], ch)):
                dzi = dzi + win * w31[k:k + 1, :]
            dlz[pl.ds(p0, ch), :] = dzi
            return carry

        lax.fori_loop(0, logical // ch, backward_chunk, 0)

        def gates(phys, logi):
            dcv, dzi = dlcv[logi, :], dlz[logi, :]
            duc_ref[phys, :] = (dcv * v_ref[phys, :].astype(F32)).astype(BF16)
            duv_ref[phys, :] = (dcv * c_ref[phys, :].astype(F32)).astype(BF16)
            s = _sigmoid(g_ref[phys, :].astype(F32))
            dua_ref[phys, :] = (dzi * s).astype(BF16)
            dug_ref[phys, :] = (dzi * a_ref[phys, :].astype(F32) * s * (1.0 - s)).astype(BF16)

        def gate_chunk(i, carry):
            t0 = pl.multiple_of(i * ch, ch)
            gates(pl.ds(t0, ch), pl.ds(n_meta + t0, ch))
            return carry

        lax.fori_loop(0, n_seq // ch, gate_chunk, 0)
        gates(pl.ds(n_seq, n_meta), pl.ds(0, n_meta))
        dub_ref[n_seq:rows, :] = jnp.zeros((rows - n_seq, cb), BF16)
        pad0 = n_seq + n_meta
        for ref in (duc_ref, duv_ref, dua_ref, dug_ref):
            ref[pad0:rows, :] = jnp.zeros((rows - pad0, cb), BF16)
        dwsc_ref[0] = jnp.sum(accsc[...].reshape(SC_WIDTH, SUBLANES, cb), axis=1)
        dwcf_ref[0] = jnp.sum(acccf[...].reshape(CF_WIDTH, SUBLANES, cb), axis=1)
        dcb_ref[...] = jnp.sum(accb[...], axis=0, keepdims=True)

    ucol = [pl.BlockSpec((rows, cb), functools.partial(lambda m, j: (0, m * nb + j), m)) for m in range(5)]
    blk = pl.BlockSpec((rows, cb), lambda j: (0, j))
    wsc_spec = pl.BlockSpec((1, SC_WIDTH, cb), lambda j: (j, 0, 0))
    wcf_spec = pl.BlockSpec((1, CF_WIDTH, cb), lambda j: (j, 0, 0))
    outs = _call(
        body, comm=comm, name=name, grid=(nb,),
        in_specs=ucol + [blk, blk, wsc_spec, wcf_spec],
        out_specs=[pl.BlockSpec((5, rows, cb), lambda j: (0, 0, j)), wsc_spec, wcf_spec,
                   pl.BlockSpec((1, cb), lambda j: (0, j))],
        out_shape=[_sds((5, rows, dc), BF16), _sds((nb, SC_WIDTH, cb), F32), _sds((nb, CF_WIDTH, cb), F32),
                   _sds((1, dc), F32)],
        scratch_shapes=[pltpu.VMEM((base + n_seq, cb), F32)] * 2 + [pltpu.VMEM((logical + tail, cb), F32)] * 2
        + [pltpu.VMEM((logical, cb), F32)] * 2
        + [pltpu.VMEM((SC_WIDTH * SUBLANES, cb), F32), pltpu.VMEM((CF_WIDTH * SUBLANES, cb), F32),
           pltpu.VMEM((SUBLANES, cb), F32)],
        compiler_params=_params(),
    )(u, u, u, u, u, dysc, dzc, wsc, wcf)
    return outs


def _row_tile(rows, cols):
    return rows // 4 if rows % 64 == 0 and rows * cols >= (1 << 18) else rows


def _pair_sum(grad, sib, idx, name, comm=()):
    _, rows, cols = grad.shape
    tr = _row_tile(rows, cols)

    def body(idx_ref, g_ref, s_ref, o_ref):
        o_ref[0] = (g_ref[0].astype(F32) + s_ref[0].astype(F32)).astype(o_ref.dtype)

    return _call(
        body, name=name,
        grid_spec=pltpu.PrefetchScalarGridSpec(
            num_scalar_prefetch=1, grid=(4, rows // tr),
            in_specs=[pl.BlockSpec((1, tr, cols), lambda k, i, idx_ref: (idx_ref[k], i, 0)),
                      pl.BlockSpec((1, tr, cols), lambda k, i, idx_ref: (idx_ref[4 + k], i, 0))],
            out_specs=pl.BlockSpec((1, tr, cols), lambda k, i, idx_ref: (k, i, 0))),
        out_shape=_sds((4, rows, cols), grad.dtype), compiler_params=_params(),
    )(idx, grad, sib)


def _adamw_math(w, g, m, v):
    m = ADAM_B1 * m + (1.0 - ADAM_B1) * g
    v = ADAM_B2 * v + (1.0 - ADAM_B2) * (g * g)
    m_hat = m / (1.0 - ADAM_B1 ** ADAM_STEP)
    v_hat = v / (1.0 - ADAM_B2 ** ADAM_STEP)
    delta = -ADAM_LR * (m_hat / (jnp.sqrt(v_hat) + ADAM_EPS) + ADAM_WD * w)
    return delta, m, v


def _adamw_sharded(own, got, w, m, v, name, comm=()):
    rows, cols = w.shape
    tr = _row_tile(rows, cols)

    def body(own_ref, g0_ref, g1_ref, g2_ref, w_ref, m_ref, v_ref, g_ref, d_ref, nm_ref, nv_ref):
        g = own_ref[0].astype(F32) + g0_ref[0].astype(F32) + g1_ref[0].astype(F32) + g2_ref[0].astype(F32)
        delta, nm, nv = _adamw_math(w_ref[...], g, m_ref[...], v_ref[...])
        g_ref[...] = g
        d_ref[...] = delta
        nm_ref[...] = nm
        nv_ref[...] = nv

    flat = pl.BlockSpec((tr, cols), lambda i: (i, 0))
    slot = [pl.BlockSpec((1, tr, cols), functools.partial(lambda k, i: (k, i, 0), k)) for k in range(3)]
    return _call(
        body, comm=comm, name=name, grid=(rows // tr,),
        in_specs=[slot[0]] + slot + [flat] * 3, out_specs=[flat] * 4, out_shape=[_sds((rows, cols), F32)] * 4,
        compiler_params=_params(),
    )(own, got, got, got, w, m, v)


def _adamw_replicated(gathered, segs, ws, ms, vs, loss_scale, name, comm=()):
    n = len(ws)

    def body(*refs):
        gat = refs[0]
        w_refs, m_refs, v_refs = refs[1:1 + n], refs[1 + n:1 + 2 * n], refs[1 + 2 * n:1 + 3 * n]
        outs = refs[1 + 3 * n:]

        def total(off, width):
            s = gat[0, :, off:off + width]
            for k in range(1, N_DEV):
                s = s + gat[k, :, off:off + width]
            return s

        outs[0][...] = loss_scale * total(segs[n][0], segs[n][1])
        for p in range(n):
            g = total(*segs[p])
            delta, nm, nv = _adamw_math(w_refs[p][...], g, m_refs[p][...], v_refs[p][...])
            for q, val in enumerate((g, delta, nm, nv)):
                outs[1 + 4 * p + q][...] = val
        for e, seg in enumerate(segs[n + 1:]):
            outs[1 + 4 * n + e][...] = total(*seg)

    return _call(
        body, name=name,
        out_shape=[_sds((1, segs[n][1]), F32)] + [_sds(w.shape, F32) for w in ws for _ in range(4)]
        + [_sds((1, seg[1]), F32) for seg in segs[n + 1:]],
        compiler_params=_params(),
    )(gathered, *ws, *ms, *vs)


def _adamw_plain(g, w, m, v, name):
    def body(g_ref, w_ref, m_ref, v_ref, d_ref, nm_ref, nv_ref):
        d_ref[...], nm_ref[...], nv_ref[...] = _adamw_math(w_ref[...], g_ref[...], m_ref[...], v_ref[...])

    return list(_call(body, name=name, out_shape=[_sds(w.shape, F32)] * 3)(g, w, m, v))


REPLICATED = ("ffn1_norm", "mix_norm", "b_in", "conv_cf_b", "ln_cf_g", "ln_cf_b", "ffn2_norm", "final_norm")
SHARDED = ("meta_tokens", "ffn1_w_gate", "ffn1_w_up", "ffn1_w_down", "w_in", "conv_sc_w", "conv_cf_w", "w_out",
           "ffn2_w_gate", "ffn2_w_up", "ffn2_w_down")
WEIGHTS = ("meta_tokens", "ffn1_norm", "ffn1_w_gate", "ffn1_w_up", "ffn1_w_down", "mix_norm", "w_in", "b_in",
           "conv_sc_w", "conv_cf_w", "conv_cf_b", "ln_cf_g", "ln_cf_b", "w_out", "ffn2_norm", "ffn2_w_gate",
           "ffn2_w_up", "ffn2_w_down", "final_norm")


def _blocks2d(a):
    return a.reshape(a.shape[-2:]) if a.ndim >= 2 else a.reshape(1, -1)


def _step(x, tgt, w, m, v):
    n_seq, d = x.shape[1], x.shape[2]
    n_meta = w["meta_tokens"].shape[0]
    rows = -(-(n_seq + n_meta) // ROW_ALIGN) * ROW_ALIGN
    tm = rows // N_ROW_TILES
    cb = w["conv_sc_w"].shape[-1]
    dc = cb * N_DEV
    w2 = {k: _blocks2d(a) for k, a in w.items()}
    m2 = {k: _blocks2d(a) for k, a in m.items()}
    v2 = {k: _blocks2d(a) for k, a in v.items()}

    def cast(k):
        return w2[k].astype(BF16)

    first = _gather_direct([cast("ffn1_w_gate"), cast("ffn1_w_up"), cast("ffn1_w_down"),
                            w2["meta_tokens"], w2["conv_sc_w"], w2["conv_cf_w"]])
    _exchange_alone(first, "gather_ffn1")
    full = dict(zip(("ffn1_w_gate", "ffn1_w_up", "ffn1_w_down", "meta_tokens", "conv_sc_w", "conv_cf_w"),
                    _exchange_alone(_gather_forward(first.results), "gather_ffn1_pass")))
    meta = jnp.transpose(full["meta_tokens"], (1, 0, 2)).reshape(n_meta, d)

    hs0 = jnp.concatenate([x[0], meta, jnp.zeros((rows - n_seq - n_meta, d), F32)], axis=0)
    tgt_rows = jnp.concatenate([tgt[0], jnp.zeros((rows - n_seq, d), F32)], axis=0)

    h1, r1 = _rms_fwd(hs0, w2["ffn1_norm"], tm, "rms_in")
    mix_w = _gather_direct([cast("w_in"), cast("w_out")])
    g1, u1 = _ffn_gu(h1, full["ffn1_w_gate"], full["ffn1_w_up"], tm, "ffn1_gu", comm=[mix_w])
    mix_w2, gate2 = _gather_forward(mix_w.results), _gather_direct([cast("ffn2_w_gate")])
    hs1, h2, r2 = _ffn_down_norm(g1, u1, full["ffn1_w_down"], hs0, w2["mix_norm"], tm, "ffn1_down", comm=[mix_w2, gate2])
    win = jnp.transpose(mix_w2.results[0], (1, 0, 2)).reshape(d, -1)
    wout = mix_w2.results[1].reshape(-1, d)
    gate2p, up2 = _gather_forward(gate2.results), _gather_direct([cast("ffn2_w_up")])
    u = _win_fwd(h2, win, w2["b_in"], dc, tm, "mix_in", comm=[gate2p, up2])
    up2p = _gather_forward(up2.results)
    ysc, zc = _conv_fwd(u, full["conv_sc_w"], full["conv_cf_w"], w2["conv_cf_b"], n_seq, n_meta, "conv_fwd", comm=[up2p])
    down2 = _gather_direct([cast("ffn2_w_down")])
    y, hs2, h3, r3 = _wout_fwd(zc, ysc, wout, hs1, w2["ln_cf_g"], w2["ln_cf_b"], w2["ffn2_norm"], tm, "mix_out",
                               comm=[down2])
    down2p = _gather_forward(down2.results)
    full["ffn2_w_gate"], full["ffn2_w_up"] = gate2p.results[0], up2p.results[0]
    g2, u2 = _ffn_gu(h3, full["ffn2_w_gate"], full["ffn2_w_up"], tm, "ffn2_gu", comm=[down2p])
    full["ffn2_w_down"] = down2p.results[0]
    dhs3, dhs3b, loss_p, dgf_p = _ffn_down_loss(
        g2, u2, full["ffn2_w_down"], hs2, w2["final_norm"], tgt_rows, n_seq, tm, "ffn2_down_loss")

    dg2, du2, a2 = _ffn_bwd_da(dhs3b, full["ffn2_w_down"], g2, u2, tm, "ffn2_bwd_da")
    dhs2, dhs2b, dn3_p = _ffn_bwd_dh(
        dg2, du2, full["ffn2_w_gate"], full["ffn2_w_up"], hs2, r3, w2["ffn2_norm"], dhs3, tm, "ffn2_bwd_dh")
    xi, yi, ci = lax.axis_index("x"), lax.axis_index("y"), lax.axis_index("c")
    chip_of = [2 * xi + yi, 2 * (1 - xi) + yi, 2 * xi + (1 - yi), 2 * (1 - xi) + (1 - yi)]
    idx = jnp.stack([2 * ch + ci for ch in chip_of] + chip_of).astype(jnp.int32)
    out = {}

    def to_pairs(parts):
        return _pair_exchange([p.reshape((4, 2) + p.shape[1:]) for p in parts])

    def pair_sums(names, parts, pairs):
        return [_pair_sum(p, s, idx, "pair_sum_" + k) for k, p, s in zip(names, parts, pairs.results)]

    def update(names, sums, chips):
        for k, own, got in zip(names, sums, chips.results):
            res = _adamw_sharded(own, got, w2[k], m2[k], v2[k], "adamw_" + k)
            out[k] = [r.reshape(w[k].shape) for r in res]

    ffn2 = ("ffn2_w_gate", "ffn2_w_up", "ffn2_w_down")
    grads2 = _ffn_bwd_dw(h3, dg2, du2, a2, dhs3b, tm, "ffn2_bwd_dw")
    pairs2 = to_pairs(grads2)
    dysc, dzc, dlg_p, dlb_p = _wout_bwd_dy(dhs2b, wout, zc, w2["ln_cf_g"], w2["ln_cf_b"], tm, "mix_out_bwd_dy",
                                           comm=[pairs2])
    sums2 = pair_sums(ffn2, grads2, pairs2)
    dwout = _wout_bwd_dw(y, dhs2b, N_DEV, tm, "mix_out_bwd_dw")
    chips2 = [_chip_exchange([s]) for s in sums2]
    du, dcsw, dccw, dcb = _conv_bwd(u, dysc, dzc, full["conv_sc_w"], full["conv_cf_w"], n_seq, n_meta, "conv_bwd",
                                    comm=[chips2[0]])
    dhs1, dhs1b, dn2_p = _win_bwd_dh(du, win, hs1, r2, w2["mix_norm"], dhs2, tm, "mix_in_bwd_dh", comm=[chips2[1]])
    dwin, dbin = _win_bwd_dw(h2, du, tm, "mix_in_bwd_dw", comm=[chips2[2]])
    for k, s, ch in zip(ffn2, sums2, chips2):
        update([k], [s], ch)

    mixer = ("w_out", "w_in", "conv_sc_w", "conv_cf_w")
    gradsm = [dwout, jnp.transpose(dwin.reshape(d, N_DEV, -1), (1, 0, 2)), dcsw, dccw]
    pairsm = to_pairs(gradsm)
    dg1, du1, a1 = _ffn_bwd_da(dhs1b, full["ffn1_w_down"], g1, u1, tm, "ffn1_bwd_da", comm=[pairsm])
    sumsm = pair_sums(mixer, gradsm, pairsm)
    chipsm = _chip_exchange(sumsm)
    ffn1 = ("ffn1_w_gate", "ffn1_w_up", "ffn1_w_down")
    grads1 = _ffn_bwd_dw(h1, dg1, du1, a1, dhs1b, tm, "ffn1_bwd_dw", comm=[chipsm])
    update(mixer, sumsm, chipsm)
    pairs1 = to_pairs(grads1)
    _exchange_alone(pairs1, "reduce_pair_ffn1")
    sums1 = pair_sums(ffn1, grads1, pairs1)
    chips1 = _chip_exchange(sums1)
    dhs0, _, dn1_p = _ffn_bwd_dh(
        dg1, du1, full["ffn1_w_gate"], full["ffn1_w_up"], hs0, r1, w2["ffn1_norm"], dhs1, tm, "ffn1_bwd_dh", comm=[chips1])
    update(ffn1, sums1, chips1)
    grad_x = dhs0[:n_seq][None]

    partial = {
        "ffn1_norm": dn1_p.sum(0), "mix_norm": dn2_p.sum(0), "b_in": dbin, "conv_cf_b": dcb,
        "ln_cf_g": dlg_p.sum(0), "ln_cf_b": dlb_p.sum(0), "ffn2_norm": dn3_p.sum(0), "final_norm": dgf_p.sum(0),
    }
    loss_seg = jnp.pad(loss_p.sum((0, 2)).reshape(1, 1), ((0, 0), (0, 127)))
    pieces = [partial[k] for k in REPLICATED] + [loss_seg, dhs0[n_seq:n_seq + n_meta].reshape(1, n_meta * d)]
    segs, off = [], 0
    for p in pieces:
        segs.append((off, p.shape[1]))
        off += p.shape[1]
    rows8 = _gather_rows(jnp.concatenate(pieces, axis=1), "gather_small")
    res = _adamw_replicated(rows8, segs, [w2[k] for k in REPLICATED], [m2[k] for k in REPLICATED],
                            [v2[k] for k in REPLICATED], 0.5 / d, "adamw_replicated")
    loss = res[0][0, 0]
    for p, k in enumerate(REPLICATED):
        out[k] = [r.reshape(w[k].shape) for r in res[1 + 4 * p:5 + 4 * p]]
    ob = w2["meta_tokens"].shape[1]
    gmeta = lax.dynamic_slice_in_dim(res[-1].reshape(n_meta, d), (4 * xi + 2 * yi + ci) * ob, ob, axis=1)
    out["meta_tokens"] = [gmeta] + _adamw_plain(gmeta, w2["meta_tokens"], m2["meta_tokens"], v2["meta_tokens"], "adamw_meta_tokens")

    return (loss, grad_x, *[out[k][0] for k in WEIGHTS], *[out[k][1] for k in WEIGHTS],
            *[out[k][2] for k in WEIGHTS], *[out[k][3] for k in WEIGHTS])


def kernel(x, meta_tokens, ffn1_norm, ffn1_w_gate, ffn1_w_up, ffn1_w_down, mix_norm, w_in, b_in, conv_sc_w, conv_cf_w, conv_cf_b, ln_cf_g, ln_cf_b, w_out, ffn2_norm, ffn2_w_gate, ffn2_w_up, ffn2_w_down, final_norm, loss_target, m_meta_tokens, m_ffn1_norm, m_ffn1_w_gate, m_ffn1_w_up, m_ffn1_w_down, m_mix_norm, m_w_in, m_b_in, m_conv_sc_w, m_conv_cf_w, m_conv_cf_b, m_ln_cf_g, m_ln_cf_b, m_w_out, m_ffn2_norm, m_ffn2_w_gate, m_ffn2_w_up, m_ffn2_w_down, m_final_norm, v_meta_tokens, v_ffn1_norm, v_ffn1_w_gate, v_ffn1_w_up, v_ffn1_w_down, v_mix_norm, v_w_in, v_b_in, v_conv_sc_w, v_conv_cf_w, v_conv_cf_b, v_ln_cf_g, v_ln_cf_b, v_w_out, v_ffn2_norm, v_ffn2_w_gate, v_ffn2_w_up, v_ffn2_w_down, v_final_norm):
    given = dict(locals())
    w = {k: given[k] for k in WEIGHTS}
    m = {k: given["m_" + k] for k in WEIGHTS}
    v = {k: given["v_" + k] for k in WEIGHTS}
    return _step(x, loss_target, w, m, v)
```

```python
import functools

import jax
import jax.numpy as jnp
from jax import lax
from jax.experimental import pallas as pl
from jax.experimental.pallas import tpu as pltpu

F32 = jnp.float32
BF16 = jnp.bfloat16
EPS = 1e-6
FFN_RES_SCALE = 0.5
SC_WIDTH = 3
CF_WIDTH = 31
ADAM_LR = 0.001
ADAM_B1 = 0.9
ADAM_B2 = 0.999
ADAM_EPS = 1e-08
ADAM_WD = 0.01
ADAM_STEP = 10

N_DEV = 8
N_ROW_TILES = 8
ROW_ALIGN = 256
CONV_PAD = 32
CONV_CH = 128
SUB_ROWS = 32
SUBLANES = 8
BF16_ROWS = 16
VMEM_LIMIT = 56 * 1024 * 1024
MESH = pl.DeviceIdType.MESH
ANY = pl.BlockSpec(memory_space=pl.ANY)

NT = (((1,), (1,)), ((), ()))
TN = (((0,), (0,)), ((), ()))


def _pallas(body, **kw):
    return pl.pallas_call(body, **kw)


class _Exchange:
    def __init__(self, inputs, out_shapes, sem_shapes, start, finish, aliases=None):
        self.inputs, self.out_shapes, self.sem_shapes = list(inputs), list(out_shapes), list(sem_shapes)
        self.start, self.finish, self.aliases = start, finish, dict(aliases or {})
        self.results = None


def _call(body, comm=(), **kw):
    if not comm:
        return _pallas(body, **kw)
    grid = kw.pop("grid")
    in_specs = list(kw.pop("in_specs"))
    out_specs, out_shape = kw.pop("out_specs"), kw.pop("out_shape")
    scratch = list(kw.pop("scratch_shapes", []))
    single = not isinstance(out_shape, (list, tuple))
    out_specs, out_shape = ([out_specs], [out_shape]) if single else (list(out_specs), list(out_shape))
    n_in, n_out, n_scr = len(in_specs), len(out_shape), len(scratch)
    c_in = [a for job in comm for a in job.inputs]
    c_out = [s for job in comm for s in job.out_shapes]
    c_sem = [s for job in comm for s in job.sem_shapes]
    aliases, i0, o0 = {}, n_in, n_out
    for job in comm:
        aliases.update({i0 + i: o0 + o for i, o in job.aliases.items()})
        i0, o0 = i0 + len(job.inputs), o0 + len(job.out_shapes)

    def hosted(*refs):
        pos = [0]

        def take(n):
            pos[0] += n
            return refs[pos[0] - n:pos[0]]

        ins, cins, outs, couts, scr, sems = take(n_in), take(len(c_in)), take(n_out), take(len(c_out)), take(n_scr), take(len(c_sem))
        ids = [pl.program_id(k) for k in range(len(grid))]
        first = functools.reduce(jnp.logical_and, [i == 0 for i in ids])
        last = functools.reduce(jnp.logical_and, [i == g - 1 for i, g in zip(ids, grid)])

        def each(phase):
            i, o, s = 0, 0, 0
            for job in comm:
                ni, no, ns = len(job.inputs), len(job.out_shapes), len(job.sem_shapes)
                getattr(job, phase)(cins[i:i + ni], couts[o:o + no], sems[s:s + ns])
                i, o, s = i + ni, o + no, s + ns

        @pl.when(first)
        def _():
            each("start")

        body(*ins, *outs, *scr)

        @pl.when(last)
        def _():
            each("finish")

    call = _pallas(
        hosted, grid=grid, in_specs=in_specs + [ANY] * len(c_in), out_specs=out_specs + [ANY] * len(c_out),
        out_shape=out_shape + c_out, scratch_shapes=scratch + c_sem, input_output_aliases=aliases, **kw)

    def run(*args):
        res = call(*args, *c_in)
        o = n_out
        for job in comm:
            job.results = list(res[o:o + len(job.out_shapes)])
            o += len(job.out_shapes)
        return res[0] if single else list(res[:n_out])

    return run


def _exchange_alone(job, name, comm=()):
    n_in, n_out = len(job.inputs), len(job.out_shapes)

    def body(*refs):
        ins, outs, sems = refs[:n_in], refs[n_in:n_in + n_out], refs[n_in + n_out:]
        job.start(ins, outs, sems)
        job.finish(ins, outs, sems)

    res = _pallas(body, name=name, in_specs=[ANY] * n_in, out_specs=[ANY] * n_out, out_shape=job.out_shapes,
                  scratch_shapes=job.sem_shapes, input_output_aliases=job.aliases)(*job.inputs)
    job.results = list(res)
    return job.results


def _params(**kw):
    return pltpu.CompilerParams(vmem_limit_bytes=VMEM_LIMIT, **kw)


def _sigmoid(x):
    return 0.5 * jnp.tanh(0.5 * x) + 0.5


def _sds(shape, dtype):
    return jax.ShapeDtypeStruct(shape, dtype)


def _place():
    x, y, c = lax.axis_index("x"), lax.axis_index("y"), lax.axis_index("c")
    chips = [(1 - x, y), (x, 1 - y), (1 - x, 1 - y)]
    return x, y, c, chips


def _slot(ref, p):
    return ref.at[4 * p[0] + 2 * p[1] + p[2]]


def _remote(src, dst, send_sem, recv_sem, to):
    return pltpu.make_async_remote_copy(src_ref=src, dst_ref=dst, send_sem=send_sem, recv_sem=recv_sem,
                                        device_id=to, device_id_type=MESH)


def _gather_direct(arrs):
    n = len(arrs)

    def copies(ins, outs, sems):
        send_sems, recv_sems, local_sems = sems
        x, y, c, chips = _place()
        me = (x, y, c)
        peers = [(x, y, 1 - c)] + [(*chip, c) for chip in chips]
        local = [pltpu.make_async_copy(ins[a], _slot(outs[a], me), local_sems.at[a]) for a in range(n)]
        sends = [_remote(ins[a], _slot(outs[a], me), send_sems.at[a, k], recv_sems.at[a, k], peer)
                 for a in range(n) for k, peer in enumerate(peers)]
        arrivals = [_remote(ins[a], _slot(outs[a], peer), send_sems.at[a, k], recv_sems.at[a, k], peer)
                    for a in range(n) for k, peer in enumerate(peers)]
        return local, sends, arrivals

    def start(ins, outs, sems):
        local, sends, _ = copies(ins, outs, sems)
        for cp in local + sends:
            cp.start()

    def finish(ins, outs, sems):
        local, sends, arrivals = copies(ins, outs, sems)
        for cp in arrivals:
            cp.wait_recv()
        for cp in sends:
            cp.wait_send()
        for cp in local:
            cp.wait()

    dma = pltpu.SemaphoreType.DMA
    return _Exchange(arrs, [_sds((N_DEV,) + a.shape, a.dtype) for a in arrs], [dma((n, 4)), dma((n, 4)), dma((n,))],
                     start, finish)


def _gather_forward(gathered):
    n = len(gathered)

    def copies(ins, outs, sems):
        send_sems, recv_sems = sems
        x, y, c, chips = _place()
        sibling = (x, y, 1 - c)
        sends = [_remote(_slot(ins[a], (*chip, c)), _slot(outs[a], (*chip, c)), send_sems.at[a, j], recv_sems.at[a, j], sibling)
                 for a in range(n) for j, chip in enumerate(chips)]
        arrivals = [_remote(_slot(ins[a], (*chip, c)), _slot(outs[a], (*chip, 1 - c)), send_sems.at[a, j], recv_sems.at[a, j], sibling)
                    for a in range(n) for j, chip in enumerate(chips)]
        return sends, arrivals

    def start(ins, outs, sems):
        for cp in copies(ins, outs, sems)[0]:
            cp.start()

    def finish(ins, outs, sems):
        sends, arrivals = copies(ins, outs, sems)
        for cp in arrivals:
            cp.wait_recv()
        for cp in sends:
            cp.wait_send()

    dma = pltpu.SemaphoreType.DMA
    return _Exchange(gathered, [_sds(a.shape, a.dtype) for a in gathered], [dma((n, 3)), dma((n, 3))], start, finish,
                     aliases={a: a for a in range(n)})


def _pair_exchange(arrs):
    n = len(arrs)

    def copies(ins, outs, sems):
        x, y, c, _ = _place()
        return [_remote(ins[a].at[:, 1 - c], outs[a], sems[0].at[a], sems[1].at[a], (x, y, 1 - c)) for a in range(n)]

    def start(ins, outs, sems):
        for cp in copies(ins, outs, sems):
            cp.start()

    def finish(ins, outs, sems):
        for cp in copies(ins, outs, sems):
            cp.wait()

    dma = pltpu.SemaphoreType.DMA
    return _Exchange(arrs, [_sds((4,) + a.shape[2:], a.dtype) for a in arrs], [dma((n,)), dma((n,))], start, finish)


def _chip_exchange(arrs):
    n = len(arrs)

    def copies(ins, outs, sems):
        x, y, c, chips = _place()
        return [_remote(ins[a].at[1 + j], outs[a].at[j], sems[0].at[a, j], sems[1].at[a, j], (*chip, c))
                for a in range(n) for j, chip in enumerate(chips)]

    def start(ins, outs, sems):
        for cp in copies(ins, outs, sems):
            cp.start()

    def finish(ins, outs, sems):
        for cp in copies(ins, outs, sems):
            cp.wait()

    dma = pltpu.SemaphoreType.DMA
    return _Exchange(arrs, [_sds((3,) + a.shape[1:], a.dtype) for a in arrs], [dma((n, 3)), dma((n, 3))], start, finish)


def _gather_rows(vec, name, comm=()):
    def body(in_ref, out_ref, send_sems, recv_sems, local_sem):
        x, y, c, _ = _place()
        me = 4 * x + 2 * y + c
        mine = pltpu.make_async_copy(in_ref, out_ref.at[me], local_sem)
        mine.start()
        copies = []
        for k in range(1, N_DEV):
            to = (1 - x if k & 4 else x, 1 - y if k & 2 else y, 1 - c if k & 1 else c)
            copies.append(pltpu.make_async_remote_copy(
                src_ref=in_ref, dst_ref=out_ref.at[me], send_sem=send_sems.at[k - 1], recv_sem=recv_sems.at[k - 1],
                device_id=to, device_id_type=MESH))
        for cp in copies:
            cp.start()
        for cp in copies:
            cp.wait()
        mine.wait()

    return _call(
        body, name=name, out_shape=_sds((N_DEV,) + vec.shape, vec.dtype), in_specs=[ANY], out_specs=ANY,
        scratch_shapes=[pltpu.SemaphoreType.DMA((7,)), pltpu.SemaphoreType.DMA((7,)), pltpu.SemaphoreType.DMA],
    )(vec)


def _for_row_groups(tm, fn):
    def step(i, carry):
        fn(pl.ds(pl.multiple_of(i * SUB_ROWS, SUB_ROWS), SUB_ROWS), i * SUB_ROWS)
        return carry

    lax.fori_loop(0, tm // SUB_ROWS, step, 0)


def _row_parts(tm, n=4):
    units = tm // BF16_ROWS
    sizes = [(units // n + (1 if i < units % n else 0)) * BF16_ROWS for i in range(n)]
    starts = [sum(sizes[:i]) for i in range(n)]
    return [slice(s, s + z) for s, z in zip(starts, sizes) if z]


def _once(shape, index_map):
    return pl.BlockSpec(shape, index_map, pipeline_mode=pl.Buffered(1))


def _rms_fwd(hs, g, tm, name, comm=()):
    rows, d = hs.shape

    def body(hs_ref, g_ref, h_ref, r_ref):
        def group(rs, _):
            xv = hs_ref[rs, :]
            r = lax.rsqrt(jnp.mean(xv * xv, axis=-1, keepdims=True) + EPS)
            h_ref[rs, :] = (xv * r * g_ref[...]).astype(BF16)
            r_ref[rs, :] = r

        _for_row_groups(tm, group)

    return _call(
        body, comm=comm, name=name, grid=(rows // tm,),
        in_specs=[pl.BlockSpec((tm, d), lambda i: (i, 0)), pl.BlockSpec((1, d), lambda i: (0, 0))],
        out_specs=[pl.BlockSpec((tm, d), lambda i: (i, 0)), pl.BlockSpec((tm, 1), lambda i: (i, 0))],
        out_shape=[_sds((rows, d), BF16), _sds((rows, 1), F32)], compiler_params=_params(),
    )(hs, g)


def _ffn_gu(h, wg, wu, tm, name, comm=()):
    rows, d = h.shape
    nb, _, fb = wg.shape

    def body(h_ref, wg_ref, wu_ref, g_ref, u_ref):
        hv = h_ref[...]
        g_ref[0] = jnp.dot(hv, wg_ref[0], preferred_element_type=F32).astype(BF16)
        u_ref[0] = jnp.dot(hv, wu_ref[0], preferred_element_type=F32).astype(BF16)

    wspec = pl.BlockSpec((1, d, fb), lambda j, i: (j, 0, 0))
    ospec = pl.BlockSpec((1, tm, fb), lambda j, i: (j, i, 0))
    return _call(
        body, comm=comm, name=name, grid=(nb, rows // tm),
        in_specs=[pl.BlockSpec((tm, d), lambda j, i: (i, 0)), wspec, wspec],
        out_specs=[ospec, ospec], out_shape=[_sds((nb, rows, fb), BF16)] * 2, compiler_params=_params(),
    )(h, wg, wu)


def _ffn_down_norm(g, u, wd, hs, gn, tm, name, comm=()):
    nb, rows, fb = g.shape
    d = hs.shape[1]

    def body(g_ref, u_ref, wd_ref, hs_ref, gn_ref, hsn_ref, hn_ref, rn_ref, acc_ref):
        j = pl.program_id(1)

        @pl.when(j == 0)
        def _():
            acc_ref[...] = jnp.zeros_like(acc_ref)

        gv = g_ref[0]
        a = gv * _sigmoid(gv) * u_ref[0]
        acc_ref[...] += jnp.dot(a, wd_ref[0], preferred_element_type=F32)

        @pl.when(j == nb - 1)
        def _():
            def group(rs, _):
                hsn = hs_ref[rs, :] + FFN_RES_SCALE * acc_ref[rs, :]
                r = lax.rsqrt(jnp.mean(hsn * hsn, axis=-1, keepdims=True) + EPS)
                hsn_ref[rs, :] = hsn
                hn_ref[rs, :] = (hsn * r * gn_ref[...]).astype(BF16)
                rn_ref[rs, :] = r

            _for_row_groups(tm, group)

    aspec = pl.BlockSpec((1, tm, fb), lambda i, j: (j, i, 0))
    row = pl.BlockSpec((tm, d), lambda i, j: (i, 0))
    return _call(
        body, comm=comm, name=name, grid=(rows // tm, nb),
        in_specs=[aspec, aspec, pl.BlockSpec((1, fb, d), lambda i, j: (j, 0, 0)), _once((tm, d), lambda i, j: (i, 0)),
                  pl.BlockSpec((1, d), lambda i, j: (0, 0))],
        out_specs=[row, row, pl.BlockSpec((tm, 1), lambda i, j: (i, 0))],
        out_shape=[_sds((rows, d), F32), _sds((rows, d), BF16), _sds((rows, 1), F32)],
        scratch_shapes=[pltpu.VMEM((tm, d), F32)], compiler_params=_params(),
    )(g, u, wd, hs, gn)


def _ffn_down_loss(g, u, wd, hs, gf, tgt, n_seq, tm, name, comm=()):
    nb, rows, fb = g.shape
    d = hs.shape[1]
    nt = rows // tm

    def body(g_ref, u_ref, wd_ref, hs_ref, gf_ref, tgt_ref, dhs_ref, dhsb_ref, loss_ref, dgf_ref, acc_ref):
        i, j = pl.program_id(0), pl.program_id(1)

        @pl.when(j == 0)
        def _():
            acc_ref[...] = jnp.zeros_like(acc_ref)

        gv = g_ref[0]
        a = gv * _sigmoid(gv) * u_ref[0]
        acc_ref[...] += jnp.dot(a, wd_ref[0], preferred_element_type=F32)

        @pl.when(j == nb - 1)
        def _():
            loss_ref[0] = jnp.zeros((1, d), F32)
            dgf_ref[0] = jnp.zeros((1, d), F32)

            def group(rs, r0):
                hs3 = hs_ref[rs, :] + FFN_RES_SCALE * acc_ref[rs, :]
                r = lax.rsqrt(jnp.mean(hs3 * hs3, axis=-1, keepdims=True) + EPS)
                gfv = gf_ref[...]
                y = hs3 * r
                rowid = i * tm + r0 + lax.broadcasted_iota(jnp.int32, (SUB_ROWS, 1), 0)
                err = jnp.where(rowid < n_seq, y * gfv - tgt_ref[rs, :], 0.0)
                loss_ref[0] += jnp.sum(err * err, axis=0, keepdims=True)
                dout = err * (1.0 / d)
                dgf_ref[0] += jnp.sum(dout * y, axis=0, keepdims=True)
                t = dout * gfv
                dhs = r * t - hs3 * (r * r * r) * jnp.mean(t * hs3, axis=-1, keepdims=True)
                dhs_ref[rs, :] = dhs
                dhsb_ref[rs, :] = dhs.astype(BF16)

            _for_row_groups(tm, group)

    aspec = pl.BlockSpec((1, tm, fb), lambda i, j: (j, i, 0))
    row = pl.BlockSpec((tm, d), lambda i, j: (i, 0))
    once = _once((tm, d), lambda i, j: (i, 0))
    part = pl.BlockSpec((1, 1, d), lambda i, j: (i, 0, 0))
    return _call(
        body, comm=comm, name=name, grid=(nt, nb),
        in_specs=[aspec, aspec, pl.BlockSpec((1, fb, d), lambda i, j: (j, 0, 0)), once,
                  pl.BlockSpec((1, d), lambda i, j: (0, 0)), once],
        out_specs=[row, row, part, part],
        out_shape=[_sds((rows, d), F32), _sds((rows, d), BF16), _sds((nt, 1, d), F32), _sds((nt, 1, d), F32)],
        scratch_shapes=[pltpu.VMEM((tm, d), F32)], compiler_params=_params(),
    )(g, u, wd, hs, gf, tgt)


def _ffn_bwd_da(do, wd, g, u, tm, name, comm=()):
    nb, rows, fb = g.shape
    d = do.shape[1]

    def body(do_ref, wd_ref, g_ref, u_ref, dg_ref, du_ref, a_ref):
        da = FFN_RES_SCALE * lax.dot_general(do_ref[...], wd_ref[0], NT, preferred_element_type=F32)
        gv = g_ref[0].astype(F32)
        uv = u_ref[0].astype(F32)
        s = _sigmoid(gv)
        sg = gv * s
        a_ref[0] = (sg * uv).astype(BF16)
        du_ref[0] = (da * sg).astype(BF16)
        dg_ref[0] = (da * uv * (s * (1.0 + gv * (1.0 - s)))).astype(BF16)

    aspec = pl.BlockSpec((1, tm, fb), lambda j, i: (j, i, 0))
    return _call(
        body, comm=comm, name=name, grid=(nb, rows // tm),
        in_specs=[pl.BlockSpec((tm, d), lambda j, i: (i, 0)), pl.BlockSpec((1, fb, d), lambda j, i: (j, 0, 0)), aspec, aspec],
        out_specs=[aspec] * 3, out_shape=[_sds((nb, rows, fb), BF16)] * 3, compiler_params=_params(),
    )(do, wd, g, u)


def _rms_bwd(tm, d, dh_ref, hs_ref, r_ref, gn_ref, dres_ref, dhs_ref, dhsb_ref, dgn_ref):
    dgn_ref[0] = jnp.zeros((1, d), F32)

    def group(rs, _):
        dh, hs, r = dh_ref[rs, :], hs_ref[rs, :], r_ref[rs, :]
        dgn_ref[0] += jnp.sum(dh * (hs * r), axis=0, keepdims=True)
        t = dh * gn_ref[...]
        dhs = dres_ref[rs, :] + r * t - hs * (r * r * r) * jnp.mean(t * hs, axis=-1, keepdims=True)
        dhs_ref[rs, :] = dhs
        dhsb_ref[rs, :] = dhs.astype(BF16)

    _for_row_groups(tm, group)


def _ffn_bwd_dh(dg, du, wg, wu, hs, r, gn, dres, tm, name, comm=()):
    nb, rows, fb = dg.shape
    d = hs.shape[1]
    nt = rows // tm

    def body(dg_ref, du_ref, wg_ref, wu_ref, hs_ref, r_ref, gn_ref, dres_ref, dhs_ref, dhsb_ref, dgn_ref, acc_ref):
        j = pl.program_id(1)

        @pl.when(j == 0)
        def _():
            acc_ref[...] = jnp.zeros_like(acc_ref)

        acc_ref[...] += (lax.dot_general(dg_ref[0], wg_ref[0], NT, preferred_element_type=F32)
                         + lax.dot_general(du_ref[0], wu_ref[0], NT, preferred_element_type=F32))

        @pl.when(j == nb - 1)
        def _():
            _rms_bwd(tm, d, acc_ref, hs_ref, r_ref, gn_ref, dres_ref, dhs_ref, dhsb_ref, dgn_ref)

    aspec = pl.BlockSpec((1, tm, fb), lambda i, j: (j, i, 0))
    wspec = pl.BlockSpec((1, d, fb), lambda i, j: (j, 0, 0))
    row = pl.BlockSpec((tm, d), lambda i, j: (i, 0))
    once = _once((tm, d), lambda i, j: (i, 0))
    return _call(
        body, comm=comm, name=name, grid=(nt, nb),
        in_specs=[aspec, aspec, wspec, wspec, once, pl.BlockSpec((tm, 1), lambda i, j: (i, 0)),
                  pl.BlockSpec((1, d), lambda i, j: (0, 0)), once],
        out_specs=[row, row, pl.BlockSpec((1, 1, d), lambda i, j: (i, 0, 0))],
        out_shape=[_sds((rows, d), F32), _sds((rows, d), BF16), _sds((nt, 1, d), F32)],
        scratch_shapes=[pltpu.VMEM((tm, d), F32)], compiler_params=_params(),
    )(dg, du, wg, wu, hs, r, gn, dres)


def _ffn_bwd_dw(h, dg, du, a, do, tk, name, comm=()):
    nb, rows, fb = dg.shape
    d = h.shape[1]
    nk = rows // tk

    def body(h_ref, dg_ref, du_ref, a_ref, do_ref, dwg_ref, dwu_ref, dwd_ref, accg, accu, accd):
        k = pl.program_id(1)

        @pl.when(k == 0)
        def _():
            accg[...] = jnp.zeros_like(accg)
            accu[...] = jnp.zeros_like(accu)
            accd[...] = jnp.zeros_like(accd)

        hv = h_ref[...]
        accg[...] += lax.dot_general(hv, dg_ref[0], TN, preferred_element_type=F32)
        accu[...] += lax.dot_general(hv, du_ref[0], TN, preferred_element_type=F32)
        accd[...] += lax.dot_general(a_ref[0], do_ref[...], TN, preferred_element_type=F32)

        @pl.when(k == nk - 1)
        def _():
            dwg_ref[0] = accg[...].astype(BF16)
            dwu_ref[0] = accu[...].astype(BF16)
            dwd_ref[0] = (FFN_RES_SCALE * accd[...]).astype(BF16)

    aspec = pl.BlockSpec((1, tk, fb), lambda j, k: (j, k, 0))
    row = pl.BlockSpec((tk, d), lambda j, k: (k, 0))
    up = pl.BlockSpec((1, d, fb), lambda j, k: (j, 0, 0))
    down = pl.BlockSpec((1, fb, d), lambda j, k: (j, 0, 0))
    return _call(
        body, comm=comm, name=name, grid=(nb, nk),
        in_specs=[row, aspec, aspec, aspec, row], out_specs=[up, up, down],
        out_shape=[_sds((nb, d, fb), BF16), _sds((nb, d, fb), BF16), _sds((nb, fb, d), BF16)],
        scratch_shapes=[pltpu.VMEM((d, fb), F32), pltpu.VMEM((d, fb), F32), pltpu.VMEM((fb, d), F32)],
        compiler_params=_params(),
    )(h, dg, du, a, do)


def _win_fwd(h, w, b, dc, tm, name, comm=()):
    rows, d = h.shape
    ng = w.shape[1] // dc

    def body(h_ref, w_ref, b_ref, u_ref):
        u_ref[...] = (jnp.dot(h_ref[...], w_ref[...], preferred_element_type=F32) + b_ref[...]).astype(BF16)

    return _call(
        body, comm=comm, name=name, grid=(ng, rows // tm),
        in_specs=[pl.BlockSpec((tm, d), lambda m, i: (i, 0)), pl.BlockSpec((d, dc), lambda m, i: (0, m)),
                  pl.BlockSpec((1, dc), lambda m, i: (0, m))],
        out_specs=pl.BlockSpec((tm, dc), lambda m, i: (i, m)), out_shape=_sds((rows, ng * dc), BF16),
        compiler_params=_params(),
    )(h, w, b)


def _win_bwd_dh(du, w, hs, r, gn, dres, tm, name, comm=()):
    rows, d = hs.shape
    ng, _, dc = du.shape
    nt = rows // tm

    def body(du_ref, w_ref, hs_ref, r_ref, gn_ref, dres_ref, dhs_ref, dhsb_ref, dgn_ref, acc_ref):
        m = pl.program_id(1)

        @pl.when(m == 0)
        def _():
            acc_ref[...] = jnp.zeros_like(acc_ref)

        acc_ref[...] += lax.dot_general(du_ref[0], w_ref[...], NT, preferred_element_type=F32)

        @pl.when(m == ng - 1)
        def _():
            _rms_bwd(tm, d, acc_ref, hs_ref, r_ref, gn_ref, dres_ref, dhs_ref, dhsb_ref, dgn_ref)

    row = pl.BlockSpec((tm, d), lambda i, m: (i, 0))
    once = _once((tm, d), lambda i, m: (i, 0))
    return _call(
        body, comm=comm, name=name, grid=(nt, ng),
        in_specs=[pl.BlockSpec((1, tm, dc), lambda i, m: (m, i, 0)), pl.BlockSpec((d, dc), lambda i, m: (0, m)), once,
                  pl.BlockSpec((tm, 1), lambda i, m: (i, 0)), pl.BlockSpec((1, d), lambda i, m: (0, 0)), once],
        out_specs=[row, row, pl.BlockSpec((1, 1, d), lambda i, m: (i, 0, 0))],
        out_shape=[_sds((rows, d), F32), _sds((rows, d), BF16), _sds((nt, 1, d), F32)],
        scratch_shapes=[pltpu.VMEM((tm, d), F32)], compiler_params=_params(),
    )(du, w, hs, r, gn, dres)


def _win_bwd_dw(h, du, tk, name, comm=()):
    rows, d = h.shape
    ng, _, dc = du.shape
    nk = rows // tk

    def body(h_ref, du_ref, dw_ref, db_ref, acc, accb):
        k = pl.program_id(1)

        @pl.when(k == 0)
        def _():
            acc[...] = jnp.zeros_like(acc)
            accb[...] = jnp.zeros_like(accb)

        duv = du_ref[0]
        acc[...] += lax.dot_general(h_ref[...], duv, TN, preferred_element_type=F32)
        accb[...] += jnp.sum(duv.astype(F32), axis=0, keepdims=True)

        @pl.when(k == nk - 1)
        def _():
            dw_ref[...] = acc[...].astype(BF16)
            db_ref[...] = accb[...]

    return _call(
        body, comm=comm, name=name, grid=(ng, nk),
        in_specs=[pl.BlockSpec((tk, d), lambda m, k: (k, 0)), pl.BlockSpec((1, tk, dc), lambda m, k: (m, k, 0))],
        out_specs=[pl.BlockSpec((d, dc), lambda m, k: (0, m)), pl.BlockSpec((1, dc), lambda m, k: (0, m))],
        out_shape=[_sds((d, ng * dc), BF16), _sds((1, ng * dc), F32)],
        scratch_shapes=[pltpu.VMEM((d, dc), F32), pltpu.VMEM((1, dc), F32)], compiler_params=_params(),
    )(h, du)


def _layernorm_silu(zc, lg, lb):
    mu = jnp.mean(zc, axis=-1, keepdims=True)
    xc = zc - mu
    rstd = lax.rsqrt(jnp.mean(xc * xc, axis=-1, keepdims=True) + EPS)
    nrm = xc * rstd
    lin = nrm * lg + lb
    s = _sigmoid(lin)
    return nrm, rstd, lin, s


def _wout_fwd(zc, ysc, wout, hs, lg, lb, gn, tm, name, comm=()):
    rows, dc = zc.shape
    d = hs.shape[1]

    def body(zc_ref, ysc_ref, w_ref, hs_ref, lg_ref, lb_ref, gn_ref, y_ref, hsn_ref, hn_ref, rn_ref):
        def mix(rs, _):
            _, _, lin, s = _layernorm_silu(zc_ref[rs, :], lg_ref[...], lb_ref[...])
            y_ref[rs, :dc] = ysc_ref[rs, :]
            y_ref[rs, dc:] = (lin * s).astype(BF16)

        _for_row_groups(tm, mix)
        hsn_ref[...] = jnp.dot(y_ref[...], w_ref[...], preferred_element_type=F32)

        def norm(rs, _):
            hsn = hs_ref[rs, :] + hsn_ref[rs, :]
            r = lax.rsqrt(jnp.mean(hsn * hsn, axis=-1, keepdims=True) + EPS)
            hsn_ref[rs, :] = hsn
            hn_ref[rs, :] = (hsn * r * gn_ref[...]).astype(BF16)
            rn_ref[rs, :] = r

        _for_row_groups(tm, norm)

    half = pl.BlockSpec((tm, dc), lambda i: (i, 0))
    row = pl.BlockSpec((tm, d), lambda i: (i, 0))
    vec_c = pl.BlockSpec((1, dc), lambda i: (0, 0))
    return _call(
        body, comm=comm, name=name, grid=(rows // tm,),
        in_specs=[half, half, _once((2 * dc, d), lambda i: (0, 0)), row, vec_c, vec_c,
                  pl.BlockSpec((1, d), lambda i: (0, 0))],
        out_specs=[pl.BlockSpec((tm, 2 * dc), lambda i: (i, 0)), row, row, pl.BlockSpec((tm, 1), lambda i: (i, 0))],
        out_shape=[_sds((rows, 2 * dc), BF16), _sds((rows, d), F32), _sds((rows, d), BF16), _sds((rows, 1), F32)],
        compiler_params=_params(),
    )(zc, ysc, wout, hs, lg, lb, gn)


def _wout_bwd_dy(do, wout, zc, lg, lb, tm, name, comm=()):
    rows, dc = zc.shape
    d = do.shape[1]
    nt = rows // tm

    def body(do_ref, w_ref, zc_ref, lg_ref, lb_ref, dysc_ref, dzc_ref, dlg_ref, dlb_ref, dy_ref):
        dy_ref[...] = lax.dot_general(do_ref[...], w_ref[...], NT, preferred_element_type=F32)
        dlb_ref[0] = jnp.zeros((1, dc), F32)
        dlg_ref[0] = jnp.zeros((1, dc), F32)

        def group(rs, _):
            dysc_ref[rs, :] = dy_ref[rs, :dc].astype(BF16)
            nrm, rstd, lin, s = _layernorm_silu(zc_ref[rs, :], lg_ref[...], lb_ref[...])
            dl = dy_ref[rs, dc:] * (s * (1.0 + lin * (1.0 - s)))
            dlb_ref[0] += jnp.sum(dl, axis=0, keepdims=True)
            dlg_ref[0] += jnp.sum(dl * nrm, axis=0, keepdims=True)
            dn = dl * lg_ref[...]
            dzc_ref[rs, :] = rstd * (dn - jnp.mean(dn, axis=-1, keepdims=True)
                                     - nrm * jnp.mean(dn * nrm, axis=-1, keepdims=True))

        _for_row_groups(tm, group)

    half = pl.BlockSpec((tm, dc), lambda i: (i, 0))
    vec_c = pl.BlockSpec((1, dc), lambda i: (0, 0))
    part = pl.BlockSpec((1, 1, dc), lambda i: (i, 0, 0))
    return _call(
        body, comm=comm, name=name, grid=(nt,),
        in_specs=[pl.BlockSpec((tm, d), lambda i: (i, 0)), _once((2 * dc, d), lambda i: (0, 0)), half, vec_c, vec_c],
        out_specs=[half, half, part, part],
        out_shape=[_sds((rows, dc), BF16), _sds((rows, dc), F32), _sds((nt, 1, dc), F32), _sds((nt, 1, dc), F32)],
        scratch_shapes=[pltpu.VMEM((tm, 2 * dc), F32)], compiler_params=_params(),
    )(do, wout, zc, lg, lb)


def _wout_bwd_dw(y, do, nb, tk, name, comm=()):
    rows, k2 = y.shape
    d = do.shape[1]
    nk = rows // tk

    def body(y_ref, do_ref, dw_ref, acc):
        k = pl.program_id(0)

        @pl.when(k == 0)
        def _():
            acc[...] = jnp.zeros_like(acc)

        acc[...] += lax.dot_general(y_ref[...], do_ref[...], TN, preferred_element_type=F32)

        @pl.when(k == nk - 1)
        def _():
            dw_ref[...] = acc[...].astype(BF16)

    return _call(
        body, comm=comm, name=name, grid=(nk,),
        in_specs=[pl.BlockSpec((tk, k2), lambda k: (k, 0)), pl.BlockSpec((tk, d), lambda k: (k, 0))],
        out_specs=_once((k2, d), lambda k: (0, 0)), out_shape=_sds((k2, d), BF16),
        scratch_shapes=[pltpu.VMEM((k2, d), F32)], compiler_params=_params(),
    )(y, do).reshape(nb, k2 // nb, d)


def _windows(win, n_res):
    length = win.shape[0]
    return [win if r == 0 else pltpu.roll(win, length - r, 0) for r in range(n_res)]


def _taps(src_ref, start, offsets, ch):
    span = -(-(max(offsets) + ch) // SUBLANES) * SUBLANES
    win = src_ref[pl.ds(start, span), :]
    shifted = _windows(win, min(SUBLANES, max(offsets) + 1))
    return [shifted[o % SUBLANES][(o // SUBLANES) * SUBLANES:(o // SUBLANES) * SUBLANES + ch] for o in offsets]


def _rows8(v):
    return jnp.sum(v.reshape(v.shape[0] // SUBLANES, SUBLANES, v.shape[1]), axis=0)


def _conv_geometry(n_seq, n_meta):
    base = CONV_PAD + n_meta
    off_cf = base - (CF_WIDTH - 1)
    off_sc = base - (SC_WIDTH - 1)
    logical = -(-(n_meta + n_seq) // CONV_CH) * CONV_CH
    return base, off_cf, off_sc, logical


def _fill_conv_inputs(c_ref, v_ref, a_ref, g_ref, scv, sz, n_seq, n_meta):
    base = CONV_PAD + n_meta
    cb = scv.shape[1]
    scv[0:CONV_PAD, :] = jnp.zeros((CONV_PAD, cb), F32)
    sz[0:CONV_PAD, :] = jnp.zeros((CONV_PAD, cb), F32)

    def put(src, dst, n):
        cv = c_ref[src, :].astype(F32) * v_ref[src, :].astype(F32)
        scv[dst, :] = cv
        sz[dst, :] = a_ref[src, :].astype(F32) * _sigmoid(g_ref[src, :].astype(F32))

    put(pl.ds(n_seq, n_meta), pl.ds(CONV_PAD, n_meta), n_meta)

    def chunk(i, carry):
        t0 = pl.multiple_of(i * CONV_CH, CONV_CH)
        put(pl.ds(t0, CONV_CH), pl.ds(base + t0, CONV_CH), CONV_CH)
        return carry

    lax.fori_loop(0, n_seq // CONV_CH, chunk, 0)


def _conv_fwd(u, wsc, wcf, cbias, n_seq, n_meta, name, comm=()):
    rows = u.shape[0]
    nb, _, cb = wsc.shape
    dc = nb * cb
    base, off_cf, off_sc, _ = _conv_geometry(n_seq, n_meta)
    a_cf, a_sc = off_cf // SUBLANES * SUBLANES, off_sc // SUBLANES * SUBLANES
    ch = CONV_CH

    def body(b_ref, c_ref, v_ref, a_ref, g_ref, wsc_ref, wcf_ref, cb_ref, ysc_ref, zc_ref, scv, sz):
        _fill_conv_inputs(c_ref, v_ref, a_ref, g_ref, scv, sz, n_seq, n_meta)
        w3, w31, bias = wsc_ref[0], wcf_ref[0], cb_ref[...]

        def chunk(i, carry):
            t0 = pl.multiple_of(i * ch, ch)
            acc = jnp.zeros((ch, cb), F32)
            for k, win in enumerate(_taps(sz, t0 + a_cf, [off_cf - a_cf + k for k in range(CF_WIDTH)], ch)):
                acc = acc + win * w31[k:k + 1, :]
            zc_ref[pl.ds(t0, ch), :] = acc + bias
            s = jnp.zeros((ch, cb), F32)
            for k, win in enumerate(_taps(scv, t0 + a_sc, [off_sc - a_sc + k for k in range(SC_WIDTH)], ch)):
                s = s + win * w3[k:k + 1, :]
            ysc_ref[pl.ds(t0, ch), :] = (b_ref[pl.ds(t0, ch), :].astype(F32) * s).astype(BF16)
            return carry

        lax.fori_loop(0, n_seq // ch, chunk, 0)
        ysc_ref[n_seq:rows, :] = jnp.zeros((rows - n_seq, cb), BF16)
        zc_ref[n_seq:rows, :] = jnp.zeros((rows - n_seq, cb), F32)

    ucol = [pl.BlockSpec((rows, cb), functools.partial(lambda m, j: (0, m * nb + j), m)) for m in range(5)]
    blk = pl.BlockSpec((rows, cb), lambda j: (0, j))
    return _call(
        body, comm=comm, name=name, grid=(nb,),
        in_specs=ucol + [pl.BlockSpec((1, SC_WIDTH, cb), lambda j: (j, 0, 0)),
                         pl.BlockSpec((1, CF_WIDTH, cb), lambda j: (j, 0, 0)), pl.BlockSpec((1, cb), lambda j: (0, j))],
        out_specs=[blk, blk], out_shape=[_sds((rows, dc), BF16), _sds((rows, dc), F32)],
        scratch_shapes=[pltpu.VMEM((base + n_seq, cb), F32)] * 2, compiler_params=_params(),
    )(u, u, u, u, u, wsc, wcf, cbias)


def _conv_bwd(u, dysc, dzc, wsc, wcf, n_seq, n_meta, name, comm=()):
    rows = u.shape[0]
    nb, _, cb = wsc.shape
    dc = nb * cb
    base, off_cf, off_sc, logical = _conv_geometry(n_seq, n_meta)
    a_cf, a_sc = off_cf // SUBLANES * SUBLANES, off_sc // SUBLANES * SUBLANES
    ch = CONV_CH
    tail = CONV_PAD

    def body(b_ref, c_ref, v_ref, a_ref, g_ref, dysc_ref, dzc_ref, wsc_ref, wcf_ref,
             du_ref, dwsc_ref, dwcf_ref, dcb_ref,
             scv, sz, sds_, sdz, dlcv, dlz, accsc, acccf, accb):
        _fill_conv_inputs(c_ref, v_ref, a_ref, g_ref, scv, sz, n_seq, n_meta)
        w3, w31 = wsc_ref[0], wcf_ref[0]
        dub_ref, duc_ref, duv_ref, dua_ref, dug_ref = (du_ref.at[m] for m in range(5))
        sds_[0:n_meta, :] = jnp.zeros((n_meta, cb), F32)
        sdz[0:n_meta, :] = jnp.zeros((n_meta, cb), F32)
        behind = logical + tail - (n_meta + n_seq)
        sds_[n_meta + n_seq:logical + tail, :] = jnp.zeros((behind, cb), F32)
        sdz[n_meta + n_seq:logical + tail, :] = jnp.zeros((behind, cb), F32)
        accsc[...] = jnp.zeros_like(accsc)
        acccf[...] = jnp.zeros_like(acccf)
        accb[...] = jnp.zeros_like(accb)

        def forward_chunk(i, carry):
            t0 = pl.multiple_of(i * ch, ch)
            rws = pl.ds(t0, ch)
            dy = dysc_ref[rws, :].astype(F32)
            ds = dy * b_ref[rws, :].astype(F32)
            dz = dzc_ref[rws, :]
            sds_[pl.ds(n_meta + t0, ch), :] = ds
            sdz[pl.ds(n_meta + t0, ch), :] = dz
            s = jnp.zeros((ch, cb), F32)
            for k, win in enumerate(_taps(scv, t0 + a_sc, [off_sc - a_sc + k for k in range(SC_WIDTH)], ch)):
                s = s + win * w3[k:k + 1, :]
                accsc[k * SUBLANES:(k + 1) * SUBLANES, :] += _rows8(ds * win)
            dub_ref[rws, :] = (dy * s).astype(BF16)
            for k, win in enumerate(_taps(sz, t0 + a_cf, [off_cf - a_cf + k for k in range(CF_WIDTH)], ch)):
                acccf[k * SUBLANES:(k + 1) * SUBLANES, :] += _rows8(dz * win)
            accb[...] += _rows8(dz)
            return carry

        lax.fori_loop(0, n_seq // ch, forward_chunk, 0)

        def backward_chunk(i, carry):
            p0 = pl.multiple_of(i * ch, ch)
            dcv = jnp.zeros((ch, cb), F32)
            for k, win in enumerate(_taps(sds_, p0, [SC_WIDTH - 1 - k for k in range(SC_WIDTH)], ch)):
                dcv = dcv + win * w3[k:k + 1, :]
            dlcv[pl.ds(p0, ch), :] = dcv
            dzi = jnp.zeros((ch, cb), F32)
            for k, win in enumerate(_taps(sdz, p0, [CF_WIDTH - 1 - k for k in range(CF_WIDTH)], ch)):
                dzi = dzi + win * w31[k:k + 1, :]
            dlz[pl.ds(p0, ch), :] = dzi
            return carry

        lax.fori_loop(0, logical // ch, backward_chunk, 0)

        def gates(phys, logi):
            dcv, dzi = dlcv[logi, :], dlz[logi, :]
            duc_ref[phys, :] = (dcv * v_ref[phys, :].astype(F32)).astype(BF16)
            duv_ref[phys, :] = (dcv * c_ref[phys, :].astype(F32)).astype(BF16)
            s = _sigmoid(g_ref[phys, :].astype(F32))
            dua_ref[phys, :] = (dzi * s).astype(BF16)
            dug_ref[phys, :] = (dzi * a_ref[phys, :].astype(F32) * s * (1.0 - s)).astype(BF16)

        def gate_chunk(i, carry):
            t0 = pl.multiple_of(i * ch, ch)
            gates(pl.ds(t0, ch), pl.ds(n_meta + t0, ch))
            return carry

        lax.fori_loop(0, n_seq // ch, gate_chunk, 0)
        gates(pl.ds(n_seq, n_meta), pl.ds(0, n_meta))
        dub_ref[n_seq:rows, :] = jnp.zeros((rows - n_seq, cb), BF16)
        pad0 = n_seq + n_meta
        for ref in (duc_ref, duv_ref, dua_ref, dug_ref):
            ref[pad0:rows, :] = jnp.zeros((rows - pad0, cb), BF16)
        dwsc_ref[0] = jnp.sum(accsc[...].reshape(SC_WIDTH, SUBLANES, cb), axis=1)
        dwcf_ref[0] = jnp.sum(acccf[...].reshape(CF_WIDTH, SUBLANES, cb), axis=1)
        dcb_ref[...] = jnp.sum(accb[...], axis=0, keepdims=True)

    ucol = [pl.BlockSpec((rows, cb), functools.partial(lambda m, j: (0, m * nb + j), m)) for m in range(5)]
    blk = pl.BlockSpec((rows, cb), lambda j: (0, j))
    wsc_spec = pl.BlockSpec((1, SC_WIDTH, cb), lambda j: (j, 0, 0))
    wcf_spec = pl.BlockSpec((1, CF_WIDTH, cb), lambda j: (j, 0, 0))
    outs = _call(
        body, comm=comm, name=name, grid=(nb,),
        in_specs=ucol + [blk, blk, wsc_spec, wcf_spec],
        out_specs=[pl.BlockSpec((5, rows, cb), lambda j: (0, 0, j)), wsc_spec, wcf_spec,
                   pl.BlockSpec((1, cb), lambda j: (0, j))],
        out_shape=[_sds((5, rows, dc), BF16), _sds((nb, SC_WIDTH, cb), F32), _sds((nb, CF_WIDTH, cb), F32),
                   _sds((1, dc), F32)],
        scratch_shapes=[pltpu.VMEM((base + n_seq, cb), F32)] * 2 + [pltpu.VMEM((logical + tail, cb), F32)] * 2
        + [pltpu.VMEM((logical, cb), F32)] * 2
        + [pltpu.VMEM((SC_WIDTH * SUBLANES, cb), F32), pltpu.VMEM((CF_WIDTH * SUBLANES, cb), F32),
           pltpu.VMEM((SUBLANES, cb), F32)],
        compiler_params=_params(),
    )(u, u, u, u, u, dysc, dzc, wsc, wcf)
    return outs


def _row_tile(rows, cols):
    return rows // 4 if rows % 64 == 0 and rows * cols >= (1 << 18) else rows


def _pair_sum(grad, sib, idx, name, comm=()):
    _, rows, cols = grad.shape
    tr = _row_tile(rows, cols)

    def body(idx_ref, g_ref, s_ref, o_ref):
        o_ref[0] = (g_ref[0].astype(F32) + s_ref[0].astype(F32)).astype(o_ref.dtype)

    return _call(
        body, name=name,
        grid_spec=pltpu.PrefetchScalarGridSpec(
            num_scalar_prefetch=1, grid=(4, rows // tr),
            in_specs=[pl.BlockSpec((1, tr, cols), lambda k, i, idx_ref: (idx_ref[k], i, 0)),
                      pl.BlockSpec((1, tr, cols), lambda k, i, idx_ref: (idx_ref[4 + k], i, 0))],
            out_specs=pl.BlockSpec((1, tr, cols), lambda k, i, idx_ref: (k, i, 0))),
        out_shape=_sds((4, rows, cols), grad.dtype), compiler_params=_params(),
    )(idx, grad, sib)


def _adamw_math(w, g, m, v):
    m = ADAM_B1 * m + (1.0 - ADAM_B1) * g
    v = ADAM_B2 * v + (1.0 - ADAM_B2) * (g * g)
    m_hat = m / (1.0 - ADAM_B1 ** ADAM_STEP)
    v_hat = v / (1.0 - ADAM_B2 ** ADAM_STEP)
    delta = -ADAM_LR * (m_hat / (jnp.sqrt(v_hat) + ADAM_EPS) + ADAM_WD * w)
    return delta, m, v


def _adamw_sharded(own, got, w, m, v, name, comm=()):
    rows, cols = w.shape
    tr = _row_tile(rows, cols)

    def body(own_ref, g0_ref, g1_ref, g2_ref, w_ref, m_ref, v_ref, g_ref, d_ref, nm_ref, nv_ref):
        g = own_ref[0].astype(F32) + g0_ref[0].astype(F32) + g1_ref[0].astype(F32) + g2_ref[0].astype(F32)
        delta, nm, nv = _adamw_math(w_ref[...], g, m_ref[...], v_ref[...])
        g_ref[...] = g
        d_ref[...] = delta
        nm_ref[...] = nm
        nv_ref[...] = nv

    flat = pl.BlockSpec((tr, cols), lambda i: (i, 0))
    slot = [pl.BlockSpec((1, tr, cols), functools.partial(lambda k, i: (k, i, 0), k)) for k in range(3)]
    return _call(
        body, comm=comm, name=name, grid=(rows // tr,),
        in_specs=[slot[0]] + slot + [flat] * 3, out_specs=[flat] * 4, out_shape=[_sds((rows, cols), F32)] * 4,
        compiler_params=_params(),
    )(own, got, got, got, w, m, v)


def _adamw_replicated(gathered, segs, ws, ms, vs, loss_scale, name, comm=()):
    n = len(ws)

    def body(*refs):
        gat = refs[0]
        w_refs, m_refs, v_refs = refs[1:1 + n], refs[1 + n:1 + 2 * n], refs[1 + 2 * n:1 + 3 * n]
        outs = refs[1 + 3 * n:]

        def total(off, width):
            s = gat[0, :, off:off + width]
            for k in range(1, N_DEV):
                s = s + gat[k, :, off:off + width]
            return s

        outs[0][...] = loss_scale * total(segs[n][0], segs[n][1])
        for p in range(n):
            g = total(*segs[p])
            delta, nm, nv = _adamw_math(w_refs[p][...], g, m_refs[p][...], v_refs[p][...])
            for q, val in enumerate((g, delta, nm, nv)):
                outs[1 + 4 * p + q][...] = val
        for e, seg in enumerate(segs[n + 1:]):
            outs[1 + 4 * n + e][...] = total(*seg)

    return _call(
        body, name=name,
        out_shape=[_sds((1, segs[n][1]), F32)] + [_sds(w.shape, F32) for w in ws for _ in range(4)]
        + [_sds((1, seg[1]), F32) for seg in segs[n + 1:]],
        compiler_params=_params(),
    )(gathered, *ws, *ms, *vs)


def _adamw_plain(g, w, m, v, name):
    def body(g_ref, w_ref, m_ref, v_ref, d_ref, nm_ref, nv_ref):
        d_ref[...], nm_ref[...], nv_ref[...] = _adamw_math(w_ref[...], g_ref[...], m_ref[...], v_ref[...])

    return list(_call(body, name=name, out_shape=[_sds(w.shape, F32)] * 3)(g, w, m, v))


REPLICATED = ("ffn1_norm", "mix_norm", "b_in", "conv_cf_b", "ln_cf_g", "ln_cf_b", "ffn2_norm", "final_norm")
SHARDED = ("meta_tokens", "ffn1_w_gate", "ffn1_w_up", "ffn1_w_down", "w_in", "conv_sc_w", "conv_cf_w", "w_out",
           "ffn2_w_gate", "ffn2_w_up", "ffn2_w_down")
WEIGHTS = ("meta_tokens", "ffn1_norm", "ffn1_w_gate", "ffn1_w_up", "ffn1_w_down", "mix_norm", "w_in", "b_in",
           "conv_sc_w", "conv_cf_w", "conv_cf_b", "ln_cf_g", "ln_cf_b", "w_out", "ffn2_norm", "ffn2_w_gate",
           "ffn2_w_up", "ffn2_w_down", "final_norm")


def _blocks2d(a):
    return a.reshape(a.shape[-2:]) if a.ndim >= 2 else a.reshape(1, -1)


def _step(x, tgt, w, m, v):
    n_seq, d = x.shape[1], x.shape[2]
    n_meta = w["meta_tokens"].shape[0]
    rows = -(-(n_seq + n_meta) // ROW_ALIGN) * ROW_ALIGN
    tm = rows // N_ROW_TILES
    cb = w["conv_sc_w"].shape[-1]
    dc = cb * N_DEV
    w2 = {k: _blocks2d(a) for k, a in w.items()}
    m2 = {k: _blocks2d(a) for k, a in m.items()}
    v2 = {k: _blocks2d(a) for k, a in v.items()}

    def cast(k):
        return w2[k].astype(BF16)

    first = _gather_direct([cast("ffn1_w_gate"), cast("ffn1_w_up"), w2["meta_tokens"], w2["conv_sc_w"], w2["conv_cf_w"]])
    _exchange_alone(first, "gather_ffn1")
    full = dict(zip(("ffn1_w_gate", "ffn1_w_up", "meta_tokens", "conv_sc_w", "conv_cf_w"),
                    _exchange_alone(_gather_forward(first.results), "gather_ffn1_pass")))
    meta = jnp.transpose(full["meta_tokens"], (1, 0, 2)).reshape(n_meta, d)

    hs0 = jnp.concatenate([x[0], meta, jnp.zeros((rows - n_seq - n_meta, d), F32)], axis=0)
    tgt_rows = jnp.concatenate([tgt[0], jnp.zeros((rows - n_seq, d), F32)], axis=0)

    h1, r1 = _rms_fwd(hs0, w2["ffn1_norm"], tm, "rms_in")
    second = _gather_direct([cast("ffn1_w_down"), cast("w_in")])
    g1, u1 = _ffn_gu(h1, full["ffn1_w_gate"], full["ffn1_w_up"], tm, "ffn1_gu", comm=[second])
    full["ffn1_w_down"], win_blocks = _exchange_alone(_gather_forward(second.results), "gather_down1_pass")
    third = _gather_direct([cast("w_out"), cast("ffn2_w_gate")])
    hs1, h2, r2 = _ffn_down_norm(g1, u1, full["ffn1_w_down"], hs0, w2["mix_norm"], tm, "ffn1_down", comm=[third])
    win = jnp.transpose(win_blocks, (1, 0, 2)).reshape(d, -1)
    third_p, up2 = _gather_forward(third.results), _gather_direct([cast("ffn2_w_up")])
    u = _win_fwd(h2, win, w2["b_in"], dc, tm, "mix_in", comm=[third_p, up2])
    wout = third_p.results[0].reshape(-1, d)
    gate2p = third_p
    up2p = _gather_forward(up2.results)
    ysc, zc = _conv_fwd(u, full["conv_sc_w"], full["conv_cf_w"], w2["conv_cf_b"], n_seq, n_meta, "conv_fwd", comm=[up2p])
    down2 = _gather_direct([cast("ffn2_w_down")])
    y, hs2, h3, r3 = _wout_fwd(zc, ysc, wout, hs1, w2["ln_cf_g"], w2["ln_cf_b"], w2["ffn2_norm"], tm, "mix_out",
                               comm=[down2])
    down2p = _gather_forward(down2.results)
    full["ffn2_w_gate"], full["ffn2_w_up"] = gate2p.results[1], up2p.results[0]
    g2, u2 = _ffn_gu(h3, full["ffn2_w_gate"], full["ffn2_w_up"], tm, "ffn2_gu", comm=[down2p])
    full["ffn2_w_down"] = down2p.results[0]
    dhs3, dhs3b, loss_p, dgf_p = _ffn_down_loss(
        g2, u2, full["ffn2_w_down"], hs2, w2["final_norm"], tgt_rows, n_seq, tm, "ffn2_down_loss")

    dg2, du2, a2 = _ffn_bwd_da(dhs3b, full["ffn2_w_down"], g2, u2, tm, "ffn2_bwd_da")
    dhs2, dhs2b, dn3_p = _ffn_bwd_dh(
        dg2, du2, full["ffn2_w_gate"], full["ffn2_w_up"], hs2, r3, w2["ffn2_norm"], dhs3, tm, "ffn2_bwd_dh")
    xi, yi, ci = lax.axis_index("x"), lax.axis_index("y"), lax.axis_index("c")
    chip_of = [2 * xi + yi, 2 * (1 - xi) + yi, 2 * xi + (1 - yi), 2 * (1 - xi) + (1 - yi)]
    idx = jnp.stack([2 * ch + ci for ch in chip_of] + chip_of).astype(jnp.int32)
    out = {}

    def to_pairs(parts):
        return _pair_exchange([p.reshape((4, 2) + p.shape[1:]) for p in parts])

    def pair_sums(names, parts, pairs):
        return [_pair_sum(p, s, idx, "pair_sum_" + k) for k, p, s in zip(names, parts, pairs.results)]

    def update(names, sums, chips):
        for k, own, got in zip(names, sums, chips.results):
            res = _adamw_sharded(own, got, w2[k], m2[k], v2[k], "adamw_" + k)
            out[k] = [r.reshape(w[k].shape) for r in res]

    ffn2 = ("ffn2_w_gate", "ffn2_w_up", "ffn2_w_down")
    tk = ROW_ALIGN
    grads2 = _ffn_bwd_dw(h3, dg2, du2, a2, dhs3b, tk, "ffn2_bwd_dw")
    pairs2 = to_pairs(grads2)
    dysc, dzc, dlg_p, dlb_p = _wout_bwd_dy(dhs2b, wout, zc, w2["ln_cf_g"], w2["ln_cf_b"], tm, "mix_out_bwd_dy",
                                           comm=[pairs2])
    sums2 = pair_sums(ffn2, grads2, pairs2)
    dwout = _wout_bwd_dw(y, dhs2b, N_DEV, tk, "mix_out_bwd_dw")
    chips2 = [_chip_exchange([s]) for s in sums2]
    du, dcsw, dccw, dcb = _conv_bwd(u, dysc, dzc, full["conv_sc_w"], full["conv_cf_w"], n_seq, n_meta, "conv_bwd",
                                    comm=[chips2[0]])
    dhs1, dhs1b, dn2_p = _win_bwd_dh(du, win, hs1, r2, w2["mix_norm"], dhs2, tm, "mix_in_bwd_dh", comm=[chips2[1]])
    dwin, dbin = _win_bwd_dw(h2, du, tk, "mix_in_bwd_dw", comm=[chips2[2]])
    for k, s, ch in zip(ffn2, sums2, chips2):
        update([k], [s], ch)

    mixer = ("w_out", "w_in", "conv_sc_w", "conv_cf_w")
    gradsm = [dwout, jnp.transpose(dwin.reshape(d, N_DEV, -1), (1, 0, 2)), dcsw, dccw]
    pairsm = to_pairs(gradsm)
    dg1, du1, a1 = _ffn_bwd_da(dhs1b, full["ffn1_w_down"], g1, u1, tm, "ffn1_bwd_da", comm=[pairsm])
    sumsm = pair_sums(mixer, gradsm, pairsm)
    chipsm = _chip_exchange(sumsm)
    ffn1 = ("ffn1_w_gate", "ffn1_w_up", "ffn1_w_down")
    grads1 = _ffn_bwd_dw(h1, dg1, du1, a1, dhs1b, tk, "ffn1_bwd_dw", comm=[chipsm])
    update(mixer, sumsm, chipsm)
    pairs1 = to_pairs(grads1)
    _exchange_alone(pairs1, "reduce_pair_ffn1")
    sums1 = pair_sums(ffn1, grads1, pairs1)
    chips1 = _chip_exchange(sums1)
    dhs0, _, dn1_p = _ffn_bwd_dh(
        dg1, du1, full["ffn1_w_gate"], full["ffn1_w_up"], hs0, r1, w2["ffn1_norm"], dhs1, tm, "ffn1_bwd_dh", comm=[chips1])
    update(ffn1, sums1, chips1)
    grad_x = dhs0[:n_seq][None]

    partial = {
        "ffn1_norm": dn1_p.sum(0), "mix_norm": dn2_p.sum(0), "b_in": dbin, "conv_cf_b": dcb,
        "ln_cf_g": dlg_p.sum(0), "ln_cf_b": dlb_p.sum(0), "ffn2_norm": dn3_p.sum(0), "final_norm": dgf_p.sum(0),
    }
    loss_seg = jnp.pad(loss_p.sum((0, 2)).reshape(1, 1), ((0, 0), (0, 127)))
    pieces = [partial[k] for k in REPLICATED] + [loss_seg, dhs0[n_seq:n_seq + n_meta].reshape(1, n_meta * d)]
    segs, off = [], 0
    for p in pieces:
        segs.append((off, p.shape[1]))
        off += p.shape[1]
    rows8 = _gather_rows(jnp.concatenate(pieces, axis=1), "gather_small")
    res = _adamw_replicated(rows8, segs, [w2[k] for k in REPLICATED], [m2[k] for k in REPLICATED],
                            [v2[k] for k in REPLICATED], 0.5 / d, "adamw_replicated")
    loss = res[0][0, 0]
    for p, k in enumerate(REPLICATED):
        out[k] = [r.reshape(w[k].shape) for r in res[1 + 4 * p:5 + 4 * p]]
    ob = w2["meta_tokens"].shape[1]
    gmeta = lax.dynamic_slice_in_dim(res[-1].reshape(n_meta, d), (4 * xi + 2 * yi + ci) * ob, ob, axis=1)
    out["meta_tokens"] = [gmeta] + _adamw_plain(gmeta, w2["meta_tokens"], m2["meta_tokens"], v2["meta_tokens"], "adamw_meta_tokens")

    return (loss, grad_x, *[out[k][0] for k in WEIGHTS], *[out[k][1] for k in WEIGHTS],
            *[out[k][2] for k in WEIGHTS], *[out[k][3] for k in WEIGHTS])


def kernel(x, meta_tokens, ffn1_norm, ffn1_w_gate, ffn1_w_up, ffn1_w_down, mix_norm, w_in, b_in, conv_sc_w, conv_cf_w, conv_cf_b, ln_cf_g, ln_cf_b, w_out, ffn2_norm, ffn2_w_gate, ffn2_w_up, ffn2_w_down, final_norm, loss_target, m_meta_tokens, m_ffn1_norm, m_ffn1_w_gate, m_ffn1_w_up, m_ffn1_w_down, m_mix_norm, m_w_in, m_b_in, m_conv_sc_w, m_conv_cf_w, m_conv_cf_b, m_ln_cf_g, m_ln_cf_b, m_w_out, m_ffn2_norm, m_ffn2_w_gate, m_ffn2_w_up, m_ffn2_w_down, m_final_norm, v_meta_tokens, v_ffn1_norm, v_ffn1_w_gate, v_ffn1_w_up, v_ffn1_w_down, v_mix_norm, v_w_in, v_b_in, v_conv_sc_w, v_conv_cf_w, v_conv_cf_b, v_ln_cf_g, v_ln_cf_b, v_w_out, v_ffn2_norm, v_ffn2_w_gate, v_ffn2_w_up, v_ffn2_w_down, v_final_norm):
    given = dict(locals())
    w = {k: given[k] for k in WEIGHTS}
    m = {k: given["m_" + k] for k in WEIGHTS}
    v = {k: given["v_" + k] for k in WEIGHTS}
    return _step(x, loss_target, w, m, v)
```

```python
import functools

import jax
import jax.numpy as jnp
from jax import lax
from jax.experimental import pallas as pl
from jax.experimental.pallas import tpu as pltpu

F32 = jnp.float32
BF16 = jnp.bfloat16
EPS = 1e-6
FFN_RES_SCALE = 0.5
SC_WIDTH = 3
CF_WIDTH = 31
ADAM_LR = 0.001
ADAM_B1 = 0.9
ADAM_B2 = 0.999
ADAM_EPS = 1e-08
ADAM_WD = 0.01
ADAM_STEP = 10

N_DEV = 8
N_ROW_TILES = 8
ROW_ALIGN = 256
CONV_PAD = 32
CONV_CH = 128
SUB_ROWS = 32
SUBLANES = 8
BF16_ROWS = 16
VMEM_LIMIT = 56 * 1024 * 1024
MESH = pl.DeviceIdType.MESH
ANY = pl.BlockSpec(memory_space=pl.ANY)

NT = (((1,), (1,)), ((), ()))
TN = (((0,), (0,)), ((), ()))


def _pallas(body, **kw):
    return pl.pallas_call(body, **kw)


class _Exchange:
    def __init__(self, inputs, out_shapes, sem_shapes, start, finish, aliases=None):
        self.inputs, self.out_shapes, self.sem_shapes = list(inputs), list(out_shapes), list(sem_shapes)
        self.start, self.finish, self.aliases = start, finish, dict(aliases or {})
        self.results = None


def _call(body, comm=(), **kw):
    if not comm:
        return _pallas(body, **kw)
    grid = kw.pop("grid")
    in_specs = list(kw.pop("in_specs"))
    out_specs, out_shape = kw.pop("out_specs"), kw.pop("out_shape")
    scratch = list(kw.pop("scratch_shapes", []))
    single = not isinstance(out_shape, (list, tuple))
    out_specs, out_shape = ([out_specs], [out_shape]) if single else (list(out_specs), list(out_shape))
    n_in, n_out, n_scr = len(in_specs), len(out_shape), len(scratch)
    c_in = [a for job in comm for a in job.inputs]
    c_out = [s for job in comm for s in job.out_shapes]
    c_sem = [s for job in comm for s in job.sem_shapes]
    aliases, i0, o0 = {}, n_in, n_out
    for job in comm:
        aliases.update({i0 + i: o0 + o for i, o in job.aliases.items()})
        i0, o0 = i0 + len(job.inputs), o0 + len(job.out_shapes)

    def hosted(*refs):
        pos = [0]

        def take(n):
            pos[0] += n
            return refs[pos[0] - n:pos[0]]

        ins, cins, outs, couts, scr, sems = take(n_in), take(len(c_in)), take(n_out), take(len(c_out)), take(n_scr), take(len(c_sem))
        ids = [pl.program_id(k) for k in range(len(grid))]
        first = functools.reduce(jnp.logical_and, [i == 0 for i in ids])
        last = functools.reduce(jnp.logical_and, [i == g - 1 for i, g in zip(ids, grid)])

        def each(phase):
            i, o, s = 0, 0, 0
            for job in comm:
                ni, no, ns = len(job.inputs), len(job.out_shapes), len(job.sem_shapes)
                getattr(job, phase)(cins[i:i + ni], couts[o:o + no], sems[s:s + ns])
                i, o, s = i + ni, o + no, s + ns

        @pl.when(first)
        def _():
            each("start")

        body(*ins, *outs, *scr)

        @pl.when(last)
        def _():
            each("finish")

    call = _pallas(
        hosted, grid=grid, in_specs=in_specs + [ANY] * len(c_in), out_specs=out_specs + [ANY] * len(c_out),
        out_shape=out_shape + c_out, scratch_shapes=scratch + c_sem, input_output_aliases=aliases, **kw)

    def run(*args):
        res = call(*args, *c_in)
        o = n_out
        for job in comm:
            job.results = list(res[o:o + len(job.out_shapes)])
            o += len(job.out_shapes)
        return res[0] if single else list(res[:n_out])

    return run


def _exchange_alone(job, name, comm=()):
    n_in, n_out = len(job.inputs), len(job.out_shapes)

    def body(*refs):
        ins, outs, sems = refs[:n_in], refs[n_in:n_in + n_out], refs[n_in + n_out:]
        job.start(ins, outs, sems)
        job.finish(ins, outs, sems)

    res = _pallas(body, name=name, in_specs=[ANY] * n_in, out_specs=[ANY] * n_out, out_shape=job.out_shapes,
                  scratch_shapes=job.sem_shapes, input_output_aliases=job.aliases)(*job.inputs)
    job.results = list(res)
    return job.results


def _params(**kw):
    return pltpu.CompilerParams(vmem_limit_bytes=VMEM_LIMIT, **kw)


def _sigmoid(x):
    return 0.5 * jnp.tanh(0.5 * x) + 0.5


def _sds(shape, dtype):
    return jax.ShapeDtypeStruct(shape, dtype)


def _place():
    x, y, c = lax.axis_index("x"), lax.axis_index("y"), lax.axis_index("c")
    chips = [(1 - x, y), (x, 1 - y), (1 - x, 1 - y)]
    return x, y, c, chips


def _slot(ref, p):
    return ref.at[4 * p[0] + 2 * p[1] + p[2]]


def _remote(src, dst, send_sem, recv_sem, to):
    return pltpu.make_async_remote_copy(src_ref=src, dst_ref=dst, send_sem=send_sem, recv_sem=recv_sem,
                                        device_id=to, device_id_type=MESH)


def _gather_direct(arrs):
    n = len(arrs)

    def copies(ins, outs, sems):
        send_sems, recv_sems, local_sems = sems
        x, y, c, chips = _place()
        me = (x, y, c)
        peers = [(x, y, 1 - c)] + [(*chip, c) for chip in chips]
        local = [pltpu.make_async_copy(ins[a], _slot(outs[a], me), local_sems.at[a]) for a in range(n)]
        sends = [_remote(ins[a], _slot(outs[a], me), send_sems.at[a, k], recv_sems.at[a, k], peer)
                 for a in range(n) for k, peer in enumerate(peers)]
        arrivals = [_remote(ins[a], _slot(outs[a], peer), send_sems.at[a, k], recv_sems.at[a, k], peer)
                    for a in range(n) for k, peer in enumerate(peers)]
        return local, sends, arrivals

    def start(ins, outs, sems):
        local, sends, _ = copies(ins, outs, sems)
        for cp in local + sends:
            cp.start()

    def finish(ins, outs, sems):
        local, sends, arrivals = copies(ins, outs, sems)
        for cp in arrivals:
            cp.wait_recv()
        for cp in sends:
            cp.wait_send()
        for cp in local:
            cp.wait()

    dma = pltpu.SemaphoreType.DMA
    return _Exchange(arrs, [_sds((N_DEV,) + a.shape, a.dtype) for a in arrs], [dma((n, 4)), dma((n, 4)), dma((n,))],
                     start, finish)


def _gather_forward(gathered):
    n = len(gathered)

    def copies(ins, outs, sems):
        send_sems, recv_sems = sems
        x, y, c, chips = _place()
        sibling = (x, y, 1 - c)
        sends = [_remote(_slot(ins[a], (*chip, c)), _slot(outs[a], (*chip, c)), send_sems.at[a, j], recv_sems.at[a, j], sibling)
                 for a in range(n) for j, chip in enumerate(chips)]
        arrivals = [_remote(_slot(ins[a], (*chip, c)), _slot(outs[a], (*chip, 1 - c)), send_sems.at[a, j], recv_sems.at[a, j], sibling)
                    for a in range(n) for j, chip in enumerate(chips)]
        return sends, arrivals

    def start(ins, outs, sems):
        for cp in copies(ins, outs, sems)[0]:
            cp.start()

    def finish(ins, outs, sems):
        sends, arrivals = copies(ins, outs, sems)
        for cp in arrivals:
            cp.wait_recv()
        for cp in sends:
            cp.wait_send()

    dma = pltpu.SemaphoreType.DMA
    return _Exchange(gathered, [_sds(a.shape, a.dtype) for a in gathered], [dma((n, 3)), dma((n, 3))], start, finish,
                     aliases={a: a for a in range(n)})


def _pair_exchange(arrs):
    n = len(arrs)

    def copies(ins, outs, sems):
        x, y, c, _ = _place()
        return [_remote(ins[a].at[:, 1 - c], outs[a], sems[0].at[a], sems[1].at[a], (x, y, 1 - c)) for a in range(n)]

    def start(ins, outs, sems):
        for cp in copies(ins, outs, sems):
            cp.start()

    def finish(ins, outs, sems):
        for cp in copies(ins, outs, sems):
            cp.wait()

    dma = pltpu.SemaphoreType.DMA
    return _Exchange(arrs, [_sds((4,) + a.shape[2:], a.dtype) for a in arrs], [dma((n,)), dma((n,))], start, finish)


def _chip_exchange(arrs):
    n = len(arrs)

    def copies(ins, outs, sems):
        x, y, c, chips = _place()
        return [_remote(ins[a].at[1 + j], outs[a].at[j], sems[0].at[a, j], sems[1].at[a, j], (*chip, c))
                for a in range(n) for j, chip in enumerate(chips)]

    def start(ins, outs, sems):
        for cp in copies(ins, outs, sems):
            cp.start()

    def finish(ins, outs, sems):
        for cp in copies(ins, outs, sems):
            cp.wait()

    dma = pltpu.SemaphoreType.DMA
    return _Exchange(arrs, [_sds((3,) + a.shape[1:], a.dtype) for a in arrs], [dma((n, 3)), dma((n, 3))], start, finish)


def _gather_rows(vec, name, comm=()):
    def body(in_ref, out_ref, send_sems, recv_sems, local_sem):
        x, y, c, _ = _place()
        me = 4 * x + 2 * y + c
        mine = pltpu.make_async_copy(in_ref, out_ref.at[me], local_sem)
        mine.start()
        copies = []
        for k in range(1, N_DEV):
            to = (1 - x if k & 4 else x, 1 - y if k & 2 else y, 1 - c if k & 1 else c)
            copies.append(pltpu.make_async_remote_copy(
                src_ref=in_ref, dst_ref=out_ref.at[me], send_sem=send_sems.at[k - 1], recv_sem=recv_sems.at[k - 1],
                device_id=to, device_id_type=MESH))
        for cp in copies:
            cp.start()
        for cp in copies:
            cp.wait()
        mine.wait()

    return _call(
        body, name=name, out_shape=_sds((N_DEV,) + vec.shape, vec.dtype), in_specs=[ANY], out_specs=ANY,
        scratch_shapes=[pltpu.SemaphoreType.DMA((7,)), pltpu.SemaphoreType.DMA((7,)), pltpu.SemaphoreType.DMA],
    )(vec)


def _for_row_groups(tm, fn):
    def step(i, carry):
        fn(pl.ds(pl.multiple_of(i * SUB_ROWS, SUB_ROWS), SUB_ROWS), i * SUB_ROWS)
        return carry

    lax.fori_loop(0, tm // SUB_ROWS, step, 0)


def _row_parts(tm, n=4):
    units = tm // BF16_ROWS
    sizes = [(units // n + (1 if i < units % n else 0)) * BF16_ROWS for i in range(n)]
    starts = [sum(sizes[:i]) for i in range(n)]
    return [slice(s, s + z) for s, z in zip(starts, sizes) if z]


def _once(shape, index_map):
    return pl.BlockSpec(shape, index_map, pipeline_mode=pl.Buffered(1))


def _rms_fwd(hs, g, tm, name, comm=()):
    rows, d = hs.shape

    def body(hs_ref, g_ref, h_ref, r_ref):
        def group(rs, _):
            xv = hs_ref[rs, :]
            r = lax.rsqrt(jnp.mean(xv * xv, axis=-1, keepdims=True) + EPS)
            h_ref[rs, :] = (xv * r * g_ref[...]).astype(BF16)
            r_ref[rs, :] = r

        _for_row_groups(tm, group)

    return _call(
        body, comm=comm, name=name, grid=(rows // tm,),
        in_specs=[pl.BlockSpec((tm, d), lambda i: (i, 0)), pl.BlockSpec((1, d), lambda i: (0, 0))],
        out_specs=[pl.BlockSpec((tm, d), lambda i: (i, 0)), pl.BlockSpec((tm, 1), lambda i: (i, 0))],
        out_shape=[_sds((rows, d), BF16), _sds((rows, 1), F32)], compiler_params=_params(),
    )(hs, g)


def _ffn_gu(h, wg, wu, tm, name, comm=()):
    rows, d = h.shape
    nb, fb, _ = wg.shape

    def body(h_ref, wg_ref, wu_ref, g_ref, u_ref):
        hv = h_ref[...]
        g_ref[0] = lax.dot_general(hv, wg_ref[0], NT, preferred_element_type=F32).astype(BF16)
        u_ref[0] = lax.dot_general(hv, wu_ref[0], NT, preferred_element_type=F32).astype(BF16)

    wspec = pl.BlockSpec((1, fb, d), lambda j, i: (j, 0, 0))
    ospec = pl.BlockSpec((1, tm, fb), lambda j, i: (j, i, 0))
    return _call(
        body, comm=comm, name=name, grid=(nb, rows // tm),
        in_specs=[pl.BlockSpec((tm, d), lambda j, i: (i, 0)), wspec, wspec],
        out_specs=[ospec, ospec], out_shape=[_sds((nb, rows, fb), BF16)] * 2, compiler_params=_params(),
    )(h, wg, wu)


def _ffn_down_norm(g, u, wd, hs, gn, tm, name, comm=()):
    nb, rows, fb = g.shape
    d = hs.shape[1]

    def body(g_ref, u_ref, wd_ref, hs_ref, gn_ref, hsn_ref, hn_ref, rn_ref, acc_ref):
        j = pl.program_id(1)

        @pl.when(j == 0)
        def _():
            acc_ref[...] = jnp.zeros_like(acc_ref)

        gv = g_ref[0]
        a = gv * _sigmoid(gv) * u_ref[0]
        acc_ref[...] += jnp.dot(a, wd_ref[0], preferred_element_type=F32)

        @pl.when(j == nb - 1)
        def _():
            def group(rs, _):
                hsn = hs_ref[rs, :] + FFN_RES_SCALE * acc_ref[rs, :]
                r = lax.rsqrt(jnp.mean(hsn * hsn, axis=-1, keepdims=True) + EPS)
                hsn_ref[rs, :] = hsn
                hn_ref[rs, :] = (hsn * r * gn_ref[...]).astype(BF16)
                rn_ref[rs, :] = r

            _for_row_groups(tm, group)

    aspec = pl.BlockSpec((1, tm, fb), lambda i, j: (j, i, 0))
    row = pl.BlockSpec((tm, d), lambda i, j: (i, 0))
    return _call(
        body, comm=comm, name=name, grid=(rows // tm, nb),
        in_specs=[aspec, aspec, pl.BlockSpec((1, fb, d), lambda i, j: (j, 0, 0)), _once((tm, d), lambda i, j: (i, 0)),
                  pl.BlockSpec((1, d), lambda i, j: (0, 0))],
        out_specs=[row, row, pl.BlockSpec((tm, 1), lambda i, j: (i, 0))],
        out_shape=[_sds((rows, d), F32), _sds((rows, d), BF16), _sds((rows, 1), F32)],
        scratch_shapes=[pltpu.VMEM((tm, d), F32)], compiler_params=_params(),
    )(g, u, wd, hs, gn)


def _ffn_down_loss(g, u, wd, hs, gf, tgt, n_seq, tm, name, comm=()):
    nb, rows, fb = g.shape
    d = hs.shape[1]
    nt = rows // tm

    def body(g_ref, u_ref, wd_ref, hs_ref, gf_ref, tgt_ref, dhs_ref, dhsb_ref, loss_ref, dgf_ref, acc_ref):
        i, j = pl.program_id(0), pl.program_id(1)

        @pl.when(j == 0)
        def _():
            acc_ref[...] = jnp.zeros_like(acc_ref)

        gv = g_ref[0]
        a = gv * _sigmoid(gv) * u_ref[0]
        acc_ref[...] += jnp.dot(a, wd_ref[0], preferred_element_type=F32)

        @pl.when(j == nb - 1)
        def _():
            loss_ref[0] = jnp.zeros((1, d), F32)
            dgf_ref[0] = jnp.zeros((1, d), F32)

            def group(rs, r0):
                hs3 = hs_ref[rs, :] + FFN_RES_SCALE * acc_ref[rs, :]
                r = lax.rsqrt(jnp.mean(hs3 * hs3, axis=-1, keepdims=True) + EPS)
                gfv = gf_ref[...]
                y = hs3 * r
                rowid = i * tm + r0 + lax.broadcasted_iota(jnp.int32, (SUB_ROWS, 1), 0)
                err = jnp.where(rowid < n_seq, y * gfv - tgt_ref[rs, :], 0.0)
                loss_ref[0] += jnp.sum(err * err, axis=0, keepdims=True)
                dout = err * (1.0 / d)
                dgf_ref[0] += jnp.sum(dout * y, axis=0, keepdims=True)
                t = dout * gfv
                dhs = r * t - hs3 * (r * r * r) * jnp.mean(t * hs3, axis=-1, keepdims=True)
                dhs_ref[rs, :] = dhs
                dhsb_ref[rs, :] = dhs.astype(BF16)

            _for_row_groups(tm, group)

    aspec = pl.BlockSpec((1, tm, fb), lambda i, j: (j, i, 0))
    row = pl.BlockSpec((tm, d), lambda i, j: (i, 0))
    once = _once((tm, d), lambda i, j: (i, 0))
    part = pl.BlockSpec((1, 1, d), lambda i, j: (i, 0, 0))
    return _call(
        body, comm=comm, name=name, grid=(nt, nb),
        in_specs=[aspec, aspec, pl.BlockSpec((1, fb, d), lambda i, j: (j, 0, 0)), once,
                  pl.BlockSpec((1, d), lambda i, j: (0, 0)), once],
        out_specs=[row, row, part, part],
        out_shape=[_sds((rows, d), F32), _sds((rows, d), BF16), _sds((nt, 1, d), F32), _sds((nt, 1, d), F32)],
        scratch_shapes=[pltpu.VMEM((tm, d), F32)], compiler_params=_params(),
    )(g, u, wd, hs, gf, tgt)


def _ffn_bwd_da(do, wd, g, u, tm, name, comm=()):
    nb, rows, fb = g.shape
    d = do.shape[1]

    def body(do_ref, wd_ref, g_ref, u_ref, dg_ref, du_ref, a_ref):
        da = FFN_RES_SCALE * lax.dot_general(do_ref[...], wd_ref[0], NT, preferred_element_type=F32)
        gv = g_ref[0].astype(F32)
        uv = u_ref[0].astype(F32)
        s = _sigmoid(gv)
        sg = gv * s
        a_ref[0] = (sg * uv).astype(BF16)
        du_ref[0] = (da * sg).astype(BF16)
        dg_ref[0] = (da * uv * (s * (1.0 + gv * (1.0 - s)))).astype(BF16)

    aspec = pl.BlockSpec((1, tm, fb), lambda j, i: (j, i, 0))
    return _call(
        body, comm=comm, name=name, grid=(nb, rows // tm),
        in_specs=[pl.BlockSpec((tm, d), lambda j, i: (i, 0)), pl.BlockSpec((1, fb, d), lambda j, i: (j, 0, 0)), aspec, aspec],
        out_specs=[aspec] * 3, out_shape=[_sds((nb, rows, fb), BF16)] * 3, compiler_params=_params(),
    )(do, wd, g, u)


def _rms_bwd(tm, d, dh_ref, hs_ref, r_ref, gn_ref, dres_ref, dhs_ref, dhsb_ref, dgn_ref):
    dgn_ref[0] = jnp.zeros((1, d), F32)

    def group(rs, _):
        dh, hs, r = dh_ref[rs, :], hs_ref[rs, :], r_ref[rs, :]
        dgn_ref[0] += jnp.sum(dh * (hs * r), axis=0, keepdims=True)
        t = dh * gn_ref[...]
        dhs = dres_ref[rs, :] + r * t - hs * (r * r * r) * jnp.mean(t * hs, axis=-1, keepdims=True)
        dhs_ref[rs, :] = dhs
        dhsb_ref[rs, :] = dhs.astype(BF16)

    _for_row_groups(tm, group)


def _ffn_bwd_dh(dg, du, wg, wu, hs, r, gn, dres, tm, name, comm=()):
    nb, rows, fb = dg.shape
    d = hs.shape[1]
    nt = rows // tm

    def body(dg_ref, du_ref, wg_ref, wu_ref, hs_ref, r_ref, gn_ref, dres_ref, dhs_ref, dhsb_ref, dgn_ref, acc_ref):
        j = pl.program_id(1)

        @pl.when(j == 0)
        def _():
            acc_ref[...] = jnp.zeros_like(acc_ref)

        acc_ref[...] += (jnp.dot(dg_ref[0], wg_ref[0], preferred_element_type=F32)
                         + jnp.dot(du_ref[0], wu_ref[0], preferred_element_type=F32))

        @pl.when(j == nb - 1)
        def _():
            _rms_bwd(tm, d, acc_ref, hs_ref, r_ref, gn_ref, dres_ref, dhs_ref, dhsb_ref, dgn_ref)

    aspec = pl.BlockSpec((1, tm, fb), lambda i, j: (j, i, 0))
    wspec = pl.BlockSpec((1, fb, d), lambda i, j: (j, 0, 0))
    row = pl.BlockSpec((tm, d), lambda i, j: (i, 0))
    once = _once((tm, d), lambda i, j: (i, 0))
    return _call(
        body, comm=comm, name=name, grid=(nt, nb),
        in_specs=[aspec, aspec, wspec, wspec, once, pl.BlockSpec((tm, 1), lambda i, j: (i, 0)),
                  pl.BlockSpec((1, d), lambda i, j: (0, 0)), once],
        out_specs=[row, row, pl.BlockSpec((1, 1, d), lambda i, j: (i, 0, 0))],
        out_shape=[_sds((rows, d), F32), _sds((rows, d), BF16), _sds((nt, 1, d), F32)],
        scratch_shapes=[pltpu.VMEM((tm, d), F32)], compiler_params=_params(),
    )(dg, du, wg, wu, hs, r, gn, dres)


def _ffn_bwd_dw(h, dg, du, a, do, tk, name, comm=()):
    nb, rows, fb = dg.shape
    d = h.shape[1]
    nk = rows // tk

    def body(h_ref, dg_ref, du_ref, a_ref, do_ref, dwg_ref, dwu_ref, dwd_ref, accg, accu, accd):
        k = pl.program_id(1)

        @pl.when(k == 0)
        def _():
            accg[...] = jnp.zeros_like(accg)
            accu[...] = jnp.zeros_like(accu)
            accd[...] = jnp.zeros_like(accd)

        hv = h_ref[...]
        accg[...] += lax.dot_general(dg_ref[0], hv, TN, preferred_element_type=F32)
        accu[...] += lax.dot_general(du_ref[0], hv, TN, preferred_element_type=F32)
        accd[...] += lax.dot_general(a_ref[0], do_ref[...], TN, preferred_element_type=F32)

        @pl.when(k == nk - 1)
        def _():
            dwg_ref[0] = accg[...].astype(BF16)
            dwu_ref[0] = accu[...].astype(BF16)
            dwd_ref[0] = (FFN_RES_SCALE * accd[...]).astype(BF16)

    aspec = pl.BlockSpec((1, tk, fb), lambda j, k: (j, k, 0))
    row = pl.BlockSpec((tk, d), lambda j, k: (k, 0))
    down = pl.BlockSpec((1, fb, d), lambda j, k: (j, 0, 0))
    return _call(
        body, comm=comm, name=name, grid=(nb, nk),
        in_specs=[row, aspec, aspec, aspec, row], out_specs=[down] * 3,
        out_shape=[_sds((nb, fb, d), BF16)] * 3, scratch_shapes=[pltpu.VMEM((fb, d), F32)] * 3,
        compiler_params=_params(),
    )(h, dg, du, a, do)


def _win_fwd(h, w, b, dc, tm, name, comm=()):
    rows, d = h.shape
    ng = w.shape[1] // dc

    def body(h_ref, w_ref, b_ref, u_ref):
        u_ref[...] = (jnp.dot(h_ref[...], w_ref[...], preferred_element_type=F32) + b_ref[...]).astype(BF16)

    return _call(
        body, comm=comm, name=name, grid=(ng, rows // tm),
        in_specs=[pl.BlockSpec((tm, d), lambda m, i: (i, 0)), pl.BlockSpec((d, dc), lambda m, i: (0, m)),
                  pl.BlockSpec((1, dc), lambda m, i: (0, m))],
        out_specs=pl.BlockSpec((tm, dc), lambda m, i: (i, m)), out_shape=_sds((rows, ng * dc), BF16),
        compiler_params=_params(),
    )(h, w, b)


def _win_bwd_dh(du, w, hs, r, gn, dres, tm, name, comm=()):
    rows, d = hs.shape
    ng, _, dc = du.shape
    nt = rows // tm

    def body(du_ref, w_ref, hs_ref, r_ref, gn_ref, dres_ref, dhs_ref, dhsb_ref, dgn_ref, acc_ref):
        m = pl.program_id(1)

        @pl.when(m == 0)
        def _():
            acc_ref[...] = jnp.zeros_like(acc_ref)

        acc_ref[...] += lax.dot_general(du_ref[0], w_ref[...], NT, preferred_element_type=F32)

        @pl.when(m == ng - 1)
        def _():
            _rms_bwd(tm, d, acc_ref, hs_ref, r_ref, gn_ref, dres_ref, dhs_ref, dhsb_ref, dgn_ref)

    row = pl.BlockSpec((tm, d), lambda i, m: (i, 0))
    once = _once((tm, d), lambda i, m: (i, 0))
    return _call(
        body, comm=comm, name=name, grid=(nt, ng),
        in_specs=[pl.BlockSpec((1, tm, dc), lambda i, m: (m, i, 0)), pl.BlockSpec((d, dc), lambda i, m: (0, m)), once,
                  pl.BlockSpec((tm, 1), lambda i, m: (i, 0)), pl.BlockSpec((1, d), lambda i, m: (0, 0)), once],
        out_specs=[row, row, pl.BlockSpec((1, 1, d), lambda i, m: (i, 0, 0))],
        out_shape=[_sds((rows, d), F32), _sds((rows, d), BF16), _sds((nt, 1, d), F32)],
        scratch_shapes=[pltpu.VMEM((tm, d), F32)], compiler_params=_params(),
    )(du, w, hs, r, gn, dres)


def _win_bwd_dw(h, du, tk, name, comm=()):
    rows, d = h.shape
    ng, _, dc = du.shape
    nk = rows // tk

    def body(h_ref, du_ref, dw_ref, db_ref, acc, accb):
        k = pl.program_id(1)

        @pl.when(k == 0)
        def _():
            acc[...] = jnp.zeros_like(acc)
            accb[...] = jnp.zeros_like(accb)

        duv = du_ref[0]
        acc[...] += lax.dot_general(h_ref[...], duv, TN, preferred_element_type=F32)
        accb[...] += jnp.sum(duv.astype(F32), axis=0, keepdims=True)

        @pl.when(k == nk - 1)
        def _():
            dw_ref[...] = acc[...].astype(BF16)
            db_ref[...] = accb[...]

    return _call(
        body, comm=comm, name=name, grid=(ng, nk),
        in_specs=[pl.BlockSpec((tk, d), lambda m, k: (k, 0)), pl.BlockSpec((1, tk, dc), lambda m, k: (m, k, 0))],
        out_specs=[pl.BlockSpec((d, dc), lambda m, k: (0, m)), pl.BlockSpec((1, dc), lambda m, k: (0, m))],
        out_shape=[_sds((d, ng * dc), BF16), _sds((1, ng * dc), F32)],
        scratch_shapes=[pltpu.VMEM((d, dc), F32), pltpu.VMEM((1, dc), F32)], compiler_params=_params(),
    )(h, du)


def _layernorm_silu(zc, lg, lb):
    mu = jnp.mean(zc, axis=-1, keepdims=True)
    xc = zc - mu
    rstd = lax.rsqrt(jnp.mean(xc * xc, axis=-1, keepdims=True) + EPS)
    nrm = xc * rstd
    lin = nrm * lg + lb
    s = _sigmoid(lin)
    return nrm, rstd, lin, s


def _wout_fwd(zc, ysc, wout, hs, lg, lb, gn, tm, name, comm=()):
    rows, dc = zc.shape
    d = hs.shape[1]

    def body(zc_ref, ysc_ref, w_ref, hs_ref, lg_ref, lb_ref, gn_ref, y_ref, hsn_ref, hn_ref, rn_ref):
        def mix(rs, _):
            _, _, lin, s = _layernorm_silu(zc_ref[rs, :], lg_ref[...], lb_ref[...])
            y_ref[rs, :dc] = ysc_ref[rs, :]
            y_ref[rs, dc:] = (lin * s).astype(BF16)

        _for_row_groups(tm, mix)
        hsn_ref[...] = jnp.dot(y_ref[...], w_ref[...], preferred_element_type=F32)

        def norm(rs, _):
            hsn = hs_ref[rs, :] + hsn_ref[rs, :]
            r = lax.rsqrt(jnp.mean(hsn * hsn, axis=-1, keepdims=True) + EPS)
            hsn_ref[rs, :] = hsn
            hn_ref[rs, :] = (hsn * r * gn_ref[...]).astype(BF16)
            rn_ref[rs, :] = r

        _for_row_groups(tm, norm)

    half = pl.BlockSpec((tm, dc), lambda i: (i, 0))
    row = pl.BlockSpec((tm, d), lambda i: (i, 0))
    vec_c = pl.BlockSpec((1, dc), lambda i: (0, 0))
    return _call(
        body, comm=comm, name=name, grid=(rows // tm,),
        in_specs=[half, half, _once((2 * dc, d), lambda i: (0, 0)), row, vec_c, vec_c,
                  pl.BlockSpec((1, d), lambda i: (0, 0))],
        out_specs=[pl.BlockSpec((tm, 2 * dc), lambda i: (i, 0)), row, row, pl.BlockSpec((tm, 1), lambda i: (i, 0))],
        out_shape=[_sds((rows, 2 * dc), BF16), _sds((rows, d), F32), _sds((rows, d), BF16), _sds((rows, 1), F32)],
        compiler_params=_params(),
    )(zc, ysc, wout, hs, lg, lb, gn)


def _wout_bwd_dy(do, wout, zc, lg, lb, tm, name, comm=()):
    rows, dc = zc.shape
    d = do.shape[1]
    nt = rows // tm

    def body(do_ref, w_ref, zc_ref, lg_ref, lb_ref, dysc_ref, dzc_ref, dlg_ref, dlb_ref, dy_ref):
        dy_ref[...] = lax.dot_general(do_ref[...], w_ref[...], NT, preferred_element_type=F32)
        dlb_ref[0] = jnp.zeros((1, dc), F32)
        dlg_ref[0] = jnp.zeros((1, dc), F32)

        def group(rs, _):
            dysc_ref[rs, :] = dy_ref[rs, :dc].astype(BF16)
            nrm, rstd, lin, s = _layernorm_silu(zc_ref[rs, :], lg_ref[...], lb_ref[...])
            dl = dy_ref[rs, dc:] * (s * (1.0 + lin * (1.0 - s)))
            dlb_ref[0] += jnp.sum(dl, axis=0, keepdims=True)
            dlg_ref[0] += jnp.sum(dl * nrm, axis=0, keepdims=True)
            dn = dl * lg_ref[...]
            dzc_ref[rs, :] = rstd * (dn - jnp.mean(dn, axis=-1, keepdims=True)
                                     - nrm * jnp.mean(dn * nrm, axis=-1, keepdims=True))

        _for_row_groups(tm, group)

    half = pl.BlockSpec((tm, dc), lambda i: (i, 0))
    vec_c = pl.BlockSpec((1, dc), lambda i: (0, 0))
    part = pl.BlockSpec((1, 1, dc), lambda i: (i, 0, 0))
    return _call(
        body, comm=comm, name=name, grid=(nt,),
        in_specs=[pl.BlockSpec((tm, d), lambda i: (i, 0)), _once((2 * dc, d), lambda i: (0, 0)), half, vec_c, vec_c],
        out_specs=[half, half, part, part],
        out_shape=[_sds((rows, dc), BF16), _sds((rows, dc), F32), _sds((nt, 1, dc), F32), _sds((nt, 1, dc), F32)],
        scratch_shapes=[pltpu.VMEM((tm, 2 * dc), F32)], compiler_params=_params(),
    )(do, wout, zc, lg, lb)


def _wout_bwd_dw(y, do, nb, tk, name, comm=()):
    rows, k2 = y.shape
    d = do.shape[1]
    nk = rows // tk

    def body(y_ref, do_ref, dw_ref, acc):
        k = pl.program_id(0)

        @pl.when(k == 0)
        def _():
            acc[...] = jnp.zeros_like(acc)

        acc[...] += lax.dot_general(y_ref[...], do_ref[...], TN, preferred_element_type=F32)

        @pl.when(k == nk - 1)
        def _():
            dw_ref[...] = acc[...].astype(BF16)

    return _call(
        body, comm=comm, name=name, grid=(nk,),
        in_specs=[pl.BlockSpec((tk, k2), lambda k: (k, 0)), pl.BlockSpec((tk, d), lambda k: (k, 0))],
        out_specs=_once((k2, d), lambda k: (0, 0)), out_shape=_sds((k2, d), BF16),
        scratch_shapes=[pltpu.VMEM((k2, d), F32)], compiler_params=_params(),
    )(y, do).reshape(nb, k2 // nb, d)


def _windows(win, n_res):
    length = win.shape[0]
    return [win if r == 0 else pltpu.roll(win, length - r, 0) for r in range(n_res)]


def _taps(src_ref, start, offsets, ch):
    span = -(-(max(offsets) + ch) // SUBLANES) * SUBLANES
    win = src_ref[pl.ds(start, span), :]
    shifted = _windows(win, min(SUBLANES, max(offsets) + 1))
    return [shifted[o % SUBLANES][(o // SUBLANES) * SUBLANES:(o // SUBLANES) * SUBLANES + ch] for o in offsets]


def _rows8(v):
    return jnp.sum(v.reshape(v.shape[0] // SUBLANES, SUBLANES, v.shape[1]), axis=0)


def _conv_geometry(n_seq, n_meta):
    base = CONV_PAD + n_meta
    off_cf = base - (CF_WIDTH - 1)
    off_sc = base - (SC_WIDTH - 1)
    logical = -(-(n_meta + n_seq) // CONV_CH) * CONV_CH
    return base, off_cf, off_sc, logical


def _fill_conv_inputs(c_ref, v_ref, a_ref, g_ref, scv, sz, n_seq, n_meta):
    base = CONV_PAD + n_meta
    cb = scv.shape[1]
    scv[0:CONV_PAD, :] = jnp.zeros((CONV_PAD, cb), F32)
    sz[0:CONV_PAD, :] = jnp.zeros((CONV_PAD, cb), F32)

    def put(src, dst, n):
        cv = c_ref[src, :].astype(F32) * v_ref[src, :].astype(F32)
        scv[dst, :] = cv
        sz[dst, :] = a_ref[src, :].astype(F32) * _sigmoid(g_ref[src, :].astype(F32))

    put(pl.ds(n_seq, n_meta), pl.ds(CONV_PAD, n_meta), n_meta)

    def chunk(i, carry):
        t0 = pl.multiple_of(i * CONV_CH, CONV_CH)
        put(pl.ds(t0, CONV_CH), pl.ds(base + t0, CONV_CH), CONV_CH)
        return carry

    lax.fori_loop(0, n_seq // CONV_CH, chunk, 0)


def _conv_fwd(u, wsc, wcf, cbias, n_seq, n_meta, name, comm=()):
    rows = u.shape[0]
    nb, _, cb = wsc.shape
    dc = nb * cb
    base, off_cf, off_sc, _ = _conv_geometry(n_seq, n_meta)
    a_cf, a_sc = off_cf // SUBLANES * SUBLANES, off_sc // SUBLANES * SUBLANES
    ch = CONV_CH

    def body(b_ref, c_ref, v_ref, a_ref, g_ref, wsc_ref, wcf_ref, cb_ref, ysc_ref, zc_ref, scv, sz):
        _fill_conv_inputs(c_ref, v_ref, a_ref, g_ref, scv, sz, n_seq, n_meta)
        w3, w31, bias = wsc_ref[0], wcf_ref[0], cb_ref[...]

        def chunk(i, carry):
            t0 = pl.multiple_of(i * ch, ch)
            acc = jnp.zeros((ch, cb), F32)
            for k, win in enumerate(_taps(sz, t0 + a_cf, [off_cf - a_cf + k for k in range(CF_WIDTH)], ch)):
                acc = acc + win * w31[k:k + 1, :]
            zc_ref[pl.ds(t0, ch), :] = acc + bias
            s = jnp.zeros((ch, cb), F32)
            for k, win in enumerate(_taps(scv, t0 + a_sc, [off_sc - a_sc + k for k in range(SC_WIDTH)], ch)):
                s = s + win * w3[k:k + 1, :]
            ysc_ref[pl.ds(t0, ch), :] = (b_ref[pl.ds(t0, ch), :].astype(F32) * s).astype(BF16)
            return carry

        lax.fori_loop(0, n_seq // ch, chunk, 0)
        ysc_ref[n_seq:rows, :] = jnp.zeros((rows - n_seq, cb), BF16)
        zc_ref[n_seq:rows, :] = jnp.zeros((rows - n_seq, cb), F32)

    ucol = [pl.BlockSpec((rows, cb), functools.partial(lambda m, j: (0, m * nb + j), m)) for m in range(5)]
    blk = pl.BlockSpec((rows, cb), lambda j: (0, j))
    return _call(
        body, comm=comm, name=name, grid=(nb,),
        in_specs=ucol + [pl.BlockSpec((1, SC_WIDTH, cb), lambda j: (j, 0, 0)),
                         pl.BlockSpec((1, CF_WIDTH, cb), lambda j: (j, 0, 0)), pl.BlockSpec((1, cb), lambda j: (0, j))],
        out_specs=[blk, blk], out_shape=[_sds((rows, dc), BF16), _sds((rows, dc), F32)],
        scratch_shapes=[pltpu.VMEM((base + n_seq, cb), F32)] * 2, compiler_params=_params(),
    )(u, u, u, u, u, wsc, wcf, cbias)


def _conv_bwd(u, dysc, dzc, wsc, wcf, n_seq, n_meta, name, comm=()):
    rows = u.shape[0]
    nb, _, cb = wsc.shape
    dc = nb * cb
    base, off_cf, off_sc, logical = _conv_geometry(n_seq, n_meta)
    a_cf, a_sc = off_cf // SUBLANES * SUBLANES, off_sc // SUBLANES * SUBLANES
    ch = CONV_CH
    tail = CONV_PAD

    def body(b_ref, c_ref, v_ref, a_ref, g_ref, dysc_ref, dzc_ref, wsc_ref, wcf_ref,
             du_ref, dwsc_ref, dwcf_ref, dcb_ref,
             scv, sz, sds_, sdz, dlcv, dlz, accsc, acccf, accb):
        _fill_conv_inputs(c_ref, v_ref, a_ref, g_ref, scv, sz, n_seq, n_meta)
        w3, w31 = wsc_ref[0], wcf_ref[0]
        dub_ref, duc_ref, duv_ref, dua_ref, dug_ref = (du_ref.at[m] for m in range(5))
        sds_[0:n_meta, :] = jnp.zeros((n_meta, cb), F32)
        sdz[0:n_meta, :] = jnp.zeros((n_meta, cb), F32)
        behind = logical + tail - (n_meta + n_seq)
        sds_[n_meta + n_seq:logical + tail, :] = jnp.zeros((behind, cb), F32)
        sdz[n_meta + n_seq:logical + tail, :] = jnp.zeros((behind, cb), F32)
        accsc[...] = jnp.zeros_like(accsc)
        acccf[...] = jnp.zeros_like(acccf)
        accb[...] = jnp.zeros_like(accb)

        def forward_chunk(i, carry):
            t0 = pl.multiple_of(i * ch, ch)
            rws = pl.ds(t0, ch)
            dy = dysc_ref[rws, :].astype(F32)
            ds = dy * b_ref[rws, :].astype(F32)
            dz = dzc_ref[rws, :]
            sds_[pl.ds(n_meta + t0, ch), :] = ds
            sdz[pl.ds(n_meta + t0, ch), :] = dz
            s = jnp.zeros((ch, cb), F32)
            for k, win in enumerate(_taps(scv, t0 + a_sc, [off_sc - a_sc + k for k in range(SC_WIDTH)], ch)):
                s = s + win * w3[k:k + 1, :]
                accsc[k * SUBLANES:(k + 1) * SUBLANES, :] += _rows8(ds * win)
            dub_ref[rws, :] = (dy * s).astype(BF16)
            for k, win in enumerate(_taps(sz, t0 + a_cf, [off_cf - a_cf + k for k in range(CF_WIDTH)], ch)):
                acccf[k * SUBLANES:(k + 1) * SUBLANES, :] += _rows8(dz * win)
            accb[...] += _rows8(dz)
            return carry

        lax.fori_loop(0, n_seq // ch, forward_chunk, 0)

        def backward_chunk(i, carry):
            p0 = pl.multiple_of(i * ch, ch)
            dcv = jnp.zeros((ch, cb), F32)
            for k, win in enumerate(_taps(sds_, p0, [SC_WIDTH - 1 - k for k in range(SC_WIDTH)], ch)):
                dcv = dcv + win * w3[k:k + 1, :]
            dlcv[pl.ds(p0, ch), :] = dcv
            dzi = jnp.zeros((ch, cb), F32)
            for k, win in enumerate(_taps(sdz, p0, [CF_WIDTH - 1 - k for k in range(CF_WIDTH)], ch)):
                dzi = dzi + win * w31[k:k + 1, :]
            dlz[pl.ds(p0, ch), :] = dzi
            return carry

        lax.fori_loop(0, logical // ch, backward_chunk, 0)

        def gates(phys, logi):
            dcv, dzi = dlcv[logi, :], dlz[logi, :]
            duc_ref[phys, :] = (dcv * v_ref[phys, :].astype(F32)).astype(BF16)
            duv_ref[phys, :] = (dcv * c_ref[phys, :].astype(F32)).astype(BF16)
            s = _sigmoid(g_ref[phys, :].astype(F32))
            dua_ref[phys, :] = (dzi * s).astype(BF16)
            dug_ref[phys, :] = (dzi * a_ref[phys, :].astype(F32) * s * (1.0 - s)).astype(BF16)

        def gate_chunk(i, carry):
            t0 = pl.multiple_of(i * ch, ch)
            gates(pl.ds(t0, ch), pl.ds(n_meta + t0, ch))
            return carry

        lax.fori_loop(0, n_seq // ch, gate_chunk, 0)
        gates(pl.ds(n_seq, n_meta), pl.ds(0, n_meta))
        dub_ref[n_seq:rows, :] = jnp.zeros((rows - n_seq, cb), BF16)
        pad0 = n_seq + n_meta
        for ref in (duc_ref, duv_ref, dua_ref, dug_ref):
            ref[pad0:rows, :] = jnp.zeros((rows - pad0, cb), BF16)
        dwsc_ref[0] = jnp.sum(accsc[...].reshape(SC_WIDTH, SUBLANES, cb), axis=1)
        dwcf_ref[0] = jnp.sum(acccf[...].reshape(CF_WIDTH, SUBLANES, cb), axis=1)
        dcb_ref[...] = jnp.sum(accb[...], axis=0, keepdims=True)

    ucol = [pl.BlockSpec((rows, cb), functools.partial(lambda m, j: (0, m * nb + j), m)) for m in range(5)]
    blk = pl.BlockSpec((rows, cb), lambda j: (0, j))
    wsc_spec = pl.BlockSpec((1, SC_WIDTH, cb), lambda j: (j, 0, 0))
    wcf_spec = pl.BlockSpec((1, CF_WIDTH, cb), lambda j: (j, 0, 0))
    outs = _call(
        body, comm=comm, name=name, grid=(nb,),
        in_specs=ucol + [blk, blk, wsc_spec, wcf_spec],
        out_specs=[pl.BlockSpec((5, rows, cb), lambda j: (0, 0, j)), wsc_spec, wcf_spec,
                   pl.BlockSpec((1, cb), lambda j: (0, j))],
        out_shape=[_sds((5, rows, dc), BF16), _sds((nb, SC_WIDTH, cb), F32), _sds((nb, CF_WIDTH, cb), F32),
                   _sds((1, dc), F32)],
        scratch_shapes=[pltpu.VMEM((base + n_seq, cb), F32)] * 2 + [pltpu.VMEM((logical + tail, cb), F32)] * 2
        + [pltpu.VMEM((logical, cb), F32)] * 2
        + [pltpu.VMEM((SC_WIDTH * SUBLANES, cb), F32), pltpu.VMEM((CF_WIDTH * SUBLANES, cb), F32),
           pltpu.VMEM((SUBLANES, cb), F32)],
        compiler_params=_params(),
    )(u, u, u, u, u, dysc, dzc, wsc, wcf)
    return outs


def _row_tile(rows, cols):
    return rows // 4 if rows % 64 == 0 and rows * cols >= (1 << 18) else rows


def _pair_sum(grad, sib, idx, name, comm=()):
    _, rows, cols = grad.shape
    tr = _row_tile(rows, cols)

    def body(idx_ref, g_ref, s_ref, o_ref):
        o_ref[0] = (g_ref[0].astype(F32) + s_ref[0].astype(F32)).astype(o_ref.dtype)

    return _call(
        body, name=name,
        grid_spec=pltpu.PrefetchScalarGridSpec(
            num_scalar_prefetch=1, grid=(4, rows // tr),
            in_specs=[pl.BlockSpec((1, tr, cols), lambda k, i, idx_ref: (idx_ref[k], i, 0)),
                      pl.BlockSpec((1, tr, cols), lambda k, i, idx_ref: (idx_ref[4 + k], i, 0))],
            out_specs=pl.BlockSpec((1, tr, cols), lambda k, i, idx_ref: (k, i, 0))),
        out_shape=_sds((4, rows, cols), grad.dtype), compiler_params=_params(),
    )(idx, grad, sib)


def _adamw_math(w, g, m, v):
    m = ADAM_B1 * m + (1.0 - ADAM_B1) * g
    v = ADAM_B2 * v + (1.0 - ADAM_B2) * (g * g)
    m_hat = m / (1.0 - ADAM_B1 ** ADAM_STEP)
    v_hat = v / (1.0 - ADAM_B2 ** ADAM_STEP)
    delta = -ADAM_LR * (m_hat / (jnp.sqrt(v_hat) + ADAM_EPS) + ADAM_WD * w)
    return delta, m, v


def _adamw_sharded(own, got, w, m, v, name, comm=()):
    rows, cols = w.shape
    tr = _row_tile(rows, cols)

    def body(own_ref, g0_ref, g1_ref, g2_ref, w_ref, m_ref, v_ref, g_ref, d_ref, nm_ref, nv_ref):
        g = own_ref[0].astype(F32) + g0_ref[0].astype(F32) + g1_ref[0].astype(F32) + g2_ref[0].astype(F32)
        delta, nm, nv = _adamw_math(w_ref[...], g, m_ref[...], v_ref[...])
        g_ref[...] = g
        d_ref[...] = delta
        nm_ref[...] = nm
        nv_ref[...] = nv

    flat = pl.BlockSpec((tr, cols), lambda i: (i, 0))
    slot = [pl.BlockSpec((1, tr, cols), functools.partial(lambda k, i: (k, i, 0), k)) for k in range(3)]
    return _call(
        body, comm=comm, name=name, grid=(rows // tr,),
        in_specs=[slot[0]] + slot + [flat] * 3, out_specs=[flat] * 4, out_shape=[_sds((rows, cols), F32)] * 4,
        compiler_params=_params(),
    )(own, got, got, got, w, m, v)


def _adamw_replicated(gathered, segs, ws, ms, vs, loss_scale, name, comm=()):
    n = len(ws)

    def body(*refs):
        gat = refs[0]
        w_refs, m_refs, v_refs = refs[1:1 + n], refs[1 + n:1 + 2 * n], refs[1 + 2 * n:1 + 3 * n]
        outs = refs[1 + 3 * n:]

        def total(off, width):
            s = gat[0, :, off:off + width]
            for k in range(1, N_DEV):
                s = s + gat[k, :, off:off + width]
            return s

        outs[0][...] = loss_scale * total(segs[n][0], segs[n][1])
        for p in range(n):
            g = total(*segs[p])
            delta, nm, nv = _adamw_math(w_refs[p][...], g, m_refs[p][...], v_refs[p][...])
            for q, val in enumerate((g, delta, nm, nv)):
                outs[1 + 4 * p + q][...] = val
        for e, seg in enumerate(segs[n + 1:]):
            outs[1 + 4 * n + e][...] = total(*seg)

    return _call(
        body, name=name,
        out_shape=[_sds((1, segs[n][1]), F32)] + [_sds(w.shape, F32) for w in ws for _ in range(4)]
        + [_sds((1, seg[1]), F32) for seg in segs[n + 1:]],
        compiler_params=_params(),
    )(gathered, *ws, *ms, *vs)


def _adamw_plain(g, w, m, v, name):
    def body(g_ref, w_ref, m_ref, v_ref, d_ref, nm_ref, nv_ref):
        d_ref[...], nm_ref[...], nv_ref[...] = _adamw_math(w_ref[...], g_ref[...], m_ref[...], v_ref[...])

    return list(_call(body, name=name, out_shape=[_sds(w.shape, F32)] * 3)(g, w, m, v))


REPLICATED = ("ffn1_norm", "mix_norm", "b_in", "conv_cf_b", "ln_cf_g", "ln_cf_b", "ffn2_norm", "final_norm")
SHARDED = ("meta_tokens", "ffn1_w_gate", "ffn1_w_up", "ffn1_w_down", "w_in", "conv_sc_w", "conv_cf_w", "w_out",
           "ffn2_w_gate", "ffn2_w_up", "ffn2_w_down")
WEIGHTS = ("meta_tokens", "ffn1_norm", "ffn1_w_gate", "ffn1_w_up", "ffn1_w_down", "mix_norm", "w_in", "b_in",
           "conv_sc_w", "conv_cf_w", "conv_cf_b", "ln_cf_g", "ln_cf_b", "w_out", "ffn2_norm", "ffn2_w_gate",
           "ffn2_w_up", "ffn2_w_down", "final_norm")


TRANSPOSED = ("ffn1_w_gate", "ffn1_w_up", "ffn2_w_gate", "ffn2_w_up")


def _blocks2d(k, a):
    a = a.reshape(a.shape[-2:]) if a.ndim >= 2 else a.reshape(1, -1)
    return a.T if k in TRANSPOSED else a


def _step(x, tgt, w, m, v):
    n_seq, d = x.shape[1], x.shape[2]
    n_meta = w["meta_tokens"].shape[0]
    rows = -(-(n_seq + n_meta) // ROW_ALIGN) * ROW_ALIGN
    tm = rows // N_ROW_TILES
    cb = w["conv_sc_w"].shape[-1]
    dc = cb * N_DEV
    w2 = {k: _blocks2d(k, a) for k, a in w.items()}
    m2 = {k: _blocks2d(k, a) for k, a in m.items()}
    v2 = {k: _blocks2d(k, a) for k, a in v.items()}

    def as_given(k, r):
        return (r.T if k in TRANSPOSED else r).reshape(w[k].shape)

    def cast(k):
        return w2[k].astype(BF16)

    first = _gather_direct([cast("ffn1_w_gate"), cast("ffn1_w_up"), w2["meta_tokens"], w2["conv_sc_w"], w2["conv_cf_w"]])
    _exchange_alone(first, "gather_ffn1")
    full = dict(zip(("ffn1_w_gate", "ffn1_w_up", "meta_tokens", "conv_sc_w", "conv_cf_w"),
                    _exchange_alone(_gather_forward(first.results), "gather_ffn1_pass")))
    meta = jnp.transpose(full["meta_tokens"], (1, 0, 2)).reshape(n_meta, d)

    hs0 = jnp.concatenate([x[0], meta, jnp.zeros((rows - n_seq - n_meta, d), F32)], axis=0)
    tgt_rows = jnp.concatenate([tgt[0], jnp.zeros((rows - n_seq, d), F32)], axis=0)

    h1, r1 = _rms_fwd(hs0, w2["ffn1_norm"], tm, "rms_in")
    second = _gather_direct([cast("ffn1_w_down"), cast("w_in")])
    g1, u1 = _ffn_gu(h1, full["ffn1_w_gate"], full["ffn1_w_up"], tm, "ffn1_gu", comm=[second])
    full["ffn1_w_down"], win_blocks = _exchange_alone(_gather_forward(second.results), "gather_down1_pass")
    third = _gather_direct([cast("w_out"), cast("ffn2_w_gate")])
    hs1, h2, r2 = _ffn_down_norm(g1, u1, full["ffn1_w_down"], hs0, w2["mix_norm"], tm, "ffn1_down", comm=[third])
    win = jnp.transpose(win_blocks, (1, 0, 2)).reshape(d, -1)
    third_p, up2 = _gather_forward(third.results), _gather_direct([cast("ffn2_w_up")])
    u = _win_fwd(h2, win, w2["b_in"], dc, tm, "mix_in", comm=[third_p, up2])
    wout = third_p.results[0].reshape(-1, d)
    gate2p = third_p
    up2p = _gather_forward(up2.results)
    ysc, zc = _conv_fwd(u, full["conv_sc_w"], full["conv_cf_w"], w2["conv_cf_b"], n_seq, n_meta, "conv_fwd", comm=[up2p])
    down2 = _gather_direct([cast("ffn2_w_down")])
    y, hs2, h3, r3 = _wout_fwd(zc, ysc, wout, hs1, w2["ln_cf_g"], w2["ln_cf_b"], w2["ffn2_norm"], tm, "mix_out",
                               comm=[down2])
    down2p = _gather_forward(down2.results)
    full["ffn2_w_gate"], full["ffn2_w_up"] = gate2p.results[1], up2p.results[0]
    g2, u2 = _ffn_gu(h3, full["ffn2_w_gate"], full["ffn2_w_up"], tm, "ffn2_gu", comm=[down2p])
    full["ffn2_w_down"] = down2p.results[0]
    dhs3, dhs3b, loss_p, dgf_p = _ffn_down_loss(
        g2, u2, full["ffn2_w_down"], hs2, w2["final_norm"], tgt_rows, n_seq, tm, "ffn2_down_loss")

    dg2, du2, a2 = _ffn_bwd_da(dhs3b, full["ffn2_w_down"], g2, u2, tm, "ffn2_bwd_da")
    dhs2, dhs2b, dn3_p = _ffn_bwd_dh(
        dg2, du2, full["ffn2_w_gate"], full["ffn2_w_up"], hs2, r3, w2["ffn2_norm"], dhs3, tm, "ffn2_bwd_dh")
    xi, yi, ci = lax.axis_index("x"), lax.axis_index("y"), lax.axis_index("c")
    chip_of = [2 * xi + yi, 2 * (1 - xi) + yi, 2 * xi + (1 - yi), 2 * (1 - xi) + (1 - yi)]
    idx = jnp.stack([2 * ch + ci for ch in chip_of] + chip_of).astype(jnp.int32)
    out = {}

    def to_pairs(parts):
        return _pair_exchange([p.reshape((4, 2) + p.shape[1:]) for p in parts])

    def pair_sums(names, parts, pairs):
        return [_pair_sum(p, s, idx, "pair_sum_" + k) for k, p, s in zip(names, parts, pairs.results)]

    def update(names, sums, chips):
        for k, own, got in zip(names, sums, chips.results):
            res = _adamw_sharded(own, got, w2[k], m2[k], v2[k], "adamw_" + k)
            out[k] = [as_given(k, r) for r in res]

    ffn2 = ("ffn2_w_gate", "ffn2_w_up", "ffn2_w_down")
    tk = ROW_ALIGN
    grads2 = _ffn_bwd_dw(h3, dg2, du2, a2, dhs3b, tk, "ffn2_bwd_dw")
    pairs2 = to_pairs(grads2)
    dysc, dzc, dlg_p, dlb_p = _wout_bwd_dy(dhs2b, wout, zc, w2["ln_cf_g"], w2["ln_cf_b"], tm, "mix_out_bwd_dy",
                                           comm=[pairs2])
    sums2 = pair_sums(ffn2, grads2, pairs2)
    dwout = _wout_bwd_dw(y, dhs2b, N_DEV, tk, "mix_out_bwd_dw")
    chips2 = [_chip_exchange([s]) for s in sums2]
    du, dcsw, dccw, dcb = _conv_bwd(u, dysc, dzc, full["conv_sc_w"], full["conv_cf_w"], n_seq, n_meta, "conv_bwd",
                                    comm=[chips2[0]])
    dhs1, dhs1b, dn2_p = _win_bwd_dh(du, win, hs1, r2, w2["mix_norm"], dhs2, tm, "mix_in_bwd_dh", comm=[chips2[1]])
    dwin, dbin = _win_bwd_dw(h2, du, tk, "mix_in_bwd_dw", comm=[chips2[2]])
    for k, s, ch in zip(ffn2, sums2, chips2):
        update([k], [s], ch)

    mixer = ("w_out", "w_in", "conv_sc_w", "conv_cf_w")
    gradsm = [dwout, jnp.transpose(dwin.reshape(d, N_DEV, -1), (1, 0, 2)), dcsw, dccw]
    pairsm = to_pairs(gradsm)
    dg1, du1, a1 = _ffn_bwd_da(dhs1b, full["ffn1_w_down"], g1, u1, tm, "ffn1_bwd_da", comm=[pairsm])
    sumsm = pair_sums(mixer, gradsm, pairsm)
    chipsm = _chip_exchange(sumsm)
    ffn1 = ("ffn1_w_gate", "ffn1_w_up", "ffn1_w_down")
    grads1 = _ffn_bwd_dw(h1, dg1, du1, a1, dhs1b, tk, "ffn1_bwd_dw", comm=[chipsm])
    update(mixer, sumsm, chipsm)
    pairs1 = to_pairs(grads1)
    _exchange_alone(pairs1, "reduce_pair_ffn1")
    sums1 = pair_sums(ffn1, grads1, pairs1)
    chips1 = _chip_exchange(sums1)
    dhs0, _, dn1_p = _ffn_bwd_dh(
        dg1, du1, full["ffn1_w_gate"], full["ffn1_w_up"], hs0, r1, w2["ffn1_norm"], dhs1, tm, "ffn1_bwd_dh", comm=[chips1])
    update(ffn1, sums1, chips1)
    grad_x = dhs0[:n_seq][None]

    partial = {
        "ffn1_norm": dn1_p.sum(0), "mix_norm": dn2_p.sum(0), "b_in": dbin, "conv_cf_b": dcb,
        "ln_cf_g": dlg_p.sum(0), "ln_cf_b": dlb_p.sum(0), "ffn2_norm": dn3_p.sum(0), "final_norm": dgf_p.sum(0),
    }
    loss_seg = jnp.pad(loss_p.sum((0, 2)).reshape(1, 1), ((0, 0), (0, 127)))
    pieces = [partial[k] for k in REPLICATED] + [loss_seg, dhs0[n_seq:n_seq + n_meta].reshape(1, n_meta * d)]
    segs, off = [], 0
    for p in pieces:
        segs.append((off, p.shape[1]))
        off += p.shape[1]
    rows8 = _gather_rows(jnp.concatenate(pieces, axis=1), "gather_small")
    res = _adamw_replicated(rows8, segs, [w2[k] for k in REPLICATED], [m2[k] for k in REPLICATED],
                            [v2[k] for k in REPLICATED], 0.5 / d, "adamw_replicated")
    loss = res[0][0, 0]
    for p, k in enumerate(REPLICATED):
        out[k] = [r.reshape(w[k].shape) for r in res[1 + 4 * p:5 + 4 * p]]
    ob = w2["meta_tokens"].shape[1]
    gmeta = lax.dynamic_slice_in_dim(res[-1].reshape(n_meta, d), (4 * xi + 2 * yi + ci) * ob, ob, axis=1)
    out["meta_tokens"] = [gmeta] + _adamw_plain(gmeta, w2["meta_tokens"], m2["meta_tokens"], v2["meta_tokens"], "adamw_meta_tokens")

    return (loss, grad_x, *[out[k][0] for k in WEIGHTS], *[out[k][1] for k in WEIGHTS],
            *[out[k][2] for k in WEIGHTS], *[out[k][3] for k in WEIGHTS])


def kernel(x, meta_tokens, ffn1_norm, ffn1_w_gate, ffn1_w_up, ffn1_w_down, mix_norm, w_in, b_in, conv_sc_w, conv_cf_w, conv_cf_b, ln_cf_g, ln_cf_b, w_out, ffn2_norm, ffn2_w_gate, ffn2_w_up, ffn2_w_down, final_norm, loss_target, m_meta_tokens, m_ffn1_norm, m_ffn1_w_gate, m_ffn1_w_up, m_ffn1_w_down, m_mix_norm, m_w_in, m_b_in, m_conv_sc_w, m_conv_cf_w, m_conv_cf_b, m_ln_cf_g, m_ln_cf_b, m_w_out, m_ffn2_norm, m_ffn2_w_gate, m_ffn2_w_up, m_ffn2_w_down, m_final_norm, v_meta_tokens, v_ffn1_norm, v_ffn1_w_gate, v_ffn1_w_up, v_ffn1_w_down, v_mix_norm, v_w_in, v_b_in, v_conv_sc_w, v_conv_cf_w, v_conv_cf_b, v_ln_cf_g, v_ln_cf_b, v_w_out, v_ffn2_norm, v_ffn2_w_gate, v_ffn2_w_up, v_ffn2_w_down, v_final_norm):
    given = dict(locals())
    w = {k: given[k] for k in WEIGHTS}
    m = {k: given["m_" + k] for k in WEIGHTS}
    v = {k: given["v_" + k] for k in WEIGHTS}
    return _step(x, loss_target, w, m, v)
```

```python
import functools

import jax
import jax.numpy as jnp
from jax import lax
from jax.experimental import pallas as pl
from jax.experimental.pallas import tpu as pltpu

F32 = jnp.float32
BF16 = jnp.bfloat16
EPS = 1e-6
FFN_RES_SCALE = 0.5
SC_WIDTH = 3
CF_WIDTH = 31
ADAM_LR = 0.001
ADAM_B1 = 0.9
ADAM_B2 = 0.999
ADAM_EPS = 1e-08
ADAM_WD = 0.01
ADAM_STEP = 10

N_DEV = 8
N_ROW_TILES = 8
ROW_ALIGN = 256
CONV_PAD = 32
CONV_CH = 128
SUB_ROWS = 32
SUBLANES = 8
BF16_ROWS = 16
VMEM_LIMIT = 56 * 1024 * 1024
MESH = pl.DeviceIdType.MESH
ANY = pl.BlockSpec(memory_space=pl.ANY)

NT = (((1,), (1,)), ((), ()))
TN = (((0,), (0,)), ((), ()))


def _pallas(body, **kw):
    return pl.pallas_call(body, **kw)


class _Exchange:
    def __init__(self, inputs, out_shapes, sem_shapes, start, finish, aliases=None):
        self.inputs, self.out_shapes, self.sem_shapes = list(inputs), list(out_shapes), list(sem_shapes)
        self.start, self.finish, self.aliases = start, finish, dict(aliases or {})
        self.results = None


def _call(body, comm=(), **kw):
    if not comm:
        return _pallas(body, **kw)
    grid = kw.pop("grid")
    in_specs = list(kw.pop("in_specs"))
    out_specs, out_shape = kw.pop("out_specs"), kw.pop("out_shape")
    scratch = list(kw.pop("scratch_shapes", []))
    single = not isinstance(out_shape, (list, tuple))
    out_specs, out_shape = ([out_specs], [out_shape]) if single else (list(out_specs), list(out_shape))
    n_in, n_out, n_scr = len(in_specs), len(out_shape), len(scratch)
    c_in = [a for job in comm for a in job.inputs]
    c_out = [s for job in comm for s in job.out_shapes]
    c_sem = [s for job in comm for s in job.sem_shapes]
    aliases, i0, o0 = dict(kw.pop("input_output_aliases", {})), n_in, n_out
    for job in comm:
        aliases.update({i0 + i: o0 + o for i, o in job.aliases.items()})
        i0, o0 = i0 + len(job.inputs), o0 + len(job.out_shapes)

    def hosted(*refs):
        pos = [0]

        def take(n):
            pos[0] += n
            return refs[pos[0] - n:pos[0]]

        ins, cins, outs, couts, scr, sems = take(n_in), take(len(c_in)), take(n_out), take(len(c_out)), take(n_scr), take(len(c_sem))
        ids = [pl.program_id(k) for k in range(len(grid))]
        first = functools.reduce(jnp.logical_and, [i == 0 for i in ids])
        last = functools.reduce(jnp.logical_and, [i == g - 1 for i, g in zip(ids, grid)])

        def each(phase):
            i, o, s = 0, 0, 0
            for job in comm:
                ni, no, ns = len(job.inputs), len(job.out_shapes), len(job.sem_shapes)
                getattr(job, phase)(cins[i:i + ni], couts[o:o + no], sems[s:s + ns])
                i, o, s = i + ni, o + no, s + ns

        @pl.when(first)
        def _():
            each("start")

        body(*ins, *outs, *scr)

        @pl.when(last)
        def _():
            each("finish")

    call = _pallas(
        hosted, grid=grid, in_specs=in_specs + [ANY] * len(c_in), out_specs=out_specs + [ANY] * len(c_out),
        out_shape=out_shape + c_out, scratch_shapes=scratch + c_sem, input_output_aliases=aliases, **kw)

    def run(*args):
        res = call(*args, *c_in)
        o = n_out
        for job in comm:
            job.results = list(res[o:o + len(job.out_shapes)])
            o += len(job.out_shapes)
        return res[0] if single else list(res[:n_out])

    return run


def _exchange_alone(job, name, comm=()):
    n_in, n_out = len(job.inputs), len(job.out_shapes)

    def body(*refs):
        ins, outs, sems = refs[:n_in], refs[n_in:n_in + n_out], refs[n_in + n_out:]
        job.start(ins, outs, sems)
        job.finish(ins, outs, sems)

    res = _pallas(body, name=name, in_specs=[ANY] * n_in, out_specs=[ANY] * n_out, out_shape=job.out_shapes,
                  scratch_shapes=job.sem_shapes, input_output_aliases=job.aliases)(*job.inputs)
    job.results = list(res)
    return job.results


def _params(**kw):
    return pltpu.CompilerParams(vmem_limit_bytes=VMEM_LIMIT, **kw)


def _sigmoid(x):
    return 0.5 * jnp.tanh(0.5 * x) + 0.5


def _sds(shape, dtype):
    return jax.ShapeDtypeStruct(shape, dtype)


def _place():
    x, y, c = lax.axis_index("x"), lax.axis_index("y"), lax.axis_index("c")
    chips = [(1 - x, y), (x, 1 - y), (1 - x, 1 - y)]
    return x, y, c, chips


def _slot(ref, p):
    return ref.at[4 * p[0] + 2 * p[1] + p[2]]


def _remote(src, dst, send_sem, recv_sem, to):
    return pltpu.make_async_remote_copy(src_ref=src, dst_ref=dst, send_sem=send_sem, recv_sem=recv_sem,
                                        device_id=to, device_id_type=MESH)


def _gather_direct(arrs):
    n = len(arrs)

    def copies(ins, outs, sems):
        send_sems, recv_sems, local_sems = sems
        x, y, c, chips = _place()
        me = (x, y, c)
        peers = [(x, y, 1 - c)] + [(*chip, c) for chip in chips]
        local = [pltpu.make_async_copy(ins[a], _slot(outs[a], me), local_sems.at[a]) for a in range(n)]
        sends = [_remote(ins[a], _slot(outs[a], me), send_sems.at[a, k], recv_sems.at[a, k], peer)
                 for a in range(n) for k, peer in enumerate(peers)]
        arrivals = [_remote(ins[a], _slot(outs[a], peer), send_sems.at[a, k], recv_sems.at[a, k], peer)
                    for a in range(n) for k, peer in enumerate(peers)]
        return local, sends, arrivals

    def start(ins, outs, sems):
        local, sends, _ = copies(ins, outs, sems)
        for cp in local + sends:
            cp.start()

    def finish(ins, outs, sems):
        local, sends, arrivals = copies(ins, outs, sems)
        for cp in arrivals:
            cp.wait_recv()
        for cp in sends:
            cp.wait_send()
        for cp in local:
            cp.wait()

    dma = pltpu.SemaphoreType.DMA
    return _Exchange(arrs, [_sds((N_DEV,) + a.shape, a.dtype) for a in arrs], [dma((n, 4)), dma((n, 4)), dma((n,))],
                     start, finish)


def _gather_forward(gathered):
    n = len(gathered)

    def copies(ins, outs, sems):
        send_sems, recv_sems = sems
        x, y, c, chips = _place()
        sibling = (x, y, 1 - c)
        sends = [_remote(_slot(ins[a], (*chip, c)), _slot(outs[a], (*chip, c)), send_sems.at[a, j], recv_sems.at[a, j], sibling)
                 for a in range(n) for j, chip in enumerate(chips)]
        arrivals = [_remote(_slot(ins[a], (*chip, c)), _slot(outs[a], (*chip, 1 - c)), send_sems.at[a, j], recv_sems.at[a, j], sibling)
                    for a in range(n) for j, chip in enumerate(chips)]
        return sends, arrivals

    def start(ins, outs, sems):
        for cp in copies(ins, outs, sems)[0]:
            cp.start()

    def finish(ins, outs, sems):
        sends, arrivals = copies(ins, outs, sems)
        for cp in arrivals:
            cp.wait_recv()
        for cp in sends:
            cp.wait_send()

    dma = pltpu.SemaphoreType.DMA
    return _Exchange(gathered, [_sds(a.shape, a.dtype) for a in gathered], [dma((n, 3)), dma((n, 3))], start, finish,
                     aliases={a: a for a in range(n)})


def _gather_all(arrs):
    direct = _gather_direct(arrs)
    n = len(arrs)
    n_sems = len(direct.sem_shapes)
    dma = pltpu.SemaphoreType.DMA

    def start(ins, outs, sems):
        direct.start(ins, outs, sems[:n_sems])

    def finish(ins, outs, sems):
        direct.finish(ins, outs, sems[:n_sems])
        passed = _gather_forward(outs)
        passed.start(outs, outs, sems[n_sems:])
        passed.finish(outs, outs, sems[n_sems:])

    return _Exchange(arrs, direct.out_shapes, direct.sem_shapes + [dma((n, 3)), dma((n, 3))], start, finish)


def _pair_exchange(arrs):
    n = len(arrs)

    def copies(ins, outs, sems):
        x, y, c, _ = _place()
        return [_remote(ins[a].at[:, 1 - c], outs[a], sems[0].at[a], sems[1].at[a], (x, y, 1 - c)) for a in range(n)]

    def start(ins, outs, sems):
        for cp in copies(ins, outs, sems):
            cp.start()

    def finish(ins, outs, sems):
        for cp in copies(ins, outs, sems):
            cp.wait()

    dma = pltpu.SemaphoreType.DMA
    return _Exchange(arrs, [_sds((4,) + a.shape[2:], a.dtype) for a in arrs], [dma((n,)), dma((n,))], start, finish)


def _chip_exchange(arrs):
    n = len(arrs)

    def copies(ins, outs, sems):
        x, y, c, chips = _place()
        return [_remote(ins[a].at[1 + j], outs[a].at[j], sems[0].at[a, j], sems[1].at[a, j], (*chip, c))
                for a in range(n) for j, chip in enumerate(chips)]

    def start(ins, outs, sems):
        for cp in copies(ins, outs, sems):
            cp.start()

    def finish(ins, outs, sems):
        for cp in copies(ins, outs, sems):
            cp.wait()

    dma = pltpu.SemaphoreType.DMA
    return _Exchange(arrs, [_sds((3,) + a.shape[1:], a.dtype) for a in arrs], [dma((n, 3)), dma((n, 3))], start, finish)


def _gather_rows(vec, name, comm=()):
    def body(in_ref, out_ref, send_sems, recv_sems, local_sem):
        x, y, c, _ = _place()
        me = 4 * x + 2 * y + c
        mine = pltpu.make_async_copy(in_ref, out_ref.at[me], local_sem)
        mine.start()
        copies = []
        for k in range(1, N_DEV):
            to = (1 - x if k & 4 else x, 1 - y if k & 2 else y, 1 - c if k & 1 else c)
            copies.append(pltpu.make_async_remote_copy(
                src_ref=in_ref, dst_ref=out_ref.at[me], send_sem=send_sems.at[k - 1], recv_sem=recv_sems.at[k - 1],
                device_id=to, device_id_type=MESH))
        for cp in copies:
            cp.start()
        for cp in copies:
            cp.wait()
        mine.wait()

    return _call(
        body, name=name, out_shape=_sds((N_DEV,) + vec.shape, vec.dtype), in_specs=[ANY], out_specs=ANY,
        scratch_shapes=[pltpu.SemaphoreType.DMA((7,)), pltpu.SemaphoreType.DMA((7,)), pltpu.SemaphoreType.DMA],
    )(vec)


def _for_row_groups(tm, fn):
    def step(i, carry):
        fn(pl.ds(pl.multiple_of(i * SUB_ROWS, SUB_ROWS), SUB_ROWS), i * SUB_ROWS)
        return carry

    lax.fori_loop(0, tm // SUB_ROWS, step, 0)


def _row_parts(tm, n=4):
    units = tm // BF16_ROWS
    sizes = [(units // n + (1 if i < units % n else 0)) * BF16_ROWS for i in range(n)]
    starts = [sum(sizes[:i]) for i in range(n)]
    return [slice(s, s + z) for s, z in zip(starts, sizes) if z]


def _once(shape, index_map):
    return pl.BlockSpec(shape, index_map, pipeline_mode=pl.Buffered(1))


def _rms_fwd(hs, g, tm, name, comm=()):
    rows, d = hs.shape

    def body(hs_ref, g_ref, h_ref, r_ref):
        def group(rs, _):
            xv = hs_ref[rs, :]
            r = lax.rsqrt(jnp.mean(xv * xv, axis=-1, keepdims=True) + EPS)
            h_ref[rs, :] = (xv * r * g_ref[...]).astype(BF16)
            r_ref[rs, :] = r

        _for_row_groups(tm, group)

    return _call(
        body, comm=comm, name=name, grid=(rows // tm,),
        in_specs=[pl.BlockSpec((tm, d), lambda i: (i, 0)), pl.BlockSpec((1, d), lambda i: (0, 0))],
        out_specs=[pl.BlockSpec((tm, d), lambda i: (i, 0)), pl.BlockSpec((tm, 1), lambda i: (i, 0))],
        out_shape=[_sds((rows, d), BF16), _sds((rows, 1), F32)], compiler_params=_params(),
    )(hs, g)


def _ffn_gu(h, wg, wu, tm, name, comm=()):
    rows, d = h.shape
    nb, fb, _ = wg.shape

    def body(h_ref, wg_ref, wu_ref, g_ref, u_ref):
        hv = h_ref[...]
        g_ref[0] = lax.dot_general(hv, wg_ref[0], NT, preferred_element_type=F32).astype(BF16)
        u_ref[0] = lax.dot_general(hv, wu_ref[0], NT, preferred_element_type=F32).astype(BF16)

    wspec = pl.BlockSpec((1, fb, d), lambda j, i: (j, 0, 0))
    ospec = pl.BlockSpec((1, tm, fb), lambda j, i: (j, i, 0))
    return _call(
        body, comm=comm, name=name, grid=(nb, rows // tm),
        in_specs=[pl.BlockSpec((tm, d), lambda j, i: (i, 0)), wspec, wspec],
        out_specs=[ospec, ospec], out_shape=[_sds((nb, rows, fb), BF16)] * 2, compiler_params=_params(),
    )(h, wg, wu)


def _ffn_proj(h, w, tm, name, comm=()):
    rows, d = h.shape
    nb, fb, _ = w.shape

    def body(h_ref, w_ref, p_ref):
        p_ref[0] = lax.dot_general(h_ref[...], w_ref[0], NT, preferred_element_type=F32).astype(BF16)

    return _call(
        body, comm=comm, name=name, grid=(nb, rows // tm),
        in_specs=[pl.BlockSpec((tm, d), lambda j, i: (i, 0)), pl.BlockSpec((1, fb, d), lambda j, i: (j, 0, 0))],
        out_specs=pl.BlockSpec((1, tm, fb), lambda j, i: (j, i, 0)), out_shape=_sds((nb, rows, fb), BF16),
        compiler_params=_params(),
    )(h, w)


def _ffn_down_norm(g, u, wd, hs, gn, tm, name, comm=()):
    nb, rows, fb = g.shape
    d = hs.shape[1]

    def body(g_ref, u_ref, wd_ref, hs_ref, gn_ref, hsn_ref, hn_ref, rn_ref, acc_ref):
        j = pl.program_id(1)

        @pl.when(j == 0)
        def _():
            acc_ref[...] = jnp.zeros_like(acc_ref)

        gv = g_ref[0]
        a = gv * _sigmoid(gv) * u_ref[0]
        acc_ref[...] += jnp.dot(a, wd_ref[0], preferred_element_type=F32)

        @pl.when(j == nb - 1)
        def _():
            def group(rs, _):
                hsn = hs_ref[rs, :] + FFN_RES_SCALE * acc_ref[rs, :]
                r = lax.rsqrt(jnp.mean(hsn * hsn, axis=-1, keepdims=True) + EPS)
                hsn_ref[rs, :] = hsn
                hn_ref[rs, :] = (hsn * r * gn_ref[...]).astype(BF16)
                rn_ref[rs, :] = r

            _for_row_groups(tm, group)

    aspec = pl.BlockSpec((1, tm, fb), lambda i, j: (j, i, 0))
    row = pl.BlockSpec((tm, d), lambda i, j: (i, 0))
    return _call(
        body, comm=comm, name=name, grid=(rows // tm, nb),
        in_specs=[aspec, aspec, pl.BlockSpec((1, fb, d), lambda i, j: (j, 0, 0)), _once((tm, d), lambda i, j: (i, 0)),
                  pl.BlockSpec((1, d), lambda i, j: (0, 0))],
        out_specs=[row, row, pl.BlockSpec((tm, 1), lambda i, j: (i, 0))],
        out_shape=[_sds((rows, d), F32), _sds((rows, d), BF16), _sds((rows, 1), F32)],
        scratch_shapes=[pltpu.VMEM((tm, d), F32)], compiler_params=_params(),
    )(g, u, wd, hs, gn)


def _ffn_down_loss(g, u, wd, hs, gf, tgt, n_seq, tm, name, comm=()):
    nb, rows, fb = g.shape
    d = hs.shape[1]
    nt = rows // tm

    def body(g_ref, u_ref, wd_ref, hs_ref, gf_ref, tgt_ref, dhs_ref, dhsb_ref, loss_ref, dgf_ref, acc_ref):
        i, j = pl.program_id(0), pl.program_id(1)

        @pl.when(j == 0)
        def _():
            acc_ref[...] = jnp.zeros_like(acc_ref)

        gv = g_ref[0]
        a = gv * _sigmoid(gv) * u_ref[0]
        acc_ref[...] += jnp.dot(a, wd_ref[0], preferred_element_type=F32)

        @pl.when(j == nb - 1)
        def _():
            loss_ref[0] = jnp.zeros((1, d), F32)
            dgf_ref[0] = jnp.zeros((1, d), F32)

            def group(rs, r0):
                hs3 = hs_ref[rs, :] + FFN_RES_SCALE * acc_ref[rs, :]
                r = lax.rsqrt(jnp.mean(hs3 * hs3, axis=-1, keepdims=True) + EPS)
                gfv = gf_ref[...]
                y = hs3 * r
                rowid = i * tm + r0 + lax.broadcasted_iota(jnp.int32, (SUB_ROWS, 1), 0)
                err = jnp.where(rowid < n_seq, y * gfv - tgt_ref[rs, :], 0.0)
                loss_ref[0] += jnp.sum(err * err, axis=0, keepdims=True)
                dout = err * (1.0 / d)
                dgf_ref[0] += jnp.sum(dout * y, axis=0, keepdims=True)
                t = dout * gfv
                dhs = r * t - hs3 * (r * r * r) * jnp.mean(t * hs3, axis=-1, keepdims=True)
                dhs_ref[rs, :] = dhs
                dhsb_ref[rs, :] = dhs.astype(BF16)

            _for_row_groups(tm, group)

    aspec = pl.BlockSpec((1, tm, fb), lambda i, j: (j, i, 0))
    row = pl.BlockSpec((tm, d), lambda i, j: (i, 0))
    once = _once((tm, d), lambda i, j: (i, 0))
    part = pl.BlockSpec((1, 1, d), lambda i, j: (i, 0, 0))
    return _call(
        body, comm=comm, name=name, grid=(nt, nb),
        in_specs=[aspec, aspec, pl.BlockSpec((1, fb, d), lambda i, j: (j, 0, 0)), once,
                  pl.BlockSpec((1, d), lambda i, j: (0, 0)), once],
        out_specs=[row, row, part, part],
        out_shape=[_sds((rows, d), F32), _sds((rows, d), BF16), _sds((nt, 1, d), F32), _sds((nt, 1, d), F32)],
        scratch_shapes=[pltpu.VMEM((tm, d), F32)], compiler_params=_params(),
    )(g, u, wd, hs, gf, tgt)


def _ffn_bwd_da(do, wd, g, u, tm, name, comm=()):
    nb, rows, fb = g.shape
    d = do.shape[1]

    def body(do_ref, wd_ref, g_ref, u_ref, dg_ref, du_ref, a_ref):
        da = FFN_RES_SCALE * lax.dot_general(do_ref[...], wd_ref[0], NT, preferred_element_type=F32)
        gv = g_ref[0].astype(F32)
        uv = u_ref[0].astype(F32)
        s = _sigmoid(gv)
        sg = gv * s
        a_ref[0] = (sg * uv).astype(BF16)
        du_ref[0] = (da * sg).astype(BF16)
        dg_ref[0] = (da * uv * (s * (1.0 + gv * (1.0 - s)))).astype(BF16)

    aspec = pl.BlockSpec((1, tm, fb), lambda j, i: (j, i, 0))
    return _call(
        body, comm=comm, name=name, grid=(nb, rows // tm),
        in_specs=[pl.BlockSpec((tm, d), lambda j, i: (i, 0)), pl.BlockSpec((1, fb, d), lambda j, i: (j, 0, 0)), aspec, aspec],
        out_specs=[aspec] * 3, out_shape=[_sds((nb, rows, fb), BF16)] * 3, compiler_params=_params(),
    )(do, wd, g, u)


def _rms_bwd(tm, d, dh_ref, hs_ref, r_ref, gn_ref, dres_ref, dhs_ref, dhsb_ref, dgn_ref):
    dgn_ref[0] = jnp.zeros((1, d), F32)

    def group(rs, _):
        dh, hs, r = dh_ref[rs, :], hs_ref[rs, :], r_ref[rs, :]
        dgn_ref[0] += jnp.sum(dh * (hs * r), axis=0, keepdims=True)
        t = dh * gn_ref[...]
        dhs = dres_ref[rs, :] + r * t - hs * (r * r * r) * jnp.mean(t * hs, axis=-1, keepdims=True)
        dhs_ref[rs, :] = dhs
        dhsb_ref[rs, :] = dhs.astype(BF16)

    _for_row_groups(tm, group)


def _ffn_bwd_dh(dg, du, wg, wu, hs, r, gn, dres, tm, name, comm=(), zeros=None):
    nb, rows, fb = dg.shape
    d = hs.shape[1]
    nt = rows // tm
    extra = [] if zeros is None else [zeros]

    def body(dg_ref, du_ref, wg_ref, wu_ref, hs_ref, r_ref, gn_ref, dres_ref, *rest):
        dhs_ref, dhsb_ref, dgn_ref, acc_ref = rest[len(extra):]
        j = pl.program_id(1)

        @pl.when(j == 0)
        def _():
            acc_ref[...] = jnp.zeros_like(acc_ref)

        acc_ref[...] += (jnp.dot(dg_ref[0], wg_ref[0], preferred_element_type=F32)
                         + jnp.dot(du_ref[0], wu_ref[0], preferred_element_type=F32))

        @pl.when(j == nb - 1)
        def _():
            _rms_bwd(tm, d, acc_ref, hs_ref, r_ref, gn_ref, dres_ref, dhs_ref, dhsb_ref, dgn_ref)

    aspec = pl.BlockSpec((1, tm, fb), lambda i, j: (j, i, 0))
    wspec = pl.BlockSpec((1, fb, d), lambda i, j: (j, 0, 0))
    row = pl.BlockSpec((tm, d), lambda i, j: (i, 0))
    once = _once((tm, d), lambda i, j: (i, 0))
    return _call(
        body, comm=comm, name=name, grid=(nt, nb),
        in_specs=[aspec, aspec, wspec, wspec, once, pl.BlockSpec((tm, 1), lambda i, j: (i, 0)),
                  pl.BlockSpec((1, d), lambda i, j: (0, 0)), once] + [ANY] * len(extra),
        out_specs=[row, row, pl.BlockSpec((1, 1, d), lambda i, j: (i, 0, 0))],
        out_shape=[_sds((rows, d) if zeros is None else zeros.shape, F32), _sds((rows, d), BF16), _sds((nt, 1, d), F32)],
        scratch_shapes=[pltpu.VMEM((tm, d), F32)], compiler_params=_params(),
        input_output_aliases={8: 0} if extra else {},
    )(dg, du, wg, wu, hs, r, gn, dres, *extra)


def _ffn_bwd_dw(h, dg, du, a, do, tk, name, comm=()):
    nb, rows, fb = dg.shape
    d = h.shape[1]
    nk = rows // tk

    def body(h_ref, dg_ref, du_ref, a_ref, do_ref, dwg_ref, dwu_ref, dwd_ref, accg, accu, accd):
        k = pl.program_id(1)

        @pl.when(k == 0)
        def _():
            accg[...] = jnp.zeros_like(accg)
            accu[...] = jnp.zeros_like(accu)
            accd[...] = jnp.zeros_like(accd)

        hv = h_ref[...]
        accg[...] += lax.dot_general(dg_ref[0], hv, TN, preferred_element_type=F32)
        accu[...] += lax.dot_general(du_ref[0], hv, TN, preferred_element_type=F32)
        accd[...] += lax.dot_general(a_ref[0], do_ref[...], TN, preferred_element_type=F32)

        @pl.when(k == nk - 1)
        def _():
            dwg_ref[0] = accg[...].astype(BF16)
            dwu_ref[0] = accu[...].astype(BF16)
            dwd_ref[0] = (FFN_RES_SCALE * accd[...]).astype(BF16)

    aspec = pl.BlockSpec((1, tk, fb), lambda j, k: (j, k, 0))
    row = pl.BlockSpec((tk, d), lambda j, k: (k, 0))
    down = pl.BlockSpec((1, fb, d), lambda j, k: (j, 0, 0))
    return _call(
        body, comm=comm, name=name, grid=(nb, nk),
        in_specs=[row, aspec, aspec, aspec, row], out_specs=[down] * 3,
        out_shape=[_sds((nb, fb, d), BF16)] * 3, scratch_shapes=[pltpu.VMEM((fb, d), F32)] * 3,
        compiler_params=_params(),
    )(h, dg, du, a, do)


def _win_fwd(h, w, b, dc, tm, name, comm=()):
    rows, d = h.shape
    ng = w.shape[1] // dc

    def body(h_ref, w_ref, b_ref, u_ref):
        u_ref[...] = (jnp.dot(h_ref[...], w_ref[...], preferred_element_type=F32) + b_ref[...]).astype(BF16)

    return _call(
        body, comm=comm, name=name, grid=(ng, rows // tm),
        in_specs=[pl.BlockSpec((tm, d), lambda m, i: (i, 0)), pl.BlockSpec((d, dc), lambda m, i: (0, m)),
                  pl.BlockSpec((1, dc), lambda m, i: (0, m))],
        out_specs=pl.BlockSpec((tm, dc), lambda m, i: (i, m)), out_shape=_sds((rows, ng * dc), BF16),
        compiler_params=_params(),
    )(h, w, b)


def _win_bwd_dh(du, w, hs, r, gn, dres, tm, name, comm=()):
    rows, d = hs.shape
    ng, _, dc = du.shape
    nt = rows // tm

    def body(du_ref, w_ref, hs_ref, r_ref, gn_ref, dres_ref, dhs_ref, dhsb_ref, dgn_ref, acc_ref):
        m = pl.program_id(1)

        @pl.when(m == 0)
        def _():
            acc_ref[...] = jnp.zeros_like(acc_ref)

        acc_ref[...] += lax.dot_general(du_ref[0], w_ref[...], NT, preferred_element_type=F32)

        @pl.when(m == ng - 1)
        def _():
            _rms_bwd(tm, d, acc_ref, hs_ref, r_ref, gn_ref, dres_ref, dhs_ref, dhsb_ref, dgn_ref)

    row = pl.BlockSpec((tm, d), lambda i, m: (i, 0))
    once = _once((tm, d), lambda i, m: (i, 0))
    return _call(
        body, comm=comm, name=name, grid=(nt, ng),
        in_specs=[pl.BlockSpec((1, tm, dc), lambda i, m: (m, i, 0)), pl.BlockSpec((d, dc), lambda i, m: (0, m)), once,
                  pl.BlockSpec((tm, 1), lambda i, m: (i, 0)), pl.BlockSpec((1, d), lambda i, m: (0, 0)), once],
        out_specs=[row, row, pl.BlockSpec((1, 1, d), lambda i, m: (i, 0, 0))],
        out_shape=[_sds((rows, d), F32), _sds((rows, d), BF16), _sds((nt, 1, d), F32)],
        scratch_shapes=[pltpu.VMEM((tm, d), F32)], compiler_params=_params(),
    )(du, w, hs, r, gn, dres)


def _win_bwd_dw(h, du, tk, name, comm=()):
    rows, d = h.shape
    ng, _, dc = du.shape
    nk = rows // tk

    def body(h_ref, du_ref, dw_ref, db_ref, acc, accb):
        k = pl.program_id(1)

        @pl.when(k == 0)
        def _():
            acc[...] = jnp.zeros_like(acc)
            accb[...] = jnp.zeros_like(accb)

        duv = du_ref[0]
        acc[...] += lax.dot_general(h_ref[...], duv, TN, preferred_element_type=F32)
        accb[...] += jnp.sum(duv.astype(F32), axis=0, keepdims=True)

        @pl.when(k == nk - 1)
        def _():
            dw_ref[...] = acc[...].astype(BF16)
            db_ref[...] = accb[...]

    return _call(
        body, comm=comm, name=name, grid=(ng, nk),
        in_specs=[pl.BlockSpec((tk, d), lambda m, k: (k, 0)), pl.BlockSpec((1, tk, dc), lambda m, k: (m, k, 0))],
        out_specs=[pl.BlockSpec((d, dc), lambda m, k: (0, m)), pl.BlockSpec((1, dc), lambda m, k: (0, m))],
        out_shape=[_sds((d, ng * dc), BF16), _sds((1, ng * dc), F32)],
        scratch_shapes=[pltpu.VMEM((d, dc), F32), pltpu.VMEM((1, dc), F32)], compiler_params=_params(),
    )(h, du)


def _layernorm_silu(zc, lg, lb):
    mu = jnp.mean(zc, axis=-1, keepdims=True)
    xc = zc - mu
    rstd = lax.rsqrt(jnp.mean(xc * xc, axis=-1, keepdims=True) + EPS)
    nrm = xc * rstd
    lin = nrm * lg + lb
    s = _sigmoid(lin)
    return nrm, rstd, lin, s


def _wout_fwd(zc, ysc, wout, hs, lg, lb, gn, rows, tm, name, comm=()):
    dc = zc.shape[1]
    d = hs.shape[1]

    def body(zc_ref, ysc_ref, w_ref, hs_ref, lg_ref, lb_ref, gn_ref, y_ref, hsn_ref, hn_ref, rn_ref):
        def mix(rs, _):
            _, _, lin, s = _layernorm_silu(zc_ref[rs, :], lg_ref[...], lb_ref[...])
            y_ref[rs, :dc] = ysc_ref[rs, :]
            y_ref[rs, dc:] = (lin * s).astype(BF16)

        _for_row_groups(tm, mix)
        hsn_ref[...] = jnp.dot(y_ref[...], w_ref[...], preferred_element_type=F32)

        def norm(rs, _):
            hsn = hs_ref[rs, :] + hsn_ref[rs, :]
            r = lax.rsqrt(jnp.mean(hsn * hsn, axis=-1, keepdims=True) + EPS)
            hsn_ref[rs, :] = hsn
            hn_ref[rs, :] = (hsn * r * gn_ref[...]).astype(BF16)
            rn_ref[rs, :] = r

        _for_row_groups(tm, norm)

    half = pl.BlockSpec((tm, dc), lambda i: (i, 0))
    row = pl.BlockSpec((tm, d), lambda i: (i, 0))
    vec_c = pl.BlockSpec((1, dc), lambda i: (0, 0))
    return _call(
        body, comm=comm, name=name, grid=(rows // tm,),
        in_specs=[half, half, _once((2 * dc, d), lambda i: (0, 0)), row, vec_c, vec_c,
                  pl.BlockSpec((1, d), lambda i: (0, 0))],
        out_specs=[pl.BlockSpec((tm, 2 * dc), lambda i: (i, 0)), row, row, pl.BlockSpec((tm, 1), lambda i: (i, 0))],
        out_shape=[_sds((rows, 2 * dc), BF16), _sds((rows, d), F32), _sds((rows, d), BF16), _sds((rows, 1), F32)],
        compiler_params=_params(),
    )(zc, ysc, wout, hs, lg, lb, gn)


def _wout_bwd_dy(do, wout, zc, lg, lb, tm, name, comm=()):
    rows, d = do.shape
    dc = zc.shape[1]
    nt = rows // tm

    def body(do_ref, w_ref, zc_ref, lg_ref, lb_ref, dysc_ref, dzc_ref, dlg_ref, dlb_ref, dy_ref):
        dy_ref[...] = lax.dot_general(do_ref[...], w_ref[...], NT, preferred_element_type=F32)
        dlb_ref[0] = jnp.zeros((1, dc), F32)
        dlg_ref[0] = jnp.zeros((1, dc), F32)

        def group(rs, _):
            dysc_ref[rs, :] = dy_ref[rs, :dc].astype(BF16)
            nrm, rstd, lin, s = _layernorm_silu(zc_ref[rs, :], lg_ref[...], lb_ref[...])
            dl = dy_ref[rs, dc:] * (s * (1.0 + lin * (1.0 - s)))
            dlb_ref[0] += jnp.sum(dl, axis=0, keepdims=True)
            dlg_ref[0] += jnp.sum(dl * nrm, axis=0, keepdims=True)
            dn = dl * lg_ref[...]
            dzc_ref[rs, :] = rstd * (dn - jnp.mean(dn, axis=-1, keepdims=True)
                                     - nrm * jnp.mean(dn * nrm, axis=-1, keepdims=True))

        _for_row_groups(tm, group)

    half = pl.BlockSpec((tm, dc), lambda i: (i, 0))
    vec_c = pl.BlockSpec((1, dc), lambda i: (0, 0))
    part = pl.BlockSpec((1, 1, dc), lambda i: (i, 0, 0))
    return _call(
        body, comm=comm, name=name, grid=(nt,),
        in_specs=[pl.BlockSpec((tm, d), lambda i: (i, 0)), _once((2 * dc, d), lambda i: (0, 0)), half, vec_c, vec_c],
        out_specs=[half, half, part, part],
        out_shape=[_sds((rows, dc), BF16), _sds((rows, dc), F32), _sds((nt, 1, dc), F32), _sds((nt, 1, dc), F32)],
        scratch_shapes=[pltpu.VMEM((tm, 2 * dc), F32)], compiler_params=_params(),
    )(do, wout, zc, lg, lb)


def _wout_bwd_dw(y, do, nb, tk, name, comm=()):
    rows, k2 = y.shape
    d = do.shape[1]
    nk = rows // tk

    def body(y_ref, do_ref, dw_ref, acc):
        k = pl.program_id(0)

        @pl.when(k == 0)
        def _():
            acc[...] = jnp.zeros_like(acc)

        acc[...] += lax.dot_general(y_ref[...], do_ref[...], TN, preferred_element_type=F32)

        @pl.when(k == nk - 1)
        def _():
            dw_ref[...] = acc[...].astype(BF16)

    return _call(
        body, comm=comm, name=name, grid=(nk,),
        in_specs=[pl.BlockSpec((tk, k2), lambda k: (k, 0)), pl.BlockSpec((tk, d), lambda k: (k, 0))],
        out_specs=_once((k2, d), lambda k: (0, 0)), out_shape=_sds((k2, d), BF16),
        scratch_shapes=[pltpu.VMEM((k2, d), F32)], compiler_params=_params(),
    )(y, do).reshape(nb, k2 // nb, d)


def _windows(win, n_res):
    length = win.shape[0]
    return [win if r == 0 else pltpu.roll(win, length - r, 0) for r in range(n_res)]


def _taps(src_ref, start, offsets, ch):
    span = -(-(max(offsets) + ch) // SUBLANES) * SUBLANES
    win = src_ref[pl.ds(start, span), :]
    shifted = _windows(win, min(SUBLANES, max(offsets) + 1))
    return [shifted[o % SUBLANES][(o // SUBLANES) * SUBLANES:(o // SUBLANES) * SUBLANES + ch] for o in offsets]


def _rows8(v):
    return jnp.sum(v.reshape(v.shape[0] // SUBLANES, SUBLANES, v.shape[1]), axis=0)


def _conv_geometry(n_seq, n_meta):
    base = CONV_PAD + n_meta
    off_cf = base - (CF_WIDTH - 1)
    off_sc = base - (SC_WIDTH - 1)
    logical = -(-(n_meta + n_seq) // CONV_CH) * CONV_CH
    return base, off_cf, off_sc, logical


def _fill_conv_inputs(c_ref, v_ref, a_ref, g_ref, scv, sz, n_seq, n_meta):
    base = CONV_PAD + n_meta
    cb = scv.shape[1]
    scv[0:CONV_PAD, :] = jnp.zeros((CONV_PAD, cb), F32)
    sz[0:CONV_PAD, :] = jnp.zeros((CONV_PAD, cb), F32)

    def put(src, dst, n):
        cv = c_ref[src, :].astype(F32) * v_ref[src, :].astype(F32)
        scv[dst, :] = cv
        sz[dst, :] = a_ref[src, :].astype(F32) * _sigmoid(g_ref[src, :].astype(F32))

    put(pl.ds(n_seq, n_meta), pl.ds(CONV_PAD, n_meta), n_meta)

    def chunk(i, carry):
        t0 = pl.multiple_of(i * CONV_CH, CONV_CH)
        put(pl.ds(t0, CONV_CH), pl.ds(base + t0, CONV_CH), CONV_CH)
        return carry

    lax.fori_loop(0, n_seq // CONV_CH, chunk, 0)


def _conv_fwd(u, wsc, wcf, cbias, n_seq, n_meta, name, comm=()):
    rows = u.shape[0]
    nb, _, cb = wsc.shape
    dc = nb * cb
    base, off_cf, off_sc, _ = _conv_geometry(n_seq, n_meta)
    a_cf, a_sc = off_cf // SUBLANES * SUBLANES, off_sc // SUBLANES * SUBLANES
    ch = CONV_CH

    def body(b_ref, c_ref, v_ref, a_ref, g_ref, wsc_ref, wcf_ref, cb_ref, ysc_ref, zc_ref, scv, sz):
        _fill_conv_inputs(c_ref, v_ref, a_ref, g_ref, scv, sz, n_seq, n_meta)
        w3, w31, bias = wsc_ref[0], wcf_ref[0], cb_ref[...]

        def chunk(i, carry):
            t0 = pl.multiple_of(i * ch, ch)
            acc = jnp.zeros((ch, cb), F32)
            for k, win in enumerate(_taps(sz, t0 + a_cf, [off_cf - a_cf + k for k in range(CF_WIDTH)], ch)):
                acc = acc + win * w31[k:k + 1, :]
            zc_ref[pl.ds(t0, ch), :] = acc + bias
            s = jnp.zeros((ch, cb), F32)
            for k, win in enumerate(_taps(scv, t0 + a_sc, [off_sc - a_sc + k for k in range(SC_WIDTH)], ch)):
                s = s + win * w3[k:k + 1, :]
            ysc_ref[pl.ds(t0, ch), :] = (b_ref[pl.ds(t0, ch), :].astype(F32) * s).astype(BF16)
            return carry

        lax.fori_loop(0, n_seq // ch, chunk, 0)
        ysc_ref[n_seq:rows, :] = jnp.zeros((rows - n_seq, cb), BF16)
        zc_ref[n_seq:rows, :] = jnp.zeros((rows - n_seq, cb), F32)

    ucol = [pl.BlockSpec((rows, cb), functools.partial(lambda m, j: (0, m * nb + j), m)) for m in range(5)]
    blk = pl.BlockSpec((rows, cb), lambda j: (0, j))
    return _call(
        body, comm=comm, name=name, grid=(nb,),
        in_specs=ucol + [pl.BlockSpec((1, SC_WIDTH, cb), lambda j: (j, 0, 0)),
                         pl.BlockSpec((1, CF_WIDTH, cb), lambda j: (j, 0, 0)), pl.BlockSpec((1, cb), lambda j: (0, j))],
        out_specs=[blk, blk], out_shape=[_sds((rows, dc), BF16), _sds((rows, dc), F32)],
        scratch_shapes=[pltpu.VMEM((base + n_seq, cb), F32)] * 2, compiler_params=_params(),
    )(u, u, u, u, u, wsc, wcf, cbias)


def _conv_bwd(u, dysc, dzc, wsc, wcf, n_seq, n_meta, name, comm=()):
    rows = u.shape[0]
    nb, _, cb = wsc.shape
    dc = nb * cb
    base, off_cf, off_sc, logical = _conv_geometry(n_seq, n_meta)
    a_cf, a_sc = off_cf // SUBLANES * SUBLANES, off_sc // SUBLANES * SUBLANES
    ch = CONV_CH
    tail = CONV_PAD

    def body(b_ref, c_ref, v_ref, a_ref, g_ref, dysc_ref, dzc_ref, wsc_ref, wcf_ref,
             du_ref, dwsc_ref, dwcf_ref, dcb_ref,
             scv, sz, sds_, sdz, dlcv, dlz, accsc, acccf, accb):
        _fill_conv_inputs(c_ref, v_ref, a_ref, g_ref, scv, sz, n_seq, n_meta)
        w3, w31 = wsc_ref[0], wcf_ref[0]
        dub_ref, duc_ref, duv_ref, dua_ref, dug_ref = (du_ref.at[m] for m in range(5))
        sds_[0:n_meta, :] = jnp.zeros((n_meta, cb), F32)
        sdz[0:n_meta, :] = jnp.zeros((n_meta, cb), F32)
        behind = logical + tail - (n_meta + n_seq)
        sds_[n_meta + n_seq:logical + tail, :] = jnp.zeros((behind, cb), F32)
        sdz[n_meta + n_seq:logical + tail, :] = jnp.zeros((behind, cb), F32)
        accsc[...] = jnp.zeros_like(accsc)
        acccf[...] = jnp.zeros_like(acccf)
        accb[...] = jnp.zeros_like(accb)

        def forward_chunk(i, carry):
            t0 = pl.multiple_of(i * ch, ch)
            rws = pl.ds(t0, ch)
            dy = dysc_ref[rws, :].astype(F32)
            ds = dy * b_ref[rws, :].astype(F32)
            dz = dzc_ref[rws, :]
            sds_[pl.ds(n_meta + t0, ch), :] = ds
            sdz[pl.ds(n_meta + t0, ch), :] = dz
            s = jnp.zeros((ch, cb), F32)
            for k, win in enumerate(_taps(scv, t0 + a_sc, [off_sc - a_sc + k for k in range(SC_WIDTH)], ch)):
                s = s + win * w3[k:k + 1, :]
                accsc[k * SUBLANES:(k + 1) * SUBLANES, :] += _rows8(ds * win)
            dub_ref[rws, :] = (dy * s).astype(BF16)
            for k, win in enumerate(_taps(sz, t0 + a_cf, [off_cf - a_cf + k for k in range(CF_WIDTH)], ch)):
                acccf[k * SUBLANES:(k + 1) * SUBLANES, :] += _rows8(dz * win)
            accb[...] += _rows8(dz)
            return carry

        lax.fori_loop(0, n_seq // ch, forward_chunk, 0)

        def backward_chunk(i, carry):
            p0 = pl.multiple_of(i * ch, ch)
            dcv = jnp.zeros((ch, cb), F32)
            for k, win in enumerate(_taps(sds_, p0, [SC_WIDTH - 1 - k for k in range(SC_WIDTH)], ch)):
                dcv = dcv + win * w3[k:k + 1, :]
            dlcv[pl.ds(p0, ch), :] = dcv
            dzi = jnp.zeros((ch, cb), F32)
            for k, win in enumerate(_taps(sdz, p0, [CF_WIDTH - 1 - k for k in range(CF_WIDTH)], ch)):
                dzi = dzi + win * w31[k:k + 1, :]
            dlz[pl.ds(p0, ch), :] = dzi
            return carry

        lax.fori_loop(0, logical // ch, backward_chunk, 0)

        def gates(phys, logi):
            dcv, dzi = dlcv[logi, :], dlz[logi, :]
            duc_ref[phys, :] = (dcv * v_ref[phys, :].astype(F32)).astype(BF16)
            duv_ref[phys, :] = (dcv * c_ref[phys, :].astype(F32)).astype(BF16)
            s = _sigmoid(g_ref[phys, :].astype(F32))
            dua_ref[phys, :] = (dzi * s).astype(BF16)
            dug_ref[phys, :] = (dzi * a_ref[phys, :].astype(F32) * s * (1.0 - s)).astype(BF16)

        def gate_chunk(i, carry):
            t0 = pl.multiple_of(i * ch, ch)
            gates(pl.ds(t0, ch), pl.ds(n_meta + t0, ch))
            return carry

        lax.fori_loop(0, n_seq // ch, gate_chunk, 0)
        gates(pl.ds(n_seq, n_meta), pl.ds(0, n_meta))
        dub_ref[n_seq:rows, :] = jnp.zeros((rows - n_seq, cb), BF16)
        pad0 = n_seq + n_meta
        for ref in (duc_ref, duv_ref, dua_ref, dug_ref):
            ref[pad0:rows, :] = jnp.zeros((rows - pad0, cb), BF16)
        dwsc_ref[0] = jnp.sum(accsc[...].reshape(SC_WIDTH, SUBLANES, cb), axis=1)
        dwcf_ref[0] = jnp.sum(acccf[...].reshape(CF_WIDTH, SUBLANES, cb), axis=1)
        dcb_ref[...] = jnp.sum(accb[...], axis=0, keepdims=True)

    ucol = [pl.BlockSpec((rows, cb), functools.partial(lambda m, j: (0, m * nb + j), m)) for m in range(5)]
    blk = pl.BlockSpec((dysc.shape[0], cb), lambda j: (0, j))
    wsc_spec = pl.BlockSpec((1, SC_WIDTH, cb), lambda j: (j, 0, 0))
    wcf_spec = pl.BlockSpec((1, CF_WIDTH, cb), lambda j: (j, 0, 0))
    outs = _call(
        body, comm=comm, name=name, grid=(nb,),
        in_specs=ucol + [blk, blk, wsc_spec, wcf_spec],
        out_specs=[pl.BlockSpec((5, rows, cb), lambda j: (0, 0, j)), wsc_spec, wcf_spec,
                   pl.BlockSpec((1, cb), lambda j: (0, j))],
        out_shape=[_sds((5, rows, dc), BF16), _sds((nb, SC_WIDTH, cb), F32), _sds((nb, CF_WIDTH, cb), F32),
                   _sds((1, dc), F32)],
        scratch_shapes=[pltpu.VMEM((base + n_seq, cb), F32)] * 2 + [pltpu.VMEM((logical + tail, cb), F32)] * 2
        + [pltpu.VMEM((logical, cb), F32)] * 2
        + [pltpu.VMEM((SC_WIDTH * SUBLANES, cb), F32), pltpu.VMEM((CF_WIDTH * SUBLANES, cb), F32),
           pltpu.VMEM((SUBLANES, cb), F32)],
        compiler_params=_params(),
    )(u, u, u, u, u, dysc, dzc, wsc, wcf)
    return outs


def _row_tile(rows, cols):
    return rows // 4 if rows % 64 == 0 and rows * cols >= (1 << 18) else rows


def _pair_sum(grad, sib, idx, name, comm=()):
    _, rows, cols = grad.shape
    tr = rows

    def body(idx_ref, g_ref, s_ref, o_ref):
        o_ref[0] = (g_ref[0].astype(F32) + s_ref[0].astype(F32)).astype(o_ref.dtype)

    return _call(
        body, name=name,
        grid_spec=pltpu.PrefetchScalarGridSpec(
            num_scalar_prefetch=1, grid=(4, rows // tr),
            in_specs=[pl.BlockSpec((1, tr, cols), lambda k, i, idx_ref: (idx_ref[k], i, 0)),
                      pl.BlockSpec((1, tr, cols), lambda k, i, idx_ref: (idx_ref[4 + k], i, 0))],
            out_specs=pl.BlockSpec((1, tr, cols), lambda k, i, idx_ref: (k, i, 0))),
        out_shape=_sds((4, rows, cols), grad.dtype), compiler_params=_params(),
    )(idx, grad, sib)


def _adamw_math(w, g, m, v):
    m = ADAM_B1 * m + (1.0 - ADAM_B1) * g
    v = ADAM_B2 * v + (1.0 - ADAM_B2) * (g * g)
    m_hat = m / (1.0 - ADAM_B1 ** ADAM_STEP)
    v_hat = v / (1.0 - ADAM_B2 ** ADAM_STEP)
    delta = -ADAM_LR * (m_hat / (jnp.sqrt(v_hat) + ADAM_EPS) + ADAM_WD * w)
    return delta, m, v


def _adamw_sharded(own, got, w, m, v, name, comm=()):
    rows, cols = w.shape
    tr = _row_tile(rows, cols)

    def body(own_ref, g0_ref, g1_ref, g2_ref, w_ref, m_ref, v_ref, g_ref, d_ref, nm_ref, nv_ref):
        g = own_ref[0].astype(F32) + g0_ref[0].astype(F32) + g1_ref[0].astype(F32) + g2_ref[0].astype(F32)
        delta, nm, nv = _adamw_math(w_ref[...], g, m_ref[...], v_ref[...])
        g_ref[...] = g
        d_ref[...] = delta
        nm_ref[...] = nm
        nv_ref[...] = nv

    flat = pl.BlockSpec((tr, cols), lambda i: (i, 0))
    slot = [pl.BlockSpec((1, tr, cols), functools.partial(lambda k, i: (k, i, 0), k)) for k in range(3)]
    return _call(
        body, comm=comm, name=name, grid=(rows // tr,),
        in_specs=[slot[0]] + slot + [flat] * 3, out_specs=[flat] * 4, out_shape=[_sds((rows, cols), F32)] * 4,
        compiler_params=_params(),
    )(own, got, got, got, w, m, v)


def _adamw_replicated(gathered, segs, ws, ms, vs, loss_scale, name, comm=()):
    n = len(ws)

    def body(*refs):
        gat = refs[0]
        w_refs, m_refs, v_refs = refs[1:1 + n], refs[1 + n:1 + 2 * n], refs[1 + 2 * n:1 + 3 * n]
        outs = refs[1 + 3 * n:]

        def total(off, width):
            s = gat[0, :, off:off + width]
            for k in range(1, N_DEV):
                s = s + gat[k, :, off:off + width]
            return s

        outs[0][...] = loss_scale * total(segs[n][0], segs[n][1])
        for p in range(n):
            g = total(*segs[p])
            delta, nm, nv = _adamw_math(w_refs[p][...], g, m_refs[p][...], v_refs[p][...])
            for q, val in enumerate((g, delta, nm, nv)):
                outs[1 + 4 * p + q][...] = val
        for e, seg in enumerate(segs[n + 1:]):
            outs[1 + 4 * n + e][...] = total(*seg)

    return _call(
        body, name=name,
        out_shape=[_sds((1, segs[n][1]), F32)] + [_sds(w.shape, F32) for w in ws for _ in range(4)]
        + [_sds((1, seg[1]), F32) for seg in segs[n + 1:]],
        compiler_params=_params(),
    )(gathered, *ws, *ms, *vs)


def _adamw_plain(g, w, m, v, name):
    def body(g_ref, w_ref, m_ref, v_ref, d_ref, nm_ref, nv_ref):
        d_ref[...], nm_ref[...], nv_ref[...] = _adamw_math(w_ref[...], g_ref[...], m_ref[...], v_ref[...])

    return list(_call(body, name=name, out_shape=[_sds(w.shape, F32)] * 3)(g, w, m, v))


REPLICATED = ("ffn1_norm", "mix_norm", "b_in", "conv_cf_b", "ln_cf_g", "ln_cf_b", "ffn2_norm", "final_norm")
SHARDED = ("meta_tokens", "ffn1_w_gate", "ffn1_w_up", "ffn1_w_down", "w_in", "conv_sc_w", "conv_cf_w", "w_out",
           "ffn2_w_gate", "ffn2_w_up", "ffn2_w_down")
WEIGHTS = ("meta_tokens", "ffn1_norm", "ffn1_w_gate", "ffn1_w_up", "ffn1_w_down", "mix_norm", "w_in", "b_in",
           "conv_sc_w", "conv_cf_w", "conv_cf_b", "ln_cf_g", "ln_cf_b", "w_out", "ffn2_norm", "ffn2_w_gate",
           "ffn2_w_up", "ffn2_w_down", "final_norm")


TRANSPOSED = ("ffn1_w_gate", "ffn1_w_up", "ffn2_w_gate", "ffn2_w_up")


def _blocks2d(k, a):
    a = a.reshape(a.shape[-2:]) if a.ndim >= 2 else a.reshape(1, -1)
    return a.T if k in TRANSPOSED else a


def _step(x, tgt, w, m, v):
    n_seq, d = x.shape[1], x.shape[2]
    n_meta = w["meta_tokens"].shape[0]
    rows = -(-(n_seq + n_meta) // ROW_ALIGN) * ROW_ALIGN
    tm = rows // N_ROW_TILES
    cb = w["conv_sc_w"].shape[-1]
    dc = cb * N_DEV
    w2 = {k: _blocks2d(k, a) for k, a in w.items()}
    m2 = {k: _blocks2d(k, a) for k, a in m.items()}
    v2 = {k: _blocks2d(k, a) for k, a in v.items()}

    def as_given(k, r):
        return (r.T if k in TRANSPOSED else r).reshape(w[k].shape)

    def cast(k):
        return w2[k].astype(BF16)

    full = dict(zip(("ffn1_w_gate", "meta_tokens", "conv_sc_w", "conv_cf_w"), _exchange_alone(
        _gather_all([cast("ffn1_w_gate"), w2["meta_tokens"], w2["conv_sc_w"], w2["conv_cf_w"]]), "gather_gate1")))
    meta = jnp.transpose(full["meta_tokens"], (1, 0, 2)).reshape(n_meta, d)
    hs0 = jnp.concatenate([x[0], meta, jnp.zeros((rows - n_seq - n_meta, d), F32)], axis=0)

    tx = n_seq // N_ROW_TILES
    h1, r1 = _rms_fwd(hs0, w2["ffn1_norm"], tm, "rms_in")
    up1 = _gather_all([cast("ffn1_w_up")])
    g1 = _ffn_proj(h1, full["ffn1_w_gate"], tm, "ffn1_gate", comm=[up1])
    down1 = _gather_all([cast("ffn1_w_down")])
    u1 = _ffn_proj(h1, up1.results[0], tm, "ffn1_up", comm=[down1])
    w_in_all, w_out1 = _gather_all([cast("w_in")]), _gather_direct([cast("w_out")])
    hs1, h2, r2 = _ffn_down_norm(g1, u1, down1.results[0], hs0, w2["mix_norm"], tm, "ffn1_down", comm=[w_in_all, w_out1])
    win = jnp.transpose(w_in_all.results[0], (1, 0, 2)).reshape(d, -1)
    w_out2, gate2 = _gather_forward(w_out1.results), _gather_direct([cast("ffn2_w_gate")])
    u = _win_fwd(h2, win, w2["b_in"], dc, tm, "mix_in", comm=[w_out2, gate2])
    wout = w_out2.results[0].reshape(-1, d)
    gate2p = _gather_forward(gate2.results)
    ysc, zc = _conv_fwd(u, full["conv_sc_w"], full["conv_cf_w"], w2["conv_cf_b"], n_seq, n_meta, "conv_fwd", comm=[gate2p])
    up2 = _gather_all([cast("ffn2_w_up")])
    y, hs2, h3, r3 = _wout_fwd(zc, ysc, wout, hs1, w2["ln_cf_g"], w2["ln_cf_b"], w2["ffn2_norm"], n_seq, tx, "mix_out",
                               comm=[up2])
    down2 = _gather_all([cast("ffn2_w_down")])
    full.update(ffn1_w_up=up1.results[0], ffn1_w_down=down1.results[0], ffn2_w_gate=gate2p.results[0],
                ffn2_w_up=up2.results[0])
    g2, u2 = _ffn_gu(h3, full["ffn2_w_gate"], full["ffn2_w_up"], tx, "ffn2_gu", comm=[down2])
    full["ffn2_w_down"] = down2.results[0]
    dhs3, dhs3b, loss_p, dgf_p = _ffn_down_loss(
        g2, u2, full["ffn2_w_down"], hs2, w2["final_norm"], tgt[0], n_seq, tx, "ffn2_down_loss")

    dg2, du2, a2 = _ffn_bwd_da(dhs3b, full["ffn2_w_down"], g2, u2, tx, "ffn2_bwd_da")
    dhs2, dhs2b, dn3_p = _ffn_bwd_dh(
        dg2, du2, full["ffn2_w_gate"], full["ffn2_w_up"], hs2, r3, w2["ffn2_norm"], dhs3, tx, "ffn2_bwd_dh",
        zeros=jnp.zeros((rows, d), F32))
    xi, yi, ci = lax.axis_index("x"), lax.axis_index("y"), lax.axis_index("c")
    chip_of = [2 * xi + yi, 2 * (1 - xi) + yi, 2 * xi + (1 - yi), 2 * (1 - xi) + (1 - yi)]
    idx = jnp.stack([2 * ch + ci for ch in chip_of] + chip_of).astype(jnp.int32)
    out = {}

    def to_pairs(parts):
        return _pair_exchange([p.reshape((4, 2) + p.shape[1:]) for p in parts])

    def pair_sums(names, parts, pairs):
        return [_pair_sum(p, s, idx, "pair_sum_" + k) for k, p, s in zip(names, parts, pairs.results)]

    def update(names, sums, chips):
        for k, own, got in zip(names, sums, chips.results):
            res = _adamw_sharded(own, got, w2[k], m2[k], v2[k], "adamw_" + k)
            out[k] = [as_given(k, r) for r in res]

    ffn2 = ("ffn2_w_gate", "ffn2_w_up", "ffn2_w_down")
    tk = ROW_ALIGN
    grads2 = _ffn_bwd_dw(h3, dg2, du2, a2, dhs3b, tk, "ffn2_bwd_dw")
    pairs2 = to_pairs(grads2)
    dysc, dzc, dlg_p, dlb_p = _wout_bwd_dy(dhs2b, wout, zc, w2["ln_cf_g"], w2["ln_cf_b"], tx, "mix_out_bwd_dy",
                                           comm=[pairs2])
    sums2 = pair_sums(ffn2, grads2, pairs2)
    dwout = _wout_bwd_dw(y, dhs2b, N_DEV, tk, "mix_out_bwd_dw")
    chips2 = [_chip_exchange([s]) for s in sums2]
    du, dcsw, dccw, dcb = _conv_bwd(u, dysc, dzc, full["conv_sc_w"], full["conv_cf_w"], n_seq, n_meta, "conv_bwd",
                                    comm=[chips2[0]])
    dhs1, dhs1b, dn2_p = _win_bwd_dh(du, win, hs1, r2, w2["mix_norm"], dhs2, tm, "mix_in_bwd_dh", comm=[chips2[1]])
    dwin, dbin = _win_bwd_dw(h2, du, tk, "mix_in_bwd_dw", comm=[chips2[2]])
    for k, s, ch in zip(ffn2, sums2, chips2):
        update([k], [s], ch)

    mixer = ("w_out", "w_in", "conv_sc_w", "conv_cf_w")
    gradsm = [dwout, jnp.transpose(dwin.reshape(d, N_DEV, -1), (1, 0, 2)), dcsw, dccw]
    pairsm = to_pairs(gradsm)
    dg1, du1, a1 = _ffn_bwd_da(dhs1b, full["ffn1_w_down"], g1, u1, tm, "ffn1_bwd_da", comm=[pairsm])
    sumsm = pair_sums(mixer, gradsm, pairsm)
    chipsm = _chip_exchange(sumsm)
    ffn1 = ("ffn1_w_gate", "ffn1_w_up", "ffn1_w_down")
    grads1 = _ffn_bwd_dw(h1, dg1, du1, a1, dhs1b, tk, "ffn1_bwd_dw", comm=[chipsm])
    update(mixer, sumsm, chipsm)
    pairs1 = to_pairs(grads1)
    _exchange_alone(pairs1, "reduce_pair_ffn1")
    sums1 = pair_sums(ffn1, grads1, pairs1)
    chips1 = _chip_exchange(sums1)
    dhs0, _, dn1_p = _ffn_bwd_dh(
        dg1, du1, full["ffn1_w_gate"], full["ffn1_w_up"], hs0, r1, w2["ffn1_norm"], dhs1, tm, "ffn1_bwd_dh", comm=[chips1])
    update(ffn1, sums1, chips1)
    grad_x = dhs0[:n_seq][None]

    partial = {
        "ffn1_norm": dn1_p.sum(0), "mix_norm": dn2_p.sum(0), "b_in": dbin, "conv_cf_b": dcb,
        "ln_cf_g": dlg_p.sum(0), "ln_cf_b": dlb_p.sum(0), "ffn2_norm": dn3_p.sum(0), "final_norm": dgf_p.sum(0),
    }
    loss_seg = jnp.pad(loss_p.sum((0, 2)).reshape(1, 1), ((0, 0), (0, 127)))
    pieces = [partial[k] for k in REPLICATED] + [loss_seg, dhs0[n_seq:n_seq + n_meta].reshape(1, n_meta * d)]
    segs, off = [], 0
    for p in pieces:
        segs.append((off, p.shape[1]))
        off += p.shape[1]
    rows8 = _gather_rows(jnp.concatenate(pieces, axis=1), "gather_small")
    res = _adamw_replicated(rows8, segs, [w2[k] for k in REPLICATED], [m2[k] for k in REPLICATED],
                            [v2[k] for k in REPLICATED], 0.5 / d, "adamw_replicated")
    loss = res[0][0, 0]
    for p, k in enumerate(REPLICATED):
        out[k] = [r.reshape(w[k].shape) for r in res[1 + 4 * p:5 + 4 * p]]
    ob = w2["meta_tokens"].shape[1]
    gmeta = lax.dynamic_slice_in_dim(res[-1].reshape(n_meta, d), (4 * xi + 2 * yi + ci) * ob, ob, axis=1)
    out["meta_tokens"] = [gmeta] + _adamw_plain(gmeta, w2["meta_tokens"], m2["meta_tokens"], v2["meta_tokens"], "adamw_meta_tokens")

    return (loss, grad_x, *[out[k][0] for k in WEIGHTS], *[out[k][1] for k in WEIGHTS],
            *[out[k][2] for k in WEIGHTS], *[out[k][3] for k in WEIGHTS])


def kernel(x, meta_tokens, ffn1_norm, ffn1_w_gate, ffn1_w_up, ffn1_w_down, mix_norm, w_in, b_in, conv_sc_w, conv_cf_w, conv_cf_b, ln_cf_g, ln_cf_b, w_out, ffn2_norm, ffn2_w_gate, ffn2_w_up, ffn2_w_down, final_norm, loss_target, m_meta_tokens, m_ffn1_norm, m_ffn1_w_gate, m_ffn1_w_up, m_ffn1_w_down, m_mix_norm, m_w_in, m_b_in, m_conv_sc_w, m_conv_cf_w, m_conv_cf_b, m_ln_cf_g, m_ln_cf_b, m_w_out, m_ffn2_norm, m_ffn2_w_gate, m_ffn2_w_up, m_ffn2_w_down, m_final_norm, v_meta_tokens, v_ffn1_norm, v_ffn1_w_gate, v_ffn1_w_up, v_ffn1_w_down, v_mix_norm, v_w_in, v_b_in, v_conv_sc_w, v_conv_cf_w, v_conv_cf_b, v_ln_cf_g, v_ln_cf_b, v_w_out, v_ffn2_norm, v_ffn2_w_gate, v_ffn2_w_up, v_ffn2_w_down, v_final_norm):
    given = dict(locals())
    w = {k: given[k] for k in WEIGHTS}
    m = {k: given["m_" + k] for k in WEIGHTS}
    v = {k: given["v_" + k] for k in WEIGHTS}
    return _step(x, loss_target, w, m, v)
```

```python
import functools

import jax
import jax.numpy as jnp
from jax import lax
from jax.experimental import pallas as pl
from jax.experimental.pallas import tpu as pltpu

F32 = jnp.float32
BF16 = jnp.bfloat16
EPS = 1e-6
FFN_RES_SCALE = 0.5
SC_WIDTH = 3
CF_WIDTH = 31
ADAM_LR = 0.001
ADAM_B1 = 0.9
ADAM_B2 = 0.999
ADAM_EPS = 1e-08
ADAM_WD = 0.01
ADAM_STEP = 10

N_DEV = 8
N_ROW_TILES = 8
ROW_ALIGN = 256
CONV_PAD = 32
CONV_CH = 128
SUB_ROWS = 32
DOWN_BLOCKS = 2
SUBLANES = 8
BF16_ROWS = 16
VMEM_LIMIT = 56 * 1024 * 1024
MESH = pl.DeviceIdType.MESH
ANY = pl.BlockSpec(memory_space=pl.ANY)

NT = (((1,), (1,)), ((), ()))
TN = (((0,), (0,)), ((), ()))


def _pallas(body, **kw):
    return pl.pallas_call(body, **kw)


class _Exchange:
    def __init__(self, inputs, out_shapes, sem_shapes, start, finish, aliases=None):
        self.inputs, self.out_shapes, self.sem_shapes = list(inputs), list(out_shapes), list(sem_shapes)
        self.start, self.finish, self.aliases = start, finish, dict(aliases or {})
        self.results = None


def _call(body, comm=(), **kw):
    if not comm:
        return _pallas(body, **kw)
    grid = kw.pop("grid")
    in_specs = list(kw.pop("in_specs"))
    out_specs, out_shape = kw.pop("out_specs"), kw.pop("out_shape")
    scratch = list(kw.pop("scratch_shapes", []))
    single = not isinstance(out_shape, (list, tuple))
    out_specs, out_shape = ([out_specs], [out_shape]) if single else (list(out_specs), list(out_shape))
    n_in, n_out, n_scr = len(in_specs), len(out_shape), len(scratch)
    c_in = [a for job in comm for a in job.inputs]
    c_out = [s for job in comm for s in job.out_shapes]
    c_sem = [s for job in comm for s in job.sem_shapes]
    aliases, i0, o0 = dict(kw.pop("input_output_aliases", {})), n_in, n_out
    for job in comm:
        aliases.update({i0 + i: o0 + o for i, o in job.aliases.items()})
        i0, o0 = i0 + len(job.inputs), o0 + len(job.out_shapes)

    def hosted(*refs):
        pos = [0]

        def take(n):
            pos[0] += n
            return refs[pos[0] - n:pos[0]]

        ins, cins, outs, couts, scr, sems = take(n_in), take(len(c_in)), take(n_out), take(len(c_out)), take(n_scr), take(len(c_sem))
        ids = [pl.program_id(k) for k in range(len(grid))]
        first = functools.reduce(jnp.logical_and, [i == 0 for i in ids])
        last = functools.reduce(jnp.logical_and, [i == g - 1 for i, g in zip(ids, grid)])

        def each(phase):
            i, o, s = 0, 0, 0
            for job in comm:
                ni, no, ns = len(job.inputs), len(job.out_shapes), len(job.sem_shapes)
                getattr(job, phase)(cins[i:i + ni], couts[o:o + no], sems[s:s + ns])
                i, o, s = i + ni, o + no, s + ns

        @pl.when(first)
        def _():
            each("start")

        body(*ins, *outs, *scr)

        @pl.when(last)
        def _():
            each("finish")

    call = _pallas(
        hosted, grid=grid, in_specs=in_specs + [ANY] * len(c_in), out_specs=out_specs + [ANY] * len(c_out),
        out_shape=out_shape + c_out, scratch_shapes=scratch + c_sem, input_output_aliases=aliases, **kw)

    def run(*args):
        res = call(*args, *c_in)
        o = n_out
        for job in comm:
            job.results = list(res[o:o + len(job.out_shapes)])
            o += len(job.out_shapes)
        return res[0] if single else list(res[:n_out])

    return run


def _exchange_alone(job, name, comm=()):
    n_in, n_out = len(job.inputs), len(job.out_shapes)

    def body(*refs):
        ins, outs, sems = refs[:n_in], refs[n_in:n_in + n_out], refs[n_in + n_out:]
        job.start(ins, outs, sems)
        job.finish(ins, outs, sems)

    res = _pallas(body, name=name, in_specs=[ANY] * n_in, out_specs=[ANY] * n_out, out_shape=job.out_shapes,
                  scratch_shapes=job.sem_shapes, input_output_aliases=job.aliases)(*job.inputs)
    job.results = list(res)
    return job.results


def _params(**kw):
    return pltpu.CompilerParams(vmem_limit_bytes=VMEM_LIMIT, **kw)


def _sigmoid(x):
    return 0.5 * jnp.tanh(0.5 * x) + 0.5


def _sds(shape, dtype):
    return jax.ShapeDtypeStruct(shape, dtype)


def _place():
    x, y, c = lax.axis_index("x"), lax.axis_index("y"), lax.axis_index("c")
    chips = [(1 - x, y), (x, 1 - y), (1 - x, 1 - y)]
    return x, y, c, chips


def _slot(ref, p):
    return ref.at[4 * p[0] + 2 * p[1] + p[2]]


def _remote(src, dst, send_sem, recv_sem, to):
    return pltpu.make_async_remote_copy(src_ref=src, dst_ref=dst, send_sem=send_sem, recv_sem=recv_sem,
                                        device_id=to, device_id_type=MESH)


def _gather_direct(arrs):
    n = len(arrs)

    def copies(ins, outs, sems):
        send_sems, recv_sems, local_sems = sems
        x, y, c, chips = _place()
        me = (x, y, c)
        peers = [(x, y, 1 - c)] + [(*chip, c) for chip in chips]
        local = [pltpu.make_async_copy(ins[a], _slot(outs[a], me), local_sems.at[a]) for a in range(n)]
        sends = [_remote(ins[a], _slot(outs[a], me), send_sems.at[a, k], recv_sems.at[a, k], peer)
                 for a in range(n) for k, peer in enumerate(peers)]
        arrivals = [_remote(ins[a], _slot(outs[a], peer), send_sems.at[a, k], recv_sems.at[a, k], peer)
                    for a in range(n) for k, peer in enumerate(peers)]
        return local, sends, arrivals

    def start(ins, outs, sems):
        local, sends, _ = copies(ins, outs, sems)
        for cp in local + sends:
            cp.start()

    def finish(ins, outs, sems):
        local, sends, arrivals = copies(ins, outs, sems)
        for cp in arrivals:
            cp.wait_recv()
        for cp in sends:
            cp.wait_send()
        for cp in local:
            cp.wait()

    dma = pltpu.SemaphoreType.DMA
    return _Exchange(arrs, [_sds((N_DEV,) + a.shape, a.dtype) for a in arrs], [dma((n, 4)), dma((n, 4)), dma((n,))],
                     start, finish)


def _gather_forward(gathered):
    n = len(gathered)

    def copies(ins, outs, sems):
        send_sems, recv_sems = sems
        x, y, c, chips = _place()
        sibling = (x, y, 1 - c)
        sends = [_remote(_slot(ins[a], (*chip, c)), _slot(outs[a], (*chip, c)), send_sems.at[a, j], recv_sems.at[a, j], sibling)
                 for a in range(n) for j, chip in enumerate(chips)]
        arrivals = [_remote(_slot(ins[a], (*chip, c)), _slot(outs[a], (*chip, 1 - c)), send_sems.at[a, j], recv_sems.at[a, j], sibling)
                    for a in range(n) for j, chip in enumerate(chips)]
        return sends, arrivals

    def start(ins, outs, sems):
        for cp in copies(ins, outs, sems)[0]:
            cp.start()

    def finish(ins, outs, sems):
        sends, arrivals = copies(ins, outs, sems)
        for cp in arrivals:
            cp.wait_recv()
        for cp in sends:
            cp.wait_send()

    dma = pltpu.SemaphoreType.DMA
    return _Exchange(gathered, [_sds(a.shape, a.dtype) for a in gathered], [dma((n, 3)), dma((n, 3))], start, finish,
                     aliases={a: a for a in range(n)})


def _gather_all(arrs):
    direct = _gather_direct(arrs)
    n = len(arrs)
    n_sems = len(direct.sem_shapes)
    dma = pltpu.SemaphoreType.DMA

    def start(ins, outs, sems):
        direct.start(ins, outs, sems[:n_sems])

    def finish(ins, outs, sems):
        direct.finish(ins, outs, sems[:n_sems])
        passed = _gather_forward(outs)
        passed.start(outs, outs, sems[n_sems:])
        passed.finish(outs, outs, sems[n_sems:])

    return _Exchange(arrs, direct.out_shapes, direct.sem_shapes + [dma((n, 3)), dma((n, 3))], start, finish)


def _pair_exchange(arrs):
    n = len(arrs)

    def copies(ins, outs, sems):
        x, y, c, _ = _place()
        return [_remote(ins[a].at[:, 1 - c], outs[a], sems[0].at[a], sems[1].at[a], (x, y, 1 - c)) for a in range(n)]

    def start(ins, outs, sems):
        for cp in copies(ins, outs, sems):
            cp.start()

    def finish(ins, outs, sems):
        for cp in copies(ins, outs, sems):
            cp.wait()

    dma = pltpu.SemaphoreType.DMA
    return _Exchange(arrs, [_sds((4,) + a.shape[2:], a.dtype) for a in arrs], [dma((n,)), dma((n,))], start, finish)


def _chip_exchange(arrs):
    n = len(arrs)

    def copies(ins, outs, sems):
        x, y, c, chips = _place()
        return [_remote(ins[a].at[1 + j], outs[a].at[j], sems[0].at[a, j], sems[1].at[a, j], (*chip, c))
                for a in range(n) for j, chip in enumerate(chips)]

    def start(ins, outs, sems):
        for cp in copies(ins, outs, sems):
            cp.start()

    def finish(ins, outs, sems):
        for cp in copies(ins, outs, sems):
            cp.wait()

    dma = pltpu.SemaphoreType.DMA
    return _Exchange(arrs, [_sds((3,) + a.shape[1:], a.dtype) for a in arrs], [dma((n, 3)), dma((n, 3))], start, finish)


def _gather_rows(vec, name, comm=()):
    def body(in_ref, out_ref, send_sems, recv_sems, local_sem):
        x, y, c, _ = _place()
        me = 4 * x + 2 * y + c
        mine = pltpu.make_async_copy(in_ref, out_ref.at[me], local_sem)
        mine.start()
        copies = []
        for k in range(1, N_DEV):
            to = (1 - x if k & 4 else x, 1 - y if k & 2 else y, 1 - c if k & 1 else c)
            copies.append(pltpu.make_async_remote_copy(
                src_ref=in_ref, dst_ref=out_ref.at[me], send_sem=send_sems.at[k - 1], recv_sem=recv_sems.at[k - 1],
                device_id=to, device_id_type=MESH))
        for cp in copies:
            cp.start()
        for cp in copies:
            cp.wait()
        mine.wait()

    return _call(
        body, name=name, out_shape=_sds((N_DEV,) + vec.shape, vec.dtype), in_specs=[ANY], out_specs=ANY,
        scratch_shapes=[pltpu.SemaphoreType.DMA((7,)), pltpu.SemaphoreType.DMA((7,)), pltpu.SemaphoreType.DMA],
    )(vec)


def _for_row_groups(tm, fn):
    def step(i, carry):
        fn(pl.ds(pl.multiple_of(i * SUB_ROWS, SUB_ROWS), SUB_ROWS), i * SUB_ROWS)
        return carry

    lax.fori_loop(0, tm // SUB_ROWS, step, 0)


def _row_parts(tm, n=4):
    units = tm // BF16_ROWS
    sizes = [(units // n + (1 if i < units % n else 0)) * BF16_ROWS for i in range(n)]
    starts = [sum(sizes[:i]) for i in range(n)]
    return [slice(s, s + z) for s, z in zip(starts, sizes) if z]


def _once(shape, index_map):
    return pl.BlockSpec(shape, index_map, pipeline_mode=pl.Buffered(1))


def _rms_fwd(hs, g, tm, name, comm=()):
    rows, d = hs.shape

    def body(hs_ref, g_ref, h_ref, r_ref):
        def group(rs, _):
            xv = hs_ref[rs, :]
            r = lax.rsqrt(jnp.mean(xv * xv, axis=-1, keepdims=True) + EPS)
            h_ref[rs, :] = (xv * r * g_ref[...]).astype(BF16)
            r_ref[rs, :] = r

        _for_row_groups(tm, group)

    return _call(
        body, comm=comm, name=name, grid=(rows // tm,),
        in_specs=[pl.BlockSpec((tm, d), lambda i: (i, 0)), pl.BlockSpec((1, d), lambda i: (0, 0))],
        out_specs=[pl.BlockSpec((tm, d), lambda i: (i, 0)), pl.BlockSpec((tm, 1), lambda i: (i, 0))],
        out_shape=[_sds((rows, d), BF16), _sds((rows, 1), F32)], compiler_params=_params(),
    )(hs, g)


def _ffn_gu(h, wg, wu, tm, name, comm=()):
    rows, d = h.shape
    nb, fb, _ = wg.shape

    def body(h_ref, wg_ref, wu_ref, g_ref, u_ref):
        hv = h_ref[...]
        g_ref[0] = lax.dot_general(hv, wg_ref[0], NT, preferred_element_type=F32).astype(BF16)
        u_ref[0] = lax.dot_general(hv, wu_ref[0], NT, preferred_element_type=F32).astype(BF16)

    wspec = pl.BlockSpec((1, fb, d), lambda j, i: (j, 0, 0))
    ospec = pl.BlockSpec((1, tm, fb), lambda j, i: (j, i, 0))
    return _call(
        body, comm=comm, name=name, grid=(nb, rows // tm),
        in_specs=[pl.BlockSpec((tm, d), lambda j, i: (i, 0)), wspec, wspec],
        out_specs=[ospec, ospec], out_shape=[_sds((nb, rows, fb), BF16)] * 2, compiler_params=_params(),
    )(h, wg, wu)


def _ffn_proj(h, w, tm, name, comm=()):
    rows, d = h.shape
    nb, fb, _ = w.shape

    def body(h_ref, w_ref, p_ref):
        p_ref[0] = lax.dot_general(h_ref[...], w_ref[0], NT, preferred_element_type=F32).astype(BF16)

    return _call(
        body, comm=comm, name=name, grid=(nb, rows // tm),
        in_specs=[pl.BlockSpec((tm, d), lambda j, i: (i, 0)), pl.BlockSpec((1, fb, d), lambda j, i: (j, 0, 0))],
        out_specs=pl.BlockSpec((1, tm, fb), lambda j, i: (j, i, 0)), out_shape=_sds((nb, rows, fb), BF16),
        compiler_params=_params(),
    )(h, w)


def _ffn_down_norm(g, u, wd, hs, gn, tm, name, comm=()):
    nb, rows, fb = g.shape
    d = hs.shape[1]

    def body(g_ref, u_ref, wd_ref, hs_ref, gn_ref, hsn_ref, hn_ref, rn_ref, acc_ref):
        j = pl.program_id(1)

        @pl.when(j == 0)
        def _():
            acc_ref[...] = jnp.zeros_like(acc_ref)

        part = None
        for b in range(DOWN_BLOCKS):
            gv = g_ref[b]
            a = gv * _sigmoid(gv) * u_ref[b]
            p = jnp.dot(a, wd_ref[b], preferred_element_type=F32)
            part = p if part is None else part + p
        acc_ref[...] += part

        @pl.when(j == nb // DOWN_BLOCKS - 1)
        def _():
            def group(rs, _):
                hsn = hs_ref[rs, :] + FFN_RES_SCALE * acc_ref[rs, :]
                r = lax.rsqrt(jnp.mean(hsn * hsn, axis=-1, keepdims=True) + EPS)
                hsn_ref[rs, :] = hsn
                hn_ref[rs, :] = (hsn * r * gn_ref[...]).astype(BF16)
                rn_ref[rs, :] = r

            _for_row_groups(tm, group)

    aspec = pl.BlockSpec((DOWN_BLOCKS, tm, fb), lambda i, j: (j, i, 0))
    row = pl.BlockSpec((tm, d), lambda i, j: (i, 0))
    return _call(
        body, comm=comm, name=name, grid=(rows // tm, nb // DOWN_BLOCKS),
        in_specs=[aspec, aspec, pl.BlockSpec((DOWN_BLOCKS, fb, d), lambda i, j: (j, 0, 0)),
                  _once((tm, d), lambda i, j: (i, 0)), pl.BlockSpec((1, d), lambda i, j: (0, 0))],
        out_specs=[row, row, pl.BlockSpec((tm, 1), lambda i, j: (i, 0))],
        out_shape=[_sds((rows, d), F32), _sds((rows, d), BF16), _sds((rows, 1), F32)],
        scratch_shapes=[pltpu.VMEM((tm, d), F32)], compiler_params=_params(),
    )(g, u, wd, hs, gn)


def _ffn_down_loss(g, u, wd, hs, gf, tgt, n_seq, tm, name, comm=()):
    nb, rows, fb = g.shape
    d = hs.shape[1]
    nt = rows // tm

    def body(g_ref, u_ref, wd_ref, hs_ref, gf_ref, tgt_ref, dhs_ref, dhsb_ref, loss_ref, dgf_ref, acc_ref):
        i, j = pl.program_id(0), pl.program_id(1)

        @pl.when(j == 0)
        def _():
            acc_ref[...] = jnp.zeros_like(acc_ref)

        part = None
        for b in range(DOWN_BLOCKS):
            gv = g_ref[b]
            a = gv * _sigmoid(gv) * u_ref[b]
            p = jnp.dot(a, wd_ref[b], preferred_element_type=F32)
            part = p if part is None else part + p
        acc_ref[...] += part

        @pl.when(j == nb // DOWN_BLOCKS - 1)
        def _():
            loss_ref[0] = jnp.zeros((1, d), F32)
            dgf_ref[0] = jnp.zeros((1, d), F32)

            def group(rs, r0):
                hs3 = hs_ref[rs, :] + FFN_RES_SCALE * acc_ref[rs, :]
                r = lax.rsqrt(jnp.mean(hs3 * hs3, axis=-1, keepdims=True) + EPS)
                gfv = gf_ref[...]
                y = hs3 * r
                rowid = i * tm + r0 + lax.broadcasted_iota(jnp.int32, (SUB_ROWS, 1), 0)
                err = jnp.where(rowid < n_seq, y * gfv - tgt_ref[rs, :], 0.0)
                loss_ref[0] += jnp.sum(err * err, axis=0, keepdims=True)
                dout = err * (1.0 / d)
                dgf_ref[0] += jnp.sum(dout * y, axis=0, keepdims=True)
                t = dout * gfv
                dhs = r * t - hs3 * (r * r * r) * jnp.mean(t * hs3, axis=-1, keepdims=True)
                dhs_ref[rs, :] = dhs
                dhsb_ref[rs, :] = dhs.astype(BF16)

            _for_row_groups(tm, group)

    aspec = pl.BlockSpec((DOWN_BLOCKS, tm, fb), lambda i, j: (j, i, 0))
    row = pl.BlockSpec((tm, d), lambda i, j: (i, 0))
    once = _once((tm, d), lambda i, j: (i, 0))
    part = pl.BlockSpec((1, 1, d), lambda i, j: (i, 0, 0))
    return _call(
        body, comm=comm, name=name, grid=(nt, nb // DOWN_BLOCKS),
        in_specs=[aspec, aspec, pl.BlockSpec((DOWN_BLOCKS, fb, d), lambda i, j: (j, 0, 0)), once,
                  pl.BlockSpec((1, d), lambda i, j: (0, 0)), once],
        out_specs=[row, row, part, part],
        out_shape=[_sds((rows, d), F32), _sds((rows, d), BF16), _sds((nt, 1, d), F32), _sds((nt, 1, d), F32)],
        scratch_shapes=[pltpu.VMEM((tm, d), F32)], compiler_params=_params(),
    )(g, u, wd, hs, gf, tgt)


def _ffn_bwd_da(do, wd, g, u, tm, name, comm=()):
    nb, rows, fb = g.shape
    d = do.shape[1]

    def body(do_ref, wd_ref, g_ref, u_ref, dg_ref, du_ref, a_ref):
        da = (FFN_RES_SCALE * lax.dot_general(do_ref[...], wd_ref[0], NT, preferred_element_type=F32)).astype(BF16)
        gv, uv = g_ref[0], u_ref[0]
        s = _sigmoid(gv)
        sg = gv * s
        a_ref[0] = sg * uv
        du_ref[0] = da * sg
        dg_ref[0] = da * uv * (s + sg * (1.0 - s))

    aspec = pl.BlockSpec((1, tm, fb), lambda j, i: (j, i, 0))
    return _call(
        body, comm=comm, name=name, grid=(nb, rows // tm),
        in_specs=[pl.BlockSpec((tm, d), lambda j, i: (i, 0)), pl.BlockSpec((1, fb, d), lambda j, i: (j, 0, 0)), aspec, aspec],
        out_specs=[aspec] * 3, out_shape=[_sds((nb, rows, fb), BF16)] * 3, compiler_params=_params(),
    )(do, wd, g, u)


def _rms_bwd(tm, d, dh_ref, hs_ref, r_ref, gn_ref, dres_ref, dhs_ref, dhsb_ref, dgn_ref):
    dgn_ref[0] = jnp.zeros((1, d), F32)

    def group(rs, _):
        dh, hs, r = dh_ref[rs, :], hs_ref[rs, :], r_ref[rs, :]
        dgn_ref[0] += jnp.sum(dh * (hs * r), axis=0, keepdims=True)
        t = dh * gn_ref[...]
        dhs = dres_ref[rs, :] + r * t - hs * (r * r * r) * jnp.mean(t * hs, axis=-1, keepdims=True)
        dhs_ref[rs, :] = dhs
        dhsb_ref[rs, :] = dhs.astype(BF16)

    _for_row_groups(tm, group)


def _ffn_bwd_dh(dg, du, wg, wu, hs, r, gn, dres, tm, name, comm=(), zeros=None):
    nb, rows, fb = dg.shape
    d = hs.shape[1]
    nt = rows // tm
    extra = [] if zeros is None else [zeros]

    def body(dg_ref, du_ref, wg_ref, wu_ref, hs_ref, r_ref, gn_ref, dres_ref, *rest):
        dhs_ref, dhsb_ref, dgn_ref, acc_ref = rest[len(extra):]
        j = pl.program_id(1)

        @pl.when(j == 0)
        def _():
            acc_ref[...] = jnp.zeros_like(acc_ref)

        acc_ref[...] += (jnp.dot(dg_ref[0], wg_ref[0], preferred_element_type=F32)
                         + jnp.dot(du_ref[0], wu_ref[0], preferred_element_type=F32))

        @pl.when(j == nb - 1)
        def _():
            _rms_bwd(tm, d, acc_ref, hs_ref, r_ref, gn_ref, dres_ref, dhs_ref, dhsb_ref, dgn_ref)

    aspec = pl.BlockSpec((1, tm, fb), lambda i, j: (j, i, 0))
    wspec = pl.BlockSpec((1, fb, d), lambda i, j: (j, 0, 0))
    row = pl.BlockSpec((tm, d), lambda i, j: (i, 0))
    once = _once((tm, d), lambda i, j: (i, 0))
    return _call(
        body, comm=comm, name=name, grid=(nt, nb),
        in_specs=[aspec, aspec, wspec, wspec, once, pl.BlockSpec((tm, 1), lambda i, j: (i, 0)),
                  pl.BlockSpec((1, d), lambda i, j: (0, 0)), once] + [ANY] * len(extra),
        out_specs=[row, row, pl.BlockSpec((1, 1, d), lambda i, j: (i, 0, 0))],
        out_shape=[_sds((rows, d) if zeros is None else zeros.shape, F32), _sds((rows, d), BF16), _sds((nt, 1, d), F32)],
        scratch_shapes=[pltpu.VMEM((tm, d), F32)], compiler_params=_params(),
        input_output_aliases={8: 0} if extra else {},
    )(dg, du, wg, wu, hs, r, gn, dres, *extra)


def _ffn_bwd_dw(h, dg, du, a, do, tk, name, comm=()):
    nb, rows, fb = dg.shape
    d = h.shape[1]
    nk = rows // tk

    def body(h_ref, dg_ref, du_ref, a_ref, do_ref, dwg_ref, dwu_ref, dwd_ref, accg, accu, accd):
        k = pl.program_id(1)

        @pl.when(k == 0)
        def _():
            accg[...] = jnp.zeros_like(accg)
            accu[...] = jnp.zeros_like(accu)
            accd[...] = jnp.zeros_like(accd)

        hv = h_ref[...]
        accg[...] += lax.dot_general(dg_ref[0], hv, TN, preferred_element_type=F32)
        accu[...] += lax.dot_general(du_ref[0], hv, TN, preferred_element_type=F32)
        accd[...] += lax.dot_general(a_ref[0], do_ref[...], TN, preferred_element_type=F32)

        @pl.when(k == nk - 1)
        def _():
            dwg_ref[0] = accg[...].astype(BF16)
            dwu_ref[0] = accu[...].astype(BF16)
            dwd_ref[0] = (FFN_RES_SCALE * accd[...]).astype(BF16)

    aspec = pl.BlockSpec((1, tk, fb), lambda j, k: (j, k, 0))
    row = pl.BlockSpec((tk, d), lambda j, k: (k, 0))
    down = pl.BlockSpec((1, fb, d), lambda j, k: (j, 0, 0))
    return _call(
        body, comm=comm, name=name, grid=(nb, nk),
        in_specs=[row, aspec, aspec, aspec, row], out_specs=[down] * 3,
        out_shape=[_sds((nb, fb, d), BF16)] * 3, scratch_shapes=[pltpu.VMEM((fb, d), F32)] * 3,
        compiler_params=_params(),
    )(h, dg, du, a, do)


def _win_fwd(h, w, b, dc, tm, name, comm=()):
    rows, d = h.shape
    ng = w.shape[1] // dc

    def body(h_ref, w_ref, b_ref, u_ref):
        u_ref[...] = (jnp.dot(h_ref[...], w_ref[...], preferred_element_type=F32) + b_ref[...]).astype(BF16)

    return _call(
        body, comm=comm, name=name, grid=(ng, rows // tm),
        in_specs=[pl.BlockSpec((tm, d), lambda m, i: (i, 0)), pl.BlockSpec((d, dc), lambda m, i: (0, m)),
                  pl.BlockSpec((1, dc), lambda m, i: (0, m))],
        out_specs=pl.BlockSpec((tm, dc), lambda m, i: (i, m)), out_shape=_sds((rows, ng * dc), BF16),
        compiler_params=_params(),
    )(h, w, b)


def _win_bwd_dh(du, w, hs, r, gn, dres, tm, name, comm=()):
    rows, d = hs.shape
    ng, _, dc = du.shape
    nt = rows // tm

    def body(du_ref, w_ref, hs_ref, r_ref, gn_ref, dres_ref, dhs_ref, dhsb_ref, dgn_ref, acc_ref):
        m = pl.program_id(1)

        @pl.when(m == 0)
        def _():
            acc_ref[...] = jnp.zeros_like(acc_ref)

        acc_ref[...] += lax.dot_general(du_ref[0], w_ref[...], NT, preferred_element_type=F32)

        @pl.when(m == ng - 1)
        def _():
            _rms_bwd(tm, d, acc_ref, hs_ref, r_ref, gn_ref, dres_ref, dhs_ref, dhsb_ref, dgn_ref)

    row = pl.BlockSpec((tm, d), lambda i, m: (i, 0))
    once = _once((tm, d), lambda i, m: (i, 0))
    return _call(
        body, comm=comm, name=name, grid=(nt, ng),
        in_specs=[pl.BlockSpec((1, tm, dc), lambda i, m: (m, i, 0)), pl.BlockSpec((d, dc), lambda i, m: (0, m)), once,
                  pl.BlockSpec((tm, 1), lambda i, m: (i, 0)), pl.BlockSpec((1, d), lambda i, m: (0, 0)), once],
        out_specs=[row, row, pl.BlockSpec((1, 1, d), lambda i, m: (i, 0, 0))],
        out_shape=[_sds((rows, d), F32), _sds((rows, d), BF16), _sds((nt, 1, d), F32)],
        scratch_shapes=[pltpu.VMEM((tm, d), F32)], compiler_params=_params(),
    )(du, w, hs, r, gn, dres)


def _win_bwd_dw(h, du, tk, name, comm=()):
    rows, d = h.shape
    ng, _, dc = du.shape
    nk = rows // tk

    def body(h_ref, du_ref, dw_ref, db_ref, acc, accb):
        k = pl.program_id(1)

        @pl.when(k == 0)
        def _():
            acc[...] = jnp.zeros_like(acc)
            accb[...] = jnp.zeros_like(accb)

        duv = du_ref[0]
        acc[...] += lax.dot_general(h_ref[...], duv, TN, preferred_element_type=F32)
        accb[...] += jnp.sum(duv.astype(F32), axis=0, keepdims=True)

        @pl.when(k == nk - 1)
        def _():
            dw_ref[...] = acc[...].astype(BF16)
            db_ref[...] = accb[...]

    return _call(
        body, comm=comm, name=name, grid=(ng, nk),
        in_specs=[pl.BlockSpec((tk, d), lambda m, k: (k, 0)), pl.BlockSpec((1, tk, dc), lambda m, k: (m, k, 0))],
        out_specs=[pl.BlockSpec((d, dc), lambda m, k: (0, m)), pl.BlockSpec((1, dc), lambda m, k: (0, m))],
        out_shape=[_sds((d, ng * dc), BF16), _sds((1, ng * dc), F32)],
        scratch_shapes=[pltpu.VMEM((d, dc), F32), pltpu.VMEM((1, dc), F32)], compiler_params=_params(),
    )(h, du)


def _layernorm_silu(zc, lg, lb):
    mu = jnp.mean(zc, axis=-1, keepdims=True)
    xc = zc - mu
    rstd = lax.rsqrt(jnp.mean(xc * xc, axis=-1, keepdims=True) + EPS)
    nrm = xc * rstd
    lin = nrm * lg + lb
    s = _sigmoid(lin)
    return nrm, rstd, lin, s


def _wout_fwd(zc, ysc, wout, hs, lg, lb, gn, rows, tm, name, comm=()):
    dc = zc.shape[1]
    d = hs.shape[1]

    def body(zc_ref, ysc_ref, w_ref, hs_ref, lg_ref, lb_ref, gn_ref, y_ref, hsn_ref, hn_ref, rn_ref):
        def mix(rs, _):
            _, _, lin, s = _layernorm_silu(zc_ref[rs, :], lg_ref[...], lb_ref[...])
            y_ref[rs, :dc] = ysc_ref[rs, :]
            y_ref[rs, dc:] = (lin * s).astype(BF16)

        _for_row_groups(tm, mix)
        hsn_ref[...] = jnp.dot(y_ref[...], w_ref[...], preferred_element_type=F32)

        def norm(rs, _):
            hsn = hs_ref[rs, :] + hsn_ref[rs, :]
            r = lax.rsqrt(jnp.mean(hsn * hsn, axis=-1, keepdims=True) + EPS)
            hsn_ref[rs, :] = hsn
            hn_ref[rs, :] = (hsn * r * gn_ref[...]).astype(BF16)
            rn_ref[rs, :] = r

        _for_row_groups(tm, norm)

    half = pl.BlockSpec((tm, dc), lambda i: (i, 0))
    row = pl.BlockSpec((tm, d), lambda i: (i, 0))
    vec_c = pl.BlockSpec((1, dc), lambda i: (0, 0))
    return _call(
        body, comm=comm, name=name, grid=(rows // tm,),
        in_specs=[half, half, _once((2 * dc, d), lambda i: (0, 0)), row, vec_c, vec_c,
                  pl.BlockSpec((1, d), lambda i: (0, 0))],
        out_specs=[pl.BlockSpec((tm, 2 * dc), lambda i: (i, 0)), row, row, pl.BlockSpec((tm, 1), lambda i: (i, 0))],
        out_shape=[_sds((rows, 2 * dc), BF16), _sds((rows, d), F32), _sds((rows, d), BF16), _sds((rows, 1), F32)],
        compiler_params=_params(),
    )(zc, ysc, wout, hs, lg, lb, gn)


def _wout_bwd_dy(do, wout, zc, lg, lb, tm, name, comm=()):
    rows, d = do.shape
    dc = zc.shape[1]
    nt = rows // tm

    def body(do_ref, w_ref, zc_ref, lg_ref, lb_ref, dysc_ref, dzc_ref, dlg_ref, dlb_ref, dy_ref):
        dy_ref[...] = lax.dot_general(do_ref[...], w_ref[...], NT, preferred_element_type=F32)
        dlb_ref[0] = jnp.zeros((1, dc), F32)
        dlg_ref[0] = jnp.zeros((1, dc), F32)

        def group(rs, _):
            dysc_ref[rs, :] = dy_ref[rs, :dc].astype(BF16)
            nrm, rstd, lin, s = _layernorm_silu(zc_ref[rs, :], lg_ref[...], lb_ref[...])
            dl = dy_ref[rs, dc:] * (s * (1.0 + lin * (1.0 - s)))
            dlb_ref[0] += jnp.sum(dl, axis=0, keepdims=True)
            dlg_ref[0] += jnp.sum(dl * nrm, axis=0, keepdims=True)
            dn = dl * lg_ref[...]
            dzc_ref[rs, :] = rstd * (dn - jnp.mean(dn, axis=-1, keepdims=True)
                                     - nrm * jnp.mean(dn * nrm, axis=-1, keepdims=True))

        _for_row_groups(tm, group)

    half = pl.BlockSpec((tm, dc), lambda i: (i, 0))
    vec_c = pl.BlockSpec((1, dc), lambda i: (0, 0))
    part = pl.BlockSpec((1, 1, dc), lambda i: (i, 0, 0))
    return _call(
        body, comm=comm, name=name, grid=(nt,),
        in_specs=[pl.BlockSpec((tm, d), lambda i: (i, 0)), _once((2 * dc, d), lambda i: (0, 0)), half, vec_c, vec_c],
        out_specs=[half, half, part, part],
        out_shape=[_sds((rows, dc), BF16), _sds((rows, dc), F32), _sds((nt, 1, dc), F32), _sds((nt, 1, dc), F32)],
        scratch_shapes=[pltpu.VMEM((tm, 2 * dc), F32)], compiler_params=_params(),
    )(do, wout, zc, lg, lb)


def _wout_bwd_dw(y, do, nb, tk, name, comm=()):
    rows, k2 = y.shape
    d = do.shape[1]
    nk = rows // tk

    def body(y_ref, do_ref, dw_ref, acc):
        k = pl.program_id(0)

        @pl.when(k == 0)
        def _():
            acc[...] = jnp.zeros_like(acc)

        acc[...] += lax.dot_general(y_ref[...], do_ref[...], TN, preferred_element_type=F32)

        @pl.when(k == nk - 1)
        def _():
            dw_ref[...] = acc[...].astype(BF16)

    return _call(
        body, comm=comm, name=name, grid=(nk,),
        in_specs=[pl.BlockSpec((tk, k2), lambda k: (k, 0)), pl.BlockSpec((tk, d), lambda k: (k, 0))],
        out_specs=_once((k2, d), lambda k: (0, 0)), out_shape=_sds((k2, d), BF16),
        scratch_shapes=[pltpu.VMEM((k2, d), F32)], compiler_params=_params(),
    )(y, do).reshape(nb, k2 // nb, d)


def _windows(win, n_res):
    length = win.shape[0]
    return [win if r == 0 else pltpu.roll(win, length - r, 0) for r in range(n_res)]


def _taps(src_ref, start, offsets, ch):
    span = -(-(max(offsets) + ch) // SUBLANES) * SUBLANES
    win = src_ref[pl.ds(start, span), :]
    shifted = _windows(win, min(SUBLANES, max(offsets) + 1))
    return [shifted[o % SUBLANES][(o // SUBLANES) * SUBLANES:(o // SUBLANES) * SUBLANES + ch] for o in offsets]


def _rows8(v):
    return jnp.sum(v.reshape(v.shape[0] // SUBLANES, SUBLANES, v.shape[1]), axis=0)


def _conv_geometry(n_seq, n_meta):
    base = CONV_PAD + n_meta
    off_cf = base - (CF_WIDTH - 1)
    off_sc = base - (SC_WIDTH - 1)
    logical = -(-(n_meta + n_seq) // CONV_CH) * CONV_CH
    return base, off_cf, off_sc, logical


def _fill_conv_inputs(c_ref, v_ref, a_ref, g_ref, scv, sz, n_seq, n_meta):
    base = CONV_PAD + n_meta
    cb = scv.shape[1]
    scv[0:CONV_PAD, :] = jnp.zeros((CONV_PAD, cb), F32)
    sz[0:CONV_PAD, :] = jnp.zeros((CONV_PAD, cb), F32)

    def put(src, dst, n):
        cv = c_ref[src, :].astype(F32) * v_ref[src, :].astype(F32)
        scv[dst, :] = cv
        sz[dst, :] = a_ref[src, :].astype(F32) * _sigmoid(g_ref[src, :].astype(F32))

    put(pl.ds(n_seq, n_meta), pl.ds(CONV_PAD, n_meta), n_meta)

    def chunk(i, carry):
        t0 = pl.multiple_of(i * CONV_CH, CONV_CH)
        put(pl.ds(t0, CONV_CH), pl.ds(base + t0, CONV_CH), CONV_CH)
        return carry

    lax.fori_loop(0, n_seq // CONV_CH, chunk, 0)


def _conv_fwd(u, wsc, wcf, cbias, n_seq, n_meta, name, comm=()):
    rows = u.shape[0]
    nb, _, cb = wsc.shape
    dc = nb * cb
    base, off_cf, off_sc, _ = _conv_geometry(n_seq, n_meta)
    a_cf, a_sc = off_cf // SUBLANES * SUBLANES, off_sc // SUBLANES * SUBLANES
    ch = CONV_CH

    def body(b_ref, c_ref, v_ref, a_ref, g_ref, wsc_ref, wcf_ref, cb_ref, ysc_ref, zc_ref, scv, sz):
        _fill_conv_inputs(c_ref, v_ref, a_ref, g_ref, scv, sz, n_seq, n_meta)
        w3, w31, bias = wsc_ref[0], wcf_ref[0], cb_ref[...]

        def chunk(i, carry):
            t0 = pl.multiple_of(i * ch, ch)
            acc = jnp.zeros((ch, cb), F32)
            for k, win in enumerate(_taps(sz, t0 + a_cf, [off_cf - a_cf + k for k in range(CF_WIDTH)], ch)):
                acc = acc + win * w31[k:k + 1, :]
            zc_ref[pl.ds(t0, ch), :] = acc + bias
            s = jnp.zeros((ch, cb), F32)
            for k, win in enumerate(_taps(scv, t0 + a_sc, [off_sc - a_sc + k for k in range(SC_WIDTH)], ch)):
                s = s + win * w3[k:k + 1, :]
            ysc_ref[pl.ds(t0, ch), :] = (b_ref[pl.ds(t0, ch), :].astype(F32) * s).astype(BF16)
            return carry

        lax.fori_loop(0, n_seq // ch, chunk, 0)
        ysc_ref[n_seq:rows, :] = jnp.zeros((rows - n_seq, cb), BF16)
        zc_ref[n_seq:rows, :] = jnp.zeros((rows - n_seq, cb), F32)

    ucol = [pl.BlockSpec((rows, cb), functools.partial(lambda m, j: (0, m * nb + j), m)) for m in range(5)]
    blk = pl.BlockSpec((rows, cb), lambda j: (0, j))
    return _call(
        body, comm=comm, name=name, grid=(nb,),
        in_specs=ucol + [pl.BlockSpec((1, SC_WIDTH, cb), lambda j: (j, 0, 0)),
                         pl.BlockSpec((1, CF_WIDTH, cb), lambda j: (j, 0, 0)), pl.BlockSpec((1, cb), lambda j: (0, j))],
        out_specs=[blk, blk], out_shape=[_sds((rows, dc), BF16), _sds((rows, dc), F32)],
        scratch_shapes=[pltpu.VMEM((base + n_seq, cb), F32)] * 2, compiler_params=_params(),
    )(u, u, u, u, u, wsc, wcf, cbias)


def _conv_bwd(u, dysc, dzc, wsc, wcf, n_seq, n_meta, name, comm=()):
    rows = u.shape[0]
    nb, _, cb = wsc.shape
    dc = nb * cb
    base, off_cf, off_sc, logical = _conv_geometry(n_seq, n_meta)
    a_cf, a_sc = off_cf // SUBLANES * SUBLANES, off_sc // SUBLANES * SUBLANES
    ch = CONV_CH
    tail = CONV_PAD

    def body(b_ref, c_ref, v_ref, a_ref, g_ref, dysc_ref, dzc_ref, wsc_ref, wcf_ref,
             du_ref, dwsc_ref, dwcf_ref, dcb_ref,
             scv, sz, sds_, sdz, dlcv, dlz, accsc, acccf, accb):
        _fill_conv_inputs(c_ref, v_ref, a_ref, g_ref, scv, sz, n_seq, n_meta)
        w3, w31 = wsc_ref[0], wcf_ref[0]
        dub_ref, duc_ref, duv_ref, dua_ref, dug_ref = (du_ref.at[m] for m in range(5))
        sds_[0:n_meta, :] = jnp.zeros((n_meta, cb), F32)
        sdz[0:n_meta, :] = jnp.zeros((n_meta, cb), F32)
        behind = logical + tail - (n_meta + n_seq)
        sds_[n_meta + n_seq:logical + tail, :] = jnp.zeros((behind, cb), F32)
        sdz[n_meta + n_seq:logical + tail, :] = jnp.zeros((behind, cb), F32)
        accsc[...] = jnp.zeros_like(accsc)
        acccf[...] = jnp.zeros_like(acccf)
        accb[...] = jnp.zeros_like(accb)

        def forward_chunk(i, carry):
            t0 = pl.multiple_of(i * ch, ch)
            rws = pl.ds(t0, ch)
            dy = dysc_ref[rws, :].astype(F32)
            ds = dy * b_ref[rws, :].astype(F32)
            dz = dzc_ref[rws, :]
            sds_[pl.ds(n_meta + t0, ch), :] = ds
            sdz[pl.ds(n_meta + t0, ch), :] = dz
            s = jnp.zeros((ch, cb), F32)
            for k, win in enumerate(_taps(scv, t0 + a_sc, [off_sc - a_sc + k for k in range(SC_WIDTH)], ch)):
                s = s + win * w3[k:k + 1, :]
                accsc[k * SUBLANES:(k + 1) * SUBLANES, :] += _rows8(ds * win)
            dub_ref[rws, :] = (dy * s).astype(BF16)
            for k, win in enumerate(_taps(sz, t0 + a_cf, [off_cf - a_cf + k for k in range(CF_WIDTH)], ch)):
                acccf[k * SUBLANES:(k + 1) * SUBLANES, :] += _rows8(dz * win)
            accb[...] += _rows8(dz)
            return carry

        lax.fori_loop(0, n_seq // ch, forward_chunk, 0)

        def backward_chunk(i, carry):
            p0 = pl.multiple_of(i * ch, ch)
            dcv = jnp.zeros((ch, cb), F32)
            for k, win in enumerate(_taps(sds_, p0, [SC_WIDTH - 1 - k for k in range(SC_WIDTH)], ch)):
                dcv = dcv + win * w3[k:k + 1, :]
            dlcv[pl.ds(p0, ch), :] = dcv
            dzi = jnp.zeros((ch, cb), F32)
            for k, win in enumerate(_taps(sdz, p0, [CF_WIDTH - 1 - k for k in range(CF_WIDTH)], ch)):
                dzi = dzi + win * w31[k:k + 1, :]
            dlz[pl.ds(p0, ch), :] = dzi
            return carry

        lax.fori_loop(0, logical // ch, backward_chunk, 0)

        def gates(phys, logi):
            dcv, dzi = dlcv[logi, :], dlz[logi, :]
            duc_ref[phys, :] = (dcv * v_ref[phys, :].astype(F32)).astype(BF16)
            duv_ref[phys, :] = (dcv * c_ref[phys, :].astype(F32)).astype(BF16)
            s = _sigmoid(g_ref[phys, :].astype(F32))
            dua_ref[phys, :] = (dzi * s).astype(BF16)
            dug_ref[phys, :] = (dzi * a_ref[phys, :].astype(F32) * s * (1.0 - s)).astype(BF16)

        def gate_chunk(i, carry):
            t0 = pl.multiple_of(i * ch, ch)
            gates(pl.ds(t0, ch), pl.ds(n_meta + t0, ch))
            return carry

        lax.fori_loop(0, n_seq // ch, gate_chunk, 0)
        gates(pl.ds(n_seq, n_meta), pl.ds(0, n_meta))
        dub_ref[n_seq:rows, :] = jnp.zeros((rows - n_seq, cb), BF16)
        pad0 = n_seq + n_meta
        for ref in (duc_ref, duv_ref, dua_ref, dug_ref):
            ref[pad0:rows, :] = jnp.zeros((rows - pad0, cb), BF16)
        dwsc_ref[0] = jnp.sum(accsc[...].reshape(SC_WIDTH, SUBLANES, cb), axis=1)
        dwcf_ref[0] = jnp.sum(acccf[...].reshape(CF_WIDTH, SUBLANES, cb), axis=1)
        dcb_ref[...] = jnp.sum(accb[...], axis=0, keepdims=True)

    ucol = [pl.BlockSpec((rows, cb), functools.partial(lambda m, j: (0, m * nb + j), m)) for m in range(5)]
    blk = pl.BlockSpec((dysc.shape[0], cb), lambda j: (0, j))
    wsc_spec = pl.BlockSpec((1, SC_WIDTH, cb), lambda j: (j, 0, 0))
    wcf_spec = pl.BlockSpec((1, CF_WIDTH, cb), lambda j: (j, 0, 0))
    outs = _call(
        body, comm=comm, name=name, grid=(nb,),
        in_specs=ucol + [blk, blk, wsc_spec, wcf_spec],
        out_specs=[pl.BlockSpec((5, rows, cb), lambda j: (0, 0, j)), wsc_spec, wcf_spec,
                   pl.BlockSpec((1, cb), lambda j: (0, j))],
        out_shape=[_sds((5, rows, dc), BF16), _sds((nb, SC_WIDTH, cb), F32), _sds((nb, CF_WIDTH, cb), F32),
                   _sds((1, dc), F32)],
        scratch_shapes=[pltpu.VMEM((base + n_seq, cb), F32)] * 2 + [pltpu.VMEM((logical + tail, cb), F32)] * 2
        + [pltpu.VMEM((logical, cb), F32)] * 2
        + [pltpu.VMEM((SC_WIDTH * SUBLANES, cb), F32), pltpu.VMEM((CF_WIDTH * SUBLANES, cb), F32),
           pltpu.VMEM((SUBLANES, cb), F32)],
        compiler_params=_params(),
    )(u, u, u, u, u, dysc, dzc, wsc, wcf)
    return outs


def _row_tile(rows, cols):
    return rows // 4 if rows % 64 == 0 and rows * cols >= (1 << 18) else rows


def _pair_sum(grad, sib, idx, name, comm=()):
    _, rows, cols = grad.shape
    tr = rows

    def body(idx_ref, g_ref, s_ref, o_ref):
        o_ref[0] = (g_ref[0].astype(F32) + s_ref[0].astype(F32)).astype(o_ref.dtype)

    return _call(
        body, name=name,
        grid_spec=pltpu.PrefetchScalarGridSpec(
            num_scalar_prefetch=1, grid=(4, rows // tr),
            in_specs=[pl.BlockSpec((1, tr, cols), lambda k, i, idx_ref: (idx_ref[k], i, 0)),
                      pl.BlockSpec((1, tr, cols), lambda k, i, idx_ref: (idx_ref[4 + k], i, 0))],
            out_specs=pl.BlockSpec((1, tr, cols), lambda k, i, idx_ref: (k, i, 0))),
        out_shape=_sds((4, rows, cols), grad.dtype), compiler_params=_params(),
    )(idx, grad, sib)


def _adamw_math(w, g, m, v):
    m = ADAM_B1 * m + (1.0 - ADAM_B1) * g
    v = ADAM_B2 * v + (1.0 - ADAM_B2) * (g * g)
    m_hat = m / (1.0 - ADAM_B1 ** ADAM_STEP)
    v_hat = v / (1.0 - ADAM_B2 ** ADAM_STEP)
    delta = -ADAM_LR * (m_hat / (jnp.sqrt(v_hat) + ADAM_EPS) + ADAM_WD * w)
    return delta, m, v


def _adamw_sharded(own, got, w, m, v, name, comm=()):
    rows, cols = w.shape
    tr = _row_tile(rows, cols)

    def body(own_ref, g0_ref, g1_ref, g2_ref, w_ref, m_ref, v_ref, g_ref, d_ref, nm_ref, nv_ref):
        g = own_ref[0].astype(F32) + g0_ref[0].astype(F32) + g1_ref[0].astype(F32) + g2_ref[0].astype(F32)
        delta, nm, nv = _adamw_math(w_ref[...], g, m_ref[...], v_ref[...])
        g_ref[...] = g
        d_ref[...] = delta
        nm_ref[...] = nm
        nv_ref[...] = nv

    flat = pl.BlockSpec((tr, cols), lambda i: (i, 0))
    slot = [pl.BlockSpec((1, tr, cols), functools.partial(lambda k, i: (k, i, 0), k)) for k in range(3)]
    return _call(
        body, comm=comm, name=name, grid=(rows // tr,),
        in_specs=[slot[0]] + slot + [flat] * 3, out_specs=[flat] * 4, out_shape=[_sds((rows, cols), F32)] * 4,
        compiler_params=_params(),
    )(own, got, got, got, w, m, v)


def _adamw_replicated(gathered, segs, ws, ms, vs, loss_scale, name, comm=()):
    n = len(ws)

    def body(*refs):
        gat = refs[0]
        w_refs, m_refs, v_refs = refs[1:1 + n], refs[1 + n:1 + 2 * n], refs[1 + 2 * n:1 + 3 * n]
        outs = refs[1 + 3 * n:]

        def total(off, width):
            s = gat[0, :, off:off + width]
            for k in range(1, N_DEV):
                s = s + gat[k, :, off:off + width]
            return s

        outs[0][...] = loss_scale * total(segs[n][0], segs[n][1])
        for p in range(n):
            g = total(*segs[p])
            delta, nm, nv = _adamw_math(w_refs[p][...], g, m_refs[p][...], v_refs[p][...])
            for q, val in enumerate((g, delta, nm, nv)):
                outs[1 + 4 * p + q][...] = val
        for e, seg in enumerate(segs[n + 1:]):
            outs[1 + 4 * n + e][...] = total(*seg)

    return _call(
        body, name=name,
        out_shape=[_sds((1, segs[n][1]), F32)] + [_sds(w.shape, F32) for w in ws for _ in range(4)]
        + [_sds((1, seg[1]), F32) for seg in segs[n + 1:]],
        compiler_params=_params(),
    )(gathered, *ws, *ms, *vs)


def _adamw_plain(g, w, m, v, name):
    def body(g_ref, w_ref, m_ref, v_ref, d_ref, nm_ref, nv_ref):
        d_ref[...], nm_ref[...], nv_ref[...] = _adamw_math(w_ref[...], g_ref[...], m_ref[...], v_ref[...])

    return list(_call(body, name=name, out_shape=[_sds(w.shape, F32)] * 3)(g, w, m, v))


REPLICATED = ("ffn1_norm", "mix_norm", "b_in", "conv_cf_b", "ln_cf_g", "ln_cf_b", "ffn2_norm", "final_norm")
SHARDED = ("meta_tokens", "ffn1_w_gate", "ffn1_w_up", "ffn1_w_down", "w_in", "conv_sc_w", "conv_cf_w", "w_out",
           "ffn2_w_gate", "ffn2_w_up", "ffn2_w_down")
WEIGHTS = ("meta_tokens", "ffn1_norm", "ffn1_w_gate", "ffn1_w_up", "ffn1_w_down", "mix_norm", "w_in", "b_in",
           "conv_sc_w", "conv_cf_w", "conv_cf_b", "ln_cf_g", "ln_cf_b", "w_out", "ffn2_norm", "ffn2_w_gate",
           "ffn2_w_up", "ffn2_w_down", "final_norm")


TRANSPOSED = ("ffn1_w_gate", "ffn1_w_up", "ffn2_w_gate", "ffn2_w_up")


def _blocks2d(k, a):
    a = a.reshape(a.shape[-2:]) if a.ndim >= 2 else a.reshape(1, -1)
    return a.T if k in TRANSPOSED else a


def _step(x, tgt, w, m, v):
    n_seq, d = x.shape[1], x.shape[2]
    n_meta = w["meta_tokens"].shape[0]
    rows = -(-(n_seq + n_meta) // ROW_ALIGN) * ROW_ALIGN
    tm = rows // N_ROW_TILES
    cb = w["conv_sc_w"].shape[-1]
    dc = cb * N_DEV
    w2 = {k: _blocks2d(k, a) for k, a in w.items()}
    m2 = {k: _blocks2d(k, a) for k, a in m.items()}
    v2 = {k: _blocks2d(k, a) for k, a in v.items()}

    def as_given(k, r):
        return (r.T if k in TRANSPOSED else r).reshape(w[k].shape)

    def cast(k):
        return w2[k].astype(BF16)

    full = dict(zip(("ffn1_w_gate", "meta_tokens", "conv_sc_w", "conv_cf_w"), _exchange_alone(
        _gather_all([cast("ffn1_w_gate"), w2["meta_tokens"], w2["conv_sc_w"], w2["conv_cf_w"]]), "gather_gate1")))
    meta = jnp.transpose(full["meta_tokens"], (1, 0, 2)).reshape(n_meta, d)
    hs0 = jnp.concatenate([x[0], meta, jnp.zeros((rows - n_seq - n_meta, d), F32)], axis=0)

    tx = n_seq // N_ROW_TILES
    h1, r1 = _rms_fwd(hs0, w2["ffn1_norm"], tm, "rms_in")
    up1 = _gather_all([cast("ffn1_w_up")])
    g1 = _ffn_proj(h1, full["ffn1_w_gate"], tm, "ffn1_gate", comm=[up1])
    down1 = _gather_all([cast("ffn1_w_down")])
    u1 = _ffn_proj(h1, up1.results[0], tm, "ffn1_up", comm=[down1])
    w_in_all, w_out1 = _gather_all([cast("w_in")]), _gather_direct([cast("w_out")])
    hs1, h2, r2 = _ffn_down_norm(g1, u1, down1.results[0], hs0, w2["mix_norm"], tm, "ffn1_down", comm=[w_in_all, w_out1])
    win = jnp.transpose(w_in_all.results[0], (1, 0, 2)).reshape(d, -1)
    w_out2, gate2 = _gather_forward(w_out1.results), _gather_direct([cast("ffn2_w_gate")])
    u = _win_fwd(h2, win, w2["b_in"], dc, tm, "mix_in", comm=[w_out2, gate2])
    wout = w_out2.results[0].reshape(-1, d)
    gate2p = _gather_forward(gate2.results)
    ysc, zc = _conv_fwd(u, full["conv_sc_w"], full["conv_cf_w"], w2["conv_cf_b"], n_seq, n_meta, "conv_fwd", comm=[gate2p])
    up2 = _gather_all([cast("ffn2_w_up")])
    y, hs2, h3, r3 = _wout_fwd(zc, ysc, wout, hs1, w2["ln_cf_g"], w2["ln_cf_b"], w2["ffn2_norm"], n_seq, tx, "mix_out",
                               comm=[up2])
    down2 = _gather_all([cast("ffn2_w_down")])
    full.update(ffn1_w_up=up1.results[0], ffn1_w_down=down1.results[0], ffn2_w_gate=gate2p.results[0],
                ffn2_w_up=up2.results[0])
    g2, u2 = _ffn_gu(h3, full["ffn2_w_gate"], full["ffn2_w_up"], tx, "ffn2_gu", comm=[down2])
    full["ffn2_w_down"] = down2.results[0]
    dhs3, dhs3b, loss_p, dgf_p = _ffn_down_loss(
        g2, u2, full["ffn2_w_down"], hs2, w2["final_norm"], tgt[0], n_seq, tx, "ffn2_down_loss")

    dg2, du2, a2 = _ffn_bwd_da(dhs3b, full["ffn2_w_down"], g2, u2, tx, "ffn2_bwd_da")
    dhs2, dhs2b, dn3_p = _ffn_bwd_dh(
        dg2, du2, full["ffn2_w_gate"], full["ffn2_w_up"], hs2, r3, w2["ffn2_norm"], dhs3, tx, "ffn2_bwd_dh",
        zeros=jnp.zeros((rows, d), F32))
    xi, yi, ci = lax.axis_index("x"), lax.axis_index("y"), lax.axis_index("c")
    chip_of = [2 * xi + yi, 2 * (1 - xi) + yi, 2 * xi + (1 - yi), 2 * (1 - xi) + (1 - yi)]
    idx = jnp.stack([2 * ch + ci for ch in chip_of] + chip_of).astype(jnp.int32)
    out = {}

    def to_pairs(parts):
        return _pair_exchange([p.reshape((4, 2) + p.shape[1:]) for p in parts])

    def pair_sums(names, parts, pairs):
        return [_pair_sum(p, s, idx, "pair_sum_" + k) for k, p, s in zip(names, parts, pairs.results)]

    def update(names, sums, chips):
        for k, own, got in zip(names, sums, chips.results):
            res = _adamw_sharded(own, got, w2[k], m2[k], v2[k], "adamw_" + k)
            out[k] = [as_given(k, r) for r in res]

    ffn2 = ("ffn2_w_gate", "ffn2_w_up", "ffn2_w_down")
    tk = ROW_ALIGN
    grads2 = _ffn_bwd_dw(h3, dg2, du2, a2, dhs3b, tk, "ffn2_bwd_dw")
    pairs2 = to_pairs(grads2)
    dysc, dzc, dlg_p, dlb_p = _wout_bwd_dy(dhs2b, wout, zc, w2["ln_cf_g"], w2["ln_cf_b"], tx, "mix_out_bwd_dy",
                                           comm=[pairs2])
    sums2 = pair_sums(ffn2, grads2, pairs2)
    dwout = _wout_bwd_dw(y, dhs2b, N_DEV, tk, "mix_out_bwd_dw")
    chips2 = [_chip_exchange([s]) for s in sums2]
    du, dcsw, dccw, dcb = _conv_bwd(u, dysc, dzc, full["conv_sc_w"], full["conv_cf_w"], n_seq, n_meta, "conv_bwd",
                                    comm=[chips2[0]])
    dhs1, dhs1b, dn2_p = _win_bwd_dh(du, win, hs1, r2, w2["mix_norm"], dhs2, tm, "mix_in_bwd_dh", comm=[chips2[1]])
    dwin, dbin = _win_bwd_dw(h2, du, tk, "mix_in_bwd_dw", comm=[chips2[2]])
    for k, s, ch in zip(ffn2, sums2, chips2):
        update([k], [s], ch)

    mixer = ("w_out", "w_in", "conv_sc_w", "conv_cf_w")
    gradsm = [dwout, jnp.transpose(dwin.reshape(d, N_DEV, -1), (1, 0, 2)), dcsw, dccw]
    pairsm = to_pairs(gradsm)
    dg1, du1, a1 = _ffn_bwd_da(dhs1b, full["ffn1_w_down"], g1, u1, tm, "ffn1_bwd_da", comm=[pairsm])
    sumsm = pair_sums(mixer, gradsm, pairsm)
    chipsm = _chip_exchange(sumsm)
    ffn1 = ("ffn1_w_gate", "ffn1_w_up", "ffn1_w_down")
    grads1 = _ffn_bwd_dw(h1, dg1, du1, a1, dhs1b, tk, "ffn1_bwd_dw", comm=[chipsm])
    update(mixer, sumsm, chipsm)
    pairs1 = to_pairs(grads1)
    _exchange_alone(pairs1, "reduce_pair_ffn1")
    sums1 = pair_sums(ffn1, grads1, pairs1)
    chips1 = _chip_exchange(sums1)
    dhs0, _, dn1_p = _ffn_bwd_dh(
        dg1, du1, full["ffn1_w_gate"], full["ffn1_w_up"], hs0, r1, w2["ffn1_norm"], dhs1, tm, "ffn1_bwd_dh", comm=[chips1])
    update(ffn1, sums1, chips1)
    grad_x = dhs0[:n_seq][None]

    partial = {
        "ffn1_norm": dn1_p.sum(0), "mix_norm": dn2_p.sum(0), "b_in": dbin, "conv_cf_b": dcb,
        "ln_cf_g": dlg_p.sum(0), "ln_cf_b": dlb_p.sum(0), "ffn2_norm": dn3_p.sum(0), "final_norm": dgf_p.sum(0),
    }
    loss_seg = jnp.pad(loss_p.sum((0, 2)).reshape(1, 1), ((0, 0), (0, 127)))
    pieces = [partial[k] for k in REPLICATED] + [loss_seg, dhs0[n_seq:n_seq + n_meta].reshape(1, n_meta * d)]
    segs, off = [], 0
    for p in pieces:
        segs.append((off, p.shape[1]))
        off += p.shape[1]
    rows8 = _gather_rows(jnp.concatenate(pieces, axis=1), "gather_small")
    res = _adamw_replicated(rows8, segs, [w2[k] for k in REPLICATED], [m2[k] for k in REPLICATED],
                            [v2[k] for k in REPLICATED], 0.5 / d, "adamw_replicated")
    loss = res[0][0, 0]
    for p, k in enumerate(REPLICATED):
        out[k] = [r.reshape(w[k].shape) for r in res[1 + 4 * p:5 + 4 * p]]
    ob = w2["meta_tokens"].shape[1]
    gmeta = lax.dynamic_slice_in_dim(res[-1].reshape(n_meta, d), (4 * xi + 2 * yi + ci) * ob, ob, axis=1)
    out["meta_tokens"] = [gmeta] + _adamw_plain(gmeta, w2["meta_tokens"], m2["meta_tokens"], v2["meta_tokens"], "adamw_meta_tokens")

    return (loss, grad_x, *[out[k][0] for k in WEIGHTS], *[out[k][1] for k in WEIGHTS],
            *[out[k][2] for k in WEIGHTS], *[out[k][3] for k in WEIGHTS])


def kernel(x, meta_tokens, ffn1_norm, ffn1_w_gate, ffn1_w_up, ffn1_w_down, mix_norm, w_in, b_in, conv_sc_w, conv_cf_w, conv_cf_b, ln_cf_g, ln_cf_b, w_out, ffn2_norm, ffn2_w_gate, ffn2_w_up, ffn2_w_down, final_norm, loss_target, m_meta_tokens, m_ffn1_norm, m_ffn1_w_gate, m_ffn1_w_up, m_ffn1_w_down, m_mix_norm, m_w_in, m_b_in, m_conv_sc_w, m_conv_cf_w, m_conv_cf_b, m_ln_cf_g, m_ln_cf_b, m_w_out, m_ffn2_norm, m_ffn2_w_gate, m_ffn2_w_up, m_ffn2_w_down, m_final_norm, v_meta_tokens, v_ffn1_norm, v_ffn1_w_gate, v_ffn1_w_up, v_ffn1_w_down, v_mix_norm, v_w_in, v_b_in, v_conv_sc_w, v_conv_cf_w, v_conv_cf_b, v_ln_cf_g, v_ln_cf_b, v_w_out, v_ffn2_norm, v_ffn2_w_gate, v_ffn2_w_up, v_ffn2_w_down, v_final_norm):
    given = dict(locals())
    w = {k: given[k] for k in WEIGHTS}
    m = {k: given["m_" + k] for k in WEIGHTS}
    v = {k: given["v_" + k] for k in WEIGHTS}
    return _step(x, loss_target, w, m, v)
```

```python
import functools

import jax
import jax.numpy as jnp
from jax import lax
from jax.experimental import pallas as pl
from jax.experimental.pallas import tpu as pltpu

F32 = jnp.float32
BF16 = jnp.bfloat16
EPS = 1e-6
FFN_RES_SCALE = 0.5
SC_WIDTH = 3
CF_WIDTH = 31
ADAM_LR = 0.001
ADAM_B1 = 0.9
ADAM_B2 = 0.999
ADAM_EPS = 1e-08
ADAM_WD = 0.01
ADAM_STEP = 10

N_DEV = 8
N_ROW_TILES = 8
ROW_ALIGN = 256
CONV_PAD = 32
CONV_CH = 128
SUB_ROWS = 16
DOWN_ROW_TILES = 16
SUBLANES = 8
BF16_ROWS = 16
VMEM_LIMIT = 56 * 1024 * 1024
MESH = pl.DeviceIdType.MESH
ANY = pl.BlockSpec(memory_space=pl.ANY)

NT = (((1,), (1,)), ((), ()))
TN = (((0,), (0,)), ((), ()))


def _pallas(body, **kw):
    return pl.pallas_call(body, **kw)


class _Exchange:
    def __init__(self, inputs, out_shapes, sem_shapes, start, finish, aliases=None):
        self.inputs, self.out_shapes, self.sem_shapes = list(inputs), list(out_shapes), list(sem_shapes)
        self.start, self.finish, self.aliases = start, finish, dict(aliases or {})
        self.results = None


def _call(body, comm=(), **kw):
    if not comm:
        return _pallas(body, **kw)
    grid = kw.pop("grid")
    in_specs = list(kw.pop("in_specs"))
    out_specs, out_shape = kw.pop("out_specs"), kw.pop("out_shape")
    scratch = list(kw.pop("scratch_shapes", []))
    single = not isinstance(out_shape, (list, tuple))
    out_specs, out_shape = ([out_specs], [out_shape]) if single else (list(out_specs), list(out_shape))
    n_in, n_out, n_scr = len(in_specs), len(out_shape), len(scratch)
    c_in = [a for job in comm for a in job.inputs]
    c_out = [s for job in comm for s in job.out_shapes]
    c_sem = [s for job in comm for s in job.sem_shapes]
    aliases, i0, o0 = dict(kw.pop("input_output_aliases", {})), n_in, n_out
    for job in comm:
        aliases.update({i0 + i: o0 + o for i, o in job.aliases.items()})
        i0, o0 = i0 + len(job.inputs), o0 + len(job.out_shapes)

    def hosted(*refs):
        pos = [0]

        def take(n):
            pos[0] += n
            return refs[pos[0] - n:pos[0]]

        ins, cins, outs, couts, scr, sems = take(n_in), take(len(c_in)), take(n_out), take(len(c_out)), take(n_scr), take(len(c_sem))
        ids = [pl.program_id(k) for k in range(len(grid))]
        first = functools.reduce(jnp.logical_and, [i == 0 for i in ids])
        last = functools.reduce(jnp.logical_and, [i == g - 1 for i, g in zip(ids, grid)])

        def each(phase):
            i, o, s = 0, 0, 0
            for job in comm:
                ni, no, ns = len(job.inputs), len(job.out_shapes), len(job.sem_shapes)
                getattr(job, phase)(cins[i:i + ni], couts[o:o + no], sems[s:s + ns])
                i, o, s = i + ni, o + no, s + ns

        @pl.when(first)
        def _():
            each("start")

        body(*ins, *outs, *scr)

        @pl.when(last)
        def _():
            each("finish")

    call = _pallas(
        hosted, grid=grid, in_specs=in_specs + [ANY] * len(c_in), out_specs=out_specs + [ANY] * len(c_out),
        out_shape=out_shape + c_out, scratch_shapes=scratch + c_sem, input_output_aliases=aliases, **kw)

    def run(*args):
        res = call(*args, *c_in)
        o = n_out
        for job in comm:
            job.results = list(res[o:o + len(job.out_shapes)])
            o += len(job.out_shapes)
        return res[0] if single else list(res[:n_out])

    return run


def _exchange_alone(job, name, comm=()):
    n_in, n_out = len(job.inputs), len(job.out_shapes)

    def body(*refs):
        ins, outs, sems = refs[:n_in], refs[n_in:n_in + n_out], refs[n_in + n_out:]
        job.start(ins, outs, sems)
        job.finish(ins, outs, sems)

    res = _pallas(body, name=name, in_specs=[ANY] * n_in, out_specs=[ANY] * n_out, out_shape=job.out_shapes,
                  scratch_shapes=job.sem_shapes, input_output_aliases=job.aliases)(*job.inputs)
    job.results = list(res)
    return job.results


def _params(**kw):
    return pltpu.CompilerParams(vmem_limit_bytes=VMEM_LIMIT, **kw)


def _sigmoid(x):
    return 0.5 * jnp.tanh(0.5 * x) + 0.5


def _sds(shape, dtype):
    return jax.ShapeDtypeStruct(shape, dtype)


def _place():
    x, y, c = lax.axis_index("x"), lax.axis_index("y"), lax.axis_index("c")
    chips = [(1 - x, y), (x, 1 - y), (1 - x, 1 - y)]
    return x, y, c, chips


def _slot(ref, p):
    return ref.at[4 * p[0] + 2 * p[1] + p[2]]


def _remote(src, dst, send_sem, recv_sem, to):
    return pltpu.make_async_remote_copy(src_ref=src, dst_ref=dst, send_sem=send_sem, recv_sem=recv_sem,
                                        device_id=to, device_id_type=MESH)


def _gather_direct(arrs):
    n = len(arrs)

    def copies(ins, outs, sems):
        send_sems, recv_sems, local_sems = sems
        x, y, c, chips = _place()
        me = (x, y, c)
        peers = [(x, y, 1 - c)] + [(*chip, c) for chip in chips]
        local = [pltpu.make_async_copy(ins[a], _slot(outs[a], me), local_sems.at[a]) for a in range(n)]
        sends = [_remote(ins[a], _slot(outs[a], me), send_sems.at[a, k], recv_sems.at[a, k], peer)
                 for a in range(n) for k, peer in enumerate(peers)]
        arrivals = [_remote(ins[a], _slot(outs[a], peer), send_sems.at[a, k], recv_sems.at[a, k], peer)
                    for a in range(n) for k, peer in enumerate(peers)]
        return local, sends, arrivals

    def start(ins, outs, sems):
        local, sends, _ = copies(ins, outs, sems)
        for cp in local + sends:
            cp.start()

    def finish(ins, outs, sems):
        local, sends, arrivals = copies(ins, outs, sems)
        for cp in arrivals:
            cp.wait_recv()
        for cp in sends:
            cp.wait_send()
        for cp in local:
            cp.wait()

    dma = pltpu.SemaphoreType.DMA
    return _Exchange(arrs, [_sds((N_DEV,) + a.shape, a.dtype) for a in arrs], [dma((n, 4)), dma((n, 4)), dma((n,))],
                     start, finish)


def _gather_forward(gathered):
    n = len(gathered)

    def copies(ins, outs, sems):
        send_sems, recv_sems = sems
        x, y, c, chips = _place()
        sibling = (x, y, 1 - c)
        sends = [_remote(_slot(ins[a], (*chip, c)), _slot(outs[a], (*chip, c)), send_sems.at[a, j], recv_sems.at[a, j], sibling)
                 for a in range(n) for j, chip in enumerate(chips)]
        arrivals = [_remote(_slot(ins[a], (*chip, c)), _slot(outs[a], (*chip, 1 - c)), send_sems.at[a, j], recv_sems.at[a, j], sibling)
                    for a in range(n) for j, chip in enumerate(chips)]
        return sends, arrivals

    def start(ins, outs, sems):
        for cp in copies(ins, outs, sems)[0]:
            cp.start()

    def finish(ins, outs, sems):
        sends, arrivals = copies(ins, outs, sems)
        for cp in arrivals:
            cp.wait_recv()
        for cp in sends:
            cp.wait_send()

    dma = pltpu.SemaphoreType.DMA
    return _Exchange(gathered, [_sds(a.shape, a.dtype) for a in gathered], [dma((n, 3)), dma((n, 3))], start, finish,
                     aliases={a: a for a in range(n)})


def _gather_all(arrs):
    direct = _gather_direct(arrs)
    n = len(arrs)
    n_sems = len(direct.sem_shapes)
    dma = pltpu.SemaphoreType.DMA

    def start(ins, outs, sems):
        direct.start(ins, outs, sems[:n_sems])

    def finish(ins, outs, sems):
        direct.finish(ins, outs, sems[:n_sems])
        passed = _gather_forward(outs)
        passed.start(outs, outs, sems[n_sems:])
        passed.finish(outs, outs, sems[n_sems:])

    return _Exchange(arrs, direct.out_shapes, direct.sem_shapes + [dma((n, 3)), dma((n, 3))], start, finish)


def _pair_exchange(arrs):
    n = len(arrs)

    def copies(ins, outs, sems):
        x, y, c, _ = _place()
        return [_remote(ins[a].at[:, 1 - c], outs[a], sems[0].at[a], sems[1].at[a], (x, y, 1 - c)) for a in range(n)]

    def start(ins, outs, sems):
        for cp in copies(ins, outs, sems):
            cp.start()

    def finish(ins, outs, sems):
        for cp in copies(ins, outs, sems):
            cp.wait()

    dma = pltpu.SemaphoreType.DMA
    return _Exchange(arrs, [_sds((4,) + a.shape[2:], a.dtype) for a in arrs], [dma((n,)), dma((n,))], start, finish)


def _chip_exchange(arrs):
    n = len(arrs)

    def copies(ins, outs, sems):
        x, y, c, chips = _place()
        return [_remote(ins[a].at[1 + j], outs[a].at[j], sems[0].at[a, j], sems[1].at[a, j], (*chip, c))
                for a in range(n) for j, chip in enumerate(chips)]

    def start(ins, outs, sems):
        for cp in copies(ins, outs, sems):
            cp.start()

    def finish(ins, outs, sems):
        for cp in copies(ins, outs, sems):
            cp.wait()

    dma = pltpu.SemaphoreType.DMA
    return _Exchange(arrs, [_sds((3,) + a.shape[1:], a.dtype) for a in arrs], [dma((n, 3)), dma((n, 3))], start, finish)


def _gather_rows(vec, name, comm=()):
    def body(in_ref, out_ref, send_sems, recv_sems, local_sem):
        x, y, c, _ = _place()
        me = 4 * x + 2 * y + c
        mine = pltpu.make_async_copy(in_ref, out_ref.at[me], local_sem)
        mine.start()
        copies = []
        for k in range(1, N_DEV):
            to = (1 - x if k & 4 else x, 1 - y if k & 2 else y, 1 - c if k & 1 else c)
            copies.append(pltpu.make_async_remote_copy(
                src_ref=in_ref, dst_ref=out_ref.at[me], send_sem=send_sems.at[k - 1], recv_sem=recv_sems.at[k - 1],
                device_id=to, device_id_type=MESH))
        for cp in copies:
            cp.start()
        for cp in copies:
            cp.wait()
        mine.wait()

    return _call(
        body, name=name, out_shape=_sds((N_DEV,) + vec.shape, vec.dtype), in_specs=[ANY], out_specs=ANY,
        scratch_shapes=[pltpu.SemaphoreType.DMA((7,)), pltpu.SemaphoreType.DMA((7,)), pltpu.SemaphoreType.DMA],
    )(vec)


def _for_row_groups(tm, fn):
    def step(i, carry):
        fn(pl.ds(pl.multiple_of(i * SUB_ROWS, SUB_ROWS), SUB_ROWS), i * SUB_ROWS)
        return carry

    lax.fori_loop(0, tm // SUB_ROWS, step, 0)


def _row_parts(tm, n=4):
    units = tm // BF16_ROWS
    sizes = [(units // n + (1 if i < units % n else 0)) * BF16_ROWS for i in range(n)]
    starts = [sum(sizes[:i]) for i in range(n)]
    return [slice(s, s + z) for s, z in zip(starts, sizes) if z]


def _once(shape, index_map):
    return pl.BlockSpec(shape, index_map, pipeline_mode=pl.Buffered(1))


def _rms_fwd(hs, g, tm, name, comm=()):
    rows, d = hs.shape

    def body(hs_ref, g_ref, h_ref, r_ref):
        def group(rs, _):
            xv = hs_ref[rs, :]
            r = lax.rsqrt(jnp.mean(xv * xv, axis=-1, keepdims=True) + EPS)
            h_ref[rs, :] = (xv * r * g_ref[...]).astype(BF16)
            r_ref[rs, :] = r

        _for_row_groups(tm, group)

    return _call(
        body, comm=comm, name=name, grid=(rows // tm,),
        in_specs=[pl.BlockSpec((tm, d), lambda i: (i, 0)), pl.BlockSpec((1, d), lambda i: (0, 0))],
        out_specs=[pl.BlockSpec((tm, d), lambda i: (i, 0)), pl.BlockSpec((tm, 1), lambda i: (i, 0))],
        out_shape=[_sds((rows, d), BF16), _sds((rows, 1), F32)], compiler_params=_params(),
    )(hs, g)


def _ffn_gu(h, wg, wu, tm, name, comm=()):
    rows, d = h.shape
    nb, fb, _ = wg.shape

    def body(h_ref, wg_ref, wu_ref, g_ref, u_ref):
        hv = h_ref[...]
        g_ref[0] = lax.dot_general(hv, wg_ref[0], NT, preferred_element_type=F32).astype(BF16)
        u_ref[0] = lax.dot_general(hv, wu_ref[0], NT, preferred_element_type=F32).astype(BF16)

    wspec = pl.BlockSpec((1, fb, d), lambda j, i: (j, 0, 0))
    ospec = pl.BlockSpec((1, tm, fb), lambda j, i: (j, i, 0))
    return _call(
        body, comm=comm, name=name, grid=(nb, rows // tm),
        in_specs=[pl.BlockSpec((tm, d), lambda j, i: (i, 0)), wspec, wspec],
        out_specs=[ospec, ospec], out_shape=[_sds((nb, rows, fb), BF16)] * 2, compiler_params=_params(),
    )(h, wg, wu)


def _ffn_proj(h, w, tm, name, comm=()):
    rows, d = h.shape
    nb, fb, _ = w.shape

    def body(h_ref, w_ref, p_ref):
        p_ref[0] = lax.dot_general(h_ref[...], w_ref[0], NT, preferred_element_type=F32).astype(BF16)

    return _call(
        body, comm=comm, name=name, grid=(nb, rows // tm),
        in_specs=[pl.BlockSpec((tm, d), lambda j, i: (i, 0)), pl.BlockSpec((1, fb, d), lambda j, i: (j, 0, 0))],
        out_specs=pl.BlockSpec((1, tm, fb), lambda j, i: (j, i, 0)), out_shape=_sds((nb, rows, fb), BF16),
        compiler_params=_params(),
    )(h, w)


def _down_product(g_ref, u_ref, wd_ref, acc_ref):
    part = None
    for b in range(g_ref.shape[0]):
        gv = g_ref[b]
        a = gv * _sigmoid(gv) * u_ref[b]
        p = jnp.dot(a, wd_ref[b], preferred_element_type=F32)
        part = p if part is None else part + p
    acc_ref[...] = part


def _ffn_down_norm(g, u, wd, hs, gn, tm, name, comm=()):
    nb, rows, fb = g.shape
    d = hs.shape[1]

    def body(g_ref, u_ref, wd_ref, hs_ref, gn_ref, hsn_ref, hn_ref, rn_ref, acc_ref):
        _down_product(g_ref, u_ref, wd_ref, acc_ref)

        def group(rs, _):
            hsn = hs_ref[rs, :] + FFN_RES_SCALE * acc_ref[rs, :]
            r = lax.rsqrt(jnp.mean(hsn * hsn, axis=-1, keepdims=True) + EPS)
            hsn_ref[rs, :] = hsn
            hn_ref[rs, :] = (hsn * r * gn_ref[...]).astype(BF16)
            rn_ref[rs, :] = r

        _for_row_groups(tm, group)

    aspec = pl.BlockSpec((nb, tm, fb), lambda i: (0, i, 0))
    row = pl.BlockSpec((tm, d), lambda i: (i, 0))
    return _call(
        body, comm=comm, name=name, grid=(rows // tm,),
        in_specs=[aspec, aspec, _once((nb, fb, d), lambda i: (0, 0, 0)), row, pl.BlockSpec((1, d), lambda i: (0, 0))],
        out_specs=[row, row, pl.BlockSpec((tm, 1), lambda i: (i, 0))],
        out_shape=[_sds((rows, d), F32), _sds((rows, d), BF16), _sds((rows, 1), F32)],
        scratch_shapes=[pltpu.VMEM((tm, d), F32)], compiler_params=_params(),
    )(g, u, wd, hs, gn)


def _ffn_down_loss(g, u, wd, hs, gf, tgt, n_seq, tm, name, comm=()):
    nb, rows, fb = g.shape
    d = hs.shape[1]
    nt = rows // tm

    def body(g_ref, u_ref, wd_ref, hs_ref, gf_ref, tgt_ref, dhs_ref, dhsb_ref, loss_ref, dgf_ref, acc_ref):
        i = pl.program_id(0)
        _down_product(g_ref, u_ref, wd_ref, acc_ref)
        loss_ref[0] = jnp.zeros((1, d), F32)
        dgf_ref[0] = jnp.zeros((1, d), F32)

        def group(rs, r0):
            hs3 = hs_ref[rs, :] + FFN_RES_SCALE * acc_ref[rs, :]
            r = lax.rsqrt(jnp.mean(hs3 * hs3, axis=-1, keepdims=True) + EPS)
            gfv = gf_ref[...]
            y = hs3 * r
            rowid = i * tm + r0 + lax.broadcasted_iota(jnp.int32, (SUB_ROWS, 1), 0)
            err = jnp.where(rowid < n_seq, y * gfv - tgt_ref[rs, :], 0.0)
            loss_ref[0] += jnp.sum(err * err, axis=0, keepdims=True)
            dout = err * (1.0 / d)
            dgf_ref[0] += jnp.sum(dout * y, axis=0, keepdims=True)
            t = dout * gfv
            dhs = r * t - hs3 * (r * r * r) * jnp.mean(t * hs3, axis=-1, keepdims=True)
            dhs_ref[rs, :] = dhs
            dhsb_ref[rs, :] = dhs.astype(BF16)

        _for_row_groups(tm, group)

    aspec = pl.BlockSpec((nb, tm, fb), lambda i: (0, i, 0))
    row = pl.BlockSpec((tm, d), lambda i: (i, 0))
    part = pl.BlockSpec((1, 1, d), lambda i: (i, 0, 0))
    return _call(
        body, comm=comm, name=name, grid=(nt,),
        in_specs=[aspec, aspec, _once((nb, fb, d), lambda i: (0, 0, 0)), row, pl.BlockSpec((1, d), lambda i: (0, 0)), row],
        out_specs=[row, row, part, part],
        out_shape=[_sds((rows, d), F32), _sds((rows, d), BF16), _sds((nt, 1, d), F32), _sds((nt, 1, d), F32)],
        scratch_shapes=[pltpu.VMEM((tm, d), F32)], compiler_params=_params(),
    )(g, u, wd, hs, gf, tgt)


def _ffn_bwd_da(do, wd, g, u, tm, name, comm=()):
    nb, rows, fb = g.shape
    d = do.shape[1]

    def body(do_ref, wd_ref, g_ref, u_ref, dg_ref, du_ref, a_ref):
        rs = pl.ds(pl.multiple_of(pl.program_id(1) * tm, tm), tm)
        da = (FFN_RES_SCALE * lax.dot_general(do_ref[rs, :], wd_ref[0], NT, preferred_element_type=F32)).astype(BF16)
        gv, uv = g_ref[0], u_ref[0]
        s = _sigmoid(gv)
        sg = gv * s
        a_ref[0] = sg * uv
        du_ref[0] = da * sg
        dg_ref[0] = da * uv * (s + sg * (1.0 - s))

    aspec = pl.BlockSpec((1, tm, fb), lambda j, i: (j, i, 0))
    return _call(
        body, comm=comm, name=name, grid=(nb, rows // tm),
        in_specs=[_once((rows, d), lambda j, i: (0, 0)), pl.BlockSpec((1, fb, d), lambda j, i: (j, 0, 0)), aspec, aspec],
        out_specs=[aspec] * 3, out_shape=[_sds((nb, rows, fb), BF16)] * 3, compiler_params=_params(),
    )(do, wd, g, u)


def _rms_bwd(tm, d, dh_ref, hs_ref, r_ref, gn_ref, dres_ref, dhs_ref, dhsb_ref, dgn_ref):
    dgn_ref[0] = jnp.zeros((1, d), F32)

    def group(rs, _):
        dh, hs, r = dh_ref[rs, :], hs_ref[rs, :], r_ref[rs, :]
        dgn_ref[0] += jnp.sum(dh * (hs * r), axis=0, keepdims=True)
        t = dh * gn_ref[...]
        dhs = dres_ref[rs, :] + r * t - hs * (r * r * r) * jnp.mean(t * hs, axis=-1, keepdims=True)
        dhs_ref[rs, :] = dhs
        dhsb_ref[rs, :] = dhs.astype(BF16)

    _for_row_groups(tm, group)


def _ffn_bwd_dh(dg, du, wg, wu, hs, r, gn, dres, tm, name, comm=(), zeros=None):
    nb, rows, fb = dg.shape
    d = hs.shape[1]
    nt = rows // tm
    extra = [] if zeros is None else [zeros]

    def body(dg_ref, du_ref, wg_ref, wu_ref, hs_ref, r_ref, gn_ref, dres_ref, *rest):
        dhs_ref, dhsb_ref, dgn_ref, acc_ref = rest[len(extra):]
        j = pl.program_id(1)

        @pl.when(j == 0)
        def _():
            acc_ref[...] = jnp.zeros_like(acc_ref)

        acc_ref[...] += (jnp.dot(dg_ref[0], wg_ref[0], preferred_element_type=F32)
                         + jnp.dot(du_ref[0], wu_ref[0], preferred_element_type=F32))

        @pl.when(j == nb - 1)
        def _():
            _rms_bwd(tm, d, acc_ref, hs_ref, r_ref, gn_ref, dres_ref, dhs_ref, dhsb_ref, dgn_ref)

    aspec = pl.BlockSpec((1, tm, fb), lambda i, j: (j, i, 0))
    wspec = pl.BlockSpec((1, fb, d), lambda i, j: (j, 0, 0))
    row = pl.BlockSpec((tm, d), lambda i, j: (i, 0))
    once = _once((tm, d), lambda i, j: (i, 0))
    return _call(
        body, comm=comm, name=name, grid=(nt, nb),
        in_specs=[aspec, aspec, wspec, wspec, once, pl.BlockSpec((tm, 1), lambda i, j: (i, 0)),
                  pl.BlockSpec((1, d), lambda i, j: (0, 0)), once] + [ANY] * len(extra),
        out_specs=[row, row, pl.BlockSpec((1, 1, d), lambda i, j: (i, 0, 0))],
        out_shape=[_sds((rows, d) if zeros is None else zeros.shape, F32), _sds((rows, d), BF16), _sds((nt, 1, d), F32)],
        scratch_shapes=[pltpu.VMEM((tm, d), F32)], compiler_params=_params(),
        input_output_aliases={8: 0} if extra else {},
    )(dg, du, wg, wu, hs, r, gn, dres, *extra)


def _ffn_bwd_dw(h, dg, du, a, do, tk, name, comm=()):
    nb, rows, fb = dg.shape
    d = h.shape[1]
    nk = rows // tk

    def body(h_ref, dg_ref, du_ref, a_ref, do_ref, dwg_ref, dwu_ref, dwd_ref, accg, accu, accd):
        k = pl.program_id(1)

        @pl.when(k == 0)
        def _():
            accg[...] = jnp.zeros_like(accg)
            accu[...] = jnp.zeros_like(accu)
            accd[...] = jnp.zeros_like(accd)

        hv = h_ref[...]
        accg[...] += lax.dot_general(dg_ref[0], hv, TN, preferred_element_type=F32)
        accu[...] += lax.dot_general(du_ref[0], hv, TN, preferred_element_type=F32)
        accd[...] += lax.dot_general(a_ref[0], do_ref[...], TN, preferred_element_type=F32)

        @pl.when(k == nk - 1)
        def _():
            dwg_ref[0] = accg[...].astype(BF16)
            dwu_ref[0] = accu[...].astype(BF16)
            dwd_ref[0] = (FFN_RES_SCALE * accd[...]).astype(BF16)

    aspec = pl.BlockSpec((1, tk, fb), lambda j, k: (j, k, 0))
    row = pl.BlockSpec((tk, d), lambda j, k: (k, 0))
    down = pl.BlockSpec((1, fb, d), lambda j, k: (j, 0, 0))
    return _call(
        body, comm=comm, name=name, grid=(nb, nk),
        in_specs=[row, aspec, aspec, aspec, row], out_specs=[down] * 3,
        out_shape=[_sds((nb, fb, d), BF16)] * 3, scratch_shapes=[pltpu.VMEM((fb, d), F32)] * 3,
        compiler_params=_params(),
    )(h, dg, du, a, do)


def _win_fwd(h, w, b, dc, tm, name, comm=()):
    rows, d = h.shape
    ng = w.shape[1] // dc

    def body(h_ref, w_ref, b_ref, u_ref):
        u_ref[...] = (jnp.dot(h_ref[...], w_ref[...], preferred_element_type=F32) + b_ref[...]).astype(BF16)

    return _call(
        body, comm=comm, name=name, grid=(ng, rows // tm),
        in_specs=[pl.BlockSpec((tm, d), lambda m, i: (i, 0)), pl.BlockSpec((d, dc), lambda m, i: (0, m)),
                  pl.BlockSpec((1, dc), lambda m, i: (0, m))],
        out_specs=pl.BlockSpec((tm, dc), lambda m, i: (i, m)), out_shape=_sds((rows, ng * dc), BF16),
        compiler_params=_params(),
    )(h, w, b)


def _win_bwd_dh(du, w, hs, r, gn, dres, tm, name, comm=()):
    rows, d = hs.shape
    ng, _, dc = du.shape
    nt = rows // tm

    def body(du_ref, w_ref, hs_ref, r_ref, gn_ref, dres_ref, dhs_ref, dhsb_ref, dgn_ref, acc_ref):
        m = pl.program_id(1)

        @pl.when(m == 0)
        def _():
            acc_ref[...] = jnp.zeros_like(acc_ref)

        acc_ref[...] += lax.dot_general(du_ref[0], w_ref[...], NT, preferred_element_type=F32)

        @pl.when(m == ng - 1)
        def _():
            _rms_bwd(tm, d, acc_ref, hs_ref, r_ref, gn_ref, dres_ref, dhs_ref, dhsb_ref, dgn_ref)

    row = pl.BlockSpec((tm, d), lambda i, m: (i, 0))
    once = _once((tm, d), lambda i, m: (i, 0))
    return _call(
        body, comm=comm, name=name, grid=(nt, ng),
        in_specs=[pl.BlockSpec((1, tm, dc), lambda i, m: (m, i, 0)), pl.BlockSpec((d, dc), lambda i, m: (0, m)), once,
                  pl.BlockSpec((tm, 1), lambda i, m: (i, 0)), pl.BlockSpec((1, d), lambda i, m: (0, 0)), once],
        out_specs=[row, row, pl.BlockSpec((1, 1, d), lambda i, m: (i, 0, 0))],
        out_shape=[_sds((rows, d), F32), _sds((rows, d), BF16), _sds((nt, 1, d), F32)],
        scratch_shapes=[pltpu.VMEM((tm, d), F32)], compiler_params=_params(),
    )(du, w, hs, r, gn, dres)


def _win_bwd_dw(h, du, tk, name, comm=()):
    rows, d = h.shape
    ng, _, dc = du.shape
    nk = rows // tk

    def body(h_ref, du_ref, dw_ref, db_ref, acc, accb):
        k = pl.program_id(1)

        @pl.when(k == 0)
        def _():
            acc[...] = jnp.zeros_like(acc)
            accb[...] = jnp.zeros_like(accb)

        duv = du_ref[0]
        acc[...] += lax.dot_general(h_ref[...], duv, TN, preferred_element_type=F32)
        accb[...] += jnp.sum(duv.astype(F32), axis=0, keepdims=True)

        @pl.when(k == nk - 1)
        def _():
            dw_ref[...] = acc[...].astype(BF16)
            db_ref[...] = accb[...]

    return _call(
        body, comm=comm, name=name, grid=(ng, nk),
        in_specs=[pl.BlockSpec((tk, d), lambda m, k: (k, 0)), pl.BlockSpec((1, tk, dc), lambda m, k: (m, k, 0))],
        out_specs=[pl.BlockSpec((d, dc), lambda m, k: (0, m)), pl.BlockSpec((1, dc), lambda m, k: (0, m))],
        out_shape=[_sds((d, ng * dc), BF16), _sds((1, ng * dc), F32)],
        scratch_shapes=[pltpu.VMEM((d, dc), F32), pltpu.VMEM((1, dc), F32)], compiler_params=_params(),
    )(h, du)


def _layernorm_silu(zc, lg, lb):
    mu = jnp.mean(zc, axis=-1, keepdims=True)
    xc = zc - mu
    rstd = lax.rsqrt(jnp.mean(xc * xc, axis=-1, keepdims=True) + EPS)
    nrm = xc * rstd
    lin = nrm * lg + lb
    s = _sigmoid(lin)
    return nrm, rstd, lin, s


def _wout_fwd(zc, ysc, wout, hs, lg, lb, gn, rows, tm, name, comm=()):
    dc = zc.shape[1]
    d = hs.shape[1]

    def body(zc_ref, ysc_ref, w_ref, hs_ref, lg_ref, lb_ref, gn_ref, y_ref, hsn_ref, hn_ref, rn_ref):
        def mix(rs, _):
            _, _, lin, s = _layernorm_silu(zc_ref[rs, :], lg_ref[...], lb_ref[...])
            y_ref[rs, :dc] = ysc_ref[rs, :]
            y_ref[rs, dc:] = (lin * s).astype(BF16)

        _for_row_groups(tm, mix)
        hsn_ref[...] = jnp.dot(y_ref[...], w_ref[...], preferred_element_type=F32)

        def norm(rs, _):
            hsn = hs_ref[rs, :] + hsn_ref[rs, :]
            r = lax.rsqrt(jnp.mean(hsn * hsn, axis=-1, keepdims=True) + EPS)
            hsn_ref[rs, :] = hsn
            hn_ref[rs, :] = (hsn * r * gn_ref[...]).astype(BF16)
            rn_ref[rs, :] = r

        _for_row_groups(tm, norm)

    half = pl.BlockSpec((tm, dc), lambda i: (i, 0))
    row = pl.BlockSpec((tm, d), lambda i: (i, 0))
    vec_c = pl.BlockSpec((1, dc), lambda i: (0, 0))
    return _call(
        body, comm=comm, name=name, grid=(rows // tm,),
        in_specs=[half, half, _once((2 * dc, d), lambda i: (0, 0)), row, vec_c, vec_c,
                  pl.BlockSpec((1, d), lambda i: (0, 0))],
        out_specs=[pl.BlockSpec((tm, 2 * dc), lambda i: (i, 0)), row, row, pl.BlockSpec((tm, 1), lambda i: (i, 0))],
        out_shape=[_sds((rows, 2 * dc), BF16), _sds((rows, d), F32), _sds((rows, d), BF16), _sds((rows, 1), F32)],
        compiler_params=_params(),
    )(zc, ysc, wout, hs, lg, lb, gn)


def _wout_bwd_dy(do, wout, zc, lg, lb, tm, name, comm=()):
    rows, d = do.shape
    dc = zc.shape[1]
    nt = rows // tm

    def body(do_ref, w_ref, zc_ref, lg_ref, lb_ref, dysc_ref, dzc_ref, dlg_ref, dlb_ref, dy_ref):
        dy_ref[...] = lax.dot_general(do_ref[...], w_ref[...], NT, preferred_element_type=F32)
        dlb_ref[0] = jnp.zeros((1, dc), F32)
        dlg_ref[0] = jnp.zeros((1, dc), F32)

        def group(rs, _):
            dysc_ref[rs, :] = dy_ref[rs, :dc].astype(BF16)
            nrm, rstd, lin, s = _layernorm_silu(zc_ref[rs, :], lg_ref[...], lb_ref[...])
            dl = dy_ref[rs, dc:] * (s * (1.0 + lin * (1.0 - s)))
            dlb_ref[0] += jnp.sum(dl, axis=0, keepdims=True)
            dlg_ref[0] += jnp.sum(dl * nrm, axis=0, keepdims=True)
            dn = dl * lg_ref[...]
            dzc_ref[rs, :] = rstd * (dn - jnp.mean(dn, axis=-1, keepdims=True)
                                     - nrm * jnp.mean(dn * nrm, axis=-1, keepdims=True))

        _for_row_groups(tm, group)

    half = pl.BlockSpec((tm, dc), lambda i: (i, 0))
    vec_c = pl.BlockSpec((1, dc), lambda i: (0, 0))
    part = pl.BlockSpec((1, 1, dc), lambda i: (i, 0, 0))
    return _call(
        body, comm=comm, name=name, grid=(nt,),
        in_specs=[pl.BlockSpec((tm, d), lambda i: (i, 0)), _once((2 * dc, d), lambda i: (0, 0)), half, vec_c, vec_c],
        out_specs=[half, half, part, part],
        out_shape=[_sds((rows, dc), BF16), _sds((rows, dc), F32), _sds((nt, 1, dc), F32), _sds((nt, 1, dc), F32)],
        scratch_shapes=[pltpu.VMEM((tm, 2 * dc), F32)], compiler_params=_params(),
    )(do, wout, zc, lg, lb)


def _wout_bwd_dw(y, do, nb, tk, name, comm=()):
    rows, k2 = y.shape
    d = do.shape[1]
    nk = rows // tk

    def body(y_ref, do_ref, dw_ref, acc):
        k = pl.program_id(0)

        @pl.when(k == 0)
        def _():
            acc[...] = jnp.zeros_like(acc)

        acc[...] += lax.dot_general(y_ref[...], do_ref[...], TN, preferred_element_type=F32)

        @pl.when(k == nk - 1)
        def _():
            dw_ref[...] = acc[...].astype(BF16)

    return _call(
        body, comm=comm, name=name, grid=(nk,),
        in_specs=[pl.BlockSpec((tk, k2), lambda k: (k, 0)), pl.BlockSpec((tk, d), lambda k: (k, 0))],
        out_specs=_once((k2, d), lambda k: (0, 0)), out_shape=_sds((k2, d), BF16),
        scratch_shapes=[pltpu.VMEM((k2, d), F32)], compiler_params=_params(),
    )(y, do).reshape(nb, k2 // nb, d)


def _windows(win, n_res):
    length = win.shape[0]
    return [win if r == 0 else pltpu.roll(win, length - r, 0) for r in range(n_res)]


def _taps(src_ref, start, offsets, ch):
    span = -(-(max(offsets) + ch) // SUBLANES) * SUBLANES
    win = src_ref[pl.ds(start, span), :]
    shifted = _windows(win, min(SUBLANES, max(offsets) + 1))
    return [shifted[o % SUBLANES][(o // SUBLANES) * SUBLANES:(o // SUBLANES) * SUBLANES + ch] for o in offsets]


def _rows8(v):
    return jnp.sum(v.reshape(v.shape[0] // SUBLANES, SUBLANES, v.shape[1]), axis=0)


def _conv_geometry(n_seq, n_meta):
    base = CONV_PAD + n_meta
    off_cf = base - (CF_WIDTH - 1)
    off_sc = base - (SC_WIDTH - 1)
    logical = -(-(n_meta + n_seq) // CONV_CH) * CONV_CH
    return base, off_cf, off_sc, logical


def _fill_conv_inputs(c_ref, v_ref, a_ref, g_ref, scv, sz, n_seq, n_meta):
    base = CONV_PAD + n_meta
    cb = scv.shape[1]
    scv[0:CONV_PAD, :] = jnp.zeros((CONV_PAD, cb), F32)
    sz[0:CONV_PAD, :] = jnp.zeros((CONV_PAD, cb), F32)

    def put(src, dst, n):
        cv = c_ref[src, :].astype(F32) * v_ref[src, :].astype(F32)
        scv[dst, :] = cv
        sz[dst, :] = a_ref[src, :].astype(F32) * _sigmoid(g_ref[src, :].astype(F32))

    put(pl.ds(n_seq, n_meta), pl.ds(CONV_PAD, n_meta), n_meta)

    def chunk(i, carry):
        t0 = pl.multiple_of(i * CONV_CH, CONV_CH)
        put(pl.ds(t0, CONV_CH), pl.ds(base + t0, CONV_CH), CONV_CH)
        return carry

    lax.fori_loop(0, n_seq // CONV_CH, chunk, 0)


def _conv_fwd(u, wsc, wcf, cbias, n_seq, n_meta, name, comm=()):
    rows = u.shape[0]
    nb, _, cb = wsc.shape
    dc = nb * cb
    base, off_cf, off_sc, _ = _conv_geometry(n_seq, n_meta)
    a_cf, a_sc = off_cf // SUBLANES * SUBLANES, off_sc // SUBLANES * SUBLANES
    ch = CONV_CH

    def body(b_ref, c_ref, v_ref, a_ref, g_ref, wsc_ref, wcf_ref, cb_ref, ysc_ref, zc_ref, scv, sz):
        _fill_conv_inputs(c_ref, v_ref, a_ref, g_ref, scv, sz, n_seq, n_meta)
        w3, w31, bias = wsc_ref[0], wcf_ref[0], cb_ref[...]

        def chunk(i, carry):
            t0 = pl.multiple_of(i * ch, ch)
            acc = jnp.zeros((ch, cb), F32)
            for k, win in enumerate(_taps(sz, t0 + a_cf, [off_cf - a_cf + k for k in range(CF_WIDTH)], ch)):
                acc = acc + win * w31[k:k + 1, :]
            zc_ref[pl.ds(t0, ch), :] = acc + bias
            s = jnp.zeros((ch, cb), F32)
            for k, win in enumerate(_taps(scv, t0 + a_sc, [off_sc - a_sc + k for k in range(SC_WIDTH)], ch)):
                s = s + win * w3[k:k + 1, :]
            ysc_ref[pl.ds(t0, ch), :] = (b_ref[pl.ds(t0, ch), :].astype(F32) * s).astype(BF16)
            return carry

        lax.fori_loop(0, n_seq // ch, chunk, 0)
        ysc_ref[n_seq:rows, :] = jnp.zeros((rows - n_seq, cb), BF16)
        zc_ref[n_seq:rows, :] = jnp.zeros((rows - n_seq, cb), F32)

    ucol = [pl.BlockSpec((rows, cb), functools.partial(lambda m, j: (0, m * nb + j), m)) for m in range(5)]
    blk = pl.BlockSpec((rows, cb), lambda j: (0, j))
    return _call(
        body, comm=comm, name=name, grid=(nb,),
        in_specs=ucol + [pl.BlockSpec((1, SC_WIDTH, cb), lambda j: (j, 0, 0)),
                         pl.BlockSpec((1, CF_WIDTH, cb), lambda j: (j, 0, 0)), pl.BlockSpec((1, cb), lambda j: (0, j))],
        out_specs=[blk, blk], out_shape=[_sds((rows, dc), BF16), _sds((rows, dc), F32)],
        scratch_shapes=[pltpu.VMEM((base + n_seq, cb), F32)] * 2, compiler_params=_params(),
    )(u, u, u, u, u, wsc, wcf, cbias)


def _conv_bwd(u, dysc, dzc, wsc, wcf, n_seq, n_meta, name, comm=()):
    rows = u.shape[0]
    nb, _, cb = wsc.shape
    dc = nb * cb
    base, off_cf, off_sc, logical = _conv_geometry(n_seq, n_meta)
    a_cf, a_sc = off_cf // SUBLANES * SUBLANES, off_sc // SUBLANES * SUBLANES
    ch = CONV_CH
    tail = CONV_PAD

    def body(b_ref, c_ref, v_ref, a_ref, g_ref, dysc_ref, dzc_ref, wsc_ref, wcf_ref,
             du_ref, dwsc_ref, dwcf_ref, dcb_ref,
             scv, sz, sds_, sdz, dlcv, dlz, accsc, acccf, accb):
        _fill_conv_inputs(c_ref, v_ref, a_ref, g_ref, scv, sz, n_seq, n_meta)
        w3, w31 = wsc_ref[0], wcf_ref[0]
        dub_ref, duc_ref, duv_ref, dua_ref, dug_ref = (du_ref.at[m] for m in range(5))
        sds_[0:n_meta, :] = jnp.zeros((n_meta, cb), F32)
        sdz[0:n_meta, :] = jnp.zeros((n_meta, cb), F32)
        behind = logical + tail - (n_meta + n_seq)
        sds_[n_meta + n_seq:logical + tail, :] = jnp.zeros((behind, cb), F32)
        sdz[n_meta + n_seq:logical + tail, :] = jnp.zeros((behind, cb), F32)
        accsc[...] = jnp.zeros_like(accsc)
        acccf[...] = jnp.zeros_like(acccf)
        accb[...] = jnp.zeros_like(accb)

        def forward_chunk(i, carry):
            t0 = pl.multiple_of(i * ch, ch)
            rws = pl.ds(t0, ch)
            dy = dysc_ref[rws, :].astype(F32)
            ds = dy * b_ref[rws, :].astype(F32)
            dz = dzc_ref[rws, :]
            sds_[pl.ds(n_meta + t0, ch), :] = ds
            sdz[pl.ds(n_meta + t0, ch), :] = dz
            s = jnp.zeros((ch, cb), F32)
            for k, win in enumerate(_taps(scv, t0 + a_sc, [off_sc - a_sc + k for k in range(SC_WIDTH)], ch)):
                s = s + win * w3[k:k + 1, :]
                accsc[k * SUBLANES:(k + 1) * SUBLANES, :] += _rows8(ds * win)
            dub_ref[rws, :] = (dy * s).astype(BF16)
            for k, win in enumerate(_taps(sz, t0 + a_cf, [off_cf - a_cf + k for k in range(CF_WIDTH)], ch)):
                acccf[k * SUBLANES:(k + 1) * SUBLANES, :] += _rows8(dz * win)
            accb[...] += _rows8(dz)
            return carry

        lax.fori_loop(0, n_seq // ch, forward_chunk, 0)

        def backward_chunk(i, carry):
            p0 = pl.multiple_of(i * ch, ch)
            dcv = jnp.zeros((ch, cb), F32)
            for k, win in enumerate(_taps(sds_, p0, [SC_WIDTH - 1 - k for k in range(SC_WIDTH)], ch)):
                dcv = dcv + win * w3[k:k + 1, :]
            dlcv[pl.ds(p0, ch), :] = dcv
            dzi = jnp.zeros((ch, cb), F32)
            for k, win in enumerate(_taps(sdz, p0, [CF_WIDTH - 1 - k for k in range(CF_WIDTH)], ch)):
                dzi = dzi + win * w31[k:k + 1, :]
            dlz[pl.ds(p0, ch), :] = dzi
            return carry

        lax.fori_loop(0, logical // ch, backward_chunk, 0)

        def gates(phys, logi):
            dcv, dzi = dlcv[logi, :], dlz[logi, :]
            duc_ref[phys, :] = (dcv * v_ref[phys, :].astype(F32)).astype(BF16)
            duv_ref[phys, :] = (dcv * c_ref[phys, :].astype(F32)).astype(BF16)
            s = _sigmoid(g_ref[phys, :].astype(F32))
            dua_ref[phys, :] = (dzi * s).astype(BF16)
            dug_ref[phys, :] = (dzi * a_ref[phys, :].astype(F32) * s * (1.0 - s)).astype(BF16)

        def gate_chunk(i, carry):
            t0 = pl.multiple_of(i * ch, ch)
            gates(pl.ds(t0, ch), pl.ds(n_meta + t0, ch))
            return carry

        lax.fori_loop(0, n_seq // ch, gate_chunk, 0)
        gates(pl.ds(n_seq, n_meta), pl.ds(0, n_meta))
        dub_ref[n_seq:rows, :] = jnp.zeros((rows - n_seq, cb), BF16)
        pad0 = n_seq + n_meta
        for ref in (duc_ref, duv_ref, dua_ref, dug_ref):
            ref[pad0:rows, :] = jnp.zeros((rows - pad0, cb), BF16)
        dwsc_ref[0] = jnp.sum(accsc[...].reshape(SC_WIDTH, SUBLANES, cb), axis=1)
        dwcf_ref[0] = jnp.sum(acccf[...].reshape(CF_WIDTH, SUBLANES, cb), axis=1)
        dcb_ref[...] = jnp.sum(accb[...], axis=0, keepdims=True)

    ucol = [pl.BlockSpec((rows, cb), functools.partial(lambda m, j: (0, m * nb + j), m)) for m in range(5)]
    blk = pl.BlockSpec((dysc.shape[0], cb), lambda j: (0, j))
    wsc_spec = pl.BlockSpec((1, SC_WIDTH, cb), lambda j: (j, 0, 0))
    wcf_spec = pl.BlockSpec((1, CF_WIDTH, cb), lambda j: (j, 0, 0))
    outs = _call(
        body, comm=comm, name=name, grid=(nb,),
        in_specs=ucol + [blk, blk, wsc_spec, wcf_spec],
        out_specs=[pl.BlockSpec((5, rows, cb), lambda j: (0, 0, j)), wsc_spec, wcf_spec,
                   pl.BlockSpec((1, cb), lambda j: (0, j))],
        out_shape=[_sds((5, rows, dc), BF16), _sds((nb, SC_WIDTH, cb), F32), _sds((nb, CF_WIDTH, cb), F32),
                   _sds((1, dc), F32)],
        scratch_shapes=[pltpu.VMEM((base + n_seq, cb), F32)] * 2 + [pltpu.VMEM((logical + tail, cb), F32)] * 2
        + [pltpu.VMEM((logical, cb), F32)] * 2
        + [pltpu.VMEM((SC_WIDTH * SUBLANES, cb), F32), pltpu.VMEM((CF_WIDTH * SUBLANES, cb), F32),
           pltpu.VMEM((SUBLANES, cb), F32)],
        compiler_params=_params(),
    )(u, u, u, u, u, dysc, dzc, wsc, wcf)
    return outs


def _row_tile(rows, cols):
    return rows // 4 if rows % 64 == 0 and rows * cols >= (1 << 18) else rows


def _pair_sum(grad, sib, idx, name, comm=()):
    _, rows, cols = grad.shape
    tr = rows

    def body(idx_ref, g_ref, s_ref, o_ref):
        o_ref[0] = (g_ref[0].astype(F32) + s_ref[0].astype(F32)).astype(o_ref.dtype)

    return _call(
        body, name=name,
        grid_spec=pltpu.PrefetchScalarGridSpec(
            num_scalar_prefetch=1, grid=(4, rows // tr),
            in_specs=[pl.BlockSpec((1, tr, cols), lambda k, i, idx_ref: (idx_ref[k], i, 0)),
                      pl.BlockSpec((1, tr, cols), lambda k, i, idx_ref: (idx_ref[4 + k], i, 0))],
            out_specs=pl.BlockSpec((1, tr, cols), lambda k, i, idx_ref: (k, i, 0))),
        out_shape=_sds((4, rows, cols), grad.dtype), compiler_params=_params(),
    )(idx, grad, sib)


def _adamw_math(w, g, m, v):
    m = ADAM_B1 * m + (1.0 - ADAM_B1) * g
    v = ADAM_B2 * v + (1.0 - ADAM_B2) * (g * g)
    m_hat = m / (1.0 - ADAM_B1 ** ADAM_STEP)
    v_hat = v / (1.0 - ADAM_B2 ** ADAM_STEP)
    delta = -ADAM_LR * (m_hat / (jnp.sqrt(v_hat) + ADAM_EPS) + ADAM_WD * w)
    return delta, m, v


def _adamw_sharded(own, got, w, m, v, name, comm=()):
    rows, cols = w.shape
    tr = _row_tile(rows, cols)

    def body(own_ref, g0_ref, g1_ref, g2_ref, w_ref, m_ref, v_ref, g_ref, d_ref, nm_ref, nv_ref):
        g = own_ref[0].astype(F32) + g0_ref[0].astype(F32) + g1_ref[0].astype(F32) + g2_ref[0].astype(F32)
        delta, nm, nv = _adamw_math(w_ref[...], g, m_ref[...], v_ref[...])
        g_ref[...] = g
        d_ref[...] = delta
        nm_ref[...] = nm
        nv_ref[...] = nv

    flat = pl.BlockSpec((tr, cols), lambda i: (i, 0))
    slot = [pl.BlockSpec((1, tr, cols), functools.partial(lambda k, i: (k, i, 0), k)) for k in range(3)]
    return _call(
        body, comm=comm, name=name, grid=(rows // tr,),
        in_specs=[slot[0]] + slot + [flat] * 3, out_specs=[flat] * 4, out_shape=[_sds((rows, cols), F32)] * 4,
        compiler_params=_params(),
    )(own, got, got, got, w, m, v)


def _adamw_replicated(gathered, segs, ws, ms, vs, loss_scale, name, comm=()):
    n = len(ws)

    def body(*refs):
        gat = refs[0]
        w_refs, m_refs, v_refs = refs[1:1 + n], refs[1 + n:1 + 2 * n], refs[1 + 2 * n:1 + 3 * n]
        outs = refs[1 + 3 * n:]

        def total(off, width):
            s = gat[0, :, off:off + width]
            for k in range(1, N_DEV):
                s = s + gat[k, :, off:off + width]
            return s

        outs[0][...] = loss_scale * total(segs[n][0], segs[n][1])
        for p in range(n):
            g = total(*segs[p])
            delta, nm, nv = _adamw_math(w_refs[p][...], g, m_refs[p][...], v_refs[p][...])
            for q, val in enumerate((g, delta, nm, nv)):
                outs[1 + 4 * p + q][...] = val
        for e, seg in enumerate(segs[n + 1:]):
            outs[1 + 4 * n + e][...] = total(*seg)

    return _call(
        body, name=name,
        out_shape=[_sds((1, segs[n][1]), F32)] + [_sds(w.shape, F32) for w in ws for _ in range(4)]
        + [_sds((1, seg[1]), F32) for seg in segs[n + 1:]],
        compiler_params=_params(),
    )(gathered, *ws, *ms, *vs)


def _adamw_plain(g, w, m, v, name):
    def body(g_ref, w_ref, m_ref, v_ref, d_ref, nm_ref, nv_ref):
        d_ref[...], nm_ref[...], nv_ref[...] = _adamw_math(w_ref[...], g_ref[...], m_ref[...], v_ref[...])

    return list(_call(body, name=name, out_shape=[_sds(w.shape, F32)] * 3)(g, w, m, v))


REPLICATED = ("ffn1_norm", "mix_norm", "b_in", "conv_cf_b", "ln_cf_g", "ln_cf_b", "ffn2_norm", "final_norm")
SHARDED = ("meta_tokens", "ffn1_w_gate", "ffn1_w_up", "ffn1_w_down", "w_in", "conv_sc_w", "conv_cf_w", "w_out",
           "ffn2_w_gate", "ffn2_w_up", "ffn2_w_down")
WEIGHTS = ("meta_tokens", "ffn1_norm", "ffn1_w_gate", "ffn1_w_up", "ffn1_w_down", "mix_norm", "w_in", "b_in",
           "conv_sc_w", "conv_cf_w", "conv_cf_b", "ln_cf_g", "ln_cf_b", "w_out", "ffn2_norm", "ffn2_w_gate",
           "ffn2_w_up", "ffn2_w_down", "final_norm")


TRANSPOSED = ("ffn1_w_gate", "ffn1_w_up", "ffn2_w_gate", "ffn2_w_up")


def _blocks2d(k, a):
    a = a.reshape(a.shape[-2:]) if a.ndim >= 2 else a.reshape(1, -1)
    return a.T if k in TRANSPOSED else a


def _step(x, tgt, w, m, v):
    n_seq, d = x.shape[1], x.shape[2]
    n_meta = w["meta_tokens"].shape[0]
    rows = -(-(n_seq + n_meta) // ROW_ALIGN) * ROW_ALIGN
    tm = rows // N_ROW_TILES
    cb = w["conv_sc_w"].shape[-1]
    dc = cb * N_DEV
    w2 = {k: _blocks2d(k, a) for k, a in w.items()}
    m2 = {k: _blocks2d(k, a) for k, a in m.items()}
    v2 = {k: _blocks2d(k, a) for k, a in v.items()}

    def as_given(k, r):
        return (r.T if k in TRANSPOSED else r).reshape(w[k].shape)

    def cast(k):
        return w2[k].astype(BF16)

    full = dict(zip(("ffn1_w_gate", "meta_tokens", "conv_sc_w", "conv_cf_w"), _exchange_alone(
        _gather_all([cast("ffn1_w_gate"), w2["meta_tokens"], w2["conv_sc_w"], w2["conv_cf_w"]]), "gather_gate1")))
    meta = jnp.transpose(full["meta_tokens"], (1, 0, 2)).reshape(n_meta, d)
    hs0 = jnp.concatenate([x[0], meta, jnp.zeros((rows - n_seq - n_meta, d), F32)], axis=0)

    tx = n_seq // N_ROW_TILES
    h1, r1 = _rms_fwd(hs0, w2["ffn1_norm"], tm, "rms_in")
    up1 = _gather_all([cast("ffn1_w_up")])
    g1 = _ffn_proj(h1, full["ffn1_w_gate"], tm, "ffn1_gate", comm=[up1])
    down1 = _gather_all([cast("ffn1_w_down")])
    u1 = _ffn_proj(h1, up1.results[0], tm, "ffn1_up", comm=[down1])
    w_in_all, w_out1 = _gather_all([cast("w_in")]), _gather_direct([cast("w_out")])
    hs1, h2, r2 = _ffn_down_norm(g1, u1, down1.results[0], hs0, w2["mix_norm"], rows // DOWN_ROW_TILES, "ffn1_down",
                                 comm=[w_in_all, w_out1])
    win = jnp.transpose(w_in_all.results[0], (1, 0, 2)).reshape(d, -1)
    w_out2, gate2 = _gather_forward(w_out1.results), _gather_direct([cast("ffn2_w_gate")])
    u = _win_fwd(h2, win, w2["b_in"], dc, tm, "mix_in", comm=[w_out2, gate2])
    wout = w_out2.results[0].reshape(-1, d)
    gate2p = _gather_forward(gate2.results)
    ysc, zc = _conv_fwd(u, full["conv_sc_w"], full["conv_cf_w"], w2["conv_cf_b"], n_seq, n_meta, "conv_fwd", comm=[gate2p])
    up2 = _gather_all([cast("ffn2_w_up")])
    y, hs2, h3, r3 = _wout_fwd(zc, ysc, wout, hs1, w2["ln_cf_g"], w2["ln_cf_b"], w2["ffn2_norm"], n_seq, tx, "mix_out",
                               comm=[up2])
    down2 = _gather_all([cast("ffn2_w_down")])
    full.update(ffn1_w_up=up1.results[0], ffn1_w_down=down1.results[0], ffn2_w_gate=gate2p.results[0],
                ffn2_w_up=up2.results[0])
    g2, u2 = _ffn_gu(h3, full["ffn2_w_gate"], full["ffn2_w_up"], tx, "ffn2_gu", comm=[down2])
    full["ffn2_w_down"] = down2.results[0]
    dhs3, dhs3b, loss_p, dgf_p = _ffn_down_loss(
        g2, u2, full["ffn2_w_down"], hs2, w2["final_norm"], tgt[0], n_seq, n_seq // DOWN_ROW_TILES, "ffn2_down_loss")

    dg2, du2, a2 = _ffn_bwd_da(dhs3b, full["ffn2_w_down"], g2, u2, tx, "ffn2_bwd_da")
    dhs2, dhs2b, dn3_p = _ffn_bwd_dh(
        dg2, du2, full["ffn2_w_gate"], full["ffn2_w_up"], hs2, r3, w2["ffn2_norm"], dhs3, tx, "ffn2_bwd_dh",
        zeros=jnp.zeros((rows, d), F32))
    xi, yi, ci = lax.axis_index("x"), lax.axis_index("y"), lax.axis_index("c")
    chip_of = [2 * xi + yi, 2 * (1 - xi) + yi, 2 * xi + (1 - yi), 2 * (1 - xi) + (1 - yi)]
    idx = jnp.stack([2 * ch + ci for ch in chip_of] + chip_of).astype(jnp.int32)
    out = {}

    def to_pairs(parts):
        return _pair_exchange([p.reshape((4, 2) + p.shape[1:]) for p in parts])

    def pair_sums(names, parts, pairs):
        return [_pair_sum(p, s, idx, "pair_sum_" + k) for k, p, s in zip(names, parts, pairs.results)]

    def update(names, sums, chips):
        for k, own, got in zip(names, sums, chips.results):
            res = _adamw_sharded(own, got, w2[k], m2[k], v2[k], "adamw_" + k)
            out[k] = [as_given(k, r) for r in res]

    ffn2 = ("ffn2_w_gate", "ffn2_w_up", "ffn2_w_down")
    tk = ROW_ALIGN
    grads2 = _ffn_bwd_dw(h3, dg2, du2, a2, dhs3b, tk, "ffn2_bwd_dw")
    pairs2 = to_pairs(grads2)
    dysc, dzc, dlg_p, dlb_p = _wout_bwd_dy(dhs2b, wout, zc, w2["ln_cf_g"], w2["ln_cf_b"], tx, "mix_out_bwd_dy",
                                           comm=[pairs2])
    sums2 = pair_sums(ffn2, grads2, pairs2)
    dwout = _wout_bwd_dw(y, dhs2b, N_DEV, tk, "mix_out_bwd_dw")
    chips2 = [_chip_exchange([s]) for s in sums2]
    du, dcsw, dccw, dcb = _conv_bwd(u, dysc, dzc, full["conv_sc_w"], full["conv_cf_w"], n_seq, n_meta, "conv_bwd",
                                    comm=[chips2[0]])
    dhs1, dhs1b, dn2_p = _win_bwd_dh(du, win, hs1, r2, w2["mix_norm"], dhs2, tm, "mix_in_bwd_dh", comm=[chips2[1]])
    dwin, dbin = _win_bwd_dw(h2, du, tk, "mix_in_bwd_dw", comm=[chips2[2]])
    for k, s, ch in zip(ffn2, sums2, chips2):
        update([k], [s], ch)

    mixer = ("w_out", "w_in", "conv_sc_w", "conv_cf_w")
    gradsm = [dwout, jnp.transpose(dwin.reshape(d, N_DEV, -1), (1, 0, 2)), dcsw, dccw]
    pairsm = to_pairs(gradsm)
    dg1, du1, a1 = _ffn_bwd_da(dhs1b, full["ffn1_w_down"], g1, u1, tm, "ffn1_bwd_da", comm=[pairsm])
    sumsm = pair_sums(mixer, gradsm, pairsm)
    chipsm = _chip_exchange(sumsm)
    ffn1 = ("ffn1_w_gate", "ffn1_w_up", "ffn1_w_down")
    grads1 = _ffn_bwd_dw(h1, dg1, du1, a1, dhs1b, tk, "ffn1_bwd_dw", comm=[chipsm])
    update(mixer, sumsm, chipsm)
    pairs1 = to_pairs(grads1)
    _exchange_alone(pairs1, "reduce_pair_ffn1")
    sums1 = pair_sums(ffn1, grads1, pairs1)
    chips1 = _chip_exchange(sums1)
    dhs0, _, dn1_p = _ffn_bwd_dh(
        dg1, du1, full["ffn1_w_gate"], full["ffn1_w_up"], hs0, r1, w2["ffn1_norm"], dhs1, tm, "ffn1_bwd_dh", comm=[chips1])
    update(ffn1, sums1, chips1)
    grad_x = dhs0[:n_seq][None]

    partial = {
        "ffn1_norm": dn1_p.sum(0), "mix_norm": dn2_p.sum(0), "b_in": dbin, "conv_cf_b": dcb,
        "ln_cf_g": dlg_p.sum(0), "ln_cf_b": dlb_p.sum(0), "ffn2_norm": dn3_p.sum(0), "final_norm": dgf_p.sum(0),
    }
    loss_seg = jnp.pad(loss_p.sum((0, 2)).reshape(1, 1), ((0, 0), (0, 127)))
    pieces = [partial[k] for k in REPLICATED] + [loss_seg, dhs0[n_seq:n_seq + n_meta].reshape(1, n_meta * d)]
    segs, off = [], 0
    for p in pieces:
        segs.append((off, p.shape[1]))
        off += p.shape[1]
    rows8 = _gather_rows(jnp.concatenate(pieces, axis=1), "gather_small")
    res = _adamw_replicated(rows8, segs, [w2[k] for k in REPLICATED], [m2[k] for k in REPLICATED],
                            [v2[k] for k in REPLICATED], 0.5 / d, "adamw_replicated")
    loss = res[0][0, 0]
    for p, k in enumerate(REPLICATED):
        out[k] = [r.reshape(w[k].shape) for r in res[1 + 4 * p:5 + 4 * p]]
    ob = w2["meta_tokens"].shape[1]
    gmeta = lax.dynamic_slice_in_dim(res[-1].reshape(n_meta, d), (4 * xi + 2 * yi + ci) * ob, ob, axis=1)
    out["meta_tokens"] = [gmeta] + _adamw_plain(gmeta, w2["meta_tokens"], m2["meta_tokens"], v2["meta_tokens"], "adamw_meta_tokens")

    return (loss, grad_x, *[out[k][0] for k in WEIGHTS], *[out[k][1] for k in WEIGHTS],
            *[out[k][2] for k in WEIGHTS], *[out[k][3] for k in WEIGHTS])


def kernel(x, meta_tokens, ffn1_norm, ffn1_w_gate, ffn1_w_up, ffn1_w_down, mix_norm, w_in, b_in, conv_sc_w, conv_cf_w, conv_cf_b, ln_cf_g, ln_cf_b, w_out, ffn2_norm, ffn2_w_gate, ffn2_w_up, ffn2_w_down, final_norm, loss_target, m_meta_tokens, m_ffn1_norm, m_ffn1_w_gate, m_ffn1_w_up, m_ffn1_w_down, m_mix_norm, m_w_in, m_b_in, m_conv_sc_w, m_conv_cf_w, m_conv_cf_b, m_ln_cf_g, m_ln_cf_b, m_w_out, m_ffn2_norm, m_ffn2_w_gate, m_ffn2_w_up, m_ffn2_w_down, m_final_norm, v_meta_tokens, v_ffn1_norm, v_ffn1_w_gate, v_ffn1_w_up, v_ffn1_w_down, v_mix_norm, v_w_in, v_b_in, v_conv_sc_w, v_conv_cf_w, v_conv_cf_b, v_ln_cf_g, v_ln_cf_b, v_w_out, v_ffn2_norm, v_ffn2_w_gate, v_ffn2_w_up, v_ffn2_w_down, v_final_norm):
    given = dict(locals())
    w = {k: given[k] for k in WEIGHTS}
    m = {k: given["m_" + k] for k in WEIGHTS}
    v = {k: given["v_" + k] for k in WEIGHTS}
    return _step(x, loss_target, w, m, v)
```

```python
import functools

import jax
import jax.numpy as jnp
from jax import lax
from jax.experimental import pallas as pl
from jax.experimental.pallas import tpu as pltpu

F32 = jnp.float32
BF16 = jnp.bfloat16
EPS = 1e-6
FFN_RES_SCALE = 0.5
SC_WIDTH = 3
CF_WIDTH = 31
ADAM_LR = 0.001
ADAM_B1 = 0.9
ADAM_B2 = 0.999
ADAM_EPS = 1e-08
ADAM_WD = 0.01
ADAM_STEP = 10

N_DEV = 8
N_ROW_TILES = 8
ROW_ALIGN = 256
CONV_PAD = 32
CONV_CH = 128
SUB_ROWS = 32
DOWN_ROW_TILES = 16
SUBLANES = 8
BF16_ROWS = 16
VMEM_LIMIT = 56 * 1024 * 1024
MESH = pl.DeviceIdType.MESH
ANY = pl.BlockSpec(memory_space=pl.ANY)

NT = (((1,), (1,)), ((), ()))
TN = (((0,), (0,)), ((), ()))


def _pallas(body, **kw):
    return pl.pallas_call(body, **kw)


class _Exchange:
    def __init__(self, inputs, out_shapes, sem_shapes, start, finish, aliases=None):
        self.inputs, self.out_shapes, self.sem_shapes = list(inputs), list(out_shapes), list(sem_shapes)
        self.start, self.finish, self.aliases = start, finish, dict(aliases or {})
        self.results = None


def _call(body, comm=(), **kw):
    if not comm:
        return _pallas(body, **kw)
    grid = kw.pop("grid")
    in_specs = list(kw.pop("in_specs"))
    out_specs, out_shape = kw.pop("out_specs"), kw.pop("out_shape")
    scratch = list(kw.pop("scratch_shapes", []))
    single = not isinstance(out_shape, (list, tuple))
    out_specs, out_shape = ([out_specs], [out_shape]) if single else (list(out_specs), list(out_shape))
    n_in, n_out, n_scr = len(in_specs), len(out_shape), len(scratch)
    c_in = [a for job in comm for a in job.inputs]
    c_out = [s for job in comm for s in job.out_shapes]
    c_sem = [s for job in comm for s in job.sem_shapes]
    aliases, i0, o0 = dict(kw.pop("input_output_aliases", {})), n_in, n_out
    for job in comm:
        aliases.update({i0 + i: o0 + o for i, o in job.aliases.items()})
        i0, o0 = i0 + len(job.inputs), o0 + len(job.out_shapes)

    def hosted(*refs):
        pos = [0]

        def take(n):
            pos[0] += n
            return refs[pos[0] - n:pos[0]]

        ins, cins, outs, couts, scr, sems = take(n_in), take(len(c_in)), take(n_out), take(len(c_out)), take(n_scr), take(len(c_sem))
        ids = [pl.program_id(k) for k in range(len(grid))]
        first = functools.reduce(jnp.logical_and, [i == 0 for i in ids])
        last = functools.reduce(jnp.logical_and, [i == g - 1 for i, g in zip(ids, grid)])

        def each(phase):
            i, o, s = 0, 0, 0
            for job in comm:
                ni, no, ns = len(job.inputs), len(job.out_shapes), len(job.sem_shapes)
                getattr(job, phase)(cins[i:i + ni], couts[o:o + no], sems[s:s + ns])
                i, o, s = i + ni, o + no, s + ns

        @pl.when(first)
        def _():
            each("start")

        body(*ins, *outs, *scr)

        @pl.when(last)
        def _():
            each("finish")

    call = _pallas(
        hosted, grid=grid, in_specs=in_specs + [ANY] * len(c_in), out_specs=out_specs + [ANY] * len(c_out),
        out_shape=out_shape + c_out, scratch_shapes=scratch + c_sem, input_output_aliases=aliases, **kw)

    def run(*args):
        res = call(*args, *c_in)
        o = n_out
        for job in comm:
            job.results = list(res[o:o + len(job.out_shapes)])
            o += len(job.out_shapes)
        return res[0] if single else list(res[:n_out])

    return run


def _exchange_alone(job, name, comm=()):
    n_in, n_out = len(job.inputs), len(job.out_shapes)

    def body(*refs):
        ins, outs, sems = refs[:n_in], refs[n_in:n_in + n_out], refs[n_in + n_out:]
        job.start(ins, outs, sems)
        job.finish(ins, outs, sems)

    res = _pallas(body, name=name, in_specs=[ANY] * n_in, out_specs=[ANY] * n_out, out_shape=job.out_shapes,
                  scratch_shapes=job.sem_shapes, input_output_aliases=job.aliases)(*job.inputs)
    job.results = list(res)
    return job.results


def _params(**kw):
    return pltpu.CompilerParams(vmem_limit_bytes=VMEM_LIMIT, **kw)


def _sigmoid(x):
    return 0.5 * jnp.tanh(0.5 * x) + 0.5


def _sds(shape, dtype):
    return jax.ShapeDtypeStruct(shape, dtype)


def _place():
    x, y, c = lax.axis_index("x"), lax.axis_index("y"), lax.axis_index("c")
    chips = [(1 - x, y), (x, 1 - y), (1 - x, 1 - y)]
    return x, y, c, chips


def _slot(ref, p):
    return ref.at[4 * p[0] + 2 * p[1] + p[2]]


def _remote(src, dst, send_sem, recv_sem, to):
    return pltpu.make_async_remote_copy(src_ref=src, dst_ref=dst, send_sem=send_sem, recv_sem=recv_sem,
                                        device_id=to, device_id_type=MESH)


def _gather_direct(arrs):
    n = len(arrs)

    def copies(ins, outs, sems):
        send_sems, recv_sems, local_sems = sems
        x, y, c, chips = _place()
        me = (x, y, c)
        peers = [(x, y, 1 - c)] + [(*chip, c) for chip in chips]
        local = [pltpu.make_async_copy(ins[a], _slot(outs[a], me), local_sems.at[a]) for a in range(n)]
        sends = [_remote(ins[a], _slot(outs[a], me), send_sems.at[a, k], recv_sems.at[a, k], peer)
                 for a in range(n) for k, peer in enumerate(peers)]
        arrivals = [_remote(ins[a], _slot(outs[a], peer), send_sems.at[a, k], recv_sems.at[a, k], peer)
                    for a in range(n) for k, peer in enumerate(peers)]
        return local, sends, arrivals

    def start(ins, outs, sems):
        local, sends, _ = copies(ins, outs, sems)
        for cp in local + sends:
            cp.start()

    def finish(ins, outs, sems):
        local, sends, arrivals = copies(ins, outs, sems)
        for cp in arrivals:
            cp.wait_recv()
        for cp in sends:
            cp.wait_send()
        for cp in local:
            cp.wait()

    dma = pltpu.SemaphoreType.DMA
    return _Exchange(arrs, [_sds((N_DEV,) + a.shape, a.dtype) for a in arrs], [dma((n, 4)), dma((n, 4)), dma((n,))],
                     start, finish)


def _gather_forward(gathered):
    n = len(gathered)

    def copies(ins, outs, sems):
        send_sems, recv_sems = sems
        x, y, c, chips = _place()
        sibling = (x, y, 1 - c)
        sends = [_remote(_slot(ins[a], (*chip, c)), _slot(outs[a], (*chip, c)), send_sems.at[a, j], recv_sems.at[a, j], sibling)
                 for a in range(n) for j, chip in enumerate(chips)]
        arrivals = [_remote(_slot(ins[a], (*chip, c)), _slot(outs[a], (*chip, 1 - c)), send_sems.at[a, j], recv_sems.at[a, j], sibling)
                    for a in range(n) for j, chip in enumerate(chips)]
        return sends, arrivals

    def start(ins, outs, sems):
        for cp in copies(ins, outs, sems)[0]:
            cp.start()

    def finish(ins, outs, sems):
        sends, arrivals = copies(ins, outs, sems)
        for cp in arrivals:
            cp.wait_recv()
        for cp in sends:
            cp.wait_send()

    dma = pltpu.SemaphoreType.DMA
    return _Exchange(gathered, [_sds(a.shape, a.dtype) for a in gathered], [dma((n, 3)), dma((n, 3))], start, finish,
                     aliases={a: a for a in range(n)})


def _gather_all(arrs):
    direct = _gather_direct(arrs)
    n = len(arrs)
    n_sems = len(direct.sem_shapes)
    dma = pltpu.SemaphoreType.DMA

    def start(ins, outs, sems):
        direct.start(ins, outs, sems[:n_sems])

    def finish(ins, outs, sems):
        direct.finish(ins, outs, sems[:n_sems])
        passed = _gather_forward(outs)
        passed.start(outs, outs, sems[n_sems:])
        passed.finish(outs, outs, sems[n_sems:])

    return _Exchange(arrs, direct.out_shapes, direct.sem_shapes + [dma((n, 3)), dma((n, 3))], start, finish)


def _pair_exchange(arrs):
    n = len(arrs)

    def copies(ins, outs, sems):
        x, y, c, _ = _place()
        return [_remote(ins[a].at[:, 1 - c], outs[a], sems[0].at[a], sems[1].at[a], (x, y, 1 - c)) for a in range(n)]

    def start(ins, outs, sems):
        for cp in copies(ins, outs, sems):
            cp.start()

    def finish(ins, outs, sems):
        for cp in copies(ins, outs, sems):
            cp.wait()

    dma = pltpu.SemaphoreType.DMA
    return _Exchange(arrs, [_sds((4,) + a.shape[2:], a.dtype) for a in arrs], [dma((n,)), dma((n,))], start, finish)


def _chip_exchange(arrs):
    n = len(arrs)

    def copies(ins, outs, sems):
        x, y, c, chips = _place()
        return [_remote(ins[a].at[1 + j], outs[a].at[j], sems[0].at[a, j], sems[1].at[a, j], (*chip, c))
                for a in range(n) for j, chip in enumerate(chips)]

    def start(ins, outs, sems):
        for cp in copies(ins, outs, sems):
            cp.start()

    def finish(ins, outs, sems):
        for cp in copies(ins, outs, sems):
            cp.wait()

    dma = pltpu.SemaphoreType.DMA
    return _Exchange(arrs, [_sds((3,) + a.shape[1:], a.dtype) for a in arrs], [dma((n, 3)), dma((n, 3))], start, finish)


def _gather_rows(vec, name, comm=()):
    def body(in_ref, out_ref, send_sems, recv_sems, local_sem):
        x, y, c, _ = _place()
        me = 4 * x + 2 * y + c
        mine = pltpu.make_async_copy(in_ref, out_ref.at[me], local_sem)
        mine.start()
        copies = []
        for k in range(1, N_DEV):
            to = (1 - x if k & 4 else x, 1 - y if k & 2 else y, 1 - c if k & 1 else c)
            copies.append(pltpu.make_async_remote_copy(
                src_ref=in_ref, dst_ref=out_ref.at[me], send_sem=send_sems.at[k - 1], recv_sem=recv_sems.at[k - 1],
                device_id=to, device_id_type=MESH))
        for cp in copies:
            cp.start()
        for cp in copies:
            cp.wait()
        mine.wait()

    return _call(
        body, name=name, out_shape=_sds((N_DEV,) + vec.shape, vec.dtype), in_specs=[ANY], out_specs=ANY,
        scratch_shapes=[pltpu.SemaphoreType.DMA((7,)), pltpu.SemaphoreType.DMA((7,)), pltpu.SemaphoreType.DMA],
    )(vec)


def _group_rows(tm):
    return SUB_ROWS if tm % SUB_ROWS == 0 else BF16_ROWS


def _for_row_groups(tm, fn):
    sub = _group_rows(tm)

    def step(i, carry):
        fn(pl.ds(pl.multiple_of(i * sub, sub), sub), i * sub)
        return carry

    lax.fori_loop(0, tm // sub, step, 0)


def _once(shape, index_map):
    return pl.BlockSpec(shape, index_map, pipeline_mode=pl.Buffered(1))


def _rms_fwd(hs, g, tm, name, comm=()):
    rows, d = hs.shape

    def body(hs_ref, g_ref, h_ref, r_ref):
        def group(rs, _):
            xv = hs_ref[rs, :]
            r = lax.rsqrt(jnp.mean(xv * xv, axis=-1, keepdims=True) + EPS)
            h_ref[rs, :] = (xv * r * g_ref[...]).astype(BF16)
            r_ref[rs, :] = r

        _for_row_groups(tm, group)

    return _call(
        body, comm=comm, name=name, grid=(rows // tm,),
        in_specs=[pl.BlockSpec((tm, d), lambda i: (i, 0)), pl.BlockSpec((1, d), lambda i: (0, 0))],
        out_specs=[pl.BlockSpec((tm, d), lambda i: (i, 0)), pl.BlockSpec((tm, 1), lambda i: (i, 0))],
        out_shape=[_sds((rows, d), BF16), _sds((rows, 1), F32)], compiler_params=_params(),
    )(hs, g)


def _ffn_gu(h, wg, wu, tm, name, comm=()):
    rows, d = h.shape
    nb, fb, _ = wg.shape

    def body(h_ref, wg_ref, wu_ref, g_ref, u_ref):
        hv = h_ref[...]
        g_ref[0] = lax.dot_general(hv, wg_ref[0], NT, preferred_element_type=F32).astype(BF16)
        u_ref[0] = lax.dot_general(hv, wu_ref[0], NT, preferred_element_type=F32).astype(BF16)

    wspec = pl.BlockSpec((1, fb, d), lambda j, i: (j, 0, 0))
    ospec = pl.BlockSpec((1, tm, fb), lambda j, i: (j, i, 0))
    return _call(
        body, comm=comm, name=name, grid=(nb, rows // tm),
        in_specs=[pl.BlockSpec((tm, d), lambda j, i: (i, 0)), wspec, wspec],
        out_specs=[ospec, ospec], out_shape=[_sds((nb, rows, fb), BF16)] * 2, compiler_params=_params(),
    )(h, wg, wu)


def _ffn_proj(h, w, tm, name, comm=()):
    rows, d = h.shape
    nb, fb, _ = w.shape

    def body(h_ref, w_ref, p_ref):
        p_ref[0] = lax.dot_general(h_ref[...], w_ref[0], NT, preferred_element_type=F32).astype(BF16)

    return _call(
        body, comm=comm, name=name, grid=(nb, rows // tm),
        in_specs=[pl.BlockSpec((tm, d), lambda j, i: (i, 0)), pl.BlockSpec((1, fb, d), lambda j, i: (j, 0, 0))],
        out_specs=pl.BlockSpec((1, tm, fb), lambda j, i: (j, i, 0)), out_shape=_sds((nb, rows, fb), BF16),
        compiler_params=_params(),
    )(h, w)


def _down_product(g_ref, u_ref, wd_ref, acc_ref):
    part = None
    for b in range(g_ref.shape[0]):
        gv = g_ref[b]
        a = gv * _sigmoid(gv) * u_ref[b]
        p = jnp.dot(a, wd_ref[b], preferred_element_type=F32)
        part = p if part is None else part + p
    acc_ref[...] = part


def _ffn_down_norm(g, u, wd, hs, gn, tm, name, comm=()):
    nb, rows, fb = g.shape
    d = hs.shape[1]

    def body(g_ref, u_ref, wd_ref, hs_ref, gn_ref, hsn_ref, hn_ref, rn_ref, acc_ref):
        _down_product(g_ref, u_ref, wd_ref, acc_ref)

        def group(rs, _):
            hsn = hs_ref[rs, :] + FFN_RES_SCALE * acc_ref[rs, :]
            r = lax.rsqrt(jnp.mean(hsn * hsn, axis=-1, keepdims=True) + EPS)
            hsn_ref[rs, :] = hsn
            hn_ref[rs, :] = (hsn * r * gn_ref[...]).astype(BF16)
            rn_ref[rs, :] = r

        _for_row_groups(tm, group)

    aspec = pl.BlockSpec((nb, tm, fb), lambda i: (0, i, 0))
    row = pl.BlockSpec((tm, d), lambda i: (i, 0))
    return _call(
        body, comm=comm, name=name, grid=(rows // tm,),
        in_specs=[aspec, aspec, _once((nb, fb, d), lambda i: (0, 0, 0)), row, pl.BlockSpec((1, d), lambda i: (0, 0))],
        out_specs=[row, row, pl.BlockSpec((tm, 1), lambda i: (i, 0))],
        out_shape=[_sds((rows, d), F32), _sds((rows, d), BF16), _sds((rows, 1), F32)],
        scratch_shapes=[pltpu.VMEM((tm, d), F32)], compiler_params=_params(),
    )(g, u, wd, hs, gn)


def _ffn_down_loss(g, u, wd, hs, gf, tgt, n_seq, tm, name, comm=()):
    nb, rows, fb = g.shape
    d = hs.shape[1]
    nt = rows // tm

    def body(g_ref, u_ref, wd_ref, hs_ref, gf_ref, tgt_ref, dhs_ref, dhsb_ref, loss_ref, dgf_ref, acc_ref):
        i = pl.program_id(0)
        _down_product(g_ref, u_ref, wd_ref, acc_ref)
        loss_ref[0] = jnp.zeros((1, d), F32)
        dgf_ref[0] = jnp.zeros((1, d), F32)

        def group(rs, r0):
            hs3 = hs_ref[rs, :] + FFN_RES_SCALE * acc_ref[rs, :]
            r = lax.rsqrt(jnp.mean(hs3 * hs3, axis=-1, keepdims=True) + EPS)
            gfv = gf_ref[...]
            y = hs3 * r
            rowid = i * tm + r0 + lax.broadcasted_iota(jnp.int32, (_group_rows(tm), 1), 0)
            err = jnp.where(rowid < n_seq, y * gfv - tgt_ref[rs, :], 0.0)
            loss_ref[0] += jnp.sum(err * err, axis=0, keepdims=True)
            dout = err * (1.0 / d)
            dgf_ref[0] += jnp.sum(dout * y, axis=0, keepdims=True)
            t = dout * gfv
            dhs = r * t - hs3 * (r * r * r) * jnp.mean(t * hs3, axis=-1, keepdims=True)
            dhs_ref[rs, :] = dhs
            dhsb_ref[rs, :] = dhs.astype(BF16)

        _for_row_groups(tm, group)

    aspec = pl.BlockSpec((nb, tm, fb), lambda i: (0, i, 0))
    row = pl.BlockSpec((tm, d), lambda i: (i, 0))
    part = pl.BlockSpec((1, 1, d), lambda i: (i, 0, 0))
    return _call(
        body, comm=comm, name=name, grid=(nt,),
        in_specs=[aspec, aspec, _once((nb, fb, d), lambda i: (0, 0, 0)), row, pl.BlockSpec((1, d), lambda i: (0, 0)), row],
        out_specs=[row, row, part, part],
        out_shape=[_sds((rows, d), F32), _sds((rows, d), BF16), _sds((nt, 1, d), F32), _sds((nt, 1, d), F32)],
        scratch_shapes=[pltpu.VMEM((tm, d), F32)], compiler_params=_params(),
    )(g, u, wd, hs, gf, tgt)


def _ffn_bwd_da(do, wd, g, u, tm, name, comm=()):
    nb, rows, fb = g.shape
    d = do.shape[1]

    def body(do_ref, wd_ref, g_ref, u_ref, dg_ref, du_ref):
        rs = pl.ds(pl.multiple_of(pl.program_id(1) * tm, tm), tm)
        da = (FFN_RES_SCALE * lax.dot_general(do_ref[rs, :], wd_ref[0], NT, preferred_element_type=F32)).astype(BF16)
        gv, uv = g_ref[0], u_ref[0]
        s = _sigmoid(gv)
        sg = gv * s
        du_ref[0] = da * sg
        dg_ref[0] = da * uv * (s + sg * (1.0 - s))

    aspec = pl.BlockSpec((1, tm, fb), lambda j, i: (j, i, 0))
    return _call(
        body, comm=comm, name=name, grid=(nb, rows // tm),
        in_specs=[_once((rows, d), lambda j, i: (0, 0)), pl.BlockSpec((1, fb, d), lambda j, i: (j, 0, 0)), aspec, aspec],
        out_specs=[aspec] * 2, out_shape=[_sds((nb, rows, fb), BF16)] * 2, compiler_params=_params(),
    )(do, wd, g, u)


def _rms_bwd(tm, d, dh_ref, hs_ref, r_ref, gn_ref, dres_ref, dhs_ref, dhsb_ref, dgn_ref):
    dgn_ref[0] = jnp.zeros((1, d), F32)

    def group(rs, _):
        dh, hs, r = dh_ref[rs, :], hs_ref[rs, :], r_ref[rs, :]
        dgn_ref[0] += jnp.sum(dh * (hs * r), axis=0, keepdims=True)
        t = dh * gn_ref[...]
        dhs = dres_ref[rs, :] + r * t - hs * (r * r * r) * jnp.mean(t * hs, axis=-1, keepdims=True)
        dhs_ref[rs, :] = dhs
        dhsb_ref[rs, :] = dhs.astype(BF16)

    _for_row_groups(tm, group)


def _blocks_dot(x_ref, w_ref):
    part = None
    for b in range(x_ref.shape[0]):
        p = jnp.dot(x_ref[b], w_ref[b], preferred_element_type=F32)
        part = p if part is None else part + p
    return part


def _ffn_bwd_dh_gate(dg, wg, tm, name, comm=()):
    nb, rows, fb = dg.shape
    d = wg.shape[2]

    def body(dg_ref, wg_ref, o_ref):
        o_ref[...] = _blocks_dot(dg_ref, wg_ref)

    return _call(
        body, comm=comm, name=name, grid=(rows // tm,),
        in_specs=[pl.BlockSpec((nb, tm, fb), lambda i: (0, i, 0)), _once((nb, fb, d), lambda i: (0, 0, 0))],
        out_specs=pl.BlockSpec((tm, d), lambda i: (i, 0)), out_shape=_sds((rows, d), F32), compiler_params=_params(),
    )(dg, wg)


def _ffn_bwd_dh(du, wu, half, hs, r, gn, dres, tm, name, comm=(), zeros=None):
    nb, rows, fb = du.shape
    d = hs.shape[1]
    nt = rows // tm
    extra = [] if zeros is None else [zeros]

    def body(du_ref, wu_ref, half_ref, hs_ref, r_ref, gn_ref, dres_ref, *rest):
        dhs_ref, dhsb_ref, dgn_ref, acc_ref = rest[len(extra):]
        acc_ref[...] = half_ref[...] + _blocks_dot(du_ref, wu_ref)
        _rms_bwd(tm, d, acc_ref, hs_ref, r_ref, gn_ref, dres_ref, dhs_ref, dhsb_ref, dgn_ref)

    row = pl.BlockSpec((tm, d), lambda i: (i, 0))
    return _call(
        body, comm=comm, name=name, grid=(nt,),
        in_specs=[pl.BlockSpec((nb, tm, fb), lambda i: (0, i, 0)), _once((nb, fb, d), lambda i: (0, 0, 0)), row, row,
                  pl.BlockSpec((tm, 1), lambda i: (i, 0)), pl.BlockSpec((1, d), lambda i: (0, 0)), row] + [ANY] * len(extra),
        out_specs=[row, row, pl.BlockSpec((1, 1, d), lambda i: (i, 0, 0))],
        out_shape=[_sds((rows, d) if zeros is None else zeros.shape, F32), _sds((rows, d), BF16), _sds((nt, 1, d), F32)],
        scratch_shapes=[pltpu.VMEM((tm, d), F32)], compiler_params=_params(),
        input_output_aliases={7: 0} if extra else {},
    )(du, wu, half, hs, r, gn, dres, *extra)


def _ffn_bwd_dw_gu(h, dg, du, tk, name, comm=()):
    nb, rows, fb = dg.shape
    d = h.shape[1]
    nk = rows // tk

    def body(h_ref, dg_ref, du_ref, dwg_ref, dwu_ref, accg, accu):
        k = pl.program_id(1)

        @pl.when(k == 0)
        def _():
            accg[...] = jnp.zeros_like(accg)
            accu[...] = jnp.zeros_like(accu)

        hv = h_ref[...]
        accg[...] += lax.dot_general(dg_ref[0], hv, TN, preferred_element_type=F32)
        accu[...] += lax.dot_general(du_ref[0], hv, TN, preferred_element_type=F32)

        @pl.when(k == nk - 1)
        def _():
            dwg_ref[0] = accg[...].astype(BF16)
            dwu_ref[0] = accu[...].astype(BF16)

    aspec = pl.BlockSpec((1, tk, fb), lambda j, k: (j, k, 0))
    wspec = pl.BlockSpec((1, fb, d), lambda j, k: (j, 0, 0))
    return _call(
        body, comm=comm, name=name, grid=(nb, nk),
        in_specs=[pl.BlockSpec((tk, d), lambda j, k: (k, 0)), aspec, aspec], out_specs=[wspec] * 2,
        out_shape=[_sds((nb, fb, d), BF16)] * 2, scratch_shapes=[pltpu.VMEM((fb, d), F32)] * 2,
        compiler_params=_params(),
    )(h, dg, du)


def _ffn_bwd_dw_down(g, u, do, tk, name, comm=()):
    nb, rows, fb = g.shape
    d = do.shape[1]
    nk = rows // tk

    def body(g_ref, u_ref, do_ref, dwd_ref, acc):
        k = pl.program_id(1)

        @pl.when(k == 0)
        def _():
            acc[...] = jnp.zeros_like(acc)

        gv = g_ref[0]
        a = gv * _sigmoid(gv) * u_ref[0]
        acc[...] += lax.dot_general(a, do_ref[...], TN, preferred_element_type=F32)

        @pl.when(k == nk - 1)
        def _():
            dwd_ref[0] = (FFN_RES_SCALE * acc[...]).astype(BF16)

    aspec = pl.BlockSpec((1, tk, fb), lambda j, k: (j, k, 0))
    return _call(
        body, comm=comm, name=name, grid=(nb, nk),
        in_specs=[aspec, aspec, pl.BlockSpec((tk, d), lambda j, k: (k, 0))],
        out_specs=pl.BlockSpec((1, fb, d), lambda j, k: (j, 0, 0)), out_shape=_sds((nb, fb, d), BF16),
        scratch_shapes=[pltpu.VMEM((fb, d), F32)], compiler_params=_params(),
    )(g, u, do)


def _win_fwd(h, w, b, dc, tm, name, comm=()):
    rows, d = h.shape
    ng = w.shape[1] // dc

    def body(h_ref, w_ref, b_ref, u_ref):
        u_ref[...] = (jnp.dot(h_ref[...], w_ref[...], preferred_element_type=F32) + b_ref[...]).astype(BF16)

    return _call(
        body, comm=comm, name=name, grid=(ng, rows // tm),
        in_specs=[pl.BlockSpec((tm, d), lambda m, i: (i, 0)), pl.BlockSpec((d, dc), lambda m, i: (0, m)),
                  pl.BlockSpec((1, dc), lambda m, i: (0, m))],
        out_specs=pl.BlockSpec((tm, dc), lambda m, i: (i, m)), out_shape=_sds((rows, ng * dc), BF16),
        compiler_params=_params(),
    )(h, w, b)


def _win_bwd_dh(du, w, hs, r, gn, dres, tm, name, comm=()):
    rows, d = hs.shape
    ng, _, dc = du.shape
    nt = rows // tm

    def body(du_ref, w_ref, hs_ref, r_ref, gn_ref, dres_ref, dhs_ref, dhsb_ref, dgn_ref, acc_ref):
        m = pl.program_id(1)

        @pl.when(m == 0)
        def _():
            acc_ref[...] = jnp.zeros_like(acc_ref)

        acc_ref[...] += lax.dot_general(du_ref[0], w_ref[...], NT, preferred_element_type=F32)

        @pl.when(m == ng - 1)
        def _():
            _rms_bwd(tm, d, acc_ref, hs_ref, r_ref, gn_ref, dres_ref, dhs_ref, dhsb_ref, dgn_ref)

    row = pl.BlockSpec((tm, d), lambda i, m: (i, 0))
    once = _once((tm, d), lambda i, m: (i, 0))
    return _call(
        body, comm=comm, name=name, grid=(nt, ng),
        in_specs=[pl.BlockSpec((1, tm, dc), lambda i, m: (m, i, 0)), pl.BlockSpec((d, dc), lambda i, m: (0, m)), once,
                  pl.BlockSpec((tm, 1), lambda i, m: (i, 0)), pl.BlockSpec((1, d), lambda i, m: (0, 0)), once],
        out_specs=[row, row, pl.BlockSpec((1, 1, d), lambda i, m: (i, 0, 0))],
        out_shape=[_sds((rows, d), F32), _sds((rows, d), BF16), _sds((nt, 1, d), F32)],
        scratch_shapes=[pltpu.VMEM((tm, d), F32)], compiler_params=_params(),
    )(du, w, hs, r, gn, dres)


def _win_bwd_dw(h, du, tk, name, comm=()):
    rows, d = h.shape
    ng, _, dc = du.shape
    nk = rows // tk

    def body(h_ref, du_ref, dw_ref, db_ref, acc, accb):
        k = pl.program_id(1)

        @pl.when(k == 0)
        def _():
            acc[...] = jnp.zeros_like(acc)
            accb[...] = jnp.zeros_like(accb)

        duv = du_ref[0]
        acc[...] += lax.dot_general(h_ref[...], duv, TN, preferred_element_type=F32)
        accb[...] += jnp.sum(duv.astype(F32), axis=0, keepdims=True)

        @pl.when(k == nk - 1)
        def _():
            dw_ref[...] = acc[...].astype(BF16)
            db_ref[...] = accb[...]

    return _call(
        body, comm=comm, name=name, grid=(ng, nk),
        in_specs=[pl.BlockSpec((tk, d), lambda m, k: (k, 0)), pl.BlockSpec((1, tk, dc), lambda m, k: (m, k, 0))],
        out_specs=[pl.BlockSpec((d, dc), lambda m, k: (0, m)), pl.BlockSpec((1, dc), lambda m, k: (0, m))],
        out_shape=[_sds((d, ng * dc), BF16), _sds((1, ng * dc), F32)],
        scratch_shapes=[pltpu.VMEM((d, dc), F32), pltpu.VMEM((1, dc), F32)], compiler_params=_params(),
    )(h, du)


def _layernorm_silu(zc, lg, lb):
    mu = jnp.mean(zc, axis=-1, keepdims=True)
    xc = zc - mu
    rstd = lax.rsqrt(jnp.mean(xc * xc, axis=-1, keepdims=True) + EPS)
    nrm = xc * rstd
    lin = nrm * lg + lb
    s = _sigmoid(lin)
    return nrm, rstd, lin, s


def _wout_fwd(zc, ysc, wout, hs, lg, lb, gn, rows, tm, name, comm=()):
    dc = zc.shape[1]
    d = hs.shape[1]

    def body(zc_ref, ysc_ref, w_ref, hs_ref, lg_ref, lb_ref, gn_ref, y_ref, hsn_ref, hn_ref, rn_ref):
        def mix(rs, _):
            _, _, lin, s = _layernorm_silu(zc_ref[rs, :], lg_ref[...], lb_ref[...])
            y_ref[rs, :dc] = ysc_ref[rs, :]
            y_ref[rs, dc:] = (lin * s).astype(BF16)

        _for_row_groups(tm, mix)
        hsn_ref[...] = jnp.dot(y_ref[...], w_ref[...], preferred_element_type=F32)

        def norm(rs, _):
            hsn = hs_ref[rs, :] + hsn_ref[rs, :]
            r = lax.rsqrt(jnp.mean(hsn * hsn, axis=-1, keepdims=True) + EPS)
            hsn_ref[rs, :] = hsn
            hn_ref[rs, :] = (hsn * r * gn_ref[...]).astype(BF16)
            rn_ref[rs, :] = r

        _for_row_groups(tm, norm)

    half = pl.BlockSpec((tm, dc), lambda i: (i, 0))
    row = pl.BlockSpec((tm, d), lambda i: (i, 0))
    vec_c = pl.BlockSpec((1, dc), lambda i: (0, 0))
    return _call(
        body, comm=comm, name=name, grid=(rows // tm,),
        in_specs=[half, half, _once((2 * dc, d), lambda i: (0, 0)), row, vec_c, vec_c,
                  pl.BlockSpec((1, d), lambda i: (0, 0))],
        out_specs=[pl.BlockSpec((tm, 2 * dc), lambda i: (i, 0)), row, row, pl.BlockSpec((tm, 1), lambda i: (i, 0))],
        out_shape=[_sds((rows, 2 * dc), BF16), _sds((rows, d), F32), _sds((rows, d), BF16), _sds((rows, 1), F32)],
        compiler_params=_params(),
    )(zc, ysc, wout, hs, lg, lb, gn)


def _wout_bwd_dy(do, wout, zc, lg, lb, tm, name, comm=()):
    rows, d = do.shape
    dc = zc.shape[1]
    nt = rows // tm

    def body(do_ref, w_ref, zc_ref, lg_ref, lb_ref, dysc_ref, dzc_ref, dlg_ref, dlb_ref, dy_ref):
        dy_ref[...] = lax.dot_general(do_ref[...], w_ref[...], NT, preferred_element_type=F32)
        dlb_ref[0] = jnp.zeros((1, dc), F32)
        dlg_ref[0] = jnp.zeros((1, dc), F32)

        def group(rs, _):
            dysc_ref[rs, :] = dy_ref[rs, :dc].astype(BF16)
            nrm, rstd, lin, s = _layernorm_silu(zc_ref[rs, :], lg_ref[...], lb_ref[...])
            dl = dy_ref[rs, dc:] * (s * (1.0 + lin * (1.0 - s)))
            dlb_ref[0] += jnp.sum(dl, axis=0, keepdims=True)
            dlg_ref[0] += jnp.sum(dl * nrm, axis=0, keepdims=True)
            dn = dl * lg_ref[...]
            dzc_ref[rs, :] = rstd * (dn - jnp.mean(dn, axis=-1, keepdims=True)
                                     - nrm * jnp.mean(dn * nrm, axis=-1, keepdims=True))

        _for_row_groups(tm, group)

    half = pl.BlockSpec((tm, dc), lambda i: (i, 0))
    vec_c = pl.BlockSpec((1, dc), lambda i: (0, 0))
    part = pl.BlockSpec((1, 1, dc), lambda i: (i, 0, 0))
    return _call(
        body, comm=comm, name=name, grid=(nt,),
        in_specs=[pl.BlockSpec((tm, d), lambda i: (i, 0)), _once((2 * dc, d), lambda i: (0, 0)), half, vec_c, vec_c],
        out_specs=[half, half, part, part],
        out_shape=[_sds((rows, dc), BF16), _sds((rows, dc), F32), _sds((nt, 1, dc), F32), _sds((nt, 1, dc), F32)],
        scratch_shapes=[pltpu.VMEM((tm, 2 * dc), F32)], compiler_params=_params(),
    )(do, wout, zc, lg, lb)


def _wout_bwd_dw(y, do, nb, tk, name, comm=()):
    rows, k2 = y.shape
    d = do.shape[1]
    nk = rows // tk

    def body(y_ref, do_ref, dw_ref, acc):
        k = pl.program_id(0)

        @pl.when(k == 0)
        def _():
            acc[...] = jnp.zeros_like(acc)

        acc[...] += lax.dot_general(y_ref[...], do_ref[...], TN, preferred_element_type=F32)

        @pl.when(k == nk - 1)
        def _():
            dw_ref[...] = acc[...].astype(BF16)

    return _call(
        body, comm=comm, name=name, grid=(nk,),
        in_specs=[pl.BlockSpec((tk, k2), lambda k: (k, 0)), pl.BlockSpec((tk, d), lambda k: (k, 0))],
        out_specs=_once((k2, d), lambda k: (0, 0)), out_shape=_sds((k2, d), BF16),
        scratch_shapes=[pltpu.VMEM((k2, d), F32)], compiler_params=_params(),
    )(y, do).reshape(nb, k2 // nb, d)


def _windows(win, n_res):
    length = win.shape[0]
    return [win if r == 0 else pltpu.roll(win, length - r, 0) for r in range(n_res)]


def _taps(src_ref, start, offsets, ch):
    span = -(-(max(offsets) + ch) // SUBLANES) * SUBLANES
    win = src_ref[pl.ds(start, span), :]
    shifted = _windows(win, min(SUBLANES, max(offsets) + 1))
    return [shifted[o % SUBLANES][(o // SUBLANES) * SUBLANES:(o // SUBLANES) * SUBLANES + ch] for o in offsets]


def _rows8(v):
    return jnp.sum(v.reshape(v.shape[0] // SUBLANES, SUBLANES, v.shape[1]), axis=0)


def _conv_geometry(n_seq, n_meta):
    base = CONV_PAD + n_meta
    off_cf = base - (CF_WIDTH - 1)
    off_sc = base - (SC_WIDTH - 1)
    logical = -(-(n_meta + n_seq) // CONV_CH) * CONV_CH
    return base, off_cf, off_sc, logical


def _fill_conv_inputs(c_ref, v_ref, a_ref, g_ref, scv, sz, n_seq, n_meta):
    base = CONV_PAD + n_meta
    cb = scv.shape[1]
    scv[0:CONV_PAD, :] = jnp.zeros((CONV_PAD, cb), F32)
    sz[0:CONV_PAD, :] = jnp.zeros((CONV_PAD, cb), F32)

    def put(src, dst, n):
        cv = c_ref[src, :].astype(F32) * v_ref[src, :].astype(F32)
        scv[dst, :] = cv
        sz[dst, :] = a_ref[src, :].astype(F32) * _sigmoid(g_ref[src, :].astype(F32))

    put(pl.ds(n_seq, n_meta), pl.ds(CONV_PAD, n_meta), n_meta)

    def chunk(i, carry):
        t0 = pl.multiple_of(i * CONV_CH, CONV_CH)
        put(pl.ds(t0, CONV_CH), pl.ds(base + t0, CONV_CH), CONV_CH)
        return carry

    lax.fori_loop(0, n_seq // CONV_CH, chunk, 0)


def _conv_fwd(u, wsc, wcf, cbias, n_seq, n_meta, name, comm=()):
    rows = u.shape[0]
    nb, _, cb = wsc.shape
    dc = nb * cb
    base, off_cf, off_sc, _ = _conv_geometry(n_seq, n_meta)
    a_cf, a_sc = off_cf // SUBLANES * SUBLANES, off_sc // SUBLANES * SUBLANES
    ch = CONV_CH

    def body(b_ref, c_ref, v_ref, a_ref, g_ref, wsc_ref, wcf_ref, cb_ref, ysc_ref, zc_ref, scv, sz):
        _fill_conv_inputs(c_ref, v_ref, a_ref, g_ref, scv, sz, n_seq, n_meta)
        w3, w31, bias = wsc_ref[0], wcf_ref[0], cb_ref[...]

        def chunk(i, carry):
            t0 = pl.multiple_of(i * ch, ch)
            acc = jnp.zeros((ch, cb), F32)
            for k, win in enumerate(_taps(sz, t0 + a_cf, [off_cf - a_cf + k for k in range(CF_WIDTH)], ch)):
                acc = acc + win * w31[k:k + 1, :]
            zc_ref[pl.ds(t0, ch), :] = acc + bias
            s = jnp.zeros((ch, cb), F32)
            for k, win in enumerate(_taps(scv, t0 + a_sc, [off_sc - a_sc + k for k in range(SC_WIDTH)], ch)):
                s = s + win * w3[k:k + 1, :]
            ysc_ref[pl.ds(t0, ch), :] = (b_ref[pl.ds(t0, ch), :].astype(F32) * s).astype(BF16)
            return carry

        lax.fori_loop(0, n_seq // ch, chunk, 0)
        ysc_ref[n_seq:rows, :] = jnp.zeros((rows - n_seq, cb), BF16)
        zc_ref[n_seq:rows, :] = jnp.zeros((rows - n_seq, cb), F32)

    ucol = [pl.BlockSpec((rows, cb), functools.partial(lambda m, j: (0, m * nb + j), m)) for m in range(5)]
    blk = pl.BlockSpec((rows, cb), lambda j: (0, j))
    return _call(
        body, comm=comm, name=name, grid=(nb,),
        in_specs=ucol + [pl.BlockSpec((1, SC_WIDTH, cb), lambda j: (j, 0, 0)),
                         pl.BlockSpec((1, CF_WIDTH, cb), lambda j: (j, 0, 0)), pl.BlockSpec((1, cb), lambda j: (0, j))],
        out_specs=[blk, blk], out_shape=[_sds((rows, dc), BF16), _sds((rows, dc), F32)],
        scratch_shapes=[pltpu.VMEM((base + n_seq, cb), F32)] * 2, compiler_params=_params(),
    )(u, u, u, u, u, wsc, wcf, cbias)


def _conv_bwd(u, dysc, dzc, wsc, wcf, n_seq, n_meta, name, comm=()):
    rows = u.shape[0]
    nb, _, cb = wsc.shape
    dc = nb * cb
    base, off_cf, off_sc, logical = _conv_geometry(n_seq, n_meta)
    a_cf, a_sc = off_cf // SUBLANES * SUBLANES, off_sc // SUBLANES * SUBLANES
    ch = CONV_CH
    tail = CONV_PAD

    def body(b_ref, c_ref, v_ref, a_ref, g_ref, dysc_ref, dzc_ref, wsc_ref, wcf_ref,
             du_ref, dwsc_ref, dwcf_ref, dcb_ref,
             scv, sz, sds_, sdz, dlcv, dlz, accsc, acccf, accb):
        _fill_conv_inputs(c_ref, v_ref, a_ref, g_ref, scv, sz, n_seq, n_meta)
        w3, w31 = wsc_ref[0], wcf_ref[0]
        dub_ref, duc_ref, duv_ref, dua_ref, dug_ref = (du_ref.at[m] for m in range(5))
        sds_[0:n_meta, :] = jnp.zeros((n_meta, cb), F32)
        sdz[0:n_meta, :] = jnp.zeros((n_meta, cb), F32)
        behind = logical + tail - (n_meta + n_seq)
        sds_[n_meta + n_seq:logical + tail, :] = jnp.zeros((behind, cb), F32)
        sdz[n_meta + n_seq:logical + tail, :] = jnp.zeros((behind, cb), F32)
        accsc[...] = jnp.zeros_like(accsc)
        acccf[...] = jnp.zeros_like(acccf)
        accb[...] = jnp.zeros_like(accb)

        def forward_chunk(i, carry):
            t0 = pl.multiple_of(i * ch, ch)
            rws = pl.ds(t0, ch)
            dy = dysc_ref[rws, :].astype(F32)
            ds = dy * b_ref[rws, :].astype(F32)
            dz = dzc_ref[rws, :]
            sds_[pl.ds(n_meta + t0, ch), :] = ds
            sdz[pl.ds(n_meta + t0, ch), :] = dz
            s = jnp.zeros((ch, cb), F32)
            for k, win in enumerate(_taps(scv, t0 + a_sc, [off_sc - a_sc + k for k in range(SC_WIDTH)], ch)):
                s = s + win * w3[k:k + 1, :]
                accsc[k * SUBLANES:(k + 1) * SUBLANES, :] += _rows8(ds * win)
            dub_ref[rws, :] = (dy * s).astype(BF16)
            for k, win in enumerate(_taps(sz, t0 + a_cf, [off_cf - a_cf + k for k in range(CF_WIDTH)], ch)):
                acccf[k * SUBLANES:(k + 1) * SUBLANES, :] += _rows8(dz * win)
            accb[...] += _rows8(dz)
            return carry

        lax.fori_loop(0, n_seq // ch, forward_chunk, 0)

        def backward_chunk(i, carry):
            p0 = pl.multiple_of(i * ch, ch)
            dcv = jnp.zeros((ch, cb), F32)
            for k, win in enumerate(_taps(sds_, p0, [SC_WIDTH - 1 - k for k in range(SC_WIDTH)], ch)):
                dcv = dcv + win * w3[k:k + 1, :]
            dlcv[pl.ds(p0, ch), :] = dcv
            dzi = jnp.zeros((ch, cb), F32)
            for k, win in enumerate(_taps(sdz, p0, [CF_WIDTH - 1 - k for k in range(CF_WIDTH)], ch)):
                dzi = dzi + win * w31[k:k + 1, :]
            dlz[pl.ds(p0, ch), :] = dzi
            return carry

        lax.fori_loop(0, logical // ch, backward_chunk, 0)

        def gates(phys, logi):
            dcv, dzi = dlcv[logi, :], dlz[logi, :]
            duc_ref[phys, :] = (dcv * v_ref[phys, :].astype(F32)).astype(BF16)
            duv_ref[phys, :] = (dcv * c_ref[phys, :].astype(F32)).astype(BF16)
            s = _sigmoid(g_ref[phys, :].astype(F32))
            dua_ref[phys, :] = (dzi * s).astype(BF16)
            dug_ref[phys, :] = (dzi * a_ref[phys, :].astype(F32) * s * (1.0 - s)).astype(BF16)

        def gate_chunk(i, carry):
            t0 = pl.multiple_of(i * ch, ch)
            gates(pl.ds(t0, ch), pl.ds(n_meta + t0, ch))
            return carry

        lax.fori_loop(0, n_seq // ch, gate_chunk, 0)
        gates(pl.ds(n_seq, n_meta), pl.ds(0, n_meta))
        dub_ref[n_seq:rows, :] = jnp.zeros((rows - n_seq, cb), BF16)
        pad0 = n_seq + n_meta
        for ref in (duc_ref, duv_ref, dua_ref, dug_ref):
            ref[pad0:rows, :] = jnp.zeros((rows - pad0, cb), BF16)
        dwsc_ref[0] = jnp.sum(accsc[...].reshape(SC_WIDTH, SUBLANES, cb), axis=1)
        dwcf_ref[0] = jnp.sum(acccf[...].reshape(CF_WIDTH, SUBLANES, cb), axis=1)
        dcb_ref[...] = jnp.sum(accb[...], axis=0, keepdims=True)

    ucol = [pl.BlockSpec((rows, cb), functools.partial(lambda m, j: (0, m * nb + j), m)) for m in range(5)]
    blk = pl.BlockSpec((dysc.shape[0], cb), lambda j: (0, j))
    wsc_spec = pl.BlockSpec((1, SC_WIDTH, cb), lambda j: (j, 0, 0))
    wcf_spec = pl.BlockSpec((1, CF_WIDTH, cb), lambda j: (j, 0, 0))
    outs = _call(
        body, comm=comm, name=name, grid=(nb,),
        in_specs=ucol + [blk, blk, wsc_spec, wcf_spec],
        out_specs=[pl.BlockSpec((5, rows, cb), lambda j: (0, 0, j)), wsc_spec, wcf_spec,
                   pl.BlockSpec((1, cb), lambda j: (0, j))],
        out_shape=[_sds((5, rows, dc), BF16), _sds((nb, SC_WIDTH, cb), F32), _sds((nb, CF_WIDTH, cb), F32),
                   _sds((1, dc), F32)],
        scratch_shapes=[pltpu.VMEM((base + n_seq, cb), F32)] * 2 + [pltpu.VMEM((logical + tail, cb), F32)] * 2
        + [pltpu.VMEM((logical, cb), F32)] * 2
        + [pltpu.VMEM((SC_WIDTH * SUBLANES, cb), F32), pltpu.VMEM((CF_WIDTH * SUBLANES, cb), F32),
           pltpu.VMEM((SUBLANES, cb), F32)],
        compiler_params=_params(),
    )(u, u, u, u, u, dysc, dzc, wsc, wcf)
    return outs


def _row_tile(rows, cols):
    return rows // 4 if rows % 64 == 0 and rows * cols >= (1 << 18) else rows


def _pair_sum(grad, sib, idx, name, comm=()):
    _, rows, cols = grad.shape
    tr = rows

    def body(idx_ref, g_ref, s_ref, o_ref):
        o_ref[0] = (g_ref[0].astype(F32) + s_ref[0].astype(F32)).astype(o_ref.dtype)

    return _call(
        body, name=name,
        grid_spec=pltpu.PrefetchScalarGridSpec(
            num_scalar_prefetch=1, grid=(4, rows // tr),
            in_specs=[pl.BlockSpec((1, tr, cols), lambda k, i, idx_ref: (idx_ref[k], i, 0)),
                      pl.BlockSpec((1, tr, cols), lambda k, i, idx_ref: (idx_ref[4 + k], i, 0))],
            out_specs=pl.BlockSpec((1, tr, cols), lambda k, i, idx_ref: (k, i, 0))),
        out_shape=_sds((4, rows, cols), grad.dtype), compiler_params=_params(),
    )(idx, grad, sib)


def _adamw_math(w, g, m, v):
    m = ADAM_B1 * m + (1.0 - ADAM_B1) * g
    v = ADAM_B2 * v + (1.0 - ADAM_B2) * (g * g)
    m_hat = m / (1.0 - ADAM_B1 ** ADAM_STEP)
    v_hat = v / (1.0 - ADAM_B2 ** ADAM_STEP)
    delta = -ADAM_LR * (m_hat / (jnp.sqrt(v_hat) + ADAM_EPS) + ADAM_WD * w)
    return delta, m, v


def _adamw_sharded(own, got, w, m, v, name, comm=()):
    rows, cols = w.shape
    tr = _row_tile(rows, cols)

    def body(own_ref, g0_ref, g1_ref, g2_ref, w_ref, m_ref, v_ref, g_ref, d_ref, nm_ref, nv_ref):
        g = own_ref[0].astype(F32) + g0_ref[0].astype(F32) + g1_ref[0].astype(F32) + g2_ref[0].astype(F32)
        delta, nm, nv = _adamw_math(w_ref[...], g, m_ref[...], v_ref[...])
        g_ref[...] = g
        d_ref[...] = delta
        nm_ref[...] = nm
        nv_ref[...] = nv

    flat = pl.BlockSpec((tr, cols), lambda i: (i, 0))
    slot = [pl.BlockSpec((1, tr, cols), functools.partial(lambda k, i: (k, i, 0), k)) for k in range(3)]
    return _call(
        body, comm=comm, name=name, grid=(rows // tr,),
        in_specs=[slot[0]] + slot + [flat] * 3, out_specs=[flat] * 4, out_shape=[_sds((rows, cols), F32)] * 4,
        compiler_params=_params(),
    )(own, got, got, got, w, m, v)


def _adamw_replicated(gathered, segs, ws, ms, vs, loss_scale, name, comm=()):
    n = len(ws)

    def body(*refs):
        gat = refs[0]
        w_refs, m_refs, v_refs = refs[1:1 + n], refs[1 + n:1 + 2 * n], refs[1 + 2 * n:1 + 3 * n]
        outs = refs[1 + 3 * n:]

        def total(off, width):
            s = gat[0, :, off:off + width]
            for k in range(1, N_DEV):
                s = s + gat[k, :, off:off + width]
            return s

        outs[0][...] = loss_scale * total(segs[n][0], segs[n][1])
        for p in range(n):
            g = total(*segs[p])
            delta, nm, nv = _adamw_math(w_refs[p][...], g, m_refs[p][...], v_refs[p][...])
            for q, val in enumerate((g, delta, nm, nv)):
                outs[1 + 4 * p + q][...] = val
        for e, seg in enumerate(segs[n + 1:]):
            outs[1 + 4 * n + e][...] = total(*seg)

    return _call(
        body, name=name,
        out_shape=[_sds((1, segs[n][1]), F32)] + [_sds(w.shape, F32) for w in ws for _ in range(4)]
        + [_sds((1, seg[1]), F32) for seg in segs[n + 1:]],
        compiler_params=_params(),
    )(gathered, *ws, *ms, *vs)


def _adamw_plain(g, w, m, v, name):
    def body(g_ref, w_ref, m_ref, v_ref, d_ref, nm_ref, nv_ref):
        d_ref[...], nm_ref[...], nv_ref[...] = _adamw_math(w_ref[...], g_ref[...], m_ref[...], v_ref[...])

    return list(_call(body, name=name, out_shape=[_sds(w.shape, F32)] * 3)(g, w, m, v))


REPLICATED = ("ffn1_norm", "mix_norm", "b_in", "conv_cf_b", "ln_cf_g", "ln_cf_b", "ffn2_norm", "final_norm")
SHARDED = ("meta_tokens", "ffn1_w_gate", "ffn1_w_up", "ffn1_w_down", "w_in", "conv_sc_w", "conv_cf_w", "w_out",
           "ffn2_w_gate", "ffn2_w_up", "ffn2_w_down")
WEIGHTS = ("meta_tokens", "ffn1_norm", "ffn1_w_gate", "ffn1_w_up", "ffn1_w_down", "mix_norm", "w_in", "b_in",
           "conv_sc_w", "conv_cf_w", "conv_cf_b", "ln_cf_g", "ln_cf_b", "w_out", "ffn2_norm", "ffn2_w_gate",
           "ffn2_w_up", "ffn2_w_down", "final_norm")


TRANSPOSED = ("ffn1_w_gate", "ffn1_w_up", "ffn2_w_gate", "ffn2_w_up")


def _blocks2d(k, a):
    a = a.reshape(a.shape[-2:]) if a.ndim >= 2 else a.reshape(1, -1)
    return a.T if k in TRANSPOSED else a


def _step(x, tgt, w, m, v):
    n_seq, d = x.shape[1], x.shape[2]
    n_meta = w["meta_tokens"].shape[0]
    rows = -(-(n_seq + n_meta) // ROW_ALIGN) * ROW_ALIGN
    tm = rows // N_ROW_TILES
    cb = w["conv_sc_w"].shape[-1]
    dc = cb * N_DEV
    w2 = {k: _blocks2d(k, a) for k, a in w.items()}
    m2 = {k: _blocks2d(k, a) for k, a in m.items()}
    v2 = {k: _blocks2d(k, a) for k, a in v.items()}

    def as_given(k, r):
        return (r.T if k in TRANSPOSED else r).reshape(w[k].shape)

    def cast(k):
        return w2[k].astype(BF16)

    full = dict(zip(("ffn1_w_gate", "meta_tokens", "conv_sc_w", "conv_cf_w"), _exchange_alone(
        _gather_all([cast("ffn1_w_gate"), w2["meta_tokens"], w2["conv_sc_w"], w2["conv_cf_w"]]), "gather_gate1")))
    meta = jnp.transpose(full["meta_tokens"], (1, 0, 2)).reshape(n_meta, d)
    hs0 = jnp.concatenate([x[0], meta, jnp.zeros((rows - n_seq - n_meta, d), F32)], axis=0)

    tx = n_seq // N_ROW_TILES
    h1, r1 = _rms_fwd(hs0, w2["ffn1_norm"], tm, "rms_in")
    up1 = _gather_all([cast("ffn1_w_up")])
    g1 = _ffn_proj(h1, full["ffn1_w_gate"], tm, "ffn1_gate", comm=[up1])
    down1 = _gather_all([cast("ffn1_w_down")])
    u1 = _ffn_proj(h1, up1.results[0], tm, "ffn1_up", comm=[down1])
    w_in_all, w_out1 = _gather_all([cast("w_in")]), _gather_direct([cast("w_out")])
    hs1, h2, r2 = _ffn_down_norm(g1, u1, down1.results[0], hs0, w2["mix_norm"], rows // DOWN_ROW_TILES, "ffn1_down",
                                 comm=[w_in_all, w_out1])
    win = jnp.transpose(w_in_all.results[0], (1, 0, 2)).reshape(d, -1)
    w_out2, gate2 = _gather_forward(w_out1.results), _gather_direct([cast("ffn2_w_gate")])
    u = _win_fwd(h2, win, w2["b_in"], dc, tm, "mix_in", comm=[w_out2, gate2])
    wout = w_out2.results[0].reshape(-1, d)
    gate2p = _gather_forward(gate2.results)
    ysc, zc = _conv_fwd(u, full["conv_sc_w"], full["conv_cf_w"], w2["conv_cf_b"], n_seq, n_meta, "conv_fwd", comm=[gate2p])
    up2 = _gather_all([cast("ffn2_w_up")])
    y, hs2, h3, r3 = _wout_fwd(zc, ysc, wout, hs1, w2["ln_cf_g"], w2["ln_cf_b"], w2["ffn2_norm"], n_seq, tx, "mix_out",
                               comm=[up2])
    down2 = _gather_all([cast("ffn2_w_down")])
    full.update(ffn1_w_up=up1.results[0], ffn1_w_down=down1.results[0], ffn2_w_gate=gate2p.results[0],
                ffn2_w_up=up2.results[0])
    g2, u2 = _ffn_gu(h3, full["ffn2_w_gate"], full["ffn2_w_up"], tx, "ffn2_gu", comm=[down2])
    full["ffn2_w_down"] = down2.results[0]
    dhs3, dhs3b, loss_p, dgf_p = _ffn_down_loss(
        g2, u2, full["ffn2_w_down"], hs2, w2["final_norm"], tgt[0], n_seq, n_seq // DOWN_ROW_TILES, "ffn2_down_loss")

    xi, yi, ci = lax.axis_index("x"), lax.axis_index("y"), lax.axis_index("c")
    chip_of = [2 * xi + yi, 2 * (1 - xi) + yi, 2 * xi + (1 - yi), 2 * (1 - xi) + (1 - yi)]
    idx = jnp.stack([2 * ch + ci for ch in chip_of] + chip_of).astype(jnp.int32)
    out = {}

    def to_pairs(parts):
        return _pair_exchange([p.reshape((4, 2) + p.shape[1:]) for p in parts])

    def pair_sums(names, parts, pairs):
        return [_pair_sum(p, s, idx, "pair_sum_" + k) for k, p, s in zip(names, parts, pairs.results)]

    def update(names, sums, chips):
        for k, own, got in zip(names, sums, chips.results):
            res = _adamw_sharded(own, got, w2[k], m2[k], v2[k], "adamw_" + k)
            out[k] = [as_given(k, r) for r in res]

    tk = ROW_ALIGN
    th_x, th_r = n_seq // DOWN_ROW_TILES, rows // DOWN_ROW_TILES

    dwd2 = _ffn_bwd_dw_down(g2, u2, dhs3b, tk, "ffn2_bwd_dw_down")
    p_d2 = to_pairs([dwd2])
    dg2, du2 = _ffn_bwd_da(dhs3b, full["ffn2_w_down"], g2, u2, tx, "ffn2_bwd_da", comm=[p_d2])
    s_d2 = pair_sums(["ffn2_w_down"], [dwd2], p_d2)
    c_d2 = _chip_exchange(s_d2)
    gu2 = _ffn_bwd_dw_gu(h3, dg2, du2, tk, "ffn2_bwd_dw_gu", comm=[c_d2])
    update(["ffn2_w_down"], s_d2, c_d2)
    p_gu2 = to_pairs(gu2)
    half2 = _ffn_bwd_dh_gate(dg2, full["ffn2_w_gate"], th_x, "ffn2_bwd_dh_gate", comm=[p_gu2])
    s_gu2 = pair_sums(["ffn2_w_gate", "ffn2_w_up"], gu2, p_gu2)
    c_g2, c_u2 = _chip_exchange(s_gu2[:1]), _chip_exchange(s_gu2[1:])
    dhs2, dhs2b, dn3_p = _ffn_bwd_dh(du2, full["ffn2_w_up"], half2, hs2, r3, w2["ffn2_norm"], dhs3, th_x, "ffn2_bwd_dh",
                                     comm=[c_g2], zeros=jnp.zeros((rows, d), F32))
    dysc, dzc, dlg_p, dlb_p = _wout_bwd_dy(dhs2b, wout, zc, w2["ln_cf_g"], w2["ln_cf_b"], tx, "mix_out_bwd_dy",
                                           comm=[c_u2])
    update(["ffn2_w_gate"], s_gu2[:1], c_g2)
    update(["ffn2_w_up"], s_gu2[1:], c_u2)

    dwout = _wout_bwd_dw(y, dhs2b, N_DEV, tk, "mix_out_bwd_dw")
    p_wo = to_pairs([dwout])
    du, dcsw, dccw, dcb = _conv_bwd(u, dysc, dzc, full["conv_sc_w"], full["conv_cf_w"], n_seq, n_meta, "conv_bwd",
                                    comm=[p_wo])
    s_wo = pair_sums(["w_out"], [dwout], p_wo)
    c_wo = _chip_exchange(s_wo)
    dhs1, dhs1b, dn2_p = _win_bwd_dh(du, win, hs1, r2, w2["mix_norm"], dhs2, tm, "mix_in_bwd_dh", comm=[c_wo])
    update(["w_out"], s_wo, c_wo)
    dwin, dbin = _win_bwd_dw(h2, du, tk, "mix_in_bwd_dw")
    mixer = ("w_in", "conv_sc_w", "conv_cf_w")
    gradsm = [jnp.transpose(dwin.reshape(d, N_DEV, -1), (1, 0, 2)), dcsw, dccw]
    p_m = to_pairs(gradsm)
    dwd1 = _ffn_bwd_dw_down(g1, u1, dhs1b, tk, "ffn1_bwd_dw_down", comm=[p_m])
    s_m = pair_sums(mixer, gradsm, p_m)
    c_m, p_d1 = _chip_exchange(s_m), to_pairs([dwd1])
    dg1, du1 = _ffn_bwd_da(dhs1b, full["ffn1_w_down"], g1, u1, tm, "ffn1_bwd_da", comm=[c_m, p_d1])
    update(mixer, s_m, c_m)
    s_d1 = pair_sums(["ffn1_w_down"], [dwd1], p_d1)
    c_d1 = _chip_exchange(s_d1)
    gu1 = _ffn_bwd_dw_gu(h1, dg1, du1, tk, "ffn1_bwd_dw_gu", comm=[c_d1])
    update(["ffn1_w_down"], s_d1, c_d1)
    p_gu1 = to_pairs(gu1)
    _exchange_alone(p_gu1, "reduce_pair_ffn1")
    s_gu1 = pair_sums(["ffn1_w_gate", "ffn1_w_up"], gu1, p_gu1)
    c_g1, c_u1 = _chip_exchange(s_gu1[:1]), _chip_exchange(s_gu1[1:])
    half1 = _ffn_bwd_dh_gate(dg1, full["ffn1_w_gate"], th_r, "ffn1_bwd_dh_gate", comm=[c_g1])
    dhs0, _, dn1_p = _ffn_bwd_dh(du1, full["ffn1_w_up"], half1, hs0, r1, w2["ffn1_norm"], dhs1, th_r, "ffn1_bwd_dh",
                                 comm=[c_u1])
    update(["ffn1_w_gate"], s_gu1[:1], c_g1)
    update(["ffn1_w_up"], s_gu1[1:], c_u1)
    grad_x = dhs0[:n_seq][None]

    partial = {
        "ffn1_norm": dn1_p.sum(0), "mix_norm": dn2_p.sum(0), "b_in": dbin, "conv_cf_b": dcb,
        "ln_cf_g": dlg_p.sum(0), "ln_cf_b": dlb_p.sum(0), "ffn2_norm": dn3_p.sum(0), "final_norm": dgf_p.sum(0),
    }
    loss_seg = jnp.pad(loss_p.sum((0, 2)).reshape(1, 1), ((0, 0), (0, 127)))
    pieces = [partial[k] for k in REPLICATED] + [loss_seg, dhs0[n_seq:n_seq + n_meta].reshape(1, n_meta * d)]
    segs, off = [], 0
    for p in pieces:
        segs.append((off, p.shape[1]))
        off += p.shape[1]
    rows8 = _gather_rows(jnp.concatenate(pieces, axis=1), "gather_small")
    res = _adamw_replicated(rows8, segs, [w2[k] for k in REPLICATED], [m2[k] for k in REPLICATED],
                            [v2[k] for k in REPLICATED], 0.5 / d, "adamw_replicated")
    loss = res[0][0, 0]
    for p, k in enumerate(REPLICATED):
        out[k] = [r.reshape(w[k].shape) for r in res[1 + 4 * p:5 + 4 * p]]
    ob = w2["meta_tokens"].shape[1]
    gmeta = lax.dynamic_slice_in_dim(res[-1].reshape(n_meta, d), (4 * xi + 2 * yi + ci) * ob, ob, axis=1)
    out["meta_tokens"] = [gmeta] + _adamw_plain(gmeta, w2["meta_tokens"], m2["meta_tokens"], v2["meta_tokens"], "adamw_meta_tokens")

    return (loss, grad_x, *[out[k][0] for k in WEIGHTS], *[out[k][1] for k in WEIGHTS],
            *[out[k][2] for k in WEIGHTS], *[out[k][3] for k in WEIGHTS])


def kernel(x, meta_tokens, ffn1_norm, ffn1_w_gate, ffn1_w_up, ffn1_w_down, mix_norm, w_in, b_in, conv_sc_w, conv_cf_w, conv_cf_b, ln_cf_g, ln_cf_b, w_out, ffn2_norm, ffn2_w_gate, ffn2_w_up, ffn2_w_down, final_norm, loss_target, m_meta_tokens, m_ffn1_norm, m_ffn1_w_gate, m_ffn1_w_up, m_ffn1_w_down, m_mix_norm, m_w_in, m_b_in, m_conv_sc_w, m_conv_cf_w, m_conv_cf_b, m_ln_cf_g, m_ln_cf_b, m_w_out, m_ffn2_norm, m_ffn2_w_gate, m_ffn2_w_up, m_ffn2_w_down, m_final_norm, v_meta_tokens, v_ffn1_norm, v_ffn1_w_gate, v_ffn1_w_up, v_ffn1_w_down, v_mix_norm, v_w_in, v_b_in, v_conv_sc_w, v_conv_cf_w, v_conv_cf_b, v_ln_cf_g, v_ln_cf_b, v_w_out, v_ffn2_norm, v_ffn2_w_gate, v_ffn2_w_up, v_ffn2_w_down, v_final_norm):
    given = dict(locals())
    w = {k: given[k] for k in WEIGHTS}
    m = {k: given["m_" + k] for k in WEIGHTS}
    v = {k: given["v_" + k] for k in WEIGHTS}
    return _step(x, loss_target, w, m, v)
```

```python
import functools

import jax
import jax.numpy as jnp
from jax import lax
from jax.experimental import pallas as pl
from jax.experimental.pallas import tpu as pltpu

F32 = jnp.float32
BF16 = jnp.bfloat16
EPS = 1e-6
FFN_RES_SCALE = 0.5
SC_WIDTH = 3
CF_WIDTH = 31
ADAM_LR = 0.001
ADAM_B1 = 0.9
ADAM_B2 = 0.999
ADAM_EPS = 1e-08
ADAM_WD = 0.01
ADAM_STEP = 10

N_DEV = 8
N_ROW_TILES = 8
ROW_ALIGN = 256
CONV_PAD = 32
CONV_CH = 128
SUB_ROWS = 32
DOWN_ROW_TILES = 16
SUBLANES = 8
BF16_ROWS = 16
VMEM_LIMIT = 56 * 1024 * 1024
MESH = pl.DeviceIdType.MESH
ANY = pl.BlockSpec(memory_space=pl.ANY)

NT = (((1,), (1,)), ((), ()))
TN = (((0,), (0,)), ((), ()))


def _pallas(body, **kw):
    return pl.pallas_call(body, **kw)


class _Exchange:
    def __init__(self, inputs, out_shapes, sem_shapes, start, finish, aliases=None):
        self.inputs, self.out_shapes, self.sem_shapes = list(inputs), list(out_shapes), list(sem_shapes)
        self.start, self.finish, self.aliases = start, finish, dict(aliases or {})
        self.results = None


def _call(body, comm=(), **kw):
    if not comm:
        return _pallas(body, **kw)
    grid = kw.pop("grid")
    in_specs = list(kw.pop("in_specs"))
    out_specs, out_shape = kw.pop("out_specs"), kw.pop("out_shape")
    scratch = list(kw.pop("scratch_shapes", []))
    single = not isinstance(out_shape, (list, tuple))
    out_specs, out_shape = ([out_specs], [out_shape]) if single else (list(out_specs), list(out_shape))
    n_in, n_out, n_scr = len(in_specs), len(out_shape), len(scratch)
    c_in = [a for job in comm for a in job.inputs]
    c_out = [s for job in comm for s in job.out_shapes]
    c_sem = [s for job in comm for s in job.sem_shapes]
    aliases, i0, o0 = dict(kw.pop("input_output_aliases", {})), n_in, n_out
    for job in comm:
        aliases.update({i0 + i: o0 + o for i, o in job.aliases.items()})
        i0, o0 = i0 + len(job.inputs), o0 + len(job.out_shapes)

    def hosted(*refs):
        pos = [0]

        def take(n):
            pos[0] += n
            return refs[pos[0] - n:pos[0]]

        ins, cins, outs, couts, scr, sems = take(n_in), take(len(c_in)), take(n_out), take(len(c_out)), take(n_scr), take(len(c_sem))
        ids = [pl.program_id(k) for k in range(len(grid))]
        first = functools.reduce(jnp.logical_and, [i == 0 for i in ids])
        last = functools.reduce(jnp.logical_and, [i == g - 1 for i, g in zip(ids, grid)])

        def each(phase):
            i, o, s = 0, 0, 0
            for job in comm:
                ni, no, ns = len(job.inputs), len(job.out_shapes), len(job.sem_shapes)
                getattr(job, phase)(cins[i:i + ni], couts[o:o + no], sems[s:s + ns])
                i, o, s = i + ni, o + no, s + ns

        @pl.when(first)
        def _():
            each("start")

        body(*ins, *outs, *scr)

        @pl.when(last)
        def _():
            each("finish")

    call = _pallas(
        hosted, grid=grid, in_specs=in_specs + [ANY] * len(c_in), out_specs=out_specs + [ANY] * len(c_out),
        out_shape=out_shape + c_out, scratch_shapes=scratch + c_sem, input_output_aliases=aliases, **kw)

    def run(*args):
        res = call(*args, *c_in)
        o = n_out
        for job in comm:
            job.results = list(res[o:o + len(job.out_shapes)])
            o += len(job.out_shapes)
        return res[0] if single else list(res[:n_out])

    return run


def _exchange_alone(job, name, comm=()):
    n_in, n_out = len(job.inputs), len(job.out_shapes)

    def body(*refs):
        ins, outs, sems = refs[:n_in], refs[n_in:n_in + n_out], refs[n_in + n_out:]
        job.start(ins, outs, sems)
        job.finish(ins, outs, sems)

    res = _pallas(body, name=name, in_specs=[ANY] * n_in, out_specs=[ANY] * n_out, out_shape=job.out_shapes,
                  scratch_shapes=job.sem_shapes, input_output_aliases=job.aliases)(*job.inputs)
    job.results = list(res)
    return job.results


def _params(**kw):
    return pltpu.CompilerParams(vmem_limit_bytes=VMEM_LIMIT, **kw)


def _sigmoid(x):
    return 0.5 * jnp.tanh(0.5 * x) + 0.5


def _sds(shape, dtype):
    return jax.ShapeDtypeStruct(shape, dtype)


def _place():
    x, y, c = lax.axis_index("x"), lax.axis_index("y"), lax.axis_index("c")
    chips = [(1 - x, y), (x, 1 - y), (1 - x, 1 - y)]
    return x, y, c, chips


def _slot(ref, p):
    return ref.at[4 * p[0] + 2 * p[1] + p[2]]


def _remote(src, dst, send_sem, recv_sem, to):
    return pltpu.make_async_remote_copy(src_ref=src, dst_ref=dst, send_sem=send_sem, recv_sem=recv_sem,
                                        device_id=to, device_id_type=MESH)


def _gather_direct(arrs):
    n = len(arrs)

    def copies(ins, outs, sems):
        send_sems, recv_sems, local_sems = sems
        x, y, c, chips = _place()
        me = (x, y, c)
        peers = [(x, y, 1 - c)] + [(*chip, c) for chip in chips]
        local = [pltpu.make_async_copy(ins[a], _slot(outs[a], me), local_sems.at[a]) for a in range(n)]
        sends = [_remote(ins[a], _slot(outs[a], me), send_sems.at[a, k], recv_sems.at[a, k], peer)
                 for a in range(n) for k, peer in enumerate(peers)]
        arrivals = [_remote(ins[a], _slot(outs[a], peer), send_sems.at[a, k], recv_sems.at[a, k], peer)
                    for a in range(n) for k, peer in enumerate(peers)]
        return local, sends, arrivals

    def start(ins, outs, sems):
        local, sends, _ = copies(ins, outs, sems)
        for cp in local + sends:
            cp.start()

    def finish(ins, outs, sems):
        local, sends, arrivals = copies(ins, outs, sems)
        for cp in arrivals:
            cp.wait_recv()
        for cp in sends:
            cp.wait_send()
        for cp in local:
            cp.wait()

    dma = pltpu.SemaphoreType.DMA
    return _Exchange(arrs, [_sds((N_DEV,) + a.shape, a.dtype) for a in arrs], [dma((n, 4)), dma((n, 4)), dma((n,))],
                     start, finish)


def _gather_forward(gathered):
    n = len(gathered)

    def copies(ins, outs, sems):
        send_sems, recv_sems = sems
        x, y, c, chips = _place()
        sibling = (x, y, 1 - c)
        sends = [_remote(_slot(ins[a], (*chip, c)), _slot(outs[a], (*chip, c)), send_sems.at[a, j], recv_sems.at[a, j], sibling)
                 for a in range(n) for j, chip in enumerate(chips)]
        arrivals = [_remote(_slot(ins[a], (*chip, c)), _slot(outs[a], (*chip, 1 - c)), send_sems.at[a, j], recv_sems.at[a, j], sibling)
                    for a in range(n) for j, chip in enumerate(chips)]
        return sends, arrivals

    def start(ins, outs, sems):
        for cp in copies(ins, outs, sems)[0]:
            cp.start()

    def finish(ins, outs, sems):
        sends, arrivals = copies(ins, outs, sems)
        for cp in arrivals:
            cp.wait_recv()
        for cp in sends:
            cp.wait_send()

    dma = pltpu.SemaphoreType.DMA
    return _Exchange(gathered, [_sds(a.shape, a.dtype) for a in gathered], [dma((n, 3)), dma((n, 3))], start, finish,
                     aliases={a: a for a in range(n)})


def _gather_all(arrs):
    direct = _gather_direct(arrs)
    n = len(arrs)
    n_sems = len(direct.sem_shapes)
    dma = pltpu.SemaphoreType.DMA

    def start(ins, outs, sems):
        direct.start(ins, outs, sems[:n_sems])

    def finish(ins, outs, sems):
        direct.finish(ins, outs, sems[:n_sems])
        passed = _gather_forward(outs)
        passed.start(outs, outs, sems[n_sems:])
        passed.finish(outs, outs, sems[n_sems:])

    return _Exchange(arrs, direct.out_shapes, direct.sem_shapes + [dma((n, 3)), dma((n, 3))], start, finish)


def _pair_exchange(arrs):
    n = len(arrs)

    def copies(ins, outs, sems):
        x, y, c, _ = _place()
        return [_remote(ins[a].at[:, 1 - c], outs[a], sems[0].at[a], sems[1].at[a], (x, y, 1 - c)) for a in range(n)]

    def start(ins, outs, sems):
        for cp in copies(ins, outs, sems):
            cp.start()

    def finish(ins, outs, sems):
        for cp in copies(ins, outs, sems):
            cp.wait()

    dma = pltpu.SemaphoreType.DMA
    return _Exchange(arrs, [_sds((4,) + a.shape[2:], a.dtype) for a in arrs], [dma((n,)), dma((n,))], start, finish)


def _chip_exchange(arrs):
    n = len(arrs)

    def copies(ins, outs, sems):
        x, y, c, chips = _place()
        return [_remote(ins[a].at[1 + j], outs[a].at[j], sems[0].at[a, j], sems[1].at[a, j], (*chip, c))
                for a in range(n) for j, chip in enumerate(chips)]

    def start(ins, outs, sems):
        for cp in copies(ins, outs, sems):
            cp.start()

    def finish(ins, outs, sems):
        for cp in copies(ins, outs, sems):
            cp.wait()

    dma = pltpu.SemaphoreType.DMA
    return _Exchange(arrs, [_sds((3,) + a.shape[1:], a.dtype) for a in arrs], [dma((n, 3)), dma((n, 3))], start, finish)


def _gather_rows(vec, name, comm=()):
    def body(in_ref, out_ref, send_sems, recv_sems, local_sem):
        x, y, c, _ = _place()
        me = 4 * x + 2 * y + c
        mine = pltpu.make_async_copy(in_ref, out_ref.at[me], local_sem)
        mine.start()
        copies = []
        for k in range(1, N_DEV):
            to = (1 - x if k & 4 else x, 1 - y if k & 2 else y, 1 - c if k & 1 else c)
            copies.append(pltpu.make_async_remote_copy(
                src_ref=in_ref, dst_ref=out_ref.at[me], send_sem=send_sems.at[k - 1], recv_sem=recv_sems.at[k - 1],
                device_id=to, device_id_type=MESH))
        for cp in copies:
            cp.start()
        for cp in copies:
            cp.wait()
        mine.wait()

    return _call(
        body, name=name, out_shape=_sds((N_DEV,) + vec.shape, vec.dtype), in_specs=[ANY], out_specs=ANY,
        scratch_shapes=[pltpu.SemaphoreType.DMA((7,)), pltpu.SemaphoreType.DMA((7,)), pltpu.SemaphoreType.DMA],
    )(vec)


def _group_rows(tm):
    return SUB_ROWS if tm % SUB_ROWS == 0 else BF16_ROWS


def _for_row_groups(tm, fn):
    sub = _group_rows(tm)

    def step(i, carry):
        fn(pl.ds(pl.multiple_of(i * sub, sub), sub), i * sub)
        return carry

    lax.fori_loop(0, tm // sub, step, 0)


def _once(shape, index_map):
    return pl.BlockSpec(shape, index_map, pipeline_mode=pl.Buffered(1))


def _rms_fwd(hs, g, tm, name, comm=()):
    rows, d = hs.shape

    def body(hs_ref, g_ref, h_ref, r_ref):
        def group(rs, _):
            xv = hs_ref[rs, :]
            r = lax.rsqrt(jnp.mean(xv * xv, axis=-1, keepdims=True) + EPS)
            h_ref[rs, :] = (xv * r * g_ref[...]).astype(BF16)
            r_ref[rs, :] = r

        _for_row_groups(tm, group)

    return _call(
        body, comm=comm, name=name, grid=(rows // tm,),
        in_specs=[pl.BlockSpec((tm, d), lambda i: (i, 0)), pl.BlockSpec((1, d), lambda i: (0, 0))],
        out_specs=[pl.BlockSpec((tm, d), lambda i: (i, 0)), pl.BlockSpec((tm, 1), lambda i: (i, 0))],
        out_shape=[_sds((rows, d), BF16), _sds((rows, 1), F32)], compiler_params=_params(),
    )(hs, g)


def _ffn_gu(h, wg, wu, tm, name, comm=()):
    rows, d = h.shape
    nb, fb, _ = wg.shape

    def body(h_ref, wg_ref, wu_ref, g_ref, u_ref):
        hv = h_ref[...]
        g_ref[0] = lax.dot_general(hv, wg_ref[0], NT, preferred_element_type=F32).astype(BF16)
        u_ref[0] = lax.dot_general(hv, wu_ref[0], NT, preferred_element_type=F32).astype(BF16)

    wspec = pl.BlockSpec((1, fb, d), lambda j, i: (j, 0, 0))
    ospec = pl.BlockSpec((1, tm, fb), lambda j, i: (j, i, 0))
    return _call(
        body, comm=comm, name=name, grid=(nb, rows // tm),
        in_specs=[pl.BlockSpec((tm, d), lambda j, i: (i, 0)), wspec, wspec],
        out_specs=[ospec, ospec], out_shape=[_sds((nb, rows, fb), BF16)] * 2, compiler_params=_params(),
    )(h, wg, wu)


def _ffn_proj(h, w, tm, name, comm=()):
    rows, d = h.shape
    nb, fb, _ = w.shape

    def body(h_ref, w_ref, p_ref):
        p_ref[0] = lax.dot_general(h_ref[...], w_ref[0], NT, preferred_element_type=F32).astype(BF16)

    return _call(
        body, comm=comm, name=name, grid=(nb, rows // tm),
        in_specs=[pl.BlockSpec((tm, d), lambda j, i: (i, 0)), pl.BlockSpec((1, fb, d), lambda j, i: (j, 0, 0))],
        out_specs=pl.BlockSpec((1, tm, fb), lambda j, i: (j, i, 0)), out_shape=_sds((nb, rows, fb), BF16),
        compiler_params=_params(),
    )(h, w)


def _down_product(g_ref, u_ref, wd_ref, acc_ref):
    part = None
    for b in range(g_ref.shape[0]):
        gv = g_ref[b]
        a = gv * _sigmoid(gv) * u_ref[b]
        p = jnp.dot(a, wd_ref[b], preferred_element_type=F32)
        part = p if part is None else part + p
    acc_ref[...] = part


def _ffn_down_norm(g, u, wd, hs, gn, tm, name, comm=()):
    nb, rows, fb = g.shape
    d = hs.shape[1]

    def body(g_ref, u_ref, wd_ref, hs_ref, gn_ref, hsn_ref, hn_ref, rn_ref, acc_ref):
        _down_product(g_ref, u_ref, wd_ref, acc_ref)

        def group(rs, _):
            hsn = hs_ref[rs, :] + FFN_RES_SCALE * acc_ref[rs, :]
            r = lax.rsqrt(jnp.mean(hsn * hsn, axis=-1, keepdims=True) + EPS)
            hsn_ref[rs, :] = hsn
            hn_ref[rs, :] = (hsn * r * gn_ref[...]).astype(BF16)
            rn_ref[rs, :] = r

        _for_row_groups(tm, group)

    aspec = pl.BlockSpec((nb, tm, fb), lambda i: (0, i, 0))
    row = pl.BlockSpec((tm, d), lambda i: (i, 0))
    return _call(
        body, comm=comm, name=name, grid=(rows // tm,),
        in_specs=[aspec, aspec, _once((nb, fb, d), lambda i: (0, 0, 0)), row, pl.BlockSpec((1, d), lambda i: (0, 0))],
        out_specs=[row, row, pl.BlockSpec((tm, 1), lambda i: (i, 0))],
        out_shape=[_sds((rows, d), F32), _sds((rows, d), BF16), _sds((rows, 1), F32)],
        scratch_shapes=[pltpu.VMEM((tm, d), F32)], compiler_params=_params(),
    )(g, u, wd, hs, gn)


def _ffn_down_loss(g, u, wd, hs, gf, tgt, n_seq, tm, name, comm=()):
    nb, rows, fb = g.shape
    d = hs.shape[1]
    nt = rows // tm

    def body(g_ref, u_ref, wd_ref, hs_ref, gf_ref, tgt_ref, dhs_ref, dhsb_ref, loss_ref, dgf_ref, acc_ref):
        i = pl.program_id(0)
        _down_product(g_ref, u_ref, wd_ref, acc_ref)
        loss_ref[0] = jnp.zeros((1, d), F32)
        dgf_ref[0] = jnp.zeros((1, d), F32)

        def group(rs, r0):
            hs3 = hs_ref[rs, :] + FFN_RES_SCALE * acc_ref[rs, :]
            r = lax.rsqrt(jnp.mean(hs3 * hs3, axis=-1, keepdims=True) + EPS)
            gfv = gf_ref[...]
            y = hs3 * r
            rowid = i * tm + r0 + lax.broadcasted_iota(jnp.int32, (_group_rows(tm), 1), 0)
            err = jnp.where(rowid < n_seq, y * gfv - tgt_ref[rs, :], 0.0)
            loss_ref[0] += jnp.sum(err * err, axis=0, keepdims=True)
            dout = err * (1.0 / d)
            dgf_ref[0] += jnp.sum(dout * y, axis=0, keepdims=True)
            t = dout * gfv
            dhs = r * t - hs3 * (r * r * r) * jnp.mean(t * hs3, axis=-1, keepdims=True)
            dhs_ref[rs, :] = dhs
            dhsb_ref[rs, :] = dhs.astype(BF16)

        _for_row_groups(tm, group)

    aspec = pl.BlockSpec((nb, tm, fb), lambda i: (0, i, 0))
    row = pl.BlockSpec((tm, d), lambda i: (i, 0))
    part = pl.BlockSpec((1, 1, d), lambda i: (i, 0, 0))
    return _call(
        body, comm=comm, name=name, grid=(nt,),
        in_specs=[aspec, aspec, _once((nb, fb, d), lambda i: (0, 0, 0)), row, pl.BlockSpec((1, d), lambda i: (0, 0)), row],
        out_specs=[row, row, part, part],
        out_shape=[_sds((rows, d), F32), _sds((rows, d), BF16), _sds((nt, 1, d), F32), _sds((nt, 1, d), F32)],
        scratch_shapes=[pltpu.VMEM((tm, d), F32)], compiler_params=_params(),
    )(g, u, wd, hs, gf, tgt)


def _ffn_bwd_da(do, wd, g, u, tm, name, comm=()):
    nb, rows, fb = g.shape
    d = do.shape[1]

    def body(do_ref, wd_ref, g_ref, u_ref, dg_ref, du_ref):
        rs = pl.ds(pl.multiple_of(pl.program_id(1) * tm, tm), tm)
        da = (FFN_RES_SCALE * lax.dot_general(do_ref[rs, :], wd_ref[0], NT, preferred_element_type=F32)).astype(BF16)
        gv, uv = g_ref[0], u_ref[0]
        s = _sigmoid(gv)
        sg = gv * s
        du_ref[0] = da * sg
        dg_ref[0] = da * uv * (s + sg * (1.0 - s))

    aspec = pl.BlockSpec((1, tm, fb), lambda j, i: (j, i, 0))
    return _call(
        body, comm=comm, name=name, grid=(nb, rows // tm),
        in_specs=[_once((rows, d), lambda j, i: (0, 0)), pl.BlockSpec((1, fb, d), lambda j, i: (j, 0, 0)), aspec, aspec],
        out_specs=[aspec] * 2, out_shape=[_sds((nb, rows, fb), BF16)] * 2, compiler_params=_params(),
    )(do, wd, g, u)


def _rms_bwd(tm, d, dh_ref, hs_ref, r_ref, gn_ref, dres_ref, dhs_ref, dhsb_ref, dgn_ref):
    dgn_ref[0] = jnp.zeros((1, d), F32)

    def group(rs, _):
        dh, hs, r = dh_ref[rs, :], hs_ref[rs, :], r_ref[rs, :]
        dgn_ref[0] += jnp.sum(dh * (hs * r), axis=0, keepdims=True)
        t = dh * gn_ref[...]
        dhs = dres_ref[rs, :] + r * t - hs * (r * r * r) * jnp.mean(t * hs, axis=-1, keepdims=True)
        dhs_ref[rs, :] = dhs
        dhsb_ref[rs, :] = dhs.astype(BF16)

    _for_row_groups(tm, group)


def _blocks_dot(x_ref, w_ref):
    part = None
    for b in range(x_ref.shape[0]):
        p = jnp.dot(x_ref[b], w_ref[b], preferred_element_type=F32)
        part = p if part is None else part + p
    return part


def _ffn_bwd_dh_gate(dg, wg, tm, name, comm=()):
    nb, rows, fb = dg.shape
    d = wg.shape[2]

    def body(dg_ref, wg_ref, o_ref):
        o_ref[...] = _blocks_dot(dg_ref, wg_ref)

    return _call(
        body, comm=comm, name=name, grid=(rows // tm,),
        in_specs=[pl.BlockSpec((nb, tm, fb), lambda i: (0, i, 0)), _once((nb, fb, d), lambda i: (0, 0, 0))],
        out_specs=pl.BlockSpec((tm, d), lambda i: (i, 0)), out_shape=_sds((rows, d), F32), compiler_params=_params(),
    )(dg, wg)


def _ffn_bwd_dh(du, wu, half, hs, r, gn, dres, tm, name, comm=(), zeros=None):
    nb, rows, fb = du.shape
    d = hs.shape[1]
    nt = rows // tm
    extra = [] if zeros is None else [zeros]

    def body(du_ref, wu_ref, half_ref, hs_ref, r_ref, gn_ref, dres_ref, *rest):
        dhs_ref, dhsb_ref, dgn_ref, acc_ref = rest[len(extra):]
        acc_ref[...] = half_ref[...] + _blocks_dot(du_ref, wu_ref)
        _rms_bwd(tm, d, acc_ref, hs_ref, r_ref, gn_ref, dres_ref, dhs_ref, dhsb_ref, dgn_ref)

    row = pl.BlockSpec((tm, d), lambda i: (i, 0))
    return _call(
        body, comm=comm, name=name, grid=(nt,),
        in_specs=[pl.BlockSpec((nb, tm, fb), lambda i: (0, i, 0)), _once((nb, fb, d), lambda i: (0, 0, 0)), row, row,
                  pl.BlockSpec((tm, 1), lambda i: (i, 0)), pl.BlockSpec((1, d), lambda i: (0, 0)), row] + [ANY] * len(extra),
        out_specs=[row, row, pl.BlockSpec((1, 1, d), lambda i: (i, 0, 0))],
        out_shape=[_sds((rows, d) if zeros is None else zeros.shape, F32), _sds((rows, d), BF16), _sds((nt, 1, d), F32)],
        scratch_shapes=[pltpu.VMEM((tm, d), F32)], compiler_params=_params(),
        input_output_aliases={7: 0} if extra else {},
    )(du, wu, half, hs, r, gn, dres, *extra)


class _KSteps:
    def __init__(self, rows, tk):
        self.main, self.rem, self.tk = rows // tk, rows % tk, tk
        self.n = self.main + (1 if self.rem else 0)
        if self.rem:
            assert rows % self.rem == 0, (rows, tk)
            self.last = rows // self.rem - 1

    def specs(self, makers):
        main_at = (lambda k: jnp.minimum(k, self.main - 1)) if self.rem else (lambda k: k)
        out = [make(self.tk, main_at) for make in makers]
        if self.rem:
            out += [make(self.rem, lambda k: self.last) for make in makers]
        return out

    def args(self, arrs):
        return list(arrs) * (2 if self.rem else 1)

    def each(self, k, refs, fn):
        if not self.rem:
            fn(*refs)
            return
        n = len(refs) // 2
        pl.when(k < self.main)(lambda: fn(*refs[:n]))
        pl.when(k == self.main)(lambda: fn(*refs[n:]))


def _ffn_bwd_dw_gu(h, dg, du, tk, name, comm=()):
    nb, rows, fb = dg.shape
    d = h.shape[1]
    steps = _KSteps(rows, tk)

    def body(*refs):
        ins, (dwg_ref, dwu_ref, accg, accu) = refs[:-4], refs[-4:]
        k = pl.program_id(1)

        @pl.when(k == 0)
        def _():
            accg[...] = jnp.zeros_like(accg)
            accu[...] = jnp.zeros_like(accu)

        def add(h_ref, dg_ref, du_ref):
            hv = h_ref[...]
            accg[...] += lax.dot_general(dg_ref[0], hv, TN, preferred_element_type=F32)
            accu[...] += lax.dot_general(du_ref[0], hv, TN, preferred_element_type=F32)

        steps.each(k, ins, add)

        @pl.when(k == steps.n - 1)
        def _():
            dwg_ref[0] = accg[...].astype(BF16)
            dwu_ref[0] = accu[...].astype(BF16)

    wspec = pl.BlockSpec((1, fb, d), lambda j, k: (j, 0, 0))
    specs = steps.specs([lambda t, at: pl.BlockSpec((t, d), lambda j, k: (at(k), 0)),
                         lambda t, at: pl.BlockSpec((1, t, fb), lambda j, k: (j, at(k), 0)),
                         lambda t, at: pl.BlockSpec((1, t, fb), lambda j, k: (j, at(k), 0))])
    return _call(
        body, comm=comm, name=name, grid=(nb, steps.n),
        in_specs=specs, out_specs=[wspec] * 2,
        out_shape=[_sds((nb, fb, d), BF16)] * 2, scratch_shapes=[pltpu.VMEM((fb, d), F32)] * 2,
        compiler_params=_params(),
    )(*steps.args([h, dg, du]))


def _ffn_bwd_dw_down(g, u, do, tk, name, comm=()):
    nb, rows, fb = g.shape
    d = do.shape[1]
    steps = _KSteps(rows, tk)

    def body(*refs):
        ins, (dwd_ref, acc) = refs[:-2], refs[-2:]
        k = pl.program_id(1)

        @pl.when(k == 0)
        def _():
            acc[...] = jnp.zeros_like(acc)

        def add(g_ref, u_ref, do_ref):
            gv = g_ref[0]
            a = gv * _sigmoid(gv) * u_ref[0]
            acc[...] += lax.dot_general(a, do_ref[...], TN, preferred_element_type=F32)

        steps.each(k, ins, add)

        @pl.when(k == steps.n - 1)
        def _():
            dwd_ref[0] = (FFN_RES_SCALE * acc[...]).astype(BF16)

    specs = steps.specs([lambda t, at: pl.BlockSpec((1, t, fb), lambda j, k: (j, at(k), 0)),
                         lambda t, at: pl.BlockSpec((1, t, fb), lambda j, k: (j, at(k), 0)),
                         lambda t, at: pl.BlockSpec((t, d), lambda j, k: (at(k), 0))])
    return _call(
        body, comm=comm, name=name, grid=(nb, steps.n), in_specs=specs,
        out_specs=pl.BlockSpec((1, fb, d), lambda j, k: (j, 0, 0)), out_shape=_sds((nb, fb, d), BF16),
        scratch_shapes=[pltpu.VMEM((fb, d), F32)], compiler_params=_params(),
    )(*steps.args([g, u, do]))


def _win_fwd(h, w, b, dc, tm, name, comm=()):
    rows, d = h.shape
    ng = w.shape[1] // dc

    def body(h_ref, w_ref, b_ref, u_ref):
        u_ref[...] = (jnp.dot(h_ref[...], w_ref[...], preferred_element_type=F32) + b_ref[...]).astype(BF16)

    return _call(
        body, comm=comm, name=name, grid=(ng, rows // tm),
        in_specs=[pl.BlockSpec((tm, d), lambda m, i: (i, 0)), pl.BlockSpec((d, dc), lambda m, i: (0, m)),
                  pl.BlockSpec((1, dc), lambda m, i: (0, m))],
        out_specs=pl.BlockSpec((tm, dc), lambda m, i: (i, m)), out_shape=_sds((rows, ng * dc), BF16),
        compiler_params=_params(),
    )(h, w, b)


def _win_bwd_dh(du, w, hs, r, gn, dres, tm, name, comm=()):
    rows, d = hs.shape
    ng, _, dc = du.shape
    nt = rows // tm

    def body(du_ref, w_ref, hs_ref, r_ref, gn_ref, dres_ref, dhs_ref, dhsb_ref, dgn_ref, acc_ref):
        part = None
        for m in range(ng):
            p = lax.dot_general(du_ref[m], w_ref[:, m * dc:(m + 1) * dc], NT, preferred_element_type=F32)
            part = p if part is None else part + p
        acc_ref[...] = part
        _rms_bwd(tm, d, acc_ref, hs_ref, r_ref, gn_ref, dres_ref, dhs_ref, dhsb_ref, dgn_ref)

    row = pl.BlockSpec((tm, d), lambda i: (i, 0))
    return _call(
        body, comm=comm, name=name, grid=(nt,),
        in_specs=[pl.BlockSpec((ng, tm, dc), lambda i: (0, i, 0)), _once((d, ng * dc), lambda i: (0, 0)), row,
                  pl.BlockSpec((tm, 1), lambda i: (i, 0)), pl.BlockSpec((1, d), lambda i: (0, 0)), row],
        out_specs=[row, row, pl.BlockSpec((1, 1, d), lambda i: (i, 0, 0))],
        out_shape=[_sds((rows, d), F32), _sds((rows, d), BF16), _sds((nt, 1, d), F32)],
        scratch_shapes=[pltpu.VMEM((tm, d), F32)], compiler_params=_params(),
    )(du, w, hs, r, gn, dres)


def _win_bwd_dw(h, du, tk, name, comm=()):
    rows, d = h.shape
    ng, _, dc = du.shape
    steps = _KSteps(rows, tk)

    def body(*refs):
        ins, (dw_ref, db_ref, acc, accb) = refs[:-4], refs[-4:]
        k = pl.program_id(1)

        @pl.when(k == 0)
        def _():
            acc[...] = jnp.zeros_like(acc)
            accb[...] = jnp.zeros_like(accb)

        def add(h_ref, du_ref):
            duv = du_ref[0]
            acc[...] += lax.dot_general(h_ref[...], duv, TN, preferred_element_type=F32)
            accb[...] += jnp.sum(duv.astype(F32), axis=0, keepdims=True)

        steps.each(k, ins, add)

        @pl.when(k == steps.n - 1)
        def _():
            dw_ref[...] = acc[...].astype(BF16)
            db_ref[...] = accb[...]

    specs = steps.specs([lambda t, at: pl.BlockSpec((t, d), lambda m, k: (at(k), 0)),
                         lambda t, at: pl.BlockSpec((1, t, dc), lambda m, k: (m, at(k), 0))])
    return _call(
        body, comm=comm, name=name, grid=(ng, steps.n), in_specs=specs,
        out_specs=[pl.BlockSpec((d, dc), lambda m, k: (0, m)), pl.BlockSpec((1, dc), lambda m, k: (0, m))],
        out_shape=[_sds((d, ng * dc), BF16), _sds((1, ng * dc), F32)],
        scratch_shapes=[pltpu.VMEM((d, dc), F32), pltpu.VMEM((1, dc), F32)], compiler_params=_params(),
    )(*steps.args([h, du]))


def _layernorm_silu(zc, lg, lb):
    mu = jnp.mean(zc, axis=-1, keepdims=True)
    xc = zc - mu
    rstd = lax.rsqrt(jnp.mean(xc * xc, axis=-1, keepdims=True) + EPS)
    nrm = xc * rstd
    lin = nrm * lg + lb
    s = _sigmoid(lin)
    return nrm, rstd, lin, s


def _wout_fwd(zc, ysc, wout, hs, lg, lb, gn, rows, tm, name, comm=()):
    dc = zc.shape[1]
    d = hs.shape[1]

    def body(zc_ref, ysc_ref, w_ref, hs_ref, lg_ref, lb_ref, gn_ref, y_ref, hsn_ref, hn_ref, rn_ref):
        def mix(rs, _):
            _, _, lin, s = _layernorm_silu(zc_ref[rs, :], lg_ref[...], lb_ref[...])
            y_ref[rs, :dc] = ysc_ref[rs, :]
            y_ref[rs, dc:] = (lin * s).astype(BF16)

        _for_row_groups(tm, mix)
        hsn_ref[...] = jnp.dot(y_ref[...], w_ref[...], preferred_element_type=F32)

        def norm(rs, _):
            hsn = hs_ref[rs, :] + hsn_ref[rs, :]
            r = lax.rsqrt(jnp.mean(hsn * hsn, axis=-1, keepdims=True) + EPS)
            hsn_ref[rs, :] = hsn
            hn_ref[rs, :] = (hsn * r * gn_ref[...]).astype(BF16)
            rn_ref[rs, :] = r

        _for_row_groups(tm, norm)

    half = pl.BlockSpec((tm, dc), lambda i: (i, 0))
    row = pl.BlockSpec((tm, d), lambda i: (i, 0))
    vec_c = pl.BlockSpec((1, dc), lambda i: (0, 0))
    return _call(
        body, comm=comm, name=name, grid=(rows // tm,),
        in_specs=[half, half, _once((2 * dc, d), lambda i: (0, 0)), row, vec_c, vec_c,
                  pl.BlockSpec((1, d), lambda i: (0, 0))],
        out_specs=[pl.BlockSpec((tm, 2 * dc), lambda i: (i, 0)), row, row, pl.BlockSpec((tm, 1), lambda i: (i, 0))],
        out_shape=[_sds((rows, 2 * dc), BF16), _sds((rows, d), F32), _sds((rows, d), BF16), _sds((rows, 1), F32)],
        compiler_params=_params(),
    )(zc, ysc, wout, hs, lg, lb, gn)


def _wout_bwd_dy(do, wout, zc, lg, lb, tm, name, comm=()):
    rows, d = do.shape
    dc = zc.shape[1]
    nt = rows // tm

    def body(do_ref, w_ref, zc_ref, lg_ref, lb_ref, dysc_ref, dzc_ref, dlg_ref, dlb_ref, dy_ref):
        dy_ref[...] = lax.dot_general(do_ref[...], w_ref[...], NT, preferred_element_type=F32)
        dlb_ref[0] = jnp.zeros((1, dc), F32)
        dlg_ref[0] = jnp.zeros((1, dc), F32)

        def group(rs, _):
            dysc_ref[rs, :] = dy_ref[rs, :dc].astype(BF16)
            nrm, rstd, lin, s = _layernorm_silu(zc_ref[rs, :], lg_ref[...], lb_ref[...])
            dl = dy_ref[rs, dc:] * (s * (1.0 + lin * (1.0 - s)))
            dlb_ref[0] += jnp.sum(dl, axis=0, keepdims=True)
            dlg_ref[0] += jnp.sum(dl * nrm, axis=0, keepdims=True)
            dn = dl * lg_ref[...]
            dzc_ref[rs, :] = rstd * (dn - jnp.mean(dn, axis=-1, keepdims=True)
                                     - nrm * jnp.mean(dn * nrm, axis=-1, keepdims=True))

        _for_row_groups(tm, group)

    half = pl.BlockSpec((tm, dc), lambda i: (i, 0))
    vec_c = pl.BlockSpec((1, dc), lambda i: (0, 0))
    part = pl.BlockSpec((1, 1, dc), lambda i: (i, 0, 0))
    return _call(
        body, comm=comm, name=name, grid=(nt,),
        in_specs=[pl.BlockSpec((tm, d), lambda i: (i, 0)), _once((2 * dc, d), lambda i: (0, 0)), half, vec_c, vec_c],
        out_specs=[half, half, part, part],
        out_shape=[_sds((rows, dc), BF16), _sds((rows, dc), F32), _sds((nt, 1, dc), F32), _sds((nt, 1, dc), F32)],
        scratch_shapes=[pltpu.VMEM((tm, 2 * dc), F32)], compiler_params=_params(),
    )(do, wout, zc, lg, lb)


def _wout_bwd_dw(y, do, nb, tk, name, comm=()):
    rows, k2 = y.shape
    d = do.shape[1]
    nk = rows // tk

    def body(y_ref, do_ref, dw_ref, acc):
        k = pl.program_id(0)

        @pl.when(k == 0)
        def _():
            acc[...] = jnp.zeros_like(acc)

        acc[...] += lax.dot_general(y_ref[...], do_ref[...], TN, preferred_element_type=F32)

        @pl.when(k == nk - 1)
        def _():
            dw_ref[...] = acc[...].astype(BF16)

    return _call(
        body, comm=comm, name=name, grid=(nk,),
        in_specs=[pl.BlockSpec((tk, k2), lambda k: (k, 0)), pl.BlockSpec((tk, d), lambda k: (k, 0))],
        out_specs=_once((k2, d), lambda k: (0, 0)), out_shape=_sds((k2, d), BF16),
        scratch_shapes=[pltpu.VMEM((k2, d), F32)], compiler_params=_params(),
    )(y, do).reshape(nb, k2 // nb, d)


def _windows(win, n_res):
    length = win.shape[0]
    return [win if r == 0 else pltpu.roll(win, length - r, 0) for r in range(n_res)]


def _taps(src_ref, start, offsets, ch):
    span = -(-(max(offsets) + ch) // SUBLANES) * SUBLANES
    win = src_ref[pl.ds(start, span), :]
    shifted = _windows(win, min(SUBLANES, max(offsets) + 1))
    return [shifted[o % SUBLANES][(o // SUBLANES) * SUBLANES:(o // SUBLANES) * SUBLANES + ch] for o in offsets]


def _rows8(v):
    return jnp.sum(v.reshape(v.shape[0] // SUBLANES, SUBLANES, v.shape[1]), axis=0)


def _conv_geometry(n_seq, n_meta):
    base = CONV_PAD + n_meta
    off_cf = base - (CF_WIDTH - 1)
    off_sc = base - (SC_WIDTH - 1)
    logical = -(-(n_meta + n_seq) // CONV_CH) * CONV_CH
    return base, off_cf, off_sc, logical


def _fill_conv_inputs(c_ref, v_ref, a_ref, g_ref, scv, sz, n_seq, n_meta):
    base = CONV_PAD + n_meta
    cb = scv.shape[1]
    scv[0:CONV_PAD, :] = jnp.zeros((CONV_PAD, cb), F32)
    sz[0:CONV_PAD, :] = jnp.zeros((CONV_PAD, cb), F32)

    def put(src, dst, n):
        cv = c_ref[src, :].astype(F32) * v_ref[src, :].astype(F32)
        scv[dst, :] = cv
        sz[dst, :] = a_ref[src, :].astype(F32) * _sigmoid(g_ref[src, :].astype(F32))

    put(pl.ds(n_seq, n_meta), pl.ds(CONV_PAD, n_meta), n_meta)

    def chunk(i, carry):
        t0 = pl.multiple_of(i * CONV_CH, CONV_CH)
        put(pl.ds(t0, CONV_CH), pl.ds(base + t0, CONV_CH), CONV_CH)
        return carry

    lax.fori_loop(0, n_seq // CONV_CH, chunk, 0)


def _conv_fwd(u, wsc, wcf, cbias, n_seq, n_meta, name, comm=()):
    rows = u.shape[0]
    nb, _, cb = wsc.shape
    dc = nb * cb
    base, off_cf, off_sc, _ = _conv_geometry(n_seq, n_meta)
    a_cf, a_sc = off_cf // SUBLANES * SUBLANES, off_sc // SUBLANES * SUBLANES
    ch = CONV_CH

    def body(b_ref, c_ref, v_ref, a_ref, g_ref, wsc_ref, wcf_ref, cb_ref, ysc_ref, zc_ref, scv, sz):
        _fill_conv_inputs(c_ref, v_ref, a_ref, g_ref, scv, sz, n_seq, n_meta)
        w3, w31, bias = wsc_ref[0], wcf_ref[0], cb_ref[...]

        def chunk(i, carry):
            t0 = pl.multiple_of(i * ch, ch)
            acc = jnp.zeros((ch, cb), F32)
            for k, win in enumerate(_taps(sz, t0 + a_cf, [off_cf - a_cf + k for k in range(CF_WIDTH)], ch)):
                acc = acc + win * w31[k:k + 1, :]
            zc_ref[pl.ds(t0, ch), :] = acc + bias
            s = jnp.zeros((ch, cb), F32)
            for k, win in enumerate(_taps(scv, t0 + a_sc, [off_sc - a_sc + k for k in range(SC_WIDTH)], ch)):
                s = s + win * w3[k:k + 1, :]
            ysc_ref[pl.ds(t0, ch), :] = (b_ref[pl.ds(t0, ch), :].astype(F32) * s).astype(BF16)
            return carry

        lax.fori_loop(0, n_seq // ch, chunk, 0)
        ysc_ref[n_seq:rows, :] = jnp.zeros((rows - n_seq, cb), BF16)
        zc_ref[n_seq:rows, :] = jnp.zeros((rows - n_seq, cb), F32)

    ucol = [pl.BlockSpec((rows, cb), functools.partial(lambda m, j: (0, m * nb + j), m)) for m in range(5)]
    blk = pl.BlockSpec((rows, cb), lambda j: (0, j))
    return _call(
        body, comm=comm, name=name, grid=(nb,),
        in_specs=ucol + [pl.BlockSpec((1, SC_WIDTH, cb), lambda j: (j, 0, 0)),
                         pl.BlockSpec((1, CF_WIDTH, cb), lambda j: (j, 0, 0)), pl.BlockSpec((1, cb), lambda j: (0, j))],
        out_specs=[blk, blk], out_shape=[_sds((rows, dc), BF16), _sds((rows, dc), F32)],
        scratch_shapes=[pltpu.VMEM((base + n_seq, cb), F32)] * 2, compiler_params=_params(),
    )(u, u, u, u, u, wsc, wcf, cbias)


def _conv_bwd(u, dysc, dzc, wsc, wcf, n_seq, n_meta, name, comm=()):
    rows = u.shape[0]
    nb, _, cb = wsc.shape
    dc = nb * cb
    base, off_cf, off_sc, logical = _conv_geometry(n_seq, n_meta)
    a_cf, a_sc = off_cf // SUBLANES * SUBLANES, off_sc // SUBLANES * SUBLANES
    ch = CONV_CH
    tail = CONV_PAD

    def body(b_ref, c_ref, v_ref, a_ref, g_ref, dysc_ref, dzc_ref, wsc_ref, wcf_ref,
             du_ref, dwsc_ref, dwcf_ref, dcb_ref,
             scv, sz, sds_, sdz, dlcv, dlz, accsc, acccf, accb):
        _fill_conv_inputs(c_ref, v_ref, a_ref, g_ref, scv, sz, n_seq, n_meta)
        w3, w31 = wsc_ref[0], wcf_ref[0]
        dub_ref, duc_ref, duv_ref, dua_ref, dug_ref = (du_ref.at[m] for m in range(5))
        sds_[0:n_meta, :] = jnp.zeros((n_meta, cb), F32)
        sdz[0:n_meta, :] = jnp.zeros((n_meta, cb), F32)
        behind = logical + tail - (n_meta + n_seq)
        sds_[n_meta + n_seq:logical + tail, :] = jnp.zeros((behind, cb), F32)
        sdz[n_meta + n_seq:logical + tail, :] = jnp.zeros((behind, cb), F32)
        accsc[...] = jnp.zeros_like(accsc)
        acccf[...] = jnp.zeros_like(acccf)
        accb[...] = jnp.zeros_like(accb)

        def forward_chunk(i, carry):
            t0 = pl.multiple_of(i * ch, ch)
            rws = pl.ds(t0, ch)
            dy = dysc_ref[rws, :].astype(F32)
            ds = dy * b_ref[rws, :].astype(F32)
            dz = dzc_ref[rws, :]
            sds_[pl.ds(n_meta + t0, ch), :] = ds
            sdz[pl.ds(n_meta + t0, ch), :] = dz
            s = jnp.zeros((ch, cb), F32)
            for k, win in enumerate(_taps(scv, t0 + a_sc, [off_sc - a_sc + k for k in range(SC_WIDTH)], ch)):
                s = s + win * w3[k:k + 1, :]
                accsc[k * SUBLANES:(k + 1) * SUBLANES, :] += _rows8(ds * win)
            dub_ref[rws, :] = (dy * s).astype(BF16)
            for k, win in enumerate(_taps(sz, t0 + a_cf, [off_cf - a_cf + k for k in range(CF_WIDTH)], ch)):
                acccf[k * SUBLANES:(k + 1) * SUBLANES, :] += _rows8(dz * win)
            accb[...] += _rows8(dz)
            return carry

        lax.fori_loop(0, n_seq // ch, forward_chunk, 0)

        def backward_chunk(i, carry):
            p0 = pl.multiple_of(i * ch, ch)
            dcv = jnp.zeros((ch, cb), F32)
            for k, win in enumerate(_taps(sds_, p0, [SC_WIDTH - 1 - k for k in range(SC_WIDTH)], ch)):
                dcv = dcv + win * w3[k:k + 1, :]
            dlcv[pl.ds(p0, ch), :] = dcv
            dzi = jnp.zeros((ch, cb), F32)
            for k, win in enumerate(_taps(sdz, p0, [CF_WIDTH - 1 - k for k in range(CF_WIDTH)], ch)):
                dzi = dzi + win * w31[k:k + 1, :]
            dlz[pl.ds(p0, ch), :] = dzi
            return carry

        lax.fori_loop(0, logical // ch, backward_chunk, 0)

        def gates(phys, logi):
            dcv, dzi = dlcv[logi, :], dlz[logi, :]
            duc_ref[phys, :] = (dcv * v_ref[phys, :].astype(F32)).astype(BF16)
            duv_ref[phys, :] = (dcv * c_ref[phys, :].astype(F32)).astype(BF16)
            s = _sigmoid(g_ref[phys, :].astype(F32))
            dua_ref[phys, :] = (dzi * s).astype(BF16)
            dug_ref[phys, :] = (dzi * a_ref[phys, :].astype(F32) * s * (1.0 - s)).astype(BF16)

        def gate_chunk(i, carry):
            t0 = pl.multiple_of(i * ch, ch)
            gates(pl.ds(t0, ch), pl.ds(n_meta + t0, ch))
            return carry

        lax.fori_loop(0, n_seq // ch, gate_chunk, 0)
        gates(pl.ds(n_seq, n_meta), pl.ds(0, n_meta))
        dub_ref[n_seq:rows, :] = jnp.zeros((rows - n_seq, cb), BF16)
        pad0 = n_seq + n_meta
        for ref in (duc_ref, duv_ref, dua_ref, dug_ref):
            ref[pad0:rows, :] = jnp.zeros((rows - pad0, cb), BF16)
        dwsc_ref[0] = jnp.sum(accsc[...].reshape(SC_WIDTH, SUBLANES, cb), axis=1)
        dwcf_ref[0] = jnp.sum(acccf[...].reshape(CF_WIDTH, SUBLANES, cb), axis=1)
        dcb_ref[...] = jnp.sum(accb[...], axis=0, keepdims=True)

    ucol = [pl.BlockSpec((rows, cb), functools.partial(lambda m, j: (0, m * nb + j), m)) for m in range(5)]
    blk = pl.BlockSpec((dysc.shape[0], cb), lambda j: (0, j))
    wsc_spec = pl.BlockSpec((1, SC_WIDTH, cb), lambda j: (j, 0, 0))
    wcf_spec = pl.BlockSpec((1, CF_WIDTH, cb), lambda j: (j, 0, 0))
    outs = _call(
        body, comm=comm, name=name, grid=(nb,),
        in_specs=ucol + [blk, blk, wsc_spec, wcf_spec],
        out_specs=[pl.BlockSpec((5, rows, cb), lambda j: (0, 0, j)), wsc_spec, wcf_spec,
                   pl.BlockSpec((1, cb), lambda j: (0, j))],
        out_shape=[_sds((5, rows, dc), BF16), _sds((nb, SC_WIDTH, cb), F32), _sds((nb, CF_WIDTH, cb), F32),
                   _sds((1, dc), F32)],
        scratch_shapes=[pltpu.VMEM((base + n_seq, cb), F32)] * 2 + [pltpu.VMEM((logical + tail, cb), F32)] * 2
        + [pltpu.VMEM((logical, cb), F32)] * 2
        + [pltpu.VMEM((SC_WIDTH * SUBLANES, cb), F32), pltpu.VMEM((CF_WIDTH * SUBLANES, cb), F32),
           pltpu.VMEM((SUBLANES, cb), F32)],
        compiler_params=_params(),
    )(u, u, u, u, u, dysc, dzc, wsc, wcf)
    return outs


def _row_tile(rows, cols):
    return rows // 4 if rows % 64 == 0 and rows * cols >= (1 << 18) else rows


def _pair_sum(grad, sib, idx, name, comm=()):
    _, rows, cols = grad.shape
    tr = rows

    def body(idx_ref, g_ref, s_ref, o_ref):
        o_ref[0] = (g_ref[0].astype(F32) + s_ref[0].astype(F32)).astype(o_ref.dtype)

    return _call(
        body, name=name,
        grid_spec=pltpu.PrefetchScalarGridSpec(
            num_scalar_prefetch=1, grid=(4, rows // tr),
            in_specs=[pl.BlockSpec((1, tr, cols), lambda k, i, idx_ref: (idx_ref[k], i, 0)),
                      pl.BlockSpec((1, tr, cols), lambda k, i, idx_ref: (idx_ref[4 + k], i, 0))],
            out_specs=pl.BlockSpec((1, tr, cols), lambda k, i, idx_ref: (k, i, 0))),
        out_shape=_sds((4, rows, cols), grad.dtype), compiler_params=_params(),
    )(idx, grad, sib)


def _adamw_math(w, g, m, v):
    m = ADAM_B1 * m + (1.0 - ADAM_B1) * g
    v = ADAM_B2 * v + (1.0 - ADAM_B2) * (g * g)
    m_hat = m / (1.0 - ADAM_B1 ** ADAM_STEP)
    v_hat = v / (1.0 - ADAM_B2 ** ADAM_STEP)
    delta = -ADAM_LR * (m_hat / (jnp.sqrt(v_hat) + ADAM_EPS) + ADAM_WD * w)
    return delta, m, v


def _adamw_sharded(own, got, w, m, v, name, comm=()):
    rows, cols = w.shape
    tr = _row_tile(rows, cols)

    def body(own_ref, g0_ref, g1_ref, g2_ref, w_ref, m_ref, v_ref, g_ref, d_ref, nm_ref, nv_ref):
        g = own_ref[0].astype(F32) + g0_ref[0].astype(F32) + g1_ref[0].astype(F32) + g2_ref[0].astype(F32)
        delta, nm, nv = _adamw_math(w_ref[...], g, m_ref[...], v_ref[...])
        g_ref[...] = g
        d_ref[...] = delta
        nm_ref[...] = nm
        nv_ref[...] = nv

    flat = pl.BlockSpec((tr, cols), lambda i: (i, 0))
    slot = [pl.BlockSpec((1, tr, cols), functools.partial(lambda k, i: (k, i, 0), k)) for k in range(3)]
    return _call(
        body, comm=comm, name=name, grid=(rows // tr,),
        in_specs=[slot[0]] + slot + [flat] * 3, out_specs=[flat] * 4, out_shape=[_sds((rows, cols), F32)] * 4,
        compiler_params=_params(),
    )(own, got, got, got, w, m, v)


def _adamw_replicated(gathered, segs, ws, ms, vs, loss_scale, name, comm=()):
    n = len(ws)

    def body(*refs):
        gat = refs[0]
        w_refs, m_refs, v_refs = refs[1:1 + n], refs[1 + n:1 + 2 * n], refs[1 + 2 * n:1 + 3 * n]
        outs = refs[1 + 3 * n:]

        def total(off, width):
            s = gat[0, :, off:off + width]
            for k in range(1, N_DEV):
                s = s + gat[k, :, off:off + width]
            return s

        outs[0][...] = loss_scale * total(segs[n][0], segs[n][1])
        for p in range(n):
            g = total(*segs[p])
            delta, nm, nv = _adamw_math(w_refs[p][...], g, m_refs[p][...], v_refs[p][...])
            for q, val in enumerate((g, delta, nm, nv)):
                outs[1 + 4 * p + q][...] = val
        for e, seg in enumerate(segs[n + 1:]):
            outs[1 + 4 * n + e][...] = total(*seg)

    return _call(
        body, name=name,
        out_shape=[_sds((1, segs[n][1]), F32)] + [_sds(w.shape, F32) for w in ws for _ in range(4)]
        + [_sds((1, seg[1]), F32) for seg in segs[n + 1:]],
        compiler_params=_params(),
    )(gathered, *ws, *ms, *vs)


def _adamw_plain(g, w, m, v, name):
    def body(g_ref, w_ref, m_ref, v_ref, d_ref, nm_ref, nv_ref):
        d_ref[...], nm_ref[...], nv_ref[...] = _adamw_math(w_ref[...], g_ref[...], m_ref[...], v_ref[...])

    return list(_call(body, name=name, out_shape=[_sds(w.shape, F32)] * 3)(g, w, m, v))


REPLICATED = ("ffn1_norm", "mix_norm", "b_in", "conv_cf_b", "ln_cf_g", "ln_cf_b", "ffn2_norm", "final_norm")
SHARDED = ("meta_tokens", "ffn1_w_gate", "ffn1_w_up", "ffn1_w_down", "w_in", "conv_sc_w", "conv_cf_w", "w_out",
           "ffn2_w_gate", "ffn2_w_up", "ffn2_w_down")
WEIGHTS = ("meta_tokens", "ffn1_norm", "ffn1_w_gate", "ffn1_w_up", "ffn1_w_down", "mix_norm", "w_in", "b_in",
           "conv_sc_w", "conv_cf_w", "conv_cf_b", "ln_cf_g", "ln_cf_b", "w_out", "ffn2_norm", "ffn2_w_gate",
           "ffn2_w_up", "ffn2_w_down", "final_norm")


TRANSPOSED = ("ffn1_w_gate", "ffn1_w_up", "ffn2_w_gate", "ffn2_w_up")


def _blocks2d(k, a):
    a = a.reshape(a.shape[-2:]) if a.ndim >= 2 else a.reshape(1, -1)
    return a.T if k in TRANSPOSED else a


def _step(x, tgt, w, m, v):
    n_seq, d = x.shape[1], x.shape[2]
    n_meta = w["meta_tokens"].shape[0]
    rows = -(-(n_seq + n_meta) // ROW_ALIGN) * ROW_ALIGN
    tm = rows // N_ROW_TILES
    cb = w["conv_sc_w"].shape[-1]
    dc = cb * N_DEV
    w2 = {k: _blocks2d(k, a) for k, a in w.items()}
    m2 = {k: _blocks2d(k, a) for k, a in m.items()}
    v2 = {k: _blocks2d(k, a) for k, a in v.items()}

    def as_given(k, r):
        return (r.T if k in TRANSPOSED else r).reshape(w[k].shape)

    def cast(k):
        return w2[k].astype(BF16)

    full = dict(zip(("ffn1_w_gate", "meta_tokens", "conv_sc_w", "conv_cf_w"), _exchange_alone(
        _gather_all([cast("ffn1_w_gate"), w2["meta_tokens"], w2["conv_sc_w"], w2["conv_cf_w"]]), "gather_gate1")))
    meta = jnp.transpose(full["meta_tokens"], (1, 0, 2)).reshape(n_meta, d)
    hs0 = jnp.concatenate([x[0], meta, jnp.zeros((rows - n_seq - n_meta, d), F32)], axis=0)

    tx = n_seq // N_ROW_TILES
    h1, r1 = _rms_fwd(hs0, w2["ffn1_norm"], tm, "rms_in")
    up1 = _gather_all([cast("ffn1_w_up")])
    g1 = _ffn_proj(h1, full["ffn1_w_gate"], tm, "ffn1_gate", comm=[up1])
    down1 = _gather_all([cast("ffn1_w_down")])
    u1 = _ffn_proj(h1, up1.results[0], tm, "ffn1_up", comm=[down1])
    w_in_all, w_out1 = _gather_all([cast("w_in")]), _gather_direct([cast("w_out")])
    hs1, h2, r2 = _ffn_down_norm(g1, u1, down1.results[0], hs0, w2["mix_norm"], rows // DOWN_ROW_TILES, "ffn1_down",
                                 comm=[w_in_all, w_out1])
    win = jnp.transpose(w_in_all.results[0], (1, 0, 2)).reshape(d, -1)
    w_out2, gate2 = _gather_forward(w_out1.results), _gather_direct([cast("ffn2_w_gate")])
    u = _win_fwd(h2, win, w2["b_in"], dc, tm, "mix_in", comm=[w_out2, gate2])
    wout = w_out2.results[0].reshape(-1, d)
    gate2p = _gather_forward(gate2.results)
    ysc, zc = _conv_fwd(u, full["conv_sc_w"], full["conv_cf_w"], w2["conv_cf_b"], n_seq, n_meta, "conv_fwd", comm=[gate2p])
    up2 = _gather_all([cast("ffn2_w_up")])
    y, hs2, h3, r3 = _wout_fwd(zc, ysc, wout, hs1, w2["ln_cf_g"], w2["ln_cf_b"], w2["ffn2_norm"], n_seq, tx, "mix_out",
                               comm=[up2])
    down2 = _gather_all([cast("ffn2_w_down")])
    full.update(ffn1_w_up=up1.results[0], ffn1_w_down=down1.results[0], ffn2_w_gate=gate2p.results[0],
                ffn2_w_up=up2.results[0])
    g2, u2 = _ffn_gu(h3, full["ffn2_w_gate"], full["ffn2_w_up"], tx, "ffn2_gu", comm=[down2])
    full["ffn2_w_down"] = down2.results[0]
    dhs3, dhs3b, loss_p, dgf_p = _ffn_down_loss(
        g2, u2, full["ffn2_w_down"], hs2, w2["final_norm"], tgt[0], n_seq, n_seq // DOWN_ROW_TILES, "ffn2_down_loss")

    xi, yi, ci = lax.axis_index("x"), lax.axis_index("y"), lax.axis_index("c")
    chip_of = [2 * xi + yi, 2 * (1 - xi) + yi, 2 * xi + (1 - yi), 2 * (1 - xi) + (1 - yi)]
    idx = jnp.stack([2 * ch + ci for ch in chip_of] + chip_of).astype(jnp.int32)
    out = {}

    def to_pairs(parts):
        return _pair_exchange([p.reshape((4, 2) + p.shape[1:]) for p in parts])

    def pair_sums(names, parts, pairs):
        return [_pair_sum(p, s, idx, "pair_sum_" + k) for k, p, s in zip(names, parts, pairs.results)]

    def update(names, sums, chips):
        for k, own, got in zip(names, sums, chips.results):
            res = _adamw_sharded(own, got, w2[k], m2[k], v2[k], "adamw_" + k)
            out[k] = [as_given(k, r) for r in res]

    tk = min(2 * ROW_ALIGN, n_seq)
    th_x, th_r = n_seq // DOWN_ROW_TILES, rows // DOWN_ROW_TILES

    dwd2 = _ffn_bwd_dw_down(g2, u2, dhs3b, tk, "ffn2_bwd_dw_down")
    p_d2 = to_pairs([dwd2])
    dg2, du2 = _ffn_bwd_da(dhs3b, full["ffn2_w_down"], g2, u2, tx, "ffn2_bwd_da", comm=[p_d2])
    s_d2 = pair_sums(["ffn2_w_down"], [dwd2], p_d2)
    c_d2 = _chip_exchange(s_d2)
    gu2 = _ffn_bwd_dw_gu(h3, dg2, du2, tk, "ffn2_bwd_dw_gu", comm=[c_d2])
    update(["ffn2_w_down"], s_d2, c_d2)
    p_gu2 = to_pairs(gu2)
    half2 = _ffn_bwd_dh_gate(dg2, full["ffn2_w_gate"], th_x, "ffn2_bwd_dh_gate", comm=[p_gu2])
    s_gu2 = pair_sums(["ffn2_w_gate", "ffn2_w_up"], gu2, p_gu2)
    c_g2, c_u2 = _chip_exchange(s_gu2[:1]), _chip_exchange(s_gu2[1:])
    dhs2, dhs2b, dn3_p = _ffn_bwd_dh(du2, full["ffn2_w_up"], half2, hs2, r3, w2["ffn2_norm"], dhs3, th_x, "ffn2_bwd_dh",
                                     comm=[c_g2], zeros=jnp.zeros((rows, d), F32))
    dysc, dzc, dlg_p, dlb_p = _wout_bwd_dy(dhs2b, wout, zc, w2["ln_cf_g"], w2["ln_cf_b"], tx, "mix_out_bwd_dy",
                                           comm=[c_u2])
    update(["ffn2_w_gate"], s_gu2[:1], c_g2)
    update(["ffn2_w_up"], s_gu2[1:], c_u2)

    dwout = _wout_bwd_dw(y, dhs2b, N_DEV, tk, "mix_out_bwd_dw")
    p_wo = to_pairs([dwout])
    du, dcsw, dccw, dcb = _conv_bwd(u, dysc, dzc, full["conv_sc_w"], full["conv_cf_w"], n_seq, n_meta, "conv_bwd",
                                    comm=[p_wo])
    s_wo = pair_sums(["w_out"], [dwout], p_wo)
    c_wo = _chip_exchange(s_wo)
    dhs1, dhs1b, dn2_p = _win_bwd_dh(du, win, hs1, r2, w2["mix_norm"], dhs2, th_r, "mix_in_bwd_dh", comm=[c_wo])
    update(["w_out"], s_wo, c_wo)
    dwin, dbin = _win_bwd_dw(h2, du, tk, "mix_in_bwd_dw")
    mixer = ("w_in", "conv_sc_w", "conv_cf_w")
    gradsm = [jnp.transpose(dwin.reshape(d, N_DEV, -1), (1, 0, 2)), dcsw, dccw]
    p_m = to_pairs(gradsm)
    dwd1 = _ffn_bwd_dw_down(g1, u1, dhs1b, tk, "ffn1_bwd_dw_down", comm=[p_m])
    s_m = pair_sums(mixer, gradsm, p_m)
    c_m, p_d1 = _chip_exchange(s_m), to_pairs([dwd1])
    dg1, du1 = _ffn_bwd_da(dhs1b, full["ffn1_w_down"], g1, u1, tm, "ffn1_bwd_da", comm=[c_m, p_d1])
    update(mixer, s_m, c_m)
    s_d1 = pair_sums(["ffn1_w_down"], [dwd1], p_d1)
    c_d1 = _chip_exchange(s_d1)
    gu1 = _ffn_bwd_dw_gu(h1, dg1, du1, tk, "ffn1_bwd_dw_gu", comm=[c_d1])
    update(["ffn1_w_down"], s_d1, c_d1)
    p_gu1 = to_pairs(gu1)
    _exchange_alone(p_gu1, "reduce_pair_ffn1")
    s_gu1 = pair_sums(["ffn1_w_gate", "ffn1_w_up"], gu1, p_gu1)
    c_g1, c_u1 = _chip_exchange(s_gu1[:1]), _chip_exchange(s_gu1[1:])
    half1 = _ffn_bwd_dh_gate(dg1, full["ffn1_w_gate"], th_r, "ffn1_bwd_dh_gate", comm=[c_g1])
    dhs0, _, dn1_p = _ffn_bwd_dh(du1, full["ffn1_w_up"], half1, hs0, r1, w2["ffn1_norm"], dhs1, th_r, "ffn1_bwd_dh",
                                 comm=[c_u1])
    update(["ffn1_w_gate"], s_gu1[:1], c_g1)
    update(["ffn1_w_up"], s_gu1[1:], c_u1)
    grad_x = dhs0[:n_seq][None]

    partial = {
        "ffn1_norm": dn1_p.sum(0), "mix_norm": dn2_p.sum(0), "b_in": dbin, "conv_cf_b": dcb,
        "ln_cf_g": dlg_p.sum(0), "ln_cf_b": dlb_p.sum(0), "ffn2_norm": dn3_p.sum(0), "final_norm": dgf_p.sum(0),
    }
    loss_seg = jnp.pad(loss_p.sum((0, 2)).reshape(1, 1), ((0, 0), (0, 127)))
    pieces = [partial[k] for k in REPLICATED] + [loss_seg, dhs0[n_seq:n_seq + n_meta].reshape(1, n_meta * d)]
    segs, off = [], 0
    for p in pieces:
        segs.append((off, p.shape[1]))
        off += p.shape[1]
    rows8 = _gather_rows(jnp.concatenate(pieces, axis=1), "gather_small")
    res = _adamw_replicated(rows8, segs, [w2[k] for k in REPLICATED], [m2[k] for k in REPLICATED],
                            [v2[k] for k in REPLICATED], 0.5 / d, "adamw_replicated")
    loss = res[0][0, 0]
    for p, k in enumerate(REPLICATED):
        out[k] = [r.reshape(w[k].shape) for r in res[1 + 4 * p:5 + 4 * p]]
    ob = w2["meta_tokens"].shape[1]
    gmeta = lax.dynamic_slice_in_dim(res[-1].reshape(n_meta, d), (4 * xi + 2 * yi + ci) * ob, ob, axis=1)
    out["meta_tokens"] = [gmeta] + _adamw_plain(gmeta, w2["meta_tokens"], m2["meta_tokens"], v2["meta_tokens"], "adamw_meta_tokens")

    return (loss, grad_x, *[out[k][0] for k in WEIGHTS], *[out[k][1] for k in WEIGHTS],
            *[out[k][2] for k in WEIGHTS], *[out[k][3] for k in WEIGHTS])


def kernel(x, meta_tokens, ffn1_norm, ffn1_w_gate, ffn1_w_up, ffn1_w_down, mix_norm, w_in, b_in, conv_sc_w, conv_cf_w, conv_cf_b, ln_cf_g, ln_cf_b, w_out, ffn2_norm, ffn2_w_gate, ffn2_w_up, ffn2_w_down, final_norm, loss_target, m_meta_tokens, m_ffn1_norm, m_ffn1_w_gate, m_ffn1_w_up, m_ffn1_w_down, m_mix_norm, m_w_in, m_b_in, m_conv_sc_w, m_conv_cf_w, m_conv_cf_b, m_ln_cf_g, m_ln_cf_b, m_w_out, m_ffn2_norm, m_ffn2_w_gate, m_ffn2_w_up, m_ffn2_w_down, m_final_norm, v_meta_tokens, v_ffn1_norm, v_ffn1_w_gate, v_ffn1_w_up, v_ffn1_w_down, v_mix_norm, v_w_in, v_b_in, v_conv_sc_w, v_conv_cf_w, v_conv_cf_b, v_ln_cf_g, v_ln_cf_b, v_w_out, v_ffn2_norm, v_ffn2_w_gate, v_ffn2_w_up, v_ffn2_w_down, v_final_norm):
    given = dict(locals())
    w = {k: given[k] for k in WEIGHTS}
    m = {k: given["m_" + k] for k in WEIGHTS}
    v = {k: given["v_" + k] for k in WEIGHTS}
    return _step(x, loss_target, w, m, v)
```

```python
import functools

import jax
import jax.numpy as jnp
from jax import lax
from jax.experimental import pallas as pl
from jax.experimental.pallas import tpu as pltpu

F32 = jnp.float32
BF16 = jnp.bfloat16
EPS = 1e-6
FFN_RES_SCALE = 0.5
SC_WIDTH = 3
CF_WIDTH = 31
ADAM_LR = 0.001
ADAM_B1 = 0.9
ADAM_B2 = 0.999
ADAM_EPS = 1e-08
ADAM_WD = 0.01
ADAM_STEP = 10

N_DEV = 8
N_ROW_TILES = 8
ROW_ALIGN = 256
CONV_PAD = 32
CONV_CH = 128
SUB_ROWS = 32
DOWN_ROW_TILES = 16
GATHER_PARTS = 2
SUBLANES = 8
BF16_ROWS = 16
VMEM_LIMIT = 56 * 1024 * 1024
MESH = pl.DeviceIdType.MESH
ANY = pl.BlockSpec(memory_space=pl.ANY)

NT = (((1,), (1,)), ((), ()))
TN = (((0,), (0,)), ((), ()))


def _pallas(body, **kw):
    return pl.pallas_call(body, **kw)


class _Exchange:
    def __init__(self, inputs, out_shapes, sem_shapes, start, finish, aliases=None):
        self.inputs, self.out_shapes, self.sem_shapes = list(inputs), list(out_shapes), list(sem_shapes)
        self.start, self.finish, self.aliases = start, finish, dict(aliases or {})
        self.results = None


def _call(body, comm=(), **kw):
    if not comm:
        return _pallas(body, **kw)
    grid = kw.pop("grid")
    in_specs = list(kw.pop("in_specs"))
    out_specs, out_shape = kw.pop("out_specs"), kw.pop("out_shape")
    scratch = list(kw.pop("scratch_shapes", []))
    single = not isinstance(out_shape, (list, tuple))
    out_specs, out_shape = ([out_specs], [out_shape]) if single else (list(out_specs), list(out_shape))
    n_in, n_out, n_scr = len(in_specs), len(out_shape), len(scratch)
    c_in = [a for job in comm for a in job.inputs]
    c_out = [s for job in comm for s in job.out_shapes]
    c_sem = [s for job in comm for s in job.sem_shapes]
    aliases, i0, o0 = dict(kw.pop("input_output_aliases", {})), n_in, n_out
    for job in comm:
        aliases.update({i0 + i: o0 + o for i, o in job.aliases.items()})
        i0, o0 = i0 + len(job.inputs), o0 + len(job.out_shapes)

    def hosted(*refs):
        pos = [0]

        def take(n):
            pos[0] += n
            return refs[pos[0] - n:pos[0]]

        ins, cins, outs, couts, scr, sems = take(n_in), take(len(c_in)), take(n_out), take(len(c_out)), take(n_scr), take(len(c_sem))
        ids = [pl.program_id(k) for k in range(len(grid))]
        first = functools.reduce(jnp.logical_and, [i == 0 for i in ids])
        last = functools.reduce(jnp.logical_and, [i == g - 1 for i, g in zip(ids, grid)])

        def each(phase):
            i, o, s = 0, 0, 0
            for job in comm:
                ni, no, ns = len(job.inputs), len(job.out_shapes), len(job.sem_shapes)
                getattr(job, phase)(cins[i:i + ni], couts[o:o + no], sems[s:s + ns])
                i, o, s = i + ni, o + no, s + ns

        @pl.when(first)
        def _():
            each("start")

        body(*ins, *outs, *scr)

        @pl.when(last)
        def _():
            each("finish")

    call = _pallas(
        hosted, grid=grid, in_specs=in_specs + [ANY] * len(c_in), out_specs=out_specs + [ANY] * len(c_out),
        out_shape=out_shape + c_out, scratch_shapes=scratch + c_sem, input_output_aliases=aliases, **kw)

    def run(*args):
        res = call(*args, *c_in)
        o = n_out
        for job in comm:
            job.results = list(res[o:o + len(job.out_shapes)])
            o += len(job.out_shapes)
        return res[0] if single else list(res[:n_out])

    return run


def _exchange_alone(job, name, comm=()):
    n_in, n_out = len(job.inputs), len(job.out_shapes)

    def body(*refs):
        ins, outs, sems = refs[:n_in], refs[n_in:n_in + n_out], refs[n_in + n_out:]
        job.start(ins, outs, sems)
        job.finish(ins, outs, sems)

    res = _pallas(body, name=name, in_specs=[ANY] * n_in, out_specs=[ANY] * n_out, out_shape=job.out_shapes,
                  scratch_shapes=job.sem_shapes, input_output_aliases=job.aliases)(*job.inputs)
    job.results = list(res)
    return job.results


def _params(**kw):
    return pltpu.CompilerParams(vmem_limit_bytes=VMEM_LIMIT, **kw)


def _sigmoid(x):
    return 0.5 * jnp.tanh(0.5 * x) + 0.5


def _sds(shape, dtype):
    return jax.ShapeDtypeStruct(shape, dtype)


def _place():
    x, y, c = lax.axis_index("x"), lax.axis_index("y"), lax.axis_index("c")
    chips = [(1 - x, y), (x, 1 - y), (1 - x, 1 - y)]
    return x, y, c, chips


def _slot(ref, p):
    return ref.at[4 * p[0] + 2 * p[1] + p[2]]


def _remote(src, dst, send_sem, recv_sem, to):
    return pltpu.make_async_remote_copy(src_ref=src, dst_ref=dst, send_sem=send_sem, recv_sem=recv_sem,
                                        device_id=to, device_id_type=MESH)


def _gather_direct(arrs):
    n = len(arrs)

    def copies(ins, outs, sems):
        send_sems, recv_sems, local_sems = sems
        x, y, c, chips = _place()
        me = (x, y, c)
        peers = [(x, y, 1 - c)] + [(*chip, c) for chip in chips]
        local = [pltpu.make_async_copy(ins[a], _slot(outs[a], me), local_sems.at[a]) for a in range(n)]
        sends = [_remote(ins[a], _slot(outs[a], me), send_sems.at[a, k], recv_sems.at[a, k], peer)
                 for a in range(n) for k, peer in enumerate(peers)]
        arrivals = [_remote(ins[a], _slot(outs[a], peer), send_sems.at[a, k], recv_sems.at[a, k], peer)
                    for a in range(n) for k, peer in enumerate(peers)]
        return local, sends, arrivals

    def start(ins, outs, sems):
        local, sends, _ = copies(ins, outs, sems)
        for cp in local + sends:
            cp.start()

    def finish(ins, outs, sems):
        local, sends, arrivals = copies(ins, outs, sems)
        for cp in arrivals:
            cp.wait_recv()
        for cp in sends:
            cp.wait_send()
        for cp in local:
            cp.wait()

    dma = pltpu.SemaphoreType.DMA
    return _Exchange(arrs, [_sds((N_DEV,) + a.shape, a.dtype) for a in arrs], [dma((n, 4)), dma((n, 4)), dma((n,))],
                     start, finish)


def _gather_forward(gathered):
    n = len(gathered)

    def copies(ins, outs, sems):
        send_sems, recv_sems = sems
        x, y, c, chips = _place()
        sibling = (x, y, 1 - c)
        sends = [_remote(_slot(ins[a], (*chip, c)), _slot(outs[a], (*chip, c)), send_sems.at[a, j], recv_sems.at[a, j], sibling)
                 for a in range(n) for j, chip in enumerate(chips)]
        arrivals = [_remote(_slot(ins[a], (*chip, c)), _slot(outs[a], (*chip, 1 - c)), send_sems.at[a, j], recv_sems.at[a, j], sibling)
                    for a in range(n) for j, chip in enumerate(chips)]
        return sends, arrivals

    def start(ins, outs, sems):
        for cp in copies(ins, outs, sems)[0]:
            cp.start()

    def finish(ins, outs, sems):
        sends, arrivals = copies(ins, outs, sems)
        for cp in arrivals:
            cp.wait_recv()
        for cp in sends:
            cp.wait_send()

    dma = pltpu.SemaphoreType.DMA
    return _Exchange(gathered, [_sds(a.shape, a.dtype) for a in gathered], [dma((n, 3)), dma((n, 3))], start, finish,
                     aliases={a: a for a in range(n)})


def _gather_all(arrs, parts=GATHER_PARTS):
    n = len(arrs)
    split = [parts if a.shape[0] % (parts * BF16_ROWS) == 0 else 1 for a in arrs]
    members = [[a for a in range(n) if p < split[a]] for p in range(parts)]

    def part_of(a, p):
        size = arrs[a].shape[0] // split[a]
        return pl.ds(p * size, size), _sds((size,) + arrs[a].shape[1:], arrs[a].dtype)

    direct = [_gather_direct([part_of(a, p)[1] for a in members[p]]) for p in range(parts)]
    passed = [_gather_forward([_sds((N_DEV,) + part_of(a, p)[1].shape, arrs[a].dtype) for a in members[p]])
              for p in range(parts)]
    jobs = direct + passed
    sem_shapes = [s for job in jobs for s in job.sem_shapes]

    def views(ins, outs, sems):
        found, at = [], 0
        for i, job in enumerate(jobs):
            p = i % parts
            rows = [part_of(a, p)[0] for a in members[p]]
            found.append(([ins[a].at[r] for a, r in zip(members[p], rows)],
                          [outs[a].at[:, r] for a, r in zip(members[p], rows)],
                          sems[at:at + len(job.sem_shapes)]))
            at += len(job.sem_shapes)
        return found

    def start(ins, outs, sems):
        for job, (i, o, s) in list(zip(jobs, views(ins, outs, sems)))[:parts]:
            job.start(i, o, s)

    def finish(ins, outs, sems):
        found = views(ins, outs, sems)
        for p in range(parts):
            i, o, s = found[p]
            direct[p].finish(i, o, s)
            _, o2, s2 = found[parts + p]
            passed[p].start(o2, o2, s2)
        for p in range(parts):
            _, o2, s2 = found[parts + p]
            passed[p].finish(o2, o2, s2)

    return _Exchange(arrs, [_sds((N_DEV,) + a.shape, a.dtype) for a in arrs], sem_shapes, start, finish)


def _pair_exchange(arrs):
    n = len(arrs)

    def copies(ins, outs, sems):
        x, y, c, _ = _place()
        return [_remote(ins[a].at[:, 1 - c], outs[a], sems[0].at[a], sems[1].at[a], (x, y, 1 - c)) for a in range(n)]

    def start(ins, outs, sems):
        for cp in copies(ins, outs, sems):
            cp.start()

    def finish(ins, outs, sems):
        for cp in copies(ins, outs, sems):
            cp.wait()

    dma = pltpu.SemaphoreType.DMA
    return _Exchange(arrs, [_sds((4,) + a.shape[2:], a.dtype) for a in arrs], [dma((n,)), dma((n,))], start, finish)


def _chip_exchange(arrs):
    n = len(arrs)

    def copies(ins, outs, sems):
        x, y, c, chips = _place()
        return [_remote(ins[a].at[1 + j], outs[a].at[j], sems[0].at[a, j], sems[1].at[a, j], (*chip, c))
                for a in range(n) for j, chip in enumerate(chips)]

    def start(ins, outs, sems):
        for cp in copies(ins, outs, sems):
            cp.start()

    def finish(ins, outs, sems):
        for cp in copies(ins, outs, sems):
            cp.wait()

    dma = pltpu.SemaphoreType.DMA
    return _Exchange(arrs, [_sds((3,) + a.shape[1:], a.dtype) for a in arrs], [dma((n, 3)), dma((n, 3))], start, finish)


def _gather_rows(vec, name, comm=()):
    def body(in_ref, out_ref, send_sems, recv_sems, local_sem):
        x, y, c, _ = _place()
        me = 4 * x + 2 * y + c
        mine = pltpu.make_async_copy(in_ref, out_ref.at[me], local_sem)
        mine.start()
        copies = []
        for k in range(1, N_DEV):
            to = (1 - x if k & 4 else x, 1 - y if k & 2 else y, 1 - c if k & 1 else c)
            copies.append(pltpu.make_async_remote_copy(
                src_ref=in_ref, dst_ref=out_ref.at[me], send_sem=send_sems.at[k - 1], recv_sem=recv_sems.at[k - 1],
                device_id=to, device_id_type=MESH))
        for cp in copies:
            cp.start()
        for cp in copies:
            cp.wait()
        mine.wait()

    return _call(
        body, name=name, out_shape=_sds((N_DEV,) + vec.shape, vec.dtype), in_specs=[ANY], out_specs=ANY,
        scratch_shapes=[pltpu.SemaphoreType.DMA((7,)), pltpu.SemaphoreType.DMA((7,)), pltpu.SemaphoreType.DMA],
    )(vec)


def _group_rows(tm):
    return SUB_ROWS if tm % SUB_ROWS == 0 else BF16_ROWS


def _for_row_groups(tm, fn):
    sub = _group_rows(tm)

    def step(i, carry):
        fn(pl.ds(pl.multiple_of(i * sub, sub), sub), i * sub)
        return carry

    lax.fori_loop(0, tm // sub, step, 0)


def _once(shape, index_map):
    return pl.BlockSpec(shape, index_map, pipeline_mode=pl.Buffered(1))


def _rms_fwd(hs, g, tm, name, comm=()):
    rows, d = hs.shape

    def body(hs_ref, g_ref, h_ref, r_ref):
        def group(rs, _):
            xv = hs_ref[rs, :]
            r = lax.rsqrt(jnp.mean(xv * xv, axis=-1, keepdims=True) + EPS)
            h_ref[rs, :] = (xv * r * g_ref[...]).astype(BF16)
            r_ref[rs, :] = r

        _for_row_groups(tm, group)

    return _call(
        body, comm=comm, name=name, grid=(rows // tm,),
        in_specs=[pl.BlockSpec((tm, d), lambda i: (i, 0)), pl.BlockSpec((1, d), lambda i: (0, 0))],
        out_specs=[pl.BlockSpec((tm, d), lambda i: (i, 0)), pl.BlockSpec((tm, 1), lambda i: (i, 0))],
        out_shape=[_sds((rows, d), BF16), _sds((rows, 1), F32)], compiler_params=_params(),
    )(hs, g)


def _ffn_gu(h, wg, wu, tm, name, comm=()):
    rows, d = h.shape
    nb, fb, _ = wg.shape

    def body(h_ref, wg_ref, wu_ref, g_ref, u_ref):
        hv = h_ref[...]
        g_ref[0] = lax.dot_general(hv, wg_ref[0], NT, preferred_element_type=F32).astype(BF16)
        u_ref[0] = lax.dot_general(hv, wu_ref[0], NT, preferred_element_type=F32).astype(BF16)

    wspec = pl.BlockSpec((1, fb, d), lambda j, i: (j, 0, 0))
    ospec = pl.BlockSpec((1, tm, fb), lambda j, i: (j, i, 0))
    return _call(
        body, comm=comm, name=name, grid=(nb, rows // tm),
        in_specs=[pl.BlockSpec((tm, d), lambda j, i: (i, 0)), wspec, wspec],
        out_specs=[ospec, ospec], out_shape=[_sds((nb, rows, fb), BF16)] * 2, compiler_params=_params(),
    )(h, wg, wu)


def _ffn_proj(h, w, tm, name, comm=()):
    rows, d = h.shape
    nb, fb, _ = w.shape

    def body(h_ref, w_ref, p_ref):
        p_ref[0] = lax.dot_general(h_ref[...], w_ref[0], NT, preferred_element_type=F32).astype(BF16)

    return _call(
        body, comm=comm, name=name, grid=(nb, rows // tm),
        in_specs=[pl.BlockSpec((tm, d), lambda j, i: (i, 0)), pl.BlockSpec((1, fb, d), lambda j, i: (j, 0, 0))],
        out_specs=pl.BlockSpec((1, tm, fb), lambda j, i: (j, i, 0)), out_shape=_sds((nb, rows, fb), BF16),
        compiler_params=_params(),
    )(h, w)


def _down_product(g_ref, u_ref, wd_ref, acc_ref):
    part = None
    for b in range(g_ref.shape[0]):
        gv = g_ref[b]
        a = gv * _sigmoid(gv) * u_ref[b]
        p = jnp.dot(a, wd_ref[b], preferred_element_type=F32)
        part = p if part is None else part + p
    acc_ref[...] = part


def _ffn_down_norm(g, u, wd, hs, gn, tm, name, comm=()):
    nb, rows, fb = g.shape
    d = hs.shape[1]

    def body(g_ref, u_ref, wd_ref, hs_ref, gn_ref, hsn_ref, hn_ref, rn_ref, acc_ref):
        _down_product(g_ref, u_ref, wd_ref, acc_ref)

        def group(rs, _):
            hsn = hs_ref[rs, :] + FFN_RES_SCALE * acc_ref[rs, :]
            r = lax.rsqrt(jnp.mean(hsn * hsn, axis=-1, keepdims=True) + EPS)
            hsn_ref[rs, :] = hsn
            hn_ref[rs, :] = (hsn * r * gn_ref[...]).astype(BF16)
            rn_ref[rs, :] = r

        _for_row_groups(tm, group)

    aspec = pl.BlockSpec((nb, tm, fb), lambda i: (0, i, 0))
    row = pl.BlockSpec((tm, d), lambda i: (i, 0))
    return _call(
        body, comm=comm, name=name, grid=(rows // tm,),
        in_specs=[aspec, aspec, _once((nb, fb, d), lambda i: (0, 0, 0)), row, pl.BlockSpec((1, d), lambda i: (0, 0))],
        out_specs=[row, row, pl.BlockSpec((tm, 1), lambda i: (i, 0))],
        out_shape=[_sds((rows, d), F32), _sds((rows, d), BF16), _sds((rows, 1), F32)],
        scratch_shapes=[pltpu.VMEM((tm, d), F32)], compiler_params=_params(),
    )(g, u, wd, hs, gn)


def _ffn_down_loss(g, u, wd, hs, gf, tgt, n_seq, tm, name, comm=()):
    nb, rows, fb = g.shape
    d = hs.shape[1]
    nt = rows // tm

    def body(g_ref, u_ref, wd_ref, hs_ref, gf_ref, tgt_ref, dhs_ref, dhsb_ref, loss_ref, dgf_ref, acc_ref):
        i = pl.program_id(0)
        _down_product(g_ref, u_ref, wd_ref, acc_ref)
        loss_ref[0] = jnp.zeros((1, d), F32)
        dgf_ref[0] = jnp.zeros((1, d), F32)

        def group(rs, r0):
            hs3 = hs_ref[rs, :] + FFN_RES_SCALE * acc_ref[rs, :]
            r = lax.rsqrt(jnp.mean(hs3 * hs3, axis=-1, keepdims=True) + EPS)
            gfv = gf_ref[...]
            y = hs3 * r
            rowid = i * tm + r0 + lax.broadcasted_iota(jnp.int32, (_group_rows(tm), 1), 0)
            err = jnp.where(rowid < n_seq, y * gfv - tgt_ref[rs, :], 0.0)
            loss_ref[0] += jnp.sum(err * err, axis=0, keepdims=True)
            dout = err * (1.0 / d)
            dgf_ref[0] += jnp.sum(dout * y, axis=0, keepdims=True)
            t = dout * gfv
            dhs = r * t - hs3 * (r * r * r) * jnp.mean(t * hs3, axis=-1, keepdims=True)
            dhs_ref[rs, :] = dhs
            dhsb_ref[rs, :] = dhs.astype(BF16)

        _for_row_groups(tm, group)

    aspec = pl.BlockSpec((nb, tm, fb), lambda i: (0, i, 0))
    row = pl.BlockSpec((tm, d), lambda i: (i, 0))
    part = pl.BlockSpec((1, 1, d), lambda i: (i, 0, 0))
    return _call(
        body, comm=comm, name=name, grid=(nt,),
        in_specs=[aspec, aspec, _once((nb, fb, d), lambda i: (0, 0, 0)), row, pl.BlockSpec((1, d), lambda i: (0, 0)), row],
        out_specs=[row, row, part, part],
        out_shape=[_sds((rows, d), F32), _sds((rows, d), BF16), _sds((nt, 1, d), F32), _sds((nt, 1, d), F32)],
        scratch_shapes=[pltpu.VMEM((tm, d), F32)], compiler_params=_params(),
    )(g, u, wd, hs, gf, tgt)


def _ffn_bwd_da(do, wd, g, u, tm, name, comm=()):
    nb, rows, fb = g.shape
    d = do.shape[1]

    def body(do_ref, wd_ref, g_ref, u_ref, dg_ref, du_ref):
        rs = pl.ds(pl.multiple_of(pl.program_id(1) * tm, tm), tm)
        da = (FFN_RES_SCALE * lax.dot_general(do_ref[rs, :], wd_ref[0], NT, preferred_element_type=F32)).astype(BF16)
        gv, uv = g_ref[0], u_ref[0]
        s = _sigmoid(gv)
        sg = gv * s
        du_ref[0] = da * sg
        dg_ref[0] = da * uv * (s + sg * (1.0 - s))

    aspec = pl.BlockSpec((1, tm, fb), lambda j, i: (j, i, 0))
    return _call(
        body, comm=comm, name=name, grid=(nb, rows // tm),
        in_specs=[_once((rows, d), lambda j, i: (0, 0)), pl.BlockSpec((1, fb, d), lambda j, i: (j, 0, 0)), aspec, aspec],
        out_specs=[aspec] * 2, out_shape=[_sds((nb, rows, fb), BF16)] * 2, compiler_params=_params(),
    )(do, wd, g, u)


def _rms_bwd(tm, d, dh_ref, hs_ref, r_ref, gn_ref, dres_ref, dhs_ref, dhsb_ref, dgn_ref):
    dgn_ref[0] = jnp.zeros((1, d), F32)

    def group(rs, _):
        dh, hs, r = dh_ref[rs, :], hs_ref[rs, :], r_ref[rs, :]
        dgn_ref[0] += jnp.sum(dh * (hs * r), axis=0, keepdims=True)
        t = dh * gn_ref[...]
        dhs = dres_ref[rs, :] + r * t - hs * (r * r * r) * jnp.mean(t * hs, axis=-1, keepdims=True)
        dhs_ref[rs, :] = dhs
        dhsb_ref[rs, :] = dhs.astype(BF16)

    _for_row_groups(tm, group)


def _blocks_dot(x_ref, w_ref):
    part = None
    for b in range(x_ref.shape[0]):
        p = jnp.dot(x_ref[b], w_ref[b], preferred_element_type=F32)
        part = p if part is None else part + p
    return part


def _ffn_bwd_dh_gate(dg, wg, tm, name, comm=()):
    nb, rows, fb = dg.shape
    d = wg.shape[2]

    def body(dg_ref, wg_ref, o_ref):
        o_ref[...] = _blocks_dot(dg_ref, wg_ref)

    return _call(
        body, comm=comm, name=name, grid=(rows // tm,),
        in_specs=[pl.BlockSpec((nb, tm, fb), lambda i: (0, i, 0)), _once((nb, fb, d), lambda i: (0, 0, 0))],
        out_specs=pl.BlockSpec((tm, d), lambda i: (i, 0)), out_shape=_sds((rows, d), F32), compiler_params=_params(),
    )(dg, wg)


def _ffn_bwd_dh(du, wu, half, hs, r, gn, dres, tm, name, comm=(), zeros=None):
    nb, rows, fb = du.shape
    d = hs.shape[1]
    nt = rows // tm
    extra = [] if zeros is None else [zeros]

    def body(du_ref, wu_ref, half_ref, hs_ref, r_ref, gn_ref, dres_ref, *rest):
        dhs_ref, dhsb_ref, dgn_ref, acc_ref = rest[len(extra):]
        acc_ref[...] = half_ref[...] + _blocks_dot(du_ref, wu_ref)
        _rms_bwd(tm, d, acc_ref, hs_ref, r_ref, gn_ref, dres_ref, dhs_ref, dhsb_ref, dgn_ref)

    row = pl.BlockSpec((tm, d), lambda i: (i, 0))
    return _call(
        body, comm=comm, name=name, grid=(nt,),
        in_specs=[pl.BlockSpec((nb, tm, fb), lambda i: (0, i, 0)), _once((nb, fb, d), lambda i: (0, 0, 0)), row, row,
                  pl.BlockSpec((tm, 1), lambda i: (i, 0)), pl.BlockSpec((1, d), lambda i: (0, 0)), row] + [ANY] * len(extra),
        out_specs=[row, row, pl.BlockSpec((1, 1, d), lambda i: (i, 0, 0))],
        out_shape=[_sds((rows, d) if zeros is None else zeros.shape, F32), _sds((rows, d), BF16), _sds((nt, 1, d), F32)],
        scratch_shapes=[pltpu.VMEM((tm, d), F32)], compiler_params=_params(),
        input_output_aliases={7: 0} if extra else {},
    )(du, wu, half, hs, r, gn, dres, *extra)


class _KSteps:
    def __init__(self, rows, tk):
        self.main, self.rem, self.tk = rows // tk, rows % tk, tk
        self.n = self.main + (1 if self.rem else 0)
        if self.rem:
            assert rows % self.rem == 0, (rows, tk)
            self.last = rows // self.rem - 1

    def specs(self, makers):
        main_at = (lambda k: jnp.minimum(k, self.main - 1)) if self.rem else (lambda k: k)
        out = [make(self.tk, main_at) for make in makers]
        if self.rem:
            out += [make(self.rem, lambda k: self.last) for make in makers]
        return out

    def args(self, arrs):
        return list(arrs) * (2 if self.rem else 1)

    def each(self, k, refs, fn):
        if not self.rem:
            fn(*refs)
            return
        n = len(refs) // 2
        pl.when(k < self.main)(lambda: fn(*refs[:n]))
        pl.when(k == self.main)(lambda: fn(*refs[n:]))


def _ffn_bwd_dw_gu(h, dg, du, tk, name, comm=()):
    nb, rows, fb = dg.shape
    d = h.shape[1]
    steps = _KSteps(rows, tk)

    def body(*refs):
        ins, (dwg_ref, dwu_ref, accg, accu) = refs[:-4], refs[-4:]
        k = pl.program_id(1)

        @pl.when(k == 0)
        def _():
            accg[...] = jnp.zeros_like(accg)
            accu[...] = jnp.zeros_like(accu)

        def add(h_ref, dg_ref, du_ref):
            hv = h_ref[...]
            accg[...] += lax.dot_general(dg_ref[0], hv, TN, preferred_element_type=F32)
            accu[...] += lax.dot_general(du_ref[0], hv, TN, preferred_element_type=F32)

        steps.each(k, ins, add)

        @pl.when(k == steps.n - 1)
        def _():
            dwg_ref[0] = accg[...].astype(BF16)
            dwu_ref[0] = accu[...].astype(BF16)

    wspec = pl.BlockSpec((1, fb, d), lambda j, k: (j, 0, 0))
    specs = steps.specs([lambda t, at: pl.BlockSpec((t, d), lambda j, k: (at(k), 0)),
                         lambda t, at: pl.BlockSpec((1, t, fb), lambda j, k: (j, at(k), 0)),
                         lambda t, at: pl.BlockSpec((1, t, fb), lambda j, k: (j, at(k), 0))])
    return _call(
        body, comm=comm, name=name, grid=(nb, steps.n),
        in_specs=specs, out_specs=[wspec] * 2,
        out_shape=[_sds((nb, fb, d), BF16)] * 2, scratch_shapes=[pltpu.VMEM((fb, d), F32)] * 2,
        compiler_params=_params(),
    )(*steps.args([h, dg, du]))


def _ffn_bwd_dw_down(g, u, do, tk, name, comm=()):
    nb, rows, fb = g.shape
    d = do.shape[1]
    steps = _KSteps(rows, tk)

    def body(*refs):
        ins, (dwd_ref, acc) = refs[:-2], refs[-2:]
        k = pl.program_id(1)

        @pl.when(k == 0)
        def _():
            acc[...] = jnp.zeros_like(acc)

        def add(g_ref, u_ref, do_ref):
            gv = g_ref[0]
            a = gv * _sigmoid(gv) * u_ref[0]
            acc[...] += lax.dot_general(a, do_ref[...], TN, preferred_element_type=F32)

        steps.each(k, ins, add)

        @pl.when(k == steps.n - 1)
        def _():
            dwd_ref[0] = (FFN_RES_SCALE * acc[...]).astype(BF16)

    specs = steps.specs([lambda t, at: pl.BlockSpec((1, t, fb), lambda j, k: (j, at(k), 0)),
                         lambda t, at: pl.BlockSpec((1, t, fb), lambda j, k: (j, at(k), 0)),
                         lambda t, at: pl.BlockSpec((t, d), lambda j, k: (at(k), 0))])
    return _call(
        body, comm=comm, name=name, grid=(nb, steps.n), in_specs=specs,
        out_specs=pl.BlockSpec((1, fb, d), lambda j, k: (j, 0, 0)), out_shape=_sds((nb, fb, d), BF16),
        scratch_shapes=[pltpu.VMEM((fb, d), F32)], compiler_params=_params(),
    )(*steps.args([g, u, do]))


def _win_fwd(h, w, b, dc, tm, name, comm=()):
    rows, d = h.shape
    ng = w.shape[1] // dc

    def body(h_ref, w_ref, b_ref, u_ref):
        u_ref[...] = (jnp.dot(h_ref[...], w_ref[...], preferred_element_type=F32) + b_ref[...]).astype(BF16)

    return _call(
        body, comm=comm, name=name, grid=(ng, rows // tm),
        in_specs=[pl.BlockSpec((tm, d), lambda m, i: (i, 0)), pl.BlockSpec((d, dc), lambda m, i: (0, m)),
                  pl.BlockSpec((1, dc), lambda m, i: (0, m))],
        out_specs=pl.BlockSpec((tm, dc), lambda m, i: (i, m)), out_shape=_sds((rows, ng * dc), BF16),
        compiler_params=_params(),
    )(h, w, b)


def _win_bwd_dh(du, w, hs, r, gn, dres, tm, name, comm=()):
    rows, d = hs.shape
    ng, _, dc = du.shape
    nt = rows // tm

    def body(du_ref, w_ref, hs_ref, r_ref, gn_ref, dres_ref, dhs_ref, dhsb_ref, dgn_ref, acc_ref):
        part = None
        for m in range(ng):
            p = lax.dot_general(du_ref[m], w_ref[:, m * dc:(m + 1) * dc], NT, preferred_element_type=F32)
            part = p if part is None else part + p
        acc_ref[...] = part
        _rms_bwd(tm, d, acc_ref, hs_ref, r_ref, gn_ref, dres_ref, dhs_ref, dhsb_ref, dgn_ref)

    row = pl.BlockSpec((tm, d), lambda i: (i, 0))
    return _call(
        body, comm=comm, name=name, grid=(nt,),
        in_specs=[pl.BlockSpec((ng, tm, dc), lambda i: (0, i, 0)), _once((d, ng * dc), lambda i: (0, 0)), row,
                  pl.BlockSpec((tm, 1), lambda i: (i, 0)), pl.BlockSpec((1, d), lambda i: (0, 0)), row],
        out_specs=[row, row, pl.BlockSpec((1, 1, d), lambda i: (i, 0, 0))],
        out_shape=[_sds((rows, d), F32), _sds((rows, d), BF16), _sds((nt, 1, d), F32)],
        scratch_shapes=[pltpu.VMEM((tm, d), F32)], compiler_params=_params(),
    )(du, w, hs, r, gn, dres)


def _win_bwd_dw(h, du, tk, name, comm=()):
    rows, d = h.shape
    ng, _, dc = du.shape
    steps = _KSteps(rows, tk)

    def body(*refs):
        ins, (dw_ref, db_ref, acc, accb) = refs[:-4], refs[-4:]
        k = pl.program_id(1)

        @pl.when(k == 0)
        def _():
            acc[...] = jnp.zeros_like(acc)
            accb[...] = jnp.zeros_like(accb)

        def add(h_ref, du_ref):
            duv = du_ref[0]
            acc[...] += lax.dot_general(h_ref[...], duv, TN, preferred_element_type=F32)
            accb[...] += jnp.sum(duv.astype(F32), axis=0, keepdims=True)

        steps.each(k, ins, add)

        @pl.when(k == steps.n - 1)
        def _():
            dw_ref[...] = acc[...].astype(BF16)
            db_ref[...] = accb[...]

    specs = steps.specs([lambda t, at: pl.BlockSpec((t, d), lambda m, k: (at(k), 0)),
                         lambda t, at: pl.BlockSpec((1, t, dc), lambda m, k: (m, at(k), 0))])
    return _call(
        body, comm=comm, name=name, grid=(ng, steps.n), in_specs=specs,
        out_specs=[pl.BlockSpec((d, dc), lambda m, k: (0, m)), pl.BlockSpec((1, dc), lambda m, k: (0, m))],
        out_shape=[_sds((d, ng * dc), BF16), _sds((1, ng * dc), F32)],
        scratch_shapes=[pltpu.VMEM((d, dc), F32), pltpu.VMEM((1, dc), F32)], compiler_params=_params(),
    )(*steps.args([h, du]))


def _layernorm_silu(zc, lg, lb):
    mu = jnp.mean(zc, axis=-1, keepdims=True)
    xc = zc - mu
    rstd = lax.rsqrt(jnp.mean(xc * xc, axis=-1, keepdims=True) + EPS)
    nrm = xc * rstd
    lin = nrm * lg + lb
    s = _sigmoid(lin)
    return nrm, rstd, lin, s


def _wout_fwd(zc, ysc, wout, hs, lg, lb, gn, rows, tm, name, comm=()):
    dc = zc.shape[1]
    d = hs.shape[1]

    def body(zc_ref, ysc_ref, w_ref, hs_ref, lg_ref, lb_ref, gn_ref, y_ref, hsn_ref, hn_ref, rn_ref):
        def mix(rs, _):
            _, _, lin, s = _layernorm_silu(zc_ref[rs, :], lg_ref[...], lb_ref[...])
            y_ref[rs, :dc] = ysc_ref[rs, :]
            y_ref[rs, dc:] = (lin * s).astype(BF16)

        _for_row_groups(tm, mix)
        hsn_ref[...] = jnp.dot(y_ref[...], w_ref[...], preferred_element_type=F32)

        def norm(rs, _):
            hsn = hs_ref[rs, :] + hsn_ref[rs, :]
            r = lax.rsqrt(jnp.mean(hsn * hsn, axis=-1, keepdims=True) + EPS)
            hsn_ref[rs, :] = hsn
            hn_ref[rs, :] = (hsn * r * gn_ref[...]).astype(BF16)
            rn_ref[rs, :] = r

        _for_row_groups(tm, norm)

    half = pl.BlockSpec((tm, dc), lambda i: (i, 0))
    row = pl.BlockSpec((tm, d), lambda i: (i, 0))
    vec_c = pl.BlockSpec((1, dc), lambda i: (0, 0))
    return _call(
        body, comm=comm, name=name, grid=(rows // tm,),
        in_specs=[half, half, _once((2 * dc, d), lambda i: (0, 0)), row, vec_c, vec_c,
                  pl.BlockSpec((1, d), lambda i: (0, 0))],
        out_specs=[pl.BlockSpec((tm, 2 * dc), lambda i: (i, 0)), row, row, pl.BlockSpec((tm, 1), lambda i: (i, 0))],
        out_shape=[_sds((rows, 2 * dc), BF16), _sds((rows, d), F32), _sds((rows, d), BF16), _sds((rows, 1), F32)],
        compiler_params=_params(),
    )(zc, ysc, wout, hs, lg, lb, gn)


def _wout_bwd_dy(do, wout, zc, lg, lb, tm, name, comm=()):
    rows, d = do.shape
    dc = zc.shape[1]
    nt = rows // tm

    def body(do_ref, w_ref, zc_ref, lg_ref, lb_ref, dysc_ref, dzc_ref, dlg_ref, dlb_ref, dy_ref):
        dy_ref[...] = lax.dot_general(do_ref[...], w_ref[...], NT, preferred_element_type=F32)
        dlb_ref[0] = jnp.zeros((1, dc), F32)
        dlg_ref[0] = jnp.zeros((1, dc), F32)

        def group(rs, _):
            dysc_ref[rs, :] = dy_ref[rs, :dc].astype(BF16)
            nrm, rstd, lin, s = _layernorm_silu(zc_ref[rs, :], lg_ref[...], lb_ref[...])
            dl = dy_ref[rs, dc:] * (s * (1.0 + lin * (1.0 - s)))
            dlb_ref[0] += jnp.sum(dl, axis=0, keepdims=True)
            dlg_ref[0] += jnp.sum(dl * nrm, axis=0, keepdims=True)
            dn = dl * lg_ref[...]
            dzc_ref[rs, :] = rstd * (dn - jnp.mean(dn, axis=-1, keepdims=True)
                                     - nrm * jnp.mean(dn * nrm, axis=-1, keepdims=True))

        _for_row_groups(tm, group)

    half = pl.BlockSpec((tm, dc), lambda i: (i, 0))
    vec_c = pl.BlockSpec((1, dc), lambda i: (0, 0))
    part = pl.BlockSpec((1, 1, dc), lambda i: (i, 0, 0))
    return _call(
        body, comm=comm, name=name, grid=(nt,),
        in_specs=[pl.BlockSpec((tm, d), lambda i: (i, 0)), _once((2 * dc, d), lambda i: (0, 0)), half, vec_c, vec_c],
        out_specs=[half, half, part, part],
        out_shape=[_sds((rows, dc), BF16), _sds((rows, dc), F32), _sds((nt, 1, dc), F32), _sds((nt, 1, dc), F32)],
        scratch_shapes=[pltpu.VMEM((tm, 2 * dc), F32)], compiler_params=_params(),
    )(do, wout, zc, lg, lb)


def _wout_bwd_dw(y, do, nb, tk, name, comm=()):
    rows, k2 = y.shape
    d = do.shape[1]
    nk = rows // tk

    def body(y_ref, do_ref, dw_ref, acc):
        k = pl.program_id(0)

        @pl.when(k == 0)
        def _():
            acc[...] = jnp.zeros_like(acc)

        acc[...] += lax.dot_general(y_ref[...], do_ref[...], TN, preferred_element_type=F32)

        @pl.when(k == nk - 1)
        def _():
            dw_ref[...] = acc[...].astype(BF16)

    return _call(
        body, comm=comm, name=name, grid=(nk,),
        in_specs=[pl.BlockSpec((tk, k2), lambda k: (k, 0)), pl.BlockSpec((tk, d), lambda k: (k, 0))],
        out_specs=_once((k2, d), lambda k: (0, 0)), out_shape=_sds((k2, d), BF16),
        scratch_shapes=[pltpu.VMEM((k2, d), F32)], compiler_params=_params(),
    )(y, do).reshape(nb, k2 // nb, d)


def _windows(win, n_res):
    length = win.shape[0]
    return [win if r == 0 else pltpu.roll(win, length - r, 0) for r in range(n_res)]


def _taps(src_ref, start, offsets, ch):
    span = -(-(max(offsets) + ch) // SUBLANES) * SUBLANES
    win = src_ref[pl.ds(start, span), :]
    shifted = _windows(win, min(SUBLANES, max(offsets) + 1))
    return [shifted[o % SUBLANES][(o // SUBLANES) * SUBLANES:(o // SUBLANES) * SUBLANES + ch] for o in offsets]


def _rows8(v):
    return jnp.sum(v.reshape(v.shape[0] // SUBLANES, SUBLANES, v.shape[1]), axis=0)


def _conv_geometry(n_seq, n_meta):
    base = CONV_PAD + n_meta
    off_cf = base - (CF_WIDTH - 1)
    off_sc = base - (SC_WIDTH - 1)
    logical = -(-(n_meta + n_seq) // CONV_CH) * CONV_CH
    return base, off_cf, off_sc, logical


def _fill_conv_inputs(c_ref, v_ref, a_ref, g_ref, scv, sz, n_seq, n_meta):
    base = CONV_PAD + n_meta
    cb = scv.shape[1]
    scv[0:CONV_PAD, :] = jnp.zeros((CONV_PAD, cb), F32)
    sz[0:CONV_PAD, :] = jnp.zeros((CONV_PAD, cb), F32)

    def put(src, dst, n):
        cv = c_ref[src, :].astype(F32) * v_ref[src, :].astype(F32)
        scv[dst, :] = cv
        sz[dst, :] = a_ref[src, :].astype(F32) * _sigmoid(g_ref[src, :].astype(F32))

    put(pl.ds(n_seq, n_meta), pl.ds(CONV_PAD, n_meta), n_meta)

    def chunk(i, carry):
        t0 = pl.multiple_of(i * CONV_CH, CONV_CH)
        put(pl.ds(t0, CONV_CH), pl.ds(base + t0, CONV_CH), CONV_CH)
        return carry

    lax.fori_loop(0, n_seq // CONV_CH, chunk, 0)


def _conv_fwd(u, wsc, wcf, cbias, n_seq, n_meta, name, comm=()):
    rows = u.shape[0]
    nb, _, cb = wsc.shape
    dc = nb * cb
    base, off_cf, off_sc, _ = _conv_geometry(n_seq, n_meta)
    a_cf, a_sc = off_cf // SUBLANES * SUBLANES, off_sc // SUBLANES * SUBLANES
    ch = CONV_CH

    def body(b_ref, c_ref, v_ref, a_ref, g_ref, wsc_ref, wcf_ref, cb_ref, ysc_ref, zc_ref, scv, sz):
        _fill_conv_inputs(c_ref, v_ref, a_ref, g_ref, scv, sz, n_seq, n_meta)
        w3, w31, bias = wsc_ref[0], wcf_ref[0], cb_ref[...]

        def chunk(i, carry):
            t0 = pl.multiple_of(i * ch, ch)
            acc = jnp.zeros((ch, cb), F32)
            for k, win in enumerate(_taps(sz, t0 + a_cf, [off_cf - a_cf + k for k in range(CF_WIDTH)], ch)):
                acc = acc + win * w31[k:k + 1, :]
            zc_ref[pl.ds(t0, ch), :] = acc + bias
            s = jnp.zeros((ch, cb), F32)
            for k, win in enumerate(_taps(scv, t0 + a_sc, [off_sc - a_sc + k for k in range(SC_WIDTH)], ch)):
                s = s + win * w3[k:k + 1, :]
            ysc_ref[pl.ds(t0, ch), :] = (b_ref[pl.ds(t0, ch), :].astype(F32) * s).astype(BF16)
            return carry

        lax.fori_loop(0, n_seq // ch, chunk, 0)
        ysc_ref[n_seq:rows, :] = jnp.zeros((rows - n_seq, cb), BF16)
        zc_ref[n_seq:rows, :] = jnp.zeros((rows - n_seq, cb), F32)

    ucol = [pl.BlockSpec((rows, cb), functools.partial(lambda m, j: (0, m * nb + j), m)) for m in range(5)]
    blk = pl.BlockSpec((rows, cb), lambda j: (0, j))
    return _call(
        body, comm=comm, name=name, grid=(nb,),
        in_specs=ucol + [pl.BlockSpec((1, SC_WIDTH, cb), lambda j: (j, 0, 0)),
                         pl.BlockSpec((1, CF_WIDTH, cb), lambda j: (j, 0, 0)), pl.BlockSpec((1, cb), lambda j: (0, j))],
        out_specs=[blk, blk], out_shape=[_sds((rows, dc), BF16), _sds((rows, dc), F32)],
        scratch_shapes=[pltpu.VMEM((base + n_seq, cb), F32)] * 2, compiler_params=_params(),
    )(u, u, u, u, u, wsc, wcf, cbias)


def _conv_bwd(u, dysc, dzc, wsc, wcf, n_seq, n_meta, name, comm=()):
    rows = u.shape[0]
    nb, _, cb = wsc.shape
    dc = nb * cb
    base, off_cf, off_sc, logical = _conv_geometry(n_seq, n_meta)
    a_cf, a_sc = off_cf // SUBLANES * SUBLANES, off_sc // SUBLANES * SUBLANES
    ch = CONV_CH
    tail = CONV_PAD

    def body(b_ref, c_ref, v_ref, a_ref, g_ref, dysc_ref, dzc_ref, wsc_ref, wcf_ref,
             du_ref, dwsc_ref, dwcf_ref, dcb_ref,
             scv, sz, sds_, sdz, dlcv, dlz, accsc, acccf, accb):
        _fill_conv_inputs(c_ref, v_ref, a_ref, g_ref, scv, sz, n_seq, n_meta)
        w3, w31 = wsc_ref[0], wcf_ref[0]
        dub_ref, duc_ref, duv_ref, dua_ref, dug_ref = (du_ref.at[m] for m in range(5))
        sds_[0:n_meta, :] = jnp.zeros((n_meta, cb), F32)
        sdz[0:n_meta, :] = jnp.zeros((n_meta, cb), F32)
        behind = logical + tail - (n_meta + n_seq)
        sds_[n_meta + n_seq:logical + tail, :] = jnp.zeros((behind, cb), F32)
        sdz[n_meta + n_seq:logical + tail, :] = jnp.zeros((behind, cb), F32)
        accsc[...] = jnp.zeros_like(accsc)
        acccf[...] = jnp.zeros_like(acccf)
        accb[...] = jnp.zeros_like(accb)

        def forward_chunk(i, carry):
            t0 = pl.multiple_of(i * ch, ch)
            rws = pl.ds(t0, ch)
            dy = dysc_ref[rws, :].astype(F32)
            ds = dy * b_ref[rws, :].astype(F32)
            dz = dzc_ref[rws, :]
            sds_[pl.ds(n_meta + t0, ch), :] = ds
            sdz[pl.ds(n_meta + t0, ch), :] = dz
            s = jnp.zeros((ch, cb), F32)
            for k, win in enumerate(_taps(scv, t0 + a_sc, [off_sc - a_sc + k for k in range(SC_WIDTH)], ch)):
                s = s + win * w3[k:k + 1, :]
                accsc[k * SUBLANES:(k + 1) * SUBLANES, :] += _rows8(ds * win)
            dub_ref[rws, :] = (dy * s).astype(BF16)
            for k, win in enumerate(_taps(sz, t0 + a_cf, [off_cf - a_cf + k for k in range(CF_WIDTH)], ch)):
                acccf[k * SUBLANES:(k + 1) * SUBLANES, :] += _rows8(dz * win)
            accb[...] += _rows8(dz)
            return carry

        lax.fori_loop(0, n_seq // ch, forward_chunk, 0)

        def backward_chunk(i, carry):
            p0 = pl.multiple_of(i * ch, ch)
            dcv = jnp.zeros((ch, cb), F32)
            for k, win in enumerate(_taps(sds_, p0, [SC_WIDTH - 1 - k for k in range(SC_WIDTH)], ch)):
                dcv = dcv + win * w3[k:k + 1, :]
            dlcv[pl.ds(p0, ch), :] = dcv
            dzi = jnp.zeros((ch, cb), F32)
            for k, win in enumerate(_taps(sdz, p0, [CF_WIDTH - 1 - k for k in range(CF_WIDTH)], ch)):
                dzi = dzi + win * w31[k:k + 1, :]
            dlz[pl.ds(p0, ch), :] = dzi
            return carry

        lax.fori_loop(0, logical // ch, backward_chunk, 0)

        def gates(phys, logi):
            dcv, dzi = dlcv[logi, :], dlz[logi, :]
            duc_ref[phys, :] = (dcv * v_ref[phys, :].astype(F32)).astype(BF16)
            duv_ref[phys, :] = (dcv * c_ref[phys, :].astype(F32)).astype(BF16)
            s = _sigmoid(g_ref[phys, :].astype(F32))
            dua_ref[phys, :] = (dzi * s).astype(BF16)
            dug_ref[phys, :] = (dzi * a_ref[phys, :].astype(F32) * s * (1.0 - s)).astype(BF16)

        def gate_chunk(i, carry):
            t0 = pl.multiple_of(i * ch, ch)
            gates(pl.ds(t0, ch), pl.ds(n_meta + t0, ch))
            return carry

        lax.fori_loop(0, n_seq // ch, gate_chunk, 0)
        gates(pl.ds(n_seq, n_meta), pl.ds(0, n_meta))
        dub_ref[n_seq:rows, :] = jnp.zeros((rows - n_seq, cb), BF16)
        pad0 = n_seq + n_meta
        for ref in (duc_ref, duv_ref, dua_ref, dug_ref):
            ref[pad0:rows, :] = jnp.zeros((rows - pad0, cb), BF16)
        dwsc_ref[0] = jnp.sum(accsc[...].reshape(SC_WIDTH, SUBLANES, cb), axis=1)
        dwcf_ref[0] = jnp.sum(acccf[...].reshape(CF_WIDTH, SUBLANES, cb), axis=1)
        dcb_ref[...] = jnp.sum(accb[...], axis=0, keepdims=True)

    ucol = [pl.BlockSpec((rows, cb), functools.partial(lambda m, j: (0, m * nb + j), m)) for m in range(5)]
    blk = pl.BlockSpec((dysc.shape[0], cb), lambda j: (0, j))
    wsc_spec = pl.BlockSpec((1, SC_WIDTH, cb), lambda j: (j, 0, 0))
    wcf_spec = pl.BlockSpec((1, CF_WIDTH, cb), lambda j: (j, 0, 0))
    outs = _call(
        body, comm=comm, name=name, grid=(nb,),
        in_specs=ucol + [blk, blk, wsc_spec, wcf_spec],
        out_specs=[pl.BlockSpec((5, rows, cb), lambda j: (0, 0, j)), wsc_spec, wcf_spec,
                   pl.BlockSpec((1, cb), lambda j: (0, j))],
        out_shape=[_sds((5, rows, dc), BF16), _sds((nb, SC_WIDTH, cb), F32), _sds((nb, CF_WIDTH, cb), F32),
                   _sds((1, dc), F32)],
        scratch_shapes=[pltpu.VMEM((base + n_seq, cb), F32)] * 2 + [pltpu.VMEM((logical + tail, cb), F32)] * 2
        + [pltpu.VMEM((logical, cb), F32)] * 2
        + [pltpu.VMEM((SC_WIDTH * SUBLANES, cb), F32), pltpu.VMEM((CF_WIDTH * SUBLANES, cb), F32),
           pltpu.VMEM((SUBLANES, cb), F32)],
        compiler_params=_params(),
    )(u, u, u, u, u, dysc, dzc, wsc, wcf)
    return outs


def _row_tile(rows, cols):
    return rows // 4 if rows % 64 == 0 and rows * cols >= (1 << 18) else rows


def _pair_sum(grad, sib, idx, name, comm=()):
    _, rows, cols = grad.shape
    tr = rows

    def body(idx_ref, g_ref, s_ref, o_ref):
        o_ref[0] = (g_ref[0].astype(F32) + s_ref[0].astype(F32)).astype(o_ref.dtype)

    return _call(
        body, name=name,
        grid_spec=pltpu.PrefetchScalarGridSpec(
            num_scalar_prefetch=1, grid=(4, rows // tr),
            in_specs=[pl.BlockSpec((1, tr, cols), lambda k, i, idx_ref: (idx_ref[k], i, 0)),
                      pl.BlockSpec((1, tr, cols), lambda k, i, idx_ref: (idx_ref[4 + k], i, 0))],
            out_specs=pl.BlockSpec((1, tr, cols), lambda k, i, idx_ref: (k, i, 0))),
        out_shape=_sds((4, rows, cols), grad.dtype), compiler_params=_params(),
    )(idx, grad, sib)


def _adamw_math(w, g, m, v):
    m = ADAM_B1 * m + (1.0 - ADAM_B1) * g
    v = ADAM_B2 * v + (1.0 - ADAM_B2) * (g * g)
    m_hat = m / (1.0 - ADAM_B1 ** ADAM_STEP)
    v_hat = v / (1.0 - ADAM_B2 ** ADAM_STEP)
    delta = -ADAM_LR * (m_hat / (jnp.sqrt(v_hat) + ADAM_EPS) + ADAM_WD * w)
    return delta, m, v


def _adamw_sharded(own, got, w, m, v, name, comm=()):
    rows, cols = w.shape
    tr = _row_tile(rows, cols)

    def body(own_ref, g0_ref, g1_ref, g2_ref, w_ref, m_ref, v_ref, g_ref, d_ref, nm_ref, nv_ref):
        g = own_ref[0].astype(F32) + g0_ref[0].astype(F32) + g1_ref[0].astype(F32) + g2_ref[0].astype(F32)
        delta, nm, nv = _adamw_math(w_ref[...], g, m_ref[...], v_ref[...])
        g_ref[...] = g
        d_ref[...] = delta
        nm_ref[...] = nm
        nv_ref[...] = nv

    flat = pl.BlockSpec((tr, cols), lambda i: (i, 0))
    slot = [pl.BlockSpec((1, tr, cols), functools.partial(lambda k, i: (k, i, 0), k)) for k in range(3)]
    return _call(
        body, comm=comm, name=name, grid=(rows // tr,),
        in_specs=[slot[0]] + slot + [flat] * 3, out_specs=[flat] * 4, out_shape=[_sds((rows, cols), F32)] * 4,
        compiler_params=_params(),
    )(own, got, got, got, w, m, v)


def _adamw_replicated(gathered, segs, ws, ms, vs, loss_scale, name, comm=()):
    n = len(ws)

    def body(*refs):
        gat = refs[0]
        w_refs, m_refs, v_refs = refs[1:1 + n], refs[1 + n:1 + 2 * n], refs[1 + 2 * n:1 + 3 * n]
        outs = refs[1 + 3 * n:]

        def total(off, width):
            s = gat[0, :, off:off + width]
            for k in range(1, N_DEV):
                s = s + gat[k, :, off:off + width]
            return s

        outs[0][...] = loss_scale * total(segs[n][0], segs[n][1])
        for p in range(n):
            g = total(*segs[p])
            delta, nm, nv = _adamw_math(w_refs[p][...], g, m_refs[p][...], v_refs[p][...])
            for q, val in enumerate((g, delta, nm, nv)):
                outs[1 + 4 * p + q][...] = val
        for e, seg in enumerate(segs[n + 1:]):
            outs[1 + 4 * n + e][...] = total(*seg)

    return _call(
        body, name=name,
        out_shape=[_sds((1, segs[n][1]), F32)] + [_sds(w.shape, F32) for w in ws for _ in range(4)]
        + [_sds((1, seg[1]), F32) for seg in segs[n + 1:]],
        compiler_params=_params(),
    )(gathered, *ws, *ms, *vs)


def _adamw_plain(g, w, m, v, name):
    def body(g_ref, w_ref, m_ref, v_ref, d_ref, nm_ref, nv_ref):
        d_ref[...], nm_ref[...], nv_ref[...] = _adamw_math(w_ref[...], g_ref[...], m_ref[...], v_ref[...])

    return list(_call(body, name=name, out_shape=[_sds(w.shape, F32)] * 3)(g, w, m, v))


REPLICATED = ("ffn1_norm", "mix_norm", "b_in", "conv_cf_b", "ln_cf_g", "ln_cf_b", "ffn2_norm", "final_norm")
SHARDED = ("meta_tokens", "ffn1_w_gate", "ffn1_w_up", "ffn1_w_down", "w_in", "conv_sc_w", "conv_cf_w", "w_out",
           "ffn2_w_gate", "ffn2_w_up", "ffn2_w_down")
WEIGHTS = ("meta_tokens", "ffn1_norm", "ffn1_w_gate", "ffn1_w_up", "ffn1_w_down", "mix_norm", "w_in", "b_in",
           "conv_sc_w", "conv_cf_w", "conv_cf_b", "ln_cf_g", "ln_cf_b", "w_out", "ffn2_norm", "ffn2_w_gate",
           "ffn2_w_up", "ffn2_w_down", "final_norm")


TRANSPOSED = ("ffn1_w_gate", "ffn1_w_up", "ffn2_w_gate", "ffn2_w_up")


def _blocks2d(k, a):
    a = a.reshape(a.shape[-2:]) if a.ndim >= 2 else a.reshape(1, -1)
    return a.T if k in TRANSPOSED else a


def _step(x, tgt, w, m, v):
    n_seq, d = x.shape[1], x.shape[2]
    n_meta = w["meta_tokens"].shape[0]
    rows = -(-(n_seq + n_meta) // ROW_ALIGN) * ROW_ALIGN
    tm = rows // N_ROW_TILES
    cb = w["conv_sc_w"].shape[-1]
    dc = cb * N_DEV
    w2 = {k: _blocks2d(k, a) for k, a in w.items()}
    m2 = {k: _blocks2d(k, a) for k, a in m.items()}
    v2 = {k: _blocks2d(k, a) for k, a in v.items()}

    def as_given(k, r):
        return (r.T if k in TRANSPOSED else r).reshape(w[k].shape)

    def cast(k):
        return w2[k].astype(BF16)

    full = dict(zip(("ffn1_w_gate", "meta_tokens", "conv_sc_w", "conv_cf_w"), _exchange_alone(
        _gather_all([cast("ffn1_w_gate"), w2["meta_tokens"], w2["conv_sc_w"], w2["conv_cf_w"]]), "gather_gate1")))
    meta = jnp.transpose(full["meta_tokens"], (1, 0, 2)).reshape(n_meta, d)
    hs0 = jnp.concatenate([x[0], meta, jnp.zeros((rows - n_seq - n_meta, d), F32)], axis=0)

    tx = n_seq // N_ROW_TILES
    h1, r1 = _rms_fwd(hs0, w2["ffn1_norm"], tm, "rms_in")
    up1 = _gather_all([cast("ffn1_w_up")])
    g1 = _ffn_proj(h1, full["ffn1_w_gate"], tm, "ffn1_gate", comm=[up1])
    down1 = _gather_all([cast("ffn1_w_down")])
    u1 = _ffn_proj(h1, up1.results[0], tm, "ffn1_up", comm=[down1])
    w_in_all, w_out1 = _gather_all([cast("w_in")]), _gather_direct([cast("w_out")])
    hs1, h2, r2 = _ffn_down_norm(g1, u1, down1.results[0], hs0, w2["mix_norm"], rows // DOWN_ROW_TILES, "ffn1_down",
                                 comm=[w_in_all, w_out1])
    win = jnp.transpose(w_in_all.results[0], (1, 0, 2)).reshape(d, -1)
    w_out2, gate2 = _gather_forward(w_out1.results), _gather_direct([cast("ffn2_w_gate")])
    u = _win_fwd(h2, win, w2["b_in"], dc, tm, "mix_in", comm=[w_out2, gate2])
    wout = w_out2.results[0].reshape(-1, d)
    gate2p = _gather_forward(gate2.results)
    ysc, zc = _conv_fwd(u, full["conv_sc_w"], full["conv_cf_w"], w2["conv_cf_b"], n_seq, n_meta, "conv_fwd", comm=[gate2p])
    up2 = _gather_all([cast("ffn2_w_up")])
    y, hs2, h3, r3 = _wout_fwd(zc, ysc, wout, hs1, w2["ln_cf_g"], w2["ln_cf_b"], w2["ffn2_norm"], n_seq, tx, "mix_out",
                               comm=[up2])
    down2 = _gather_all([cast("ffn2_w_down")])
    full.update(ffn1_w_up=up1.results[0], ffn1_w_down=down1.results[0], ffn2_w_gate=gate2p.results[0],
                ffn2_w_up=up2.results[0])
    g2, u2 = _ffn_gu(h3, full["ffn2_w_gate"], full["ffn2_w_up"], tx, "ffn2_gu", comm=[down2])
    full["ffn2_w_down"] = down2.results[0]
    dhs3, dhs3b, loss_p, dgf_p = _ffn_down_loss(
        g2, u2, full["ffn2_w_down"], hs2, w2["final_norm"], tgt[0], n_seq, n_seq // DOWN_ROW_TILES, "ffn2_down_loss")

    xi, yi, ci = lax.axis_index("x"), lax.axis_index("y"), lax.axis_index("c")
    chip_of = [2 * xi + yi, 2 * (1 - xi) + yi, 2 * xi + (1 - yi), 2 * (1 - xi) + (1 - yi)]
    idx = jnp.stack([2 * ch + ci for ch in chip_of] + chip_of).astype(jnp.int32)
    out = {}

    def to_pairs(parts):
        return _pair_exchange([p.reshape((4, 2) + p.shape[1:]) for p in parts])

    def pair_sums(names, parts, pairs):
        return [_pair_sum(p, s, idx, "pair_sum_" + k) for k, p, s in zip(names, parts, pairs.results)]

    def update(names, sums, chips):
        for k, own, got in zip(names, sums, chips.results):
            res = _adamw_sharded(own, got, w2[k], m2[k], v2[k], "adamw_" + k)
            out[k] = [as_given(k, r) for r in res]

    tk = min(2 * ROW_ALIGN, n_seq)
    th_x, th_r = n_seq // DOWN_ROW_TILES, rows // DOWN_ROW_TILES

    dwd2 = _ffn_bwd_dw_down(g2, u2, dhs3b, tk, "ffn2_bwd_dw_down")
    p_d2 = to_pairs([dwd2])
    dg2, du2 = _ffn_bwd_da(dhs3b, full["ffn2_w_down"], g2, u2, tx, "ffn2_bwd_da", comm=[p_d2])
    s_d2 = pair_sums(["ffn2_w_down"], [dwd2], p_d2)
    c_d2 = _chip_exchange(s_d2)
    gu2 = _ffn_bwd_dw_gu(h3, dg2, du2, tk, "ffn2_bwd_dw_gu", comm=[c_d2])
    update(["ffn2_w_down"], s_d2, c_d2)
    p_gu2 = to_pairs(gu2)
    half2 = _ffn_bwd_dh_gate(dg2, full["ffn2_w_gate"], th_x, "ffn2_bwd_dh_gate", comm=[p_gu2])
    s_gu2 = pair_sums(["ffn2_w_gate", "ffn2_w_up"], gu2, p_gu2)
    c_g2, c_u2 = _chip_exchange(s_gu2[:1]), _chip_exchange(s_gu2[1:])
    dhs2, dhs2b, dn3_p = _ffn_bwd_dh(du2, full["ffn2_w_up"], half2, hs2, r3, w2["ffn2_norm"], dhs3, th_x, "ffn2_bwd_dh",
                                     comm=[c_g2], zeros=jnp.zeros((rows, d), F32))
    dysc, dzc, dlg_p, dlb_p = _wout_bwd_dy(dhs2b, wout, zc, w2["ln_cf_g"], w2["ln_cf_b"], tx, "mix_out_bwd_dy",
                                           comm=[c_u2])
    update(["ffn2_w_gate"], s_gu2[:1], c_g2)
    update(["ffn2_w_up"], s_gu2[1:], c_u2)

    dwout = _wout_bwd_dw(y, dhs2b, N_DEV, tk, "mix_out_bwd_dw")
    p_wo = to_pairs([dwout])
    du, dcsw, dccw, dcb = _conv_bwd(u, dysc, dzc, full["conv_sc_w"], full["conv_cf_w"], n_seq, n_meta, "conv_bwd",
                                    comm=[p_wo])
    s_wo = pair_sums(["w_out"], [dwout], p_wo)
    c_wo = _chip_exchange(s_wo)
    dhs1, dhs1b, dn2_p = _win_bwd_dh(du, win, hs1, r2, w2["mix_norm"], dhs2, th_r, "mix_in_bwd_dh", comm=[c_wo])
    update(["w_out"], s_wo, c_wo)
    dwin, dbin = _win_bwd_dw(h2, du, tk, "mix_in_bwd_dw")
    mixer = ("w_in", "conv_sc_w", "conv_cf_w")
    gradsm = [jnp.transpose(dwin.reshape(d, N_DEV, -1), (1, 0, 2)), dcsw, dccw]
    p_m = to_pairs(gradsm)
    dwd1 = _ffn_bwd_dw_down(g1, u1, dhs1b, tk, "ffn1_bwd_dw_down", comm=[p_m])
    s_m = pair_sums(mixer, gradsm, p_m)
    c_m, p_d1 = _chip_exchange(s_m), to_pairs([dwd1])
    dg1, du1 = _ffn_bwd_da(dhs1b, full["ffn1_w_down"], g1, u1, tm, "ffn1_bwd_da", comm=[c_m, p_d1])
    update(mixer, s_m, c_m)
    s_d1 = pair_sums(["ffn1_w_down"], [dwd1], p_d1)
    c_d1 = _chip_exchange(s_d1)
    gu1 = _ffn_bwd_dw_gu(h1, dg1, du1, tk, "ffn1_bwd_dw_gu", comm=[c_d1])
    update(["ffn1_w_down"], s_d1, c_d1)
    p_gu1 = to_pairs(gu1)
    _exchange_alone(p_gu1, "reduce_pair_ffn1")
    s_gu1 = pair_sums(["ffn1_w_gate", "ffn1_w_up"], gu1, p_gu1)
    c_g1, c_u1 = _chip_exchange(s_gu1[:1]), _chip_exchange(s_gu1[1:])
    half1 = _ffn_bwd_dh_gate(dg1, full["ffn1_w_gate"], th_r, "ffn1_bwd_dh_gate", comm=[c_g1])
    dhs0, _, dn1_p = _ffn_bwd_dh(du1, full["ffn1_w_up"], half1, hs0, r1, w2["ffn1_norm"], dhs1, th_r, "ffn1_bwd_dh",
                                 comm=[c_u1])
    update(["ffn1_w_gate"], s_gu1[:1], c_g1)
    update(["ffn1_w_up"], s_gu1[1:], c_u1)
    grad_x = dhs0[:n_seq][None]

    partial = {
        "ffn1_norm": dn1_p.sum(0), "mix_norm": dn2_p.sum(0), "b_in": dbin, "conv_cf_b": dcb,
        "ln_cf_g": dlg_p.sum(0), "ln_cf_b": dlb_p.sum(0), "ffn2_norm": dn3_p.sum(0), "final_norm": dgf_p.sum(0),
    }
    loss_seg = jnp.pad(loss_p.sum((0, 2)).reshape(1, 1), ((0, 0), (0, 127)))
    pieces = [partial[k] for k in REPLICATED] + [loss_seg, dhs0[n_seq:n_seq + n_meta].reshape(1, n_meta * d)]
    segs, off = [], 0
    for p in pieces:
        segs.append((off, p.shape[1]))
        off += p.shape[1]
    rows8 = _gather_rows(jnp.concatenate(pieces, axis=1), "gather_small")
    res = _adamw_replicated(rows8, segs, [w2[k] for k in REPLICATED], [m2[k] for k in REPLICATED],
                            [v2[k] for k in REPLICATED], 0.5 / d, "adamw_replicated")
    loss = res[0][0, 0]
    for p, k in enumerate(REPLICATED):
        out[k] = [r.reshape(w[k].shape) for r in res[1 + 4 * p:5 + 4 * p]]
    ob = w2["meta_tokens"].shape[1]
    gmeta = lax.dynamic_slice_in_dim(res[-1].reshape(n_meta, d), (4 * xi + 2 * yi + ci) * ob, ob, axis=1)
    out["meta_tokens"] = [gmeta] + _adamw_plain(gmeta, w2["meta_tokens"], m2["meta_tokens"], v2["meta_tokens"], "adamw_meta_tokens")

    return (loss, grad_x, *[out[k][0] for k in WEIGHTS], *[out[k][1] for k in WEIGHTS],
            *[out[k][2] for k in WEIGHTS], *[out[k][3] for k in WEIGHTS])


def kernel(x, meta_tokens, ffn1_norm, ffn1_w_gate, ffn1_w_up, ffn1_w_down, mix_norm, w_in, b_in, conv_sc_w, conv_cf_w, conv_cf_b, ln_cf_g, ln_cf_b, w_out, ffn2_norm, ffn2_w_gate, ffn2_w_up, ffn2_w_down, final_norm, loss_target, m_meta_tokens, m_ffn1_norm, m_ffn1_w_gate, m_ffn1_w_up, m_ffn1_w_down, m_mix_norm, m_w_in, m_b_in, m_conv_sc_w, m_conv_cf_w, m_conv_cf_b, m_ln_cf_g, m_ln_cf_b, m_w_out, m_ffn2_norm, m_ffn2_w_gate, m_ffn2_w_up, m_ffn2_w_down, m_final_norm, v_meta_tokens, v_ffn1_norm, v_ffn1_w_gate, v_ffn1_w_up, v_ffn1_w_down, v_mix_norm, v_w_in, v_b_in, v_conv_sc_w, v_conv_cf_w, v_conv_cf_b, v_ln_cf_g, v_ln_cf_b, v_w_out, v_ffn2_norm, v_ffn2_w_gate, v_ffn2_w_up, v_ffn2_w_down, v_final_norm):
    given = dict(locals())
    w = {k: given[k] for k in WEIGHTS}
    m = {k: given["m_" + k] for k in WEIGHTS}
    v = {k: given["v_" + k] for k in WEIGHTS}
    return _step(x, loss_target, w, m, v)
```

```python
import functools

import jax
import jax.numpy as jnp
from jax import lax
from jax.experimental import pallas as pl
from jax.experimental.pallas import tpu as pltpu

F32 = jnp.float32
BF16 = jnp.bfloat16
EPS = 1e-6
FFN_RES_SCALE = 0.5
SC_WIDTH = 3
CF_WIDTH = 31
ADAM_LR = 0.001
ADAM_B1 = 0.9
ADAM_B2 = 0.999
ADAM_EPS = 1e-08
ADAM_WD = 0.01
ADAM_STEP = 10

N_DEV = 8
N_ROW_TILES = 8
ROW_ALIGN = 256
CONV_PAD = 32
CONV_CH = 128
SUB_ROWS = 32
DOWN_ROW_TILES = 16
GATHER_PARTS = 2
SUBLANES = 8
BF16_ROWS = 16
VMEM_LIMIT = 56 * 1024 * 1024
MESH = pl.DeviceIdType.MESH
ANY = pl.BlockSpec(memory_space=pl.ANY)

NT = (((1,), (1,)), ((), ()))
TN = (((0,), (0,)), ((), ()))


def _pallas(body, **kw):
    return pl.pallas_call(body, **kw)


class _Exchange:
    def __init__(self, inputs, out_shapes, sem_shapes, start, finish, aliases=None):
        self.inputs, self.out_shapes, self.sem_shapes = list(inputs), list(out_shapes), list(sem_shapes)
        self.start, self.finish, self.aliases = start, finish, dict(aliases or {})
        self.results = None


def _call(body, comm=(), **kw):
    if not comm:
        return _pallas(body, **kw)
    grid = kw.pop("grid")
    in_specs = list(kw.pop("in_specs"))
    out_specs, out_shape = kw.pop("out_specs"), kw.pop("out_shape")
    scratch = list(kw.pop("scratch_shapes", []))
    single = not isinstance(out_shape, (list, tuple))
    out_specs, out_shape = ([out_specs], [out_shape]) if single else (list(out_specs), list(out_shape))
    n_in, n_out, n_scr = len(in_specs), len(out_shape), len(scratch)
    c_in = [a for job in comm for a in job.inputs]
    c_out = [s for job in comm for s in job.out_shapes]
    c_sem = [s for job in comm for s in job.sem_shapes]
    aliases, i0, o0 = dict(kw.pop("input_output_aliases", {})), n_in, n_out
    for job in comm:
        aliases.update({i0 + i: o0 + o for i, o in job.aliases.items()})
        i0, o0 = i0 + len(job.inputs), o0 + len(job.out_shapes)

    def hosted(*refs):
        pos = [0]

        def take(n):
            pos[0] += n
            return refs[pos[0] - n:pos[0]]

        ins, cins, outs, couts, scr, sems = take(n_in), take(len(c_in)), take(n_out), take(len(c_out)), take(n_scr), take(len(c_sem))
        ids = [pl.program_id(k) for k in range(len(grid))]
        first = functools.reduce(jnp.logical_and, [i == 0 for i in ids])
        last = functools.reduce(jnp.logical_and, [i == g - 1 for i, g in zip(ids, grid)])

        def each(phase):
            i, o, s = 0, 0, 0
            for job in comm:
                ni, no, ns = len(job.inputs), len(job.out_shapes), len(job.sem_shapes)
                getattr(job, phase)(cins[i:i + ni], couts[o:o + no], sems[s:s + ns])
                i, o, s = i + ni, o + no, s + ns

        @pl.when(first)
        def _():
            each("start")

        body(*ins, *outs, *scr)

        @pl.when(last)
        def _():
            each("finish")

    call = _pallas(
        hosted, grid=grid, in_specs=in_specs + [ANY] * len(c_in), out_specs=out_specs + [ANY] * len(c_out),
        out_shape=out_shape + c_out, scratch_shapes=scratch + c_sem, input_output_aliases=aliases, **kw)

    def run(*args):
        res = call(*args, *c_in)
        o = n_out
        for job in comm:
            job.results = list(res[o:o + len(job.out_shapes)])
            o += len(job.out_shapes)
        return res[0] if single else list(res[:n_out])

    return run


def _exchange_alone(job, name, comm=()):
    n_in, n_out = len(job.inputs), len(job.out_shapes)

    def body(*refs):
        ins, outs, sems = refs[:n_in], refs[n_in:n_in + n_out], refs[n_in + n_out:]
        job.start(ins, outs, sems)
        job.finish(ins, outs, sems)

    res = _pallas(body, name=name, in_specs=[ANY] * n_in, out_specs=[ANY] * n_out, out_shape=job.out_shapes,
                  scratch_shapes=job.sem_shapes, input_output_aliases=job.aliases)(*job.inputs)
    job.results = list(res)
    return job.results


def _params(**kw):
    return pltpu.CompilerParams(vmem_limit_bytes=VMEM_LIMIT, **kw)


def _sigmoid(x):
    return 0.5 * jnp.tanh(0.5 * x) + 0.5


def _sds(shape, dtype):
    return jax.ShapeDtypeStruct(shape, dtype)


def _place():
    x, y, c = lax.axis_index("x"), lax.axis_index("y"), lax.axis_index("c")
    chips = [(1 - x, y), (x, 1 - y), (1 - x, 1 - y)]
    return x, y, c, chips


def _slot(ref, p):
    return ref.at[4 * p[0] + 2 * p[1] + p[2]]


def _remote(src, dst, send_sem, recv_sem, to):
    return pltpu.make_async_remote_copy(src_ref=src, dst_ref=dst, send_sem=send_sem, recv_sem=recv_sem,
                                        device_id=to, device_id_type=MESH)


def _gather_direct(arrs):
    n = len(arrs)

    def copies(ins, outs, sems):
        send_sems, recv_sems, local_sems = sems
        x, y, c, chips = _place()
        me = (x, y, c)
        peers = [(x, y, 1 - c)] + [(*chip, c) for chip in chips]
        local = [pltpu.make_async_copy(ins[a], _slot(outs[a], me), local_sems.at[a]) for a in range(n)]
        sends = [_remote(ins[a], _slot(outs[a], me), send_sems.at[a, k], recv_sems.at[a, k], peer)
                 for a in range(n) for k, peer in enumerate(peers)]
        arrivals = [_remote(ins[a], _slot(outs[a], peer), send_sems.at[a, k], recv_sems.at[a, k], peer)
                    for a in range(n) for k, peer in enumerate(peers)]
        return local, sends, arrivals

    def start(ins, outs, sems):
        local, sends, _ = copies(ins, outs, sems)
        for cp in local + sends:
            cp.start()

    def finish(ins, outs, sems):
        local, sends, arrivals = copies(ins, outs, sems)
        for cp in arrivals:
            cp.wait_recv()
        for cp in sends:
            cp.wait_send()
        for cp in local:
            cp.wait()

    dma = pltpu.SemaphoreType.DMA
    return _Exchange(arrs, [_sds((N_DEV,) + a.shape, a.dtype) for a in arrs], [dma((n, 4)), dma((n, 4)), dma((n,))],
                     start, finish)


def _gather_forward(gathered):
    n = len(gathered)

    def copies(ins, outs, sems):
        send_sems, recv_sems = sems
        x, y, c, chips = _place()
        sibling = (x, y, 1 - c)
        sends = [_remote(_slot(ins[a], (*chip, c)), _slot(outs[a], (*chip, c)), send_sems.at[a, j], recv_sems.at[a, j], sibling)
                 for a in range(n) for j, chip in enumerate(chips)]
        arrivals = [_remote(_slot(ins[a], (*chip, c)), _slot(outs[a], (*chip, 1 - c)), send_sems.at[a, j], recv_sems.at[a, j], sibling)
                    for a in range(n) for j, chip in enumerate(chips)]
        return sends, arrivals

    def start(ins, outs, sems):
        for cp in copies(ins, outs, sems)[0]:
            cp.start()

    def finish(ins, outs, sems):
        sends, arrivals = copies(ins, outs, sems)
        for cp in arrivals:
            cp.wait_recv()
        for cp in sends:
            cp.wait_send()

    dma = pltpu.SemaphoreType.DMA
    return _Exchange(gathered, [_sds(a.shape, a.dtype) for a in gathered], [dma((n, 3)), dma((n, 3))], start, finish,
                     aliases={a: a for a in range(n)})


def _gather_all(arrs, parts=GATHER_PARTS):
    n = len(arrs)
    split = [parts if a.shape[0] % (parts * BF16_ROWS) == 0 else 1 for a in arrs]
    members = [[a for a in range(n) if p < split[a]] for p in range(parts)]

    def part_of(a, p):
        size = arrs[a].shape[0] // split[a]
        return pl.ds(p * size, size), _sds((size,) + arrs[a].shape[1:], arrs[a].dtype)

    direct = [_gather_direct([part_of(a, p)[1] for a in members[p]]) for p in range(parts)]
    passed = [_gather_forward([_sds((N_DEV,) + part_of(a, p)[1].shape, arrs[a].dtype) for a in members[p]])
              for p in range(parts)]
    jobs = direct + passed
    sem_shapes = [s for job in jobs for s in job.sem_shapes]

    def views(ins, outs, sems):
        found, at = [], 0
        for i, job in enumerate(jobs):
            p = i % parts
            rows = [part_of(a, p)[0] for a in members[p]]
            found.append(([ins[a].at[r] for a, r in zip(members[p], rows)],
                          [outs[a].at[:, r] for a, r in zip(members[p], rows)],
                          sems[at:at + len(job.sem_shapes)]))
            at += len(job.sem_shapes)
        return found

    def start(ins, outs, sems):
        for job, (i, o, s) in list(zip(jobs, views(ins, outs, sems)))[:parts]:
            job.start(i, o, s)

    def finish(ins, outs, sems):
        found = views(ins, outs, sems)
        for p in range(parts):
            i, o, s = found[p]
            direct[p].finish(i, o, s)
            _, o2, s2 = found[parts + p]
            passed[p].start(o2, o2, s2)
        for p in range(parts):
            _, o2, s2 = found[parts + p]
            passed[p].finish(o2, o2, s2)

    return _Exchange(arrs, [_sds((N_DEV,) + a.shape, a.dtype) for a in arrs], sem_shapes, start, finish)


def _pair_exchange(arrs):
    n = len(arrs)

    def copies(ins, outs, sems):
        x, y, c, _ = _place()
        return [_remote(ins[a].at[:, 1 - c], outs[a], sems[0].at[a], sems[1].at[a], (x, y, 1 - c)) for a in range(n)]

    def start(ins, outs, sems):
        for cp in copies(ins, outs, sems):
            cp.start()

    def finish(ins, outs, sems):
        for cp in copies(ins, outs, sems):
            cp.wait()

    dma = pltpu.SemaphoreType.DMA
    return _Exchange(arrs, [_sds((4,) + a.shape[2:], a.dtype) for a in arrs], [dma((n,)), dma((n,))], start, finish)


def _chip_exchange(arrs):
    n = len(arrs)

    def copies(ins, outs, sems):
        x, y, c, chips = _place()
        return [_remote(ins[a].at[1 + j], outs[a].at[j], sems[0].at[a, j], sems[1].at[a, j], (*chip, c))
                for a in range(n) for j, chip in enumerate(chips)]

    def start(ins, outs, sems):
        for cp in copies(ins, outs, sems):
            cp.start()

    def finish(ins, outs, sems):
        for cp in copies(ins, outs, sems):
            cp.wait()

    dma = pltpu.SemaphoreType.DMA
    return _Exchange(arrs, [_sds((3,) + a.shape[1:], a.dtype) for a in arrs], [dma((n, 3)), dma((n, 3))], start, finish)


def _gather_rows(vec, name, comm=()):
    def body(in_ref, out_ref, send_sems, recv_sems, local_sem):
        x, y, c, _ = _place()
        me = 4 * x + 2 * y + c
        mine = pltpu.make_async_copy(in_ref, out_ref.at[me], local_sem)
        mine.start()
        copies = []
        for k in range(1, N_DEV):
            to = (1 - x if k & 4 else x, 1 - y if k & 2 else y, 1 - c if k & 1 else c)
            copies.append(pltpu.make_async_remote_copy(
                src_ref=in_ref, dst_ref=out_ref.at[me], send_sem=send_sems.at[k - 1], recv_sem=recv_sems.at[k - 1],
                device_id=to, device_id_type=MESH))
        for cp in copies:
            cp.start()
        for cp in copies:
            cp.wait()
        mine.wait()

    return _call(
        body, name=name, out_shape=_sds((N_DEV,) + vec.shape, vec.dtype), in_specs=[ANY], out_specs=ANY,
        scratch_shapes=[pltpu.SemaphoreType.DMA((7,)), pltpu.SemaphoreType.DMA((7,)), pltpu.SemaphoreType.DMA],
    )(vec)


def _group_rows(tm):
    return SUB_ROWS if tm % SUB_ROWS == 0 else BF16_ROWS


def _for_row_groups(tm, fn):
    sub = _group_rows(tm)

    def step(i, carry):
        fn(pl.ds(pl.multiple_of(i * sub, sub), sub), i * sub)
        return carry

    lax.fori_loop(0, tm // sub, step, 0)


def _once(shape, index_map):
    return pl.BlockSpec(shape, index_map, pipeline_mode=pl.Buffered(1))


def _rms_fwd(hs, g, tm, name, comm=()):
    rows, d = hs.shape

    def body(hs_ref, g_ref, h_ref, r_ref):
        def group(rs, _):
            xv = hs_ref[rs, :]
            r = lax.rsqrt(jnp.mean(xv * xv, axis=-1, keepdims=True) + EPS)
            h_ref[rs, :] = (xv * r * g_ref[...]).astype(BF16)
            r_ref[rs, :] = r

        _for_row_groups(tm, group)

    return _call(
        body, comm=comm, name=name, grid=(rows // tm,),
        in_specs=[pl.BlockSpec((tm, d), lambda i: (i, 0)), pl.BlockSpec((1, d), lambda i: (0, 0))],
        out_specs=[pl.BlockSpec((tm, d), lambda i: (i, 0)), pl.BlockSpec((tm, 1), lambda i: (i, 0))],
        out_shape=[_sds((rows, d), BF16), _sds((rows, 1), F32)], compiler_params=_params(),
    )(hs, g)


def _ffn_gu(h, wg, wu, tm, name, comm=()):
    rows, d = h.shape
    nb, fb, _ = wg.shape

    def body(h_ref, wg_ref, wu_ref, g_ref, u_ref):
        hv = h_ref[...]
        g_ref[0] = lax.dot_general(hv, wg_ref[0], NT, preferred_element_type=F32).astype(BF16)
        u_ref[0] = lax.dot_general(hv, wu_ref[0], NT, preferred_element_type=F32).astype(BF16)

    wspec = pl.BlockSpec((1, fb, d), lambda j, i: (j, 0, 0))
    ospec = pl.BlockSpec((1, tm, fb), lambda j, i: (j, i, 0))
    return _call(
        body, comm=comm, name=name, grid=(nb, rows // tm),
        in_specs=[pl.BlockSpec((tm, d), lambda j, i: (i, 0)), wspec, wspec],
        out_specs=[ospec, ospec], out_shape=[_sds((nb, rows, fb), BF16)] * 2, compiler_params=_params(),
    )(h, wg, wu)


def _ffn_proj(h, w, tm, name, comm=()):
    rows, d = h.shape
    nb, fb, _ = w.shape

    def body(h_ref, w_ref, p_ref):
        p_ref[0] = lax.dot_general(h_ref[...], w_ref[0], NT, preferred_element_type=F32).astype(BF16)

    return _call(
        body, comm=comm, name=name, grid=(nb, rows // tm),
        in_specs=[pl.BlockSpec((tm, d), lambda j, i: (i, 0)), pl.BlockSpec((1, fb, d), lambda j, i: (j, 0, 0))],
        out_specs=pl.BlockSpec((1, tm, fb), lambda j, i: (j, i, 0)), out_shape=_sds((nb, rows, fb), BF16),
        compiler_params=_params(),
    )(h, w)


def _down_product(g_ref, u_ref, wd_ref, acc_ref):
    part = None
    for b in range(g_ref.shape[0]):
        gv = g_ref[b]
        a = gv * _sigmoid(gv) * u_ref[b]
        p = jnp.dot(a, wd_ref[b], preferred_element_type=F32)
        part = p if part is None else part + p
    acc_ref[...] = part


def _ffn_down_norm(g, u, wd, hs, gn, tm, name, comm=()):
    nb, rows, fb = g.shape
    d = hs.shape[1]

    def body(g_ref, u_ref, wd_ref, hs_ref, gn_ref, hsn_ref, hn_ref, rn_ref, acc_ref):
        _down_product(g_ref, u_ref, wd_ref, acc_ref)

        def group(rs, _):
            hsn = hs_ref[rs, :] + FFN_RES_SCALE * acc_ref[rs, :]
            r = lax.rsqrt(jnp.mean(hsn * hsn, axis=-1, keepdims=True) + EPS)
            hsn_ref[rs, :] = hsn
            hn_ref[rs, :] = (hsn * r * gn_ref[...]).astype(BF16)
            rn_ref[rs, :] = r

        _for_row_groups(tm, group)

    aspec = pl.BlockSpec((nb, tm, fb), lambda i: (0, i, 0))
    row = pl.BlockSpec((tm, d), lambda i: (i, 0))
    return _call(
        body, comm=comm, name=name, grid=(rows // tm,),
        in_specs=[aspec, aspec, _once((nb, fb, d), lambda i: (0, 0, 0)), row, pl.BlockSpec((1, d), lambda i: (0, 0))],
        out_specs=[row, row, pl.BlockSpec((tm, 1), lambda i: (i, 0))],
        out_shape=[_sds((rows, d), F32), _sds((rows, d), BF16), _sds((rows, 1), F32)],
        scratch_shapes=[pltpu.VMEM((tm, d), F32)], compiler_params=_params(),
    )(g, u, wd, hs, gn)


def _ffn_down_loss(g, u, wd, hs, gf, tgt, n_seq, tm, name, comm=()):
    nb, rows, fb = g.shape
    d = hs.shape[1]
    nt = rows // tm

    def body(g_ref, u_ref, wd_ref, hs_ref, gf_ref, tgt_ref, dhs_ref, dhsb_ref, loss_ref, dgf_ref, acc_ref):
        i = pl.program_id(0)
        _down_product(g_ref, u_ref, wd_ref, acc_ref)
        loss_ref[0] = jnp.zeros((1, d), F32)
        dgf_ref[0] = jnp.zeros((1, d), F32)

        def group(rs, r0):
            hs3 = hs_ref[rs, :] + FFN_RES_SCALE * acc_ref[rs, :]
            r = lax.rsqrt(jnp.mean(hs3 * hs3, axis=-1, keepdims=True) + EPS)
            gfv = gf_ref[...]
            y = hs3 * r
            rowid = i * tm + r0 + lax.broadcasted_iota(jnp.int32, (_group_rows(tm), 1), 0)
            err = jnp.where(rowid < n_seq, y * gfv - tgt_ref[rs, :], 0.0)
            loss_ref[0] += jnp.sum(err * err, axis=0, keepdims=True)
            dout = err * (1.0 / d)
            dgf_ref[0] += jnp.sum(dout * y, axis=0, keepdims=True)
            t = dout * gfv
            dhs = r * t - hs3 * (r * r * r) * jnp.mean(t * hs3, axis=-1, keepdims=True)
            dhs_ref[rs, :] = dhs
            dhsb_ref[rs, :] = dhs.astype(BF16)

        _for_row_groups(tm, group)

    aspec = pl.BlockSpec((nb, tm, fb), lambda i: (0, i, 0))
    row = pl.BlockSpec((tm, d), lambda i: (i, 0))
    part = pl.BlockSpec((1, 1, d), lambda i: (i, 0, 0))
    return _call(
        body, comm=comm, name=name, grid=(nt,),
        in_specs=[aspec, aspec, _once((nb, fb, d), lambda i: (0, 0, 0)), row, pl.BlockSpec((1, d), lambda i: (0, 0)), row],
        out_specs=[row, row, part, part],
        out_shape=[_sds((rows, d), F32), _sds((rows, d), BF16), _sds((nt, 1, d), F32), _sds((nt, 1, d), F32)],
        scratch_shapes=[pltpu.VMEM((tm, d), F32)], compiler_params=_params(),
    )(g, u, wd, hs, gf, tgt)


def _ffn_bwd_da(do, wd, g, u, tm, name, comm=()):
    nb, rows, fb = g.shape
    d = do.shape[1]

    def body(do_ref, wd_ref, g_ref, u_ref, dg_ref, du_ref):
        rs = pl.ds(pl.multiple_of(pl.program_id(1) * tm, tm), tm)
        da = (FFN_RES_SCALE * lax.dot_general(do_ref[rs, :], wd_ref[0], NT, preferred_element_type=F32)).astype(BF16)
        gv, uv = g_ref[0], u_ref[0]
        s = _sigmoid(gv)
        sg = gv * s
        du_ref[0] = da * sg
        dg_ref[0] = da * uv * (s + sg * (1.0 - s))

    aspec = pl.BlockSpec((1, tm, fb), lambda j, i: (j, i, 0))
    return _call(
        body, comm=comm, name=name, grid=(nb, rows // tm),
        in_specs=[_once((rows, d), lambda j, i: (0, 0)), pl.BlockSpec((1, fb, d), lambda j, i: (j, 0, 0)), aspec, aspec],
        out_specs=[aspec] * 2, out_shape=[_sds((nb, rows, fb), BF16)] * 2, compiler_params=_params(),
    )(do, wd, g, u)


def _rms_bwd(tm, d, dh_ref, hs_ref, r_ref, gn_ref, dres_ref, dhs_ref, dhsb_ref, dgn_ref):
    dgn_ref[0] = jnp.zeros((1, d), F32)

    def group(rs, _):
        dh, hs, r = dh_ref[rs, :], hs_ref[rs, :], r_ref[rs, :]
        dgn_ref[0] += jnp.sum(dh * (hs * r), axis=0, keepdims=True)
        t = dh * gn_ref[...]
        dhs = dres_ref[rs, :] + r * t - hs * (r * r * r) * jnp.mean(t * hs, axis=-1, keepdims=True)
        dhs_ref[rs, :] = dhs
        dhsb_ref[rs, :] = dhs.astype(BF16)

    _for_row_groups(tm, group)


def _blocks_dot(x_ref, w_ref):
    part = None
    for b in range(x_ref.shape[0]):
        p = jnp.dot(x_ref[b], w_ref[b], preferred_element_type=F32)
        part = p if part is None else part + p
    return part


def _ffn_bwd_dh_gate(dg, wg, tm, name, comm=()):
    nb, rows, fb = dg.shape
    d = wg.shape[2]

    def body(dg_ref, wg_ref, o_ref):
        o_ref[...] = _blocks_dot(dg_ref, wg_ref)

    return _call(
        body, comm=comm, name=name, grid=(rows // tm,),
        in_specs=[pl.BlockSpec((nb, tm, fb), lambda i: (0, i, 0)), _once((nb, fb, d), lambda i: (0, 0, 0))],
        out_specs=pl.BlockSpec((tm, d), lambda i: (i, 0)), out_shape=_sds((rows, d), F32), compiler_params=_params(),
    )(dg, wg)


def _ffn_bwd_dh(du, wu, half, hs, r, gn, dres, tm, name, comm=(), zeros=None):
    nb, rows, fb = du.shape
    d = hs.shape[1]
    nt = rows // tm
    extra = [] if zeros is None else [zeros]

    def body(du_ref, wu_ref, half_ref, hs_ref, r_ref, gn_ref, dres_ref, *rest):
        dhs_ref, dhsb_ref, dgn_ref, acc_ref = rest[len(extra):]
        acc_ref[...] = half_ref[...] + _blocks_dot(du_ref, wu_ref)
        _rms_bwd(tm, d, acc_ref, hs_ref, r_ref, gn_ref, dres_ref, dhs_ref, dhsb_ref, dgn_ref)

    row = pl.BlockSpec((tm, d), lambda i: (i, 0))
    return _call(
        body, comm=comm, name=name, grid=(nt,),
        in_specs=[pl.BlockSpec((nb, tm, fb), lambda i: (0, i, 0)), _once((nb, fb, d), lambda i: (0, 0, 0)), row, row,
                  pl.BlockSpec((tm, 1), lambda i: (i, 0)), pl.BlockSpec((1, d), lambda i: (0, 0)), row] + [ANY] * len(extra),
        out_specs=[row, row, pl.BlockSpec((1, 1, d), lambda i: (i, 0, 0))],
        out_shape=[_sds((rows, d) if zeros is None else zeros.shape, F32), _sds((rows, d), BF16), _sds((nt, 1, d), F32)],
        scratch_shapes=[pltpu.VMEM((tm, d), F32)], compiler_params=_params(),
        input_output_aliases={7: 0} if extra else {},
    )(du, wu, half, hs, r, gn, dres, *extra)


class _KSteps:
    def __init__(self, rows, tk):
        self.main, self.rem, self.tk = rows // tk, rows % tk, tk
        self.n = self.main + (1 if self.rem else 0)
        if self.rem:
            assert rows % self.rem == 0, (rows, tk)
            self.last = rows // self.rem - 1

    def specs(self, makers):
        main_at = (lambda k: jnp.minimum(k, self.main - 1)) if self.rem else (lambda k: k)
        out = [make(self.tk, main_at) for make in makers]
        if self.rem:
            out += [make(self.rem, lambda k: self.last) for make in makers]
        return out

    def args(self, arrs):
        return list(arrs) * (2 if self.rem else 1)

    def each(self, k, refs, fn):
        if not self.rem:
            fn(*refs)
            return
        n = len(refs) // 2
        pl.when(k < self.main)(lambda: fn(*refs[:n]))
        pl.when(k == self.main)(lambda: fn(*refs[n:]))


def _ffn_bwd_dw_gu(h, dg, du, tk, name, comm=()):
    nb, rows, fb = dg.shape
    d = h.shape[1]
    steps = _KSteps(rows, tk)

    def body(*refs):
        ins, (dwg_ref, dwu_ref, accg, accu) = refs[:-4], refs[-4:]
        k = pl.program_id(1)

        @pl.when(k == 0)
        def _():
            accg[...] = jnp.zeros_like(accg)
            accu[...] = jnp.zeros_like(accu)

        def add(h_ref, dg_ref, du_ref):
            hv = h_ref[...]
            accg[...] += lax.dot_general(dg_ref[0], hv, TN, preferred_element_type=F32)
            accu[...] += lax.dot_general(du_ref[0], hv, TN, preferred_element_type=F32)

        steps.each(k, ins, add)

        @pl.when(k == steps.n - 1)
        def _():
            dwg_ref[0] = accg[...].astype(BF16)
            dwu_ref[0] = accu[...].astype(BF16)

    wspec = pl.BlockSpec((1, fb, d), lambda j, k: (j, 0, 0))
    specs = steps.specs([lambda t, at: pl.BlockSpec((t, d), lambda j, k: (at(k), 0)),
                         lambda t, at: pl.BlockSpec((1, t, fb), lambda j, k: (j, at(k), 0)),
                         lambda t, at: pl.BlockSpec((1, t, fb), lambda j, k: (j, at(k), 0))])
    return _call(
        body, comm=comm, name=name, grid=(nb, steps.n),
        in_specs=specs, out_specs=[wspec] * 2,
        out_shape=[_sds((nb, fb, d), BF16)] * 2, scratch_shapes=[pltpu.VMEM((fb, d), F32)] * 2,
        compiler_params=_params(),
    )(*steps.args([h, dg, du]))


def _ffn_bwd_dw_down(g, u, do, tk, name, comm=()):
    nb, rows, fb = g.shape
    d = do.shape[1]
    steps = _KSteps(rows, tk)

    def body(*refs):
        ins, (dwd_ref, acc) = refs[:-2], refs[-2:]
        k = pl.program_id(1)

        @pl.when(k == 0)
        def _():
            acc[...] = jnp.zeros_like(acc)

        def add(g_ref, u_ref, do_ref):
            gv = g_ref[0]
            a = gv * _sigmoid(gv) * u_ref[0]
            acc[...] += lax.dot_general(a, do_ref[...], TN, preferred_element_type=F32)

        steps.each(k, ins, add)

        @pl.when(k == steps.n - 1)
        def _():
            dwd_ref[0] = (FFN_RES_SCALE * acc[...]).astype(BF16)

    specs = steps.specs([lambda t, at: pl.BlockSpec((1, t, fb), lambda j, k: (j, at(k), 0)),
                         lambda t, at: pl.BlockSpec((1, t, fb), lambda j, k: (j, at(k), 0)),
                         lambda t, at: pl.BlockSpec((t, d), lambda j, k: (at(k), 0))])
    return _call(
        body, comm=comm, name=name, grid=(nb, steps.n), in_specs=specs,
        out_specs=pl.BlockSpec((1, fb, d), lambda j, k: (j, 0, 0)), out_shape=_sds((nb, fb, d), BF16),
        scratch_shapes=[pltpu.VMEM((fb, d), F32)], compiler_params=_params(),
    )(*steps.args([g, u, do]))


def _win_fwd(h, w, b, dc, tm, name, comm=()):
    rows, d = h.shape
    ng = w.shape[1] // dc

    def body(h_ref, w_ref, b_ref, u_ref):
        u_ref[...] = (jnp.dot(h_ref[...], w_ref[...], preferred_element_type=F32) + b_ref[...]).astype(BF16)

    return _call(
        body, comm=comm, name=name, grid=(ng, rows // tm),
        in_specs=[pl.BlockSpec((tm, d), lambda m, i: (i, 0)), pl.BlockSpec((d, dc), lambda m, i: (0, m)),
                  pl.BlockSpec((1, dc), lambda m, i: (0, m))],
        out_specs=pl.BlockSpec((tm, dc), lambda m, i: (i, m)), out_shape=_sds((rows, ng * dc), BF16),
        compiler_params=_params(),
    )(h, w, b)


def _win_bwd_dh(du, w, hs, r, gn, dres, tm, name, comm=()):
    rows, d = hs.shape
    ng, _, dc = du.shape
    nt = rows // tm

    def body(du_ref, w_ref, hs_ref, r_ref, gn_ref, dres_ref, dhs_ref, dhsb_ref, dgn_ref, acc_ref):
        part = None
        for m in range(ng):
            p = lax.dot_general(du_ref[m], w_ref[:, m * dc:(m + 1) * dc], NT, preferred_element_type=F32)
            part = p if part is None else part + p
        acc_ref[...] = part
        _rms_bwd(tm, d, acc_ref, hs_ref, r_ref, gn_ref, dres_ref, dhs_ref, dhsb_ref, dgn_ref)

    row = pl.BlockSpec((tm, d), lambda i: (i, 0))
    return _call(
        body, comm=comm, name=name, grid=(nt,),
        in_specs=[pl.BlockSpec((ng, tm, dc), lambda i: (0, i, 0)), _once((d, ng * dc), lambda i: (0, 0)), row,
                  pl.BlockSpec((tm, 1), lambda i: (i, 0)), pl.BlockSpec((1, d), lambda i: (0, 0)), row],
        out_specs=[row, row, pl.BlockSpec((1, 1, d), lambda i: (i, 0, 0))],
        out_shape=[_sds((rows, d), F32), _sds((rows, d), BF16), _sds((nt, 1, d), F32)],
        scratch_shapes=[pltpu.VMEM((tm, d), F32)], compiler_params=_params(),
    )(du, w, hs, r, gn, dres)


def _win_bwd_dw(h, du, tk, name, comm=()):
    rows, d = h.shape
    ng, _, dc = du.shape
    steps = _KSteps(rows, tk)

    def body(*refs):
        ins, (dw_ref, db_ref, acc, accb) = refs[:-4], refs[-4:]
        k = pl.program_id(1)

        @pl.when(k == 0)
        def _():
            acc[...] = jnp.zeros_like(acc)
            accb[...] = jnp.zeros_like(accb)

        def add(h_ref, du_ref):
            duv = du_ref[0]
            acc[...] += lax.dot_general(h_ref[...], duv, TN, preferred_element_type=F32)
            accb[...] += jnp.sum(duv.astype(F32), axis=0, keepdims=True)

        steps.each(k, ins, add)

        @pl.when(k == steps.n - 1)
        def _():
            dw_ref[...] = acc[...].astype(BF16)
            db_ref[...] = accb[...]

    specs = steps.specs([lambda t, at: pl.BlockSpec((t, d), lambda m, k: (at(k), 0)),
                         lambda t, at: pl.BlockSpec((1, t, dc), lambda m, k: (m, at(k), 0))])
    return _call(
        body, comm=comm, name=name, grid=(ng, steps.n), in_specs=specs,
        out_specs=[pl.BlockSpec((d, dc), lambda m, k: (0, m)), pl.BlockSpec((1, dc), lambda m, k: (0, m))],
        out_shape=[_sds((d, ng * dc), BF16), _sds((1, ng * dc), F32)],
        scratch_shapes=[pltpu.VMEM((d, dc), F32), pltpu.VMEM((1, dc), F32)], compiler_params=_params(),
    )(*steps.args([h, du]))


def _layernorm_silu(zc, lg, lb):
    mu = jnp.mean(zc, axis=-1, keepdims=True)
    xc = zc - mu
    rstd = lax.rsqrt(jnp.mean(xc * xc, axis=-1, keepdims=True) + EPS)
    nrm = xc * rstd
    lin = nrm * lg + lb
    s = _sigmoid(lin)
    return nrm, rstd, lin, s


def _wout_fwd(zc, ysc, wout, hs, lg, lb, gn, rows, tm, name, comm=()):
    dc = zc.shape[1]
    d = hs.shape[1]

    def body(zc_ref, ysc_ref, w_ref, hs_ref, lg_ref, lb_ref, gn_ref, y_ref, hsn_ref, hn_ref, rn_ref):
        def mix(rs, _):
            _, _, lin, s = _layernorm_silu(zc_ref[rs, :], lg_ref[...], lb_ref[...])
            y_ref[rs, :dc] = ysc_ref[rs, :]
            y_ref[rs, dc:] = (lin * s).astype(BF16)

        _for_row_groups(tm, mix)
        hsn_ref[...] = jnp.dot(y_ref[...], w_ref[...], preferred_element_type=F32)

        def norm(rs, _):
            hsn = hs_ref[rs, :] + hsn_ref[rs, :]
            r = lax.rsqrt(jnp.mean(hsn * hsn, axis=-1, keepdims=True) + EPS)
            hsn_ref[rs, :] = hsn
            hn_ref[rs, :] = (hsn * r * gn_ref[...]).astype(BF16)
            rn_ref[rs, :] = r

        _for_row_groups(tm, norm)

    half = pl.BlockSpec((tm, dc), lambda i: (i, 0))
    row = pl.BlockSpec((tm, d), lambda i: (i, 0))
    vec_c = pl.BlockSpec((1, dc), lambda i: (0, 0))
    return _call(
        body, comm=comm, name=name, grid=(rows // tm,),
        in_specs=[half, half, _once((2 * dc, d), lambda i: (0, 0)), row, vec_c, vec_c,
                  pl.BlockSpec((1, d), lambda i: (0, 0))],
        out_specs=[pl.BlockSpec((tm, 2 * dc), lambda i: (i, 0)), row, row, pl.BlockSpec((tm, 1), lambda i: (i, 0))],
        out_shape=[_sds((rows, 2 * dc), BF16), _sds((rows, d), F32), _sds((rows, d), BF16), _sds((rows, 1), F32)],
        compiler_params=_params(),
    )(zc, ysc, wout, hs, lg, lb, gn)


def _wout_bwd_dy(do, wout, zc, lg, lb, tm, name, comm=()):
    rows, d = do.shape
    dc = zc.shape[1]
    nt = rows // tm

    def body(do_ref, w_ref, zc_ref, lg_ref, lb_ref, dysc_ref, dzc_ref, dlg_ref, dlb_ref, dy_ref):
        dy_ref[...] = lax.dot_general(do_ref[...], w_ref[...], NT, preferred_element_type=F32)
        dlb_ref[0] = jnp.zeros((1, dc), F32)
        dlg_ref[0] = jnp.zeros((1, dc), F32)

        def group(rs, _):
            dysc_ref[rs, :] = dy_ref[rs, :dc].astype(BF16)
            nrm, rstd, lin, s = _layernorm_silu(zc_ref[rs, :], lg_ref[...], lb_ref[...])
            dl = dy_ref[rs, dc:] * (s * (1.0 + lin * (1.0 - s)))
            dlb_ref[0] += jnp.sum(dl, axis=0, keepdims=True)
            dlg_ref[0] += jnp.sum(dl * nrm, axis=0, keepdims=True)
            dn = dl * lg_ref[...]
            dzc_ref[rs, :] = rstd * (dn - jnp.mean(dn, axis=-1, keepdims=True)
                                     - nrm * jnp.mean(dn * nrm, axis=-1, keepdims=True))

        _for_row_groups(tm, group)

    half = pl.BlockSpec((tm, dc), lambda i: (i, 0))
    vec_c = pl.BlockSpec((1, dc), lambda i: (0, 0))
    part = pl.BlockSpec((1, 1, dc), lambda i: (i, 0, 0))
    return _call(
        body, comm=comm, name=name, grid=(nt,),
        in_specs=[pl.BlockSpec((tm, d), lambda i: (i, 0)), _once((2 * dc, d), lambda i: (0, 0)), half, vec_c, vec_c],
        out_specs=[half, half, part, part],
        out_shape=[_sds((rows, dc), BF16), _sds((rows, dc), F32), _sds((nt, 1, dc), F32), _sds((nt, 1, dc), F32)],
        scratch_shapes=[pltpu.VMEM((tm, 2 * dc), F32)], compiler_params=_params(),
    )(do, wout, zc, lg, lb)


def _wout_bwd_dw(y, do, nb, tk, name, comm=()):
    rows, k2 = y.shape
    d = do.shape[1]
    nk = rows // tk

    def body(y_ref, do_ref, dw_ref, acc):
        k = pl.program_id(0)

        @pl.when(k == 0)
        def _():
            acc[...] = jnp.zeros_like(acc)

        acc[...] += lax.dot_general(y_ref[...], do_ref[...], TN, preferred_element_type=F32)

        @pl.when(k == nk - 1)
        def _():
            dw_ref[...] = acc[...].astype(BF16)

    return _call(
        body, comm=comm, name=name, grid=(nk,),
        in_specs=[pl.BlockSpec((tk, k2), lambda k: (k, 0)), pl.BlockSpec((tk, d), lambda k: (k, 0))],
        out_specs=_once((k2, d), lambda k: (0, 0)), out_shape=_sds((k2, d), BF16),
        scratch_shapes=[pltpu.VMEM((k2, d), F32)], compiler_params=_params(),
    )(y, do).reshape(nb, k2 // nb, d)


def _windows(win, n_res):
    length = win.shape[0]
    return [win if r == 0 else pltpu.roll(win, length - r, 0) for r in range(n_res)]


def _taps(src_ref, start, offsets, ch):
    span = -(-(max(offsets) + ch) // SUBLANES) * SUBLANES
    win = src_ref[pl.ds(start, span), :]
    shifted = _windows(win, min(SUBLANES, max(offsets) + 1))
    return [shifted[o % SUBLANES][(o // SUBLANES) * SUBLANES:(o // SUBLANES) * SUBLANES + ch] for o in offsets]


def _rows8(v):
    return jnp.sum(v.reshape(v.shape[0] // SUBLANES, SUBLANES, v.shape[1]), axis=0)


def _conv_geometry(n_seq, n_meta):
    base = CONV_PAD + n_meta
    off_cf = base - (CF_WIDTH - 1)
    off_sc = base - (SC_WIDTH - 1)
    logical = -(-(n_meta + n_seq) // CONV_CH) * CONV_CH
    return base, off_cf, off_sc, logical


def _fill_conv_inputs(c_ref, v_ref, a_ref, g_ref, scv, sz, n_seq, n_meta):
    base = CONV_PAD + n_meta
    cb = scv.shape[1]
    scv[0:CONV_PAD, :] = jnp.zeros((CONV_PAD, cb), F32)
    sz[0:CONV_PAD, :] = jnp.zeros((CONV_PAD, cb), F32)

    def put(src, dst, n):
        cv = c_ref[src, :].astype(F32) * v_ref[src, :].astype(F32)
        scv[dst, :] = cv
        sz[dst, :] = a_ref[src, :].astype(F32) * _sigmoid(g_ref[src, :].astype(F32))

    put(pl.ds(n_seq, n_meta), pl.ds(CONV_PAD, n_meta), n_meta)

    def chunk(i, carry):
        t0 = pl.multiple_of(i * CONV_CH, CONV_CH)
        put(pl.ds(t0, CONV_CH), pl.ds(base + t0, CONV_CH), CONV_CH)
        return carry

    lax.fori_loop(0, n_seq // CONV_CH, chunk, 0)


def _conv_fwd(u, wsc, wcf, cbias, n_seq, n_meta, name, comm=()):
    rows = u.shape[0]
    nb, _, cb = wsc.shape
    dc = nb * cb
    base, off_cf, off_sc, _ = _conv_geometry(n_seq, n_meta)
    a_cf, a_sc = off_cf // SUBLANES * SUBLANES, off_sc // SUBLANES * SUBLANES
    ch = CONV_CH

    def body(b_ref, c_ref, v_ref, a_ref, g_ref, wsc_ref, wcf_ref, cb_ref, ysc_ref, zc_ref, scv, sz):
        _fill_conv_inputs(c_ref, v_ref, a_ref, g_ref, scv, sz, n_seq, n_meta)
        w3, w31, bias = wsc_ref[0], wcf_ref[0], cb_ref[...]

        def chunk(i, carry):
            t0 = pl.multiple_of(i * ch, ch)
            acc = jnp.zeros((ch, cb), F32)
            for k, win in enumerate(_taps(sz, t0 + a_cf, [off_cf - a_cf + k for k in range(CF_WIDTH)], ch)):
                acc = acc + win * w31[k:k + 1, :]
            zc_ref[pl.ds(t0, ch), :] = acc + bias
            s = jnp.zeros((ch, cb), F32)
            for k, win in enumerate(_taps(scv, t0 + a_sc, [off_sc - a_sc + k for k in range(SC_WIDTH)], ch)):
                s = s + win * w3[k:k + 1, :]
            ysc_ref[pl.ds(t0, ch), :] = (b_ref[pl.ds(t0, ch), :].astype(F32) * s).astype(BF16)
            return carry

        lax.fori_loop(0, n_seq // ch, chunk, 0)
        ysc_ref[n_seq:rows, :] = jnp.zeros((rows - n_seq, cb), BF16)
        zc_ref[n_seq:rows, :] = jnp.zeros((rows - n_seq, cb), F32)

    ucol = [pl.BlockSpec((rows, cb), functools.partial(lambda m, j: (0, m * nb + j), m)) for m in range(5)]
    blk = pl.BlockSpec((rows, cb), lambda j: (0, j))
    return _call(
        body, comm=comm, name=name, grid=(nb,),
        in_specs=ucol + [pl.BlockSpec((1, SC_WIDTH, cb), lambda j: (j, 0, 0)),
                         pl.BlockSpec((1, CF_WIDTH, cb), lambda j: (j, 0, 0)), pl.BlockSpec((1, cb), lambda j: (0, j))],
        out_specs=[blk, blk], out_shape=[_sds((rows, dc), BF16), _sds((rows, dc), F32)],
        scratch_shapes=[pltpu.VMEM((base + n_seq, cb), F32)] * 2, compiler_params=_params(),
    )(u, u, u, u, u, wsc, wcf, cbias)


def _conv_bwd(u, dysc, dzc, wsc, wcf, n_seq, n_meta, name, comm=()):
    rows = u.shape[0]
    nb, _, cb = wsc.shape
    dc = nb * cb
    base, off_cf, off_sc, logical = _conv_geometry(n_seq, n_meta)
    a_cf, a_sc = off_cf // SUBLANES * SUBLANES, off_sc // SUBLANES * SUBLANES
    ch = CONV_CH
    tail = CONV_PAD

    def body(b_ref, c_ref, v_ref, a_ref, g_ref, dysc_ref, dzc_ref, wsc_ref, wcf_ref,
             du_ref, dwsc_ref, dwcf_ref, dcb_ref,
             scv, sz, sds_, sdz, dlcv, dlz, accsc, acccf, accb):
        _fill_conv_inputs(c_ref, v_ref, a_ref, g_ref, scv, sz, n_seq, n_meta)
        w3, w31 = wsc_ref[0], wcf_ref[0]
        dub_ref, duc_ref, duv_ref, dua_ref, dug_ref = (du_ref.at[m] for m in range(5))
        sds_[0:n_meta, :] = jnp.zeros((n_meta, cb), F32)
        sdz[0:n_meta, :] = jnp.zeros((n_meta, cb), F32)
        behind = logical + tail - (n_meta + n_seq)
        sds_[n_meta + n_seq:logical + tail, :] = jnp.zeros((behind, cb), F32)
        sdz[n_meta + n_seq:logical + tail, :] = jnp.zeros((behind, cb), F32)
        accsc[...] = jnp.zeros_like(accsc)
        acccf[...] = jnp.zeros_like(acccf)
        accb[...] = jnp.zeros_like(accb)

        def forward_chunk(i, carry):
            t0 = pl.multiple_of(i * ch, ch)
            rws = pl.ds(t0, ch)
            dy = dysc_ref[rws, :].astype(F32)
            ds = dy * b_ref[rws, :].astype(F32)
            dz = dzc_ref[rws, :]
            sds_[pl.ds(n_meta + t0, ch), :] = ds
            sdz[pl.ds(n_meta + t0, ch), :] = dz
            s = jnp.zeros((ch, cb), F32)
            for k, win in enumerate(_taps(scv, t0 + a_sc, [off_sc - a_sc + k for k in range(SC_WIDTH)], ch)):
                s = s + win * w3[k:k + 1, :]
                accsc[k * SUBLANES:(k + 1) * SUBLANES, :] += _rows8(ds * win)
            dub_ref[rws, :] = (dy * s).astype(BF16)
            for k, win in enumerate(_taps(sz, t0 + a_cf, [off_cf - a_cf + k for k in range(CF_WIDTH)], ch)):
                acccf[k * SUBLANES:(k + 1) * SUBLANES, :] += _rows8(dz * win)
            accb[...] += _rows8(dz)
            return carry

        lax.fori_loop(0, n_seq // ch, forward_chunk, 0)

        def backward_chunk(i, carry):
            p0 = pl.multiple_of(i * ch, ch)
            dcv = jnp.zeros((ch, cb), F32)
            for k, win in enumerate(_taps(sds_, p0, [SC_WIDTH - 1 - k for k in range(SC_WIDTH)], ch)):
                dcv = dcv + win * w3[k:k + 1, :]
            dlcv[pl.ds(p0, ch), :] = dcv
            dzi = jnp.zeros((ch, cb), F32)
            for k, win in enumerate(_taps(sdz, p0, [CF_WIDTH - 1 - k for k in range(CF_WIDTH)], ch)):
                dzi = dzi + win * w31[k:k + 1, :]
            dlz[pl.ds(p0, ch), :] = dzi
            return carry

        lax.fori_loop(0, logical // ch, backward_chunk, 0)

        def gates(phys, logi):
            dcv, dzi = dlcv[logi, :], dlz[logi, :]
            duc_ref[phys, :] = (dcv * v_ref[phys, :].astype(F32)).astype(BF16)
            duv_ref[phys, :] = (dcv * c_ref[phys, :].astype(F32)).astype(BF16)
            s = _sigmoid(g_ref[phys, :].astype(F32))
            dua_ref[phys, :] = (dzi * s).astype(BF16)
            dug_ref[phys, :] = (dzi * a_ref[phys, :].astype(F32) * s * (1.0 - s)).astype(BF16)

        def gate_chunk(i, carry):
            t0 = pl.multiple_of(i * ch, ch)
            gates(pl.ds(t0, ch), pl.ds(n_meta + t0, ch))
            return carry

        lax.fori_loop(0, n_seq // ch, gate_chunk, 0)
        gates(pl.ds(n_seq, n_meta), pl.ds(0, n_meta))
        dub_ref[n_seq:rows, :] = jnp.zeros((rows - n_seq, cb), BF16)
        pad0 = n_seq + n_meta
        for ref in (duc_ref, duv_ref, dua_ref, dug_ref):
            ref[pad0:rows, :] = jnp.zeros((rows - pad0, cb), BF16)
        dwsc_ref[0] = jnp.sum(accsc[...].reshape(SC_WIDTH, SUBLANES, cb), axis=1)
        dwcf_ref[0] = jnp.sum(acccf[...].reshape(CF_WIDTH, SUBLANES, cb), axis=1)
        dcb_ref[...] = jnp.sum(accb[...], axis=0, keepdims=True)

    ucol = [pl.BlockSpec((rows, cb), functools.partial(lambda m, j: (0, m * nb + j), m)) for m in range(5)]
    blk = pl.BlockSpec((dysc.shape[0], cb), lambda j: (0, j))
    wsc_spec = pl.BlockSpec((1, SC_WIDTH, cb), lambda j: (j, 0, 0))
    wcf_spec = pl.BlockSpec((1, CF_WIDTH, cb), lambda j: (j, 0, 0))
    outs = _call(
        body, comm=comm, name=name, grid=(nb,),
        in_specs=ucol + [blk, blk, wsc_spec, wcf_spec],
        out_specs=[pl.BlockSpec((5, rows, cb), lambda j: (0, 0, j)), wsc_spec, wcf_spec,
                   pl.BlockSpec((1, cb), lambda j: (0, j))],
        out_shape=[_sds((5, rows, dc), BF16), _sds((nb, SC_WIDTH, cb), F32), _sds((nb, CF_WIDTH, cb), F32),
                   _sds((1, dc), F32)],
        scratch_shapes=[pltpu.VMEM((base + n_seq, cb), F32)] * 2 + [pltpu.VMEM((logical + tail, cb), F32)] * 2
        + [pltpu.VMEM((logical, cb), F32)] * 2
        + [pltpu.VMEM((SC_WIDTH * SUBLANES, cb), F32), pltpu.VMEM((CF_WIDTH * SUBLANES, cb), F32),
           pltpu.VMEM((SUBLANES, cb), F32)],
        compiler_params=_params(),
    )(u, u, u, u, u, dysc, dzc, wsc, wcf)
    return outs


def _row_tile(rows, cols):
    return rows // 4 if rows % 64 == 0 and rows * cols >= (1 << 18) else rows


def _pair_sum(grad, sib, idx, name, comm=()):
    _, rows, cols = grad.shape
    tr = rows

    def body(idx_ref, g_ref, s_ref, o_ref):
        o_ref[0] = (g_ref[0].astype(F32) + s_ref[0].astype(F32)).astype(o_ref.dtype)

    return _call(
        body, name=name,
        grid_spec=pltpu.PrefetchScalarGridSpec(
            num_scalar_prefetch=1, grid=(4, rows // tr),
            in_specs=[pl.BlockSpec((1, tr, cols), lambda k, i, idx_ref: (idx_ref[k], i, 0)),
                      pl.BlockSpec((1, tr, cols), lambda k, i, idx_ref: (idx_ref[4 + k], i, 0))],
            out_specs=pl.BlockSpec((1, tr, cols), lambda k, i, idx_ref: (k, i, 0))),
        out_shape=_sds((4, rows, cols), grad.dtype), compiler_params=_params(),
    )(idx, grad, sib)


def _adamw_math(w, g, m, v):
    m = ADAM_B1 * m + (1.0 - ADAM_B1) * g
    v = ADAM_B2 * v + (1.0 - ADAM_B2) * (g * g)
    m_hat = m / (1.0 - ADAM_B1 ** ADAM_STEP)
    v_hat = v / (1.0 - ADAM_B2 ** ADAM_STEP)
    delta = -ADAM_LR * (m_hat / (jnp.sqrt(v_hat) + ADAM_EPS) + ADAM_WD * w)
    return delta, m, v


def _adamw_sharded(own, got, w, m, v, name, comm=()):
    rows, cols = w.shape
    tr = _row_tile(rows, cols)

    def body(own_ref, g0_ref, g1_ref, g2_ref, w_ref, m_ref, v_ref, g_ref, d_ref, nm_ref, nv_ref):
        g = own_ref[0].astype(F32) + g0_ref[0].astype(F32) + g1_ref[0].astype(F32) + g2_ref[0].astype(F32)
        delta, nm, nv = _adamw_math(w_ref[...], g, m_ref[...], v_ref[...])
        g_ref[...] = g
        d_ref[...] = delta
        nm_ref[...] = nm
        nv_ref[...] = nv

    flat = pl.BlockSpec((tr, cols), lambda i: (i, 0))
    slot = [pl.BlockSpec((1, tr, cols), functools.partial(lambda k, i: (k, i, 0), k)) for k in range(3)]
    return _call(
        body, comm=comm, name=name, grid=(rows // tr,),
        in_specs=[slot[0]] + slot + [flat] * 3, out_specs=[flat] * 4, out_shape=[_sds((rows, cols), F32)] * 4,
        compiler_params=_params(),
    )(own, got, got, got, w, m, v)


def _adamw_replicated(gathered, segs, ws, ms, vs, loss_scale, name, comm=()):
    n = len(ws)

    def body(*refs):
        gat = refs[0]
        w_refs, m_refs, v_refs = refs[1:1 + n], refs[1 + n:1 + 2 * n], refs[1 + 2 * n:1 + 3 * n]
        outs = refs[1 + 3 * n:]

        def total(off, width):
            s = gat[0, :, off:off + width]
            for k in range(1, N_DEV):
                s = s + gat[k, :, off:off + width]
            return s

        outs[0][...] = loss_scale * total(segs[n][0], segs[n][1])
        for p in range(n):
            g = total(*segs[p])
            delta, nm, nv = _adamw_math(w_refs[p][...], g, m_refs[p][...], v_refs[p][...])
            for q, val in enumerate((g, delta, nm, nv)):
                outs[1 + 4 * p + q][...] = val
        for e, seg in enumerate(segs[n + 1:]):
            outs[1 + 4 * n + e][...] = total(*seg)

    return _call(
        body, name=name,
        out_shape=[_sds((1, segs[n][1]), F32)] + [_sds(w.shape, F32) for w in ws for _ in range(4)]
        + [_sds((1, seg[1]), F32) for seg in segs[n + 1:]],
        compiler_params=_params(),
    )(gathered, *ws, *ms, *vs)


def _adamw_plain(g, w, m, v, name):
    def body(g_ref, w_ref, m_ref, v_ref, d_ref, nm_ref, nv_ref):
        d_ref[...], nm_ref[...], nv_ref[...] = _adamw_math(w_ref[...], g_ref[...], m_ref[...], v_ref[...])

    return list(_call(body, name=name, out_shape=[_sds(w.shape, F32)] * 3)(g, w, m, v))


REPLICATED = ("ffn1_norm", "mix_norm", "b_in", "conv_cf_b", "ln_cf_g", "ln_cf_b", "ffn2_norm", "final_norm")
SHARDED = ("meta_tokens", "ffn1_w_gate", "ffn1_w_up", "ffn1_w_down", "w_in", "conv_sc_w", "conv_cf_w", "w_out",
           "ffn2_w_gate", "ffn2_w_up", "ffn2_w_down")
WEIGHTS = ("meta_tokens", "ffn1_norm", "ffn1_w_gate", "ffn1_w_up", "ffn1_w_down", "mix_norm", "w_in", "b_in",
           "conv_sc_w", "conv_cf_w", "conv_cf_b", "ln_cf_g", "ln_cf_b", "w_out", "ffn2_norm", "ffn2_w_gate",
           "ffn2_w_up", "ffn2_w_down", "final_norm")


TRANSPOSED = ("ffn1_w_gate", "ffn1_w_up", "ffn2_w_gate", "ffn2_w_up")


def _blocks2d(k, a):
    a = a.reshape(a.shape[-2:]) if a.ndim >= 2 else a.reshape(1, -1)
    return a.T if k in TRANSPOSED else a


def _step(x, tgt, w, m, v):
    n_seq, d = x.shape[1], x.shape[2]
    n_meta = w["meta_tokens"].shape[0]
    rows = -(-(n_seq + n_meta) // ROW_ALIGN) * ROW_ALIGN
    tm = rows // N_ROW_TILES
    cb = w["conv_sc_w"].shape[-1]
    dc = cb * N_DEV
    w2 = {k: _blocks2d(k, a) for k, a in w.items()}
    m2 = {k: _blocks2d(k, a) for k, a in m.items()}
    v2 = {k: _blocks2d(k, a) for k, a in v.items()}

    def as_given(k, r):
        return (r.T if k in TRANSPOSED else r).reshape(w[k].shape)

    def cast(k):
        return w2[k].astype(BF16)

    full = dict(zip(("ffn1_w_gate", "meta_tokens", "conv_sc_w", "conv_cf_w"), _exchange_alone(
        _gather_all([cast("ffn1_w_gate"), w2["meta_tokens"], w2["conv_sc_w"], w2["conv_cf_w"]]), "gather_gate1")))
    meta = jnp.transpose(full["meta_tokens"], (1, 0, 2)).reshape(n_meta, d)
    hs0 = jnp.concatenate([x[0], meta, jnp.zeros((rows - n_seq - n_meta, d), F32)], axis=0)

    tx = n_seq // N_ROW_TILES
    h1, r1 = _rms_fwd(hs0, w2["ffn1_norm"], tm, "rms_in")
    up1 = _gather_all([cast("ffn1_w_up")])
    g1 = _ffn_proj(h1, full["ffn1_w_gate"], tm, "ffn1_gate", comm=[up1])
    down1 = _gather_all([cast("ffn1_w_down")])
    u1 = _ffn_proj(h1, up1.results[0], tm, "ffn1_up", comm=[down1])
    w_in_all, w_out1 = _gather_all([cast("w_in")]), _gather_direct([cast("w_out")])
    hs1, h2, r2 = _ffn_down_norm(g1, u1, down1.results[0], hs0, w2["mix_norm"], rows // DOWN_ROW_TILES, "ffn1_down",
                                 comm=[w_in_all, w_out1])
    win = jnp.transpose(w_in_all.results[0], (1, 0, 2)).reshape(d, -1)
    w_out2, gate2 = _gather_forward(w_out1.results), _gather_direct([cast("ffn2_w_gate")])
    u = _win_fwd(h2, win, w2["b_in"], dc, tm, "mix_in", comm=[w_out2, gate2])
    wout = w_out2.results[0].reshape(-1, d)
    gate2p = _gather_forward(gate2.results)
    ysc, zc = _conv_fwd(u, full["conv_sc_w"], full["conv_cf_w"], w2["conv_cf_b"], n_seq, n_meta, "conv_fwd", comm=[gate2p])
    up2 = _gather_all([cast("ffn2_w_up")])
    y, hs2, h3, r3 = _wout_fwd(zc, ysc, wout, hs1, w2["ln_cf_g"], w2["ln_cf_b"], w2["ffn2_norm"], n_seq, tx, "mix_out",
                               comm=[up2])
    down2 = _gather_all([cast("ffn2_w_down")])
    full.update(ffn1_w_up=up1.results[0], ffn1_w_down=down1.results[0], ffn2_w_gate=gate2p.results[0],
                ffn2_w_up=up2.results[0])
    g2, u2 = _ffn_gu(h3, full["ffn2_w_gate"], full["ffn2_w_up"], tx, "ffn2_gu", comm=[down2])
    full["ffn2_w_down"] = down2.results[0]
    dhs3, dhs3b, loss_p, dgf_p = _ffn_down_loss(
        g2, u2, full["ffn2_w_down"], hs2, w2["final_norm"], tgt[0], n_seq, n_seq // DOWN_ROW_TILES, "ffn2_down_loss")

    xi, yi, ci = lax.axis_index("x"), lax.axis_index("y"), lax.axis_index("c")
    chip_of = [2 * xi + yi, 2 * (1 - xi) + yi, 2 * xi + (1 - yi), 2 * (1 - xi) + (1 - yi)]
    idx = jnp.stack([2 * ch + ci for ch in chip_of] + chip_of).astype(jnp.int32)
    out = {}

    def to_pairs(parts):
        return _pair_exchange([p.reshape((4, 2) + p.shape[1:]) for p in parts])

    def pair_sums(names, parts, pairs):
        return [_pair_sum(p, s, idx, "pair_sum_" + k) for k, p, s in zip(names, parts, pairs.results)]

    def update(names, sums, chips):
        for k, own, got in zip(names, sums, chips.results):
            res = _adamw_sharded(own, got, w2[k], m2[k], v2[k], "adamw_" + k)
            out[k] = [as_given(k, r) for r in res]

    tk = min(2 * ROW_ALIGN, n_seq)
    th_x, th_r = n_seq // DOWN_ROW_TILES, rows // DOWN_ROW_TILES

    dwd2 = _ffn_bwd_dw_down(g2, u2, dhs3b, tk, "ffn2_bwd_dw_down")
    p_d2 = to_pairs([dwd2])
    dg2, du2 = _ffn_bwd_da(dhs3b, full["ffn2_w_down"], g2, u2, tx, "ffn2_bwd_da", comm=[p_d2])
    s_d2 = pair_sums(["ffn2_w_down"], [dwd2], p_d2)
    c_d2 = _chip_exchange(s_d2)
    gu2 = _ffn_bwd_dw_gu(h3, dg2, du2, tk, "ffn2_bwd_dw_gu", comm=[c_d2])
    update(["ffn2_w_down"], s_d2, c_d2)
    p_gu2 = to_pairs(gu2)
    half2 = _ffn_bwd_dh_gate(dg2, full["ffn2_w_gate"], th_x, "ffn2_bwd_dh_gate", comm=[p_gu2])
    s_gu2 = pair_sums(["ffn2_w_gate", "ffn2_w_up"], gu2, p_gu2)
    c_g2, c_u2 = _chip_exchange(s_gu2[:1]), _chip_exchange(s_gu2[1:])
    dhs2, dhs2b, dn3_p = _ffn_bwd_dh(du2, full["ffn2_w_up"], half2, hs2, r3, w2["ffn2_norm"], dhs3, th_x, "ffn2_bwd_dh",
                                     zeros=jnp.zeros((rows, d), F32))
    dysc, dzc, dlg_p, dlb_p = _wout_bwd_dy(dhs2b, wout, zc, w2["ln_cf_g"], w2["ln_cf_b"], tx, "mix_out_bwd_dy")

    dwout = _wout_bwd_dw(y, dhs2b, N_DEV, tk, "mix_out_bwd_dw")
    p_wo = to_pairs([dwout])
    du, dcsw, dccw, dcb = _conv_bwd(u, dysc, dzc, full["conv_sc_w"], full["conv_cf_w"], n_seq, n_meta, "conv_bwd",
                                    comm=[c_g2, p_wo])
    update(["ffn2_w_gate"], s_gu2[:1], c_g2)
    s_wo = pair_sums(["w_out"], [dwout], p_wo)
    c_wo = _chip_exchange(s_wo)
    dhs1, dhs1b, dn2_p = _win_bwd_dh(du, win, hs1, r2, w2["mix_norm"], dhs2, th_r, "mix_in_bwd_dh", comm=[c_u2])
    update(["ffn2_w_up"], s_gu2[1:], c_u2)
    dwin, dbin = _win_bwd_dw(h2, du, tk, "mix_in_bwd_dw", comm=[c_wo])
    update(["w_out"], s_wo, c_wo)
    mixer = ("w_in", "conv_sc_w", "conv_cf_w")
    gradsm = [jnp.transpose(dwin.reshape(d, N_DEV, -1), (1, 0, 2)), dcsw, dccw]
    p_m = to_pairs(gradsm)
    dwd1 = _ffn_bwd_dw_down(g1, u1, dhs1b, tk, "ffn1_bwd_dw_down", comm=[p_m])
    s_m = pair_sums(mixer, gradsm, p_m)
    c_m, p_d1 = _chip_exchange(s_m), to_pairs([dwd1])
    dg1, du1 = _ffn_bwd_da(dhs1b, full["ffn1_w_down"], g1, u1, tm, "ffn1_bwd_da", comm=[c_m, p_d1])
    update(mixer, s_m, c_m)
    s_d1 = pair_sums(["ffn1_w_down"], [dwd1], p_d1)
    c_d1 = _chip_exchange(s_d1)
    gu1 = _ffn_bwd_dw_gu(h1, dg1, du1, tk, "ffn1_bwd_dw_gu", comm=[c_d1])
    update(["ffn1_w_down"], s_d1, c_d1)
    p_gu1 = to_pairs(gu1)
    _exchange_alone(p_gu1, "reduce_pair_ffn1")
    s_gu1 = pair_sums(["ffn1_w_gate", "ffn1_w_up"], gu1, p_gu1)
    c_g1, c_u1 = _chip_exchange(s_gu1[:1]), _chip_exchange(s_gu1[1:])
    half1 = _ffn_bwd_dh_gate(dg1, full["ffn1_w_gate"], th_r, "ffn1_bwd_dh_gate", comm=[c_g1])
    dhs0, _, dn1_p = _ffn_bwd_dh(du1, full["ffn1_w_up"], half1, hs0, r1, w2["ffn1_norm"], dhs1, th_r, "ffn1_bwd_dh",
                                 comm=[c_u1])
    update(["ffn1_w_gate"], s_gu1[:1], c_g1)
    update(["ffn1_w_up"], s_gu1[1:], c_u1)
    grad_x = dhs0[:n_seq][None]

    partial = {
        "ffn1_norm": dn1_p.sum(0), "mix_norm": dn2_p.sum(0), "b_in": dbin, "conv_cf_b": dcb,
        "ln_cf_g": dlg_p.sum(0), "ln_cf_b": dlb_p.sum(0), "ffn2_norm": dn3_p.sum(0), "final_norm": dgf_p.sum(0),
    }
    loss_seg = jnp.pad(loss_p.sum((0, 2)).reshape(1, 1), ((0, 0), (0, 127)))
    pieces = [partial[k] for k in REPLICATED] + [loss_seg, dhs0[n_seq:n_seq + n_meta].reshape(1, n_meta * d)]
    segs, off = [], 0
    for p in pieces:
        segs.append((off, p.shape[1]))
        off += p.shape[1]
    rows8 = _gather_rows(jnp.concatenate(pieces, axis=1), "gather_small")
    res = _adamw_replicated(rows8, segs, [w2[k] for k in REPLICATED], [m2[k] for k in REPLICATED],
                            [v2[k] for k in REPLICATED], 0.5 / d, "adamw_replicated")
    loss = res[0][0, 0]
    for p, k in enumerate(REPLICATED):
        out[k] = [r.reshape(w[k].shape) for r in res[1 + 4 * p:5 + 4 * p]]
    ob = w2["meta_tokens"].shape[1]
    gmeta = lax.dynamic_slice_in_dim(res[-1].reshape(n_meta, d), (4 * xi + 2 * yi + ci) * ob, ob, axis=1)
    out["meta_tokens"] = [gmeta] + _adamw_plain(gmeta, w2["meta_tokens"], m2["meta_tokens"], v2["meta_tokens"], "adamw_meta_tokens")

    return (loss, grad_x, *[out[k][0] for k in WEIGHTS], *[out[k][1] for k in WEIGHTS],
            *[out[k][2] for k in WEIGHTS], *[out[k][3] for k in WEIGHTS])


def kernel(x, meta_tokens, ffn1_norm, ffn1_w_gate, ffn1_w_up, ffn1_w_down, mix_norm, w_in, b_in, conv_sc_w, conv_cf_w, conv_cf_b, ln_cf_g, ln_cf_b, w_out, ffn2_norm, ffn2_w_gate, ffn2_w_up, ffn2_w_down, final_norm, loss_target, m_meta_tokens, m_ffn1_norm, m_ffn1_w_gate, m_ffn1_w_up, m_ffn1_w_down, m_mix_norm, m_w_in, m_b_in, m_conv_sc_w, m_conv_cf_w, m_conv_cf_b, m_ln_cf_g, m_ln_cf_b, m_w_out, m_ffn2_norm, m_ffn2_w_gate, m_ffn2_w_up, m_ffn2_w_down, m_final_norm, v_meta_tokens, v_ffn1_norm, v_ffn1_w_gate, v_ffn1_w_up, v_ffn1_w_down, v_mix_norm, v_w_in, v_b_in, v_conv_sc_w, v_conv_cf_w, v_conv_cf_b, v_ln_cf_g, v_ln_cf_b, v_w_out, v_ffn2_norm, v_ffn2_w_gate, v_ffn2_w_up, v_ffn2_w_down, v_final_norm):
    given = dict(locals())
    w = {k: given[k] for k in WEIGHTS}
    m = {k: given["m_" + k] for k in WEIGHTS}
    v = {k: given["v_" + k] for k in WEIGHTS}
    return _step(x, loss_target, w, m, v)
```

```python
import functools

import jax
import jax.numpy as jnp
from jax import lax
from jax.experimental import pallas as pl
from jax.experimental.pallas import tpu as pltpu

F32 = jnp.float32
BF16 = jnp.bfloat16
EPS = 1e-6
FFN_RES_SCALE = 0.5
SC_WIDTH = 3
CF_WIDTH = 31
ADAM_LR = 0.001
ADAM_B1 = 0.9
ADAM_B2 = 0.999
ADAM_EPS = 1e-08
ADAM_WD = 0.01
ADAM_STEP = 10

N_DEV = 8
N_ROW_TILES = 8
ROW_ALIGN = 256
CONV_PAD = 32
CONV_CH = 128
SUB_ROWS = 32
DOWN_ROW_TILES = 16
GATHER_PARTS = 2
SUBLANES = 8
BF16_ROWS = 16
VMEM_LIMIT = 56 * 1024 * 1024
MESH = pl.DeviceIdType.MESH
ANY = pl.BlockSpec(memory_space=pl.ANY)

NT = (((1,), (1,)), ((), ()))
TN = (((0,), (0,)), ((), ()))


def _pallas(body, **kw):
    return pl.pallas_call(body, **kw)


class _Exchange:
    def __init__(self, inputs, out_shapes, sem_shapes, start, finish, aliases=None):
        self.inputs, self.out_shapes, self.sem_shapes = list(inputs), list(out_shapes), list(sem_shapes)
        self.start, self.finish, self.aliases = start, finish, dict(aliases or {})
        self.results = None


def _call(body, comm=(), **kw):
    if not comm:
        return _pallas(body, **kw)
    grid = kw.pop("grid")
    in_specs = list(kw.pop("in_specs"))
    out_specs, out_shape = kw.pop("out_specs"), kw.pop("out_shape")
    scratch = list(kw.pop("scratch_shapes", []))
    single = not isinstance(out_shape, (list, tuple))
    out_specs, out_shape = ([out_specs], [out_shape]) if single else (list(out_specs), list(out_shape))
    n_in, n_out, n_scr = len(in_specs), len(out_shape), len(scratch)
    c_in = [a for job in comm for a in job.inputs]
    c_out = [s for job in comm for s in job.out_shapes]
    c_sem = [s for job in comm for s in job.sem_shapes]
    aliases, i0, o0 = dict(kw.pop("input_output_aliases", {})), n_in, n_out
    for job in comm:
        aliases.update({i0 + i: o0 + o for i, o in job.aliases.items()})
        i0, o0 = i0 + len(job.inputs), o0 + len(job.out_shapes)

    def hosted(*refs):
        pos = [0]

        def take(n):
            pos[0] += n
            return refs[pos[0] - n:pos[0]]

        ins, cins, outs, couts, scr, sems = take(n_in), take(len(c_in)), take(n_out), take(len(c_out)), take(n_scr), take(len(c_sem))
        ids = [pl.program_id(k) for k in range(len(grid))]
        first = functools.reduce(jnp.logical_and, [i == 0 for i in ids])
        last = functools.reduce(jnp.logical_and, [i == g - 1 for i, g in zip(ids, grid)])

        def each(phase):
            i, o, s = 0, 0, 0
            for job in comm:
                ni, no, ns = len(job.inputs), len(job.out_shapes), len(job.sem_shapes)
                getattr(job, phase)(cins[i:i + ni], couts[o:o + no], sems[s:s + ns])
                i, o, s = i + ni, o + no, s + ns

        @pl.when(first)
        def _():
            each("start")

        body(*ins, *outs, *scr)

        @pl.when(last)
        def _():
            each("finish")

    call = _pallas(
        hosted, grid=grid, in_specs=in_specs + [ANY] * len(c_in), out_specs=out_specs + [ANY] * len(c_out),
        out_shape=out_shape + c_out, scratch_shapes=scratch + c_sem, input_output_aliases=aliases, **kw)

    def run(*args):
        res = call(*args, *c_in)
        o = n_out
        for job in comm:
            job.results = list(res[o:o + len(job.out_shapes)])
            o += len(job.out_shapes)
        return res[0] if single else list(res[:n_out])

    return run


def _exchange_alone(job, name, comm=()):
    n_in, n_out = len(job.inputs), len(job.out_shapes)

    def body(*refs):
        ins, outs, sems = refs[:n_in], refs[n_in:n_in + n_out], refs[n_in + n_out:]
        job.start(ins, outs, sems)
        job.finish(ins, outs, sems)

    res = _pallas(body, name=name, in_specs=[ANY] * n_in, out_specs=[ANY] * n_out, out_shape=job.out_shapes,
                  scratch_shapes=job.sem_shapes, input_output_aliases=job.aliases)(*job.inputs)
    job.results = list(res)
    return job.results


def _params(**kw):
    return pltpu.CompilerParams(vmem_limit_bytes=VMEM_LIMIT, **kw)


def _sigmoid(x):
    return 0.5 * jnp.tanh(0.5 * x) + 0.5


def _sds(shape, dtype):
    return jax.ShapeDtypeStruct(shape, dtype)


def _place():
    x, y, c = lax.axis_index("x"), lax.axis_index("y"), lax.axis_index("c")
    chips = [(1 - x, y), (x, 1 - y), (1 - x, 1 - y)]
    return x, y, c, chips


def _slot(ref, p):
    return ref.at[4 * p[0] + 2 * p[1] + p[2]]


def _remote(src, dst, send_sem, recv_sem, to):
    return pltpu.make_async_remote_copy(src_ref=src, dst_ref=dst, send_sem=send_sem, recv_sem=recv_sem,
                                        device_id=to, device_id_type=MESH)


def _gather_direct(arrs):
    n = len(arrs)

    def copies(ins, outs, sems):
        send_sems, recv_sems, local_sems = sems
        x, y, c, chips = _place()
        me = (x, y, c)
        peers = [(x, y, 1 - c)] + [(*chip, c) for chip in chips]
        local = [pltpu.make_async_copy(ins[a], _slot(outs[a], me), local_sems.at[a]) for a in range(n)]
        sends = [_remote(ins[a], _slot(outs[a], me), send_sems.at[a, k], recv_sems.at[a, k], peer)
                 for a in range(n) for k, peer in enumerate(peers)]
        arrivals = [_remote(ins[a], _slot(outs[a], peer), send_sems.at[a, k], recv_sems.at[a, k], peer)
                    for a in range(n) for k, peer in enumerate(peers)]
        return local, sends, arrivals

    def start(ins, outs, sems):
        local, sends, _ = copies(ins, outs, sems)
        for cp in local + sends:
            cp.start()

    def finish(ins, outs, sems):
        local, sends, arrivals = copies(ins, outs, sems)
        for cp in arrivals:
            cp.wait_recv()
        for cp in sends:
            cp.wait_send()
        for cp in local:
            cp.wait()

    dma = pltpu.SemaphoreType.DMA
    return _Exchange(arrs, [_sds((N_DEV,) + a.shape, a.dtype) for a in arrs], [dma((n, 4)), dma((n, 4)), dma((n,))],
                     start, finish)


def _gather_forward(gathered):
    n = len(gathered)

    def copies(ins, outs, sems):
        send_sems, recv_sems = sems
        x, y, c, chips = _place()
        sibling = (x, y, 1 - c)
        sends = [_remote(_slot(ins[a], (*chip, c)), _slot(outs[a], (*chip, c)), send_sems.at[a, j], recv_sems.at[a, j], sibling)
                 for a in range(n) for j, chip in enumerate(chips)]
        arrivals = [_remote(_slot(ins[a], (*chip, c)), _slot(outs[a], (*chip, 1 - c)), send_sems.at[a, j], recv_sems.at[a, j], sibling)
                    for a in range(n) for j, chip in enumerate(chips)]
        return sends, arrivals

    def start(ins, outs, sems):
        for cp in copies(ins, outs, sems)[0]:
            cp.start()

    def finish(ins, outs, sems):
        sends, arrivals = copies(ins, outs, sems)
        for cp in arrivals:
            cp.wait_recv()
        for cp in sends:
            cp.wait_send()

    dma = pltpu.SemaphoreType.DMA
    return _Exchange(gathered, [_sds(a.shape, a.dtype) for a in gathered], [dma((n, 3)), dma((n, 3))], start, finish,
                     aliases={a: a for a in range(n)})


def _gather_all(arrs, parts=GATHER_PARTS):
    n = len(arrs)
    split = [parts if a.shape[0] % (parts * BF16_ROWS) == 0 else 1 for a in arrs]
    members = [[a for a in range(n) if p < split[a]] for p in range(parts)]

    def part_of(a, p):
        size = arrs[a].shape[0] // split[a]
        return pl.ds(p * size, size), _sds((size,) + arrs[a].shape[1:], arrs[a].dtype)

    direct = [_gather_direct([part_of(a, p)[1] for a in members[p]]) for p in range(parts)]
    passed = [_gather_forward([_sds((N_DEV,) + part_of(a, p)[1].shape, arrs[a].dtype) for a in members[p]])
              for p in range(parts)]
    jobs = direct + passed
    sem_shapes = [s for job in jobs for s in job.sem_shapes]

    def views(ins, outs, sems):
        found, at = [], 0
        for i, job in enumerate(jobs):
            p = i % parts
            rows = [part_of(a, p)[0] for a in members[p]]
            found.append(([ins[a].at[r] for a, r in zip(members[p], rows)],
                          [outs[a].at[:, r] for a, r in zip(members[p], rows)],
                          sems[at:at + len(job.sem_shapes)]))
            at += len(job.sem_shapes)
        return found

    def start(ins, outs, sems):
        for job, (i, o, s) in list(zip(jobs, views(ins, outs, sems)))[:parts]:
            job.start(i, o, s)

    def finish(ins, outs, sems):
        found = views(ins, outs, sems)
        for p in range(parts):
            i, o, s = found[p]
            direct[p].finish(i, o, s)
            _, o2, s2 = found[parts + p]
            passed[p].start(o2, o2, s2)
        for p in range(parts):
            _, o2, s2 = found[parts + p]
            passed[p].finish(o2, o2, s2)

    return _Exchange(arrs, [_sds((N_DEV,) + a.shape, a.dtype) for a in arrs], sem_shapes, start, finish)


def _gather_range(arr, part, n_parts, prev=None):
    size = arr.shape[0] // n_parts
    rows = pl.ds(part * size, size)
    inner = _gather_all([_sds((size,) + arr.shape[1:], arr.dtype)], parts=1)

    def start(ins, outs, sems):
        inner.start([ins[0].at[rows]], [outs[0].at[:, rows]], sems)

    def finish(ins, outs, sems):
        inner.finish([ins[0].at[rows]], [outs[0].at[:, rows]], sems)

    return _Exchange([arr] if prev is None else [arr, prev], [_sds((N_DEV,) + arr.shape, arr.dtype)], inner.sem_shapes,
                     start, finish, aliases=None if prev is None else {1: 0})


def _pair_exchange(arrs):
    n = len(arrs)

    def copies(ins, outs, sems):
        x, y, c, _ = _place()
        return [_remote(ins[a].at[:, 1 - c], outs[a], sems[0].at[a], sems[1].at[a], (x, y, 1 - c)) for a in range(n)]

    def start(ins, outs, sems):
        for cp in copies(ins, outs, sems):
            cp.start()

    def finish(ins, outs, sems):
        for cp in copies(ins, outs, sems):
            cp.wait()

    dma = pltpu.SemaphoreType.DMA
    return _Exchange(arrs, [_sds((4,) + a.shape[2:], a.dtype) for a in arrs], [dma((n,)), dma((n,))], start, finish)


def _chip_exchange(arrs):
    n = len(arrs)

    def copies(ins, outs, sems):
        x, y, c, chips = _place()
        return [_remote(ins[a].at[1 + j], outs[a].at[j], sems[0].at[a, j], sems[1].at[a, j], (*chip, c))
                for a in range(n) for j, chip in enumerate(chips)]

    def start(ins, outs, sems):
        for cp in copies(ins, outs, sems):
            cp.start()

    def finish(ins, outs, sems):
        for cp in copies(ins, outs, sems):
            cp.wait()

    dma = pltpu.SemaphoreType.DMA
    return _Exchange(arrs, [_sds((3,) + a.shape[1:], a.dtype) for a in arrs], [dma((n, 3)), dma((n, 3))], start, finish)


def _gather_rows(vec, name, comm=()):
    def body(in_ref, out_ref, send_sems, recv_sems, local_sem):
        x, y, c, _ = _place()
        me = 4 * x + 2 * y + c
        mine = pltpu.make_async_copy(in_ref, out_ref.at[me], local_sem)
        mine.start()
        copies = []
        for k in range(1, N_DEV):
            to = (1 - x if k & 4 else x, 1 - y if k & 2 else y, 1 - c if k & 1 else c)
            copies.append(pltpu.make_async_remote_copy(
                src_ref=in_ref, dst_ref=out_ref.at[me], send_sem=send_sems.at[k - 1], recv_sem=recv_sems.at[k - 1],
                device_id=to, device_id_type=MESH))
        for cp in copies:
            cp.start()
        for cp in copies:
            cp.wait()
        mine.wait()

    return _call(
        body, name=name, out_shape=_sds((N_DEV,) + vec.shape, vec.dtype), in_specs=[ANY], out_specs=ANY,
        scratch_shapes=[pltpu.SemaphoreType.DMA((7,)), pltpu.SemaphoreType.DMA((7,)), pltpu.SemaphoreType.DMA],
    )(vec)


def _group_rows(tm):
    return SUB_ROWS if tm % SUB_ROWS == 0 else BF16_ROWS


def _for_row_groups(tm, fn):
    sub = _group_rows(tm)

    def step(i, carry):
        fn(pl.ds(pl.multiple_of(i * sub, sub), sub), i * sub)
        return carry

    lax.fori_loop(0, tm // sub, step, 0)


def _once(shape, index_map):
    return pl.BlockSpec(shape, index_map, pipeline_mode=pl.Buffered(1))


def _rms_fwd(hs, g, tm, name, comm=()):
    rows, d = hs.shape

    def body(hs_ref, g_ref, h_ref, r_ref):
        def group(rs, _):
            xv = hs_ref[rs, :]
            r = lax.rsqrt(jnp.mean(xv * xv, axis=-1, keepdims=True) + EPS)
            h_ref[rs, :] = (xv * r * g_ref[...]).astype(BF16)
            r_ref[rs, :] = r

        _for_row_groups(tm, group)

    return _call(
        body, comm=comm, name=name, grid=(rows // tm,),
        in_specs=[pl.BlockSpec((tm, d), lambda i: (i, 0)), pl.BlockSpec((1, d), lambda i: (0, 0))],
        out_specs=[pl.BlockSpec((tm, d), lambda i: (i, 0)), pl.BlockSpec((tm, 1), lambda i: (i, 0))],
        out_shape=[_sds((rows, d), BF16), _sds((rows, 1), F32)], compiler_params=_params(),
    )(hs, g)


def _ffn_gu(h, wg, wu, tm, name, comm=()):
    rows, d = h.shape
    nb, fb, _ = wg.shape

    def body(h_ref, wg_ref, wu_ref, g_ref, u_ref):
        hv = h_ref[...]
        g_ref[0] = lax.dot_general(hv, wg_ref[0], NT, preferred_element_type=F32).astype(BF16)
        u_ref[0] = lax.dot_general(hv, wu_ref[0], NT, preferred_element_type=F32).astype(BF16)

    wspec = pl.BlockSpec((1, fb, d), lambda j, i: (j, 0, 0))
    ospec = pl.BlockSpec((1, tm, fb), lambda j, i: (j, i, 0))
    return _call(
        body, comm=comm, name=name, grid=(nb, rows // tm),
        in_specs=[pl.BlockSpec((tm, d), lambda j, i: (i, 0)), wspec, wspec],
        out_specs=[ospec, ospec], out_shape=[_sds((nb, rows, fb), BF16)] * 2, compiler_params=_params(),
    )(h, wg, wu)


def _ffn_proj(h, w, tm, name, comm=()):
    rows, d = h.shape
    nb, fb, _ = w.shape

    def body(h_ref, w_ref, p_ref):
        p_ref[0] = lax.dot_general(h_ref[...], w_ref[0], NT, preferred_element_type=F32).astype(BF16)

    return _call(
        body, comm=comm, name=name, grid=(nb, rows // tm),
        in_specs=[pl.BlockSpec((tm, d), lambda j, i: (i, 0)), pl.BlockSpec((1, fb, d), lambda j, i: (j, 0, 0))],
        out_specs=pl.BlockSpec((1, tm, fb), lambda j, i: (j, i, 0)), out_shape=_sds((nb, rows, fb), BF16),
        compiler_params=_params(),
    )(h, w)


def _down_product(g_ref, u_ref, wd_ref, acc_ref):
    part = None
    for b in range(g_ref.shape[0]):
        gv = g_ref[b]
        a = gv * _sigmoid(gv) * u_ref[b]
        p = jnp.dot(a, wd_ref[b], preferred_element_type=F32)
        part = p if part is None else part + p
    acc_ref[...] = part


def _ffn_down_norm(g, u, wd, hs, gn, tm, name, comm=()):
    nb, rows, fb = g.shape
    d = hs.shape[1]

    def body(g_ref, u_ref, wd_ref, hs_ref, gn_ref, hsn_ref, hn_ref, rn_ref, acc_ref):
        _down_product(g_ref, u_ref, wd_ref, acc_ref)

        def group(rs, _):
            hsn = hs_ref[rs, :] + FFN_RES_SCALE * acc_ref[rs, :]
            r = lax.rsqrt(jnp.mean(hsn * hsn, axis=-1, keepdims=True) + EPS)
            hsn_ref[rs, :] = hsn
            hn_ref[rs, :] = (hsn * r * gn_ref[...]).astype(BF16)
            rn_ref[rs, :] = r

        _for_row_groups(tm, group)

    aspec = pl.BlockSpec((nb, tm, fb), lambda i: (0, i, 0))
    row = pl.BlockSpec((tm, d), lambda i: (i, 0))
    return _call(
        body, comm=comm, name=name, grid=(rows // tm,),
        in_specs=[aspec, aspec, _once((nb, fb, d), lambda i: (0, 0, 0)), row, pl.BlockSpec((1, d), lambda i: (0, 0))],
        out_specs=[row, row, pl.BlockSpec((tm, 1), lambda i: (i, 0))],
        out_shape=[_sds((rows, d), F32), _sds((rows, d), BF16), _sds((rows, 1), F32)],
        scratch_shapes=[pltpu.VMEM((tm, d), F32)], compiler_params=_params(),
    )(g, u, wd, hs, gn)


def _ffn_down_loss(g, u, wd, hs, gf, tgt, n_seq, tm, name, comm=()):
    nb, rows, fb = g.shape
    d = hs.shape[1]
    nt = rows // tm

    def body(g_ref, u_ref, wd_ref, hs_ref, gf_ref, tgt_ref, dhs_ref, dhsb_ref, loss_ref, dgf_ref, acc_ref):
        i = pl.program_id(0)
        _down_product(g_ref, u_ref, wd_ref, acc_ref)
        loss_ref[0] = jnp.zeros((1, d), F32)
        dgf_ref[0] = jnp.zeros((1, d), F32)

        def group(rs, r0):
            hs3 = hs_ref[rs, :] + FFN_RES_SCALE * acc_ref[rs, :]
            r = lax.rsqrt(jnp.mean(hs3 * hs3, axis=-1, keepdims=True) + EPS)
            gfv = gf_ref[...]
            y = hs3 * r
            rowid = i * tm + r0 + lax.broadcasted_iota(jnp.int32, (_group_rows(tm), 1), 0)
            err = jnp.where(rowid < n_seq, y * gfv - tgt_ref[rs, :], 0.0)
            loss_ref[0] += jnp.sum(err * err, axis=0, keepdims=True)
            dout = err * (1.0 / d)
            dgf_ref[0] += jnp.sum(dout * y, axis=0, keepdims=True)
            t = dout * gfv
            dhs = r * t - hs3 * (r * r * r) * jnp.mean(t * hs3, axis=-1, keepdims=True)
            dhs_ref[rs, :] = dhs
            dhsb_ref[rs, :] = dhs.astype(BF16)

        _for_row_groups(tm, group)

    aspec = pl.BlockSpec((nb, tm, fb), lambda i: (0, i, 0))
    row = pl.BlockSpec((tm, d), lambda i: (i, 0))
    part = pl.BlockSpec((1, 1, d), lambda i: (i, 0, 0))
    return _call(
        body, comm=comm, name=name, grid=(nt,),
        in_specs=[aspec, aspec, _once((nb, fb, d), lambda i: (0, 0, 0)), row, pl.BlockSpec((1, d), lambda i: (0, 0)), row],
        out_specs=[row, row, part, part],
        out_shape=[_sds((rows, d), F32), _sds((rows, d), BF16), _sds((nt, 1, d), F32), _sds((nt, 1, d), F32)],
        scratch_shapes=[pltpu.VMEM((tm, d), F32)], compiler_params=_params(),
    )(g, u, wd, hs, gf, tgt)


def _ffn_bwd_da(do, wd, g, u, tm, name, comm=()):
    nb, rows, fb = g.shape
    d = do.shape[1]

    def body(do_ref, wd_ref, g_ref, u_ref, dg_ref, du_ref):
        rs = pl.ds(pl.multiple_of(pl.program_id(1) * tm, tm), tm)
        da = (FFN_RES_SCALE * lax.dot_general(do_ref[rs, :], wd_ref[0], NT, preferred_element_type=F32)).astype(BF16)
        gv, uv = g_ref[0], u_ref[0]
        s = _sigmoid(gv)
        sg = gv * s
        du_ref[0] = da * sg
        dg_ref[0] = da * uv * (s + sg * (1.0 - s))

    aspec = pl.BlockSpec((1, tm, fb), lambda j, i: (j, i, 0))
    return _call(
        body, comm=comm, name=name, grid=(nb, rows // tm),
        in_specs=[_once((rows, d), lambda j, i: (0, 0)), pl.BlockSpec((1, fb, d), lambda j, i: (j, 0, 0)), aspec, aspec],
        out_specs=[aspec] * 2, out_shape=[_sds((nb, rows, fb), BF16)] * 2, compiler_params=_params(),
    )(do, wd, g, u)


def _rms_bwd(tm, d, dh_ref, hs_ref, r_ref, gn_ref, dres_ref, dhs_ref, dhsb_ref, dgn_ref):
    dgn_ref[0] = jnp.zeros((1, d), F32)

    def group(rs, _):
        dh, hs, r = dh_ref[rs, :], hs_ref[rs, :], r_ref[rs, :]
        dgn_ref[0] += jnp.sum(dh * (hs * r), axis=0, keepdims=True)
        t = dh * gn_ref[...]
        dhs = dres_ref[rs, :] + r * t - hs * (r * r * r) * jnp.mean(t * hs, axis=-1, keepdims=True)
        dhs_ref[rs, :] = dhs
        dhsb_ref[rs, :] = dhs.astype(BF16)

    _for_row_groups(tm, group)


def _blocks_dot(x_ref, w_ref):
    part = None
    for b in range(x_ref.shape[0]):
        p = jnp.dot(x_ref[b], w_ref[b], preferred_element_type=F32)
        part = p if part is None else part + p
    return part


def _ffn_bwd_dh_gate(dg, wg, tm, name, comm=()):
    nb, rows, fb = dg.shape
    d = wg.shape[2]

    def body(dg_ref, wg_ref, o_ref):
        o_ref[...] = _blocks_dot(dg_ref, wg_ref)

    return _call(
        body, comm=comm, name=name, grid=(rows // tm,),
        in_specs=[pl.BlockSpec((nb, tm, fb), lambda i: (0, i, 0)), _once((nb, fb, d), lambda i: (0, 0, 0))],
        out_specs=pl.BlockSpec((tm, d), lambda i: (i, 0)), out_shape=_sds((rows, d), F32), compiler_params=_params(),
    )(dg, wg)


def _ffn_bwd_dh(du, wu, half, hs, r, gn, dres, tm, name, comm=(), zeros=None):
    nb, rows, fb = du.shape
    d = hs.shape[1]
    nt = rows // tm
    extra = [] if zeros is None else [zeros]

    def body(du_ref, wu_ref, half_ref, hs_ref, r_ref, gn_ref, dres_ref, *rest):
        dhs_ref, dhsb_ref, dgn_ref, acc_ref = rest[len(extra):]
        acc_ref[...] = half_ref[...] + _blocks_dot(du_ref, wu_ref)
        _rms_bwd(tm, d, acc_ref, hs_ref, r_ref, gn_ref, dres_ref, dhs_ref, dhsb_ref, dgn_ref)

    row = pl.BlockSpec((tm, d), lambda i: (i, 0))
    return _call(
        body, comm=comm, name=name, grid=(nt,),
        in_specs=[pl.BlockSpec((nb, tm, fb), lambda i: (0, i, 0)), _once((nb, fb, d), lambda i: (0, 0, 0)), row, row,
                  pl.BlockSpec((tm, 1), lambda i: (i, 0)), pl.BlockSpec((1, d), lambda i: (0, 0)), row] + [ANY] * len(extra),
        out_specs=[row, row, pl.BlockSpec((1, 1, d), lambda i: (i, 0, 0))],
        out_shape=[_sds((rows, d) if zeros is None else zeros.shape, F32), _sds((rows, d), BF16), _sds((nt, 1, d), F32)],
        scratch_shapes=[pltpu.VMEM((tm, d), F32)], compiler_params=_params(),
        input_output_aliases={7: 0} if extra else {},
    )(du, wu, half, hs, r, gn, dres, *extra)


class _KSteps:
    def __init__(self, rows, tk):
        self.main, self.rem, self.tk = rows // tk, rows % tk, tk
        self.n = self.main + (1 if self.rem else 0)
        if self.rem:
            assert rows % self.rem == 0, (rows, tk)
            self.last = rows // self.rem - 1

    def specs(self, makers):
        main_at = (lambda k: jnp.minimum(k, self.main - 1)) if self.rem else (lambda k: k)
        out = [make(self.tk, main_at) for make in makers]
        if self.rem:
            out += [make(self.rem, lambda k: self.last) for make in makers]
        return out

    def args(self, arrs):
        return list(arrs) * (2 if self.rem else 1)

    def each(self, k, refs, fn):
        if not self.rem:
            fn(*refs)
            return
        n = len(refs) // 2
        pl.when(k < self.main)(lambda: fn(*refs[:n]))
        pl.when(k == self.main)(lambda: fn(*refs[n:]))


def _ffn_bwd_dw_gu(h, dg, du, tk, name, comm=()):
    nb, rows, fb = dg.shape
    d = h.shape[1]
    steps = _KSteps(rows, tk)

    def body(*refs):
        ins, (dwg_ref, dwu_ref, accg, accu) = refs[:-4], refs[-4:]
        k = pl.program_id(1)

        @pl.when(k == 0)
        def _():
            accg[...] = jnp.zeros_like(accg)
            accu[...] = jnp.zeros_like(accu)

        def add(h_ref, dg_ref, du_ref):
            hv = h_ref[...]
            accg[...] += lax.dot_general(dg_ref[0], hv, TN, preferred_element_type=F32)
            accu[...] += lax.dot_general(du_ref[0], hv, TN, preferred_element_type=F32)

        steps.each(k, ins, add)

        @pl.when(k == steps.n - 1)
        def _():
            dwg_ref[0] = accg[...].astype(BF16)
            dwu_ref[0] = accu[...].astype(BF16)

    wspec = pl.BlockSpec((1, fb, d), lambda j, k: (j, 0, 0))
    specs = steps.specs([lambda t, at: pl.BlockSpec((t, d), lambda j, k: (at(k), 0)),
                         lambda t, at: pl.BlockSpec((1, t, fb), lambda j, k: (j, at(k), 0)),
                         lambda t, at: pl.BlockSpec((1, t, fb), lambda j, k: (j, at(k), 0))])
    return _call(
        body, comm=comm, name=name, grid=(nb, steps.n),
        in_specs=specs, out_specs=[wspec] * 2,
        out_shape=[_sds((nb, fb, d), BF16)] * 2, scratch_shapes=[pltpu.VMEM((fb, d), F32)] * 2,
        compiler_params=_params(),
    )(*steps.args([h, dg, du]))


def _ffn_bwd_dw_down(g, u, do, tk, name, comm=()):
    nb, rows, fb = g.shape
    d = do.shape[1]
    steps = _KSteps(rows, tk)

    def body(*refs):
        ins, (dwd_ref, acc) = refs[:-2], refs[-2:]
        k = pl.program_id(1)

        @pl.when(k == 0)
        def _():
            acc[...] = jnp.zeros_like(acc)

        def add(g_ref, u_ref, do_ref):
            gv = g_ref[0]
            a = gv * _sigmoid(gv) * u_ref[0]
            acc[...] += lax.dot_general(a, do_ref[...], TN, preferred_element_type=F32)

        steps.each(k, ins, add)

        @pl.when(k == steps.n - 1)
        def _():
            dwd_ref[0] = (FFN_RES_SCALE * acc[...]).astype(BF16)

    specs = steps.specs([lambda t, at: pl.BlockSpec((1, t, fb), lambda j, k: (j, at(k), 0)),
                         lambda t, at: pl.BlockSpec((1, t, fb), lambda j, k: (j, at(k), 0)),
                         lambda t, at: pl.BlockSpec((t, d), lambda j, k: (at(k), 0))])
    return _call(
        body, comm=comm, name=name, grid=(nb, steps.n), in_specs=specs,
        out_specs=pl.BlockSpec((1, fb, d), lambda j, k: (j, 0, 0)), out_shape=_sds((nb, fb, d), BF16),
        scratch_shapes=[pltpu.VMEM((fb, d), F32)], compiler_params=_params(),
    )(*steps.args([g, u, do]))


def _win_fwd(h, w, b, dc, tm, name, comm=()):
    rows, d = h.shape
    ng = w.shape[1] // dc

    def body(h_ref, w_ref, b_ref, u_ref):
        u_ref[...] = (jnp.dot(h_ref[...], w_ref[...], preferred_element_type=F32) + b_ref[...]).astype(BF16)

    return _call(
        body, comm=comm, name=name, grid=(ng, rows // tm),
        in_specs=[pl.BlockSpec((tm, d), lambda m, i: (i, 0)), pl.BlockSpec((d, dc), lambda m, i: (0, m)),
                  pl.BlockSpec((1, dc), lambda m, i: (0, m))],
        out_specs=pl.BlockSpec((tm, dc), lambda m, i: (i, m)), out_shape=_sds((rows, ng * dc), BF16),
        compiler_params=_params(),
    )(h, w, b)


def _win_bwd_dh(du, w, hs, r, gn, dres, tm, name, comm=()):
    rows, d = hs.shape
    ng, _, dc = du.shape
    nt = rows // tm

    def body(du_ref, w_ref, hs_ref, r_ref, gn_ref, dres_ref, dhs_ref, dhsb_ref, dgn_ref, acc_ref):
        part = None
        for m in range(ng):
            p = lax.dot_general(du_ref[m], w_ref[:, m * dc:(m + 1) * dc], NT, preferred_element_type=F32)
            part = p if part is None else part + p
        acc_ref[...] = part
        _rms_bwd(tm, d, acc_ref, hs_ref, r_ref, gn_ref, dres_ref, dhs_ref, dhsb_ref, dgn_ref)

    row = pl.BlockSpec((tm, d), lambda i: (i, 0))
    return _call(
        body, comm=comm, name=name, grid=(nt,),
        in_specs=[pl.BlockSpec((ng, tm, dc), lambda i: (0, i, 0)), _once((d, ng * dc), lambda i: (0, 0)), row,
                  pl.BlockSpec((tm, 1), lambda i: (i, 0)), pl.BlockSpec((1, d), lambda i: (0, 0)), row],
        out_specs=[row, row, pl.BlockSpec((1, 1, d), lambda i: (i, 0, 0))],
        out_shape=[_sds((rows, d), F32), _sds((rows, d), BF16), _sds((nt, 1, d), F32)],
        scratch_shapes=[pltpu.VMEM((tm, d), F32)], compiler_params=_params(),
    )(du, w, hs, r, gn, dres)


def _win_bwd_dw(h, du, tk, name, comm=()):
    rows, d = h.shape
    ng, _, dc = du.shape
    steps = _KSteps(rows, tk)

    def body(*refs):
        ins, (dw_ref, db_ref, acc, accb) = refs[:-4], refs[-4:]
        k = pl.program_id(1)

        @pl.when(k == 0)
        def _():
            acc[...] = jnp.zeros_like(acc)
            accb[...] = jnp.zeros_like(accb)

        def add(h_ref, du_ref):
            duv = du_ref[0]
            acc[...] += lax.dot_general(h_ref[...], duv, TN, preferred_element_type=F32)
            accb[...] += jnp.sum(duv.astype(F32), axis=0, keepdims=True)

        steps.each(k, ins, add)

        @pl.when(k == steps.n - 1)
        def _():
            dw_ref[...] = acc[...].astype(BF16)
            db_ref[...] = accb[...]

    specs = steps.specs([lambda t, at: pl.BlockSpec((t, d), lambda m, k: (at(k), 0)),
                         lambda t, at: pl.BlockSpec((1, t, dc), lambda m, k: (m, at(k), 0))])
    return _call(
        body, comm=comm, name=name, grid=(ng, steps.n), in_specs=specs,
        out_specs=[pl.BlockSpec((d, dc), lambda m, k: (0, m)), pl.BlockSpec((1, dc), lambda m, k: (0, m))],
        out_shape=[_sds((d, ng * dc), BF16), _sds((1, ng * dc), F32)],
        scratch_shapes=[pltpu.VMEM((d, dc), F32), pltpu.VMEM((1, dc), F32)], compiler_params=_params(),
    )(*steps.args([h, du]))


def _layernorm_silu(zc, lg, lb):
    mu = jnp.mean(zc, axis=-1, keepdims=True)
    xc = zc - mu
    rstd = lax.rsqrt(jnp.mean(xc * xc, axis=-1, keepdims=True) + EPS)
    nrm = xc * rstd
    lin = nrm * lg + lb
    s = _sigmoid(lin)
    return nrm, rstd, lin, s


def _wout_fwd(zc, ysc, wout, hs, lg, lb, gn, rows, tm, name, comm=()):
    dc = zc.shape[1]
    d = hs.shape[1]

    def body(zc_ref, ysc_ref, w_ref, hs_ref, lg_ref, lb_ref, gn_ref, y_ref, hsn_ref, hn_ref, rn_ref):
        def mix(rs, _):
            _, _, lin, s = _layernorm_silu(zc_ref[rs, :], lg_ref[...], lb_ref[...])
            y_ref[rs, :dc] = ysc_ref[rs, :]
            y_ref[rs, dc:] = (lin * s).astype(BF16)

        _for_row_groups(tm, mix)
        hsn_ref[...] = jnp.dot(y_ref[...], w_ref[...], preferred_element_type=F32)

        def norm(rs, _):
            hsn = hs_ref[rs, :] + hsn_ref[rs, :]
            r = lax.rsqrt(jnp.mean(hsn * hsn, axis=-1, keepdims=True) + EPS)
            hsn_ref[rs, :] = hsn
            hn_ref[rs, :] = (hsn * r * gn_ref[...]).astype(BF16)
            rn_ref[rs, :] = r

        _for_row_groups(tm, norm)

    half = pl.BlockSpec((tm, dc), lambda i: (i, 0))
    row = pl.BlockSpec((tm, d), lambda i: (i, 0))
    vec_c = pl.BlockSpec((1, dc), lambda i: (0, 0))
    return _call(
        body, comm=comm, name=name, grid=(rows // tm,),
        in_specs=[half, half, _once((2 * dc, d), lambda i: (0, 0)), row, vec_c, vec_c,
                  pl.BlockSpec((1, d), lambda i: (0, 0))],
        out_specs=[pl.BlockSpec((tm, 2 * dc), lambda i: (i, 0)), row, row, pl.BlockSpec((tm, 1), lambda i: (i, 0))],
        out_shape=[_sds((rows, 2 * dc), BF16), _sds((rows, d), F32), _sds((rows, d), BF16), _sds((rows, 1), F32)],
        compiler_params=_params(),
    )(zc, ysc, wout, hs, lg, lb, gn)


def _wout_bwd_dy(do, wout, zc, lg, lb, tm, name, comm=()):
    rows, d = do.shape
    dc = zc.shape[1]
    nt = rows // tm

    def body(do_ref, w_ref, zc_ref, lg_ref, lb_ref, dysc_ref, dzc_ref, dlg_ref, dlb_ref, dy_ref):
        dy_ref[...] = lax.dot_general(do_ref[...], w_ref[...], NT, preferred_element_type=F32)
        dlb_ref[0] = jnp.zeros((1, dc), F32)
        dlg_ref[0] = jnp.zeros((1, dc), F32)

        def group(rs, _):
            dysc_ref[rs, :] = dy_ref[rs, :dc].astype(BF16)
            nrm, rstd, lin, s = _layernorm_silu(zc_ref[rs, :], lg_ref[...], lb_ref[...])
            dl = dy_ref[rs, dc:] * (s * (1.0 + lin * (1.0 - s)))
            dlb_ref[0] += jnp.sum(dl, axis=0, keepdims=True)
            dlg_ref[0] += jnp.sum(dl * nrm, axis=0, keepdims=True)
            dn = dl * lg_ref[...]
            dzc_ref[rs, :] = rstd * (dn - jnp.mean(dn, axis=-1, keepdims=True)
                                     - nrm * jnp.mean(dn * nrm, axis=-1, keepdims=True))

        _for_row_groups(tm, group)

    half = pl.BlockSpec((tm, dc), lambda i: (i, 0))
    vec_c = pl.BlockSpec((1, dc), lambda i: (0, 0))
    part = pl.BlockSpec((1, 1, dc), lambda i: (i, 0, 0))
    return _call(
        body, comm=comm, name=name, grid=(nt,),
        in_specs=[pl.BlockSpec((tm, d), lambda i: (i, 0)), _once((2 * dc, d), lambda i: (0, 0)), half, vec_c, vec_c],
        out_specs=[half, half, part, part],
        out_shape=[_sds((rows, dc), BF16), _sds((rows, dc), F32), _sds((nt, 1, dc), F32), _sds((nt, 1, dc), F32)],
        scratch_shapes=[pltpu.VMEM((tm, 2 * dc), F32)], compiler_params=_params(),
    )(do, wout, zc, lg, lb)


def _wout_bwd_dw(y, do, nb, tk, name, comm=()):
    rows, k2 = y.shape
    d = do.shape[1]
    nk = rows // tk

    def body(y_ref, do_ref, dw_ref, acc):
        k = pl.program_id(0)

        @pl.when(k == 0)
        def _():
            acc[...] = jnp.zeros_like(acc)

        acc[...] += lax.dot_general(y_ref[...], do_ref[...], TN, preferred_element_type=F32)

        @pl.when(k == nk - 1)
        def _():
            dw_ref[...] = acc[...].astype(BF16)

    return _call(
        body, comm=comm, name=name, grid=(nk,),
        in_specs=[pl.BlockSpec((tk, k2), lambda k: (k, 0)), pl.BlockSpec((tk, d), lambda k: (k, 0))],
        out_specs=_once((k2, d), lambda k: (0, 0)), out_shape=_sds((k2, d), BF16),
        scratch_shapes=[pltpu.VMEM((k2, d), F32)], compiler_params=_params(),
    )(y, do).reshape(nb, k2 // nb, d)


def _windows(win, n_res):
    length = win.shape[0]
    return [win if r == 0 else pltpu.roll(win, length - r, 0) for r in range(n_res)]


def _taps(src_ref, start, offsets, ch):
    span = -(-(max(offsets) + ch) // SUBLANES) * SUBLANES
    win = src_ref[pl.ds(start, span), :]
    shifted = _windows(win, min(SUBLANES, max(offsets) + 1))
    return [shifted[o % SUBLANES][(o // SUBLANES) * SUBLANES:(o // SUBLANES) * SUBLANES + ch] for o in offsets]


def _rows8(v):
    return jnp.sum(v.reshape(v.shape[0] // SUBLANES, SUBLANES, v.shape[1]), axis=0)


def _conv_geometry(n_seq, n_meta):
    base = CONV_PAD + n_meta
    off_cf = base - (CF_WIDTH - 1)
    off_sc = base - (SC_WIDTH - 1)
    logical = -(-(n_meta + n_seq) // CONV_CH) * CONV_CH
    return base, off_cf, off_sc, logical


def _fill_conv_inputs(c_ref, v_ref, a_ref, g_ref, scv, sz, n_seq, n_meta):
    base = CONV_PAD + n_meta
    cb = scv.shape[1]
    scv[0:CONV_PAD, :] = jnp.zeros((CONV_PAD, cb), F32)
    sz[0:CONV_PAD, :] = jnp.zeros((CONV_PAD, cb), F32)

    def put(src, dst, n):
        cv = c_ref[src, :].astype(F32) * v_ref[src, :].astype(F32)
        scv[dst, :] = cv
        sz[dst, :] = a_ref[src, :].astype(F32) * _sigmoid(g_ref[src, :].astype(F32))

    put(pl.ds(n_seq, n_meta), pl.ds(CONV_PAD, n_meta), n_meta)

    def chunk(i, carry):
        t0 = pl.multiple_of(i * CONV_CH, CONV_CH)
        put(pl.ds(t0, CONV_CH), pl.ds(base + t0, CONV_CH), CONV_CH)
        return carry

    lax.fori_loop(0, n_seq // CONV_CH, chunk, 0)


def _conv_fwd(u, wsc, wcf, cbias, n_seq, n_meta, name, comm=()):
    rows = u.shape[0]
    nb, _, cb = wsc.shape
    dc = nb * cb
    base, off_cf, off_sc, _ = _conv_geometry(n_seq, n_meta)
    a_cf, a_sc = off_cf // SUBLANES * SUBLANES, off_sc // SUBLANES * SUBLANES
    ch = CONV_CH

    def body(b_ref, c_ref, v_ref, a_ref, g_ref, wsc_ref, wcf_ref, cb_ref, ysc_ref, zc_ref, scv, sz):
        _fill_conv_inputs(c_ref, v_ref, a_ref, g_ref, scv, sz, n_seq, n_meta)
        w3, w31, bias = wsc_ref[0], wcf_ref[0], cb_ref[...]

        def chunk(i, carry):
            t0 = pl.multiple_of(i * ch, ch)
            acc = jnp.zeros((ch, cb), F32)
            for k, win in enumerate(_taps(sz, t0 + a_cf, [off_cf - a_cf + k for k in range(CF_WIDTH)], ch)):
                acc = acc + win * w31[k:k + 1, :]
            zc_ref[pl.ds(t0, ch), :] = acc + bias
            s = jnp.zeros((ch, cb), F32)
            for k, win in enumerate(_taps(scv, t0 + a_sc, [off_sc - a_sc + k for k in range(SC_WIDTH)], ch)):
                s = s + win * w3[k:k + 1, :]
            ysc_ref[pl.ds(t0, ch), :] = (b_ref[pl.ds(t0, ch), :].astype(F32) * s).astype(BF16)
            return carry

        lax.fori_loop(0, n_seq // ch, chunk, 0)
        ysc_ref[n_seq:rows, :] = jnp.zeros((rows - n_seq, cb), BF16)
        zc_ref[n_seq:rows, :] = jnp.zeros((rows - n_seq, cb), F32)

    ucol = [pl.BlockSpec((rows, cb), functools.partial(lambda m, j: (0, m * nb + j), m)) for m in range(5)]
    blk = pl.BlockSpec((rows, cb), lambda j: (0, j))
    return _call(
        body, comm=comm, name=name, grid=(nb,),
        in_specs=ucol + [pl.BlockSpec((1, SC_WIDTH, cb), lambda j: (j, 0, 0)),
                         pl.BlockSpec((1, CF_WIDTH, cb), lambda j: (j, 0, 0)), pl.BlockSpec((1, cb), lambda j: (0, j))],
        out_specs=[blk, blk], out_shape=[_sds((rows, dc), BF16), _sds((rows, dc), F32)],
        scratch_shapes=[pltpu.VMEM((base + n_seq, cb), F32)] * 2, compiler_params=_params(),
    )(u, u, u, u, u, wsc, wcf, cbias)


def _conv_bwd(u, dysc, dzc, wsc, wcf, n_seq, n_meta, name, comm=()):
    rows = u.shape[0]
    nb, _, cb = wsc.shape
    dc = nb * cb
    base, off_cf, off_sc, logical = _conv_geometry(n_seq, n_meta)
    a_cf, a_sc = off_cf // SUBLANES * SUBLANES, off_sc // SUBLANES * SUBLANES
    ch = CONV_CH
    tail = CONV_PAD

    def body(b_ref, c_ref, v_ref, a_ref, g_ref, dysc_ref, dzc_ref, wsc_ref, wcf_ref,
             du_ref, dwsc_ref, dwcf_ref, dcb_ref,
             scv, sz, sds_, sdz, dlcv, dlz, accsc, acccf, accb):
        _fill_conv_inputs(c_ref, v_ref, a_ref, g_ref, scv, sz, n_seq, n_meta)
        w3, w31 = wsc_ref[0], wcf_ref[0]
        dub_ref, duc_ref, duv_ref, dua_ref, dug_ref = (du_ref.at[m] for m in range(5))
        sds_[0:n_meta, :] = jnp.zeros((n_meta, cb), F32)
        sdz[0:n_meta, :] = jnp.zeros((n_meta, cb), F32)
        behind = logical + tail - (n_meta + n_seq)
        sds_[n_meta + n_seq:logical + tail, :] = jnp.zeros((behind, cb), F32)
        sdz[n_meta + n_seq:logical + tail, :] = jnp.zeros((behind, cb), F32)
        accsc[...] = jnp.zeros_like(accsc)
        acccf[...] = jnp.zeros_like(acccf)
        accb[...] = jnp.zeros_like(accb)

        def forward_chunk(i, carry):
            t0 = pl.multiple_of(i * ch, ch)
            rws = pl.ds(t0, ch)
            dy = dysc_ref[rws, :].astype(F32)
            ds = dy * b_ref[rws, :].astype(F32)
            dz = dzc_ref[rws, :]
            sds_[pl.ds(n_meta + t0, ch), :] = ds
            sdz[pl.ds(n_meta + t0, ch), :] = dz
            s = jnp.zeros((ch, cb), F32)
            for k, win in enumerate(_taps(scv, t0 + a_sc, [off_sc - a_sc + k for k in range(SC_WIDTH)], ch)):
                s = s + win * w3[k:k + 1, :]
                accsc[k * SUBLANES:(k + 1) * SUBLANES, :] += _rows8(ds * win)
            dub_ref[rws, :] = (dy * s).astype(BF16)
            for k, win in enumerate(_taps(sz, t0 + a_cf, [off_cf - a_cf + k for k in range(CF_WIDTH)], ch)):
                acccf[k * SUBLANES:(k + 1) * SUBLANES, :] += _rows8(dz * win)
            accb[...] += _rows8(dz)
            return carry

        lax.fori_loop(0, n_seq // ch, forward_chunk, 0)

        def backward_chunk(i, carry):
            p0 = pl.multiple_of(i * ch, ch)
            dcv = jnp.zeros((ch, cb), F32)
            for k, win in enumerate(_taps(sds_, p0, [SC_WIDTH - 1 - k for k in range(SC_WIDTH)], ch)):
                dcv = dcv + win * w3[k:k + 1, :]
            dlcv[pl.ds(p0, ch), :] = dcv
            dzi = jnp.zeros((ch, cb), F32)
            for k, win in enumerate(_taps(sdz, p0, [CF_WIDTH - 1 - k for k in range(CF_WIDTH)], ch)):
                dzi = dzi + win * w31[k:k + 1, :]
            dlz[pl.ds(p0, ch), :] = dzi
            return carry

        lax.fori_loop(0, logical // ch, backward_chunk, 0)

        def gates(phys, logi):
            dcv, dzi = dlcv[logi, :], dlz[logi, :]
            duc_ref[phys, :] = (dcv * v_ref[phys, :].astype(F32)).astype(BF16)
            duv_ref[phys, :] = (dcv * c_ref[phys, :].astype(F32)).astype(BF16)
            s = _sigmoid(g_ref[phys, :].astype(F32))
            dua_ref[phys, :] = (dzi * s).astype(BF16)
            dug_ref[phys, :] = (dzi * a_ref[phys, :].astype(F32) * s * (1.0 - s)).astype(BF16)

        def gate_chunk(i, carry):
            t0 = pl.multiple_of(i * ch, ch)
            gates(pl.ds(t0, ch), pl.ds(n_meta + t0, ch))
            return carry

        lax.fori_loop(0, n_seq // ch, gate_chunk, 0)
        gates(pl.ds(n_seq, n_meta), pl.ds(0, n_meta))
        dub_ref[n_seq:rows, :] = jnp.zeros((rows - n_seq, cb), BF16)
        pad0 = n_seq + n_meta
        for ref in (duc_ref, duv_ref, dua_ref, dug_ref):
            ref[pad0:rows, :] = jnp.zeros((rows - pad0, cb), BF16)
        dwsc_ref[0] = jnp.sum(accsc[...].reshape(SC_WIDTH, SUBLANES, cb), axis=1)
        dwcf_ref[0] = jnp.sum(acccf[...].reshape(CF_WIDTH, SUBLANES, cb), axis=1)
        dcb_ref[...] = jnp.sum(accb[...], axis=0, keepdims=True)

    ucol = [pl.BlockSpec((rows, cb), functools.partial(lambda m, j: (0, m * nb + j), m)) for m in range(5)]
    blk = pl.BlockSpec((dysc.shape[0], cb), lambda j: (0, j))
    wsc_spec = pl.BlockSpec((1, SC_WIDTH, cb), lambda j: (j, 0, 0))
    wcf_spec = pl.BlockSpec((1, CF_WIDTH, cb), lambda j: (j, 0, 0))
    outs = _call(
        body, comm=comm, name=name, grid=(nb,),
        in_specs=ucol + [blk, blk, wsc_spec, wcf_spec],
        out_specs=[pl.BlockSpec((5, rows, cb), lambda j: (0, 0, j)), wsc_spec, wcf_spec,
                   pl.BlockSpec((1, cb), lambda j: (0, j))],
        out_shape=[_sds((5, rows, dc), BF16), _sds((nb, SC_WIDTH, cb), F32), _sds((nb, CF_WIDTH, cb), F32),
                   _sds((1, dc), F32)],
        scratch_shapes=[pltpu.VMEM((base + n_seq, cb), F32)] * 2 + [pltpu.VMEM((logical + tail, cb), F32)] * 2
        + [pltpu.VMEM((logical, cb), F32)] * 2
        + [pltpu.VMEM((SC_WIDTH * SUBLANES, cb), F32), pltpu.VMEM((CF_WIDTH * SUBLANES, cb), F32),
           pltpu.VMEM((SUBLANES, cb), F32)],
        compiler_params=_params(),
    )(u, u, u, u, u, dysc, dzc, wsc, wcf)
    return outs


def _row_tile(rows, cols):
    return rows // 4 if rows % 64 == 0 and rows * cols >= (1 << 18) else rows


def _pair_sum(grad, sib, idx, name, comm=()):
    _, rows, cols = grad.shape
    tr = rows

    def body(idx_ref, g_ref, s_ref, o_ref):
        o_ref[0] = (g_ref[0].astype(F32) + s_ref[0].astype(F32)).astype(o_ref.dtype)

    return _call(
        body, name=name,
        grid_spec=pltpu.PrefetchScalarGridSpec(
            num_scalar_prefetch=1, grid=(4, rows // tr),
            in_specs=[pl.BlockSpec((1, tr, cols), lambda k, i, idx_ref: (idx_ref[k], i, 0)),
                      pl.BlockSpec((1, tr, cols), lambda k, i, idx_ref: (idx_ref[4 + k], i, 0))],
            out_specs=pl.BlockSpec((1, tr, cols), lambda k, i, idx_ref: (k, i, 0))),
        out_shape=_sds((4, rows, cols), grad.dtype), compiler_params=_params(),
    )(idx, grad, sib)


def _adamw_math(w, g, m, v):
    m = ADAM_B1 * m + (1.0 - ADAM_B1) * g
    v = ADAM_B2 * v + (1.0 - ADAM_B2) * (g * g)
    m_hat = m / (1.0 - ADAM_B1 ** ADAM_STEP)
    v_hat = v / (1.0 - ADAM_B2 ** ADAM_STEP)
    delta = -ADAM_LR * (m_hat / (jnp.sqrt(v_hat) + ADAM_EPS) + ADAM_WD * w)
    return delta, m, v


def _adamw_sharded(own, got, w, m, v, name, comm=()):
    rows, cols = w.shape
    tr = _row_tile(rows, cols)

    def body(own_ref, g0_ref, g1_ref, g2_ref, w_ref, m_ref, v_ref, g_ref, d_ref, nm_ref, nv_ref):
        g = own_ref[0].astype(F32) + g0_ref[0].astype(F32) + g1_ref[0].astype(F32) + g2_ref[0].astype(F32)
        delta, nm, nv = _adamw_math(w_ref[...], g, m_ref[...], v_ref[...])
        g_ref[...] = g
        d_ref[...] = delta
        nm_ref[...] = nm
        nv_ref[...] = nv

    flat = pl.BlockSpec((tr, cols), lambda i: (i, 0))
    slot = [pl.BlockSpec((1, tr, cols), functools.partial(lambda k, i: (k, i, 0), k)) for k in range(3)]
    return _call(
        body, comm=comm, name=name, grid=(rows // tr,),
        in_specs=[slot[0]] + slot + [flat] * 3, out_specs=[flat] * 4, out_shape=[_sds((rows, cols), F32)] * 4,
        compiler_params=_params(),
    )(own, got, got, got, w, m, v)


def _adamw_replicated(gathered, segs, ws, ms, vs, loss_scale, name, comm=()):
    n = len(ws)

    def body(*refs):
        gat = refs[0]
        w_refs, m_refs, v_refs = refs[1:1 + n], refs[1 + n:1 + 2 * n], refs[1 + 2 * n:1 + 3 * n]
        outs = refs[1 + 3 * n:]

        def total(off, width):
            s = gat[0, :, off:off + width]
            for k in range(1, N_DEV):
                s = s + gat[k, :, off:off + width]
            return s

        outs[0][...] = loss_scale * total(segs[n][0], segs[n][1])
        for p in range(n):
            g = total(*segs[p])
            delta, nm, nv = _adamw_math(w_refs[p][...], g, m_refs[p][...], v_refs[p][...])
            for q, val in enumerate((g, delta, nm, nv)):
                outs[1 + 4 * p + q][...] = val
        for e, seg in enumerate(segs[n + 1:]):
            outs[1 + 4 * n + e][...] = total(*seg)

    return _call(
        body, name=name,
        out_shape=[_sds((1, segs[n][1]), F32)] + [_sds(w.shape, F32) for w in ws for _ in range(4)]
        + [_sds((1, seg[1]), F32) for seg in segs[n + 1:]],
        compiler_params=_params(),
    )(gathered, *ws, *ms, *vs)


def _adamw_plain(g, w, m, v, name):
    def body(g_ref, w_ref, m_ref, v_ref, d_ref, nm_ref, nv_ref):
        d_ref[...], nm_ref[...], nv_ref[...] = _adamw_math(w_ref[...], g_ref[...], m_ref[...], v_ref[...])

    return list(_call(body, name=name, out_shape=[_sds(w.shape, F32)] * 3)(g, w, m, v))


REPLICATED = ("ffn1_norm", "mix_norm", "b_in", "conv_cf_b", "ln_cf_g", "ln_cf_b", "ffn2_norm", "final_norm")
SHARDED = ("meta_tokens", "ffn1_w_gate", "ffn1_w_up", "ffn1_w_down", "w_in", "conv_sc_w", "conv_cf_w", "w_out",
           "ffn2_w_gate", "ffn2_w_up", "ffn2_w_down")
WEIGHTS = ("meta_tokens", "ffn1_norm", "ffn1_w_gate", "ffn1_w_up", "ffn1_w_down", "mix_norm", "w_in", "b_in",
           "conv_sc_w", "conv_cf_w", "conv_cf_b", "ln_cf_g", "ln_cf_b", "w_out", "ffn2_norm", "ffn2_w_gate",
           "ffn2_w_up", "ffn2_w_down", "final_norm")


TRANSPOSED = ("ffn1_w_gate", "ffn1_w_up", "ffn2_w_gate", "ffn2_w_up")


def _blocks2d(k, a):
    a = a.reshape(a.shape[-2:]) if a.ndim >= 2 else a.reshape(1, -1)
    return a.T if k in TRANSPOSED else a


def _step(x, tgt, w, m, v):
    n_seq, d = x.shape[1], x.shape[2]
    n_meta = w["meta_tokens"].shape[0]
    rows = -(-(n_seq + n_meta) // ROW_ALIGN) * ROW_ALIGN
    tm = rows // N_ROW_TILES
    cb = w["conv_sc_w"].shape[-1]
    dc = cb * N_DEV
    w2 = {k: _blocks2d(k, a) for k, a in w.items()}
    m2 = {k: _blocks2d(k, a) for k, a in m.items()}
    v2 = {k: _blocks2d(k, a) for k, a in v.items()}

    def as_given(k, r):
        return (r.T if k in TRANSPOSED else r).reshape(w[k].shape)

    def cast(k):
        return w2[k].astype(BF16)

    full = dict(zip(("ffn1_w_gate", "meta_tokens", "conv_sc_w", "conv_cf_w"), _exchange_alone(
        _gather_all([cast("ffn1_w_gate"), w2["meta_tokens"], w2["conv_sc_w"], w2["conv_cf_w"]]), "gather_gate1")))
    meta = jnp.transpose(full["meta_tokens"], (1, 0, 2)).reshape(n_meta, d)
    hs0 = jnp.concatenate([x[0], meta, jnp.zeros((rows - n_seq - n_meta, d), F32)], axis=0)

    tx = n_seq // N_ROW_TILES
    h1, r1 = _rms_fwd(hs0, w2["ffn1_norm"], tm, "rms_in")
    up1 = _gather_all([cast("ffn1_w_up")])
    g1 = _ffn_proj(h1, full["ffn1_w_gate"], tm, "ffn1_gate", comm=[up1])
    down1 = _gather_all([cast("ffn1_w_down")])
    u1 = _ffn_proj(h1, up1.results[0], tm, "ffn1_up", comm=[down1])
    w_in_all = _gather_all([cast("w_in")])
    hs1, h2, r2 = _ffn_down_norm(g1, u1, down1.results[0], hs0, w2["mix_norm"], rows // DOWN_ROW_TILES, "ffn1_down",
                                 comm=[w_in_all])
    win = jnp.transpose(w_in_all.results[0], (1, 0, 2)).reshape(d, -1)
    gate2_w, up2_w, down2_w = cast("ffn2_w_gate"), cast("ffn2_w_up"), cast("ffn2_w_down")
    w_out_all, gate2a = _gather_all([cast("w_out")]), _gather_range(gate2_w, 0, 2)
    u = _win_fwd(h2, win, w2["b_in"], dc, tm, "mix_in", comm=[w_out_all, gate2a])
    wout = w_out_all.results[0].reshape(-1, d)
    gate2b = _gather_range(gate2_w, 1, 2, prev=gate2a.results[0])
    ysc, zc = _conv_fwd(u, full["conv_sc_w"], full["conv_cf_w"], w2["conv_cf_b"], n_seq, n_meta, "conv_fwd", comm=[gate2b])
    up2a = _gather_range(up2_w, 0, 2)
    y, hs2, h3, r3 = _wout_fwd(zc, ysc, wout, hs1, w2["ln_cf_g"], w2["ln_cf_b"], w2["ffn2_norm"], n_seq, tx, "mix_out",
                               comm=[up2a])
    up2b, down2a = _gather_range(up2_w, 1, 2, prev=up2a.results[0]), _gather_range(down2_w, 0, 2)
    g2 = _ffn_proj(h3, gate2b.results[0], tx, "ffn2_gate", comm=[up2b, down2a])
    down2b = _gather_range(down2_w, 1, 2, prev=down2a.results[0])
    u2 = _ffn_proj(h3, up2b.results[0], tx, "ffn2_up", comm=[down2b])
    full.update(ffn1_w_up=up1.results[0], ffn1_w_down=down1.results[0], ffn2_w_gate=gate2b.results[0],
                ffn2_w_up=up2b.results[0], ffn2_w_down=down2b.results[0])
    dhs3, dhs3b, loss_p, dgf_p = _ffn_down_loss(
        g2, u2, full["ffn2_w_down"], hs2, w2["final_norm"], tgt[0], n_seq, n_seq // DOWN_ROW_TILES, "ffn2_down_loss")

    xi, yi, ci = lax.axis_index("x"), lax.axis_index("y"), lax.axis_index("c")
    chip_of = [2 * xi + yi, 2 * (1 - xi) + yi, 2 * xi + (1 - yi), 2 * (1 - xi) + (1 - yi)]
    idx = jnp.stack([2 * ch + ci for ch in chip_of] + chip_of).astype(jnp.int32)
    out = {}

    def to_pairs(parts):
        return _pair_exchange([p.reshape((4, 2) + p.shape[1:]) for p in parts])

    def pair_sums(names, parts, pairs):
        return [_pair_sum(p, s, idx, "pair_sum_" + k) for k, p, s in zip(names, parts, pairs.results)]

    def update(names, sums, chips):
        for k, own, got in zip(names, sums, chips.results):
            res = _adamw_sharded(own, got, w2[k], m2[k], v2[k], "adamw_" + k)
            out[k] = [as_given(k, r) for r in res]

    tk = min(2 * ROW_ALIGN, n_seq)
    th_x, th_r = n_seq // DOWN_ROW_TILES, rows // DOWN_ROW_TILES

    dwd2 = _ffn_bwd_dw_down(g2, u2, dhs3b, tk, "ffn2_bwd_dw_down")
    p_d2 = to_pairs([dwd2])
    dg2, du2 = _ffn_bwd_da(dhs3b, full["ffn2_w_down"], g2, u2, tx, "ffn2_bwd_da", comm=[p_d2])
    s_d2 = pair_sums(["ffn2_w_down"], [dwd2], p_d2)
    c_d2 = _chip_exchange(s_d2)
    gu2 = _ffn_bwd_dw_gu(h3, dg2, du2, tk, "ffn2_bwd_dw_gu", comm=[c_d2])
    update(["ffn2_w_down"], s_d2, c_d2)
    p_gu2 = to_pairs(gu2)
    half2 = _ffn_bwd_dh_gate(dg2, full["ffn2_w_gate"], th_x, "ffn2_bwd_dh_gate", comm=[p_gu2])
    s_gu2 = pair_sums(["ffn2_w_gate", "ffn2_w_up"], gu2, p_gu2)
    c_g2, c_u2 = _chip_exchange(s_gu2[:1]), _chip_exchange(s_gu2[1:])
    dhs2, dhs2b, dn3_p = _ffn_bwd_dh(du2, full["ffn2_w_up"], half2, hs2, r3, w2["ffn2_norm"], dhs3, th_x, "ffn2_bwd_dh",
                                     zeros=jnp.zeros((rows, d), F32))
    dysc, dzc, dlg_p, dlb_p = _wout_bwd_dy(dhs2b, wout, zc, w2["ln_cf_g"], w2["ln_cf_b"], tx, "mix_out_bwd_dy")

    dwout = _wout_bwd_dw(y, dhs2b, N_DEV, tk, "mix_out_bwd_dw")
    p_wo = to_pairs([dwout])
    du, dcsw, dccw, dcb = _conv_bwd(u, dysc, dzc, full["conv_sc_w"], full["conv_cf_w"], n_seq, n_meta, "conv_bwd",
                                    comm=[c_g2, p_wo])
    update(["ffn2_w_gate"], s_gu2[:1], c_g2)
    s_wo = pair_sums(["w_out"], [dwout], p_wo)
    c_wo = _chip_exchange(s_wo)
    dhs1, dhs1b, dn2_p = _win_bwd_dh(du, win, hs1, r2, w2["mix_norm"], dhs2, th_r, "mix_in_bwd_dh", comm=[c_u2])
    update(["ffn2_w_up"], s_gu2[1:], c_u2)
    dwin, dbin = _win_bwd_dw(h2, du, tk, "mix_in_bwd_dw", comm=[c_wo])
    update(["w_out"], s_wo, c_wo)
    mixer = ("w_in", "conv_sc_w", "conv_cf_w")
    gradsm = [jnp.transpose(dwin.reshape(d, N_DEV, -1), (1, 0, 2)), dcsw, dccw]
    p_m = to_pairs(gradsm)
    dwd1 = _ffn_bwd_dw_down(g1, u1, dhs1b, tk, "ffn1_bwd_dw_down", comm=[p_m])
    s_m = pair_sums(mixer, gradsm, p_m)
    c_m, p_d1 = _chip_exchange(s_m), to_pairs([dwd1])
    dg1, du1 = _ffn_bwd_da(dhs1b, full["ffn1_w_down"], g1, u1, tm, "ffn1_bwd_da", comm=[c_m, p_d1])
    update(mixer, s_m, c_m)
    s_d1 = pair_sums(["ffn1_w_down"], [dwd1], p_d1)
    c_d1 = _chip_exchange(s_d1)
    gu1 = _ffn_bwd_dw_gu(h1, dg1, du1, tk, "ffn1_bwd_dw_gu", comm=[c_d1])
    update(["ffn1_w_down"], s_d1, c_d1)
    p_gu1 = to_pairs(gu1)
    _exchange_alone(p_gu1, "reduce_pair_ffn1")
    s_gu1 = pair_sums(["ffn1_w_gate", "ffn1_w_up"], gu1, p_gu1)
    c_g1, c_u1 = _chip_exchange(s_gu1[:1]), _chip_exchange(s_gu1[1:])
    half1 = _ffn_bwd_dh_gate(dg1, full["ffn1_w_gate"], th_r, "ffn1_bwd_dh_gate", comm=[c_g1])
    dhs0, _, dn1_p = _ffn_bwd_dh(du1, full["ffn1_w_up"], half1, hs0, r1, w2["ffn1_norm"], dhs1, th_r, "ffn1_bwd_dh",
                                 comm=[c_u1])
    update(["ffn1_w_gate"], s_gu1[:1], c_g1)
    update(["ffn1_w_up"], s_gu1[1:], c_u1)
    grad_x = dhs0[:n_seq][None]

    partial = {
        "ffn1_norm": dn1_p.sum(0), "mix_norm": dn2_p.sum(0), "b_in": dbin, "conv_cf_b": dcb,
        "ln_cf_g": dlg_p.sum(0), "ln_cf_b": dlb_p.sum(0), "ffn2_norm": dn3_p.sum(0), "final_norm": dgf_p.sum(0),
    }
    loss_seg = jnp.pad(loss_p.sum((0, 2)).reshape(1, 1), ((0, 0), (0, 127)))
    pieces = [partial[k] for k in REPLICATED] + [loss_seg, dhs0[n_seq:n_seq + n_meta].reshape(1, n_meta * d)]
    segs, off = [], 0
    for p in pieces:
        segs.append((off, p.shape[1]))
        off += p.shape[1]
    rows8 = _gather_rows(jnp.concatenate(pieces, axis=1), "gather_small")
    res = _adamw_replicated(rows8, segs, [w2[k] for k in REPLICATED], [m2[k] for k in REPLICATED],
                            [v2[k] for k in REPLICATED], 0.5 / d, "adamw_replicated")
    loss = res[0][0, 0]
    for p, k in enumerate(REPLICATED):
        out[k] = [r.reshape(w[k].shape) for r in res[1 + 4 * p:5 + 4 * p]]
    ob = w2["meta_tokens"].shape[1]
    gmeta = lax.dynamic_slice_in_dim(res[-1].reshape(n_meta, d), (4 * xi + 2 * yi + ci) * ob, ob, axis=1)
    out["meta_tokens"] = [gmeta] + _adamw_plain(gmeta, w2["meta_tokens"], m2["meta_tokens"], v2["meta_tokens"], "adamw_meta_tokens")

    return (loss, grad_x, *[out[k][0] for k in WEIGHTS], *[out[k][1] for k in WEIGHTS],
            *[out[k][2] for k in WEIGHTS], *[out[k][3] for k in WEIGHTS])


def kernel(x, meta_tokens, ffn1_norm, ffn1_w_gate, ffn1_w_up, ffn1_w_down, mix_norm, w_in, b_in, conv_sc_w, conv_cf_w, conv_cf_b, ln_cf_g, ln_cf_b, w_out, ffn2_norm, ffn2_w_gate, ffn2_w_up, ffn2_w_down, final_norm, loss_target, m_meta_tokens, m_ffn1_norm, m_ffn1_w_gate, m_ffn1_w_up, m_ffn1_w_down, m_mix_norm, m_w_in, m_b_in, m_conv_sc_w, m_conv_cf_w, m_conv_cf_b, m_ln_cf_g, m_ln_cf_b, m_w_out, m_ffn2_norm, m_ffn2_w_gate, m_ffn2_w_up, m_ffn2_w_down, m_final_norm, v_meta_tokens, v_ffn1_norm, v_ffn1_w_gate, v_ffn1_w_up, v_ffn1_w_down, v_mix_norm, v_w_in, v_b_in, v_conv_sc_w, v_conv_cf_w, v_conv_cf_b, v_ln_cf_g, v_ln_cf_b, v_w_out, v_ffn2_norm, v_ffn2_w_gate, v_ffn2_w_up, v_ffn2_w_down, v_final_norm):
    given = dict(locals())
    w = {k: given[k] for k in WEIGHTS}
    m = {k: given["m_" + k] for k in WEIGHTS}
    v = {k: given["v_" + k] for k in WEIGHTS}
    return _step(x, loss_target, w, m, v)
```

```python
import functools

import jax
import jax.numpy as jnp
from jax import lax
from jax.experimental import pallas as pl
from jax.experimental.pallas import tpu as pltpu

F32 = jnp.float32
BF16 = jnp.bfloat16
EPS = 1e-6
FFN_RES_SCALE = 0.5
SC_WIDTH = 3
CF_WIDTH = 31
ADAM_LR = 0.001
ADAM_B1 = 0.9
ADAM_B2 = 0.999
ADAM_EPS = 1e-08
ADAM_WD = 0.01
ADAM_STEP = 10

N_DEV = 8
N_ROW_TILES = 8
ROW_ALIGN = 256
CONV_PAD = 32
CONV_CH = 128
SUB_ROWS = 32
DOWN_ROW_TILES = 16
GATHER_PARTS = 2
SUBLANES = 8
BF16_ROWS = 16
VMEM_LIMIT = 56 * 1024 * 1024
MESH = pl.DeviceIdType.MESH
ANY = pl.BlockSpec(memory_space=pl.ANY)

NT = (((1,), (1,)), ((), ()))
TN = (((0,), (0,)), ((), ()))


def _pallas(body, **kw):
    return pl.pallas_call(body, **kw)


class _Exchange:
    def __init__(self, inputs, out_shapes, sem_shapes, start, finish, aliases=None, early=None):
        self.inputs, self.out_shapes, self.sem_shapes = list(inputs), list(out_shapes), list(sem_shapes)
        self.start, self.finish, self.aliases = start, finish, dict(aliases or {})
        self.early, self.early_at = early, None
        self.results = None

    def before(self, share):
        self.early_at = share
        return self


def _call(body, comm=(), **kw):
    if not comm:
        return _pallas(body, **kw)
    grid = kw.pop("grid")
    in_specs = list(kw.pop("in_specs"))
    out_specs, out_shape = kw.pop("out_specs"), kw.pop("out_shape")
    scratch = list(kw.pop("scratch_shapes", []))
    single = not isinstance(out_shape, (list, tuple))
    out_specs, out_shape = ([out_specs], [out_shape]) if single else (list(out_specs), list(out_shape))
    n_in, n_out, n_scr = len(in_specs), len(out_shape), len(scratch)
    c_in = [a for job in comm for a in job.inputs]
    c_out = [s for job in comm for s in job.out_shapes]
    c_sem = [s for job in comm for s in job.sem_shapes]
    aliases, i0, o0 = dict(kw.pop("input_output_aliases", {})), n_in, n_out
    for job in comm:
        aliases.update({i0 + i: o0 + o for i, o in job.aliases.items()})
        i0, o0 = i0 + len(job.inputs), o0 + len(job.out_shapes)

    def hosted(*refs):
        pos = [0]

        def take(n):
            pos[0] += n
            return refs[pos[0] - n:pos[0]]

        ins, cins, outs, couts, scr, sems = take(n_in), take(len(c_in)), take(n_out), take(len(c_out)), take(n_scr), take(len(c_sem))
        ids = [pl.program_id(k) for k in range(len(grid))]
        first = functools.reduce(jnp.logical_and, [i == 0 for i in ids])
        last = functools.reduce(jnp.logical_and, [i == g - 1 for i, g in zip(ids, grid)])

        n_steps = functools.reduce(lambda a, b: a * b, grid, 1)
        step = functools.reduce(lambda acc, ig: acc * ig[1] + ig[0], zip(ids, grid), 0)

        def early_step(job):
            if job.early is None or job.early_at is None or n_steps < 2:
                return None
            return min(max(int(job.early_at * n_steps), 0), n_steps - 2)

        def each(phase, chosen=None):
            i, o, s = 0, 0, 0
            for job in comm:
                ni, no, ns = len(job.inputs), len(job.out_shapes), len(job.sem_shapes)
                if chosen is None or chosen(job):
                    getattr(job, phase)(cins[i:i + ni], couts[o:o + no], sems[s:s + ns])
                i, o, s = i + ni, o + no, s + ns

        @pl.when(first)
        def _():
            each("start")

        body(*ins, *outs, *scr)

        for at in sorted({early_step(job) for job in comm} - {None}):
            @pl.when(step == at)
            def _():
                each("early", lambda job: early_step(job) == at)

        @pl.when(last)
        def _():
            each("early", lambda job: job.early is not None and early_step(job) is None)
            each("finish")

    call = _pallas(
        hosted, grid=grid, in_specs=in_specs + [ANY] * len(c_in), out_specs=out_specs + [ANY] * len(c_out),
        out_shape=out_shape + c_out, scratch_shapes=scratch + c_sem, input_output_aliases=aliases, **kw)

    def run(*args):
        res = call(*args, *c_in)
        o = n_out
        for job in comm:
            job.results = list(res[o:o + len(job.out_shapes)])
            o += len(job.out_shapes)
        return res[0] if single else list(res[:n_out])

    return run


def _exchange_alone(job, name, comm=()):
    n_in, n_out = len(job.inputs), len(job.out_shapes)

    def body(*refs):
        ins, outs, sems = refs[:n_in], refs[n_in:n_in + n_out], refs[n_in + n_out:]
        job.start(ins, outs, sems)
        if job.early is not None:
            job.early(ins, outs, sems)
        job.finish(ins, outs, sems)

    res = _pallas(body, name=name, in_specs=[ANY] * n_in, out_specs=[ANY] * n_out, out_shape=job.out_shapes,
                  scratch_shapes=job.sem_shapes, input_output_aliases=job.aliases)(*job.inputs)
    job.results = list(res)
    return job.results


def _params(**kw):
    return pltpu.CompilerParams(vmem_limit_bytes=VMEM_LIMIT, **kw)


def _sigmoid(x):
    return 0.5 * jnp.tanh(0.5 * x) + 0.5


def _sds(shape, dtype):
    return jax.ShapeDtypeStruct(shape, dtype)


def _place():
    x, y, c = lax.axis_index("x"), lax.axis_index("y"), lax.axis_index("c")
    chips = [(1 - x, y), (x, 1 - y), (1 - x, 1 - y)]
    return x, y, c, chips


def _slot(ref, p):
    return ref.at[4 * p[0] + 2 * p[1] + p[2]]


def _remote(src, dst, send_sem, recv_sem, to):
    return pltpu.make_async_remote_copy(src_ref=src, dst_ref=dst, send_sem=send_sem, recv_sem=recv_sem,
                                        device_id=to, device_id_type=MESH)


def _gather_direct(arrs):
    n = len(arrs)

    def copies(ins, outs, sems):
        send_sems, recv_sems, local_sems = sems
        x, y, c, chips = _place()
        me = (x, y, c)
        peers = [(x, y, 1 - c)] + [(*chip, c) for chip in chips]
        local = [pltpu.make_async_copy(ins[a], _slot(outs[a], me), local_sems.at[a]) for a in range(n)]
        sends = [_remote(ins[a], _slot(outs[a], me), send_sems.at[a, k], recv_sems.at[a, k], peer)
                 for a in range(n) for k, peer in enumerate(peers)]
        arrivals = [_remote(ins[a], _slot(outs[a], peer), send_sems.at[a, k], recv_sems.at[a, k], peer)
                    for a in range(n) for k, peer in enumerate(peers)]
        return local, sends, arrivals

    def start(ins, outs, sems):
        local, sends, _ = copies(ins, outs, sems)
        for cp in local + sends:
            cp.start()

    def finish(ins, outs, sems):
        local, sends, arrivals = copies(ins, outs, sems)
        for cp in arrivals:
            cp.wait_recv()
        for cp in sends:
            cp.wait_send()
        for cp in local:
            cp.wait()

    dma = pltpu.SemaphoreType.DMA
    return _Exchange(arrs, [_sds((N_DEV,) + a.shape, a.dtype) for a in arrs], [dma((n, 4)), dma((n, 4)), dma((n,))],
                     start, finish)


def _gather_forward(gathered):
    n = len(gathered)

    def copies(ins, outs, sems):
        send_sems, recv_sems = sems
        x, y, c, chips = _place()
        sibling = (x, y, 1 - c)
        sends = [_remote(_slot(ins[a], (*chip, c)), _slot(outs[a], (*chip, c)), send_sems.at[a, j], recv_sems.at[a, j], sibling)
                 for a in range(n) for j, chip in enumerate(chips)]
        arrivals = [_remote(_slot(ins[a], (*chip, c)), _slot(outs[a], (*chip, 1 - c)), send_sems.at[a, j], recv_sems.at[a, j], sibling)
                    for a in range(n) for j, chip in enumerate(chips)]
        return sends, arrivals

    def start(ins, outs, sems):
        for cp in copies(ins, outs, sems)[0]:
            cp.start()

    def finish(ins, outs, sems):
        sends, arrivals = copies(ins, outs, sems)
        for cp in arrivals:
            cp.wait_recv()
        for cp in sends:
            cp.wait_send()

    dma = pltpu.SemaphoreType.DMA
    return _Exchange(gathered, [_sds(a.shape, a.dtype) for a in gathered], [dma((n, 3)), dma((n, 3))], start, finish,
                     aliases={a: a for a in range(n)})


def _gather_all(arrs, parts=GATHER_PARTS):
    n = len(arrs)
    split = [parts if a.shape[0] % (parts * BF16_ROWS) == 0 else 1 for a in arrs]
    members = [[a for a in range(n) if p < split[a]] for p in range(parts)]

    def part_of(a, p):
        size = arrs[a].shape[0] // split[a]
        return pl.ds(p * size, size), _sds((size,) + arrs[a].shape[1:], arrs[a].dtype)

    direct = [_gather_direct([part_of(a, p)[1] for a in members[p]]) for p in range(parts)]
    passed = [_gather_forward([_sds((N_DEV,) + part_of(a, p)[1].shape, arrs[a].dtype) for a in members[p]])
              for p in range(parts)]
    jobs = direct + passed
    sem_shapes = [s for job in jobs for s in job.sem_shapes]

    def views(ins, outs, sems):
        found, at = [], 0
        for i, job in enumerate(jobs):
            p = i % parts
            rows = [part_of(a, p)[0] for a in members[p]]
            found.append(([ins[a].at[r] for a, r in zip(members[p], rows)],
                          [outs[a].at[:, r] for a, r in zip(members[p], rows)],
                          sems[at:at + len(job.sem_shapes)]))
            at += len(job.sem_shapes)
        return found

    def start(ins, outs, sems):
        for job, (i, o, s) in list(zip(jobs, views(ins, outs, sems)))[:parts]:
            job.start(i, o, s)

    def early(ins, outs, sems):
        found = views(ins, outs, sems)
        for p in range(parts):
            i, o, s = found[p]
            direct[p].finish(i, o, s)
            _, o2, s2 = found[parts + p]
            passed[p].start(o2, o2, s2)

    def finish(ins, outs, sems):
        found = views(ins, outs, sems)
        for p in range(parts):
            _, o2, s2 = found[parts + p]
            passed[p].finish(o2, o2, s2)

    return _Exchange(arrs, [_sds((N_DEV,) + a.shape, a.dtype) for a in arrs], sem_shapes, start, finish, early=early)


def _pair_exchange(arrs):
    n = len(arrs)

    def copies(ins, outs, sems):
        x, y, c, _ = _place()
        return [_remote(ins[a].at[:, 1 - c], outs[a], sems[0].at[a], sems[1].at[a], (x, y, 1 - c)) for a in range(n)]

    def start(ins, outs, sems):
        for cp in copies(ins, outs, sems):
            cp.start()

    def finish(ins, outs, sems):
        for cp in copies(ins, outs, sems):
            cp.wait()

    dma = pltpu.SemaphoreType.DMA
    return _Exchange(arrs, [_sds((4,) + a.shape[2:], a.dtype) for a in arrs], [dma((n,)), dma((n,))], start, finish)


def _chip_exchange(arrs):
    n = len(arrs)

    def copies(ins, outs, sems):
        x, y, c, chips = _place()
        return [_remote(ins[a].at[1 + j], outs[a].at[j], sems[0].at[a, j], sems[1].at[a, j], (*chip, c))
                for a in range(n) for j, chip in enumerate(chips)]

    def start(ins, outs, sems):
        for cp in copies(ins, outs, sems):
            cp.start()

    def finish(ins, outs, sems):
        for cp in copies(ins, outs, sems):
            cp.wait()

    dma = pltpu.SemaphoreType.DMA
    return _Exchange(arrs, [_sds((3,) + a.shape[1:], a.dtype) for a in arrs], [dma((n, 3)), dma((n, 3))], start, finish)


def _gather_rows(vec, name, comm=()):
    def body(in_ref, out_ref, send_sems, recv_sems, local_sem):
        x, y, c, _ = _place()
        me = 4 * x + 2 * y + c
        mine = pltpu.make_async_copy(in_ref, out_ref.at[me], local_sem)
        mine.start()
        copies = []
        for k in range(1, N_DEV):
            to = (1 - x if k & 4 else x, 1 - y if k & 2 else y, 1 - c if k & 1 else c)
            copies.append(pltpu.make_async_remote_copy(
                src_ref=in_ref, dst_ref=out_ref.at[me], send_sem=send_sems.at[k - 1], recv_sem=recv_sems.at[k - 1],
                device_id=to, device_id_type=MESH))
        for cp in copies:
            cp.start()
        for cp in copies:
            cp.wait()
        mine.wait()

    return _call(
        body, name=name, out_shape=_sds((N_DEV,) + vec.shape, vec.dtype), in_specs=[ANY], out_specs=ANY,
        scratch_shapes=[pltpu.SemaphoreType.DMA((7,)), pltpu.SemaphoreType.DMA((7,)), pltpu.SemaphoreType.DMA],
    )(vec)


def _group_rows(tm):
    return SUB_ROWS if tm % SUB_ROWS == 0 else BF16_ROWS


def _for_row_groups(tm, fn):
    sub = _group_rows(tm)

    def step(i, carry):
        fn(pl.ds(pl.multiple_of(i * sub, sub), sub), i * sub)
        return carry

    lax.fori_loop(0, tm // sub, step, 0)


def _once(shape, index_map):
    return pl.BlockSpec(shape, index_map, pipeline_mode=pl.Buffered(1))


def _rms_fwd(hs, g, tm, name, comm=()):
    rows, d = hs.shape

    def body(hs_ref, g_ref, h_ref, r_ref):
        def group(rs, _):
            xv = hs_ref[rs, :]
            r = lax.rsqrt(jnp.mean(xv * xv, axis=-1, keepdims=True) + EPS)
            h_ref[rs, :] = (xv * r * g_ref[...]).astype(BF16)
            r_ref[rs, :] = r

        _for_row_groups(tm, group)

    return _call(
        body, comm=comm, name=name, grid=(rows // tm,),
        in_specs=[pl.BlockSpec((tm, d), lambda i: (i, 0)), pl.BlockSpec((1, d), lambda i: (0, 0))],
        out_specs=[pl.BlockSpec((tm, d), lambda i: (i, 0)), pl.BlockSpec((tm, 1), lambda i: (i, 0))],
        out_shape=[_sds((rows, d), BF16), _sds((rows, 1), F32)], compiler_params=_params(),
    )(hs, g)


def _ffn_gu(h, wg, wu, tm, name, comm=()):
    rows, d = h.shape
    nb, fb, _ = wg.shape

    def body(h_ref, wg_ref, wu_ref, g_ref, u_ref):
        hv = h_ref[...]
        g_ref[0] = lax.dot_general(hv, wg_ref[0], NT, preferred_element_type=F32).astype(BF16)
        u_ref[0] = lax.dot_general(hv, wu_ref[0], NT, preferred_element_type=F32).astype(BF16)

    wspec = pl.BlockSpec((1, fb, d), lambda j, i: (j, 0, 0))
    ospec = pl.BlockSpec((1, tm, fb), lambda j, i: (j, i, 0))
    return _call(
        body, comm=comm, name=name, grid=(nb, rows // tm),
        in_specs=[pl.BlockSpec((tm, d), lambda j, i: (i, 0)), wspec, wspec],
        out_specs=[ospec, ospec], out_shape=[_sds((nb, rows, fb), BF16)] * 2, compiler_params=_params(),
    )(h, wg, wu)


def _ffn_proj(h, w, tm, name, comm=()):
    rows, d = h.shape
    nb, fb, _ = w.shape

    def body(h_ref, w_ref, p_ref):
        p_ref[0] = lax.dot_general(h_ref[...], w_ref[0], NT, preferred_element_type=F32).astype(BF16)

    return _call(
        body, comm=comm, name=name, grid=(nb, rows // tm),
        in_specs=[pl.BlockSpec((tm, d), lambda j, i: (i, 0)), pl.BlockSpec((1, fb, d), lambda j, i: (j, 0, 0))],
        out_specs=pl.BlockSpec((1, tm, fb), lambda j, i: (j, i, 0)), out_shape=_sds((nb, rows, fb), BF16),
        compiler_params=_params(),
    )(h, w)


def _down_product(g_ref, u_ref, wd_ref, acc_ref):
    part = None
    for b in range(g_ref.shape[0]):
        gv = g_ref[b]
        a = gv * _sigmoid(gv) * u_ref[b]
        p = jnp.dot(a, wd_ref[b], preferred_element_type=F32)
        part = p if part is None else part + p
    acc_ref[...] = part


def _ffn_down_norm(g, u, wd, hs, gn, tm, name, comm=()):
    nb, rows, fb = g.shape
    d = hs.shape[1]

    def body(g_ref, u_ref, wd_ref, hs_ref, gn_ref, hsn_ref, hn_ref, rn_ref, acc_ref):
        _down_product(g_ref, u_ref, wd_ref, acc_ref)

        def group(rs, _):
            hsn = hs_ref[rs, :] + FFN_RES_SCALE * acc_ref[rs, :]
            r = lax.rsqrt(jnp.mean(hsn * hsn, axis=-1, keepdims=True) + EPS)
            hsn_ref[rs, :] = hsn
            hn_ref[rs, :] = (hsn * r * gn_ref[...]).astype(BF16)
            rn_ref[rs, :] = r

        _for_row_groups(tm, group)

    aspec = pl.BlockSpec((nb, tm, fb), lambda i: (0, i, 0))
    row = pl.BlockSpec((tm, d), lambda i: (i, 0))
    return _call(
        body, comm=comm, name=name, grid=(rows // tm,),
        in_specs=[aspec, aspec, _once((nb, fb, d), lambda i: (0, 0, 0)), row, pl.BlockSpec((1, d), lambda i: (0, 0))],
        out_specs=[row, row, pl.BlockSpec((tm, 1), lambda i: (i, 0))],
        out_shape=[_sds((rows, d), F32), _sds((rows, d), BF16), _sds((rows, 1), F32)],
        scratch_shapes=[pltpu.VMEM((tm, d), F32)], compiler_params=_params(),
    )(g, u, wd, hs, gn)


def _ffn_down_loss(g, u, wd, hs, gf, tgt, n_seq, tm, name, comm=()):
    nb, rows, fb = g.shape
    d = hs.shape[1]
    nt = rows // tm

    def body(g_ref, u_ref, wd_ref, hs_ref, gf_ref, tgt_ref, dhs_ref, dhsb_ref, loss_ref, dgf_ref, acc_ref):
        i = pl.program_id(0)
        _down_product(g_ref, u_ref, wd_ref, acc_ref)
        loss_ref[0] = jnp.zeros((1, d), F32)
        dgf_ref[0] = jnp.zeros((1, d), F32)

        def group(rs, r0):
            hs3 = hs_ref[rs, :] + FFN_RES_SCALE * acc_ref[rs, :]
            r = lax.rsqrt(jnp.mean(hs3 * hs3, axis=-1, keepdims=True) + EPS)
            gfv = gf_ref[...]
            y = hs3 * r
            rowid = i * tm + r0 + lax.broadcasted_iota(jnp.int32, (_group_rows(tm), 1), 0)
            err = jnp.where(rowid < n_seq, y * gfv - tgt_ref[rs, :], 0.0)
            loss_ref[0] += jnp.sum(err * err, axis=0, keepdims=True)
            dout = err * (1.0 / d)
            dgf_ref[0] += jnp.sum(dout * y, axis=0, keepdims=True)
            t = dout * gfv
            dhs = r * t - hs3 * (r * r * r) * jnp.mean(t * hs3, axis=-1, keepdims=True)
            dhs_ref[rs, :] = dhs
            dhsb_ref[rs, :] = dhs.astype(BF16)

        _for_row_groups(tm, group)

    aspec = pl.BlockSpec((nb, tm, fb), lambda i: (0, i, 0))
    row = pl.BlockSpec((tm, d), lambda i: (i, 0))
    part = pl.BlockSpec((1, 1, d), lambda i: (i, 0, 0))
    return _call(
        body, comm=comm, name=name, grid=(nt,),
        in_specs=[aspec, aspec, _once((nb, fb, d), lambda i: (0, 0, 0)), row, pl.BlockSpec((1, d), lambda i: (0, 0)), row],
        out_specs=[row, row, part, part],
        out_shape=[_sds((rows, d), F32), _sds((rows, d), BF16), _sds((nt, 1, d), F32), _sds((nt, 1, d), F32)],
        scratch_shapes=[pltpu.VMEM((tm, d), F32)], compiler_params=_params(),
    )(g, u, wd, hs, gf, tgt)


def _ffn_bwd_da(do, wd, g, u, tm, name, comm=()):
    nb, rows, fb = g.shape
    d = do.shape[1]

    def body(do_ref, wd_ref, g_ref, u_ref, dg_ref, du_ref):
        rs = pl.ds(pl.multiple_of(pl.program_id(1) * tm, tm), tm)
        da = (FFN_RES_SCALE * lax.dot_general(do_ref[rs, :], wd_ref[0], NT, preferred_element_type=F32)).astype(BF16)
        gv, uv = g_ref[0], u_ref[0]
        s = _sigmoid(gv)
        sg = gv * s
        du_ref[0] = da * sg
        dg_ref[0] = da * uv * (s + sg * (1.0 - s))

    aspec = pl.BlockSpec((1, tm, fb), lambda j, i: (j, i, 0))
    return _call(
        body, comm=comm, name=name, grid=(nb, rows // tm),
        in_specs=[_once((rows, d), lambda j, i: (0, 0)), pl.BlockSpec((1, fb, d), lambda j, i: (j, 0, 0)), aspec, aspec],
        out_specs=[aspec] * 2, out_shape=[_sds((nb, rows, fb), BF16)] * 2, compiler_params=_params(),
    )(do, wd, g, u)


def _rms_bwd(tm, d, dh_ref, hs_ref, r_ref, gn_ref, dres_ref, dhs_ref, dhsb_ref, dgn_ref):
    dgn_ref[0] = jnp.zeros((1, d), F32)

    def group(rs, _):
        dh, hs, r = dh_ref[rs, :], hs_ref[rs, :], r_ref[rs, :]
        dgn_ref[0] += jnp.sum(dh * (hs * r), axis=0, keepdims=True)
        t = dh * gn_ref[...]
        dhs = dres_ref[rs, :] + r * t - hs * (r * r * r) * jnp.mean(t * hs, axis=-1, keepdims=True)
        dhs_ref[rs, :] = dhs
        dhsb_ref[rs, :] = dhs.astype(BF16)

    _for_row_groups(tm, group)


def _blocks_dot(x_ref, w_ref):
    part = None
    for b in range(x_ref.shape[0]):
        p = jnp.dot(x_ref[b], w_ref[b], preferred_element_type=F32)
        part = p if part is None else part + p
    return part


def _ffn_bwd_dh_gate(dg, wg, tm, name, comm=()):
    nb, rows, fb = dg.shape
    d = wg.shape[2]

    def body(dg_ref, wg_ref, o_ref):
        o_ref[...] = _blocks_dot(dg_ref, wg_ref)

    return _call(
        body, comm=comm, name=name, grid=(rows // tm,),
        in_specs=[pl.BlockSpec((nb, tm, fb), lambda i: (0, i, 0)), _once((nb, fb, d), lambda i: (0, 0, 0))],
        out_specs=pl.BlockSpec((tm, d), lambda i: (i, 0)), out_shape=_sds((rows, d), F32), compiler_params=_params(),
    )(dg, wg)


def _ffn_bwd_dh(du, wu, half, hs, r, gn, dres, tm, name, comm=(), zeros=None):
    nb, rows, fb = du.shape
    d = hs.shape[1]
    nt = rows // tm
    extra = [] if zeros is None else [zeros]

    def body(du_ref, wu_ref, half_ref, hs_ref, r_ref, gn_ref, dres_ref, *rest):
        dhs_ref, dhsb_ref, dgn_ref, acc_ref = rest[len(extra):]
        acc_ref[...] = half_ref[...] + _blocks_dot(du_ref, wu_ref)
        _rms_bwd(tm, d, acc_ref, hs_ref, r_ref, gn_ref, dres_ref, dhs_ref, dhsb_ref, dgn_ref)

    row = pl.BlockSpec((tm, d), lambda i: (i, 0))
    return _call(
        body, comm=comm, name=name, grid=(nt,),
        in_specs=[pl.BlockSpec((nb, tm, fb), lambda i: (0, i, 0)), _once((nb, fb, d), lambda i: (0, 0, 0)), row, row,
                  pl.BlockSpec((tm, 1), lambda i: (i, 0)), pl.BlockSpec((1, d), lambda i: (0, 0)), row] + [ANY] * len(extra),
        out_specs=[row, row, pl.BlockSpec((1, 1, d), lambda i: (i, 0, 0))],
        out_shape=[_sds((rows, d) if zeros is None else zeros.shape, F32), _sds((rows, d), BF16), _sds((nt, 1, d), F32)],
        scratch_shapes=[pltpu.VMEM((tm, d), F32)], compiler_params=_params(),
        input_output_aliases={7: 0} if extra else {},
    )(du, wu, half, hs, r, gn, dres, *extra)


class _KSteps:
    def __init__(self, rows, tk):
        self.main, self.rem, self.tk = rows // tk, rows % tk, tk
        self.n = self.main + (1 if self.rem else 0)
        if self.rem:
            assert rows % self.rem == 0, (rows, tk)
            self.last = rows // self.rem - 1

    def specs(self, makers):
        main_at = (lambda k: jnp.minimum(k, self.main - 1)) if self.rem else (lambda k: k)
        out = [make(self.tk, main_at) for make in makers]
        if self.rem:
            out += [make(self.rem, lambda k: self.last) for make in makers]
        return out

    def args(self, arrs):
        return list(arrs) * (2 if self.rem else 1)

    def each(self, k, refs, fn):
        if not self.rem:
            fn(*refs)
            return
        n = len(refs) // 2
        pl.when(k < self.main)(lambda: fn(*refs[:n]))
        pl.when(k == self.main)(lambda: fn(*refs[n:]))


def _ffn_bwd_dw_gu(h, dg, du, tk, name, comm=()):
    nb, rows, fb = dg.shape
    d = h.shape[1]
    steps = _KSteps(rows, tk)

    def body(*refs):
        ins, (dwg_ref, dwu_ref, accg, accu) = refs[:-4], refs[-4:]
        k = pl.program_id(1)

        @pl.when(k == 0)
        def _():
            accg[...] = jnp.zeros_like(accg)
            accu[...] = jnp.zeros_like(accu)

        def add(h_ref, dg_ref, du_ref):
            hv = h_ref[...]
            accg[...] += lax.dot_general(dg_ref[0], hv, TN, preferred_element_type=F32)
            accu[...] += lax.dot_general(du_ref[0], hv, TN, preferred_element_type=F32)

        steps.each(k, ins, add)

        @pl.when(k == steps.n - 1)
        def _():
            dwg_ref[0] = accg[...].astype(BF16)
            dwu_ref[0] = accu[...].astype(BF16)

    wspec = pl.BlockSpec((1, fb, d), lambda j, k: (j, 0, 0))
    specs = steps.specs([lambda t, at: pl.BlockSpec((t, d), lambda j, k: (at(k), 0)),
                         lambda t, at: pl.BlockSpec((1, t, fb), lambda j, k: (j, at(k), 0)),
                         lambda t, at: pl.BlockSpec((1, t, fb), lambda j, k: (j, at(k), 0))])
    return _call(
        body, comm=comm, name=name, grid=(nb, steps.n),
        in_specs=specs, out_specs=[wspec] * 2,
        out_shape=[_sds((nb, fb, d), BF16)] * 2, scratch_shapes=[pltpu.VMEM((fb, d), F32)] * 2,
        compiler_params=_params(),
    )(*steps.args([h, dg, du]))


def _ffn_bwd_dw_down(g, u, do, tk, name, comm=()):
    nb, rows, fb = g.shape
    d = do.shape[1]
    steps = _KSteps(rows, tk)

    def body(*refs):
        ins, (dwd_ref, acc) = refs[:-2], refs[-2:]
        k = pl.program_id(1)

        @pl.when(k == 0)
        def _():
            acc[...] = jnp.zeros_like(acc)

        def add(g_ref, u_ref, do_ref):
            gv = g_ref[0]
            a = gv * _sigmoid(gv) * u_ref[0]
            acc[...] += lax.dot_general(a, do_ref[...], TN, preferred_element_type=F32)

        steps.each(k, ins, add)

        @pl.when(k == steps.n - 1)
        def _():
            dwd_ref[0] = (FFN_RES_SCALE * acc[...]).astype(BF16)

    specs = steps.specs([lambda t, at: pl.BlockSpec((1, t, fb), lambda j, k: (j, at(k), 0)),
                         lambda t, at: pl.BlockSpec((1, t, fb), lambda j, k: (j, at(k), 0)),
                         lambda t, at: pl.BlockSpec((t, d), lambda j, k: (at(k), 0))])
    return _call(
        body, comm=comm, name=name, grid=(nb, steps.n), in_specs=specs,
        out_specs=pl.BlockSpec((1, fb, d), lambda j, k: (j, 0, 0)), out_shape=_sds((nb, fb, d), BF16),
        scratch_shapes=[pltpu.VMEM((fb, d), F32)], compiler_params=_params(),
    )(*steps.args([g, u, do]))


def _win_fwd(h, w, b, dc, tm, name, comm=()):
    rows, d = h.shape
    ng = w.shape[1] // dc

    def body(h_ref, w_ref, b_ref, u_ref):
        u_ref[...] = (jnp.dot(h_ref[...], w_ref[...], preferred_element_type=F32) + b_ref[...]).astype(BF16)

    return _call(
        body, comm=comm, name=name, grid=(ng, rows // tm),
        in_specs=[pl.BlockSpec((tm, d), lambda m, i: (i, 0)), pl.BlockSpec((d, dc), lambda m, i: (0, m)),
                  pl.BlockSpec((1, dc), lambda m, i: (0, m))],
        out_specs=pl.BlockSpec((tm, dc), lambda m, i: (i, m)), out_shape=_sds((rows, ng * dc), BF16),
        compiler_params=_params(),
    )(h, w, b)


def _win_bwd_dh(du, w, hs, r, gn, dres, tm, name, comm=()):
    rows, d = hs.shape
    ng, _, dc = du.shape
    nt = rows // tm

    def body(du_ref, w_ref, hs_ref, r_ref, gn_ref, dres_ref, dhs_ref, dhsb_ref, dgn_ref, acc_ref):
        part = None
        for m in range(ng):
            p = lax.dot_general(du_ref[m], w_ref[:, m * dc:(m + 1) * dc], NT, preferred_element_type=F32)
            part = p if part is None else part + p
        acc_ref[...] = part
        _rms_bwd(tm, d, acc_ref, hs_ref, r_ref, gn_ref, dres_ref, dhs_ref, dhsb_ref, dgn_ref)

    row = pl.BlockSpec((tm, d), lambda i: (i, 0))
    return _call(
        body, comm=comm, name=name, grid=(nt,),
        in_specs=[pl.BlockSpec((ng, tm, dc), lambda i: (0, i, 0)), _once((d, ng * dc), lambda i: (0, 0)), row,
                  pl.BlockSpec((tm, 1), lambda i: (i, 0)), pl.BlockSpec((1, d), lambda i: (0, 0)), row],
        out_specs=[row, row, pl.BlockSpec((1, 1, d), lambda i: (i, 0, 0))],
        out_shape=[_sds((rows, d), F32), _sds((rows, d), BF16), _sds((nt, 1, d), F32)],
        scratch_shapes=[pltpu.VMEM((tm, d), F32)], compiler_params=_params(),
    )(du, w, hs, r, gn, dres)


def _win_bwd_dw(h, du, tk, name, comm=()):
    rows, d = h.shape
    ng, _, dc = du.shape
    steps = _KSteps(rows, tk)

    def body(*refs):
        ins, (dw_ref, db_ref, acc, accb) = refs[:-4], refs[-4:]
        k = pl.program_id(1)

        @pl.when(k == 0)
        def _():
            acc[...] = jnp.zeros_like(acc)
            accb[...] = jnp.zeros_like(accb)

        def add(h_ref, du_ref):
            duv = du_ref[0]
            acc[...] += lax.dot_general(h_ref[...], duv, TN, preferred_element_type=F32)
            accb[...] += jnp.sum(duv.astype(F32), axis=0, keepdims=True)

        steps.each(k, ins, add)

        @pl.when(k == steps.n - 1)
        def _():
            dw_ref[...] = acc[...].astype(BF16)
            db_ref[...] = accb[...]

    specs = steps.specs([lambda t, at: pl.BlockSpec((t, d), lambda m, k: (at(k), 0)),
                         lambda t, at: pl.BlockSpec((1, t, dc), lambda m, k: (m, at(k), 0))])
    return _call(
        body, comm=comm, name=name, grid=(ng, steps.n), in_specs=specs,
        out_specs=[pl.BlockSpec((d, dc), lambda m, k: (0, m)), pl.BlockSpec((1, dc), lambda m, k: (0, m))],
        out_shape=[_sds((d, ng * dc), BF16), _sds((1, ng * dc), F32)],
        scratch_shapes=[pltpu.VMEM((d, dc), F32), pltpu.VMEM((1, dc), F32)], compiler_params=_params(),
    )(*steps.args([h, du]))


def _layernorm_silu(zc, lg, lb):
    mu = jnp.mean(zc, axis=-1, keepdims=True)
    xc = zc - mu
    rstd = lax.rsqrt(jnp.mean(xc * xc, axis=-1, keepdims=True) + EPS)
    nrm = xc * rstd
    lin = nrm * lg + lb
    s = _sigmoid(lin)
    return nrm, rstd, lin, s


def _wout_fwd(zc, ysc, wout, hs, lg, lb, gn, rows, tm, name, comm=()):
    dc = zc.shape[1]
    d = hs.shape[1]

    def body(zc_ref, ysc_ref, w_ref, hs_ref, lg_ref, lb_ref, gn_ref, y_ref, hsn_ref, hn_ref, rn_ref):
        def mix(rs, _):
            _, _, lin, s = _layernorm_silu(zc_ref[rs, :], lg_ref[...], lb_ref[...])
            y_ref[rs, :dc] = ysc_ref[rs, :]
            y_ref[rs, dc:] = (lin * s).astype(BF16)

        _for_row_groups(tm, mix)
        hsn_ref[...] = jnp.dot(y_ref[...], w_ref[...], preferred_element_type=F32)

        def norm(rs, _):
            hsn = hs_ref[rs, :] + hsn_ref[rs, :]
            r = lax.rsqrt(jnp.mean(hsn * hsn, axis=-1, keepdims=True) + EPS)
            hsn_ref[rs, :] = hsn
            hn_ref[rs, :] = (hsn * r * gn_ref[...]).astype(BF16)
            rn_ref[rs, :] = r

        _for_row_groups(tm, norm)

    half = pl.BlockSpec((tm, dc), lambda i: (i, 0))
    row = pl.BlockSpec((tm, d), lambda i: (i, 0))
    vec_c = pl.BlockSpec((1, dc), lambda i: (0, 0))
    return _call(
        body, comm=comm, name=name, grid=(rows // tm,),
        in_specs=[half, half, _once((2 * dc, d), lambda i: (0, 0)), row, vec_c, vec_c,
                  pl.BlockSpec((1, d), lambda i: (0, 0))],
        out_specs=[pl.BlockSpec((tm, 2 * dc), lambda i: (i, 0)), row, row, pl.BlockSpec((tm, 1), lambda i: (i, 0))],
        out_shape=[_sds((rows, 2 * dc), BF16), _sds((rows, d), F32), _sds((rows, d), BF16), _sds((rows, 1), F32)],
        compiler_params=_params(),
    )(zc, ysc, wout, hs, lg, lb, gn)


def _wout_bwd_dy(do, wout, zc, lg, lb, tm, name, comm=()):
    rows, d = do.shape
    dc = zc.shape[1]
    nt = rows // tm

    def body(do_ref, w_ref, zc_ref, lg_ref, lb_ref, dysc_ref, dzc_ref, dlg_ref, dlb_ref, dy_ref):
        dy_ref[...] = lax.dot_general(do_ref[...], w_ref[...], NT, preferred_element_type=F32)
        dlb_ref[0] = jnp.zeros((1, dc), F32)
        dlg_ref[0] = jnp.zeros((1, dc), F32)

        def group(rs, _):
            dysc_ref[rs, :] = dy_ref[rs, :dc].astype(BF16)
            nrm, rstd, lin, s = _layernorm_silu(zc_ref[rs, :], lg_ref[...], lb_ref[...])
            dl = dy_ref[rs, dc:] * (s * (1.0 + lin * (1.0 - s)))
            dlb_ref[0] += jnp.sum(dl, axis=0, keepdims=True)
            dlg_ref[0] += jnp.sum(dl * nrm, axis=0, keepdims=True)
            dn = dl * lg_ref[...]
            dzc_ref[rs, :] = rstd * (dn - jnp.mean(dn, axis=-1, keepdims=True)
                                     - nrm * jnp.mean(dn * nrm, axis=-1, keepdims=True))

        _for_row_groups(tm, group)

    half = pl.BlockSpec((tm, dc), lambda i: (i, 0))
    vec_c = pl.BlockSpec((1, dc), lambda i: (0, 0))
    part = pl.BlockSpec((1, 1, dc), lambda i: (i, 0, 0))
    return _call(
        body, comm=comm, name=name, grid=(nt,),
        in_specs=[pl.BlockSpec((tm, d), lambda i: (i, 0)), _once((2 * dc, d), lambda i: (0, 0)), half, vec_c, vec_c],
        out_specs=[half, half, part, part],
        out_shape=[_sds((rows, dc), BF16), _sds((rows, dc), F32), _sds((nt, 1, dc), F32), _sds((nt, 1, dc), F32)],
        scratch_shapes=[pltpu.VMEM((tm, 2 * dc), F32)], compiler_params=_params(),
    )(do, wout, zc, lg, lb)


def _wout_bwd_dw(y, do, nb, tk, name, comm=()):
    rows, k2 = y.shape
    d = do.shape[1]
    nk = rows // tk

    def body(y_ref, do_ref, dw_ref, acc):
        k = pl.program_id(0)

        @pl.when(k == 0)
        def _():
            acc[...] = jnp.zeros_like(acc)

        acc[...] += lax.dot_general(y_ref[...], do_ref[...], TN, preferred_element_type=F32)

        @pl.when(k == nk - 1)
        def _():
            dw_ref[...] = acc[...].astype(BF16)

    return _call(
        body, comm=comm, name=name, grid=(nk,),
        in_specs=[pl.BlockSpec((tk, k2), lambda k: (k, 0)), pl.BlockSpec((tk, d), lambda k: (k, 0))],
        out_specs=_once((k2, d), lambda k: (0, 0)), out_shape=_sds((k2, d), BF16),
        scratch_shapes=[pltpu.VMEM((k2, d), F32)], compiler_params=_params(),
    )(y, do).reshape(nb, k2 // nb, d)


def _windows(win, n_res):
    length = win.shape[0]
    return [win if r == 0 else pltpu.roll(win, length - r, 0) for r in range(n_res)]


def _taps(src_ref, start, offsets, ch):
    span = -(-(max(offsets) + ch) // SUBLANES) * SUBLANES
    win = src_ref[pl.ds(start, span), :]
    shifted = _windows(win, min(SUBLANES, max(offsets) + 1))
    return [shifted[o % SUBLANES][(o // SUBLANES) * SUBLANES:(o // SUBLANES) * SUBLANES + ch] for o in offsets]


def _rows8(v):
    return jnp.sum(v.reshape(v.shape[0] // SUBLANES, SUBLANES, v.shape[1]), axis=0)


def _conv_geometry(n_seq, n_meta):
    base = CONV_PAD + n_meta
    off_cf = base - (CF_WIDTH - 1)
    off_sc = base - (SC_WIDTH - 1)
    logical = -(-(n_meta + n_seq) // CONV_CH) * CONV_CH
    return base, off_cf, off_sc, logical


def _fill_conv_inputs(c_ref, v_ref, a_ref, g_ref, scv, sz, n_seq, n_meta):
    base = CONV_PAD + n_meta
    cb = scv.shape[1]
    scv[0:CONV_PAD, :] = jnp.zeros((CONV_PAD, cb), F32)
    sz[0:CONV_PAD, :] = jnp.zeros((CONV_PAD, cb), F32)

    def put(src, dst, n):
        cv = c_ref[src, :].astype(F32) * v_ref[src, :].astype(F32)
        scv[dst, :] = cv
        sz[dst, :] = a_ref[src, :].astype(F32) * _sigmoid(g_ref[src, :].astype(F32))

    put(pl.ds(n_seq, n_meta), pl.ds(CONV_PAD, n_meta), n_meta)

    def chunk(i, carry):
        t0 = pl.multiple_of(i * CONV_CH, CONV_CH)
        put(pl.ds(t0, CONV_CH), pl.ds(base + t0, CONV_CH), CONV_CH)
        return carry

    lax.fori_loop(0, n_seq // CONV_CH, chunk, 0)


def _conv_fwd(u, wsc, wcf, cbias, n_seq, n_meta, name, comm=()):
    rows = u.shape[0]
    nb, _, cb = wsc.shape
    dc = nb * cb
    base, off_cf, off_sc, _ = _conv_geometry(n_seq, n_meta)
    a_cf, a_sc = off_cf // SUBLANES * SUBLANES, off_sc // SUBLANES * SUBLANES
    ch = CONV_CH

    def body(b_ref, c_ref, v_ref, a_ref, g_ref, wsc_ref, wcf_ref, cb_ref, ysc_ref, zc_ref, scv, sz):
        _fill_conv_inputs(c_ref, v_ref, a_ref, g_ref, scv, sz, n_seq, n_meta)
        w3, w31, bias = wsc_ref[0], wcf_ref[0], cb_ref[...]

        def chunk(i, carry):
            t0 = pl.multiple_of(i * ch, ch)
            acc = jnp.zeros((ch, cb), F32)
            for k, win in enumerate(_taps(sz, t0 + a_cf, [off_cf - a_cf + k for k in range(CF_WIDTH)], ch)):
                acc = acc + win * w31[k:k + 1, :]
            zc_ref[pl.ds(t0, ch), :] = acc + bias
            s = jnp.zeros((ch, cb), F32)
            for k, win in enumerate(_taps(scv, t0 + a_sc, [off_sc - a_sc + k for k in range(SC_WIDTH)], ch)):
                s = s + win * w3[k:k + 1, :]
            ysc_ref[pl.ds(t0, ch), :] = (b_ref[pl.ds(t0, ch), :].astype(F32) * s).astype(BF16)
            return carry

        lax.fori_loop(0, n_seq // ch, chunk, 0)
        ysc_ref[n_seq:rows, :] = jnp.zeros((rows - n_seq, cb), BF16)
        zc_ref[n_seq:rows, :] = jnp.zeros((rows - n_seq, cb), F32)

    ucol = [pl.BlockSpec((rows, cb), functools.partial(lambda m, j: (0, m * nb + j), m)) for m in range(5)]
    blk = pl.BlockSpec((rows, cb), lambda j: (0, j))
    return _call(
        body, comm=comm, name=name, grid=(nb,),
        in_specs=ucol + [pl.BlockSpec((1, SC_WIDTH, cb), lambda j: (j, 0, 0)),
                         pl.BlockSpec((1, CF_WIDTH, cb), lambda j: (j, 0, 0)), pl.BlockSpec((1, cb), lambda j: (0, j))],
        out_specs=[blk, blk], out_shape=[_sds((rows, dc), BF16), _sds((rows, dc), F32)],
        scratch_shapes=[pltpu.VMEM((base + n_seq, cb), F32)] * 2, compiler_params=_params(),
    )(u, u, u, u, u, wsc, wcf, cbias)


def _conv_bwd(u, dysc, dzc, wsc, wcf, n_seq, n_meta, name, comm=()):
    rows = u.shape[0]
    nb, _, cb = wsc.shape
    dc = nb * cb
    base, off_cf, off_sc, logical = _conv_geometry(n_seq, n_meta)
    a_cf, a_sc = off_cf // SUBLANES * SUBLANES, off_sc // SUBLANES * SUBLANES
    ch = CONV_CH
    tail = CONV_PAD

    def body(b_ref, c_ref, v_ref, a_ref, g_ref, dysc_ref, dzc_ref, wsc_ref, wcf_ref,
             du_ref, dwsc_ref, dwcf_ref, dcb_ref,
             scv, sz, sds_, sdz, dlcv, dlz, accsc, acccf, accb):
        _fill_conv_inputs(c_ref, v_ref, a_ref, g_ref, scv, sz, n_seq, n_meta)
        w3, w31 = wsc_ref[0], wcf_ref[0]
        dub_ref, duc_ref, duv_ref, dua_ref, dug_ref = (du_ref.at[m] for m in range(5))
        sds_[0:n_meta, :] = jnp.zeros((n_meta, cb), F32)
        sdz[0:n_meta, :] = jnp.zeros((n_meta, cb), F32)
        behind = logical + tail - (n_meta + n_seq)
        sds_[n_meta + n_seq:logical + tail, :] = jnp.zeros((behind, cb), F32)
        sdz[n_meta + n_seq:logical + tail, :] = jnp.zeros((behind, cb), F32)
        accsc[...] = jnp.zeros_like(accsc)
        acccf[...] = jnp.zeros_like(acccf)
        accb[...] = jnp.zeros_like(accb)

        def forward_chunk(i, carry):
            t0 = pl.multiple_of(i * ch, ch)
            rws = pl.ds(t0, ch)
            dy = dysc_ref[rws, :].astype(F32)
            ds = dy * b_ref[rws, :].astype(F32)
            dz = dzc_ref[rws, :]
            sds_[pl.ds(n_meta + t0, ch), :] = ds
            sdz[pl.ds(n_meta + t0, ch), :] = dz
            s = jnp.zeros((ch, cb), F32)
            for k, win in enumerate(_taps(scv, t0 + a_sc, [off_sc - a_sc + k for k in range(SC_WIDTH)], ch)):
                s = s + win * w3[k:k + 1, :]
                accsc[k * SUBLANES:(k + 1) * SUBLANES, :] += _rows8(ds * win)
            dub_ref[rws, :] = (dy * s).astype(BF16)
            for k, win in enumerate(_taps(sz, t0 + a_cf, [off_cf - a_cf + k for k in range(CF_WIDTH)], ch)):
                acccf[k * SUBLANES:(k + 1) * SUBLANES, :] += _rows8(dz * win)
            accb[...] += _rows8(dz)
            return carry

        lax.fori_loop(0, n_seq // ch, forward_chunk, 0)

        def backward_chunk(i, carry):
            p0 = pl.multiple_of(i * ch, ch)
            dcv = jnp.zeros((ch, cb), F32)
            for k, win in enumerate(_taps(sds_, p0, [SC_WIDTH - 1 - k for k in range(SC_WIDTH)], ch)):
                dcv = dcv + win * w3[k:k + 1, :]
            dlcv[pl.ds(p0, ch), :] = dcv
            dzi = jnp.zeros((ch, cb), F32)
            for k, win in enumerate(_taps(sdz, p0, [CF_WIDTH - 1 - k for k in range(CF_WIDTH)], ch)):
                dzi = dzi + win * w31[k:k + 1, :]
            dlz[pl.ds(p0, ch), :] = dzi
            return carry

        lax.fori_loop(0, logical // ch, backward_chunk, 0)

        def gates(phys, logi):
            dcv, dzi = dlcv[logi, :], dlz[logi, :]
            duc_ref[phys, :] = (dcv * v_ref[phys, :].astype(F32)).astype(BF16)
            duv_ref[phys, :] = (dcv * c_ref[phys, :].astype(F32)).astype(BF16)
            s = _sigmoid(g_ref[phys, :].astype(F32))
            dua_ref[phys, :] = (dzi * s).astype(BF16)
            dug_ref[phys, :] = (dzi * a_ref[phys, :].astype(F32) * s * (1.0 - s)).astype(BF16)

        def gate_chunk(i, carry):
            t0 = pl.multiple_of(i * ch, ch)
            gates(pl.ds(t0, ch), pl.ds(n_meta + t0, ch))
            return carry

        lax.fori_loop(0, n_seq // ch, gate_chunk, 0)
        gates(pl.ds(n_seq, n_meta), pl.ds(0, n_meta))
        dub_ref[n_seq:rows, :] = jnp.zeros((rows - n_seq, cb), BF16)
        pad0 = n_seq + n_meta
        for ref in (duc_ref, duv_ref, dua_ref, dug_ref):
            ref[pad0:rows, :] = jnp.zeros((rows - pad0, cb), BF16)
        dwsc_ref[0] = jnp.sum(accsc[...].reshape(SC_WIDTH, SUBLANES, cb), axis=1)
        dwcf_ref[0] = jnp.sum(acccf[...].reshape(CF_WIDTH, SUBLANES, cb), axis=1)
        dcb_ref[...] = jnp.sum(accb[...], axis=0, keepdims=True)

    ucol = [pl.BlockSpec((rows, cb), functools.partial(lambda m, j: (0, m * nb + j), m)) for m in range(5)]
    blk = pl.BlockSpec((dysc.shape[0], cb), lambda j: (0, j))
    wsc_spec = pl.BlockSpec((1, SC_WIDTH, cb), lambda j: (j, 0, 0))
    wcf_spec = pl.BlockSpec((1, CF_WIDTH, cb), lambda j: (j, 0, 0))
    outs = _call(
        body, comm=comm, name=name, grid=(nb,),
        in_specs=ucol + [blk, blk, wsc_spec, wcf_spec],
        out_specs=[pl.BlockSpec((5, rows, cb), lambda j: (0, 0, j)), wsc_spec, wcf_spec,
                   pl.BlockSpec((1, cb), lambda j: (0, j))],
        out_shape=[_sds((5, rows, dc), BF16), _sds((nb, SC_WIDTH, cb), F32), _sds((nb, CF_WIDTH, cb), F32),
                   _sds((1, dc), F32)],
        scratch_shapes=[pltpu.VMEM((base + n_seq, cb), F32)] * 2 + [pltpu.VMEM((logical + tail, cb), F32)] * 2
        + [pltpu.VMEM((logical, cb), F32)] * 2
        + [pltpu.VMEM((SC_WIDTH * SUBLANES, cb), F32), pltpu.VMEM((CF_WIDTH * SUBLANES, cb), F32),
           pltpu.VMEM((SUBLANES, cb), F32)],
        compiler_params=_params(),
    )(u, u, u, u, u, dysc, dzc, wsc, wcf)
    return outs


def _row_tile(rows, cols):
    return rows // 4 if rows % 64 == 0 and rows * cols >= (1 << 18) else rows


def _pair_sum(grad, sib, idx, name, comm=()):
    _, rows, cols = grad.shape
    tr = rows

    def body(idx_ref, g_ref, s_ref, o_ref):
        o_ref[0] = (g_ref[0].astype(F32) + s_ref[0].astype(F32)).astype(o_ref.dtype)

    return _call(
        body, name=name,
        grid_spec=pltpu.PrefetchScalarGridSpec(
            num_scalar_prefetch=1, grid=(4, rows // tr),
            in_specs=[pl.BlockSpec((1, tr, cols), lambda k, i, idx_ref: (idx_ref[k], i, 0)),
                      pl.BlockSpec((1, tr, cols), lambda k, i, idx_ref: (idx_ref[4 + k], i, 0))],
            out_specs=pl.BlockSpec((1, tr, cols), lambda k, i, idx_ref: (k, i, 0))),
        out_shape=_sds((4, rows, cols), grad.dtype), compiler_params=_params(),
    )(idx, grad, sib)


def _adamw_math(w, g, m, v):
    m = ADAM_B1 * m + (1.0 - ADAM_B1) * g
    v = ADAM_B2 * v + (1.0 - ADAM_B2) * (g * g)
    m_hat = m / (1.0 - ADAM_B1 ** ADAM_STEP)
    v_hat = v / (1.0 - ADAM_B2 ** ADAM_STEP)
    delta = -ADAM_LR * (m_hat / (jnp.sqrt(v_hat) + ADAM_EPS) + ADAM_WD * w)
    return delta, m, v


def _adamw_sharded(own, got, w, m, v, name, comm=()):
    rows, cols = w.shape
    tr = _row_tile(rows, cols)

    def body(own_ref, g0_ref, g1_ref, g2_ref, w_ref, m_ref, v_ref, g_ref, d_ref, nm_ref, nv_ref):
        g = own_ref[0].astype(F32) + g0_ref[0].astype(F32) + g1_ref[0].astype(F32) + g2_ref[0].astype(F32)
        delta, nm, nv = _adamw_math(w_ref[...], g, m_ref[...], v_ref[...])
        g_ref[...] = g
        d_ref[...] = delta
        nm_ref[...] = nm
        nv_ref[...] = nv

    flat = pl.BlockSpec((tr, cols), lambda i: (i, 0))
    slot = [pl.BlockSpec((1, tr, cols), functools.partial(lambda k, i: (k, i, 0), k)) for k in range(3)]
    return _call(
        body, comm=comm, name=name, grid=(rows // tr,),
        in_specs=[slot[0]] + slot + [flat] * 3, out_specs=[flat] * 4, out_shape=[_sds((rows, cols), F32)] * 4,
        compiler_params=_params(),
    )(own, got, got, got, w, m, v)


def _adamw_replicated(gathered, segs, ws, ms, vs, loss_scale, name, comm=()):
    n = len(ws)

    def body(*refs):
        gat = refs[0]
        w_refs, m_refs, v_refs = refs[1:1 + n], refs[1 + n:1 + 2 * n], refs[1 + 2 * n:1 + 3 * n]
        outs = refs[1 + 3 * n:]

        def total(off, width):
            s = gat[0, :, off:off + width]
            for k in range(1, N_DEV):
                s = s + gat[k, :, off:off + width]
            return s

        outs[0][...] = loss_scale * total(segs[n][0], segs[n][1])
        for p in range(n):
            g = total(*segs[p])
            delta, nm, nv = _adamw_math(w_refs[p][...], g, m_refs[p][...], v_refs[p][...])
            for q, val in enumerate((g, delta, nm, nv)):
                outs[1 + 4 * p + q][...] = val
        for e, seg in enumerate(segs[n + 1:]):
            outs[1 + 4 * n + e][...] = total(*seg)

    return _call(
        body, name=name,
        out_shape=[_sds((1, segs[n][1]), F32)] + [_sds(w.shape, F32) for w in ws for _ in range(4)]
        + [_sds((1, seg[1]), F32) for seg in segs[n + 1:]],
        compiler_params=_params(),
    )(gathered, *ws, *ms, *vs)


def _adamw_plain(g, w, m, v, name):
    def body(g_ref, w_ref, m_ref, v_ref, d_ref, nm_ref, nv_ref):
        d_ref[...], nm_ref[...], nv_ref[...] = _adamw_math(w_ref[...], g_ref[...], m_ref[...], v_ref[...])

    return list(_call(body, name=name, out_shape=[_sds(w.shape, F32)] * 3)(g, w, m, v))


REPLICATED = ("ffn1_norm", "mix_norm", "b_in", "conv_cf_b", "ln_cf_g", "ln_cf_b", "ffn2_norm", "final_norm")
SHARDED = ("meta_tokens", "ffn1_w_gate", "ffn1_w_up", "ffn1_w_down", "w_in", "conv_sc_w", "conv_cf_w", "w_out",
           "ffn2_w_gate", "ffn2_w_up", "ffn2_w_down")
WEIGHTS = ("meta_tokens", "ffn1_norm", "ffn1_w_gate", "ffn1_w_up", "ffn1_w_down", "mix_norm", "w_in", "b_in",
           "conv_sc_w", "conv_cf_w", "conv_cf_b", "ln_cf_g", "ln_cf_b", "w_out", "ffn2_norm", "ffn2_w_gate",
           "ffn2_w_up", "ffn2_w_down", "final_norm")


TRANSPOSED = ("ffn1_w_gate", "ffn1_w_up", "ffn2_w_gate", "ffn2_w_up")


def _blocks2d(k, a):
    a = a.reshape(a.shape[-2:]) if a.ndim >= 2 else a.reshape(1, -1)
    return a.T if k in TRANSPOSED else a


def _step(x, tgt, w, m, v):
    n_seq, d = x.shape[1], x.shape[2]
    n_meta = w["meta_tokens"].shape[0]
    rows = -(-(n_seq + n_meta) // ROW_ALIGN) * ROW_ALIGN
    tm = rows // N_ROW_TILES
    cb = w["conv_sc_w"].shape[-1]
    dc = cb * N_DEV
    w2 = {k: _blocks2d(k, a) for k, a in w.items()}
    m2 = {k: _blocks2d(k, a) for k, a in m.items()}
    v2 = {k: _blocks2d(k, a) for k, a in v.items()}

    def as_given(k, r):
        return (r.T if k in TRANSPOSED else r).reshape(w[k].shape)

    def cast(k):
        return w2[k].astype(BF16)

    full = dict(zip(("ffn1_w_gate", "meta_tokens", "conv_sc_w", "conv_cf_w"), _exchange_alone(
        _gather_all([cast("ffn1_w_gate"), w2["meta_tokens"], w2["conv_sc_w"], w2["conv_cf_w"]]), "gather_gate1")))
    meta = jnp.transpose(full["meta_tokens"], (1, 0, 2)).reshape(n_meta, d)
    hs0 = jnp.concatenate([x[0], meta, jnp.zeros((rows - n_seq - n_meta, d), F32)], axis=0)

    tx = n_seq // N_ROW_TILES
    h1, r1 = _rms_fwd(hs0, w2["ffn1_norm"], tm, "rms_in")
    up1 = _gather_all([cast("ffn1_w_up")]).before(15 / 16)
    g1 = _ffn_proj(h1, full["ffn1_w_gate"], tm, "ffn1_gate", comm=[up1])
    down1 = _gather_all([cast("ffn1_w_down")]).before(15 / 16)
    u1 = _ffn_proj(h1, up1.results[0], tm, "ffn1_up", comm=[down1])
    w_in_all, w_out1 = _gather_all([cast("w_in")]).before(3 / 4), _gather_direct([cast("w_out")])
    hs1, h2, r2 = _ffn_down_norm(g1, u1, down1.results[0], hs0, w2["mix_norm"], rows // DOWN_ROW_TILES, "ffn1_down",
                                 comm=[w_in_all, w_out1])
    win = jnp.transpose(w_in_all.results[0], (1, 0, 2)).reshape(d, -1)
    w_out2, gate2 = _gather_forward(w_out1.results), _gather_direct([cast("ffn2_w_gate")])
    u = _win_fwd(h2, win, w2["b_in"], dc, tm, "mix_in", comm=[w_out2, gate2])
    wout = w_out2.results[0].reshape(-1, d)
    gate2p = _gather_forward(gate2.results)
    ysc, zc = _conv_fwd(u, full["conv_sc_w"], full["conv_cf_w"], w2["conv_cf_b"], n_seq, n_meta, "conv_fwd", comm=[gate2p])
    up2 = _gather_all([cast("ffn2_w_up")])
    y, hs2, h3, r3 = _wout_fwd(zc, ysc, wout, hs1, w2["ln_cf_g"], w2["ln_cf_b"], w2["ffn2_norm"], n_seq, tx, "mix_out",
                               comm=[up2])
    down2 = _gather_all([cast("ffn2_w_down")]).before(5 / 8)
    full.update(ffn1_w_up=up1.results[0], ffn1_w_down=down1.results[0], ffn2_w_gate=gate2p.results[0],
                ffn2_w_up=up2.results[0])
    g2, u2 = _ffn_gu(h3, full["ffn2_w_gate"], full["ffn2_w_up"], tx, "ffn2_gu", comm=[down2])
    full["ffn2_w_down"] = down2.results[0]
    dhs3, dhs3b, loss_p, dgf_p = _ffn_down_loss(
        g2, u2, full["ffn2_w_down"], hs2, w2["final_norm"], tgt[0], n_seq, n_seq // DOWN_ROW_TILES, "ffn2_down_loss")

    xi, yi, ci = lax.axis_index("x"), lax.axis_index("y"), lax.axis_index("c")
    chip_of = [2 * xi + yi, 2 * (1 - xi) + yi, 2 * xi + (1 - yi), 2 * (1 - xi) + (1 - yi)]
    idx = jnp.stack([2 * ch + ci for ch in chip_of] + chip_of).astype(jnp.int32)
    out = {}

    def to_pairs(parts):
        return _pair_exchange([p.reshape((4, 2) + p.shape[1:]) for p in parts])

    def pair_sums(names, parts, pairs):
        return [_pair_sum(p, s, idx, "pair_sum_" + k) for k, p, s in zip(names, parts, pairs.results)]

    def update(names, sums, chips):
        for k, own, got in zip(names, sums, chips.results):
            res = _adamw_sharded(own, got, w2[k], m2[k], v2[k], "adamw_" + k)
            out[k] = [as_given(k, r) for r in res]

    tk = min(2 * ROW_ALIGN, n_seq)
    th_x, th_r = n_seq // DOWN_ROW_TILES, rows // DOWN_ROW_TILES

    dwd2 = _ffn_bwd_dw_down(g2, u2, dhs3b, tk, "ffn2_bwd_dw_down")
    p_d2 = to_pairs([dwd2])
    dg2, du2 = _ffn_bwd_da(dhs3b, full["ffn2_w_down"], g2, u2, tx, "ffn2_bwd_da", comm=[p_d2])
    s_d2 = pair_sums(["ffn2_w_down"], [dwd2], p_d2)
    c_d2 = _chip_exchange(s_d2)
    gu2 = _ffn_bwd_dw_gu(h3, dg2, du2, tk, "ffn2_bwd_dw_gu", comm=[c_d2])
    update(["ffn2_w_down"], s_d2, c_d2)
    p_gu2 = to_pairs(gu2)
    half2 = _ffn_bwd_dh_gate(dg2, full["ffn2_w_gate"], th_x, "ffn2_bwd_dh_gate", comm=[p_gu2])
    s_gu2 = pair_sums(["ffn2_w_gate", "ffn2_w_up"], gu2, p_gu2)
    c_g2, c_u2 = _chip_exchange(s_gu2[:1]), _chip_exchange(s_gu2[1:])
    dhs2, dhs2b, dn3_p = _ffn_bwd_dh(du2, full["ffn2_w_up"], half2, hs2, r3, w2["ffn2_norm"], dhs3, th_x, "ffn2_bwd_dh",
                                     zeros=jnp.zeros((rows, d), F32))
    dysc, dzc, dlg_p, dlb_p = _wout_bwd_dy(dhs2b, wout, zc, w2["ln_cf_g"], w2["ln_cf_b"], tx, "mix_out_bwd_dy")

    dwout = _wout_bwd_dw(y, dhs2b, N_DEV, tk, "mix_out_bwd_dw")
    p_wo = to_pairs([dwout])
    du, dcsw, dccw, dcb = _conv_bwd(u, dysc, dzc, full["conv_sc_w"], full["conv_cf_w"], n_seq, n_meta, "conv_bwd",
                                    comm=[c_g2, p_wo])
    update(["ffn2_w_gate"], s_gu2[:1], c_g2)
    s_wo = pair_sums(["w_out"], [dwout], p_wo)
    c_wo = _chip_exchange(s_wo)
    dhs1, dhs1b, dn2_p = _win_bwd_dh(du, win, hs1, r2, w2["mix_norm"], dhs2, th_r, "mix_in_bwd_dh", comm=[c_u2])
    update(["ffn2_w_up"], s_gu2[1:], c_u2)
    dwin, dbin = _win_bwd_dw(h2, du, tk, "mix_in_bwd_dw", comm=[c_wo])
    update(["w_out"], s_wo, c_wo)
    mixer = ("w_in", "conv_sc_w", "conv_cf_w")
    gradsm = [jnp.transpose(dwin.reshape(d, N_DEV, -1), (1, 0, 2)), dcsw, dccw]
    p_m = to_pairs(gradsm)
    dwd1 = _ffn_bwd_dw_down(g1, u1, dhs1b, tk, "ffn1_bwd_dw_down", comm=[p_m])
    s_m = pair_sums(mixer, gradsm, p_m)
    c_m, p_d1 = _chip_exchange(s_m), to_pairs([dwd1])
    dg1, du1 = _ffn_bwd_da(dhs1b, full["ffn1_w_down"], g1, u1, tm, "ffn1_bwd_da", comm=[c_m, p_d1])
    update(mixer, s_m, c_m)
    s_d1 = pair_sums(["ffn1_w_down"], [dwd1], p_d1)
    c_d1 = _chip_exchange(s_d1)
    gu1 = _ffn_bwd_dw_gu(h1, dg1, du1, tk, "ffn1_bwd_dw_gu", comm=[c_d1])
    update(["ffn1_w_down"], s_d1, c_d1)
    p_gu1 = to_pairs(gu1)
    _exchange_alone(p_gu1, "reduce_pair_ffn1")
    s_gu1 = pair_sums(["ffn1_w_gate", "ffn1_w_up"], gu1, p_gu1)
    c_g1, c_u1 = _chip_exchange(s_gu1[:1]), _chip_exchange(s_gu1[1:])
    half1 = _ffn_bwd_dh_gate(dg1, full["ffn1_w_gate"], th_r, "ffn1_bwd_dh_gate", comm=[c_g1])
    dhs0, _, dn1_p = _ffn_bwd_dh(du1, full["ffn1_w_up"], half1, hs0, r1, w2["ffn1_norm"], dhs1, th_r, "ffn1_bwd_dh",
                                 comm=[c_u1])
    update(["ffn1_w_gate"], s_gu1[:1], c_g1)
    update(["ffn1_w_up"], s_gu1[1:], c_u1)
    grad_x = dhs0[:n_seq][None]

    partial = {
        "ffn1_norm": dn1_p.sum(0), "mix_norm": dn2_p.sum(0), "b_in": dbin, "conv_cf_b": dcb,
        "ln_cf_g": dlg_p.sum(0), "ln_cf_b": dlb_p.sum(0), "ffn2_norm": dn3_p.sum(0), "final_norm": dgf_p.sum(0),
    }
    loss_seg = jnp.pad(loss_p.sum((0, 2)).reshape(1, 1), ((0, 0), (0, 127)))
    pieces = [partial[k] for k in REPLICATED] + [loss_seg, dhs0[n_seq:n_seq + n_meta].reshape(1, n_meta * d)]
    segs, off = [], 0
    for p in pieces:
        segs.append((off, p.shape[1]))
        off += p.shape[1]
    rows8 = _gather_rows(jnp.concatenate(pieces, axis=1), "gather_small")
    res = _adamw_replicated(rows8, segs, [w2[k] for k in REPLICATED], [m2[k] for k in REPLICATED],
                            [v2[k] for k in REPLICATED], 0.5 / d, "adamw_replicated")
    loss = res[0][0, 0]
    for p, k in enumerate(REPLICATED):
        out[k] = [r.reshape(w[k].shape) for r in res[1 + 4 * p:5 + 4 * p]]
    ob = w2["meta_tokens"].shape[1]
    gmeta = lax.dynamic_slice_in_dim(res[-1].reshape(n_meta, d), (4 * xi + 2 * yi + ci) * ob, ob, axis=1)
    out["meta_tokens"] = [gmeta] + _adamw_plain(gmeta, w2["meta_tokens"], m2["meta_tokens"], v2["meta_tokens"], "adamw_meta_tokens")

    return (loss, grad_x, *[out[k][0] for k in WEIGHTS], *[out[k][1] for k in WEIGHTS],
            *[out[k][2] for k in WEIGHTS], *[out[k][3] for k in WEIGHTS])


def kernel(x, meta_tokens, ffn1_norm, ffn1_w_gate, ffn1_w_up, ffn1_w_down, mix_norm, w_in, b_in, conv_sc_w, conv_cf_w, conv_cf_b, ln_cf_g, ln_cf_b, w_out, ffn2_norm, ffn2_w_gate, ffn2_w_up, ffn2_w_down, final_norm, loss_target, m_meta_tokens, m_ffn1_norm, m_ffn1_w_gate, m_ffn1_w_up, m_ffn1_w_down, m_mix_norm, m_w_in, m_b_in, m_conv_sc_w, m_conv_cf_w, m_conv_cf_b, m_ln_cf_g, m_ln_cf_b, m_w_out, m_ffn2_norm, m_ffn2_w_gate, m_ffn2_w_up, m_ffn2_w_down, m_final_norm, v_meta_tokens, v_ffn1_norm, v_ffn1_w_gate, v_ffn1_w_up, v_ffn1_w_down, v_mix_norm, v_w_in, v_b_in, v_conv_sc_w, v_conv_cf_w, v_conv_cf_b, v_ln_cf_g, v_ln_cf_b, v_w_out, v_ffn2_norm, v_ffn2_w_gate, v_ffn2_w_up, v_ffn2_w_down, v_final_norm):
    given = dict(locals())
    w = {k: given[k] for k in WEIGHTS}
    m = {k: given["m_" + k] for k in WEIGHTS}
    v = {k: given["v_" + k] for k in WEIGHTS}
    return _step(x, loss_target, w, m, v)
```

```python
import functools

import jax
import jax.numpy as jnp
from jax import lax
from jax.experimental import pallas as pl
from jax.experimental.pallas import tpu as pltpu

F32 = jnp.float32
BF16 = jnp.bfloat16
EPS = 1e-6
FFN_RES_SCALE = 0.5
SC_WIDTH = 3
CF_WIDTH = 31
ADAM_LR = 0.001
ADAM_B1 = 0.9
ADAM_B2 = 0.999
ADAM_EPS = 1e-08
ADAM_WD = 0.01
ADAM_STEP = 10

N_DEV = 8
N_ROW_TILES = 8
ROW_ALIGN = 256
CONV_PAD = 32
CONV_CH = 128
SUB_ROWS = 32
DOWN_ROW_TILES = 16
GATHER_PARTS = 2
SUBLANES = 8
BF16_ROWS = 16
VMEM_LIMIT = 56 * 1024 * 1024
MESH = pl.DeviceIdType.MESH
ANY = pl.BlockSpec(memory_space=pl.ANY)

NT = (((1,), (1,)), ((), ()))
TN = (((0,), (0,)), ((), ()))


def _pallas(body, **kw):
    return pl.pallas_call(body, **kw)


class _Exchange:
    def __init__(self, inputs, out_shapes, sem_shapes, start, finish, aliases=None, early=None):
        self.inputs, self.out_shapes, self.sem_shapes = list(inputs), list(out_shapes), list(sem_shapes)
        self.start, self.finish, self.aliases = start, finish, dict(aliases or {})
        self.early, self.early_at = early, None
        self.results = None

    def before(self, share):
        self.early_at = share
        return self


def _call(body, comm=(), **kw):
    if not comm:
        return _pallas(body, **kw)
    grid = kw.pop("grid")
    in_specs = list(kw.pop("in_specs"))
    out_specs, out_shape = kw.pop("out_specs"), kw.pop("out_shape")
    scratch = list(kw.pop("scratch_shapes", []))
    single = not isinstance(out_shape, (list, tuple))
    out_specs, out_shape = ([out_specs], [out_shape]) if single else (list(out_specs), list(out_shape))
    n_in, n_out, n_scr = len(in_specs), len(out_shape), len(scratch)
    c_in = [a for job in comm for a in job.inputs]
    c_out = [s for job in comm for s in job.out_shapes]
    c_sem = [s for job in comm for s in job.sem_shapes]
    aliases, i0, o0 = dict(kw.pop("input_output_aliases", {})), n_in, n_out
    for job in comm:
        aliases.update({i0 + i: o0 + o for i, o in job.aliases.items()})
        i0, o0 = i0 + len(job.inputs), o0 + len(job.out_shapes)

    def hosted(*refs):
        pos = [0]

        def take(n):
            pos[0] += n
            return refs[pos[0] - n:pos[0]]

        ins, cins, outs, couts, scr, sems = take(n_in), take(len(c_in)), take(n_out), take(len(c_out)), take(n_scr), take(len(c_sem))
        ids = [pl.program_id(k) for k in range(len(grid))]
        first = functools.reduce(jnp.logical_and, [i == 0 for i in ids])
        last = functools.reduce(jnp.logical_and, [i == g - 1 for i, g in zip(ids, grid)])

        n_steps = functools.reduce(lambda a, b: a * b, grid, 1)
        step = functools.reduce(lambda acc, ig: acc * ig[1] + ig[0], zip(ids, grid), 0)

        def early_step(job):
            if job.early is None or job.early_at is None or n_steps < 2:
                return None
            return min(max(int(job.early_at * n_steps), 0), n_steps - 2)

        def each(phase, chosen=None):
            i, o, s = 0, 0, 0
            for job in comm:
                ni, no, ns = len(job.inputs), len(job.out_shapes), len(job.sem_shapes)
                if chosen is None or chosen(job):
                    getattr(job, phase)(cins[i:i + ni], couts[o:o + no], sems[s:s + ns])
                i, o, s = i + ni, o + no, s + ns

        @pl.when(first)
        def _():
            each("start")

        body(*ins, *outs, *scr)

        for at in sorted({early_step(job) for job in comm} - {None}):
            @pl.when(step == at)
            def _():
                each("early", lambda job: early_step(job) == at)

        @pl.when(last)
        def _():
            each("early", lambda job: job.early is not None and early_step(job) is None)
            each("finish")

    call = _pallas(
        hosted, grid=grid, in_specs=in_specs + [ANY] * len(c_in), out_specs=out_specs + [ANY] * len(c_out),
        out_shape=out_shape + c_out, scratch_shapes=scratch + c_sem, input_output_aliases=aliases, **kw)

    def run(*args):
        res = call(*args, *c_in)
        o = n_out
        for job in comm:
            job.results = list(res[o:o + len(job.out_shapes)])
            o += len(job.out_shapes)
        return res[0] if single else list(res[:n_out])

    return run


def _exchange_alone(job, name, comm=()):
    n_in, n_out = len(job.inputs), len(job.out_shapes)

    def body(*refs):
        ins, outs, sems = refs[:n_in], refs[n_in:n_in + n_out], refs[n_in + n_out:]
        job.start(ins, outs, sems)
        if job.early is not None:
            job.early(ins, outs, sems)
        job.finish(ins, outs, sems)

    res = _pallas(body, name=name, in_specs=[ANY] * n_in, out_specs=[ANY] * n_out, out_shape=job.out_shapes,
                  scratch_shapes=job.sem_shapes, input_output_aliases=job.aliases)(*job.inputs)
    job.results = list(res)
    return job.results


def _params(**kw):
    return pltpu.CompilerParams(vmem_limit_bytes=VMEM_LIMIT, **kw)


def _sigmoid(x):
    return 0.5 * jnp.tanh(0.5 * x) + 0.5


def _sds(shape, dtype):
    return jax.ShapeDtypeStruct(shape, dtype)


def _place():
    x, y, c = lax.axis_index("x"), lax.axis_index("y"), lax.axis_index("c")
    chips = [(1 - x, y), (x, 1 - y), (1 - x, 1 - y)]
    return x, y, c, chips


def _slot(ref, p):
    return ref.at[4 * p[0] + 2 * p[1] + p[2]]


def _remote(src, dst, send_sem, recv_sem, to):
    return pltpu.make_async_remote_copy(src_ref=src, dst_ref=dst, send_sem=send_sem, recv_sem=recv_sem,
                                        device_id=to, device_id_type=MESH)


def _gather_direct(arrs):
    n = len(arrs)

    def copies(ins, outs, sems):
        send_sems, recv_sems, local_sems = sems
        x, y, c, chips = _place()
        me = (x, y, c)
        peers = [(x, y, 1 - c)] + [(*chip, c) for chip in chips]
        local = [pltpu.make_async_copy(ins[a], _slot(outs[a], me), local_sems.at[a]) for a in range(n)]
        sends = [_remote(ins[a], _slot(outs[a], me), send_sems.at[a, k], recv_sems.at[a, k], peer)
                 for a in range(n) for k, peer in enumerate(peers)]
        arrivals = [_remote(ins[a], _slot(outs[a], peer), send_sems.at[a, k], recv_sems.at[a, k], peer)
                    for a in range(n) for k, peer in enumerate(peers)]
        return local, sends, arrivals

    def start(ins, outs, sems):
        local, sends, _ = copies(ins, outs, sems)
        for cp in local + sends:
            cp.start()

    def finish(ins, outs, sems):
        local, sends, arrivals = copies(ins, outs, sems)
        for cp in arrivals:
            cp.wait_recv()
        for cp in sends:
            cp.wait_send()
        for cp in local:
            cp.wait()

    dma = pltpu.SemaphoreType.DMA
    return _Exchange(arrs, [_sds((N_DEV,) + a.shape, a.dtype) for a in arrs], [dma((n, 4)), dma((n, 4)), dma((n,))],
                     start, finish)


def _gather_forward(gathered):
    n = len(gathered)

    def copies(ins, outs, sems):
        send_sems, recv_sems = sems
        x, y, c, chips = _place()
        sibling = (x, y, 1 - c)
        sends = [_remote(_slot(ins[a], (*chip, c)), _slot(outs[a], (*chip, c)), send_sems.at[a, j], recv_sems.at[a, j], sibling)
                 for a in range(n) for j, chip in enumerate(chips)]
        arrivals = [_remote(_slot(ins[a], (*chip, c)), _slot(outs[a], (*chip, 1 - c)), send_sems.at[a, j], recv_sems.at[a, j], sibling)
                    for a in range(n) for j, chip in enumerate(chips)]
        return sends, arrivals

    def start(ins, outs, sems):
        for cp in copies(ins, outs, sems)[0]:
            cp.start()

    def finish(ins, outs, sems):
        sends, arrivals = copies(ins, outs, sems)
        for cp in arrivals:
            cp.wait_recv()
        for cp in sends:
            cp.wait_send()

    dma = pltpu.SemaphoreType.DMA
    return _Exchange(gathered, [_sds(a.shape, a.dtype) for a in gathered], [dma((n, 3)), dma((n, 3))], start, finish,
                     aliases={a: a for a in range(n)})


def _gather_all(arrs, parts=GATHER_PARTS):
    n = len(arrs)
    split = [parts if a.shape[0] % (parts * BF16_ROWS) == 0 else 1 for a in arrs]
    members = [[a for a in range(n) if p < split[a]] for p in range(parts)]

    def part_of(a, p):
        size = arrs[a].shape[0] // split[a]
        return pl.ds(p * size, size), _sds((size,) + arrs[a].shape[1:], arrs[a].dtype)

    direct = [_gather_direct([part_of(a, p)[1] for a in members[p]]) for p in range(parts)]
    passed = [_gather_forward([_sds((N_DEV,) + part_of(a, p)[1].shape, arrs[a].dtype) for a in members[p]])
              for p in range(parts)]
    jobs = direct + passed
    sem_shapes = [s for job in jobs for s in job.sem_shapes]

    def views(ins, outs, sems):
        found, at = [], 0
        for i, job in enumerate(jobs):
            p = i % parts
            rows = [part_of(a, p)[0] for a in members[p]]
            found.append(([ins[a].at[r] for a, r in zip(members[p], rows)],
                          [outs[a].at[:, r] for a, r in zip(members[p], rows)],
                          sems[at:at + len(job.sem_shapes)]))
            at += len(job.sem_shapes)
        return found

    def start(ins, outs, sems):
        for job, (i, o, s) in list(zip(jobs, views(ins, outs, sems)))[:parts]:
            job.start(i, o, s)

    def early(ins, outs, sems):
        found = views(ins, outs, sems)
        for p in range(parts):
            i, o, s = found[p]
            direct[p].finish(i, o, s)
            _, o2, s2 = found[parts + p]
            passed[p].start(o2, o2, s2)

    def finish(ins, outs, sems):
        found = views(ins, outs, sems)
        for p in range(parts):
            _, o2, s2 = found[parts + p]
            passed[p].finish(o2, o2, s2)

    return _Exchange(arrs, [_sds((N_DEV,) + a.shape, a.dtype) for a in arrs], sem_shapes, start, finish, early=early)


def _pair_exchange(arrs):
    n = len(arrs)

    def copies(ins, outs, sems):
        x, y, c, _ = _place()
        return [_remote(ins[a].at[:, 1 - c], outs[a], sems[0].at[a], sems[1].at[a], (x, y, 1 - c)) for a in range(n)]

    def start(ins, outs, sems):
        for cp in copies(ins, outs, sems):
            cp.start()

    def finish(ins, outs, sems):
        for cp in copies(ins, outs, sems):
            cp.wait()

    dma = pltpu.SemaphoreType.DMA
    return _Exchange(arrs, [_sds((4,) + a.shape[2:], a.dtype) for a in arrs], [dma((n,)), dma((n,))], start, finish)


def _chip_exchange(arrs):
    n = len(arrs)

    def copies(ins, outs, sems):
        x, y, c, chips = _place()
        return [_remote(ins[a].at[1 + j], outs[a].at[j], sems[0].at[a, j], sems[1].at[a, j], (*chip, c))
                for a in range(n) for j, chip in enumerate(chips)]

    def start(ins, outs, sems):
        for cp in copies(ins, outs, sems):
            cp.start()

    def finish(ins, outs, sems):
        for cp in copies(ins, outs, sems):
            cp.wait()

    dma = pltpu.SemaphoreType.DMA
    return _Exchange(arrs, [_sds((3,) + a.shape[1:], a.dtype) for a in arrs], [dma((n, 3)), dma((n, 3))], start, finish)


def _gather_rows(vec, name, comm=()):
    def body(in_ref, out_ref, send_sems, recv_sems, local_sem):
        x, y, c, _ = _place()
        me = 4 * x + 2 * y + c
        mine = pltpu.make_async_copy(in_ref, out_ref.at[me], local_sem)
        mine.start()
        copies = []
        for k in range(1, N_DEV):
            to = (1 - x if k & 4 else x, 1 - y if k & 2 else y, 1 - c if k & 1 else c)
            copies.append(pltpu.make_async_remote_copy(
                src_ref=in_ref, dst_ref=out_ref.at[me], send_sem=send_sems.at[k - 1], recv_sem=recv_sems.at[k - 1],
                device_id=to, device_id_type=MESH))
        for cp in copies:
            cp.start()
        for cp in copies:
            cp.wait()
        mine.wait()

    return _call(
        body, name=name, out_shape=_sds((N_DEV,) + vec.shape, vec.dtype), in_specs=[ANY], out_specs=ANY,
        scratch_shapes=[pltpu.SemaphoreType.DMA((7,)), pltpu.SemaphoreType.DMA((7,)), pltpu.SemaphoreType.DMA],
    )(vec)


def _group_rows(tm):
    return SUB_ROWS if tm % SUB_ROWS == 0 else BF16_ROWS


def _for_row_groups(tm, fn):
    sub = _group_rows(tm)

    def step(i, carry):
        fn(pl.ds(pl.multiple_of(i * sub, sub), sub), i * sub)
        return carry

    lax.fori_loop(0, tm // sub, step, 0)


def _once(shape, index_map):
    return pl.BlockSpec(shape, index_map, pipeline_mode=pl.Buffered(1))


def _rms_fwd(hs, g, tm, name, comm=()):
    rows, d = hs.shape

    def body(hs_ref, g_ref, h_ref, r_ref):
        def group(rs, _):
            xv = hs_ref[rs, :]
            r = lax.rsqrt(jnp.mean(xv * xv, axis=-1, keepdims=True) + EPS)
            h_ref[rs, :] = (xv * r * g_ref[...]).astype(BF16)
            r_ref[rs, :] = r

        _for_row_groups(tm, group)

    return _call(
        body, comm=comm, name=name, grid=(rows // tm,),
        in_specs=[pl.BlockSpec((tm, d), lambda i: (i, 0)), pl.BlockSpec((1, d), lambda i: (0, 0))],
        out_specs=[pl.BlockSpec((tm, d), lambda i: (i, 0)), pl.BlockSpec((tm, 1), lambda i: (i, 0))],
        out_shape=[_sds((rows, d), BF16), _sds((rows, 1), F32)], compiler_params=_params(),
    )(hs, g)


def _ffn_gu(h, wg, wu, tm, name, comm=()):
    rows, d = h.shape
    nb, fb, _ = wg.shape

    def body(h_ref, wg_ref, wu_ref, g_ref, u_ref):
        hv = h_ref[...]
        g_ref[0] = lax.dot_general(hv, wg_ref[0], NT, preferred_element_type=F32).astype(BF16)
        u_ref[0] = lax.dot_general(hv, wu_ref[0], NT, preferred_element_type=F32).astype(BF16)

    wspec = pl.BlockSpec((1, fb, d), lambda j, i: (j, 0, 0))
    ospec = pl.BlockSpec((1, tm, fb), lambda j, i: (j, i, 0))
    return _call(
        body, comm=comm, name=name, grid=(nb, rows // tm),
        in_specs=[pl.BlockSpec((tm, d), lambda j, i: (i, 0)), wspec, wspec],
        out_specs=[ospec, ospec], out_shape=[_sds((nb, rows, fb), BF16)] * 2, compiler_params=_params(),
    )(h, wg, wu)


def _ffn_proj(h, w, tm, name, comm=()):
    rows, d = h.shape
    nb, fb, _ = w.shape

    def body(h_ref, w_ref, p_ref):
        p_ref[0] = lax.dot_general(h_ref[...], w_ref[0], NT, preferred_element_type=F32).astype(BF16)

    return _call(
        body, comm=comm, name=name, grid=(nb, rows // tm),
        in_specs=[pl.BlockSpec((tm, d), lambda j, i: (i, 0)), pl.BlockSpec((1, fb, d), lambda j, i: (j, 0, 0))],
        out_specs=pl.BlockSpec((1, tm, fb), lambda j, i: (j, i, 0)), out_shape=_sds((nb, rows, fb), BF16),
        compiler_params=_params(),
    )(h, w)


def _down_product(g_ref, u_ref, wd_ref, acc_ref):
    part = None
    for b in range(g_ref.shape[0]):
        gv = g_ref[b]
        a = gv * _sigmoid(gv) * u_ref[b]
        p = jnp.dot(a, wd_ref[b], preferred_element_type=F32)
        part = p if part is None else part + p
    acc_ref[...] = part


def _ffn_down_norm(g, u, wd, hs, gn, tm, name, comm=()):
    nb, rows, fb = g.shape
    d = hs.shape[1]

    def body(g_ref, u_ref, wd_ref, hs_ref, gn_ref, hsn_ref, hn_ref, rn_ref, acc_ref):
        _down_product(g_ref, u_ref, wd_ref, acc_ref)

        def group(rs, _):
            hsn = hs_ref[rs, :] + FFN_RES_SCALE * acc_ref[rs, :]
            r = lax.rsqrt(jnp.mean(hsn * hsn, axis=-1, keepdims=True) + EPS)
            hsn_ref[rs, :] = hsn
            hn_ref[rs, :] = (hsn * r * gn_ref[...]).astype(BF16)
            rn_ref[rs, :] = r

        _for_row_groups(tm, group)

    aspec = pl.BlockSpec((nb, tm, fb), lambda i: (0, i, 0))
    row = pl.BlockSpec((tm, d), lambda i: (i, 0))
    return _call(
        body, comm=comm, name=name, grid=(rows // tm,),
        in_specs=[aspec, aspec, _once((nb, fb, d), lambda i: (0, 0, 0)), row, pl.BlockSpec((1, d), lambda i: (0, 0))],
        out_specs=[row, row, pl.BlockSpec((tm, 1), lambda i: (i, 0))],
        out_shape=[_sds((rows, d), F32), _sds((rows, d), BF16), _sds((rows, 1), F32)],
        scratch_shapes=[pltpu.VMEM((tm, d), F32)], compiler_params=_params(),
    )(g, u, wd, hs, gn)


def _ffn_down_loss(g, u, wd, hs, gf, tgt, n_seq, tm, name, comm=()):
    nb, rows, fb = g.shape
    d = hs.shape[1]
    nt = rows // tm

    def body(g_ref, u_ref, wd_ref, hs_ref, gf_ref, tgt_ref, dhs_ref, dhsb_ref, loss_ref, dgf_ref, acc_ref):
        i = pl.program_id(0)
        _down_product(g_ref, u_ref, wd_ref, acc_ref)
        loss_ref[0] = jnp.zeros((1, d), F32)
        dgf_ref[0] = jnp.zeros((1, d), F32)

        def group(rs, r0):
            hs3 = hs_ref[rs, :] + FFN_RES_SCALE * acc_ref[rs, :]
            r = lax.rsqrt(jnp.mean(hs3 * hs3, axis=-1, keepdims=True) + EPS)
            gfv = gf_ref[...]
            y = hs3 * r
            rowid = i * tm + r0 + lax.broadcasted_iota(jnp.int32, (_group_rows(tm), 1), 0)
            err = jnp.where(rowid < n_seq, y * gfv - tgt_ref[rs, :], 0.0)
            loss_ref[0] += jnp.sum(err * err, axis=0, keepdims=True)
            dout = err * (1.0 / d)
            dgf_ref[0] += jnp.sum(dout * y, axis=0, keepdims=True)
            t = dout * gfv
            dhs = r * t - hs3 * (r * r * r) * jnp.mean(t * hs3, axis=-1, keepdims=True)
            dhs_ref[rs, :] = dhs
            dhsb_ref[rs, :] = dhs.astype(BF16)

        _for_row_groups(tm, group)

    aspec = pl.BlockSpec((nb, tm, fb), lambda i: (0, i, 0))
    row = pl.BlockSpec((tm, d), lambda i: (i, 0))
    part = pl.BlockSpec((1, 1, d), lambda i: (i, 0, 0))
    return _call(
        body, comm=comm, name=name, grid=(nt,),
        in_specs=[aspec, aspec, _once((nb, fb, d), lambda i: (0, 0, 0)), row, pl.BlockSpec((1, d), lambda i: (0, 0)), row],
        out_specs=[row, row, part, part],
        out_shape=[_sds((rows, d), F32), _sds((rows, d), BF16), _sds((nt, 1, d), F32), _sds((nt, 1, d), F32)],
        scratch_shapes=[pltpu.VMEM((tm, d), F32)], compiler_params=_params(),
    )(g, u, wd, hs, gf, tgt)


def _ffn_bwd_da(do, wd, g, u, tm, name, comm=()):
    nb, rows, fb = g.shape
    d = do.shape[1]

    def body(do_ref, wd_ref, g_ref, u_ref, dg_ref, du_ref):
        rs = pl.ds(pl.multiple_of(pl.program_id(1) * tm, tm), tm)
        da = (FFN_RES_SCALE * lax.dot_general(do_ref[rs, :], wd_ref[0], NT, preferred_element_type=F32)).astype(BF16)
        gv, uv = g_ref[0], u_ref[0]
        s = _sigmoid(gv)
        sg = gv * s
        du_ref[0] = da * sg
        dg_ref[0] = da * uv * (s + sg * (1.0 - s))

    aspec = pl.BlockSpec((1, tm, fb), lambda j, i: (j, i, 0))
    return _call(
        body, comm=comm, name=name, grid=(nb, rows // tm),
        in_specs=[_once((rows, d), lambda j, i: (0, 0)), pl.BlockSpec((1, fb, d), lambda j, i: (j, 0, 0)), aspec, aspec],
        out_specs=[aspec] * 2, out_shape=[_sds((nb, rows, fb), BF16)] * 2, compiler_params=_params(),
    )(do, wd, g, u)


def _rms_bwd(tm, d, dh_ref, hs_ref, r_ref, gn_ref, dres_ref, dhs_ref, dhsb_ref, dgn_ref):
    dgn_ref[0] = jnp.zeros((1, d), F32)

    def group(rs, _):
        dh, hs, r = dh_ref[rs, :], hs_ref[rs, :], r_ref[rs, :]
        dgn_ref[0] += jnp.sum(dh * (hs * r), axis=0, keepdims=True)
        t = dh * gn_ref[...]
        dhs = dres_ref[rs, :] + r * t - hs * (r * r * r) * jnp.mean(t * hs, axis=-1, keepdims=True)
        dhs_ref[rs, :] = dhs
        dhsb_ref[rs, :] = dhs.astype(BF16)

    _for_row_groups(tm, group)


def _blocks_dot(x_ref, w_ref):
    part = None
    for b in range(x_ref.shape[0]):
        p = jnp.dot(x_ref[b], w_ref[b], preferred_element_type=F32)
        part = p if part is None else part + p
    return part


def _ffn_bwd_dh_gate(dg, wg, tm, name, comm=()):
    nb, rows, fb = dg.shape
    d = wg.shape[2]

    def body(dg_ref, wg_ref, o_ref):
        o_ref[...] = _blocks_dot(dg_ref, wg_ref)

    return _call(
        body, comm=comm, name=name, grid=(rows // tm,),
        in_specs=[pl.BlockSpec((nb, tm, fb), lambda i: (0, i, 0)), _once((nb, fb, d), lambda i: (0, 0, 0))],
        out_specs=pl.BlockSpec((tm, d), lambda i: (i, 0)), out_shape=_sds((rows, d), F32), compiler_params=_params(),
    )(dg, wg)


def _ffn_bwd_dh(du, wu, half, hs, r, gn, dres, tm, name, comm=(), zeros=None):
    nb, rows, fb = du.shape
    d = hs.shape[1]
    nt = rows // tm
    extra = [] if zeros is None else [zeros]

    def body(du_ref, wu_ref, half_ref, hs_ref, r_ref, gn_ref, dres_ref, *rest):
        dhs_ref, dhsb_ref, dgn_ref, acc_ref = rest[len(extra):]
        acc_ref[...] = half_ref[...] + _blocks_dot(du_ref, wu_ref)
        _rms_bwd(tm, d, acc_ref, hs_ref, r_ref, gn_ref, dres_ref, dhs_ref, dhsb_ref, dgn_ref)

    row = pl.BlockSpec((tm, d), lambda i: (i, 0))
    return _call(
        body, comm=comm, name=name, grid=(nt,),
        in_specs=[pl.BlockSpec((nb, tm, fb), lambda i: (0, i, 0)), _once((nb, fb, d), lambda i: (0, 0, 0)), row, row,
                  pl.BlockSpec((tm, 1), lambda i: (i, 0)), pl.BlockSpec((1, d), lambda i: (0, 0)), row] + [ANY] * len(extra),
        out_specs=[row, row, pl.BlockSpec((1, 1, d), lambda i: (i, 0, 0))],
        out_shape=[_sds((rows, d) if zeros is None else zeros.shape, F32), _sds((rows, d), BF16), _sds((nt, 1, d), F32)],
        scratch_shapes=[pltpu.VMEM((tm, d), F32)], compiler_params=_params(),
        input_output_aliases={7: 0} if extra else {},
    )(du, wu, half, hs, r, gn, dres, *extra)


class _KSteps:
    def __init__(self, rows, tk):
        self.main, self.rem, self.tk = rows // tk, rows % tk, tk
        self.n = self.main + (1 if self.rem else 0)
        if self.rem:
            assert rows % self.rem == 0, (rows, tk)
            self.last = rows // self.rem - 1

    def specs(self, makers):
        main_at = (lambda k: jnp.minimum(k, self.main - 1)) if self.rem else (lambda k: k)
        out = [make(self.tk, main_at) for make in makers]
        if self.rem:
            out += [make(self.rem, lambda k: self.last) for make in makers]
        return out

    def args(self, arrs):
        return list(arrs) * (2 if self.rem else 1)

    def each(self, k, refs, fn):
        if not self.rem:
            fn(*refs)
            return
        n = len(refs) // 2
        pl.when(k < self.main)(lambda: fn(*refs[:n]))
        pl.when(k == self.main)(lambda: fn(*refs[n:]))


def _ffn_bwd_dw_gu(h, dg, du, tk, name, comm=()):
    nb, rows, fb = dg.shape
    d = h.shape[1]
    steps = _KSteps(rows, tk)

    def body(*refs):
        ins, (dwg_ref, dwu_ref, accg, accu) = refs[:-4], refs[-4:]
        k = pl.program_id(1)

        @pl.when(k == 0)
        def _():
            accg[...] = jnp.zeros_like(accg)
            accu[...] = jnp.zeros_like(accu)

        def add(h_ref, dg_ref, du_ref):
            hv = h_ref[...]
            accg[...] += lax.dot_general(dg_ref[0], hv, TN, preferred_element_type=F32)
            accu[...] += lax.dot_general(du_ref[0], hv, TN, preferred_element_type=F32)

        steps.each(k, ins, add)

        @pl.when(k == steps.n - 1)
        def _():
            dwg_ref[0] = accg[...].astype(BF16)
            dwu_ref[0] = accu[...].astype(BF16)

    wspec = pl.BlockSpec((1, fb, d), lambda j, k: (j, 0, 0))
    specs = steps.specs([lambda t, at: pl.BlockSpec((t, d), lambda j, k: (at(k), 0)),
                         lambda t, at: pl.BlockSpec((1, t, fb), lambda j, k: (j, at(k), 0)),
                         lambda t, at: pl.BlockSpec((1, t, fb), lambda j, k: (j, at(k), 0))])
    return _call(
        body, comm=comm, name=name, grid=(nb, steps.n),
        in_specs=specs, out_specs=[wspec] * 2,
        out_shape=[_sds((nb, fb, d), BF16)] * 2, scratch_shapes=[pltpu.VMEM((fb, d), F32)] * 2,
        compiler_params=_params(),
    )(*steps.args([h, dg, du]))


def _ffn_bwd_dw_down(g, u, do, tk, name, comm=()):
    nb, rows, fb = g.shape
    d = do.shape[1]
    steps = _KSteps(rows, tk)

    def body(*refs):
        ins, (dwd_ref, acc) = refs[:-2], refs[-2:]
        k = pl.program_id(1)

        @pl.when(k == 0)
        def _():
            acc[...] = jnp.zeros_like(acc)

        def add(g_ref, u_ref, do_ref):
            gv = g_ref[0]
            a = gv * _sigmoid(gv) * u_ref[0]
            acc[...] += lax.dot_general(a, do_ref[...], TN, preferred_element_type=F32)

        steps.each(k, ins, add)

        @pl.when(k == steps.n - 1)
        def _():
            dwd_ref[0] = (FFN_RES_SCALE * acc[...]).astype(BF16)

    specs = steps.specs([lambda t, at: pl.BlockSpec((1, t, fb), lambda j, k: (j, at(k), 0)),
                         lambda t, at: pl.BlockSpec((1, t, fb), lambda j, k: (j, at(k), 0)),
                         lambda t, at: pl.BlockSpec((t, d), lambda j, k: (at(k), 0))])
    return _call(
        body, comm=comm, name=name, grid=(nb, steps.n), in_specs=specs,
        out_specs=pl.BlockSpec((1, fb, d), lambda j, k: (j, 0, 0)), out_shape=_sds((nb, fb, d), BF16),
        scratch_shapes=[pltpu.VMEM((fb, d), F32)], compiler_params=_params(),
    )(*steps.args([g, u, do]))


def _win_fwd(h, w, b, dc, tm, name, comm=()):
    rows, d = h.shape
    ng = w.shape[1] // dc

    def body(h_ref, w_ref, b_ref, u_ref):
        u_ref[...] = (jnp.dot(h_ref[...], w_ref[...], preferred_element_type=F32) + b_ref[...]).astype(BF16)

    return _call(
        body, comm=comm, name=name, grid=(ng, rows // tm),
        in_specs=[pl.BlockSpec((tm, d), lambda m, i: (i, 0)), pl.BlockSpec((d, dc), lambda m, i: (0, m)),
                  pl.BlockSpec((1, dc), lambda m, i: (0, m))],
        out_specs=pl.BlockSpec((tm, dc), lambda m, i: (i, m)), out_shape=_sds((rows, ng * dc), BF16),
        compiler_params=_params(),
    )(h, w, b)


def _win_bwd_dh(du, w, hs, r, gn, dres, tm, name, comm=()):
    rows, d = hs.shape
    ng, _, dc = du.shape
    nt = rows // tm

    def body(du_ref, w_ref, hs_ref, r_ref, gn_ref, dres_ref, dhs_ref, dhsb_ref, dgn_ref, acc_ref):
        part = None
        for m in range(ng):
            p = lax.dot_general(du_ref[m], w_ref[:, m * dc:(m + 1) * dc], NT, preferred_element_type=F32)
            part = p if part is None else part + p
        acc_ref[...] = part
        _rms_bwd(tm, d, acc_ref, hs_ref, r_ref, gn_ref, dres_ref, dhs_ref, dhsb_ref, dgn_ref)

    row = pl.BlockSpec((tm, d), lambda i: (i, 0))
    return _call(
        body, comm=comm, name=name, grid=(nt,),
        in_specs=[pl.BlockSpec((ng, tm, dc), lambda i: (0, i, 0)), _once((d, ng * dc), lambda i: (0, 0)), row,
                  pl.BlockSpec((tm, 1), lambda i: (i, 0)), pl.BlockSpec((1, d), lambda i: (0, 0)), row],
        out_specs=[row, row, pl.BlockSpec((1, 1, d), lambda i: (i, 0, 0))],
        out_shape=[_sds((rows, d), F32), _sds((rows, d), BF16), _sds((nt, 1, d), F32)],
        scratch_shapes=[pltpu.VMEM((tm, d), F32)], compiler_params=_params(),
    )(du, w, hs, r, gn, dres)


def _win_bwd_dw(h, du, tk, name, comm=()):
    rows, d = h.shape
    ng, _, dc = du.shape
    steps = _KSteps(rows, tk)

    def body(*refs):
        ins, (dw_ref, db_ref, acc, accb) = refs[:-4], refs[-4:]
        k = pl.program_id(1)

        @pl.when(k == 0)
        def _():
            acc[...] = jnp.zeros_like(acc)
            accb[...] = jnp.zeros_like(accb)

        def add(h_ref, du_ref):
            duv = du_ref[0]
            acc[...] += lax.dot_general(h_ref[...], duv, TN, preferred_element_type=F32)
            accb[...] += jnp.sum(duv.astype(F32), axis=0, keepdims=True)

        steps.each(k, ins, add)

        @pl.when(k == steps.n - 1)
        def _():
            dw_ref[...] = acc[...].astype(BF16)
            db_ref[...] = accb[...]

    specs = steps.specs([lambda t, at: pl.BlockSpec((t, d), lambda m, k: (at(k), 0)),
                         lambda t, at: pl.BlockSpec((1, t, dc), lambda m, k: (m, at(k), 0))])
    return _call(
        body, comm=comm, name=name, grid=(ng, steps.n), in_specs=specs,
        out_specs=[pl.BlockSpec((d, dc), lambda m, k: (0, m)), pl.BlockSpec((1, dc), lambda m, k: (0, m))],
        out_shape=[_sds((d, ng * dc), BF16), _sds((1, ng * dc), F32)],
        scratch_shapes=[pltpu.VMEM((d, dc), F32), pltpu.VMEM((1, dc), F32)], compiler_params=_params(),
    )(*steps.args([h, du]))


def _layernorm_silu(zc, lg, lb):
    mu = jnp.mean(zc, axis=-1, keepdims=True)
    xc = zc - mu
    rstd = lax.rsqrt(jnp.mean(xc * xc, axis=-1, keepdims=True) + EPS)
    nrm = xc * rstd
    lin = nrm * lg + lb
    s = _sigmoid(lin)
    return nrm, rstd, lin, s


def _wout_fwd(zc, ysc, wout, hs, lg, lb, gn, rows, tm, name, comm=()):
    dc = zc.shape[1]
    d = hs.shape[1]

    def body(zc_ref, ysc_ref, w_ref, hs_ref, lg_ref, lb_ref, gn_ref, y_ref, hsn_ref, hn_ref, rn_ref):
        def mix(rs, _):
            _, _, lin, s = _layernorm_silu(zc_ref[rs, :], lg_ref[...], lb_ref[...])
            y_ref[rs, :dc] = ysc_ref[rs, :]
            y_ref[rs, dc:] = (lin * s).astype(BF16)

        _for_row_groups(tm, mix)
        hsn_ref[...] = jnp.dot(y_ref[...], w_ref[...], preferred_element_type=F32)

        def norm(rs, _):
            hsn = hs_ref[rs, :] + hsn_ref[rs, :]
            r = lax.rsqrt(jnp.mean(hsn * hsn, axis=-1, keepdims=True) + EPS)
            hsn_ref[rs, :] = hsn
            hn_ref[rs, :] = (hsn * r * gn_ref[...]).astype(BF16)
            rn_ref[rs, :] = r

        _for_row_groups(tm, norm)

    half = pl.BlockSpec((tm, dc), lambda i: (i, 0))
    row = pl.BlockSpec((tm, d), lambda i: (i, 0))
    vec_c = pl.BlockSpec((1, dc), lambda i: (0, 0))
    return _call(
        body, comm=comm, name=name, grid=(rows // tm,),
        in_specs=[half, half, _once((2 * dc, d), lambda i: (0, 0)), row, vec_c, vec_c,
                  pl.BlockSpec((1, d), lambda i: (0, 0))],
        out_specs=[pl.BlockSpec((tm, 2 * dc), lambda i: (i, 0)), row, row, pl.BlockSpec((tm, 1), lambda i: (i, 0))],
        out_shape=[_sds((rows, 2 * dc), BF16), _sds((rows, d), F32), _sds((rows, d), BF16), _sds((rows, 1), F32)],
        compiler_params=_params(),
    )(zc, ysc, wout, hs, lg, lb, gn)


def _wout_bwd_dy(do, wout, zc, lg, lb, tm, name, comm=()):
    rows, d = do.shape
    dc = zc.shape[1]
    nt = rows // tm

    def body(do_ref, w_ref, zc_ref, lg_ref, lb_ref, dysc_ref, dzc_ref, dlg_ref, dlb_ref, dy_ref):
        dy_ref[...] = lax.dot_general(do_ref[...], w_ref[...], NT, preferred_element_type=F32)
        dlb_ref[0] = jnp.zeros((1, dc), F32)
        dlg_ref[0] = jnp.zeros((1, dc), F32)

        def group(rs, _):
            dysc_ref[rs, :] = dy_ref[rs, :dc].astype(BF16)
            nrm, rstd, lin, s = _layernorm_silu(zc_ref[rs, :], lg_ref[...], lb_ref[...])
            dl = dy_ref[rs, dc:] * (s * (1.0 + lin * (1.0 - s)))
            dlb_ref[0] += jnp.sum(dl, axis=0, keepdims=True)
            dlg_ref[0] += jnp.sum(dl * nrm, axis=0, keepdims=True)
            dn = dl * lg_ref[...]
            dzc_ref[rs, :] = rstd * (dn - jnp.mean(dn, axis=-1, keepdims=True)
                                     - nrm * jnp.mean(dn * nrm, axis=-1, keepdims=True))

        _for_row_groups(tm, group)

    half = pl.BlockSpec((tm, dc), lambda i: (i, 0))
    vec_c = pl.BlockSpec((1, dc), lambda i: (0, 0))
    part = pl.BlockSpec((1, 1, dc), lambda i: (i, 0, 0))
    return _call(
        body, comm=comm, name=name, grid=(nt,),
        in_specs=[pl.BlockSpec((tm, d), lambda i: (i, 0)), _once((2 * dc, d), lambda i: (0, 0)), half, vec_c, vec_c],
        out_specs=[half, half, part, part],
        out_shape=[_sds((rows, dc), BF16), _sds((rows, dc), F32), _sds((nt, 1, dc), F32), _sds((nt, 1, dc), F32)],
        scratch_shapes=[pltpu.VMEM((tm, 2 * dc), F32)], compiler_params=_params(),
    )(do, wout, zc, lg, lb)


def _wout_bwd_dw(y, do, nb, tk, name, comm=()):
    rows, k2 = y.shape
    d = do.shape[1]
    nk = rows // tk

    def body(y_ref, do_ref, dw_ref, acc):
        k = pl.program_id(0)

        @pl.when(k == 0)
        def _():
            acc[...] = jnp.zeros_like(acc)

        acc[...] += lax.dot_general(y_ref[...], do_ref[...], TN, preferred_element_type=F32)

        @pl.when(k == nk - 1)
        def _():
            dw_ref[...] = acc[...].astype(BF16)

    return _call(
        body, comm=comm, name=name, grid=(nk,),
        in_specs=[pl.BlockSpec((tk, k2), lambda k: (k, 0)), pl.BlockSpec((tk, d), lambda k: (k, 0))],
        out_specs=_once((k2, d), lambda k: (0, 0)), out_shape=_sds((k2, d), BF16),
        scratch_shapes=[pltpu.VMEM((k2, d), F32)], compiler_params=_params(),
    )(y, do).reshape(nb, k2 // nb, d)


def _windows(win, n_res):
    length = win.shape[0]
    return [win if r == 0 else pltpu.roll(win, length - r, 0) for r in range(n_res)]


def _taps(src_ref, start, offsets, ch):
    span = -(-(max(offsets) + ch) // SUBLANES) * SUBLANES
    win = src_ref[pl.ds(start, span), :]
    shifted = _windows(win, min(SUBLANES, max(offsets) + 1))
    return [shifted[o % SUBLANES][(o // SUBLANES) * SUBLANES:(o // SUBLANES) * SUBLANES + ch] for o in offsets]


def _rows8(v):
    return jnp.sum(v.reshape(v.shape[0] // SUBLANES, SUBLANES, v.shape[1]), axis=0)


def _conv_geometry(n_seq, n_meta):
    base = CONV_PAD + n_meta
    off_cf = base - (CF_WIDTH - 1)
    off_sc = base - (SC_WIDTH - 1)
    logical = -(-(n_meta + n_seq) // CONV_CH) * CONV_CH
    return base, off_cf, off_sc, logical


def _fill_conv_inputs(c_ref, v_ref, a_ref, g_ref, scv, sz, n_seq, n_meta):
    base = CONV_PAD + n_meta
    cb = scv.shape[1]
    scv[0:CONV_PAD, :] = jnp.zeros((CONV_PAD, cb), F32)
    sz[0:CONV_PAD, :] = jnp.zeros((CONV_PAD, cb), F32)

    def put(src, dst, n):
        cv = c_ref[src, :].astype(F32) * v_ref[src, :].astype(F32)
        scv[dst, :] = cv
        sz[dst, :] = a_ref[src, :].astype(F32) * _sigmoid(g_ref[src, :].astype(F32))

    put(pl.ds(n_seq, n_meta), pl.ds(CONV_PAD, n_meta), n_meta)

    def chunk(i, carry):
        t0 = pl.multiple_of(i * CONV_CH, CONV_CH)
        put(pl.ds(t0, CONV_CH), pl.ds(base + t0, CONV_CH), CONV_CH)
        return carry

    lax.fori_loop(0, n_seq // CONV_CH, chunk, 0)


def _conv_fwd(u, wsc, wcf, cbias, n_seq, n_meta, name, comm=()):
    rows = u.shape[0]
    nb, _, cb = wsc.shape
    dc = nb * cb
    base, off_cf, off_sc, _ = _conv_geometry(n_seq, n_meta)
    a_cf, a_sc = off_cf // SUBLANES * SUBLANES, off_sc // SUBLANES * SUBLANES
    ch = CONV_CH

    def body(b_ref, c_ref, v_ref, a_ref, g_ref, wsc_ref, wcf_ref, cb_ref, ysc_ref, zc_ref, scv, sz):
        _fill_conv_inputs(c_ref, v_ref, a_ref, g_ref, scv, sz, n_seq, n_meta)
        w3, w31, bias = wsc_ref[0], wcf_ref[0], cb_ref[...]

        def chunk(i, carry):
            t0 = pl.multiple_of(i * ch, ch)
            acc = jnp.zeros((ch, cb), F32)
            for k, win in enumerate(_taps(sz, t0 + a_cf, [off_cf - a_cf + k for k in range(CF_WIDTH)], ch)):
                acc = acc + win * w31[k:k + 1, :]
            zc_ref[pl.ds(t0, ch), :] = acc + bias
            s = jnp.zeros((ch, cb), F32)
            for k, win in enumerate(_taps(scv, t0 + a_sc, [off_sc - a_sc + k for k in range(SC_WIDTH)], ch)):
                s = s + win * w3[k:k + 1, :]
            ysc_ref[pl.ds(t0, ch), :] = (b_ref[pl.ds(t0, ch), :].astype(F32) * s).astype(BF16)
            return carry

        lax.fori_loop(0, n_seq // ch, chunk, 0)
        ysc_ref[n_seq:rows, :] = jnp.zeros((rows - n_seq, cb), BF16)
        zc_ref[n_seq:rows, :] = jnp.zeros((rows - n_seq, cb), F32)

    ucol = [pl.BlockSpec((rows, cb), functools.partial(lambda m, j: (0, m * nb + j), m)) for m in range(5)]
    blk = pl.BlockSpec((rows, cb), lambda j: (0, j))
    return _call(
        body, comm=comm, name=name, grid=(nb,),
        in_specs=ucol + [pl.BlockSpec((1, SC_WIDTH, cb), lambda j: (j, 0, 0)),
                         pl.BlockSpec((1, CF_WIDTH, cb), lambda j: (j, 0, 0)), pl.BlockSpec((1, cb), lambda j: (0, j))],
        out_specs=[blk, blk], out_shape=[_sds((rows, dc), BF16), _sds((rows, dc), F32)],
        scratch_shapes=[pltpu.VMEM((base + n_seq, cb), F32)] * 2, compiler_params=_params(),
    )(u, u, u, u, u, wsc, wcf, cbias)


def _conv_bwd(u, dysc, dzc, wsc, wcf, n_seq, n_meta, name, comm=()):
    rows = u.shape[0]
    nb, _, cb = wsc.shape
    dc = nb * cb
    base, off_cf, off_sc, logical = _conv_geometry(n_seq, n_meta)
    a_cf, a_sc = off_cf // SUBLANES * SUBLANES, off_sc // SUBLANES * SUBLANES
    ch = CONV_CH
    tail = CONV_PAD

    def body(b_ref, c_ref, v_ref, a_ref, g_ref, dysc_ref, dzc_ref, wsc_ref, wcf_ref,
             du_ref, dwsc_ref, dwcf_ref, dcb_ref,
             scv, sz, sds_, sdz, dlcv, dlz, accsc, acccf, accb):
        _fill_conv_inputs(c_ref, v_ref, a_ref, g_ref, scv, sz, n_seq, n_meta)
        w3, w31 = wsc_ref[0], wcf_ref[0]
        dub_ref, duc_ref, duv_ref, dua_ref, dug_ref = (du_ref.at[m] for m in range(5))
        sds_[0:n_meta, :] = jnp.zeros((n_meta, cb), F32)
        sdz[0:n_meta, :] = jnp.zeros((n_meta, cb), F32)
        behind = logical + tail - (n_meta + n_seq)
        sds_[n_meta + n_seq:logical + tail, :] = jnp.zeros((behind, cb), F32)
        sdz[n_meta + n_seq:logical + tail, :] = jnp.zeros((behind, cb), F32)
        accsc[...] = jnp.zeros_like(accsc)
        acccf[...] = jnp.zeros_like(acccf)
        accb[...] = jnp.zeros_like(accb)

        def forward_chunk(i, carry):
            t0 = pl.multiple_of(i * ch, ch)
            rws = pl.ds(t0, ch)
            dy = dysc_ref[rws, :].astype(F32)
            ds = dy * b_ref[rws, :].astype(F32)
            dz = dzc_ref[rws, :]
            sds_[pl.ds(n_meta + t0, ch), :] = ds
            sdz[pl.ds(n_meta + t0, ch), :] = dz
            s = jnp.zeros((ch, cb), F32)
            for k, win in enumerate(_taps(scv, t0 + a_sc, [off_sc - a_sc + k for k in range(SC_WIDTH)], ch)):
                s = s + win * w3[k:k + 1, :]
                accsc[k * SUBLANES:(k + 1) * SUBLANES, :] += _rows8(ds * win)
            dub_ref[rws, :] = (dy * s).astype(BF16)
            for k, win in enumerate(_taps(sz, t0 + a_cf, [off_cf - a_cf + k for k in range(CF_WIDTH)], ch)):
                acccf[k * SUBLANES:(k + 1) * SUBLANES, :] += _rows8(dz * win)
            accb[...] += _rows8(dz)
            return carry

        lax.fori_loop(0, n_seq // ch, forward_chunk, 0)

        def backward_chunk(i, carry):
            p0 = pl.multiple_of(i * ch, ch)
            dcv = jnp.zeros((ch, cb), F32)
            for k, win in enumerate(_taps(sds_, p0, [SC_WIDTH - 1 - k for k in range(SC_WIDTH)], ch)):
                dcv = dcv + win * w3[k:k + 1, :]
            dlcv[pl.ds(p0, ch), :] = dcv
            dzi = jnp.zeros((ch, cb), F32)
            for k, win in enumerate(_taps(sdz, p0, [CF_WIDTH - 1 - k for k in range(CF_WIDTH)], ch)):
                dzi = dzi + win * w31[k:k + 1, :]
            dlz[pl.ds(p0, ch), :] = dzi
            return carry

        lax.fori_loop(0, logical // ch, backward_chunk, 0)

        def gates(phys, logi):
            dcv, dzi = dlcv[logi, :], dlz[logi, :]
            duc_ref[phys, :] = (dcv * v_ref[phys, :].astype(F32)).astype(BF16)
            duv_ref[phys, :] = (dcv * c_ref[phys, :].astype(F32)).astype(BF16)
            s = _sigmoid(g_ref[phys, :].astype(F32))
            dua_ref[phys, :] = (dzi * s).astype(BF16)
            dug_ref[phys, :] = (dzi * a_ref[phys, :].astype(F32) * s * (1.0 - s)).astype(BF16)

        def gate_chunk(i, carry):
            t0 = pl.multiple_of(i * ch, ch)
            gates(pl.ds(t0, ch), pl.ds(n_meta + t0, ch))
            return carry

        lax.fori_loop(0, n_seq // ch, gate_chunk, 0)
        gates(pl.ds(n_seq, n_meta), pl.ds(0, n_meta))
        dub_ref[n_seq:rows, :] = jnp.zeros((rows - n_seq, cb), BF16)
        pad0 = n_seq + n_meta
        for ref in (duc_ref, duv_ref, dua_ref, dug_ref):
            ref[pad0:rows, :] = jnp.zeros((rows - pad0, cb), BF16)
        dwsc_ref[0] = jnp.sum(accsc[...].reshape(SC_WIDTH, SUBLANES, cb), axis=1)
        dwcf_ref[0] = jnp.sum(acccf[...].reshape(CF_WIDTH, SUBLANES, cb), axis=1)
        dcb_ref[...] = jnp.sum(accb[...], axis=0, keepdims=True)

    ucol = [pl.BlockSpec((rows, cb), functools.partial(lambda m, j: (0, m * nb + j), m)) for m in range(5)]
    blk = pl.BlockSpec((dysc.shape[0], cb), lambda j: (0, j))
    wsc_spec = pl.BlockSpec((1, SC_WIDTH, cb), lambda j: (j, 0, 0))
    wcf_spec = pl.BlockSpec((1, CF_WIDTH, cb), lambda j: (j, 0, 0))
    outs = _call(
        body, comm=comm, name=name, grid=(nb,),
        in_specs=ucol + [blk, blk, wsc_spec, wcf_spec],
        out_specs=[pl.BlockSpec((5, rows, cb), lambda j: (0, 0, j)), wsc_spec, wcf_spec,
                   pl.BlockSpec((1, cb), lambda j: (0, j))],
        out_shape=[_sds((5, rows, dc), BF16), _sds((nb, SC_WIDTH, cb), F32), _sds((nb, CF_WIDTH, cb), F32),
                   _sds((1, dc), F32)],
        scratch_shapes=[pltpu.VMEM((base + n_seq, cb), F32)] * 2 + [pltpu.VMEM((logical + tail, cb), F32)] * 2
        + [pltpu.VMEM((logical, cb), F32)] * 2
        + [pltpu.VMEM((SC_WIDTH * SUBLANES, cb), F32), pltpu.VMEM((CF_WIDTH * SUBLANES, cb), F32),
           pltpu.VMEM((SUBLANES, cb), F32)],
        compiler_params=_params(),
    )(u, u, u, u, u, dysc, dzc, wsc, wcf)
    return outs


def _row_tile(rows, cols):
    return rows // 4 if rows % 64 == 0 and rows * cols >= (1 << 18) else rows


def _pair_sum(grad, sib, idx, name, comm=()):
    _, rows, cols = grad.shape
    tr = rows

    def body(idx_ref, g_ref, s_ref, o_ref):
        o_ref[0] = (g_ref[0].astype(F32) + s_ref[0].astype(F32)).astype(o_ref.dtype)

    return _call(
        body, name=name,
        grid_spec=pltpu.PrefetchScalarGridSpec(
            num_scalar_prefetch=1, grid=(4, rows // tr),
            in_specs=[pl.BlockSpec((1, tr, cols), lambda k, i, idx_ref: (idx_ref[k], i, 0)),
                      pl.BlockSpec((1, tr, cols), lambda k, i, idx_ref: (idx_ref[4 + k], i, 0))],
            out_specs=pl.BlockSpec((1, tr, cols), lambda k, i, idx_ref: (k, i, 0))),
        out_shape=_sds((4, rows, cols), grad.dtype), compiler_params=_params(),
    )(idx, grad, sib)


def _adamw_math(w, g, m, v):
    m = ADAM_B1 * m + (1.0 - ADAM_B1) * g
    v = ADAM_B2 * v + (1.0 - ADAM_B2) * (g * g)
    m_hat = m / (1.0 - ADAM_B1 ** ADAM_STEP)
    v_hat = v / (1.0 - ADAM_B2 ** ADAM_STEP)
    delta = -ADAM_LR * (m_hat / (jnp.sqrt(v_hat) + ADAM_EPS) + ADAM_WD * w)
    return delta, m, v


def _adamw_sharded(own, got, w, m, v, name, comm=()):
    rows, cols = w.shape
    tr = _row_tile(rows, cols)

    def body(own_ref, g0_ref, g1_ref, g2_ref, w_ref, m_ref, v_ref, g_ref, d_ref, nm_ref, nv_ref):
        g = own_ref[0].astype(F32) + g0_ref[0].astype(F32) + g1_ref[0].astype(F32) + g2_ref[0].astype(F32)
        delta, nm, nv = _adamw_math(w_ref[...], g, m_ref[...], v_ref[...])
        g_ref[...] = g
        d_ref[...] = delta
        nm_ref[...] = nm
        nv_ref[...] = nv

    flat = pl.BlockSpec((tr, cols), lambda i: (i, 0))
    slot = [pl.BlockSpec((1, tr, cols), functools.partial(lambda k, i: (k, i, 0), k)) for k in range(3)]
    return _call(
        body, comm=comm, name=name, grid=(rows // tr,),
        in_specs=[slot[0]] + slot + [flat] * 3, out_specs=[flat] * 4, out_shape=[_sds((rows, cols), F32)] * 4,
        compiler_params=_params(),
    )(own, got, got, got, w, m, v)


def _adamw_replicated(gathered, segs, ws, ms, vs, loss_scale, name, comm=()):
    n = len(ws)

    def body(*refs):
        gat = refs[0]
        w_refs, m_refs, v_refs = refs[1:1 + n], refs[1 + n:1 + 2 * n], refs[1 + 2 * n:1 + 3 * n]
        outs = refs[1 + 3 * n:]

        def total(off, width):
            s = gat[0, :, off:off + width]
            for k in range(1, N_DEV):
                s = s + gat[k, :, off:off + width]
            return s

        outs[0][...] = loss_scale * total(segs[n][0], segs[n][1])
        for p in range(n):
            g = total(*segs[p])
            delta, nm, nv = _adamw_math(w_refs[p][...], g, m_refs[p][...], v_refs[p][...])
            for q, val in enumerate((g, delta, nm, nv)):
                outs[1 + 4 * p + q][...] = val
        for e, seg in enumerate(segs[n + 1:]):
            outs[1 + 4 * n + e][...] = total(*seg)

    return _call(
        body, name=name,
        out_shape=[_sds((1, segs[n][1]), F32)] + [_sds(w.shape, F32) for w in ws for _ in range(4)]
        + [_sds((1, seg[1]), F32) for seg in segs[n + 1:]],
        compiler_params=_params(),
    )(gathered, *ws, *ms, *vs)


def _adamw_plain(g, w, m, v, name):
    def body(g_ref, w_ref, m_ref, v_ref, d_ref, nm_ref, nv_ref):
        d_ref[...], nm_ref[...], nv_ref[...] = _adamw_math(w_ref[...], g_ref[...], m_ref[...], v_ref[...])

    return list(_call(body, name=name, out_shape=[_sds(w.shape, F32)] * 3)(g, w, m, v))


REPLICATED = ("ffn1_norm", "mix_norm", "b_in", "conv_cf_b", "ln_cf_g", "ln_cf_b", "ffn2_norm", "final_norm")
SHARDED = ("meta_tokens", "ffn1_w_gate", "ffn1_w_up", "ffn1_w_down", "w_in", "conv_sc_w", "conv_cf_w", "w_out",
           "ffn2_w_gate", "ffn2_w_up", "ffn2_w_down")
WEIGHTS = ("meta_tokens", "ffn1_norm", "ffn1_w_gate", "ffn1_w_up", "ffn1_w_down", "mix_norm", "w_in", "b_in",
           "conv_sc_w", "conv_cf_w", "conv_cf_b", "ln_cf_g", "ln_cf_b", "w_out", "ffn2_norm", "ffn2_w_gate",
           "ffn2_w_up", "ffn2_w_down", "final_norm")


TRANSPOSED = ("ffn1_w_gate", "ffn1_w_up", "ffn2_w_gate", "ffn2_w_up")


def _blocks2d(k, a):
    a = a.reshape(a.shape[-2:]) if a.ndim >= 2 else a.reshape(1, -1)
    return a.T if k in TRANSPOSED else a


def _step(x, tgt, w, m, v):
    n_seq, d = x.shape[1], x.shape[2]
    n_meta = w["meta_tokens"].shape[0]
    rows = -(-(n_seq + n_meta) // ROW_ALIGN) * ROW_ALIGN
    tm = rows // N_ROW_TILES
    cb = w["conv_sc_w"].shape[-1]
    dc = cb * N_DEV
    w2 = {k: _blocks2d(k, a) for k, a in w.items()}
    m2 = {k: _blocks2d(k, a) for k, a in m.items()}
    v2 = {k: _blocks2d(k, a) for k, a in v.items()}

    def as_given(k, r):
        return (r.T if k in TRANSPOSED else r).reshape(w[k].shape)

    def cast(k):
        return w2[k].astype(BF16)

    full = dict(zip(("ffn1_w_gate", "meta_tokens", "conv_sc_w", "conv_cf_w"), _exchange_alone(
        _gather_all([cast("ffn1_w_gate"), w2["meta_tokens"], w2["conv_sc_w"], w2["conv_cf_w"]]), "gather_gate1")))
    meta = jnp.transpose(full["meta_tokens"], (1, 0, 2)).reshape(n_meta, d)
    hs0 = jnp.concatenate([x[0], meta, jnp.zeros((rows - n_seq - n_meta, d), F32)], axis=0)

    tx = n_seq // N_ROW_TILES
    h1, r1 = _rms_fwd(hs0, w2["ffn1_norm"], tm, "rms_in")
    up1 = _gather_all([cast("ffn1_w_up")]).before(15 / 16)
    tm2, tx2 = 2 * tm, 2 * tx
    g1 = _ffn_proj(h1, full["ffn1_w_gate"], tm2, "ffn1_gate", comm=[up1])
    down1 = _gather_all([cast("ffn1_w_down")]).before(15 / 16)
    u1 = _ffn_proj(h1, up1.results[0], tm2, "ffn1_up", comm=[down1])
    w_in_all, w_out1 = _gather_all([cast("w_in")]).before(3 / 4), _gather_direct([cast("w_out")])
    hs1, h2, r2 = _ffn_down_norm(g1, u1, down1.results[0], hs0, w2["mix_norm"], rows // DOWN_ROW_TILES, "ffn1_down",
                                 comm=[w_in_all, w_out1])
    win = jnp.transpose(w_in_all.results[0], (1, 0, 2)).reshape(d, -1)
    w_out2, gate2 = _gather_forward(w_out1.results), _gather_direct([cast("ffn2_w_gate")])
    u = _win_fwd(h2, win, w2["b_in"], dc, tm, "mix_in", comm=[w_out2, gate2])
    wout = w_out2.results[0].reshape(-1, d)
    gate2p = _gather_forward(gate2.results)
    ysc, zc = _conv_fwd(u, full["conv_sc_w"], full["conv_cf_w"], w2["conv_cf_b"], n_seq, n_meta, "conv_fwd", comm=[gate2p])
    up2 = _gather_all([cast("ffn2_w_up")])
    y, hs2, h3, r3 = _wout_fwd(zc, ysc, wout, hs1, w2["ln_cf_g"], w2["ln_cf_b"], w2["ffn2_norm"], n_seq, tx, "mix_out",
                               comm=[up2])
    down2 = _gather_all([cast("ffn2_w_down")]).before(5 / 8)
    full.update(ffn1_w_up=up1.results[0], ffn1_w_down=down1.results[0], ffn2_w_gate=gate2p.results[0],
                ffn2_w_up=up2.results[0])
    g2, u2 = _ffn_gu(h3, full["ffn2_w_gate"], full["ffn2_w_up"], tx2, "ffn2_gu", comm=[down2])
    full["ffn2_w_down"] = down2.results[0]
    dhs3, dhs3b, loss_p, dgf_p = _ffn_down_loss(
        g2, u2, full["ffn2_w_down"], hs2, w2["final_norm"], tgt[0], n_seq, n_seq // DOWN_ROW_TILES, "ffn2_down_loss")

    xi, yi, ci = lax.axis_index("x"), lax.axis_index("y"), lax.axis_index("c")
    chip_of = [2 * xi + yi, 2 * (1 - xi) + yi, 2 * xi + (1 - yi), 2 * (1 - xi) + (1 - yi)]
    idx = jnp.stack([2 * ch + ci for ch in chip_of] + chip_of).astype(jnp.int32)
    out = {}

    def to_pairs(parts):
        return _pair_exchange([p.reshape((4, 2) + p.shape[1:]) for p in parts])

    def pair_sums(names, parts, pairs):
        return [_pair_sum(p, s, idx, "pair_sum_" + k) for k, p, s in zip(names, parts, pairs.results)]

    def update(names, sums, chips):
        for k, own, got in zip(names, sums, chips.results):
            res = _adamw_sharded(own, got, w2[k], m2[k], v2[k], "adamw_" + k)
            out[k] = [as_given(k, r) for r in res]

    tk = min(2 * ROW_ALIGN, n_seq)
    th_x, th_r = n_seq // DOWN_ROW_TILES, rows // DOWN_ROW_TILES

    dwd2 = _ffn_bwd_dw_down(g2, u2, dhs3b, tk, "ffn2_bwd_dw_down")
    p_d2 = to_pairs([dwd2])
    dg2, du2 = _ffn_bwd_da(dhs3b, full["ffn2_w_down"], g2, u2, tx2, "ffn2_bwd_da", comm=[p_d2])
    s_d2 = pair_sums(["ffn2_w_down"], [dwd2], p_d2)
    c_d2 = _chip_exchange(s_d2)
    gu2 = _ffn_bwd_dw_gu(h3, dg2, du2, tk, "ffn2_bwd_dw_gu", comm=[c_d2])
    update(["ffn2_w_down"], s_d2, c_d2)
    p_gu2 = to_pairs(gu2)
    half2 = _ffn_bwd_dh_gate(dg2, full["ffn2_w_gate"], th_x, "ffn2_bwd_dh_gate", comm=[p_gu2])
    s_gu2 = pair_sums(["ffn2_w_gate", "ffn2_w_up"], gu2, p_gu2)
    c_g2, c_u2 = _chip_exchange(s_gu2[:1]), _chip_exchange(s_gu2[1:])
    dhs2, dhs2b, dn3_p = _ffn_bwd_dh(du2, full["ffn2_w_up"], half2, hs2, r3, w2["ffn2_norm"], dhs3, th_x, "ffn2_bwd_dh",
                                     zeros=jnp.zeros((rows, d), F32))
    dysc, dzc, dlg_p, dlb_p = _wout_bwd_dy(dhs2b, wout, zc, w2["ln_cf_g"], w2["ln_cf_b"], tx, "mix_out_bwd_dy")

    dwout = _wout_bwd_dw(y, dhs2b, N_DEV, tk, "mix_out_bwd_dw")
    p_wo = to_pairs([dwout])
    du, dcsw, dccw, dcb = _conv_bwd(u, dysc, dzc, full["conv_sc_w"], full["conv_cf_w"], n_seq, n_meta, "conv_bwd",
                                    comm=[c_g2, p_wo])
    update(["ffn2_w_gate"], s_gu2[:1], c_g2)
    s_wo = pair_sums(["w_out"], [dwout], p_wo)
    c_wo = _chip_exchange(s_wo)
    dhs1, dhs1b, dn2_p = _win_bwd_dh(du, win, hs1, r2, w2["mix_norm"], dhs2, th_r, "mix_in_bwd_dh", comm=[c_u2])
    update(["ffn2_w_up"], s_gu2[1:], c_u2)
    dwin, dbin = _win_bwd_dw(h2, du, tk, "mix_in_bwd_dw", comm=[c_wo])
    update(["w_out"], s_wo, c_wo)
    mixer = ("w_in", "conv_sc_w", "conv_cf_w")
    gradsm = [jnp.transpose(dwin.reshape(d, N_DEV, -1), (1, 0, 2)), dcsw, dccw]
    p_m = to_pairs(gradsm)
    dwd1 = _ffn_bwd_dw_down(g1, u1, dhs1b, tk, "ffn1_bwd_dw_down", comm=[p_m])
    s_m = pair_sums(mixer, gradsm, p_m)
    c_m, p_d1 = _chip_exchange(s_m), to_pairs([dwd1])
    dg1, du1 = _ffn_bwd_da(dhs1b, full["ffn1_w_down"], g1, u1, tm2, "ffn1_bwd_da", comm=[c_m, p_d1])
    update(mixer, s_m, c_m)
    s_d1 = pair_sums(["ffn1_w_down"], [dwd1], p_d1)
    c_d1 = _chip_exchange(s_d1)
    gu1 = _ffn_bwd_dw_gu(h1, dg1, du1, tk, "ffn1_bwd_dw_gu", comm=[c_d1])
    update(["ffn1_w_down"], s_d1, c_d1)
    p_gu1 = to_pairs(gu1)
    _exchange_alone(p_gu1, "reduce_pair_ffn1")
    s_gu1 = pair_sums(["ffn1_w_gate", "ffn1_w_up"], gu1, p_gu1)
    c_g1, c_u1 = _chip_exchange(s_gu1[:1]), _chip_exchange(s_gu1[1:])
    half1 = _ffn_bwd_dh_gate(dg1, full["ffn1_w_gate"], th_r, "ffn1_bwd_dh_gate", comm=[c_g1])
    dhs0, _, dn1_p = _ffn_bwd_dh(du1, full["ffn1_w_up"], half1, hs0, r1, w2["ffn1_norm"], dhs1, th_r, "ffn1_bwd_dh",
                                 comm=[c_u1])
    update(["ffn1_w_gate"], s_gu1[:1], c_g1)
    update(["ffn1_w_up"], s_gu1[1:], c_u1)
    grad_x = dhs0[:n_seq][None]

    partial = {
        "ffn1_norm": dn1_p.sum(0), "mix_norm": dn2_p.sum(0), "b_in": dbin, "conv_cf_b": dcb,
        "ln_cf_g": dlg_p.sum(0), "ln_cf_b": dlb_p.sum(0), "ffn2_norm": dn3_p.sum(0), "final_norm": dgf_p.sum(0),
    }
    loss_seg = jnp.pad(loss_p.sum((0, 2)).reshape(1, 1), ((0, 0), (0, 127)))
    pieces = [partial[k] for k in REPLICATED] + [loss_seg, dhs0[n_seq:n_seq + n_meta].reshape(1, n_meta * d)]
    segs, off = [], 0
    for p in pieces:
        segs.append((off, p.shape[1]))
        off += p.shape[1]
    rows8 = _gather_rows(jnp.concatenate(pieces, axis=1), "gather_small")
    res = _adamw_replicated(rows8, segs, [w2[k] for k in REPLICATED], [m2[k] for k in REPLICATED],
                            [v2[k] for k in REPLICATED], 0.5 / d, "adamw_replicated")
    loss = res[0][0, 0]
    for p, k in enumerate(REPLICATED):
        out[k] = [r.reshape(w[k].shape) for r in res[1 + 4 * p:5 + 4 * p]]
    ob = w2["meta_tokens"].shape[1]
    gmeta = lax.dynamic_slice_in_dim(res[-1].reshape(n_meta, d), (4 * xi + 2 * yi + ci) * ob, ob, axis=1)
    out["meta_tokens"] = [gmeta] + _adamw_plain(gmeta, w2["meta_tokens"], m2["meta_tokens"], v2["meta_tokens"], "adamw_meta_tokens")

    return (loss, grad_x, *[out[k][0] for k in WEIGHTS], *[out[k][1] for k in WEIGHTS],
            *[out[k][2] for k in WEIGHTS], *[out[k][3] for k in WEIGHTS])


def kernel(x, meta_tokens, ffn1_norm, ffn1_w_gate, ffn1_w_up, ffn1_w_down, mix_norm, w_in, b_in, conv_sc_w, conv_cf_w, conv_cf_b, ln_cf_g, ln_cf_b, w_out, ffn2_norm, ffn2_w_gate, ffn2_w_up, ffn2_w_down, final_norm, loss_target, m_meta_tokens, m_ffn1_norm, m_ffn1_w_gate, m_ffn1_w_up, m_ffn1_w_down, m_mix_norm, m_w_in, m_b_in, m_conv_sc_w, m_conv_cf_w, m_conv_cf_b, m_ln_cf_g, m_ln_cf_b, m_w_out, m_ffn2_norm, m_ffn2_w_gate, m_ffn2_w_up, m_ffn2_w_down, m_final_norm, v_meta_tokens, v_ffn1_norm, v_ffn1_w_gate, v_ffn1_w_up, v_ffn1_w_down, v_mix_norm, v_w_in, v_b_in, v_conv_sc_w, v_conv_cf_w, v_conv_cf_b, v_ln_cf_g, v_ln_cf_b, v_w_out, v_ffn2_norm, v_ffn2_w_gate, v_ffn2_w_up, v_ffn2_w_down, v_final_norm):
    given = dict(locals())
    w = {k: given[k] for k in WEIGHTS}
    m = {k: given["m_" + k] for k in WEIGHTS}
    v = {k: given["v_" + k] for k in WEIGHTS}
    return _step(x, loss_target, w, m, v)
```

```python
import functools

import jax
import jax.numpy as jnp
from jax import lax
from jax.experimental import pallas as pl
from jax.experimental.pallas import tpu as pltpu

F32 = jnp.float32
BF16 = jnp.bfloat16
EPS = 1e-6
FFN_RES_SCALE = 0.5
SC_WIDTH = 3
CF_WIDTH = 31
ADAM_LR = 0.001
ADAM_B1 = 0.9
ADAM_B2 = 0.999
ADAM_EPS = 1e-08
ADAM_WD = 0.01
ADAM_STEP = 10

N_DEV = 8
N_ROW_TILES = 8
ROW_ALIGN = 256
CONV_PAD = 32
CONV_CH = 128
SUB_ROWS = 32
DOWN_ROW_TILES = 16
GATHER_PARTS = 2
SUBLANES = 8
BF16_ROWS = 16
VMEM_LIMIT = 56 * 1024 * 1024
MESH = pl.DeviceIdType.MESH
ANY = pl.BlockSpec(memory_space=pl.ANY)

NT = (((1,), (1,)), ((), ()))
TN = (((0,), (0,)), ((), ()))


def _pallas(body, **kw):
    return pl.pallas_call(body, **kw)


class _Exchange:
    def __init__(self, inputs, out_shapes, sem_shapes, start, finish, aliases=None, early=None):
        self.inputs, self.out_shapes, self.sem_shapes = list(inputs), list(out_shapes), list(sem_shapes)
        self.start, self.finish, self.aliases = start, finish, dict(aliases or {})
        self.early, self.early_at = early, None
        self.results = None

    def before(self, share):
        self.early_at = share
        return self


def _call(body, comm=(), **kw):
    if not comm:
        return _pallas(body, **kw)
    grid = kw.pop("grid")
    in_specs = list(kw.pop("in_specs"))
    out_specs, out_shape = kw.pop("out_specs"), kw.pop("out_shape")
    scratch = list(kw.pop("scratch_shapes", []))
    single = not isinstance(out_shape, (list, tuple))
    out_specs, out_shape = ([out_specs], [out_shape]) if single else (list(out_specs), list(out_shape))
    n_in, n_out, n_scr = len(in_specs), len(out_shape), len(scratch)
    c_in = [a for job in comm for a in job.inputs]
    c_out = [s for job in comm for s in job.out_shapes]
    c_sem = [s for job in comm for s in job.sem_shapes]
    aliases, i0, o0 = dict(kw.pop("input_output_aliases", {})), n_in, n_out
    for job in comm:
        aliases.update({i0 + i: o0 + o for i, o in job.aliases.items()})
        i0, o0 = i0 + len(job.inputs), o0 + len(job.out_shapes)

    def hosted(*refs):
        pos = [0]

        def take(n):
            pos[0] += n
            return refs[pos[0] - n:pos[0]]

        ins, cins, outs, couts, scr, sems = take(n_in), take(len(c_in)), take(n_out), take(len(c_out)), take(n_scr), take(len(c_sem))
        ids = [pl.program_id(k) for k in range(len(grid))]
        first = functools.reduce(jnp.logical_and, [i == 0 for i in ids])
        last = functools.reduce(jnp.logical_and, [i == g - 1 for i, g in zip(ids, grid)])

        n_steps = functools.reduce(lambda a, b: a * b, grid, 1)
        step = functools.reduce(lambda acc, ig: acc * ig[1] + ig[0], zip(ids, grid), 0)

        def early_step(job):
            if job.early is None or job.early_at is None or n_steps < 2:
                return None
            return min(max(int(job.early_at * n_steps), 0), n_steps - 2)

        def each(phase, chosen=None):
            i, o, s = 0, 0, 0
            for job in comm:
                ni, no, ns = len(job.inputs), len(job.out_shapes), len(job.sem_shapes)
                if chosen is None or chosen(job):
                    getattr(job, phase)(cins[i:i + ni], couts[o:o + no], sems[s:s + ns])
                i, o, s = i + ni, o + no, s + ns

        @pl.when(first)
        def _():
            each("start")

        body(*ins, *outs, *scr)

        for at in sorted({early_step(job) for job in comm} - {None}):
            @pl.when(step == at)
            def _():
                each("early", lambda job: early_step(job) == at)

        @pl.when(last)
        def _():
            each("early", lambda job: job.early is not None and early_step(job) is None)
            each("finish")

    call = _pallas(
        hosted, grid=grid, in_specs=in_specs + [ANY] * len(c_in), out_specs=out_specs + [ANY] * len(c_out),
        out_shape=out_shape + c_out, scratch_shapes=scratch + c_sem, input_output_aliases=aliases, **kw)

    def run(*args):
        res = call(*args, *c_in)
        o = n_out
        for job in comm:
            job.results = list(res[o:o + len(job.out_shapes)])
            o += len(job.out_shapes)
        return res[0] if single else list(res[:n_out])

    return run


def _exchange_alone(job, name, comm=()):
    n_in, n_out = len(job.inputs), len(job.out_shapes)

    def body(*refs):
        ins, outs, sems = refs[:n_in], refs[n_in:n_in + n_out], refs[n_in + n_out:]
        job.start(ins, outs, sems)
        if job.early is not None:
            job.early(ins, outs, sems)
        job.finish(ins, outs, sems)

    res = _pallas(body, name=name, in_specs=[ANY] * n_in, out_specs=[ANY] * n_out, out_shape=job.out_shapes,
                  scratch_shapes=job.sem_shapes, input_output_aliases=job.aliases)(*job.inputs)
    job.results = list(res)
    return job.results


def _params(**kw):
    return pltpu.CompilerParams(vmem_limit_bytes=VMEM_LIMIT, **kw)


def _sigmoid(x):
    return 0.5 * jnp.tanh(0.5 * x) + 0.5


def _sds(shape, dtype):
    return jax.ShapeDtypeStruct(shape, dtype)


def _place():
    x, y, c = lax.axis_index("x"), lax.axis_index("y"), lax.axis_index("c")
    chips = [(1 - x, y), (x, 1 - y), (1 - x, 1 - y)]
    return x, y, c, chips


def _slot(ref, p):
    return ref.at[4 * p[0] + 2 * p[1] + p[2]]


def _remote(src, dst, send_sem, recv_sem, to):
    return pltpu.make_async_remote_copy(src_ref=src, dst_ref=dst, send_sem=send_sem, recv_sem=recv_sem,
                                        device_id=to, device_id_type=MESH)


def _gather_direct(arrs):
    n = len(arrs)

    def copies(ins, outs, sems):
        send_sems, recv_sems, local_sems = sems
        x, y, c, chips = _place()
        me = (x, y, c)
        peers = [(x, y, 1 - c)] + [(*chip, c) for chip in chips]
        local = [pltpu.make_async_copy(ins[a], _slot(outs[a], me), local_sems.at[a]) for a in range(n)]
        sends = [_remote(ins[a], _slot(outs[a], me), send_sems.at[a, k], recv_sems.at[a, k], peer)
                 for a in range(n) for k, peer in enumerate(peers)]
        arrivals = [_remote(ins[a], _slot(outs[a], peer), send_sems.at[a, k], recv_sems.at[a, k], peer)
                    for a in range(n) for k, peer in enumerate(peers)]
        return local, sends, arrivals

    def start(ins, outs, sems):
        local, sends, _ = copies(ins, outs, sems)
        for cp in local + sends:
            cp.start()

    def finish(ins, outs, sems):
        local, sends, arrivals = copies(ins, outs, sems)
        for cp in arrivals:
            cp.wait_recv()
        for cp in sends:
            cp.wait_send()
        for cp in local:
            cp.wait()

    dma = pltpu.SemaphoreType.DMA
    return _Exchange(arrs, [_sds((N_DEV,) + a.shape, a.dtype) for a in arrs], [dma((n, 4)), dma((n, 4)), dma((n,))],
                     start, finish)


def _gather_forward(gathered):
    n = len(gathered)

    def copies(ins, outs, sems):
        send_sems, recv_sems = sems
        x, y, c, chips = _place()
        sibling = (x, y, 1 - c)
        sends = [_remote(_slot(ins[a], (*chip, c)), _slot(outs[a], (*chip, c)), send_sems.at[a, j], recv_sems.at[a, j], sibling)
                 for a in range(n) for j, chip in enumerate(chips)]
        arrivals = [_remote(_slot(ins[a], (*chip, c)), _slot(outs[a], (*chip, 1 - c)), send_sems.at[a, j], recv_sems.at[a, j], sibling)
                    for a in range(n) for j, chip in enumerate(chips)]
        return sends, arrivals

    def start(ins, outs, sems):
        for cp in copies(ins, outs, sems)[0]:
            cp.start()

    def finish(ins, outs, sems):
        sends, arrivals = copies(ins, outs, sems)
        for cp in arrivals:
            cp.wait_recv()
        for cp in sends:
            cp.wait_send()

    dma = pltpu.SemaphoreType.DMA
    return _Exchange(gathered, [_sds(a.shape, a.dtype) for a in gathered], [dma((n, 3)), dma((n, 3))], start, finish,
                     aliases={a: a for a in range(n)})


def _gather_all(arrs, parts=GATHER_PARTS):
    n = len(arrs)
    split = [parts if a.shape[0] % (parts * BF16_ROWS) == 0 else 1 for a in arrs]
    members = [[a for a in range(n) if p < split[a]] for p in range(parts)]

    def part_of(a, p):
        size = arrs[a].shape[0] // split[a]
        return pl.ds(p * size, size), _sds((size,) + arrs[a].shape[1:], arrs[a].dtype)

    direct = [_gather_direct([part_of(a, p)[1] for a in members[p]]) for p in range(parts)]
    passed = [_gather_forward([_sds((N_DEV,) + part_of(a, p)[1].shape, arrs[a].dtype) for a in members[p]])
              for p in range(parts)]
    jobs = direct + passed
    sem_shapes = [s for job in jobs for s in job.sem_shapes]

    def views(ins, outs, sems):
        found, at = [], 0
        for i, job in enumerate(jobs):
            p = i % parts
            rows = [part_of(a, p)[0] for a in members[p]]
            found.append(([ins[a].at[r] for a, r in zip(members[p], rows)],
                          [outs[a].at[:, r] for a, r in zip(members[p], rows)],
                          sems[at:at + len(job.sem_shapes)]))
            at += len(job.sem_shapes)
        return found

    def start(ins, outs, sems):
        for job, (i, o, s) in list(zip(jobs, views(ins, outs, sems)))[:parts]:
            job.start(i, o, s)

    def early(ins, outs, sems):
        found = views(ins, outs, sems)
        for p in range(parts):
            i, o, s = found[p]
            direct[p].finish(i, o, s)
            _, o2, s2 = found[parts + p]
            passed[p].start(o2, o2, s2)

    def finish(ins, outs, sems):
        found = views(ins, outs, sems)
        for p in range(parts):
            _, o2, s2 = found[parts + p]
            passed[p].finish(o2, o2, s2)

    return _Exchange(arrs, [_sds((N_DEV,) + a.shape, a.dtype) for a in arrs], sem_shapes, start, finish, early=early)


def _pair_exchange(arrs):
    n = len(arrs)

    def copies(ins, outs, sems):
        x, y, c, _ = _place()
        return [_remote(ins[a].at[:, 1 - c], outs[a], sems[0].at[a], sems[1].at[a], (x, y, 1 - c)) for a in range(n)]

    def start(ins, outs, sems):
        for cp in copies(ins, outs, sems):
            cp.start()

    def finish(ins, outs, sems):
        for cp in copies(ins, outs, sems):
            cp.wait()

    dma = pltpu.SemaphoreType.DMA
    return _Exchange(arrs, [_sds((4,) + a.shape[2:], a.dtype) for a in arrs], [dma((n,)), dma((n,))], start, finish)


def _chip_exchange(arrs):
    n = len(arrs)

    def copies(ins, outs, sems):
        x, y, c, chips = _place()
        return [_remote(ins[a].at[1 + j], outs[a].at[j], sems[0].at[a, j], sems[1].at[a, j], (*chip, c))
                for a in range(n) for j, chip in enumerate(chips)]

    def start(ins, outs, sems):
        for cp in copies(ins, outs, sems):
            cp.start()

    def finish(ins, outs, sems):
        for cp in copies(ins, outs, sems):
            cp.wait()

    dma = pltpu.SemaphoreType.DMA
    return _Exchange(arrs, [_sds((3,) + a.shape[1:], a.dtype) for a in arrs], [dma((n, 3)), dma((n, 3))], start, finish)


def _gather_rows(vec, name, comm=()):
    def body(in_ref, out_ref, send_sems, recv_sems, local_sem):
        x, y, c, _ = _place()
        me = 4 * x + 2 * y + c
        mine = pltpu.make_async_copy(in_ref, out_ref.at[me], local_sem)
        mine.start()
        copies = []
        for k in range(1, N_DEV):
            to = (1 - x if k & 4 else x, 1 - y if k & 2 else y, 1 - c if k & 1 else c)
            copies.append(pltpu.make_async_remote_copy(
                src_ref=in_ref, dst_ref=out_ref.at[me], send_sem=send_sems.at[k - 1], recv_sem=recv_sems.at[k - 1],
                device_id=to, device_id_type=MESH))
        for cp in copies:
            cp.start()
        for cp in copies:
            cp.wait()
        mine.wait()

    return _call(
        body, name=name, out_shape=_sds((N_DEV,) + vec.shape, vec.dtype), in_specs=[ANY], out_specs=ANY,
        scratch_shapes=[pltpu.SemaphoreType.DMA((7,)), pltpu.SemaphoreType.DMA((7,)), pltpu.SemaphoreType.DMA],
    )(vec)


def _group_rows(tm):
    return SUB_ROWS if tm % SUB_ROWS == 0 else BF16_ROWS


def _for_row_groups(tm, fn):
    sub = _group_rows(tm)

    def step(i, carry):
        fn(pl.ds(pl.multiple_of(i * sub, sub), sub), i * sub)
        return carry

    lax.fori_loop(0, tm // sub, step, 0)


def _once(shape, index_map):
    return pl.BlockSpec(shape, index_map, pipeline_mode=pl.Buffered(1))


def _rms_fwd(hs, g, tm, name, comm=()):
    rows, d = hs.shape

    def body(hs_ref, g_ref, h_ref, r_ref):
        def group(rs, _):
            xv = hs_ref[rs, :]
            r = lax.rsqrt(jnp.mean(xv * xv, axis=-1, keepdims=True) + EPS)
            h_ref[rs, :] = (xv * r * g_ref[...]).astype(BF16)
            r_ref[rs, :] = r

        _for_row_groups(tm, group)

    return _call(
        body, comm=comm, name=name, grid=(rows // tm,),
        in_specs=[pl.BlockSpec((tm, d), lambda i: (i, 0)), pl.BlockSpec((1, d), lambda i: (0, 0))],
        out_specs=[pl.BlockSpec((tm, d), lambda i: (i, 0)), pl.BlockSpec((tm, 1), lambda i: (i, 0))],
        out_shape=[_sds((rows, d), BF16), _sds((rows, 1), F32)], compiler_params=_params(),
    )(hs, g)


def _ffn_gu(h, wg, wu, tm, name, comm=()):
    rows, d = h.shape
    nb, fb, _ = wg.shape

    def body(h_ref, wg_ref, wu_ref, g_ref, u_ref):
        hv = h_ref[...]
        g_ref[0] = lax.dot_general(hv, wg_ref[0], NT, preferred_element_type=F32).astype(BF16)
        u_ref[0] = lax.dot_general(hv, wu_ref[0], NT, preferred_element_type=F32).astype(BF16)

    wspec = pl.BlockSpec((1, fb, d), lambda j, i: (j, 0, 0))
    ospec = pl.BlockSpec((1, tm, fb), lambda j, i: (j, i, 0))
    return _call(
        body, comm=comm, name=name, grid=(nb, rows // tm),
        in_specs=[pl.BlockSpec((tm, d), lambda j, i: (i, 0)), wspec, wspec],
        out_specs=[ospec, ospec], out_shape=[_sds((nb, rows, fb), BF16)] * 2, compiler_params=_params(),
    )(h, wg, wu)


def _ffn_proj(h, w, tm, name, comm=()):
    rows, d = h.shape
    nb, fb, _ = w.shape

    def body(h_ref, w_ref, p_ref):
        p_ref[0] = lax.dot_general(h_ref[...], w_ref[0], NT, preferred_element_type=F32).astype(BF16)

    return _call(
        body, comm=comm, name=name, grid=(nb, rows // tm),
        in_specs=[pl.BlockSpec((tm, d), lambda j, i: (i, 0)), pl.BlockSpec((1, fb, d), lambda j, i: (j, 0, 0))],
        out_specs=pl.BlockSpec((1, tm, fb), lambda j, i: (j, i, 0)), out_shape=_sds((nb, rows, fb), BF16),
        compiler_params=_params(),
    )(h, w)


def _down_product(g_ref, u_ref, wd_ref, acc_ref):
    part = None
    for b in range(g_ref.shape[0]):
        gv = g_ref[b]
        a = gv * _sigmoid(gv) * u_ref[b]
        p = jnp.dot(a, wd_ref[b], preferred_element_type=F32)
        part = p if part is None else part + p
    acc_ref[...] = part


def _ffn_down_norm(g, u, wd, hs, gn, tm, name, comm=()):
    nb, rows, fb = g.shape
    d = hs.shape[1]

    def body(g_ref, u_ref, wd_ref, hs_ref, gn_ref, hsn_ref, hn_ref, rn_ref, acc_ref):
        _down_product(g_ref, u_ref, wd_ref, acc_ref)

        def group(rs, _):
            hsn = hs_ref[rs, :] + FFN_RES_SCALE * acc_ref[rs, :]
            r = lax.rsqrt(jnp.mean(hsn * hsn, axis=-1, keepdims=True) + EPS)
            hsn_ref[rs, :] = hsn
            hn_ref[rs, :] = (hsn * r * gn_ref[...]).astype(BF16)
            rn_ref[rs, :] = r

        _for_row_groups(tm, group)

    aspec = pl.BlockSpec((nb, tm, fb), lambda i: (0, i, 0))
    row = pl.BlockSpec((tm, d), lambda i: (i, 0))
    return _call(
        body, comm=comm, name=name, grid=(rows // tm,),
        in_specs=[aspec, aspec, _once((nb, fb, d), lambda i: (0, 0, 0)), row, pl.BlockSpec((1, d), lambda i: (0, 0))],
        out_specs=[row, row, pl.BlockSpec((tm, 1), lambda i: (i, 0))],
        out_shape=[_sds((rows, d), F32), _sds((rows, d), BF16), _sds((rows, 1), F32)],
        scratch_shapes=[pltpu.VMEM((tm, d), F32)], compiler_params=_params(),
    )(g, u, wd, hs, gn)


def _ffn_down_loss(g, u, wd, hs, gf, tgt, n_seq, tm, name, comm=()):
    nb, rows, fb = g.shape
    d = hs.shape[1]
    nt = rows // tm

    def body(g_ref, u_ref, wd_ref, hs_ref, gf_ref, tgt_ref, dhs_ref, dhsb_ref, loss_ref, dgf_ref, acc_ref):
        i = pl.program_id(0)
        _down_product(g_ref, u_ref, wd_ref, acc_ref)
        loss_ref[0] = jnp.zeros((1, d), F32)
        dgf_ref[0] = jnp.zeros((1, d), F32)

        def group(rs, r0):
            hs3 = hs_ref[rs, :] + FFN_RES_SCALE * acc_ref[rs, :]
            r = lax.rsqrt(jnp.mean(hs3 * hs3, axis=-1, keepdims=True) + EPS)
            gfv = gf_ref[...]
            y = hs3 * r
            rowid = i * tm + r0 + lax.broadcasted_iota(jnp.int32, (_group_rows(tm), 1), 0)
            err = jnp.where(rowid < n_seq, y * gfv - tgt_ref[rs, :], 0.0)
            loss_ref[0] += jnp.sum(err * err, axis=0, keepdims=True)
            dout = err * (1.0 / d)
            dgf_ref[0] += jnp.sum(dout * y, axis=0, keepdims=True)
            t = dout * gfv
            dhs = r * t - hs3 * (r * r * r) * jnp.mean(t * hs3, axis=-1, keepdims=True)
            dhs_ref[rs, :] = dhs
            dhsb_ref[rs, :] = dhs.astype(BF16)

        _for_row_groups(tm, group)

    aspec = pl.BlockSpec((nb, tm, fb), lambda i: (0, i, 0))
    row = pl.BlockSpec((tm, d), lambda i: (i, 0))
    part = pl.BlockSpec((1, 1, d), lambda i: (i, 0, 0))
    return _call(
        body, comm=comm, name=name, grid=(nt,),
        in_specs=[aspec, aspec, _once((nb, fb, d), lambda i: (0, 0, 0)), row, pl.BlockSpec((1, d), lambda i: (0, 0)), row],
        out_specs=[row, row, part, part],
        out_shape=[_sds((rows, d), F32), _sds((rows, d), BF16), _sds((nt, 1, d), F32), _sds((nt, 1, d), F32)],
        scratch_shapes=[pltpu.VMEM((tm, d), F32)], compiler_params=_params(),
    )(g, u, wd, hs, gf, tgt)


def _ffn_bwd_da(do, wd, g, u, tm, name, comm=()):
    nb, rows, fb = g.shape
    d = do.shape[1]

    def body(do_ref, wd_ref, g_ref, u_ref, dg_ref, du_ref):
        rs = pl.ds(pl.multiple_of(pl.program_id(1) * tm, tm), tm)
        da = (FFN_RES_SCALE * lax.dot_general(do_ref[rs, :], wd_ref[0], NT, preferred_element_type=F32)).astype(BF16)
        gv, uv = g_ref[0], u_ref[0]
        s = _sigmoid(gv)
        sg = gv * s
        du_ref[0] = da * sg
        dg_ref[0] = da * uv * (s + sg * (1.0 - s))

    aspec = pl.BlockSpec((1, tm, fb), lambda j, i: (j, i, 0))
    return _call(
        body, comm=comm, name=name, grid=(nb, rows // tm),
        in_specs=[_once((rows, d), lambda j, i: (0, 0)), pl.BlockSpec((1, fb, d), lambda j, i: (j, 0, 0)), aspec, aspec],
        out_specs=[aspec] * 2, out_shape=[_sds((nb, rows, fb), BF16)] * 2, compiler_params=_params(),
    )(do, wd, g, u)


def _rms_bwd(tm, d, dh_ref, hs_ref, r_ref, gn_ref, dres_ref, dhs_ref, dhsb_ref, dgn_ref):
    dgn_ref[0] = jnp.zeros((1, d), F32)

    def group(rs, _):
        dh, hs, r = dh_ref[rs, :], hs_ref[rs, :], r_ref[rs, :]
        dgn_ref[0] += jnp.sum(dh * (hs * r), axis=0, keepdims=True)
        t = dh * gn_ref[...]
        dhs = dres_ref[rs, :] + r * t - hs * (r * r * r) * jnp.mean(t * hs, axis=-1, keepdims=True)
        dhs_ref[rs, :] = dhs
        dhsb_ref[rs, :] = dhs.astype(BF16)

    _for_row_groups(tm, group)


def _blocks_dot(x_ref, w_ref):
    part = None
    for b in range(x_ref.shape[0]):
        p = jnp.dot(x_ref[b], w_ref[b], preferred_element_type=F32)
        part = p if part is None else part + p
    return part


def _ffn_bwd_dh_gate(dg, wg, tm, name, comm=()):
    nb, rows, fb = dg.shape
    d = wg.shape[2]

    def body(dg_ref, wg_ref, o_ref):
        o_ref[...] = _blocks_dot(dg_ref, wg_ref)

    return _call(
        body, comm=comm, name=name, grid=(rows // tm,),
        in_specs=[pl.BlockSpec((nb, tm, fb), lambda i: (0, i, 0)), _once((nb, fb, d), lambda i: (0, 0, 0))],
        out_specs=pl.BlockSpec((tm, d), lambda i: (i, 0)), out_shape=_sds((rows, d), F32), compiler_params=_params(),
    )(dg, wg)


def _ffn_bwd_dh(du, wu, half, hs, r, gn, dres, tm, name, comm=(), zeros=None):
    nb, rows, fb = du.shape
    d = hs.shape[1]
    nt = rows // tm
    extra = [] if zeros is None else [zeros]

    def body(du_ref, wu_ref, half_ref, hs_ref, r_ref, gn_ref, dres_ref, *rest):
        dhs_ref, dhsb_ref, dgn_ref, acc_ref = rest[len(extra):]
        acc_ref[...] = half_ref[...] + _blocks_dot(du_ref, wu_ref)
        _rms_bwd(tm, d, acc_ref, hs_ref, r_ref, gn_ref, dres_ref, dhs_ref, dhsb_ref, dgn_ref)

    row = pl.BlockSpec((tm, d), lambda i: (i, 0))
    return _call(
        body, comm=comm, name=name, grid=(nt,),
        in_specs=[pl.BlockSpec((nb, tm, fb), lambda i: (0, i, 0)), _once((nb, fb, d), lambda i: (0, 0, 0)), row, row,
                  pl.BlockSpec((tm, 1), lambda i: (i, 0)), pl.BlockSpec((1, d), lambda i: (0, 0)), row] + [ANY] * len(extra),
        out_specs=[row, row, pl.BlockSpec((1, 1, d), lambda i: (i, 0, 0))],
        out_shape=[_sds((rows, d) if zeros is None else zeros.shape, F32), _sds((rows, d), BF16), _sds((nt, 1, d), F32)],
        scratch_shapes=[pltpu.VMEM((tm, d), F32)], compiler_params=_params(),
        input_output_aliases={7: 0} if extra else {},
    )(du, wu, half, hs, r, gn, dres, *extra)


class _KSteps:
    def __init__(self, rows, tk):
        self.main, self.rem, self.tk = rows // tk, rows % tk, tk
        self.n = self.main + (1 if self.rem else 0)
        if self.rem:
            assert rows % self.rem == 0, (rows, tk)
            self.last = rows // self.rem - 1

    def specs(self, makers):
        main_at = (lambda k: jnp.minimum(k, self.main - 1)) if self.rem else (lambda k: k)
        out = [make(self.tk, main_at) for make in makers]
        if self.rem:
            out += [make(self.rem, lambda k: self.last) for make in makers]
        return out

    def args(self, arrs):
        return list(arrs) * (2 if self.rem else 1)

    def each(self, k, refs, fn):
        if not self.rem:
            fn(*refs)
            return
        n = len(refs) // 2
        pl.when(k < self.main)(lambda: fn(*refs[:n]))
        pl.when(k == self.main)(lambda: fn(*refs[n:]))


def _ffn_bwd_dw_gu(h, dg, du, tk, name, comm=()):
    nb, rows, fb = dg.shape
    d = h.shape[1]
    steps = _KSteps(rows, tk)

    def body(*refs):
        ins, (dwg_ref, dwu_ref, accg, accu) = refs[:-4], refs[-4:]
        k = pl.program_id(1)

        @pl.when(k == 0)
        def _():
            accg[...] = jnp.zeros_like(accg)
            accu[...] = jnp.zeros_like(accu)

        def add(h_ref, dg_ref, du_ref):
            hv = h_ref[...]
            accg[...] += lax.dot_general(dg_ref[0], hv, TN, preferred_element_type=F32)
            accu[...] += lax.dot_general(du_ref[0], hv, TN, preferred_element_type=F32)

        steps.each(k, ins, add)

        @pl.when(k == steps.n - 1)
        def _():
            dwg_ref[0] = accg[...].astype(BF16)
            dwu_ref[0] = accu[...].astype(BF16)

    wspec = pl.BlockSpec((1, fb, d), lambda j, k: (j, 0, 0))
    specs = steps.specs([lambda t, at: pl.BlockSpec((t, d), lambda j, k: (at(k), 0)),
                         lambda t, at: pl.BlockSpec((1, t, fb), lambda j, k: (j, at(k), 0)),
                         lambda t, at: pl.BlockSpec((1, t, fb), lambda j, k: (j, at(k), 0))])
    return _call(
        body, comm=comm, name=name, grid=(nb, steps.n),
        in_specs=specs, out_specs=[wspec] * 2,
        out_shape=[_sds((nb, fb, d), BF16)] * 2, scratch_shapes=[pltpu.VMEM((fb, d), F32)] * 2,
        compiler_params=_params(),
    )(*steps.args([h, dg, du]))


def _ffn_bwd_dw_down(g, u, do, tk, name, comm=()):
    nb, rows, fb = g.shape
    d = do.shape[1]
    steps = _KSteps(rows, tk)

    def body(*refs):
        ins, (dwd_ref, acc) = refs[:-2], refs[-2:]
        k = pl.program_id(1)

        @pl.when(k == 0)
        def _():
            acc[...] = jnp.zeros_like(acc)

        def add(g_ref, u_ref, do_ref):
            gv = g_ref[0]
            a = gv * _sigmoid(gv) * u_ref[0]
            acc[...] += lax.dot_general(a, do_ref[...], TN, preferred_element_type=F32)

        steps.each(k, ins, add)

        @pl.when(k == steps.n - 1)
        def _():
            dwd_ref[0] = (FFN_RES_SCALE * acc[...]).astype(BF16)

    specs = steps.specs([lambda t, at: pl.BlockSpec((1, t, fb), lambda j, k: (j, at(k), 0)),
                         lambda t, at: pl.BlockSpec((1, t, fb), lambda j, k: (j, at(k), 0)),
                         lambda t, at: pl.BlockSpec((t, d), lambda j, k: (at(k), 0))])
    return _call(
        body, comm=comm, name=name, grid=(nb, steps.n), in_specs=specs,
        out_specs=pl.BlockSpec((1, fb, d), lambda j, k: (j, 0, 0)), out_shape=_sds((nb, fb, d), BF16),
        scratch_shapes=[pltpu.VMEM((fb, d), F32)], compiler_params=_params(),
    )(*steps.args([g, u, do]))


def _win_fwd(h, w, b, dc, tm, name, comm=()):
    rows, d = h.shape
    ng = w.shape[1] // dc

    def body(h_ref, w_ref, b_ref, u_ref):
        u_ref[...] = (jnp.dot(h_ref[...], w_ref[...], preferred_element_type=F32) + b_ref[...]).astype(BF16)

    return _call(
        body, comm=comm, name=name, grid=(ng, rows // tm),
        in_specs=[pl.BlockSpec((tm, d), lambda m, i: (i, 0)), pl.BlockSpec((d, dc), lambda m, i: (0, m)),
                  pl.BlockSpec((1, dc), lambda m, i: (0, m))],
        out_specs=pl.BlockSpec((tm, dc), lambda m, i: (i, m)), out_shape=_sds((rows, ng * dc), BF16),
        compiler_params=_params(),
    )(h, w, b)


def _win_bwd_dh(du, w, hs, r, gn, dres, tm, name, comm=()):
    rows, d = hs.shape
    ng, _, dc = du.shape
    nt = rows // tm

    def body(du_ref, w_ref, hs_ref, r_ref, gn_ref, dres_ref, dhs_ref, dhsb_ref, dgn_ref, acc_ref):
        part = None
        for m in range(ng):
            p = lax.dot_general(du_ref[m], w_ref[:, m * dc:(m + 1) * dc], NT, preferred_element_type=F32)
            part = p if part is None else part + p
        acc_ref[...] = part
        _rms_bwd(tm, d, acc_ref, hs_ref, r_ref, gn_ref, dres_ref, dhs_ref, dhsb_ref, dgn_ref)

    row = pl.BlockSpec((tm, d), lambda i: (i, 0))
    return _call(
        body, comm=comm, name=name, grid=(nt,),
        in_specs=[pl.BlockSpec((ng, tm, dc), lambda i: (0, i, 0)), _once((d, ng * dc), lambda i: (0, 0)), row,
                  pl.BlockSpec((tm, 1), lambda i: (i, 0)), pl.BlockSpec((1, d), lambda i: (0, 0)), row],
        out_specs=[row, row, pl.BlockSpec((1, 1, d), lambda i: (i, 0, 0))],
        out_shape=[_sds((rows, d), F32), _sds((rows, d), BF16), _sds((nt, 1, d), F32)],
        scratch_shapes=[pltpu.VMEM((tm, d), F32)], compiler_params=_params(),
    )(du, w, hs, r, gn, dres)


def _win_bwd_dw(h, du, tk, name, comm=()):
    rows, d = h.shape
    ng, _, dc = du.shape
    steps = _KSteps(rows, tk)

    def body(*refs):
        ins, (dw_ref, db_ref, acc, accb) = refs[:-4], refs[-4:]
        k = pl.program_id(1)

        @pl.when(k == 0)
        def _():
            acc[...] = jnp.zeros_like(acc)
            accb[...] = jnp.zeros_like(accb)

        def add(h_ref, du_ref):
            duv = du_ref[0]
            acc[...] += lax.dot_general(h_ref[...], duv, TN, preferred_element_type=F32)
            accb[...] += jnp.sum(duv.astype(F32), axis=0, keepdims=True)

        steps.each(k, ins, add)

        @pl.when(k == steps.n - 1)
        def _():
            dw_ref[...] = acc[...].astype(BF16)
            db_ref[...] = accb[...]

    specs = steps.specs([lambda t, at: pl.BlockSpec((t, d), lambda m, k: (at(k), 0)),
                         lambda t, at: pl.BlockSpec((1, t, dc), lambda m, k: (m, at(k), 0))])
    return _call(
        body, comm=comm, name=name, grid=(ng, steps.n), in_specs=specs,
        out_specs=[pl.BlockSpec((d, dc), lambda m, k: (0, m)), pl.BlockSpec((1, dc), lambda m, k: (0, m))],
        out_shape=[_sds((d, ng * dc), BF16), _sds((1, ng * dc), F32)],
        scratch_shapes=[pltpu.VMEM((d, dc), F32), pltpu.VMEM((1, dc), F32)], compiler_params=_params(),
    )(*steps.args([h, du]))


def _layernorm_silu(zc, lg, lb):
    mu = jnp.mean(zc, axis=-1, keepdims=True)
    xc = zc - mu
    rstd = lax.rsqrt(jnp.mean(xc * xc, axis=-1, keepdims=True) + EPS)
    nrm = xc * rstd
    lin = nrm * lg + lb
    s = _sigmoid(lin)
    return nrm, rstd, lin, s


def _wout_fwd(zc, ysc, wout, hs, lg, lb, gn, rows, tm, name, comm=()):
    dc = zc.shape[1]
    d = hs.shape[1]

    def body(zc_ref, ysc_ref, w_ref, hs_ref, lg_ref, lb_ref, gn_ref, y_ref, hsn_ref, hn_ref, rn_ref):
        def mix(rs, _):
            _, _, lin, s = _layernorm_silu(zc_ref[rs, :], lg_ref[...], lb_ref[...])
            y_ref[rs, :dc] = ysc_ref[rs, :]
            y_ref[rs, dc:] = (lin * s).astype(BF16)

        _for_row_groups(tm, mix)
        hsn_ref[...] = jnp.dot(y_ref[...], w_ref[...], preferred_element_type=F32)

        def norm(rs, _):
            hsn = hs_ref[rs, :] + hsn_ref[rs, :]
            r = lax.rsqrt(jnp.mean(hsn * hsn, axis=-1, keepdims=True) + EPS)
            hsn_ref[rs, :] = hsn
            hn_ref[rs, :] = (hsn * r * gn_ref[...]).astype(BF16)
            rn_ref[rs, :] = r

        _for_row_groups(tm, norm)

    half = pl.BlockSpec((tm, dc), lambda i: (i, 0))
    row = pl.BlockSpec((tm, d), lambda i: (i, 0))
    vec_c = pl.BlockSpec((1, dc), lambda i: (0, 0))
    return _call(
        body, comm=comm, name=name, grid=(rows // tm,),
        in_specs=[half, half, _once((2 * dc, d), lambda i: (0, 0)), row, vec_c, vec_c,
                  pl.BlockSpec((1, d), lambda i: (0, 0))],
        out_specs=[pl.BlockSpec((tm, 2 * dc), lambda i: (i, 0)), row, row, pl.BlockSpec((tm, 1), lambda i: (i, 0))],
        out_shape=[_sds((rows, 2 * dc), BF16), _sds((rows, d), F32), _sds((rows, d), BF16), _sds((rows, 1), F32)],
        compiler_params=_params(),
    )(zc, ysc, wout, hs, lg, lb, gn)


def _wout_bwd_dy(do, wout, zc, lg, lb, tm, name, comm=()):
    rows, d = do.shape
    dc = zc.shape[1]
    nt = rows // tm

    def body(do_ref, w_ref, zc_ref, lg_ref, lb_ref, dysc_ref, dzc_ref, dlg_ref, dlb_ref, dy_ref):
        dy_ref[...] = lax.dot_general(do_ref[...], w_ref[...], NT, preferred_element_type=F32)
        dlb_ref[0] = jnp.zeros((1, dc), F32)
        dlg_ref[0] = jnp.zeros((1, dc), F32)

        def group(rs, _):
            dysc_ref[rs, :] = dy_ref[rs, :dc].astype(BF16)
            nrm, rstd, lin, s = _layernorm_silu(zc_ref[rs, :], lg_ref[...], lb_ref[...])
            dl = dy_ref[rs, dc:] * (s * (1.0 + lin * (1.0 - s)))
            dlb_ref[0] += jnp.sum(dl, axis=0, keepdims=True)
            dlg_ref[0] += jnp.sum(dl * nrm, axis=0, keepdims=True)
            dn = dl * lg_ref[...]
            dzc_ref[rs, :] = rstd * (dn - jnp.mean(dn, axis=-1, keepdims=True)
                                     - nrm * jnp.mean(dn * nrm, axis=-1, keepdims=True))

        _for_row_groups(tm, group)

    half = pl.BlockSpec((tm, dc), lambda i: (i, 0))
    vec_c = pl.BlockSpec((1, dc), lambda i: (0, 0))
    part = pl.BlockSpec((1, 1, dc), lambda i: (i, 0, 0))
    return _call(
        body, comm=comm, name=name, grid=(nt,),
        in_specs=[pl.BlockSpec((tm, d), lambda i: (i, 0)), _once((2 * dc, d), lambda i: (0, 0)), half, vec_c, vec_c],
        out_specs=[half, half, part, part],
        out_shape=[_sds((rows, dc), BF16), _sds((rows, dc), F32), _sds((nt, 1, dc), F32), _sds((nt, 1, dc), F32)],
        scratch_shapes=[pltpu.VMEM((tm, 2 * dc), F32)], compiler_params=_params(),
    )(do, wout, zc, lg, lb)


def _wout_bwd_dw(y, do, nb, tk, name, comm=()):
    rows, k2 = y.shape
    d = do.shape[1]
    nk = rows // tk

    def body(y_ref, do_ref, dw_ref, acc):
        k = pl.program_id(0)

        @pl.when(k == 0)
        def _():
            acc[...] = jnp.zeros_like(acc)

        acc[...] += lax.dot_general(y_ref[...], do_ref[...], TN, preferred_element_type=F32)

        @pl.when(k == nk - 1)
        def _():
            dw_ref[...] = acc[...].astype(BF16)

    return _call(
        body, comm=comm, name=name, grid=(nk,),
        in_specs=[pl.BlockSpec((tk, k2), lambda k: (k, 0)), pl.BlockSpec((tk, d), lambda k: (k, 0))],
        out_specs=_once((k2, d), lambda k: (0, 0)), out_shape=_sds((k2, d), BF16),
        scratch_shapes=[pltpu.VMEM((k2, d), F32)], compiler_params=_params(),
    )(y, do).reshape(nb, k2 // nb, d)


def _windows(win, n_res):
    length = win.shape[0]
    return [win if r == 0 else pltpu.roll(win, length - r, 0) for r in range(n_res)]


def _taps(src_ref, start, offsets, ch):
    span = -(-(max(offsets) + ch) // SUBLANES) * SUBLANES
    win = src_ref[pl.ds(start, span), :]
    shifted = _windows(win, min(SUBLANES, max(offsets) + 1))
    return [shifted[o % SUBLANES][(o // SUBLANES) * SUBLANES:(o // SUBLANES) * SUBLANES + ch] for o in offsets]


def _rows8(v):
    return jnp.sum(v.reshape(v.shape[0] // SUBLANES, SUBLANES, v.shape[1]), axis=0)


def _conv_geometry(n_seq, n_meta):
    base = CONV_PAD + n_meta
    off_cf = base - (CF_WIDTH - 1)
    off_sc = base - (SC_WIDTH - 1)
    logical = -(-(n_meta + n_seq) // CONV_CH) * CONV_CH
    return base, off_cf, off_sc, logical


def _fill_conv_inputs(c_ref, v_ref, a_ref, g_ref, scv, sz, n_seq, n_meta):
    base = CONV_PAD + n_meta
    cb = scv.shape[1]
    scv[0:CONV_PAD, :] = jnp.zeros((CONV_PAD, cb), F32)
    sz[0:CONV_PAD, :] = jnp.zeros((CONV_PAD, cb), F32)

    def put(src, dst, n):
        cv = c_ref[src, :].astype(F32) * v_ref[src, :].astype(F32)
        scv[dst, :] = cv
        sz[dst, :] = a_ref[src, :].astype(F32) * _sigmoid(g_ref[src, :].astype(F32))

    put(pl.ds(n_seq, n_meta), pl.ds(CONV_PAD, n_meta), n_meta)

    def chunk(i, carry):
        t0 = pl.multiple_of(i * CONV_CH, CONV_CH)
        put(pl.ds(t0, CONV_CH), pl.ds(base + t0, CONV_CH), CONV_CH)
        return carry

    lax.fori_loop(0, n_seq // CONV_CH, chunk, 0)


def _conv_fwd(u, wsc, wcf, cbias, n_seq, n_meta, name, comm=()):
    rows = u.shape[0]
    nb, _, cb = wsc.shape
    dc = nb * cb
    base, off_cf, off_sc, _ = _conv_geometry(n_seq, n_meta)
    a_cf, a_sc = off_cf // SUBLANES * SUBLANES, off_sc // SUBLANES * SUBLANES
    ch = CONV_CH

    def body(b_ref, c_ref, v_ref, a_ref, g_ref, wsc_ref, wcf_ref, cb_ref, ysc_ref, zc_ref, scv, sz):
        _fill_conv_inputs(c_ref, v_ref, a_ref, g_ref, scv, sz, n_seq, n_meta)
        w3, w31, bias = wsc_ref[0], wcf_ref[0], cb_ref[...]

        def chunk(i, carry):
            t0 = pl.multiple_of(i * ch, ch)
            acc = jnp.zeros((ch, cb), F32)
            for k, win in enumerate(_taps(sz, t0 + a_cf, [off_cf - a_cf + k for k in range(CF_WIDTH)], ch)):
                acc = acc + win * w31[k:k + 1, :]
            zc_ref[pl.ds(t0, ch), :] = acc + bias
            s = jnp.zeros((ch, cb), F32)
            for k, win in enumerate(_taps(scv, t0 + a_sc, [off_sc - a_sc + k for k in range(SC_WIDTH)], ch)):
                s = s + win * w3[k:k + 1, :]
            ysc_ref[pl.ds(t0, ch), :] = (b_ref[pl.ds(t0, ch), :].astype(F32) * s).astype(BF16)
            return carry

        lax.fori_loop(0, n_seq // ch, chunk, 0)
        ysc_ref[n_seq:rows, :] = jnp.zeros((rows - n_seq, cb), BF16)
        zc_ref[n_seq:rows, :] = jnp.zeros((rows - n_seq, cb), F32)

    ucol = [pl.BlockSpec((rows, cb), functools.partial(lambda m, j: (0, m * nb + j), m)) for m in range(5)]
    blk = pl.BlockSpec((rows, cb), lambda j: (0, j))
    return _call(
        body, comm=comm, name=name, grid=(nb,),
        in_specs=ucol + [pl.BlockSpec((1, SC_WIDTH, cb), lambda j: (j, 0, 0)),
                         pl.BlockSpec((1, CF_WIDTH, cb), lambda j: (j, 0, 0)), pl.BlockSpec((1, cb), lambda j: (0, j))],
        out_specs=[blk, blk], out_shape=[_sds((rows, dc), BF16), _sds((rows, dc), F32)],
        scratch_shapes=[pltpu.VMEM((base + n_seq, cb), F32)] * 2, compiler_params=_params(),
    )(u, u, u, u, u, wsc, wcf, cbias)


def _conv_bwd(u, dysc, dzc, wsc, wcf, n_seq, n_meta, name, comm=()):
    rows = u.shape[0]
    nb, _, cb = wsc.shape
    dc = nb * cb
    base, off_cf, off_sc, logical = _conv_geometry(n_seq, n_meta)
    a_cf, a_sc = off_cf // SUBLANES * SUBLANES, off_sc // SUBLANES * SUBLANES
    ch = CONV_CH
    tail = CONV_PAD

    def body(b_ref, c_ref, v_ref, a_ref, g_ref, dysc_ref, dzc_ref, wsc_ref, wcf_ref,
             du_ref, dwsc_ref, dwcf_ref, dcb_ref,
             scv, sz, sds_, sdz, dlcv, dlz, accsc, acccf, accb):
        _fill_conv_inputs(c_ref, v_ref, a_ref, g_ref, scv, sz, n_seq, n_meta)
        w3, w31 = wsc_ref[0], wcf_ref[0]
        dub_ref, duc_ref, duv_ref, dua_ref, dug_ref = (du_ref.at[m] for m in range(5))
        sds_[0:n_meta, :] = jnp.zeros((n_meta, cb), F32)
        sdz[0:n_meta, :] = jnp.zeros((n_meta, cb), F32)
        behind = logical + tail - (n_meta + n_seq)
        sds_[n_meta + n_seq:logical + tail, :] = jnp.zeros((behind, cb), F32)
        sdz[n_meta + n_seq:logical + tail, :] = jnp.zeros((behind, cb), F32)
        accsc[...] = jnp.zeros_like(accsc)
        acccf[...] = jnp.zeros_like(acccf)
        accb[...] = jnp.zeros_like(accb)

        def forward_chunk(i, carry):
            t0 = pl.multiple_of(i * ch, ch)
            rws = pl.ds(t0, ch)
            dy = dysc_ref[rws, :].astype(F32)
            ds = dy * b_ref[rws, :].astype(F32)
            dz = dzc_ref[rws, :]
            sds_[pl.ds(n_meta + t0, ch), :] = ds
            sdz[pl.ds(n_meta + t0, ch), :] = dz
            s = jnp.zeros((ch, cb), F32)
            for k, win in enumerate(_taps(scv, t0 + a_sc, [off_sc - a_sc + k for k in range(SC_WIDTH)], ch)):
                s = s + win * w3[k:k + 1, :]
                accsc[k * SUBLANES:(k + 1) * SUBLANES, :] += _rows8(ds * win)
            dub_ref[rws, :] = (dy * s).astype(BF16)
            for k, win in enumerate(_taps(sz, t0 + a_cf, [off_cf - a_cf + k for k in range(CF_WIDTH)], ch)):
                acccf[k * SUBLANES:(k + 1) * SUBLANES, :] += _rows8(dz * win)
            accb[...] += _rows8(dz)
            return carry

        lax.fori_loop(0, n_seq // ch, forward_chunk, 0)

        def backward_chunk(i, carry):
            p0 = pl.multiple_of(i * ch, ch)
            dcv = jnp.zeros((ch, cb), F32)
            for k, win in enumerate(_taps(sds_, p0, [SC_WIDTH - 1 - k for k in range(SC_WIDTH)], ch)):
                dcv = dcv + win * w3[k:k + 1, :]
            dlcv[pl.ds(p0, ch), :] = dcv
            dzi = jnp.zeros((ch, cb), F32)
            for k, win in enumerate(_taps(sdz, p0, [CF_WIDTH - 1 - k for k in range(CF_WIDTH)], ch)):
                dzi = dzi + win * w31[k:k + 1, :]
            dlz[pl.ds(p0, ch), :] = dzi
            return carry

        lax.fori_loop(0, logical // ch, backward_chunk, 0)

        def gates(phys, logi):
            dcv, dzi = dlcv[logi, :], dlz[logi, :]
            duc_ref[phys, :] = (dcv * v_ref[phys, :].astype(F32)).astype(BF16)
            duv_ref[phys, :] = (dcv * c_ref[phys, :].astype(F32)).astype(BF16)
            s = _sigmoid(g_ref[phys, :].astype(F32))
            dua_ref[phys, :] = (dzi * s).astype(BF16)
            dug_ref[phys, :] = (dzi * a_ref[phys, :].astype(F32) * s * (1.0 - s)).astype(BF16)

        def gate_chunk(i, carry):
            t0 = pl.multiple_of(i * ch, ch)
            gates(pl.ds(t0, ch), pl.ds(n_meta + t0, ch))
            return carry

        lax.fori_loop(0, n_seq // ch, gate_chunk, 0)
        gates(pl.ds(n_seq, n_meta), pl.ds(0, n_meta))
        dub_ref[n_seq:rows, :] = jnp.zeros((rows - n_seq, cb), BF16)
        pad0 = n_seq + n_meta
        for ref in (duc_ref, duv_ref, dua_ref, dug_ref):
            ref[pad0:rows, :] = jnp.zeros((rows - pad0, cb), BF16)
        dwsc_ref[0] = jnp.sum(accsc[...].reshape(SC_WIDTH, SUBLANES, cb), axis=1)
        dwcf_ref[0] = jnp.sum(acccf[...].reshape(CF_WIDTH, SUBLANES, cb), axis=1)
        dcb_ref[...] = jnp.sum(accb[...], axis=0, keepdims=True)

    ucol = [pl.BlockSpec((rows, cb), functools.partial(lambda m, j: (0, m * nb + j), m)) for m in range(5)]
    blk = pl.BlockSpec((dysc.shape[0], cb), lambda j: (0, j))
    wsc_spec = pl.BlockSpec((1, SC_WIDTH, cb), lambda j: (j, 0, 0))
    wcf_spec = pl.BlockSpec((1, CF_WIDTH, cb), lambda j: (j, 0, 0))
    outs = _call(
        body, comm=comm, name=name, grid=(nb,),
        in_specs=ucol + [blk, blk, wsc_spec, wcf_spec],
        out_specs=[pl.BlockSpec((5, rows, cb), lambda j: (0, 0, j)), wsc_spec, wcf_spec,
                   pl.BlockSpec((1, cb), lambda j: (0, j))],
        out_shape=[_sds((5, rows, dc), BF16), _sds((nb, SC_WIDTH, cb), F32), _sds((nb, CF_WIDTH, cb), F32),
                   _sds((1, dc), F32)],
        scratch_shapes=[pltpu.VMEM((base + n_seq, cb), F32)] * 2 + [pltpu.VMEM((logical + tail, cb), F32)] * 2
        + [pltpu.VMEM((logical, cb), F32)] * 2
        + [pltpu.VMEM((SC_WIDTH * SUBLANES, cb), F32), pltpu.VMEM((CF_WIDTH * SUBLANES, cb), F32),
           pltpu.VMEM((SUBLANES, cb), F32)],
        compiler_params=_params(),
    )(u, u, u, u, u, dysc, dzc, wsc, wcf)
    return outs


def _row_tile(rows, cols):
    return rows // 4 if rows % 64 == 0 and rows * cols >= (1 << 18) else rows


def _pair_sum(grad, sib, idx, name, comm=()):
    _, rows, cols = grad.shape
    tr = rows

    def body(idx_ref, g_ref, s_ref, o_ref):
        o_ref[0] = (g_ref[0].astype(F32) + s_ref[0].astype(F32)).astype(o_ref.dtype)

    return _call(
        body, name=name,
        grid_spec=pltpu.PrefetchScalarGridSpec(
            num_scalar_prefetch=1, grid=(4, rows // tr),
            in_specs=[pl.BlockSpec((1, tr, cols), lambda k, i, idx_ref: (idx_ref[k], i, 0)),
                      pl.BlockSpec((1, tr, cols), lambda k, i, idx_ref: (idx_ref[4 + k], i, 0))],
            out_specs=pl.BlockSpec((1, tr, cols), lambda k, i, idx_ref: (k, i, 0))),
        out_shape=_sds((4, rows, cols), grad.dtype), compiler_params=_params(),
    )(idx, grad, sib)


def _adamw_math(w, g, m, v):
    m = ADAM_B1 * m + (1.0 - ADAM_B1) * g
    v = ADAM_B2 * v + (1.0 - ADAM_B2) * (g * g)
    m_hat = m / (1.0 - ADAM_B1 ** ADAM_STEP)
    v_hat = v / (1.0 - ADAM_B2 ** ADAM_STEP)
    delta = -ADAM_LR * (m_hat / (jnp.sqrt(v_hat) + ADAM_EPS) + ADAM_WD * w)
    return delta, m, v


def _adamw_sharded(own, got, w, m, v, name, comm=()):
    rows, cols = w.shape
    tr = _row_tile(rows, cols)

    def body(own_ref, g0_ref, g1_ref, g2_ref, w_ref, m_ref, v_ref, g_ref, d_ref, nm_ref, nv_ref):
        g = own_ref[0].astype(F32) + g0_ref[0].astype(F32) + g1_ref[0].astype(F32) + g2_ref[0].astype(F32)
        delta, nm, nv = _adamw_math(w_ref[...], g, m_ref[...], v_ref[...])
        g_ref[...] = g
        d_ref[...] = delta
        nm_ref[...] = nm
        nv_ref[...] = nv

    flat = pl.BlockSpec((tr, cols), lambda i: (i, 0))
    slot = [pl.BlockSpec((1, tr, cols), functools.partial(lambda k, i: (k, i, 0), k)) for k in range(3)]
    return _call(
        body, comm=comm, name=name, grid=(rows // tr,),
        in_specs=[slot[0]] + slot + [flat] * 3, out_specs=[flat] * 4, out_shape=[_sds((rows, cols), F32)] * 4,
        compiler_params=_params(),
    )(own, got, got, got, w, m, v)


def _adamw_replicated(gathered, segs, ws, ms, vs, loss_scale, name, comm=()):
    n = len(ws)

    def body(*refs):
        gat = refs[0]
        w_refs, m_refs, v_refs = refs[1:1 + n], refs[1 + n:1 + 2 * n], refs[1 + 2 * n:1 + 3 * n]
        outs = refs[1 + 3 * n:]

        def total(off, width):
            s = gat[0, :, off:off + width]
            for k in range(1, N_DEV):
                s = s + gat[k, :, off:off + width]
            return s

        outs[0][...] = loss_scale * total(segs[n][0], segs[n][1])
        for p in range(n):
            g = total(*segs[p])
            delta, nm, nv = _adamw_math(w_refs[p][...], g, m_refs[p][...], v_refs[p][...])
            for q, val in enumerate((g, delta, nm, nv)):
                outs[1 + 4 * p + q][...] = val
        for e, seg in enumerate(segs[n + 1:]):
            outs[1 + 4 * n + e][...] = total(*seg)

    return _call(
        body, name=name,
        out_shape=[_sds((1, segs[n][1]), F32)] + [_sds(w.shape, F32) for w in ws for _ in range(4)]
        + [_sds((1, seg[1]), F32) for seg in segs[n + 1:]],
        compiler_params=_params(),
    )(gathered, *ws, *ms, *vs)


def _adamw_plain(g, w, m, v, name):
    def body(g_ref, w_ref, m_ref, v_ref, d_ref, nm_ref, nv_ref):
        d_ref[...], nm_ref[...], nv_ref[...] = _adamw_math(w_ref[...], g_ref[...], m_ref[...], v_ref[...])

    return list(_call(body, name=name, out_shape=[_sds(w.shape, F32)] * 3)(g, w, m, v))


REPLICATED = ("ffn1_norm", "mix_norm", "b_in", "conv_cf_b", "ln_cf_g", "ln_cf_b", "ffn2_norm", "final_norm")
SHARDED = ("meta_tokens", "ffn1_w_gate", "ffn1_w_up", "ffn1_w_down", "w_in", "conv_sc_w", "conv_cf_w", "w_out",
           "ffn2_w_gate", "ffn2_w_up", "ffn2_w_down")
WEIGHTS = ("meta_tokens", "ffn1_norm", "ffn1_w_gate", "ffn1_w_up", "ffn1_w_down", "mix_norm", "w_in", "b_in",
           "conv_sc_w", "conv_cf_w", "conv_cf_b", "ln_cf_g", "ln_cf_b", "w_out", "ffn2_norm", "ffn2_w_gate",
           "ffn2_w_up", "ffn2_w_down", "final_norm")


TRANSPOSED = ("ffn1_w_gate", "ffn1_w_up", "ffn2_w_gate", "ffn2_w_up")


def _blocks2d(k, a):
    a = a.reshape(a.shape[-2:]) if a.ndim >= 2 else a.reshape(1, -1)
    return a.T if k in TRANSPOSED else a


def _step(x, tgt, w, m, v):
    n_seq, d = x.shape[1], x.shape[2]
    n_meta = w["meta_tokens"].shape[0]
    rows = -(-(n_seq + n_meta) // ROW_ALIGN) * ROW_ALIGN
    tm = rows // N_ROW_TILES
    cb = w["conv_sc_w"].shape[-1]
    dc = cb * N_DEV
    w2 = {k: _blocks2d(k, a) for k, a in w.items()}
    m2 = {k: _blocks2d(k, a) for k, a in m.items()}
    v2 = {k: _blocks2d(k, a) for k, a in v.items()}

    def as_given(k, r):
        return (r.T if k in TRANSPOSED else r).reshape(w[k].shape)

    def cast(k):
        return w2[k].astype(BF16)

    full = dict(zip(("ffn1_w_gate", "meta_tokens", "conv_sc_w", "conv_cf_w"), _exchange_alone(
        _gather_all([cast("ffn1_w_gate"), w2["meta_tokens"], w2["conv_sc_w"], w2["conv_cf_w"]]), "gather_gate1")))
    meta = jnp.transpose(full["meta_tokens"], (1, 0, 2)).reshape(n_meta, d)
    hs0 = jnp.concatenate([x[0], meta, jnp.zeros((rows - n_seq - n_meta, d), F32)], axis=0)

    tx = n_seq // N_ROW_TILES
    h1, r1 = _rms_fwd(hs0, w2["ffn1_norm"], tm, "rms_in")
    up1 = _gather_all([cast("ffn1_w_up")]).before(15 / 16)
    tm2, tx2 = 2 * tm, 2 * tx
    g1 = _ffn_proj(h1, full["ffn1_w_gate"], tm2, "ffn1_gate", comm=[up1])
    down1 = _gather_all([cast("ffn1_w_down")]).before(15 / 16)
    u1 = _ffn_proj(h1, up1.results[0], tm2, "ffn1_up", comm=[down1])
    w_in_all, w_out1 = _gather_all([cast("w_in")]).before(3 / 4), _gather_direct([cast("w_out")])
    hs1, h2, r2 = _ffn_down_norm(g1, u1, down1.results[0], hs0, w2["mix_norm"], rows // DOWN_ROW_TILES, "ffn1_down",
                                 comm=[w_in_all, w_out1])
    win = jnp.transpose(w_in_all.results[0], (1, 0, 2)).reshape(d, -1)
    w_out2, gate2 = _gather_forward(w_out1.results), _gather_direct([cast("ffn2_w_gate")])
    u = _win_fwd(h2, win, w2["b_in"], dc, tm, "mix_in", comm=[w_out2, gate2])
    wout = w_out2.results[0].reshape(-1, d)
    gate2p = _gather_forward(gate2.results)
    ysc, zc = _conv_fwd(u, full["conv_sc_w"], full["conv_cf_w"], w2["conv_cf_b"], n_seq, n_meta, "conv_fwd", comm=[gate2p])
    up2 = _gather_all([cast("ffn2_w_up")])
    y, hs2, h3, r3 = _wout_fwd(zc, ysc, wout, hs1, w2["ln_cf_g"], w2["ln_cf_b"], w2["ffn2_norm"], n_seq, tx, "mix_out",
                               comm=[up2])
    down2 = _gather_all([cast("ffn2_w_down")]).before(5 / 8)
    full.update(ffn1_w_up=up1.results[0], ffn1_w_down=down1.results[0], ffn2_w_gate=gate2p.results[0],
                ffn2_w_up=up2.results[0])
    g2, u2 = _ffn_gu(h3, full["ffn2_w_gate"], full["ffn2_w_up"], tx2, "ffn2_gu", comm=[down2])
    full["ffn2_w_down"] = down2.results[0]
    dhs3, dhs3b, loss_p, dgf_p = _ffn_down_loss(
        g2, u2, full["ffn2_w_down"], hs2, w2["final_norm"], tgt[0], n_seq, n_seq // DOWN_ROW_TILES, "ffn2_down_loss")

    xi, yi, ci = lax.axis_index("x"), lax.axis_index("y"), lax.axis_index("c")
    chip_of = [2 * xi + yi, 2 * (1 - xi) + yi, 2 * xi + (1 - yi), 2 * (1 - xi) + (1 - yi)]
    idx = jnp.stack([2 * ch + ci for ch in chip_of] + chip_of).astype(jnp.int32)
    out = {}

    def to_pairs(parts):
        return _pair_exchange([p.reshape((4, 2) + p.shape[1:]) for p in parts])

    def pair_sums(names, parts, pairs):
        return [_pair_sum(p, s, idx, "pair_sum_" + k) for k, p, s in zip(names, parts, pairs.results)]

    def update(names, sums, chips):
        for k, own, got in zip(names, sums, chips.results):
            res = _adamw_sharded(own, got, w2[k], m2[k], v2[k], "adamw_" + k)
            out[k] = [as_given(k, r) for r in res]

    tk = min(4 * ROW_ALIGN, n_seq)
    th_x, th_r = n_seq // DOWN_ROW_TILES, rows // DOWN_ROW_TILES

    dwd2 = _ffn_bwd_dw_down(g2, u2, dhs3b, tk, "ffn2_bwd_dw_down")
    p_d2 = to_pairs([dwd2])
    dg2, du2 = _ffn_bwd_da(dhs3b, full["ffn2_w_down"], g2, u2, tx2, "ffn2_bwd_da", comm=[p_d2])
    s_d2 = pair_sums(["ffn2_w_down"], [dwd2], p_d2)
    c_d2 = _chip_exchange(s_d2)
    gu2 = _ffn_bwd_dw_gu(h3, dg2, du2, tk, "ffn2_bwd_dw_gu", comm=[c_d2])
    update(["ffn2_w_down"], s_d2, c_d2)
    p_gu2 = to_pairs(gu2)
    half2 = _ffn_bwd_dh_gate(dg2, full["ffn2_w_gate"], tx, "ffn2_bwd_dh_gate", comm=[p_gu2])
    s_gu2 = pair_sums(["ffn2_w_gate", "ffn2_w_up"], gu2, p_gu2)
    c_g2, c_u2 = _chip_exchange(s_gu2[:1]), _chip_exchange(s_gu2[1:])
    dhs2, dhs2b, dn3_p = _ffn_bwd_dh(du2, full["ffn2_w_up"], half2, hs2, r3, w2["ffn2_norm"], dhs3, th_x, "ffn2_bwd_dh",
                                     zeros=jnp.zeros((rows, d), F32))
    dysc, dzc, dlg_p, dlb_p = _wout_bwd_dy(dhs2b, wout, zc, w2["ln_cf_g"], w2["ln_cf_b"], tx, "mix_out_bwd_dy")

    dwout = _wout_bwd_dw(y, dhs2b, N_DEV, tk, "mix_out_bwd_dw")
    p_wo = to_pairs([dwout])
    du, dcsw, dccw, dcb = _conv_bwd(u, dysc, dzc, full["conv_sc_w"], full["conv_cf_w"], n_seq, n_meta, "conv_bwd",
                                    comm=[c_g2, p_wo])
    update(["ffn2_w_gate"], s_gu2[:1], c_g2)
    s_wo = pair_sums(["w_out"], [dwout], p_wo)
    c_wo = _chip_exchange(s_wo)
    dhs1, dhs1b, dn2_p = _win_bwd_dh(du, win, hs1, r2, w2["mix_norm"], dhs2, th_r, "mix_in_bwd_dh", comm=[c_u2])
    update(["ffn2_w_up"], s_gu2[1:], c_u2)
    dwin, dbin = _win_bwd_dw(h2, du, tk, "mix_in_bwd_dw", comm=[c_wo])
    update(["w_out"], s_wo, c_wo)
    mixer = ("w_in", "conv_sc_w", "conv_cf_w")
    gradsm = [jnp.transpose(dwin.reshape(d, N_DEV, -1), (1, 0, 2)), dcsw, dccw]
    p_m = to_pairs(gradsm)
    dwd1 = _ffn_bwd_dw_down(g1, u1, dhs1b, tk, "ffn1_bwd_dw_down", comm=[p_m])
    s_m = pair_sums(mixer, gradsm, p_m)
    c_m, p_d1 = _chip_exchange(s_m), to_pairs([dwd1])
    dg1, du1 = _ffn_bwd_da(dhs1b, full["ffn1_w_down"], g1, u1, tm2, "ffn1_bwd_da", comm=[c_m, p_d1])
    update(mixer, s_m, c_m)
    s_d1 = pair_sums(["ffn1_w_down"], [dwd1], p_d1)
    c_d1 = _chip_exchange(s_d1)
    gu1 = _ffn_bwd_dw_gu(h1, dg1, du1, tk, "ffn1_bwd_dw_gu", comm=[c_d1])
    update(["ffn1_w_down"], s_d1, c_d1)
    p_gu1 = to_pairs(gu1)
    _exchange_alone(p_gu1, "reduce_pair_ffn1")
    s_gu1 = pair_sums(["ffn1_w_gate", "ffn1_w_up"], gu1, p_gu1)
    c_g1, c_u1 = _chip_exchange(s_gu1[:1]), _chip_exchange(s_gu1[1:])
    half1 = _ffn_bwd_dh_gate(dg1, full["ffn1_w_gate"], tm, "ffn1_bwd_dh_gate", comm=[c_g1])
    dhs0, _, dn1_p = _ffn_bwd_dh(du1, full["ffn1_w_up"], half1, hs0, r1, w2["ffn1_norm"], dhs1, th_r, "ffn1_bwd_dh",
                                 comm=[c_u1])
    update(["ffn1_w_gate"], s_gu1[:1], c_g1)
    update(["ffn1_w_up"], s_gu1[1:], c_u1)
    grad_x = dhs0[:n_seq][None]

    partial = {
        "ffn1_norm": dn1_p.sum(0), "mix_norm": dn2_p.sum(0), "b_in": dbin, "conv_cf_b": dcb,
        "ln_cf_g": dlg_p.sum(0), "ln_cf_b": dlb_p.sum(0), "ffn2_norm": dn3_p.sum(0), "final_norm": dgf_p.sum(0),
    }
    loss_seg = jnp.pad(loss_p.sum((0, 2)).reshape(1, 1), ((0, 0), (0, 127)))
    pieces = [partial[k] for k in REPLICATED] + [loss_seg, dhs0[n_seq:n_seq + n_meta].reshape(1, n_meta * d)]
    segs, off = [], 0
    for p in pieces:
        segs.append((off, p.shape[1]))
        off += p.shape[1]
    rows8 = _gather_rows(jnp.concatenate(pieces, axis=1), "gather_small")
    res = _adamw_replicated(rows8, segs, [w2[k] for k in REPLICATED], [m2[k] for k in REPLICATED],
                            [v2[k] for k in REPLICATED], 0.5 / d, "adamw_replicated")
    loss = res[0][0, 0]
    for p, k in enumerate(REPLICATED):
        out[k] = [r.reshape(w[k].shape) for r in res[1 + 4 * p:5 + 4 * p]]
    ob = w2["meta_tokens"].shape[1]
    gmeta = lax.dynamic_slice_in_dim(res[-1].reshape(n_meta, d), (4 * xi + 2 * yi + ci) * ob, ob, axis=1)
    out["meta_tokens"] = [gmeta] + _adamw_plain(gmeta, w2["meta_tokens"], m2["meta_tokens"], v2["meta_tokens"], "adamw_meta_tokens")

    return (loss, grad_x, *[out[k][0] for k in WEIGHTS], *[out[k][1] for k in WEIGHTS],
            *[out[k][2] for k in WEIGHTS], *[out[k][3] for k in WEIGHTS])


def kernel(x, meta_tokens, ffn1_norm, ffn1_w_gate, ffn1_w_up, ffn1_w_down, mix_norm, w_in, b_in, conv_sc_w, conv_cf_w, conv_cf_b, ln_cf_g, ln_cf_b, w_out, ffn2_norm, ffn2_w_gate, ffn2_w_up, ffn2_w_down, final_norm, loss_target, m_meta_tokens, m_ffn1_norm, m_ffn1_w_gate, m_ffn1_w_up, m_ffn1_w_down, m_mix_norm, m_w_in, m_b_in, m_conv_sc_w, m_conv_cf_w, m_conv_cf_b, m_ln_cf_g, m_ln_cf_b, m_w_out, m_ffn2_norm, m_ffn2_w_gate, m_ffn2_w_up, m_ffn2_w_down, m_final_norm, v_meta_tokens, v_ffn1_norm, v_ffn1_w_gate, v_ffn1_w_up, v_ffn1_w_down, v_mix_norm, v_w_in, v_b_in, v_conv_sc_w, v_conv_cf_w, v_conv_cf_b, v_ln_cf_g, v_ln_cf_b, v_w_out, v_ffn2_norm, v_ffn2_w_gate, v_ffn2_w_up, v_ffn2_w_down, v_final_norm):
    given = dict(locals())
    w = {k: given[k] for k in WEIGHTS}
    m = {k: given["m_" + k] for k in WEIGHTS}
    v = {k: given["v_" + k] for k in WEIGHTS}
    return _step(x, loss_target, w, m, v)
```

```python
import functools

import jax
import jax.numpy as jnp
from jax import lax
from jax.experimental import pallas as pl
from jax.experimental.pallas import tpu as pltpu

F32 = jnp.float32
BF16 = jnp.bfloat16
EPS = 1e-6
FFN_RES_SCALE = 0.5
SC_WIDTH = 3
CF_WIDTH = 31
ADAM_LR = 0.001
ADAM_B1 = 0.9
ADAM_B2 = 0.999
ADAM_EPS = 1e-08
ADAM_WD = 0.01
ADAM_STEP = 10

N_DEV = 8
N_ROW_TILES = 8
ROW_ALIGN = 256
CONV_PAD = 32
CONV_CH = 128
SUB_ROWS = 32
DOWN_ROW_TILES = 16
GATHER_PARTS = 2
SUBLANES = 8
BF16_ROWS = 16
VMEM_LIMIT = 56 * 1024 * 1024
MESH = pl.DeviceIdType.MESH
ANY = pl.BlockSpec(memory_space=pl.ANY)

NT = (((1,), (1,)), ((), ()))
TN = (((0,), (0,)), ((), ()))


def _pallas(body, **kw):
    return pl.pallas_call(body, **kw)


class _Exchange:
    def __init__(self, inputs, out_shapes, sem_shapes, start, finish, aliases=None, early=None):
        self.inputs, self.out_shapes, self.sem_shapes = list(inputs), list(out_shapes), list(sem_shapes)
        self.start, self.finish, self.aliases = start, finish, dict(aliases or {})
        self.early, self.early_at = early, None
        self.results = None

    def before(self, share):
        self.early_at = share
        return self


def _call(body, comm=(), **kw):
    if not comm:
        return _pallas(body, **kw)
    grid = kw.pop("grid")
    in_specs = list(kw.pop("in_specs"))
    out_specs, out_shape = kw.pop("out_specs"), kw.pop("out_shape")
    scratch = list(kw.pop("scratch_shapes", []))
    single = not isinstance(out_shape, (list, tuple))
    out_specs, out_shape = ([out_specs], [out_shape]) if single else (list(out_specs), list(out_shape))
    n_in, n_out, n_scr = len(in_specs), len(out_shape), len(scratch)
    c_in = [a for job in comm for a in job.inputs]
    c_out = [s for job in comm for s in job.out_shapes]
    c_sem = [s for job in comm for s in job.sem_shapes]
    aliases, i0, o0 = dict(kw.pop("input_output_aliases", {})), n_in, n_out
    for job in comm:
        aliases.update({i0 + i: o0 + o for i, o in job.aliases.items()})
        i0, o0 = i0 + len(job.inputs), o0 + len(job.out_shapes)

    def hosted(*refs):
        pos = [0]

        def take(n):
            pos[0] += n
            return refs[pos[0] - n:pos[0]]

        ins, cins, outs, couts, scr, sems = take(n_in), take(len(c_in)), take(n_out), take(len(c_out)), take(n_scr), take(len(c_sem))
        ids = [pl.program_id(k) for k in range(len(grid))]
        first = functools.reduce(jnp.logical_and, [i == 0 for i in ids])
        last = functools.reduce(jnp.logical_and, [i == g - 1 for i, g in zip(ids, grid)])

        n_steps = functools.reduce(lambda a, b: a * b, grid, 1)
        step = functools.reduce(lambda acc, ig: acc * ig[1] + ig[0], zip(ids, grid), 0)

        def early_step(job):
            if job.early is None or job.early_at is None or n_steps < 2:
                return None
            return min(max(int(job.early_at * n_steps), 0), n_steps - 2)

        def each(phase, chosen=None):
            i, o, s = 0, 0, 0
            for job in comm:
                ni, no, ns = len(job.inputs), len(job.out_shapes), len(job.sem_shapes)
                if chosen is None or chosen(job):
                    getattr(job, phase)(cins[i:i + ni], couts[o:o + no], sems[s:s + ns])
                i, o, s = i + ni, o + no, s + ns

        @pl.when(first)
        def _():
            each("start")

        body(*ins, *outs, *scr)

        for at in sorted({early_step(job) for job in comm} - {None}):
            @pl.when(step == at)
            def _():
                each("early", lambda job: early_step(job) == at)

        @pl.when(last)
        def _():
            each("early", lambda job: job.early is not None and early_step(job) is None)
            each("finish")

    call = _pallas(
        hosted, grid=grid, in_specs=in_specs + [ANY] * len(c_in), out_specs=out_specs + [ANY] * len(c_out),
        out_shape=out_shape + c_out, scratch_shapes=scratch + c_sem, input_output_aliases=aliases, **kw)

    def run(*args):
        res = call(*args, *c_in)
        o = n_out
        for job in comm:
            job.results = list(res[o:o + len(job.out_shapes)])
            o += len(job.out_shapes)
        return res[0] if single else list(res[:n_out])

    return run


def _exchange_alone(job, name, comm=()):
    n_in, n_out = len(job.inputs), len(job.out_shapes)

    def body(*refs):
        ins, outs, sems = refs[:n_in], refs[n_in:n_in + n_out], refs[n_in + n_out:]
        job.start(ins, outs, sems)
        if job.early is not None:
            job.early(ins, outs, sems)
        job.finish(ins, outs, sems)

    res = _pallas(body, name=name, in_specs=[ANY] * n_in, out_specs=[ANY] * n_out, out_shape=job.out_shapes,
                  scratch_shapes=job.sem_shapes, input_output_aliases=job.aliases)(*job.inputs)
    job.results = list(res)
    return job.results


def _params(**kw):
    return pltpu.CompilerParams(vmem_limit_bytes=VMEM_LIMIT, **kw)


def _sigmoid(x):
    return 0.5 * jnp.tanh(0.5 * x) + 0.5


def _sds(shape, dtype):
    return jax.ShapeDtypeStruct(shape, dtype)


def _place():
    x, y, c = lax.axis_index("x"), lax.axis_index("y"), lax.axis_index("c")
    chips = [(1 - x, y), (x, 1 - y), (1 - x, 1 - y)]
    return x, y, c, chips


def _slot(ref, p):
    return ref.at[4 * p[0] + 2 * p[1] + p[2]]


def _remote(src, dst, send_sem, recv_sem, to):
    return pltpu.make_async_remote_copy(src_ref=src, dst_ref=dst, send_sem=send_sem, recv_sem=recv_sem,
                                        device_id=to, device_id_type=MESH)


def _gather_direct(arrs):
    n = len(arrs)

    def copies(ins, outs, sems):
        send_sems, recv_sems, local_sems = sems
        x, y, c, chips = _place()
        me = (x, y, c)
        peers = [(x, y, 1 - c)] + [(*chip, c) for chip in chips]
        local = [pltpu.make_async_copy(ins[a], _slot(outs[a], me), local_sems.at[a]) for a in range(n)]
        sends = [_remote(ins[a], _slot(outs[a], me), send_sems.at[a, k], recv_sems.at[a, k], peer)
                 for a in range(n) for k, peer in enumerate(peers)]
        arrivals = [_remote(ins[a], _slot(outs[a], peer), send_sems.at[a, k], recv_sems.at[a, k], peer)
                    for a in range(n) for k, peer in enumerate(peers)]
        return local, sends, arrivals

    def start(ins, outs, sems):
        local, sends, _ = copies(ins, outs, sems)
        for cp in local + sends:
            cp.start()

    def finish(ins, outs, sems):
        local, sends, arrivals = copies(ins, outs, sems)
        for cp in arrivals:
            cp.wait_recv()
        for cp in sends:
            cp.wait_send()
        for cp in local:
            cp.wait()

    dma = pltpu.SemaphoreType.DMA
    return _Exchange(arrs, [_sds((N_DEV,) + a.shape, a.dtype) for a in arrs], [dma((n, 4)), dma((n, 4)), dma((n,))],
                     start, finish)


def _gather_forward(gathered):
    n = len(gathered)

    def copies(ins, outs, sems):
        send_sems, recv_sems = sems
        x, y, c, chips = _place()
        sibling = (x, y, 1 - c)
        sends = [_remote(_slot(ins[a], (*chip, c)), _slot(outs[a], (*chip, c)), send_sems.at[a, j], recv_sems.at[a, j], sibling)
                 for a in range(n) for j, chip in enumerate(chips)]
        arrivals = [_remote(_slot(ins[a], (*chip, c)), _slot(outs[a], (*chip, 1 - c)), send_sems.at[a, j], recv_sems.at[a, j], sibling)
                    for a in range(n) for j, chip in enumerate(chips)]
        return sends, arrivals

    def start(ins, outs, sems):
        for cp in copies(ins, outs, sems)[0]:
            cp.start()

    def finish(ins, outs, sems):
        sends, arrivals = copies(ins, outs, sems)
        for cp in arrivals:
            cp.wait_recv()
        for cp in sends:
            cp.wait_send()

    dma = pltpu.SemaphoreType.DMA
    return _Exchange(gathered, [_sds(a.shape, a.dtype) for a in gathered], [dma((n, 3)), dma((n, 3))], start, finish,
                     aliases={a: a for a in range(n)})


def _gather_all(arrs, parts=GATHER_PARTS):
    n = len(arrs)
    split = [parts if a.shape[0] % (parts * BF16_ROWS) == 0 else 1 for a in arrs]
    members = [[a for a in range(n) if p < split[a]] for p in range(parts)]

    def part_of(a, p):
        size = arrs[a].shape[0] // split[a]
        return pl.ds(p * size, size), _sds((size,) + arrs[a].shape[1:], arrs[a].dtype)

    direct = [_gather_direct([part_of(a, p)[1] for a in members[p]]) for p in range(parts)]
    passed = [_gather_forward([_sds((N_DEV,) + part_of(a, p)[1].shape, arrs[a].dtype) for a in members[p]])
              for p in range(parts)]
    jobs = direct + passed
    sem_shapes = [s for job in jobs for s in job.sem_shapes]

    def views(ins, outs, sems):
        found, at = [], 0
        for i, job in enumerate(jobs):
            p = i % parts
            rows = [part_of(a, p)[0] for a in members[p]]
            found.append(([ins[a].at[r] for a, r in zip(members[p], rows)],
                          [outs[a].at[:, r] for a, r in zip(members[p], rows)],
                          sems[at:at + len(job.sem_shapes)]))
            at += len(job.sem_shapes)
        return found

    def start(ins, outs, sems):
        for job, (i, o, s) in list(zip(jobs, views(ins, outs, sems)))[:parts]:
            job.start(i, o, s)

    def early(ins, outs, sems):
        found = views(ins, outs, sems)
        for p in range(parts):
            i, o, s = found[p]
            direct[p].finish(i, o, s)
            _, o2, s2 = found[parts + p]
            passed[p].start(o2, o2, s2)

    def finish(ins, outs, sems):
        found = views(ins, outs, sems)
        for p in range(parts):
            _, o2, s2 = found[parts + p]
            passed[p].finish(o2, o2, s2)

    return _Exchange(arrs, [_sds((N_DEV,) + a.shape, a.dtype) for a in arrs], sem_shapes, start, finish, early=early)


def _pair_exchange(arrs):
    n = len(arrs)

    def copies(ins, outs, sems):
        x, y, c, _ = _place()
        return [_remote(ins[a].at[:, 1 - c], outs[a], sems[0].at[a], sems[1].at[a], (x, y, 1 - c)) for a in range(n)]

    def start(ins, outs, sems):
        for cp in copies(ins, outs, sems):
            cp.start()

    def finish(ins, outs, sems):
        for cp in copies(ins, outs, sems):
            cp.wait()

    dma = pltpu.SemaphoreType.DMA
    return _Exchange(arrs, [_sds((4,) + a.shape[2:], a.dtype) for a in arrs], [dma((n,)), dma((n,))], start, finish)


def _chip_exchange(arrs):
    n = len(arrs)

    def copies(ins, outs, sems):
        x, y, c, chips = _place()
        return [_remote(ins[a].at[1 + j], outs[a].at[j], sems[0].at[a, j], sems[1].at[a, j], (*chip, c))
                for a in range(n) for j, chip in enumerate(chips)]

    def start(ins, outs, sems):
        for cp in copies(ins, outs, sems):
            cp.start()

    def finish(ins, outs, sems):
        for cp in copies(ins, outs, sems):
            cp.wait()

    dma = pltpu.SemaphoreType.DMA
    return _Exchange(arrs, [_sds((3,) + a.shape[1:], a.dtype) for a in arrs], [dma((n, 3)), dma((n, 3))], start, finish)


def _gather_rows(vec, name, comm=()):
    def body(in_ref, out_ref, send_sems, recv_sems, local_sem):
        x, y, c, _ = _place()
        me = 4 * x + 2 * y + c
        mine = pltpu.make_async_copy(in_ref, out_ref.at[me], local_sem)
        mine.start()
        copies = []
        for k in range(1, N_DEV):
            to = (1 - x if k & 4 else x, 1 - y if k & 2 else y, 1 - c if k & 1 else c)
            copies.append(pltpu.make_async_remote_copy(
                src_ref=in_ref, dst_ref=out_ref.at[me], send_sem=send_sems.at[k - 1], recv_sem=recv_sems.at[k - 1],
                device_id=to, device_id_type=MESH))
        for cp in copies:
            cp.start()
        for cp in copies:
            cp.wait()
        mine.wait()

    return _call(
        body, name=name, out_shape=_sds((N_DEV,) + vec.shape, vec.dtype), in_specs=[ANY], out_specs=ANY,
        scratch_shapes=[pltpu.SemaphoreType.DMA((7,)), pltpu.SemaphoreType.DMA((7,)), pltpu.SemaphoreType.DMA],
    )(vec)


def _group_rows(tm):
    return SUB_ROWS if tm % SUB_ROWS == 0 else BF16_ROWS


def _for_row_groups(tm, fn):
    sub = _group_rows(tm)

    def step(i, carry):
        fn(pl.ds(pl.multiple_of(i * sub, sub), sub), i * sub)
        return carry

    lax.fori_loop(0, tm // sub, step, 0)


def _once(shape, index_map):
    return pl.BlockSpec(shape, index_map, pipeline_mode=pl.Buffered(1))


def _rms_fwd(hs, g, tm, name, comm=()):
    rows, d = hs.shape

    def body(hs_ref, g_ref, h_ref, r_ref):
        def group(rs, _):
            xv = hs_ref[rs, :]
            r = lax.rsqrt(jnp.mean(xv * xv, axis=-1, keepdims=True) + EPS)
            h_ref[rs, :] = (xv * r * g_ref[...]).astype(BF16)
            r_ref[rs, :] = r

        _for_row_groups(tm, group)

    return _call(
        body, comm=comm, name=name, grid=(rows // tm,),
        in_specs=[pl.BlockSpec((tm, d), lambda i: (i, 0)), pl.BlockSpec((1, d), lambda i: (0, 0))],
        out_specs=[pl.BlockSpec((tm, d), lambda i: (i, 0)), pl.BlockSpec((tm, 1), lambda i: (i, 0))],
        out_shape=[_sds((rows, d), BF16), _sds((rows, 1), F32)], compiler_params=_params(),
    )(hs, g)


def _ffn_gu(h, wg, wu, tm, name, comm=()):
    rows, d = h.shape
    nb, fb, _ = wg.shape

    def body(h_ref, wg_ref, wu_ref, g_ref, u_ref):
        hv = h_ref[...]
        g_ref[0] = lax.dot_general(hv, wg_ref[0], NT, preferred_element_type=F32).astype(BF16)
        u_ref[0] = lax.dot_general(hv, wu_ref[0], NT, preferred_element_type=F32).astype(BF16)

    wspec = pl.BlockSpec((1, fb, d), lambda j, i: (j, 0, 0))
    ospec = pl.BlockSpec((1, tm, fb), lambda j, i: (j, i, 0))
    return _call(
        body, comm=comm, name=name, grid=(nb, rows // tm),
        in_specs=[pl.BlockSpec((tm, d), lambda j, i: (i, 0)), wspec, wspec],
        out_specs=[ospec, ospec], out_shape=[_sds((nb, rows, fb), BF16)] * 2, compiler_params=_params(),
    )(h, wg, wu)


def _ffn_proj(h, w, tm, name, comm=()):
    rows, d = h.shape
    nb, fb, _ = w.shape

    def body(h_ref, w_ref, p_ref):
        p_ref[0] = lax.dot_general(h_ref[...], w_ref[0], NT, preferred_element_type=F32).astype(BF16)

    return _call(
        body, comm=comm, name=name, grid=(nb, rows // tm),
        in_specs=[pl.BlockSpec((tm, d), lambda j, i: (i, 0)), pl.BlockSpec((1, fb, d), lambda j, i: (j, 0, 0))],
        out_specs=pl.BlockSpec((1, tm, fb), lambda j, i: (j, i, 0)), out_shape=_sds((nb, rows, fb), BF16),
        compiler_params=_params(),
    )(h, w)


def _down_product(g_ref, u_ref, wd_ref, acc_ref):
    part = None
    for b in range(g_ref.shape[0]):
        gv = g_ref[b]
        a = gv * _sigmoid(gv) * u_ref[b]
        p = jnp.dot(a, wd_ref[b], preferred_element_type=F32)
        part = p if part is None else part + p
    acc_ref[...] = part


def _ffn_down_norm(g, u, wd, hs, gn, tm, name, comm=()):
    nb, rows, fb = g.shape
    d = hs.shape[1]

    def body(g_ref, u_ref, wd_ref, hs_ref, gn_ref, hsn_ref, hn_ref, rn_ref, acc_ref):
        _down_product(g_ref, u_ref, wd_ref, acc_ref)

        def group(rs, _):
            hsn = hs_ref[rs, :] + FFN_RES_SCALE * acc_ref[rs, :]
            r = lax.rsqrt(jnp.mean(hsn * hsn, axis=-1, keepdims=True) + EPS)
            hsn_ref[rs, :] = hsn
            hn_ref[rs, :] = (hsn * r * gn_ref[...]).astype(BF16)
            rn_ref[rs, :] = r

        _for_row_groups(tm, group)

    aspec = pl.BlockSpec((nb, tm, fb), lambda i: (0, i, 0))
    row = pl.BlockSpec((tm, d), lambda i: (i, 0))
    return _call(
        body, comm=comm, name=name, grid=(rows // tm,),
        in_specs=[aspec, aspec, _once((nb, fb, d), lambda i: (0, 0, 0)), row, pl.BlockSpec((1, d), lambda i: (0, 0))],
        out_specs=[row, row, pl.BlockSpec((tm, 1), lambda i: (i, 0))],
        out_shape=[_sds((rows, d), F32), _sds((rows, d), BF16), _sds((rows, 1), F32)],
        scratch_shapes=[pltpu.VMEM((tm, d), F32)], compiler_params=_params(),
    )(g, u, wd, hs, gn)


def _ffn_down_loss(g, u, wd, hs, gf, tgt, n_seq, tm, name, comm=()):
    nb, rows, fb = g.shape
    d = hs.shape[1]
    nt = rows // tm

    def body(g_ref, u_ref, wd_ref, hs_ref, gf_ref, tgt_ref, dhs_ref, dhsb_ref, loss_ref, dgf_ref, acc_ref):
        i = pl.program_id(0)
        _down_product(g_ref, u_ref, wd_ref, acc_ref)
        loss_ref[0] = jnp.zeros((1, d), F32)
        dgf_ref[0] = jnp.zeros((1, d), F32)

        def group(rs, r0):
            hs3 = hs_ref[rs, :] + FFN_RES_SCALE * acc_ref[rs, :]
            r = lax.rsqrt(jnp.mean(hs3 * hs3, axis=-1, keepdims=True) + EPS)
            gfv = gf_ref[...]
            y = hs3 * r
            rowid = i * tm + r0 + lax.broadcasted_iota(jnp.int32, (_group_rows(tm), 1), 0)
            err = jnp.where(rowid < n_seq, y * gfv - tgt_ref[rs, :], 0.0)
            loss_ref[0] += jnp.sum(err * err, axis=0, keepdims=True)
            dout = err * (1.0 / d)
            dgf_ref[0] += jnp.sum(dout * y, axis=0, keepdims=True)
            t = dout * gfv
            dhs = r * t - hs3 * (r * r * r) * jnp.mean(t * hs3, axis=-1, keepdims=True)
            dhs_ref[rs, :] = dhs
            dhsb_ref[rs, :] = dhs.astype(BF16)

        _for_row_groups(tm, group)

    aspec = pl.BlockSpec((nb, tm, fb), lambda i: (0, i, 0))
    row = pl.BlockSpec((tm, d), lambda i: (i, 0))
    part = pl.BlockSpec((1, 1, d), lambda i: (i, 0, 0))
    return _call(
        body, comm=comm, name=name, grid=(nt,),
        in_specs=[aspec, aspec, _once((nb, fb, d), lambda i: (0, 0, 0)), row, pl.BlockSpec((1, d), lambda i: (0, 0)), row],
        out_specs=[row, row, part, part],
        out_shape=[_sds((rows, d), F32), _sds((rows, d), BF16), _sds((nt, 1, d), F32), _sds((nt, 1, d), F32)],
        scratch_shapes=[pltpu.VMEM((tm, d), F32)], compiler_params=_params(),
    )(g, u, wd, hs, gf, tgt)


def _ffn_bwd_da(do, wd, g, u, tm, name, comm=()):
    nb, rows, fb = g.shape
    d = do.shape[1]

    def body(do_ref, wd_ref, g_ref, u_ref, dg_ref, du_ref):
        rs = pl.ds(pl.multiple_of(pl.program_id(1) * tm, tm), tm)
        da = (FFN_RES_SCALE * lax.dot_general(do_ref[rs, :], wd_ref[0], NT, preferred_element_type=F32)).astype(BF16)
        gv, uv = g_ref[0], u_ref[0]
        s = _sigmoid(gv)
        sg = gv * s
        du_ref[0] = da * sg
        dg_ref[0] = da * uv * (s + sg * (1.0 - s))

    aspec = pl.BlockSpec((1, tm, fb), lambda j, i: (j, i, 0))
    return _call(
        body, comm=comm, name=name, grid=(nb, rows // tm),
        in_specs=[_once((rows, d), lambda j, i: (0, 0)), pl.BlockSpec((1, fb, d), lambda j, i: (j, 0, 0)), aspec, aspec],
        out_specs=[aspec] * 2, out_shape=[_sds((nb, rows, fb), BF16)] * 2, compiler_params=_params(),
    )(do, wd, g, u)


def _rms_bwd(tm, d, dh_ref, hs_ref, r_ref, gn_ref, dres_ref, dhs_ref, dhsb_ref, dgn_ref):
    dgn_ref[0] = jnp.zeros((1, d), F32)

    def group(rs, _):
        dh, hs, r = dh_ref[rs, :], hs_ref[rs, :], r_ref[rs, :]
        dgn_ref[0] += jnp.sum(dh * (hs * r), axis=0, keepdims=True)
        t = dh * gn_ref[...]
        dhs = dres_ref[rs, :] + r * t - hs * (r * r * r) * jnp.mean(t * hs, axis=-1, keepdims=True)
        dhs_ref[rs, :] = dhs
        dhsb_ref[rs, :] = dhs.astype(BF16)

    _for_row_groups(tm, group)


def _blocks_dot(x_ref, w_ref):
    part = None
    for b in range(x_ref.shape[0]):
        p = jnp.dot(x_ref[b], w_ref[b], preferred_element_type=F32)
        part = p if part is None else part + p
    return part


def _ffn_bwd_dh_gate(dg, wg, tm, name, comm=()):
    nb, rows, fb = dg.shape
    d = wg.shape[2]

    def body(dg_ref, wg_ref, o_ref):
        o_ref[...] = _blocks_dot(dg_ref, wg_ref)

    return _call(
        body, comm=comm, name=name, grid=(rows // tm,),
        in_specs=[pl.BlockSpec((nb, tm, fb), lambda i: (0, i, 0)), _once((nb, fb, d), lambda i: (0, 0, 0))],
        out_specs=pl.BlockSpec((tm, d), lambda i: (i, 0)), out_shape=_sds((rows, d), F32), compiler_params=_params(),
    )(dg, wg)


def _ffn_bwd_dh(du, wu, half, hs, r, gn, dres, tm, name, comm=(), zeros=None):
    nb, rows, fb = du.shape
    d = hs.shape[1]
    nt = rows // tm
    extra = [] if zeros is None else [zeros]

    def body(du_ref, wu_ref, half_ref, hs_ref, r_ref, gn_ref, dres_ref, *rest):
        dhs_ref, dhsb_ref, dgn_ref, acc_ref = rest[len(extra):]
        acc_ref[...] = half_ref[...] + _blocks_dot(du_ref, wu_ref)
        _rms_bwd(tm, d, acc_ref, hs_ref, r_ref, gn_ref, dres_ref, dhs_ref, dhsb_ref, dgn_ref)

    row = pl.BlockSpec((tm, d), lambda i: (i, 0))
    return _call(
        body, comm=comm, name=name, grid=(nt,),
        in_specs=[pl.BlockSpec((nb, tm, fb), lambda i: (0, i, 0)), _once((nb, fb, d), lambda i: (0, 0, 0)), row, row,
                  pl.BlockSpec((tm, 1), lambda i: (i, 0)), pl.BlockSpec((1, d), lambda i: (0, 0)), row] + [ANY] * len(extra),
        out_specs=[row, row, pl.BlockSpec((1, 1, d), lambda i: (i, 0, 0))],
        out_shape=[_sds((rows, d) if zeros is None else zeros.shape, F32), _sds((rows, d), BF16), _sds((nt, 1, d), F32)],
        scratch_shapes=[pltpu.VMEM((tm, d), F32)], compiler_params=_params(),
        input_output_aliases={7: 0} if extra else {},
    )(du, wu, half, hs, r, gn, dres, *extra)


class _KSteps:
    def __init__(self, rows, tk):
        self.main, self.rem, self.tk = rows // tk, rows % tk, tk
        self.n = self.main + (1 if self.rem else 0)
        if self.rem:
            assert rows % self.rem == 0, (rows, tk)
            self.last = rows // self.rem - 1

    def specs(self, makers):
        main_at = (lambda k: jnp.minimum(k, self.main - 1)) if self.rem else (lambda k: k)
        out = [make(self.tk, main_at) for make in makers]
        if self.rem:
            out += [make(self.rem, lambda k: self.last) for make in makers]
        return out

    def args(self, arrs):
        return list(arrs) * (2 if self.rem else 1)

    def each(self, k, refs, fn):
        if not self.rem:
            fn(*refs)
            return
        n = len(refs) // 2
        pl.when(k < self.main)(lambda: fn(*refs[:n]))
        pl.when(k == self.main)(lambda: fn(*refs[n:]))


def _ffn_bwd_dw_gu(h, dg, du, tk, name, comm=()):
    nb, rows, fb = dg.shape
    d = h.shape[1]
    steps = _KSteps(rows, tk)

    def body(*refs):
        ins, (dwg_ref, dwu_ref, accg, accu) = refs[:-4], refs[-4:]
        k = pl.program_id(1)

        @pl.when(k == 0)
        def _():
            accg[...] = jnp.zeros_like(accg)
            accu[...] = jnp.zeros_like(accu)

        def add(h_ref, dg_ref, du_ref):
            hv = h_ref[...]
            accg[...] += lax.dot_general(dg_ref[0], hv, TN, preferred_element_type=F32)
            accu[...] += lax.dot_general(du_ref[0], hv, TN, preferred_element_type=F32)

        steps.each(k, ins, add)

        @pl.when(k == steps.n - 1)
        def _():
            dwg_ref[0] = accg[...].astype(BF16)
            dwu_ref[0] = accu[...].astype(BF16)

    wspec = pl.BlockSpec((1, fb, d), lambda j, k: (j, 0, 0))
    specs = steps.specs([lambda t, at: pl.BlockSpec((t, d), lambda j, k: (at(k), 0)),
                         lambda t, at: pl.BlockSpec((1, t, fb), lambda j, k: (j, at(k), 0)),
                         lambda t, at: pl.BlockSpec((1, t, fb), lambda j, k: (j, at(k), 0))])
    return _call(
        body, comm=comm, name=name, grid=(nb, steps.n),
        in_specs=specs, out_specs=[wspec] * 2,
        out_shape=[_sds((nb, fb, d), BF16)] * 2, scratch_shapes=[pltpu.VMEM((fb, d), F32)] * 2,
        compiler_params=_params(),
    )(*steps.args([h, dg, du]))


def _ffn_bwd_dw_down(g, u, do, tk, name, comm=()):
    nb, rows, fb = g.shape
    d = do.shape[1]
    steps = _KSteps(rows, tk)

    def body(*refs):
        ins, (dwd_ref, acc) = refs[:-2], refs[-2:]
        k = pl.program_id(1)

        @pl.when(k == 0)
        def _():
            acc[...] = jnp.zeros_like(acc)

        def add(g_ref, u_ref, do_ref):
            gv = g_ref[0]
            a = gv * _sigmoid(gv) * u_ref[0]
            acc[...] += lax.dot_general(a, do_ref[...], TN, preferred_element_type=F32)

        steps.each(k, ins, add)

        @pl.when(k == steps.n - 1)
        def _():
            dwd_ref[0] = (FFN_RES_SCALE * acc[...]).astype(BF16)

    specs = steps.specs([lambda t, at: pl.BlockSpec((1, t, fb), lambda j, k: (j, at(k), 0)),
                         lambda t, at: pl.BlockSpec((1, t, fb), lambda j, k: (j, at(k), 0)),
                         lambda t, at: pl.BlockSpec((t, d), lambda j, k: (at(k), 0))])
    return _call(
        body, comm=comm, name=name, grid=(nb, steps.n), in_specs=specs,
        out_specs=pl.BlockSpec((1, fb, d), lambda j, k: (j, 0, 0)), out_shape=_sds((nb, fb, d), BF16),
        scratch_shapes=[pltpu.VMEM((fb, d), F32)], compiler_params=_params(),
    )(*steps.args([g, u, do]))


def _win_fwd(h, w, b, dc, tm, name, comm=()):
    rows, d = h.shape
    ng = w.shape[1] // dc

    def body(h_ref, w_ref, b_ref, u_ref):
        u_ref[...] = (jnp.dot(h_ref[...], w_ref[...], preferred_element_type=F32) + b_ref[...]).astype(BF16)

    return _call(
        body, comm=comm, name=name, grid=(ng, rows // tm),
        in_specs=[pl.BlockSpec((tm, d), lambda m, i: (i, 0)), pl.BlockSpec((d, dc), lambda m, i: (0, m)),
                  pl.BlockSpec((1, dc), lambda m, i: (0, m))],
        out_specs=pl.BlockSpec((tm, dc), lambda m, i: (i, m)), out_shape=_sds((rows, ng * dc), BF16),
        compiler_params=_params(),
    )(h, w, b)


def _win_bwd_dh(du, w, hs, r, gn, dres, tm, name, comm=()):
    rows, d = hs.shape
    ng, _, dc = du.shape
    nt = rows // tm

    def body(du_ref, w_ref, hs_ref, r_ref, gn_ref, dres_ref, dhs_ref, dhsb_ref, dgn_ref, acc_ref):
        part = None
        for m in range(ng):
            p = lax.dot_general(du_ref[m], w_ref[:, m * dc:(m + 1) * dc], NT, preferred_element_type=F32)
            part = p if part is None else part + p
        acc_ref[...] = part
        _rms_bwd(tm, d, acc_ref, hs_ref, r_ref, gn_ref, dres_ref, dhs_ref, dhsb_ref, dgn_ref)

    row = pl.BlockSpec((tm, d), lambda i: (i, 0))
    return _call(
        body, comm=comm, name=name, grid=(nt,),
        in_specs=[pl.BlockSpec((ng, tm, dc), lambda i: (0, i, 0)), _once((d, ng * dc), lambda i: (0, 0)), row,
                  pl.BlockSpec((tm, 1), lambda i: (i, 0)), pl.BlockSpec((1, d), lambda i: (0, 0)), row],
        out_specs=[row, row, pl.BlockSpec((1, 1, d), lambda i: (i, 0, 0))],
        out_shape=[_sds((rows, d), F32), _sds((rows, d), BF16), _sds((nt, 1, d), F32)],
        scratch_shapes=[pltpu.VMEM((tm, d), F32)], compiler_params=_params(),
    )(du, w, hs, r, gn, dres)


def _win_bwd_dw(h, du, tk, name, comm=()):
    rows, d = h.shape
    ng, _, dc = du.shape
    steps = _KSteps(rows, tk)

    def body(*refs):
        ins, (dw_ref, db_ref, acc, accb) = refs[:-4], refs[-4:]
        k = pl.program_id(1)

        @pl.when(k == 0)
        def _():
            acc[...] = jnp.zeros_like(acc)
            accb[...] = jnp.zeros_like(accb)

        def add(h_ref, du_ref):
            duv = du_ref[0]
            acc[...] += lax.dot_general(h_ref[...], duv, TN, preferred_element_type=F32)
            accb[...] += jnp.sum(duv.astype(F32), axis=0, keepdims=True)

        steps.each(k, ins, add)

        @pl.when(k == steps.n - 1)
        def _():
            dw_ref[...] = acc[...].astype(BF16)
            db_ref[...] = accb[...]

    specs = steps.specs([lambda t, at: pl.BlockSpec((t, d), lambda m, k: (at(k), 0)),
                         lambda t, at: pl.BlockSpec((1, t, dc), lambda m, k: (m, at(k), 0))])
    return _call(
        body, comm=comm, name=name, grid=(ng, steps.n), in_specs=specs,
        out_specs=[pl.BlockSpec((d, dc), lambda m, k: (0, m)), pl.BlockSpec((1, dc), lambda m, k: (0, m))],
        out_shape=[_sds((d, ng * dc), BF16), _sds((1, ng * dc), F32)],
        scratch_shapes=[pltpu.VMEM((d, dc), F32), pltpu.VMEM((1, dc), F32)], compiler_params=_params(),
    )(*steps.args([h, du]))


def _layernorm_silu(zc, lg, lb):
    mu = jnp.mean(zc, axis=-1, keepdims=True)
    xc = zc - mu
    rstd = lax.rsqrt(jnp.mean(xc * xc, axis=-1, keepdims=True) + EPS)
    nrm = xc * rstd
    lin = nrm * lg + lb
    s = _sigmoid(lin)
    return nrm, rstd, lin, s


def _wout_fwd(zc, ysc, wout, hs, lg, lb, gn, rows, tm, name, comm=()):
    dc = zc.shape[1]
    d = hs.shape[1]

    def body(zc_ref, ysc_ref, w_ref, hs_ref, lg_ref, lb_ref, gn_ref, y_ref, hsn_ref, hn_ref, rn_ref):
        def mix(rs, _):
            _, _, lin, s = _layernorm_silu(zc_ref[rs, :], lg_ref[...], lb_ref[...])
            y_ref[rs, :dc] = ysc_ref[rs, :]
            y_ref[rs, dc:] = (lin * s).astype(BF16)

        _for_row_groups(tm, mix)
        hsn_ref[...] = jnp.dot(y_ref[...], w_ref[...], preferred_element_type=F32)

        def norm(rs, _):
            hsn = hs_ref[rs, :] + hsn_ref[rs, :]
            r = lax.rsqrt(jnp.mean(hsn * hsn, axis=-1, keepdims=True) + EPS)
            hsn_ref[rs, :] = hsn
            hn_ref[rs, :] = (hsn * r * gn_ref[...]).astype(BF16)
            rn_ref[rs, :] = r

        _for_row_groups(tm, norm)

    half = pl.BlockSpec((tm, dc), lambda i: (i, 0))
    row = pl.BlockSpec((tm, d), lambda i: (i, 0))
    vec_c = pl.BlockSpec((1, dc), lambda i: (0, 0))
    return _call(
        body, comm=comm, name=name, grid=(rows // tm,),
        in_specs=[half, half, _once((2 * dc, d), lambda i: (0, 0)), row, vec_c, vec_c,
                  pl.BlockSpec((1, d), lambda i: (0, 0))],
        out_specs=[pl.BlockSpec((tm, 2 * dc), lambda i: (i, 0)), row, row, pl.BlockSpec((tm, 1), lambda i: (i, 0))],
        out_shape=[_sds((rows, 2 * dc), BF16), _sds((rows, d), F32), _sds((rows, d), BF16), _sds((rows, 1), F32)],
        compiler_params=_params(),
    )(zc, ysc, wout, hs, lg, lb, gn)


def _wout_bwd_dy(do, wout, zc, lg, lb, tm, name, comm=()):
    rows, d = do.shape
    dc = zc.shape[1]
    nt = rows // tm

    def body(do_ref, w_ref, zc_ref, lg_ref, lb_ref, dysc_ref, dzc_ref, dlg_ref, dlb_ref, dy_ref):
        dy_ref[...] = lax.dot_general(do_ref[...], w_ref[...], NT, preferred_element_type=F32)
        dlb_ref[0] = jnp.zeros((1, dc), F32)
        dlg_ref[0] = jnp.zeros((1, dc), F32)

        def group(rs, _):
            dysc_ref[rs, :] = dy_ref[rs, :dc].astype(BF16)
            nrm, rstd, lin, s = _layernorm_silu(zc_ref[rs, :], lg_ref[...], lb_ref[...])
            dl = dy_ref[rs, dc:] * (s * (1.0 + lin * (1.0 - s)))
            dlb_ref[0] += jnp.sum(dl, axis=0, keepdims=True)
            dlg_ref[0] += jnp.sum(dl * nrm, axis=0, keepdims=True)
            dn = dl * lg_ref[...]
            dzc_ref[rs, :] = rstd * (dn - jnp.mean(dn, axis=-1, keepdims=True)
                                     - nrm * jnp.mean(dn * nrm, axis=-1, keepdims=True))

        _for_row_groups(tm, group)

    half = pl.BlockSpec((tm, dc), lambda i: (i, 0))
    vec_c = pl.BlockSpec((1, dc), lambda i: (0, 0))
    part = pl.BlockSpec((1, 1, dc), lambda i: (i, 0, 0))
    return _call(
        body, comm=comm, name=name, grid=(nt,),
        in_specs=[pl.BlockSpec((tm, d), lambda i: (i, 0)), _once((2 * dc, d), lambda i: (0, 0)), half, vec_c, vec_c],
        out_specs=[half, half, part, part],
        out_shape=[_sds((rows, dc), BF16), _sds((rows, dc), F32), _sds((nt, 1, dc), F32), _sds((nt, 1, dc), F32)],
        scratch_shapes=[pltpu.VMEM((tm, 2 * dc), F32)], compiler_params=_params(),
    )(do, wout, zc, lg, lb)


def _wout_bwd_dw(y, do, nb, tk, name, comm=()):
    rows, k2 = y.shape
    d = do.shape[1]
    nk = rows // tk

    def body(y_ref, do_ref, dw_ref, acc):
        k = pl.program_id(0)

        @pl.when(k == 0)
        def _():
            acc[...] = jnp.zeros_like(acc)

        acc[...] += lax.dot_general(y_ref[...], do_ref[...], TN, preferred_element_type=F32)

        @pl.when(k == nk - 1)
        def _():
            dw_ref[...] = acc[...].astype(BF16)

    return _call(
        body, comm=comm, name=name, grid=(nk,),
        in_specs=[pl.BlockSpec((tk, k2), lambda k: (k, 0)), pl.BlockSpec((tk, d), lambda k: (k, 0))],
        out_specs=_once((k2, d), lambda k: (0, 0)), out_shape=_sds((k2, d), BF16),
        scratch_shapes=[pltpu.VMEM((k2, d), F32)], compiler_params=_params(),
    )(y, do).reshape(nb, k2 // nb, d)


def _windows(win, n_res):
    length = win.shape[0]
    return [win if r == 0 else pltpu.roll(win, length - r, 0) for r in range(n_res)]


def _taps(src_ref, start, offsets, ch):
    span = -(-(max(offsets) + ch) // SUBLANES) * SUBLANES
    win = src_ref[pl.ds(start, span), :]
    shifted = _windows(win, min(SUBLANES, max(offsets) + 1))
    return [shifted[o % SUBLANES][(o // SUBLANES) * SUBLANES:(o // SUBLANES) * SUBLANES + ch] for o in offsets]


def _rows8(v):
    return jnp.sum(v.reshape(v.shape[0] // SUBLANES, SUBLANES, v.shape[1]), axis=0)


def _conv_geometry(n_seq, n_meta):
    base = CONV_PAD + n_meta
    off_cf = base - (CF_WIDTH - 1)
    off_sc = base - (SC_WIDTH - 1)
    logical = -(-(n_meta + n_seq) // CONV_CH) * CONV_CH
    return base, off_cf, off_sc, logical


def _fill_conv_inputs(c_ref, v_ref, a_ref, g_ref, scv, sz, n_seq, n_meta):
    base = CONV_PAD + n_meta
    cb = scv.shape[1]
    scv[0:CONV_PAD, :] = jnp.zeros((CONV_PAD, cb), F32)
    sz[0:CONV_PAD, :] = jnp.zeros((CONV_PAD, cb), F32)

    def put(src, dst, n):
        cv = c_ref[src, :].astype(F32) * v_ref[src, :].astype(F32)
        scv[dst, :] = cv
        sz[dst, :] = a_ref[src, :].astype(F32) * _sigmoid(g_ref[src, :].astype(F32))

    put(pl.ds(n_seq, n_meta), pl.ds(CONV_PAD, n_meta), n_meta)

    def chunk(i, carry):
        t0 = pl.multiple_of(i * CONV_CH, CONV_CH)
        put(pl.ds(t0, CONV_CH), pl.ds(base + t0, CONV_CH), CONV_CH)
        return carry

    lax.fori_loop(0, n_seq // CONV_CH, chunk, 0)


def _conv_fwd(u, wsc, wcf, cbias, n_seq, n_meta, name, comm=()):
    rows = u.shape[0]
    nb, _, cb = wsc.shape
    dc = nb * cb
    base, off_cf, off_sc, _ = _conv_geometry(n_seq, n_meta)
    a_cf, a_sc = off_cf // SUBLANES * SUBLANES, off_sc // SUBLANES * SUBLANES
    ch = CONV_CH

    def body(b_ref, c_ref, v_ref, a_ref, g_ref, wsc_ref, wcf_ref, cb_ref, ysc_ref, zc_ref, scv, sz):
        _fill_conv_inputs(c_ref, v_ref, a_ref, g_ref, scv, sz, n_seq, n_meta)
        w3, w31, bias = wsc_ref[0], wcf_ref[0], cb_ref[...]

        def chunk(i, carry):
            t0 = pl.multiple_of(i * ch, ch)
            acc = jnp.zeros((ch, cb), F32)
            for k, win in enumerate(_taps(sz, t0 + a_cf, [off_cf - a_cf + k for k in range(CF_WIDTH)], ch)):
                acc = acc + win * w31[k:k + 1, :]
            zc_ref[pl.ds(t0, ch), :] = acc + bias
            s = jnp.zeros((ch, cb), F32)
            for k, win in enumerate(_taps(scv, t0 + a_sc, [off_sc - a_sc + k for k in range(SC_WIDTH)], ch)):
                s = s + win * w3[k:k + 1, :]
            ysc_ref[pl.ds(t0, ch), :] = (b_ref[pl.ds(t0, ch), :].astype(F32) * s).astype(BF16)
            return carry

        lax.fori_loop(0, n_seq // ch, chunk, 0)
        ysc_ref[n_seq:rows, :] = jnp.zeros((rows - n_seq, cb), BF16)
        zc_ref[n_seq:rows, :] = jnp.zeros((rows - n_seq, cb), F32)

    ucol = [pl.BlockSpec((rows, cb), functools.partial(lambda m, j: (0, m * nb + j), m)) for m in range(5)]
    blk = pl.BlockSpec((rows, cb), lambda j: (0, j))
    return _call(
        body, comm=comm, name=name, grid=(nb,),
        in_specs=ucol + [pl.BlockSpec((1, SC_WIDTH, cb), lambda j: (j, 0, 0)),
                         pl.BlockSpec((1, CF_WIDTH, cb), lambda j: (j, 0, 0)), pl.BlockSpec((1, cb), lambda j: (0, j))],
        out_specs=[blk, blk], out_shape=[_sds((rows, dc), BF16), _sds((rows, dc), F32)],
        scratch_shapes=[pltpu.VMEM((base + n_seq, cb), F32)] * 2, compiler_params=_params(),
    )(u, u, u, u, u, wsc, wcf, cbias)


def _conv_bwd(u, dysc, dzc, wsc, wcf, n_seq, n_meta, name, comm=()):
    rows = u.shape[0]
    nb, _, cb = wsc.shape
    dc = nb * cb
    base, off_cf, off_sc, logical = _conv_geometry(n_seq, n_meta)
    a_cf, a_sc = off_cf // SUBLANES * SUBLANES, off_sc // SUBLANES * SUBLANES
    ch = CONV_CH
    tail = CONV_PAD

    def body(b_ref, c_ref, v_ref, a_ref, g_ref, dysc_ref, dzc_ref, wsc_ref, wcf_ref,
             du_ref, dwsc_ref, dwcf_ref, dcb_ref,
             scv, sz, sds_, sdz, dlcv, dlz, accsc, acccf, accb):
        _fill_conv_inputs(c_ref, v_ref, a_ref, g_ref, scv, sz, n_seq, n_meta)
        w3, w31 = wsc_ref[0], wcf_ref[0]
        dub_ref, duc_ref, duv_ref, dua_ref, dug_ref = (du_ref.at[m] for m in range(5))
        sds_[0:n_meta, :] = jnp.zeros((n_meta, cb), F32)
        sdz[0:n_meta, :] = jnp.zeros((n_meta, cb), F32)
        behind = logical + tail - (n_meta + n_seq)
        sds_[n_meta + n_seq:logical + tail, :] = jnp.zeros((behind, cb), F32)
        sdz[n_meta + n_seq:logical + tail, :] = jnp.zeros((behind, cb), F32)
        accsc[...] = jnp.zeros_like(accsc)
        acccf[...] = jnp.zeros_like(acccf)
        accb[...] = jnp.zeros_like(accb)

        def forward_chunk(i, carry):
            t0 = pl.multiple_of(i * ch, ch)
            rws = pl.ds(t0, ch)
            dy = dysc_ref[rws, :].astype(F32)
            ds = dy * b_ref[rws, :].astype(F32)
            dz = dzc_ref[rws, :]
            sds_[pl.ds(n_meta + t0, ch), :] = ds
            sdz[pl.ds(n_meta + t0, ch), :] = dz
            s = jnp.zeros((ch, cb), F32)
            for k, win in enumerate(_taps(scv, t0 + a_sc, [off_sc - a_sc + k for k in range(SC_WIDTH)], ch)):
                s = s + win * w3[k:k + 1, :]
                accsc[k * SUBLANES:(k + 1) * SUBLANES, :] += _rows8(ds * win)
            dub_ref[rws, :] = (dy * s).astype(BF16)
            for k, win in enumerate(_taps(sz, t0 + a_cf, [off_cf - a_cf + k for k in range(CF_WIDTH)], ch)):
                acccf[k * SUBLANES:(k + 1) * SUBLANES, :] += _rows8(dz * win)
            accb[...] += _rows8(dz)
            return carry

        lax.fori_loop(0, n_seq // ch, forward_chunk, 0)

        def backward_chunk(i, carry):
            p0 = pl.multiple_of(i * ch, ch)
            dcv = jnp.zeros((ch, cb), F32)
            for k, win in enumerate(_taps(sds_, p0, [SC_WIDTH - 1 - k for k in range(SC_WIDTH)], ch)):
                dcv = dcv + win * w3[k:k + 1, :]
            dlcv[pl.ds(p0, ch), :] = dcv
            dzi = jnp.zeros((ch, cb), F32)
            for k, win in enumerate(_taps(sdz, p0, [CF_WIDTH - 1 - k for k in range(CF_WIDTH)], ch)):
                dzi = dzi + win * w31[k:k + 1, :]
            dlz[pl.ds(p0, ch), :] = dzi
            return carry

        lax.fori_loop(0, logical // ch, backward_chunk, 0)

        def gates(phys, logi):
            dcv, dzi = dlcv[logi, :], dlz[logi, :]
            duc_ref[phys, :] = (dcv * v_ref[phys, :].astype(F32)).astype(BF16)
            duv_ref[phys, :] = (dcv * c_ref[phys, :].astype(F32)).astype(BF16)
            s = _sigmoid(g_ref[phys, :].astype(F32))
            dua_ref[phys, :] = (dzi * s).astype(BF16)
            dug_ref[phys, :] = (dzi * a_ref[phys, :].astype(F32) * s * (1.0 - s)).astype(BF16)

        def gate_chunk(i, carry):
            t0 = pl.multiple_of(i * ch, ch)
            gates(pl.ds(t0, ch), pl.ds(n_meta + t0, ch))
            return carry

        lax.fori_loop(0, n_seq // ch, gate_chunk, 0)
        gates(pl.ds(n_seq, n_meta), pl.ds(0, n_meta))
        dub_ref[n_seq:rows, :] = jnp.zeros((rows - n_seq, cb), BF16)
        pad0 = n_seq + n_meta
        for ref in (duc_ref, duv_ref, dua_ref, dug_ref):
            ref[pad0:rows, :] = jnp.zeros((rows - pad0, cb), BF16)
        dwsc_ref[0] = jnp.sum(accsc[...].reshape(SC_WIDTH, SUBLANES, cb), axis=1)
        dwcf_ref[0] = jnp.sum(acccf[...].reshape(CF_WIDTH, SUBLANES, cb), axis=1)
        dcb_ref[...] = jnp.sum(accb[...], axis=0, keepdims=True)

    ucol = [pl.BlockSpec((rows, cb), functools.partial(lambda m, j: (0, m * nb + j), m)) for m in range(5)]
    blk = pl.BlockSpec((dysc.shape[0], cb), lambda j: (0, j))
    wsc_spec = pl.BlockSpec((1, SC_WIDTH, cb), lambda j: (j, 0, 0))
    wcf_spec = pl.BlockSpec((1, CF_WIDTH, cb), lambda j: (j, 0, 0))
    outs = _call(
        body, comm=comm, name=name, grid=(nb,),
        in_specs=ucol + [blk, blk, wsc_spec, wcf_spec],
        out_specs=[pl.BlockSpec((5, rows, cb), lambda j: (0, 0, j)), wsc_spec, wcf_spec,
                   pl.BlockSpec((1, cb), lambda j: (0, j))],
        out_shape=[_sds((5, rows, dc), BF16), _sds((nb, SC_WIDTH, cb), F32), _sds((nb, CF_WIDTH, cb), F32),
                   _sds((1, dc), F32)],
        scratch_shapes=[pltpu.VMEM((base + n_seq, cb), F32)] * 2 + [pltpu.VMEM((logical + tail, cb), F32)] * 2
        + [pltpu.VMEM((logical, cb), F32)] * 2
        + [pltpu.VMEM((SC_WIDTH * SUBLANES, cb), F32), pltpu.VMEM((CF_WIDTH * SUBLANES, cb), F32),
           pltpu.VMEM((SUBLANES, cb), F32)],
        compiler_params=_params(),
    )(u, u, u, u, u, dysc, dzc, wsc, wcf)
    return outs


def _row_tile(rows, cols):
    return rows // 4 if rows % 64 == 0 and rows * cols >= (1 << 18) else rows


def _pair_sum(grad, sib, idx, name, comm=()):
    _, rows, cols = grad.shape
    tr = rows

    def body(idx_ref, g_ref, s_ref, o_ref):
        o_ref[0] = (g_ref[0].astype(F32) + s_ref[0].astype(F32)).astype(o_ref.dtype)

    return _call(
        body, name=name,
        grid_spec=pltpu.PrefetchScalarGridSpec(
            num_scalar_prefetch=1, grid=(4, rows // tr),
            in_specs=[pl.BlockSpec((1, tr, cols), lambda k, i, idx_ref: (idx_ref[k], i, 0)),
                      pl.BlockSpec((1, tr, cols), lambda k, i, idx_ref: (idx_ref[4 + k], i, 0))],
            out_specs=pl.BlockSpec((1, tr, cols), lambda k, i, idx_ref: (k, i, 0))),
        out_shape=_sds((4, rows, cols), grad.dtype), compiler_params=_params(),
    )(idx, grad, sib)


def _adamw_math(w, g, m, v):
    m = ADAM_B1 * m + (1.0 - ADAM_B1) * g
    v = ADAM_B2 * v + (1.0 - ADAM_B2) * (g * g)
    m_hat = m / (1.0 - ADAM_B1 ** ADAM_STEP)
    v_hat = v / (1.0 - ADAM_B2 ** ADAM_STEP)
    delta = -ADAM_LR * (m_hat / (jnp.sqrt(v_hat) + ADAM_EPS) + ADAM_WD * w)
    return delta, m, v


def _adamw_sharded(own, got, w, m, v, name, comm=()):
    rows, cols = w.shape
    tr = _row_tile(rows, cols)

    def body(own_ref, g0_ref, g1_ref, g2_ref, w_ref, m_ref, v_ref, g_ref, d_ref, nm_ref, nv_ref):
        g = own_ref[0].astype(F32) + g0_ref[0].astype(F32) + g1_ref[0].astype(F32) + g2_ref[0].astype(F32)
        delta, nm, nv = _adamw_math(w_ref[...], g, m_ref[...], v_ref[...])
        g_ref[...] = g
        d_ref[...] = delta
        nm_ref[...] = nm
        nv_ref[...] = nv

    flat = pl.BlockSpec((tr, cols), lambda i: (i, 0))
    slot = [pl.BlockSpec((1, tr, cols), functools.partial(lambda k, i: (k, i, 0), k)) for k in range(3)]
    return _call(
        body, comm=comm, name=name, grid=(rows // tr,),
        in_specs=[slot[0]] + slot + [flat] * 3, out_specs=[flat] * 4, out_shape=[_sds((rows, cols), F32)] * 4,
        compiler_params=_params(),
    )(own, got, got, got, w, m, v)


def _adamw_replicated(gathered, segs, ws, ms, vs, loss_scale, name, comm=()):
    n = len(ws)

    def body(*refs):
        gat = refs[0]
        w_refs, m_refs, v_refs = refs[1:1 + n], refs[1 + n:1 + 2 * n], refs[1 + 2 * n:1 + 3 * n]
        outs = refs[1 + 3 * n:]

        def total(off, width):
            s = gat[0, :, off:off + width]
            for k in range(1, N_DEV):
                s = s + gat[k, :, off:off + width]
            return s

        outs[0][...] = loss_scale * total(segs[n][0], segs[n][1])
        for p in range(n):
            g = total(*segs[p])
            delta, nm, nv = _adamw_math(w_refs[p][...], g, m_refs[p][...], v_refs[p][...])
            for q, val in enumerate((g, delta, nm, nv)):
                outs[1 + 4 * p + q][...] = val
        for e, seg in enumerate(segs[n + 1:]):
            outs[1 + 4 * n + e][...] = total(*seg)

    return _call(
        body, name=name,
        out_shape=[_sds((1, segs[n][1]), F32)] + [_sds(w.shape, F32) for w in ws for _ in range(4)]
        + [_sds((1, seg[1]), F32) for seg in segs[n + 1:]],
        compiler_params=_params(),
    )(gathered, *ws, *ms, *vs)


def _adamw_plain(g, w, m, v, name):
    def body(g_ref, w_ref, m_ref, v_ref, d_ref, nm_ref, nv_ref):
        d_ref[...], nm_ref[...], nv_ref[...] = _adamw_math(w_ref[...], g_ref[...], m_ref[...], v_ref[...])

    return list(_call(body, name=name, out_shape=[_sds(w.shape, F32)] * 3)(g, w, m, v))


REPLICATED = ("ffn1_norm", "mix_norm", "b_in", "conv_cf_b", "ln_cf_g", "ln_cf_b", "ffn2_norm", "final_norm")
SHARDED = ("meta_tokens", "ffn1_w_gate", "ffn1_w_up", "ffn1_w_down", "w_in", "conv_sc_w", "conv_cf_w", "w_out",
           "ffn2_w_gate", "ffn2_w_up", "ffn2_w_down")
WEIGHTS = ("meta_tokens", "ffn1_norm", "ffn1_w_gate", "ffn1_w_up", "ffn1_w_down", "mix_norm", "w_in", "b_in",
           "conv_sc_w", "conv_cf_w", "conv_cf_b", "ln_cf_g", "ln_cf_b", "w_out", "ffn2_norm", "ffn2_w_gate",
           "ffn2_w_up", "ffn2_w_down", "final_norm")


TRANSPOSED = ("ffn1_w_gate", "ffn1_w_up", "ffn2_w_gate", "ffn2_w_up")


def _blocks2d(k, a):
    a = a.reshape(a.shape[-2:]) if a.ndim >= 2 else a.reshape(1, -1)
    return a.T if k in TRANSPOSED else a


def _step(x, tgt, w, m, v):
    n_seq, d = x.shape[1], x.shape[2]
    n_meta = w["meta_tokens"].shape[0]
    rows = -(-(n_seq + n_meta) // ROW_ALIGN) * ROW_ALIGN
    tm = rows // N_ROW_TILES
    cb = w["conv_sc_w"].shape[-1]
    dc = cb * N_DEV
    w2 = {k: _blocks2d(k, a) for k, a in w.items()}
    m2 = {k: _blocks2d(k, a) for k, a in m.items()}
    v2 = {k: _blocks2d(k, a) for k, a in v.items()}

    def as_given(k, r):
        return (r.T if k in TRANSPOSED else r).reshape(w[k].shape)

    def cast(k):
        return w2[k].astype(BF16)

    full = dict(zip(("ffn1_w_gate", "meta_tokens", "conv_sc_w", "conv_cf_w"), _exchange_alone(
        _gather_all([cast("ffn1_w_gate"), w2["meta_tokens"], w2["conv_sc_w"], w2["conv_cf_w"]]), "gather_gate1")))
    meta = jnp.transpose(full["meta_tokens"], (1, 0, 2)).reshape(n_meta, d)
    hs0 = jnp.concatenate([x[0], meta, jnp.zeros((rows - n_seq - n_meta, d), F32)], axis=0)

    tx = n_seq // N_ROW_TILES
    h1, r1 = _rms_fwd(hs0, w2["ffn1_norm"], tm, "rms_in")
    up1 = _gather_all([cast("ffn1_w_up")]).before(15 / 16)
    tm2, tx2 = 2 * tm, 2 * tx
    g1 = _ffn_proj(h1, full["ffn1_w_gate"], tm2, "ffn1_gate", comm=[up1])
    down1 = _gather_all([cast("ffn1_w_down")]).before(15 / 16)
    u1 = _ffn_proj(h1, up1.results[0], tm2, "ffn1_up", comm=[down1])
    w_in_all, w_out1 = _gather_all([cast("w_in")]).before(3 / 4), _gather_direct([cast("w_out")])
    hs1, h2, r2 = _ffn_down_norm(g1, u1, down1.results[0], hs0, w2["mix_norm"], rows // DOWN_ROW_TILES, "ffn1_down",
                                 comm=[w_in_all, w_out1])
    win = jnp.transpose(w_in_all.results[0], (1, 0, 2)).reshape(d, -1)
    w_out2, gate2 = _gather_forward(w_out1.results), _gather_direct([cast("ffn2_w_gate")])
    u = _win_fwd(h2, win, w2["b_in"], dc, tm, "mix_in", comm=[w_out2, gate2])
    wout = w_out2.results[0].reshape(-1, d)
    gate2p = _gather_forward(gate2.results)
    ysc, zc = _conv_fwd(u, full["conv_sc_w"], full["conv_cf_w"], w2["conv_cf_b"], n_seq, n_meta, "conv_fwd", comm=[gate2p])
    up2 = _gather_all([cast("ffn2_w_up")])
    y, hs2, h3, r3 = _wout_fwd(zc, ysc, wout, hs1, w2["ln_cf_g"], w2["ln_cf_b"], w2["ffn2_norm"], n_seq, tx, "mix_out",
                               comm=[up2])
    down2 = _gather_all([cast("ffn2_w_down")]).before(5 / 8)
    full.update(ffn1_w_up=up1.results[0], ffn1_w_down=down1.results[0], ffn2_w_gate=gate2p.results[0],
                ffn2_w_up=up2.results[0])
    g2, u2 = _ffn_gu(h3, full["ffn2_w_gate"], full["ffn2_w_up"], 2 * tx2, "ffn2_gu", comm=[down2])
    full["ffn2_w_down"] = down2.results[0]
    dhs3, dhs3b, loss_p, dgf_p = _ffn_down_loss(
        g2, u2, full["ffn2_w_down"], hs2, w2["final_norm"], tgt[0], n_seq, n_seq // DOWN_ROW_TILES, "ffn2_down_loss")

    xi, yi, ci = lax.axis_index("x"), lax.axis_index("y"), lax.axis_index("c")
    chip_of = [2 * xi + yi, 2 * (1 - xi) + yi, 2 * xi + (1 - yi), 2 * (1 - xi) + (1 - yi)]
    idx = jnp.stack([2 * ch + ci for ch in chip_of] + chip_of).astype(jnp.int32)
    out = {}

    def to_pairs(parts):
        return _pair_exchange([p.reshape((4, 2) + p.shape[1:]) for p in parts])

    def pair_sums(names, parts, pairs):
        return [_pair_sum(p, s, idx, "pair_sum_" + k) for k, p, s in zip(names, parts, pairs.results)]

    def update(names, sums, chips):
        for k, own, got in zip(names, sums, chips.results):
            res = _adamw_sharded(own, got, w2[k], m2[k], v2[k], "adamw_" + k)
            out[k] = [as_given(k, r) for r in res]

    tk = min(4 * ROW_ALIGN, n_seq)
    th_x, th_r = n_seq // DOWN_ROW_TILES, rows // DOWN_ROW_TILES

    dwd2 = _ffn_bwd_dw_down(g2, u2, dhs3b, tk, "ffn2_bwd_dw_down")
    p_d2 = to_pairs([dwd2])
    dg2, du2 = _ffn_bwd_da(dhs3b, full["ffn2_w_down"], g2, u2, tx2, "ffn2_bwd_da", comm=[p_d2])
    s_d2 = pair_sums(["ffn2_w_down"], [dwd2], p_d2)
    c_d2 = _chip_exchange(s_d2)
    gu2 = _ffn_bwd_dw_gu(h3, dg2, du2, tk, "ffn2_bwd_dw_gu", comm=[c_d2])
    update(["ffn2_w_down"], s_d2, c_d2)
    p_gu2 = to_pairs(gu2)
    half2 = _ffn_bwd_dh_gate(dg2, full["ffn2_w_gate"], tx, "ffn2_bwd_dh_gate", comm=[p_gu2])
    s_gu2 = pair_sums(["ffn2_w_gate", "ffn2_w_up"], gu2, p_gu2)
    c_g2, c_u2 = _chip_exchange(s_gu2[:1]), _chip_exchange(s_gu2[1:])
    dhs2, dhs2b, dn3_p = _ffn_bwd_dh(du2, full["ffn2_w_up"], half2, hs2, r3, w2["ffn2_norm"], dhs3, th_x, "ffn2_bwd_dh",
                                     zeros=jnp.zeros((rows, d), F32))
    dysc, dzc, dlg_p, dlb_p = _wout_bwd_dy(dhs2b, wout, zc, w2["ln_cf_g"], w2["ln_cf_b"], tx, "mix_out_bwd_dy")

    dwout = _wout_bwd_dw(y, dhs2b, N_DEV, tk, "mix_out_bwd_dw")
    p_wo = to_pairs([dwout])
    du, dcsw, dccw, dcb = _conv_bwd(u, dysc, dzc, full["conv_sc_w"], full["conv_cf_w"], n_seq, n_meta, "conv_bwd",
                                    comm=[c_g2, p_wo])
    update(["ffn2_w_gate"], s_gu2[:1], c_g2)
    s_wo = pair_sums(["w_out"], [dwout], p_wo)
    c_wo = _chip_exchange(s_wo)
    dhs1, dhs1b, dn2_p = _win_bwd_dh(du, win, hs1, r2, w2["mix_norm"], dhs2, th_r, "mix_in_bwd_dh", comm=[c_u2])
    update(["ffn2_w_up"], s_gu2[1:], c_u2)
    dwin, dbin = _win_bwd_dw(h2, du, tk, "mix_in_bwd_dw", comm=[c_wo])
    update(["w_out"], s_wo, c_wo)
    mixer = ("w_in", "conv_sc_w", "conv_cf_w")
    gradsm = [jnp.transpose(dwin.reshape(d, N_DEV, -1), (1, 0, 2)), dcsw, dccw]
    p_m = to_pairs(gradsm)
    dwd1 = _ffn_bwd_dw_down(g1, u1, dhs1b, tk, "ffn1_bwd_dw_down", comm=[p_m])
    s_m = pair_sums(mixer, gradsm, p_m)
    c_m, p_d1 = _chip_exchange(s_m), to_pairs([dwd1])
    dg1, du1 = _ffn_bwd_da(dhs1b, full["ffn1_w_down"], g1, u1, tm2, "ffn1_bwd_da", comm=[c_m, p_d1])
    update(mixer, s_m, c_m)
    s_d1 = pair_sums(["ffn1_w_down"], [dwd1], p_d1)
    c_d1 = _chip_exchange(s_d1)
    gu1 = _ffn_bwd_dw_gu(h1, dg1, du1, tk, "ffn1_bwd_dw_gu", comm=[c_d1])
    update(["ffn1_w_down"], s_d1, c_d1)
    p_gu1 = to_pairs(gu1)
    _exchange_alone(p_gu1, "reduce_pair_ffn1")
    s_gu1 = pair_sums(["ffn1_w_gate", "ffn1_w_up"], gu1, p_gu1)
    c_g1, c_u1 = _chip_exchange(s_gu1[:1]), _chip_exchange(s_gu1[1:])
    half1 = _ffn_bwd_dh_gate(dg1, full["ffn1_w_gate"], tm, "ffn1_bwd_dh_gate", comm=[c_g1])
    dhs0, _, dn1_p = _ffn_bwd_dh(du1, full["ffn1_w_up"], half1, hs0, r1, w2["ffn1_norm"], dhs1, th_r, "ffn1_bwd_dh",
                                 comm=[c_u1])
    update(["ffn1_w_gate"], s_gu1[:1], c_g1)
    update(["ffn1_w_up"], s_gu1[1:], c_u1)
    grad_x = dhs0[:n_seq][None]

    partial = {
        "ffn1_norm": dn1_p.sum(0), "mix_norm": dn2_p.sum(0), "b_in": dbin, "conv_cf_b": dcb,
        "ln_cf_g": dlg_p.sum(0), "ln_cf_b": dlb_p.sum(0), "ffn2_norm": dn3_p.sum(0), "final_norm": dgf_p.sum(0),
    }
    loss_seg = jnp.pad(loss_p.sum((0, 2)).reshape(1, 1), ((0, 0), (0, 127)))
    pieces = [partial[k] for k in REPLICATED] + [loss_seg, dhs0[n_seq:n_seq + n_meta].reshape(1, n_meta * d)]
    segs, off = [], 0
    for p in pieces:
        segs.append((off, p.shape[1]))
        off += p.shape[1]
    rows8 = _gather_rows(jnp.concatenate(pieces, axis=1), "gather_small")
    res = _adamw_replicated(rows8, segs, [w2[k] for k in REPLICATED], [m2[k] for k in REPLICATED],
                            [v2[k] for k in REPLICATED], 0.5 / d, "adamw_replicated")
    loss = res[0][0, 0]
    for p, k in enumerate(REPLICATED):
        out[k] = [r.reshape(w[k].shape) for r in res[1 + 4 * p:5 + 4 * p]]
    ob = w2["meta_tokens"].shape[1]
    gmeta = lax.dynamic_slice_in_dim(res[-1].reshape(n_meta, d), (4 * xi + 2 * yi + ci) * ob, ob, axis=1)
    out["meta_tokens"] = [gmeta] + _adamw_plain(gmeta, w2["meta_tokens"], m2["meta_tokens"], v2["meta_tokens"], "adamw_meta_tokens")

    return (loss, grad_x, *[out[k][0] for k in WEIGHTS], *[out[k][1] for k in WEIGHTS],
            *[out[k][2] for k in WEIGHTS], *[out[k][3] for k in WEIGHTS])


def kernel(x, meta_tokens, ffn1_norm, ffn1_w_gate, ffn1_w_up, ffn1_w_down, mix_norm, w_in, b_in, conv_sc_w, conv_cf_w, conv_cf_b, ln_cf_g, ln_cf_b, w_out, ffn2_norm, ffn2_w_gate, ffn2_w_up, ffn2_w_down, final_norm, loss_target, m_meta_tokens, m_ffn1_norm, m_ffn1_w_gate, m_ffn1_w_up, m_ffn1_w_down, m_mix_norm, m_w_in, m_b_in, m_conv_sc_w, m_conv_cf_w, m_conv_cf_b, m_ln_cf_g, m_ln_cf_b, m_w_out, m_ffn2_norm, m_ffn2_w_gate, m_ffn2_w_up, m_ffn2_w_down, m_final_norm, v_meta_tokens, v_ffn1_norm, v_ffn1_w_gate, v_ffn1_w_up, v_ffn1_w_down, v_mix_norm, v_w_in, v_b_in, v_conv_sc_w, v_conv_cf_w, v_conv_cf_b, v_ln_cf_g, v_ln_cf_b, v_w_out, v_ffn2_norm, v_ffn2_w_gate, v_ffn2_w_up, v_ffn2_w_down, v_final_norm):
    given = dict(locals())
    w = {k: given[k] for k in WEIGHTS}
    m = {k: given["m_" + k] for k in WEIGHTS}
    v = {k: given["v_" + k] for k in WEIGHTS}
    return _step(x, loss_target, w, m, v)
```

```python
import functools

import jax
import jax.numpy as jnp
from jax import lax
from jax.experimental import pallas as pl
from jax.experimental.pallas import tpu as pltpu

F32 = jnp.float32
BF16 = jnp.bfloat16
EPS = 1e-6
FFN_RES_SCALE = 0.5
SC_WIDTH = 3
CF_WIDTH = 31
ADAM_LR = 0.001
ADAM_B1 = 0.9
ADAM_B2 = 0.999
ADAM_EPS = 1e-08
ADAM_WD = 0.01
ADAM_STEP = 10

N_DEV = 8
N_ROW_TILES = 8
ROW_ALIGN = 256
CONV_PAD = 32
CONV_CH = 128
SUB_ROWS = 32
DOWN_ROW_TILES = 16
GATHER_PARTS = 2
STREAM_DEPTH = 3
SUBLANES = 8
BF16_ROWS = 16
VMEM_LIMIT = 56 * 1024 * 1024
MESH = pl.DeviceIdType.MESH
ANY = pl.BlockSpec(memory_space=pl.ANY)

NT = (((1,), (1,)), ((), ()))
TN = (((0,), (0,)), ((), ()))


def _pallas(body, **kw):
    return pl.pallas_call(body, **kw)


class _Exchange:
    def __init__(self, inputs, out_shapes, sem_shapes, start, finish, aliases=None, early=None):
        self.inputs, self.out_shapes, self.sem_shapes = list(inputs), list(out_shapes), list(sem_shapes)
        self.start, self.finish, self.aliases = start, finish, dict(aliases or {})
        self.early, self.early_at = early, None
        self.results = None

    def before(self, share):
        self.early_at = share
        return self


def _call(body, comm=(), **kw):
    if not comm:
        return _pallas(body, **kw)
    grid = kw.pop("grid")
    in_specs = list(kw.pop("in_specs"))
    out_specs, out_shape = kw.pop("out_specs"), kw.pop("out_shape")
    scratch = list(kw.pop("scratch_shapes", []))
    single = not isinstance(out_shape, (list, tuple))
    out_specs, out_shape = ([out_specs], [out_shape]) if single else (list(out_specs), list(out_shape))
    n_in, n_out, n_scr = len(in_specs), len(out_shape), len(scratch)
    c_in = [a for job in comm for a in job.inputs]
    c_out = [s for job in comm for s in job.out_shapes]
    c_sem = [s for job in comm for s in job.sem_shapes]
    aliases, i0, o0 = dict(kw.pop("input_output_aliases", {})), n_in, n_out
    for job in comm:
        aliases.update({i0 + i: o0 + o for i, o in job.aliases.items()})
        i0, o0 = i0 + len(job.inputs), o0 + len(job.out_shapes)

    def hosted(*refs):
        pos = [0]

        def take(n):
            pos[0] += n
            return refs[pos[0] - n:pos[0]]

        ins, cins, outs, couts, scr, sems = take(n_in), take(len(c_in)), take(n_out), take(len(c_out)), take(n_scr), take(len(c_sem))
        ids = [pl.program_id(k) for k in range(len(grid))]
        first = functools.reduce(jnp.logical_and, [i == 0 for i in ids])
        last = functools.reduce(jnp.logical_and, [i == g - 1 for i, g in zip(ids, grid)])

        n_steps = functools.reduce(lambda a, b: a * b, grid, 1)
        step = functools.reduce(lambda acc, ig: acc * ig[1] + ig[0], zip(ids, grid), 0)

        def early_step(job):
            if job.early is None or job.early_at is None or n_steps < 2:
                return None
            return min(max(int(job.early_at * n_steps), 0), n_steps - 2)

        def each(phase, chosen=None):
            i, o, s = 0, 0, 0
            for job in comm:
                ni, no, ns = len(job.inputs), len(job.out_shapes), len(job.sem_shapes)
                if chosen is None or chosen(job):
                    getattr(job, phase)(cins[i:i + ni], couts[o:o + no], sems[s:s + ns])
                i, o, s = i + ni, o + no, s + ns

        @pl.when(first)
        def _():
            each("start")

        body(*ins, *outs, *scr)

        for at in sorted({early_step(job) for job in comm} - {None}):
            @pl.when(step == at)
            def _():
                each("early", lambda job: early_step(job) == at)

        @pl.when(last)
        def _():
            each("early", lambda job: job.early is not None and early_step(job) is None)
            each("finish")

    call = _pallas(
        hosted, grid=grid, in_specs=in_specs + [ANY] * len(c_in), out_specs=out_specs + [ANY] * len(c_out),
        out_shape=out_shape + c_out, scratch_shapes=scratch + c_sem, input_output_aliases=aliases, **kw)

    def run(*args):
        res = call(*args, *c_in)
        o = n_out
        for job in comm:
            job.results = list(res[o:o + len(job.out_shapes)])
            o += len(job.out_shapes)
        return res[0] if single else list(res[:n_out])

    return run


def _exchange_alone(job, name, comm=()):
    n_in, n_out = len(job.inputs), len(job.out_shapes)

    def body(*refs):
        ins, outs, sems = refs[:n_in], refs[n_in:n_in + n_out], refs[n_in + n_out:]
        job.start(ins, outs, sems)
        if job.early is not None:
            job.early(ins, outs, sems)
        job.finish(ins, outs, sems)

    res = _pallas(body, name=name, in_specs=[ANY] * n_in, out_specs=[ANY] * n_out, out_shape=job.out_shapes,
                  scratch_shapes=job.sem_shapes, input_output_aliases=job.aliases)(*job.inputs)
    job.results = list(res)
    return job.results


def _params(**kw):
    return pltpu.CompilerParams(vmem_limit_bytes=VMEM_LIMIT, **kw)


def _sigmoid(x):
    return 0.5 * jnp.tanh(0.5 * x) + 0.5


def _sds(shape, dtype):
    return jax.ShapeDtypeStruct(shape, dtype)


def _place():
    x, y, c = lax.axis_index("x"), lax.axis_index("y"), lax.axis_index("c")
    chips = [(1 - x, y), (x, 1 - y), (1 - x, 1 - y)]
    return x, y, c, chips


def _slot(ref, p):
    return ref.at[4 * p[0] + 2 * p[1] + p[2]]


def _remote(src, dst, send_sem, recv_sem, to):
    return pltpu.make_async_remote_copy(src_ref=src, dst_ref=dst, send_sem=send_sem, recv_sem=recv_sem,
                                        device_id=to, device_id_type=MESH)


def _gather_direct(arrs):
    n = len(arrs)

    def copies(ins, outs, sems):
        send_sems, recv_sems, local_sems = sems
        x, y, c, chips = _place()
        me = (x, y, c)
        peers = [(x, y, 1 - c)] + [(*chip, c) for chip in chips]
        local = [pltpu.make_async_copy(ins[a], _slot(outs[a], me), local_sems.at[a]) for a in range(n)]
        sends = [_remote(ins[a], _slot(outs[a], me), send_sems.at[a, k], recv_sems.at[a, k], peer)
                 for a in range(n) for k, peer in enumerate(peers)]
        arrivals = [_remote(ins[a], _slot(outs[a], peer), send_sems.at[a, k], recv_sems.at[a, k], peer)
                    for a in range(n) for k, peer in enumerate(peers)]
        return local, sends, arrivals

    def start(ins, outs, sems):
        local, sends, _ = copies(ins, outs, sems)
        for cp in local + sends:
            cp.start()

    def finish(ins, outs, sems):
        local, sends, arrivals = copies(ins, outs, sems)
        for cp in arrivals:
            cp.wait_recv()
        for cp in sends:
            cp.wait_send()
        for cp in local:
            cp.wait()

    dma = pltpu.SemaphoreType.DMA
    return _Exchange(arrs, [_sds((N_DEV,) + a.shape, a.dtype) for a in arrs], [dma((n, 4)), dma((n, 4)), dma((n,))],
                     start, finish)


def _gather_forward(gathered):
    n = len(gathered)

    def copies(ins, outs, sems):
        send_sems, recv_sems = sems
        x, y, c, chips = _place()
        sibling = (x, y, 1 - c)
        sends = [_remote(_slot(ins[a], (*chip, c)), _slot(outs[a], (*chip, c)), send_sems.at[a, j], recv_sems.at[a, j], sibling)
                 for a in range(n) for j, chip in enumerate(chips)]
        arrivals = [_remote(_slot(ins[a], (*chip, c)), _slot(outs[a], (*chip, 1 - c)), send_sems.at[a, j], recv_sems.at[a, j], sibling)
                    for a in range(n) for j, chip in enumerate(chips)]
        return sends, arrivals

    def start(ins, outs, sems):
        for cp in copies(ins, outs, sems)[0]:
            cp.start()

    def finish(ins, outs, sems):
        sends, arrivals = copies(ins, outs, sems)
        for cp in arrivals:
            cp.wait_recv()
        for cp in sends:
            cp.wait_send()

    dma = pltpu.SemaphoreType.DMA
    return _Exchange(gathered, [_sds(a.shape, a.dtype) for a in gathered], [dma((n, 3)), dma((n, 3))], start, finish,
                     aliases={a: a for a in range(n)})


def _gather_all(arrs, parts=GATHER_PARTS):
    n = len(arrs)
    split = [parts if a.shape[0] % (parts * BF16_ROWS) == 0 else 1 for a in arrs]
    members = [[a for a in range(n) if p < split[a]] for p in range(parts)]

    def part_of(a, p):
        size = arrs[a].shape[0] // split[a]
        return pl.ds(p * size, size), _sds((size,) + arrs[a].shape[1:], arrs[a].dtype)

    direct = [_gather_direct([part_of(a, p)[1] for a in members[p]]) for p in range(parts)]
    passed = [_gather_forward([_sds((N_DEV,) + part_of(a, p)[1].shape, arrs[a].dtype) for a in members[p]])
              for p in range(parts)]
    jobs = direct + passed
    sem_shapes = [s for job in jobs for s in job.sem_shapes]

    def views(ins, outs, sems):
        found, at = [], 0
        for i, job in enumerate(jobs):
            p = i % parts
            rows = [part_of(a, p)[0] for a in members[p]]
            found.append(([ins[a].at[r] for a, r in zip(members[p], rows)],
                          [outs[a].at[:, r] for a, r in zip(members[p], rows)],
                          sems[at:at + len(job.sem_shapes)]))
            at += len(job.sem_shapes)
        return found

    def start(ins, outs, sems):
        for job, (i, o, s) in list(zip(jobs, views(ins, outs, sems)))[:parts]:
            job.start(i, o, s)

    def early(ins, outs, sems):
        found = views(ins, outs, sems)
        for p in range(parts):
            i, o, s = found[p]
            direct[p].finish(i, o, s)
            _, o2, s2 = found[parts + p]
            passed[p].start(o2, o2, s2)

    def finish(ins, outs, sems):
        found = views(ins, outs, sems)
        for p in range(parts):
            _, o2, s2 = found[parts + p]
            passed[p].finish(o2, o2, s2)

    return _Exchange(arrs, [_sds((N_DEV,) + a.shape, a.dtype) for a in arrs], sem_shapes, start, finish, early=early)


def _pair_exchange(arrs):
    n = len(arrs)

    def copies(ins, outs, sems):
        x, y, c, _ = _place()
        return [_remote(ins[a].at[:, 1 - c], outs[a], sems[0].at[a], sems[1].at[a], (x, y, 1 - c)) for a in range(n)]

    def start(ins, outs, sems):
        for cp in copies(ins, outs, sems):
            cp.start()

    def finish(ins, outs, sems):
        for cp in copies(ins, outs, sems):
            cp.wait()

    dma = pltpu.SemaphoreType.DMA
    return _Exchange(arrs, [_sds((4,) + a.shape[2:], a.dtype) for a in arrs], [dma((n,)), dma((n,))], start, finish)


def _chip_exchange(arrs):
    n = len(arrs)

    def copies(ins, outs, sems):
        x, y, c, chips = _place()
        return [_remote(ins[a].at[1 + j], outs[a].at[j], sems[0].at[a, j], sems[1].at[a, j], (*chip, c))
                for a in range(n) for j, chip in enumerate(chips)]

    def start(ins, outs, sems):
        for cp in copies(ins, outs, sems):
            cp.start()

    def finish(ins, outs, sems):
        for cp in copies(ins, outs, sems):
            cp.wait()

    dma = pltpu.SemaphoreType.DMA
    return _Exchange(arrs, [_sds((3,) + a.shape[1:], a.dtype) for a in arrs], [dma((n, 3)), dma((n, 3))], start, finish)


def _gather_rows(vec, name, comm=()):
    def body(in_ref, out_ref, send_sems, recv_sems, local_sem):
        x, y, c, _ = _place()
        me = 4 * x + 2 * y + c
        mine = pltpu.make_async_copy(in_ref, out_ref.at[me], local_sem)
        mine.start()
        copies = []
        for k in range(1, N_DEV):
            to = (1 - x if k & 4 else x, 1 - y if k & 2 else y, 1 - c if k & 1 else c)
            copies.append(pltpu.make_async_remote_copy(
                src_ref=in_ref, dst_ref=out_ref.at[me], send_sem=send_sems.at[k - 1], recv_sem=recv_sems.at[k - 1],
                device_id=to, device_id_type=MESH))
        for cp in copies:
            cp.start()
        for cp in copies:
            cp.wait()
        mine.wait()

    return _call(
        body, name=name, out_shape=_sds((N_DEV,) + vec.shape, vec.dtype), in_specs=[ANY], out_specs=ANY,
        scratch_shapes=[pltpu.SemaphoreType.DMA((7,)), pltpu.SemaphoreType.DMA((7,)), pltpu.SemaphoreType.DMA],
    )(vec)


def _group_rows(tm):
    return SUB_ROWS if tm % SUB_ROWS == 0 else BF16_ROWS


def _for_row_groups(tm, fn):
    sub = _group_rows(tm)

    def step(i, carry):
        fn(pl.ds(pl.multiple_of(i * sub, sub), sub), i * sub)
        return carry

    lax.fori_loop(0, tm // sub, step, 0)


def _once(shape, index_map):
    return pl.BlockSpec(shape, index_map, pipeline_mode=pl.Buffered(1))


def _rms_fwd(hs, g, tm, name, comm=()):
    rows, d = hs.shape

    def body(hs_ref, g_ref, h_ref, r_ref):
        def group(rs, _):
            xv = hs_ref[rs, :]
            r = lax.rsqrt(jnp.mean(xv * xv, axis=-1, keepdims=True) + EPS)
            h_ref[rs, :] = (xv * r * g_ref[...]).astype(BF16)
            r_ref[rs, :] = r

        _for_row_groups(tm, group)

    return _call(
        body, comm=comm, name=name, grid=(rows // tm,),
        in_specs=[pl.BlockSpec((tm, d), lambda i: (i, 0)), pl.BlockSpec((1, d), lambda i: (0, 0))],
        out_specs=[pl.BlockSpec((tm, d), lambda i: (i, 0)), pl.BlockSpec((tm, 1), lambda i: (i, 0))],
        out_shape=[_sds((rows, d), BF16), _sds((rows, 1), F32)], compiler_params=_params(),
    )(hs, g)


def _ffn_gu(h, wg, wu, tm, name, comm=()):
    rows, d = h.shape
    nb, fb, _ = wg.shape

    def body(h_ref, wg_ref, wu_ref, g_ref, u_ref):
        hv = h_ref[...]
        g_ref[0] = lax.dot_general(hv, wg_ref[0], NT, preferred_element_type=F32).astype(BF16)
        u_ref[0] = lax.dot_general(hv, wu_ref[0], NT, preferred_element_type=F32).astype(BF16)

    wspec = pl.BlockSpec((1, fb, d), lambda j, i: (j, 0, 0))
    ospec = pl.BlockSpec((1, tm, fb), lambda j, i: (j, i, 0))
    return _call(
        body, comm=comm, name=name, grid=(nb, rows // tm),
        in_specs=[pl.BlockSpec((tm, d), lambda j, i: (i, 0)), wspec, wspec],
        out_specs=[ospec, ospec], out_shape=[_sds((nb, rows, fb), BF16)] * 2, compiler_params=_params(),
    )(h, wg, wu)


def _ffn_proj(h, w, tm, name, comm=()):
    rows, d = h.shape
    nb, fb, _ = w.shape

    def body(h_ref, w_ref, p_ref):
        p_ref[0] = lax.dot_general(h_ref[...], w_ref[0], NT, preferred_element_type=F32).astype(BF16)

    return _call(
        body, comm=comm, name=name, grid=(nb, rows // tm),
        in_specs=[pl.BlockSpec((tm, d), lambda j, i: (i, 0)), pl.BlockSpec((1, fb, d), lambda j, i: (j, 0, 0))],
        out_specs=pl.BlockSpec((1, tm, fb), lambda j, i: (j, i, 0)), out_shape=_sds((nb, rows, fb), BF16),
        compiler_params=_params(),
    )(h, w)


def _down_product(g_ref, u_ref, wd_ref, acc_ref):
    part = None
    for b in range(g_ref.shape[0]):
        gv = g_ref[b]
        a = gv * _sigmoid(gv) * u_ref[b]
        p = jnp.dot(a, wd_ref[b], preferred_element_type=F32)
        part = p if part is None else part + p
    acc_ref[...] = part


def _ffn_down_norm(g, u, wd, hs, gn, tm, name, comm=()):
    nb, rows, fb = g.shape
    d = hs.shape[1]

    def body(g_ref, u_ref, wd_ref, hs_ref, gn_ref, hsn_ref, hn_ref, rn_ref, acc_ref):
        _down_product(g_ref, u_ref, wd_ref, acc_ref)

        def group(rs, _):
            hsn = hs_ref[rs, :] + FFN_RES_SCALE * acc_ref[rs, :]
            r = lax.rsqrt(jnp.mean(hsn * hsn, axis=-1, keepdims=True) + EPS)
            hsn_ref[rs, :] = hsn
            hn_ref[rs, :] = (hsn * r * gn_ref[...]).astype(BF16)
            rn_ref[rs, :] = r

        _for_row_groups(tm, group)

    aspec = pl.BlockSpec((nb, tm, fb), lambda i: (0, i, 0))
    row = pl.BlockSpec((tm, d), lambda i: (i, 0))
    return _call(
        body, comm=comm, name=name, grid=(rows // tm,),
        in_specs=[aspec, aspec, _once((nb, fb, d), lambda i: (0, 0, 0)), row, pl.BlockSpec((1, d), lambda i: (0, 0))],
        out_specs=[row, row, pl.BlockSpec((tm, 1), lambda i: (i, 0))],
        out_shape=[_sds((rows, d), F32), _sds((rows, d), BF16), _sds((rows, 1), F32)],
        scratch_shapes=[pltpu.VMEM((tm, d), F32)], compiler_params=_params(),
    )(g, u, wd, hs, gn)


def _ffn_down_loss(g, u, wd, hs, gf, tgt, n_seq, tm, name, comm=()):
    nb, rows, fb = g.shape
    d = hs.shape[1]
    nt = rows // tm

    def body(g_ref, u_ref, wd_ref, hs_ref, gf_ref, tgt_ref, dhs_ref, dhsb_ref, loss_ref, dgf_ref, acc_ref):
        i = pl.program_id(0)
        _down_product(g_ref, u_ref, wd_ref, acc_ref)
        loss_ref[0] = jnp.zeros((1, d), F32)
        dgf_ref[0] = jnp.zeros((1, d), F32)

        def group(rs, r0):
            hs3 = hs_ref[rs, :] + FFN_RES_SCALE * acc_ref[rs, :]
            r = lax.rsqrt(jnp.mean(hs3 * hs3, axis=-1, keepdims=True) + EPS)
            gfv = gf_ref[...]
            y = hs3 * r
            rowid = i * tm + r0 + lax.broadcasted_iota(jnp.int32, (_group_rows(tm), 1), 0)
            err = jnp.where(rowid < n_seq, y * gfv - tgt_ref[rs, :], 0.0)
            loss_ref[0] += jnp.sum(err * err, axis=0, keepdims=True)
            dout = err * (1.0 / d)
            dgf_ref[0] += jnp.sum(dout * y, axis=0, keepdims=True)
            t = dout * gfv
            dhs = r * t - hs3 * (r * r * r) * jnp.mean(t * hs3, axis=-1, keepdims=True)
            dhs_ref[rs, :] = dhs
            dhsb_ref[rs, :] = dhs.astype(BF16)

        _for_row_groups(tm, group)

    aspec = pl.BlockSpec((nb, tm, fb), lambda i: (0, i, 0))
    row = pl.BlockSpec((tm, d), lambda i: (i, 0))
    part = pl.BlockSpec((1, 1, d), lambda i: (i, 0, 0))
    return _call(
        body, comm=comm, name=name, grid=(nt,),
        in_specs=[aspec, aspec, _once((nb, fb, d), lambda i: (0, 0, 0)), row, pl.BlockSpec((1, d), lambda i: (0, 0)), row],
        out_specs=[row, row, part, part],
        out_shape=[_sds((rows, d), F32), _sds((rows, d), BF16), _sds((nt, 1, d), F32), _sds((nt, 1, d), F32)],
        scratch_shapes=[pltpu.VMEM((tm, d), F32)], compiler_params=_params(),
    )(g, u, wd, hs, gf, tgt)


def _ffn_bwd_da(do, wd, g, u, tm, name, comm=()):
    nb, rows, fb = g.shape
    d = do.shape[1]

    def body(do_ref, wd_ref, g_ref, u_ref, dg_ref, du_ref):
        rs = pl.ds(pl.multiple_of(pl.program_id(1) * tm, tm), tm)
        da = (FFN_RES_SCALE * lax.dot_general(do_ref[rs, :], wd_ref[0], NT, preferred_element_type=F32)).astype(BF16)
        gv, uv = g_ref[0], u_ref[0]
        s = _sigmoid(gv)
        sg = gv * s
        du_ref[0] = da * sg
        dg_ref[0] = da * uv * (s + sg * (1.0 - s))

    aspec = pl.BlockSpec((1, tm, fb), lambda j, i: (j, i, 0))
    return _call(
        body, comm=comm, name=name, grid=(nb, rows // tm),
        in_specs=[_once((rows, d), lambda j, i: (0, 0)), pl.BlockSpec((1, fb, d), lambda j, i: (j, 0, 0)), aspec, aspec],
        out_specs=[aspec] * 2, out_shape=[_sds((nb, rows, fb), BF16)] * 2, compiler_params=_params(),
    )(do, wd, g, u)


def _rms_bwd(tm, d, dh_ref, hs_ref, r_ref, gn_ref, dres_ref, dhs_ref, dhsb_ref, dgn_ref):
    dgn_ref[0] = jnp.zeros((1, d), F32)

    def group(rs, _):
        dh, hs, r = dh_ref[rs, :], hs_ref[rs, :], r_ref[rs, :]
        dgn_ref[0] += jnp.sum(dh * (hs * r), axis=0, keepdims=True)
        t = dh * gn_ref[...]
        dhs = dres_ref[rs, :] + r * t - hs * (r * r * r) * jnp.mean(t * hs, axis=-1, keepdims=True)
        dhs_ref[rs, :] = dhs
        dhsb_ref[rs, :] = dhs.astype(BF16)

    _for_row_groups(tm, group)


def _blocks_dot(x_ref, w_ref):
    part = None
    for b in range(x_ref.shape[0]):
        p = jnp.dot(x_ref[b], w_ref[b], preferred_element_type=F32)
        part = p if part is None else part + p
    return part


def _ffn_bwd_dh_gate(dg, wg, tm, name, comm=()):
    nb, rows, fb = dg.shape
    d = wg.shape[2]

    def body(dg_ref, wg_ref, o_ref):
        o_ref[...] = _blocks_dot(dg_ref, wg_ref)

    return _call(
        body, comm=comm, name=name, grid=(rows // tm,),
        in_specs=[pl.BlockSpec((nb, tm, fb), lambda i: (0, i, 0)), _once((nb, fb, d), lambda i: (0, 0, 0))],
        out_specs=pl.BlockSpec((tm, d), lambda i: (i, 0)), out_shape=_sds((rows, d), F32), compiler_params=_params(),
    )(dg, wg)


def _ffn_bwd_dh(du, wu, half, hs, r, gn, dres, tm, name, comm=(), zeros=None):
    nb, rows, fb = du.shape
    d = hs.shape[1]
    nt = rows // tm
    extra = [] if zeros is None else [zeros]

    def body(du_ref, wu_ref, half_ref, hs_ref, r_ref, gn_ref, dres_ref, *rest):
        dhs_ref, dhsb_ref, dgn_ref, acc_ref = rest[len(extra):]
        acc_ref[...] = half_ref[...] + _blocks_dot(du_ref, wu_ref)
        _rms_bwd(tm, d, acc_ref, hs_ref, r_ref, gn_ref, dres_ref, dhs_ref, dhsb_ref, dgn_ref)

    row = pl.BlockSpec((tm, d), lambda i: (i, 0))
    return _call(
        body, comm=comm, name=name, grid=(nt,),
        in_specs=[pl.BlockSpec((nb, tm, fb), lambda i: (0, i, 0)), _once((nb, fb, d), lambda i: (0, 0, 0)), row, row,
                  pl.BlockSpec((tm, 1), lambda i: (i, 0)), pl.BlockSpec((1, d), lambda i: (0, 0)), row] + [ANY] * len(extra),
        out_specs=[row, row, pl.BlockSpec((1, 1, d), lambda i: (i, 0, 0))],
        out_shape=[_sds((rows, d) if zeros is None else zeros.shape, F32), _sds((rows, d), BF16), _sds((nt, 1, d), F32)],
        scratch_shapes=[pltpu.VMEM((tm, d), F32)], compiler_params=_params(),
        input_output_aliases={7: 0} if extra else {},
    )(du, wu, half, hs, r, gn, dres, *extra)


class _KSteps:
    def __init__(self, rows, tk):
        self.main, self.rem, self.tk = rows // tk, rows % tk, tk
        self.n = self.main + (1 if self.rem else 0)
        if self.rem:
            assert rows % self.rem == 0, (rows, tk)
            self.last = rows // self.rem - 1

    def specs(self, makers):
        main_at = (lambda k: jnp.minimum(k, self.main - 1)) if self.rem else (lambda k: k)
        out = [make(self.tk, main_at) for make in makers]
        if self.rem:
            out += [make(self.rem, lambda k: self.last) for make in makers]
        return out

    def args(self, arrs):
        return list(arrs) * (2 if self.rem else 1)

    def each(self, k, refs, fn):
        if not self.rem:
            fn(*refs)
            return
        n = len(refs) // 2
        pl.when(k < self.main)(lambda: fn(*refs[:n]))
        pl.when(k == self.main)(lambda: fn(*refs[n:]))


def _ffn_bwd_dw_gu(h, dg, du, tk, name, comm=()):
    nb, rows, fb = dg.shape
    d = h.shape[1]
    steps = _KSteps(rows, tk)

    def body(*refs):
        ins, (dwg_ref, dwu_ref, accg, accu) = refs[:-4], refs[-4:]
        k = pl.program_id(1)

        @pl.when(k == 0)
        def _():
            accg[...] = jnp.zeros_like(accg)
            accu[...] = jnp.zeros_like(accu)

        def add(h_ref, dg_ref, du_ref):
            hv = h_ref[...]
            accg[...] += lax.dot_general(dg_ref[0], hv, TN, preferred_element_type=F32)
            accu[...] += lax.dot_general(du_ref[0], hv, TN, preferred_element_type=F32)

        steps.each(k, ins, add)

        @pl.when(k == steps.n - 1)
        def _():
            dwg_ref[0] = accg[...].astype(BF16)
            dwu_ref[0] = accu[...].astype(BF16)

    wspec = pl.BlockSpec((1, fb, d), lambda j, k: (j, 0, 0))
    specs = steps.specs([lambda t, at: pl.BlockSpec((t, d), lambda j, k: (at(k), 0)),
                         lambda t, at: pl.BlockSpec((1, t, fb), lambda j, k: (j, at(k), 0)),
                         lambda t, at: pl.BlockSpec((1, t, fb), lambda j, k: (j, at(k), 0))])
    return _call(
        body, comm=comm, name=name, grid=(nb, steps.n),
        in_specs=specs, out_specs=[wspec] * 2,
        out_shape=[_sds((nb, fb, d), BF16)] * 2, scratch_shapes=[pltpu.VMEM((fb, d), F32)] * 2,
        compiler_params=_params(),
    )(*steps.args([h, dg, du]))


def _ffn_bwd_dw_down(g, u, do, tk, name, comm=()):
    nb, rows, fb = g.shape
    d = do.shape[1]
    steps = _KSteps(rows, tk)

    def body(*refs):
        ins, (dwd_ref, acc) = refs[:-2], refs[-2:]
        k = pl.program_id(1)

        @pl.when(k == 0)
        def _():
            acc[...] = jnp.zeros_like(acc)

        def add(g_ref, u_ref, do_ref):
            gv = g_ref[0]
            a = gv * _sigmoid(gv) * u_ref[0]
            acc[...] += lax.dot_general(a, do_ref[...], TN, preferred_element_type=F32)

        steps.each(k, ins, add)

        @pl.when(k == steps.n - 1)
        def _():
            dwd_ref[0] = (FFN_RES_SCALE * acc[...]).astype(BF16)

    specs = steps.specs([lambda t, at: pl.BlockSpec((1, t, fb), lambda j, k: (j, at(k), 0)),
                         lambda t, at: pl.BlockSpec((1, t, fb), lambda j, k: (j, at(k), 0)),
                         lambda t, at: pl.BlockSpec((t, d), lambda j, k: (at(k), 0))])
    return _call(
        body, comm=comm, name=name, grid=(nb, steps.n), in_specs=specs,
        out_specs=pl.BlockSpec((1, fb, d), lambda j, k: (j, 0, 0)), out_shape=_sds((nb, fb, d), BF16),
        scratch_shapes=[pltpu.VMEM((fb, d), F32)], compiler_params=_params(),
    )(*steps.args([g, u, do]))


def _win_fwd(h, w, b, dc, tm, name, comm=()):
    rows, d = h.shape
    ng = w.shape[1] // dc

    def body(h_ref, w_ref, b_ref, u_ref):
        u_ref[...] = (jnp.dot(h_ref[...], w_ref[...], preferred_element_type=F32) + b_ref[...]).astype(BF16)

    return _call(
        body, comm=comm, name=name, grid=(ng, rows // tm),
        in_specs=[pl.BlockSpec((tm, d), lambda m, i: (i, 0)), pl.BlockSpec((d, dc), lambda m, i: (0, m)),
                  pl.BlockSpec((1, dc), lambda m, i: (0, m))],
        out_specs=pl.BlockSpec((tm, dc), lambda m, i: (i, m)), out_shape=_sds((rows, ng * dc), BF16),
        compiler_params=_params(),
    )(h, w, b)


def _win_bwd_dh(du, w, hs, r, gn, dres, tm, name, comm=()):
    rows, d = hs.shape
    ng, _, dc = du.shape
    nt = rows // tm

    depth = min(STREAM_DEPTH, nt)

    def body(du_hbm, w_ref, hs_ref, r_ref, gn_ref, dres_ref, dhs_ref, dhsb_ref, dgn_ref, acc_ref, du_buf, du_sems):
        i = pl.program_id(0)

        def fetch(t, slot):
            return pltpu.make_async_copy(du_hbm.at[:, pl.ds(pl.multiple_of(t * tm, tm), tm), :], du_buf.at[slot],
                                         du_sems.at[slot])

        @pl.when(i == 0)
        def _():
            for t in range(depth):
                fetch(t, t).start()

        slot = lax.rem(i, depth)
        fetch(i, slot).wait()
        part = None
        for m in range(ng):
            p = lax.dot_general(du_buf[slot, m], w_ref[:, m * dc:(m + 1) * dc], NT, preferred_element_type=F32)
            part = p if part is None else part + p
        acc_ref[...] = part

        @pl.when(i + depth < nt)
        def _():
            fetch(i + depth, slot).start()

        _rms_bwd(tm, d, acc_ref, hs_ref, r_ref, gn_ref, dres_ref, dhs_ref, dhsb_ref, dgn_ref)

    row = pl.BlockSpec((tm, d), lambda i: (i, 0))
    return _call(
        body, comm=comm, name=name, grid=(nt,),
        in_specs=[ANY, _once((d, ng * dc), lambda i: (0, 0)), row,
                  pl.BlockSpec((tm, 1), lambda i: (i, 0)), pl.BlockSpec((1, d), lambda i: (0, 0)), row],
        out_specs=[row, row, pl.BlockSpec((1, 1, d), lambda i: (i, 0, 0))],
        out_shape=[_sds((rows, d), F32), _sds((rows, d), BF16), _sds((nt, 1, d), F32)],
        scratch_shapes=[pltpu.VMEM((tm, d), F32), pltpu.VMEM((depth, ng, tm, dc), BF16),
                        pltpu.SemaphoreType.DMA((depth,))],
        compiler_params=_params(),
    )(du, w, hs, r, gn, dres)


def _win_bwd_dw(h, du, tk, name, comm=()):
    rows, d = h.shape
    ng, _, dc = du.shape
    steps = _KSteps(rows, tk)

    def body(*refs):
        ins, (dw_ref, db_ref, acc, accb) = refs[:-4], refs[-4:]
        k = pl.program_id(1)

        @pl.when(k == 0)
        def _():
            acc[...] = jnp.zeros_like(acc)
            accb[...] = jnp.zeros_like(accb)

        def add(h_ref, du_ref):
            duv = du_ref[0]
            acc[...] += lax.dot_general(h_ref[...], duv, TN, preferred_element_type=F32)
            accb[...] += jnp.sum(duv.astype(F32), axis=0, keepdims=True)

        steps.each(k, ins, add)

        @pl.when(k == steps.n - 1)
        def _():
            dw_ref[...] = acc[...].astype(BF16)
            db_ref[...] = accb[...]

    specs = steps.specs([lambda t, at: pl.BlockSpec((t, d), lambda m, k: (at(k), 0)),
                         lambda t, at: pl.BlockSpec((1, t, dc), lambda m, k: (m, at(k), 0))])
    return _call(
        body, comm=comm, name=name, grid=(ng, steps.n), in_specs=specs,
        out_specs=[pl.BlockSpec((d, dc), lambda m, k: (0, m)), pl.BlockSpec((1, dc), lambda m, k: (0, m))],
        out_shape=[_sds((d, ng * dc), BF16), _sds((1, ng * dc), F32)],
        scratch_shapes=[pltpu.VMEM((d, dc), F32), pltpu.VMEM((1, dc), F32)], compiler_params=_params(),
    )(*steps.args([h, du]))


def _layernorm_silu(zc, lg, lb):
    mu = jnp.mean(zc, axis=-1, keepdims=True)
    xc = zc - mu
    rstd = lax.rsqrt(jnp.mean(xc * xc, axis=-1, keepdims=True) + EPS)
    nrm = xc * rstd
    lin = nrm * lg + lb
    s = _sigmoid(lin)
    return nrm, rstd, lin, s


def _wout_fwd(zc, ysc, wout, hs, lg, lb, gn, rows, tm, name, comm=()):
    dc = zc.shape[1]
    d = hs.shape[1]

    def body(zc_ref, ysc_ref, w_ref, hs_ref, lg_ref, lb_ref, gn_ref, y_ref, hsn_ref, hn_ref, rn_ref):
        def mix(rs, _):
            _, _, lin, s = _layernorm_silu(zc_ref[rs, :], lg_ref[...], lb_ref[...])
            y_ref[rs, :dc] = ysc_ref[rs, :]
            y_ref[rs, dc:] = (lin * s).astype(BF16)

        _for_row_groups(tm, mix)
        hsn_ref[...] = jnp.dot(y_ref[...], w_ref[...], preferred_element_type=F32)

        def norm(rs, _):
            hsn = hs_ref[rs, :] + hsn_ref[rs, :]
            r = lax.rsqrt(jnp.mean(hsn * hsn, axis=-1, keepdims=True) + EPS)
            hsn_ref[rs, :] = hsn
            hn_ref[rs, :] = (hsn * r * gn_ref[...]).astype(BF16)
            rn_ref[rs, :] = r

        _for_row_groups(tm, norm)

    half = pl.BlockSpec((tm, dc), lambda i: (i, 0))
    row = pl.BlockSpec((tm, d), lambda i: (i, 0))
    vec_c = pl.BlockSpec((1, dc), lambda i: (0, 0))
    return _call(
        body, comm=comm, name=name, grid=(rows // tm,),
        in_specs=[half, half, _once((2 * dc, d), lambda i: (0, 0)), row, vec_c, vec_c,
                  pl.BlockSpec((1, d), lambda i: (0, 0))],
        out_specs=[pl.BlockSpec((tm, 2 * dc), lambda i: (i, 0)), row, row, pl.BlockSpec((tm, 1), lambda i: (i, 0))],
        out_shape=[_sds((rows, 2 * dc), BF16), _sds((rows, d), F32), _sds((rows, d), BF16), _sds((rows, 1), F32)],
        compiler_params=_params(),
    )(zc, ysc, wout, hs, lg, lb, gn)


def _wout_bwd_dy(do, wout, zc, lg, lb, tm, name, comm=()):
    rows, d = do.shape
    dc = zc.shape[1]
    nt = rows // tm

    def body(do_ref, w_ref, zc_ref, lg_ref, lb_ref, dysc_ref, dzc_ref, dlg_ref, dlb_ref, dy_ref):
        dy_ref[...] = lax.dot_general(do_ref[...], w_ref[...], NT, preferred_element_type=F32)
        dlb_ref[0] = jnp.zeros((1, dc), F32)
        dlg_ref[0] = jnp.zeros((1, dc), F32)

        def group(rs, _):
            dysc_ref[rs, :] = dy_ref[rs, :dc].astype(BF16)
            nrm, rstd, lin, s = _layernorm_silu(zc_ref[rs, :], lg_ref[...], lb_ref[...])
            dl = dy_ref[rs, dc:] * (s * (1.0 + lin * (1.0 - s)))
            dlb_ref[0] += jnp.sum(dl, axis=0, keepdims=True)
            dlg_ref[0] += jnp.sum(dl * nrm, axis=0, keepdims=True)
            dn = dl * lg_ref[...]
            dzc_ref[rs, :] = rstd * (dn - jnp.mean(dn, axis=-1, keepdims=True)
                                     - nrm * jnp.mean(dn * nrm, axis=-1, keepdims=True))

        _for_row_groups(tm, group)

    half = pl.BlockSpec((tm, dc), lambda i: (i, 0))
    vec_c = pl.BlockSpec((1, dc), lambda i: (0, 0))
    part = pl.BlockSpec((1, 1, dc), lambda i: (i, 0, 0))
    return _call(
        body, comm=comm, name=name, grid=(nt,),
        in_specs=[pl.BlockSpec((tm, d), lambda i: (i, 0)), _once((2 * dc, d), lambda i: (0, 0)), half, vec_c, vec_c],
        out_specs=[half, half, part, part],
        out_shape=[_sds((rows, dc), BF16), _sds((rows, dc), F32), _sds((nt, 1, dc), F32), _sds((nt, 1, dc), F32)],
        scratch_shapes=[pltpu.VMEM((tm, 2 * dc), F32)], compiler_params=_params(),
    )(do, wout, zc, lg, lb)


def _wout_bwd_dw(y, do, nb, tk, name, comm=()):
    rows, k2 = y.shape
    d = do.shape[1]
    nk = rows // tk

    def body(y_ref, do_ref, dw_ref, acc):
        k = pl.program_id(0)

        @pl.when(k == 0)
        def _():
            acc[...] = jnp.zeros_like(acc)

        acc[...] += lax.dot_general(y_ref[...], do_ref[...], TN, preferred_element_type=F32)

        @pl.when(k == nk - 1)
        def _():
            dw_ref[...] = acc[...].astype(BF16)

    return _call(
        body, comm=comm, name=name, grid=(nk,),
        in_specs=[pl.BlockSpec((tk, k2), lambda k: (k, 0)), pl.BlockSpec((tk, d), lambda k: (k, 0))],
        out_specs=_once((k2, d), lambda k: (0, 0)), out_shape=_sds((k2, d), BF16),
        scratch_shapes=[pltpu.VMEM((k2, d), F32)], compiler_params=_params(),
    )(y, do).reshape(nb, k2 // nb, d)


def _windows(win, n_res):
    length = win.shape[0]
    return [win if r == 0 else pltpu.roll(win, length - r, 0) for r in range(n_res)]


def _taps(src_ref, start, offsets, ch):
    span = -(-(max(offsets) + ch) // SUBLANES) * SUBLANES
    win = src_ref[pl.ds(start, span), :]
    shifted = _windows(win, min(SUBLANES, max(offsets) + 1))
    return [shifted[o % SUBLANES][(o // SUBLANES) * SUBLANES:(o // SUBLANES) * SUBLANES + ch] for o in offsets]


def _rows8(v):
    return jnp.sum(v.reshape(v.shape[0] // SUBLANES, SUBLANES, v.shape[1]), axis=0)


def _conv_geometry(n_seq, n_meta):
    base = CONV_PAD + n_meta
    off_cf = base - (CF_WIDTH - 1)
    off_sc = base - (SC_WIDTH - 1)
    logical = -(-(n_meta + n_seq) // CONV_CH) * CONV_CH
    return base, off_cf, off_sc, logical


def _fill_conv_inputs(c_ref, v_ref, a_ref, g_ref, scv, sz, n_seq, n_meta):
    base = CONV_PAD + n_meta
    cb = scv.shape[1]
    scv[0:CONV_PAD, :] = jnp.zeros((CONV_PAD, cb), F32)
    sz[0:CONV_PAD, :] = jnp.zeros((CONV_PAD, cb), F32)

    def put(src, dst, n):
        cv = c_ref[src, :].astype(F32) * v_ref[src, :].astype(F32)
        scv[dst, :] = cv
        sz[dst, :] = a_ref[src, :].astype(F32) * _sigmoid(g_ref[src, :].astype(F32))

    put(pl.ds(n_seq, n_meta), pl.ds(CONV_PAD, n_meta), n_meta)

    def chunk(i, carry):
        t0 = pl.multiple_of(i * CONV_CH, CONV_CH)
        put(pl.ds(t0, CONV_CH), pl.ds(base + t0, CONV_CH), CONV_CH)
        return carry

    lax.fori_loop(0, n_seq // CONV_CH, chunk, 0)


def _conv_fwd(u, wsc, wcf, cbias, n_seq, n_meta, name, comm=()):
    rows = u.shape[0]
    nb, _, cb = wsc.shape
    dc = nb * cb
    base, off_cf, off_sc, _ = _conv_geometry(n_seq, n_meta)
    a_cf, a_sc = off_cf // SUBLANES * SUBLANES, off_sc // SUBLANES * SUBLANES
    ch = CONV_CH

    def body(b_ref, c_ref, v_ref, a_ref, g_ref, wsc_ref, wcf_ref, cb_ref, ysc_ref, zc_ref, scv, sz):
        _fill_conv_inputs(c_ref, v_ref, a_ref, g_ref, scv, sz, n_seq, n_meta)
        w3, w31, bias = wsc_ref[0], wcf_ref[0], cb_ref[...]

        def chunk(i, carry):
            t0 = pl.multiple_of(i * ch, ch)
            acc = jnp.zeros((ch, cb), F32)
            for k, win in enumerate(_taps(sz, t0 + a_cf, [off_cf - a_cf + k for k in range(CF_WIDTH)], ch)):
                acc = acc + win * w31[k:k + 1, :]
            zc_ref[pl.ds(t0, ch), :] = acc + bias
            s = jnp.zeros((ch, cb), F32)
            for k, win in enumerate(_taps(scv, t0 + a_sc, [off_sc - a_sc + k for k in range(SC_WIDTH)], ch)):
                s = s + win * w3[k:k + 1, :]
            ysc_ref[pl.ds(t0, ch), :] = (b_ref[pl.ds(t0, ch), :].astype(F32) * s).astype(BF16)
            return carry

        lax.fori_loop(0, n_seq // ch, chunk, 0)
        ysc_ref[n_seq:rows, :] = jnp.zeros((rows - n_seq, cb), BF16)
        zc_ref[n_seq:rows, :] = jnp.zeros((rows - n_seq, cb), F32)

    ucol = [pl.BlockSpec((rows, cb), functools.partial(lambda m, j: (0, m * nb + j), m)) for m in range(5)]
    blk = pl.BlockSpec((rows, cb), lambda j: (0, j))
    return _call(
        body, comm=comm, name=name, grid=(nb,),
        in_specs=ucol + [pl.BlockSpec((1, SC_WIDTH, cb), lambda j: (j, 0, 0)),
                         pl.BlockSpec((1, CF_WIDTH, cb), lambda j: (j, 0, 0)), pl.BlockSpec((1, cb), lambda j: (0, j))],
        out_specs=[blk, blk], out_shape=[_sds((rows, dc), BF16), _sds((rows, dc), F32)],
        scratch_shapes=[pltpu.VMEM((base + n_seq, cb), F32)] * 2, compiler_params=_params(),
    )(u, u, u, u, u, wsc, wcf, cbias)


def _conv_bwd(u, dysc, dzc, wsc, wcf, n_seq, n_meta, name, comm=()):
    rows = u.shape[0]
    nb, _, cb = wsc.shape
    dc = nb * cb
    base, off_cf, off_sc, logical = _conv_geometry(n_seq, n_meta)
    a_cf, a_sc = off_cf // SUBLANES * SUBLANES, off_sc // SUBLANES * SUBLANES
    ch = CONV_CH
    tail = CONV_PAD

    def body(b_ref, c_ref, v_ref, a_ref, g_ref, dysc_ref, dzc_ref, wsc_ref, wcf_ref,
             du_ref, dwsc_ref, dwcf_ref, dcb_ref,
             scv, sz, sds_, sdz, dlcv, dlz, accsc, acccf, accb):
        _fill_conv_inputs(c_ref, v_ref, a_ref, g_ref, scv, sz, n_seq, n_meta)
        w3, w31 = wsc_ref[0], wcf_ref[0]
        dub_ref, duc_ref, duv_ref, dua_ref, dug_ref = (du_ref.at[m] for m in range(5))
        sds_[0:n_meta, :] = jnp.zeros((n_meta, cb), F32)
        sdz[0:n_meta, :] = jnp.zeros((n_meta, cb), F32)
        behind = logical + tail - (n_meta + n_seq)
        sds_[n_meta + n_seq:logical + tail, :] = jnp.zeros((behind, cb), F32)
        sdz[n_meta + n_seq:logical + tail, :] = jnp.zeros((behind, cb), F32)
        accsc[...] = jnp.zeros_like(accsc)
        acccf[...] = jnp.zeros_like(acccf)
        accb[...] = jnp.zeros_like(accb)

        def forward_chunk(i, carry):
            t0 = pl.multiple_of(i * ch, ch)
            rws = pl.ds(t0, ch)
            dy = dysc_ref[rws, :].astype(F32)
            ds = dy * b_ref[rws, :].astype(F32)
            dz = dzc_ref[rws, :]
            sds_[pl.ds(n_meta + t0, ch), :] = ds
            sdz[pl.ds(n_meta + t0, ch), :] = dz
            s = jnp.zeros((ch, cb), F32)
            for k, win in enumerate(_taps(scv, t0 + a_sc, [off_sc - a_sc + k for k in range(SC_WIDTH)], ch)):
                s = s + win * w3[k:k + 1, :]
                accsc[k * SUBLANES:(k + 1) * SUBLANES, :] += _rows8(ds * win)
            dub_ref[rws, :] = (dy * s).astype(BF16)
            for k, win in enumerate(_taps(sz, t0 + a_cf, [off_cf - a_cf + k for k in range(CF_WIDTH)], ch)):
                acccf[k * SUBLANES:(k + 1) * SUBLANES, :] += _rows8(dz * win)
            accb[...] += _rows8(dz)
            return carry

        lax.fori_loop(0, n_seq // ch, forward_chunk, 0)

        def backward_chunk(i, carry):
            p0 = pl.multiple_of(i * ch, ch)
            dcv = jnp.zeros((ch, cb), F32)
            for k, win in enumerate(_taps(sds_, p0, [SC_WIDTH - 1 - k for k in range(SC_WIDTH)], ch)):
                dcv = dcv + win * w3[k:k + 1, :]
            dlcv[pl.ds(p0, ch), :] = dcv
            dzi = jnp.zeros((ch, cb), F32)
            for k, win in enumerate(_taps(sdz, p0, [CF_WIDTH - 1 - k for k in range(CF_WIDTH)], ch)):
                dzi = dzi + win * w31[k:k + 1, :]
            dlz[pl.ds(p0, ch), :] = dzi
            return carry

        lax.fori_loop(0, logical // ch, backward_chunk, 0)

        def gates(phys, logi):
            dcv, dzi = dlcv[logi, :], dlz[logi, :]
            duc_ref[phys, :] = (dcv * v_ref[phys, :].astype(F32)).astype(BF16)
            duv_ref[phys, :] = (dcv * c_ref[phys, :].astype(F32)).astype(BF16)
            s = _sigmoid(g_ref[phys, :].astype(F32))
            dua_ref[phys, :] = (dzi * s).astype(BF16)
            dug_ref[phys, :] = (dzi * a_ref[phys, :].astype(F32) * s * (1.0 - s)).astype(BF16)

        def gate_chunk(i, carry):
            t0 = pl.multiple_of(i * ch, ch)
            gates(pl.ds(t0, ch), pl.ds(n_meta + t0, ch))
            return carry

        lax.fori_loop(0, n_seq // ch, gate_chunk, 0)
        gates(pl.ds(n_seq, n_meta), pl.ds(0, n_meta))
        dub_ref[n_seq:rows, :] = jnp.zeros((rows - n_seq, cb), BF16)
        pad0 = n_seq + n_meta
        for ref in (duc_ref, duv_ref, dua_ref, dug_ref):
            ref[pad0:rows, :] = jnp.zeros((rows - pad0, cb), BF16)
        dwsc_ref[0] = jnp.sum(accsc[...].reshape(SC_WIDTH, SUBLANES, cb), axis=1)
        dwcf_ref[0] = jnp.sum(acccf[...].reshape(CF_WIDTH, SUBLANES, cb), axis=1)
        dcb_ref[...] = jnp.sum(accb[...], axis=0, keepdims=True)

    ucol = [pl.BlockSpec((rows, cb), functools.partial(lambda m, j: (0, m * nb + j), m)) for m in range(5)]
    blk = pl.BlockSpec((dysc.shape[0], cb), lambda j: (0, j))
    wsc_spec = pl.BlockSpec((1, SC_WIDTH, cb), lambda j: (j, 0, 0))
    wcf_spec = pl.BlockSpec((1, CF_WIDTH, cb), lambda j: (j, 0, 0))
    outs = _call(
        body, comm=comm, name=name, grid=(nb,),
        in_specs=ucol + [blk, blk, wsc_spec, wcf_spec],
        out_specs=[pl.BlockSpec((5, rows, cb), lambda j: (0, 0, j)), wsc_spec, wcf_spec,
                   pl.BlockSpec((1, cb), lambda j: (0, j))],
        out_shape=[_sds((5, rows, dc), BF16), _sds((nb, SC_WIDTH, cb), F32), _sds((nb, CF_WIDTH, cb), F32),
                   _sds((1, dc), F32)],
        scratch_shapes=[pltpu.VMEM((base + n_seq, cb), F32)] * 2 + [pltpu.VMEM((logical + tail, cb), F32)] * 2
        + [pltpu.VMEM((logical, cb), F32)] * 2
        + [pltpu.VMEM((SC_WIDTH * SUBLANES, cb), F32), pltpu.VMEM((CF_WIDTH * SUBLANES, cb), F32),
           pltpu.VMEM((SUBLANES, cb), F32)],
        compiler_params=_params(),
    )(u, u, u, u, u, dysc, dzc, wsc, wcf)
    return outs


def _row_tile(rows, cols):
    return rows // 4 if rows % 64 == 0 and rows * cols >= (1 << 18) else rows


def _pair_sum(grad, sib, idx, name, comm=()):
    _, rows, cols = grad.shape
    tr = rows

    def body(idx_ref, g_ref, s_ref, o_ref):
        o_ref[0] = (g_ref[0].astype(F32) + s_ref[0].astype(F32)).astype(o_ref.dtype)

    return _call(
        body, name=name,
        grid_spec=pltpu.PrefetchScalarGridSpec(
            num_scalar_prefetch=1, grid=(4, rows // tr),
            in_specs=[pl.BlockSpec((1, tr, cols), lambda k, i, idx_ref: (idx_ref[k], i, 0)),
                      pl.BlockSpec((1, tr, cols), lambda k, i, idx_ref: (idx_ref[4 + k], i, 0))],
            out_specs=pl.BlockSpec((1, tr, cols), lambda k, i, idx_ref: (k, i, 0))),
        out_shape=_sds((4, rows, cols), grad.dtype), compiler_params=_params(),
    )(idx, grad, sib)


def _adamw_math(w, g, m, v):
    m = ADAM_B1 * m + (1.0 - ADAM_B1) * g
    v = ADAM_B2 * v + (1.0 - ADAM_B2) * (g * g)
    m_hat = m / (1.0 - ADAM_B1 ** ADAM_STEP)
    v_hat = v / (1.0 - ADAM_B2 ** ADAM_STEP)
    delta = -ADAM_LR * (m_hat / (jnp.sqrt(v_hat) + ADAM_EPS) + ADAM_WD * w)
    return delta, m, v


def _adamw_sharded(own, got, w, m, v, name, comm=()):
    rows, cols = w.shape
    tr = _row_tile(rows, cols)

    def body(own_ref, g0_ref, g1_ref, g2_ref, w_ref, m_ref, v_ref, g_ref, d_ref, nm_ref, nv_ref):
        g = own_ref[0].astype(F32) + g0_ref[0].astype(F32) + g1_ref[0].astype(F32) + g2_ref[0].astype(F32)
        delta, nm, nv = _adamw_math(w_ref[...], g, m_ref[...], v_ref[...])
        g_ref[...] = g
        d_ref[...] = delta
        nm_ref[...] = nm
        nv_ref[...] = nv

    flat = pl.BlockSpec((tr, cols), lambda i: (i, 0))
    slot = [pl.BlockSpec((1, tr, cols), functools.partial(lambda k, i: (k, i, 0), k)) for k in range(3)]
    return _call(
        body, comm=comm, name=name, grid=(rows // tr,),
        in_specs=[slot[0]] + slot + [flat] * 3, out_specs=[flat] * 4, out_shape=[_sds((rows, cols), F32)] * 4,
        compiler_params=_params(),
    )(own, got, got, got, w, m, v)


def _adamw_replicated(gathered, segs, ws, ms, vs, loss_scale, name, comm=()):
    n = len(ws)

    def body(*refs):
        gat = refs[0]
        w_refs, m_refs, v_refs = refs[1:1 + n], refs[1 + n:1 + 2 * n], refs[1 + 2 * n:1 + 3 * n]
        outs = refs[1 + 3 * n:]

        def total(off, width):
            s = gat[0, :, off:off + width]
            for k in range(1, N_DEV):
                s = s + gat[k, :, off:off + width]
            return s

        outs[0][...] = loss_scale * total(segs[n][0], segs[n][1])
        for p in range(n):
            g = total(*segs[p])
            delta, nm, nv = _adamw_math(w_refs[p][...], g, m_refs[p][...], v_refs[p][...])
            for q, val in enumerate((g, delta, nm, nv)):
                outs[1 + 4 * p + q][...] = val
        for e, seg in enumerate(segs[n + 1:]):
            outs[1 + 4 * n + e][...] = total(*seg)

    return _call(
        body, name=name,
        out_shape=[_sds((1, segs[n][1]), F32)] + [_sds(w.shape, F32) for w in ws for _ in range(4)]
        + [_sds((1, seg[1]), F32) for seg in segs[n + 1:]],
        compiler_params=_params(),
    )(gathered, *ws, *ms, *vs)


def _adamw_plain(g, w, m, v, name):
    def body(g_ref, w_ref, m_ref, v_ref, d_ref, nm_ref, nv_ref):
        d_ref[...], nm_ref[...], nv_ref[...] = _adamw_math(w_ref[...], g_ref[...], m_ref[...], v_ref[...])

    return list(_call(body, name=name, out_shape=[_sds(w.shape, F32)] * 3)(g, w, m, v))


REPLICATED = ("ffn1_norm", "mix_norm", "b_in", "conv_cf_b", "ln_cf_g", "ln_cf_b", "ffn2_norm", "final_norm")
SHARDED = ("meta_tokens", "ffn1_w_gate", "ffn1_w_up", "ffn1_w_down", "w_in", "conv_sc_w", "conv_cf_w", "w_out",
           "ffn2_w_gate", "ffn2_w_up", "ffn2_w_down")
WEIGHTS = ("meta_tokens", "ffn1_norm", "ffn1_w_gate", "ffn1_w_up", "ffn1_w_down", "mix_norm", "w_in", "b_in",
           "conv_sc_w", "conv_cf_w", "conv_cf_b", "ln_cf_g", "ln_cf_b", "w_out", "ffn2_norm", "ffn2_w_gate",
           "ffn2_w_up", "ffn2_w_down", "final_norm")


TRANSPOSED = ("ffn1_w_gate", "ffn1_w_up", "ffn2_w_gate", "ffn2_w_up")


def _blocks2d(k, a):
    a = a.reshape(a.shape[-2:]) if a.ndim >= 2 else a.reshape(1, -1)
    return a.T if k in TRANSPOSED else a


def _step(x, tgt, w, m, v):
    n_seq, d = x.shape[1], x.shape[2]
    n_meta = w["meta_tokens"].shape[0]
    rows = -(-(n_seq + n_meta) // ROW_ALIGN) * ROW_ALIGN
    tm = rows // N_ROW_TILES
    cb = w["conv_sc_w"].shape[-1]
    dc = cb * N_DEV
    w2 = {k: _blocks2d(k, a) for k, a in w.items()}
    m2 = {k: _blocks2d(k, a) for k, a in m.items()}
    v2 = {k: _blocks2d(k, a) for k, a in v.items()}

    def as_given(k, r):
        return (r.T if k in TRANSPOSED else r).reshape(w[k].shape)

    def cast(k):
        return w2[k].astype(BF16)

    full = dict(zip(("ffn1_w_gate", "meta_tokens", "conv_sc_w", "conv_cf_w"), _exchange_alone(
        _gather_all([cast("ffn1_w_gate"), w2["meta_tokens"], w2["conv_sc_w"], w2["conv_cf_w"]]), "gather_gate1")))
    meta = jnp.transpose(full["meta_tokens"], (1, 0, 2)).reshape(n_meta, d)
    hs0 = jnp.concatenate([x[0], meta, jnp.zeros((rows - n_seq - n_meta, d), F32)], axis=0)

    tx = n_seq // N_ROW_TILES
    h1, r1 = _rms_fwd(hs0, w2["ffn1_norm"], tm, "rms_in")
    up1 = _gather_all([cast("ffn1_w_up")]).before(15 / 16)
    tm2, tx2 = 2 * tm, 2 * tx
    g1 = _ffn_proj(h1, full["ffn1_w_gate"], tm2, "ffn1_gate", comm=[up1])
    down1 = _gather_all([cast("ffn1_w_down")]).before(15 / 16)
    u1 = _ffn_proj(h1, up1.results[0], tm2, "ffn1_up", comm=[down1])
    w_in_all, w_out1 = _gather_all([cast("w_in")]).before(3 / 4), _gather_direct([cast("w_out")])
    hs1, h2, r2 = _ffn_down_norm(g1, u1, down1.results[0], hs0, w2["mix_norm"], rows // DOWN_ROW_TILES, "ffn1_down",
                                 comm=[w_in_all, w_out1])
    win = jnp.transpose(w_in_all.results[0], (1, 0, 2)).reshape(d, -1)
    w_out2, gate2 = _gather_forward(w_out1.results), _gather_direct([cast("ffn2_w_gate")])
    u = _win_fwd(h2, win, w2["b_in"], dc, tm, "mix_in", comm=[w_out2, gate2])
    wout = w_out2.results[0].reshape(-1, d)
    gate2p = _gather_forward(gate2.results)
    ysc, zc = _conv_fwd(u, full["conv_sc_w"], full["conv_cf_w"], w2["conv_cf_b"], n_seq, n_meta, "conv_fwd", comm=[gate2p])
    up2 = _gather_all([cast("ffn2_w_up")])
    y, hs2, h3, r3 = _wout_fwd(zc, ysc, wout, hs1, w2["ln_cf_g"], w2["ln_cf_b"], w2["ffn2_norm"], n_seq, tx, "mix_out",
                               comm=[up2])
    down2 = _gather_all([cast("ffn2_w_down")]).before(5 / 8)
    full.update(ffn1_w_up=up1.results[0], ffn1_w_down=down1.results[0], ffn2_w_gate=gate2p.results[0],
                ffn2_w_up=up2.results[0])
    g2, u2 = _ffn_gu(h3, full["ffn2_w_gate"], full["ffn2_w_up"], 2 * tx2, "ffn2_gu", comm=[down2])
    full["ffn2_w_down"] = down2.results[0]
    dhs3, dhs3b, loss_p, dgf_p = _ffn_down_loss(
        g2, u2, full["ffn2_w_down"], hs2, w2["final_norm"], tgt[0], n_seq, n_seq // DOWN_ROW_TILES, "ffn2_down_loss")

    xi, yi, ci = lax.axis_index("x"), lax.axis_index("y"), lax.axis_index("c")
    chip_of = [2 * xi + yi, 2 * (1 - xi) + yi, 2 * xi + (1 - yi), 2 * (1 - xi) + (1 - yi)]
    idx = jnp.stack([2 * ch + ci for ch in chip_of] + chip_of).astype(jnp.int32)
    out = {}

    def to_pairs(parts):
        return _pair_exchange([p.reshape((4, 2) + p.shape[1:]) for p in parts])

    def pair_sums(names, parts, pairs):
        return [_pair_sum(p, s, idx, "pair_sum_" + k) for k, p, s in zip(names, parts, pairs.results)]

    def update(names, sums, chips):
        for k, own, got in zip(names, sums, chips.results):
            res = _adamw_sharded(own, got, w2[k], m2[k], v2[k], "adamw_" + k)
            out[k] = [as_given(k, r) for r in res]

    tk = min(4 * ROW_ALIGN, n_seq)
    th_x, th_r = n_seq // DOWN_ROW_TILES, rows // DOWN_ROW_TILES

    dwd2 = _ffn_bwd_dw_down(g2, u2, dhs3b, tk, "ffn2_bwd_dw_down")
    p_d2 = to_pairs([dwd2])
    dg2, du2 = _ffn_bwd_da(dhs3b, full["ffn2_w_down"], g2, u2, tx2, "ffn2_bwd_da", comm=[p_d2])
    s_d2 = pair_sums(["ffn2_w_down"], [dwd2], p_d2)
    c_d2 = _chip_exchange(s_d2)
    gu2 = _ffn_bwd_dw_gu(h3, dg2, du2, tk, "ffn2_bwd_dw_gu", comm=[c_d2])
    update(["ffn2_w_down"], s_d2, c_d2)
    p_gu2 = to_pairs(gu2)
    half2 = _ffn_bwd_dh_gate(dg2, full["ffn2_w_gate"], tx, "ffn2_bwd_dh_gate", comm=[p_gu2])
    s_gu2 = pair_sums(["ffn2_w_gate", "ffn2_w_up"], gu2, p_gu2)
    c_g2, c_u2 = _chip_exchange(s_gu2[:1]), _chip_exchange(s_gu2[1:])
    dhs2, dhs2b, dn3_p = _ffn_bwd_dh(du2, full["ffn2_w_up"], half2, hs2, r3, w2["ffn2_norm"], dhs3, th_x, "ffn2_bwd_dh",
                                     zeros=jnp.zeros((rows, d), F32))
    dysc, dzc, dlg_p, dlb_p = _wout_bwd_dy(dhs2b, wout, zc, w2["ln_cf_g"], w2["ln_cf_b"], tx, "mix_out_bwd_dy")

    dwout = _wout_bwd_dw(y, dhs2b, N_DEV, tk, "mix_out_bwd_dw")
    p_wo = to_pairs([dwout])
    du, dcsw, dccw, dcb = _conv_bwd(u, dysc, dzc, full["conv_sc_w"], full["conv_cf_w"], n_seq, n_meta, "conv_bwd",
                                    comm=[c_g2, p_wo])
    update(["ffn2_w_gate"], s_gu2[:1], c_g2)
    s_wo = pair_sums(["w_out"], [dwout], p_wo)
    c_wo = _chip_exchange(s_wo)
    dhs1, dhs1b, dn2_p = _win_bwd_dh(du, win, hs1, r2, w2["mix_norm"], dhs2, th_r, "mix_in_bwd_dh", comm=[c_u2])
    update(["ffn2_w_up"], s_gu2[1:], c_u2)
    dwin, dbin = _win_bwd_dw(h2, du, tk, "mix_in_bwd_dw", comm=[c_wo])
    update(["w_out"], s_wo, c_wo)
    mixer = ("w_in", "conv_sc_w", "conv_cf_w")
    gradsm = [jnp.transpose(dwin.reshape(d, N_DEV, -1), (1, 0, 2)), dcsw, dccw]
    p_m = to_pairs(gradsm)
    dwd1 = _ffn_bwd_dw_down(g1, u1, dhs1b, tk, "ffn1_bwd_dw_down", comm=[p_m])
    s_m = pair_sums(mixer, gradsm, p_m)
    c_m, p_d1 = _chip_exchange(s_m), to_pairs([dwd1])
    dg1, du1 = _ffn_bwd_da(dhs1b, full["ffn1_w_down"], g1, u1, tm2, "ffn1_bwd_da", comm=[c_m, p_d1])
    update(mixer, s_m, c_m)
    s_d1 = pair_sums(["ffn1_w_down"], [dwd1], p_d1)
    c_d1 = _chip_exchange(s_d1)
    gu1 = _ffn_bwd_dw_gu(h1, dg1, du1, tk, "ffn1_bwd_dw_gu", comm=[c_d1])
    update(["ffn1_w_down"], s_d1, c_d1)
    p_gu1 = to_pairs(gu1)
    _exchange_alone(p_gu1, "reduce_pair_ffn1")
    s_gu1 = pair_sums(["ffn1_w_gate", "ffn1_w_up"], gu1, p_gu1)
    c_g1, c_u1 = _chip_exchange(s_gu1[:1]), _chip_exchange(s_gu1[1:])
    half1 = _ffn_bwd_dh_gate(dg1, full["ffn1_w_gate"], tm, "ffn1_bwd_dh_gate", comm=[c_g1])
    dhs0, _, dn1_p = _ffn_bwd_dh(du1, full["ffn1_w_up"], half1, hs0, r1, w2["ffn1_norm"], dhs1, th_r, "ffn1_bwd_dh",
                                 comm=[c_u1])
    update(["ffn1_w_gate"], s_gu1[:1], c_g1)
    update(["ffn1_w_up"], s_gu1[1:], c_u1)
    grad_x = dhs0[:n_seq][None]

    partial = {
        "ffn1_norm": dn1_p.sum(0), "mix_norm": dn2_p.sum(0), "b_in": dbin, "conv_cf_b": dcb,
        "ln_cf_g": dlg_p.sum(0), "ln_cf_b": dlb_p.sum(0), "ffn2_norm": dn3_p.sum(0), "final_norm": dgf_p.sum(0),
    }
    loss_seg = jnp.pad(loss_p.sum((0, 2)).reshape(1, 1), ((0, 0), (0, 127)))
    pieces = [partial[k] for k in REPLICATED] + [loss_seg, dhs0[n_seq:n_seq + n_meta].reshape(1, n_meta * d)]
    segs, off = [], 0
    for p in pieces:
        segs.append((off, p.shape[1]))
        off += p.shape[1]
    rows8 = _gather_rows(jnp.concatenate(pieces, axis=1), "gather_small")
    res = _adamw_replicated(rows8, segs, [w2[k] for k in REPLICATED], [m2[k] for k in REPLICATED],
                            [v2[k] for k in REPLICATED], 0.5 / d, "adamw_replicated")
    loss = res[0][0, 0]
    for p, k in enumerate(REPLICATED):
        out[k] = [r.reshape(w[k].shape) for r in res[1 + 4 * p:5 + 4 * p]]
    ob = w2["meta_tokens"].shape[1]
    gmeta = lax.dynamic_slice_in_dim(res[-1].reshape(n_meta, d), (4 * xi + 2 * yi + ci) * ob, ob, axis=1)
    out["meta_tokens"] = [gmeta] + _adamw_plain(gmeta, w2["meta_tokens"], m2["meta_tokens"], v2["meta_tokens"], "adamw_meta_tokens")

    return (loss, grad_x, *[out[k][0] for k in WEIGHTS], *[out[k][1] for k in WEIGHTS],
            *[out[k][2] for k in WEIGHTS], *[out[k][3] for k in WEIGHTS])


def kernel(x, meta_tokens, ffn1_norm, ffn1_w_gate, ffn1_w_up, ffn1_w_down, mix_norm, w_in, b_in, conv_sc_w, conv_cf_w, conv_cf_b, ln_cf_g, ln_cf_b, w_out, ffn2_norm, ffn2_w_gate, ffn2_w_up, ffn2_w_down, final_norm, loss_target, m_meta_tokens, m_ffn1_norm, m_ffn1_w_gate, m_ffn1_w_up, m_ffn1_w_down, m_mix_norm, m_w_in, m_b_in, m_conv_sc_w, m_conv_cf_w, m_conv_cf_b, m_ln_cf_g, m_ln_cf_b, m_w_out, m_ffn2_norm, m_ffn2_w_gate, m_ffn2_w_up, m_ffn2_w_down, m_final_norm, v_meta_tokens, v_ffn1_norm, v_ffn1_w_gate, v_ffn1_w_up, v_ffn1_w_down, v_mix_norm, v_w_in, v_b_in, v_conv_sc_w, v_conv_cf_w, v_conv_cf_b, v_ln_cf_g, v_ln_cf_b, v_w_out, v_ffn2_norm, v_ffn2_w_gate, v_ffn2_w_up, v_ffn2_w_down, v_final_norm):
    given = dict(locals())
    w = {k: given[k] for k in WEIGHTS}
    m = {k: given["m_" + k] for k in WEIGHTS}
    v = {k: given["v_" + k] for k in WEIGHTS}
    return _step(x, loss_target, w, m, v)
```
